```python
import jax, jax.numpy as jnp
from jax import lax
import numpy as np

D_MODEL = 1024
BATCH = 8
SEQ = 4096
DEPTH = 2

LRU_HEADS = 16
LRU_HEAD_DIM = 64
D_LRU = LRU_HEADS * LRU_HEAD_DIM
SC_GROUPS = 8
SC_GROUP_DIM = 64
D_SC = SC_GROUPS * SC_GROUP_DIM
D_MIX = D_LRU + D_SC
D_IN = 2 * D_LRU + 3 * D_SC
LRU_CONV_WIDTH = 4
SC_CONV_WIDTH = 3
RG_C = 8.0
D_FF = 3 * D_MODEL
FFN_CONV_WIDTH = 3
EPS = 1e-6

kernel_name = "hymba_style_rglru_shortconv_convffn"


def rms_norm(x, g):
    xf = x.astype(jnp.float32)
    y = xf * lax.rsqrt(jnp.mean(xf * xf, axis=-1, keepdims=True) + EPS)
    return (y * g.astype(jnp.float32)).astype(x.dtype)


def causal_dwconv(x, w):
    k_width = w.shape[0]
    s = x.shape[1]
    xp = jnp.pad(x, ((0, 0), (k_width - 1, 0), (0, 0)))
    y = xp[:, 0:s] * w[0]
    for k in range(1, k_width):
        y = y + xp[:, k:k + s] * w[k]
    return y


def rg_lru(x, wa, ba, wx, bx, lam):
    bsz, s, c = x.shape
    xh = x.reshape(bsz, s, LRU_HEADS, LRU_HEAD_DIM)
    r = jax.nn.sigmoid(jnp.einsum('bshi,hij->bshj', xh, wa).reshape(bsz, s, c) + ba)
    i = jax.nn.sigmoid(jnp.einsum('bshi,hij->bshj', xh, wx).reshape(bsz, s, c) + bx)
    log_a = -RG_C * r.astype(jnp.float32) * jax.nn.softplus(-lam.astype(jnp.float32))
    a = jnp.exp(log_a)
    mult = jnp.sqrt(-jnp.expm1(2.0 * log_a))
    b = mult * (i * x).astype(jnp.float32)

    def combine(left, right):
        a1, b1 = left
        a2, b2 = right
        return a1 * a2, a2 * b1 + b2

    _, h = lax.associative_scan(combine, (a, b), axis=1)
    return h.astype(x.dtype)


def _fwd_setup_inputs(seed: int = 0) -> dict:
    key = jax.random.key(seed)
    ks = jax.random.split(key, 20)
    f32 = jnp.float32
    res_scale = (2.0 * DEPTH) ** -0.5
    x = jax.random.normal(ks[0], (BATCH, SEQ, D_MODEL), f32)
    norm1_g = 1.0 + 0.02 * jax.random.normal(ks[1], (DEPTH, D_MODEL), f32)
    w_in = jax.random.normal(ks[2], (DEPTH, D_MODEL, D_IN), f32) * D_MODEL ** -0.5
    lru_conv_w = jax.random.normal(ks[3], (DEPTH, LRU_CONV_WIDTH, D_LRU), f32) * LRU_CONV_WIDTH ** -0.5
    lru_conv_b = 0.02 * jax.random.normal(ks[4], (DEPTH, D_LRU), f32)
    lru_wa = jax.random.normal(ks[5], (DEPTH, LRU_HEADS, LRU_HEAD_DIM, LRU_HEAD_DIM), f32) * LRU_HEAD_DIM ** -0.5
    lru_ba = 0.02 * jax.random.normal(ks[6], (DEPTH, D_LRU), f32)
    lru_wx = jax.random.normal(ks[7], (DEPTH, LRU_HEADS, LRU_HEAD_DIM, LRU_HEAD_DIM), f32) * LRU_HEAD_DIM ** -0.5
    lru_bx = 0.02 * jax.random.normal(ks[8], (DEPTH, D_LRU), f32)
    u = jax.random.uniform(ks[9], (DEPTH, D_LRU), f32, minval=0.9, maxval=0.999)
    a0 = u ** (1.0 / RG_C)
    lru_lambda = jnp.log(a0) - jnp.log1p(-a0)
    sc_conv_w = jax.random.normal(ks[10], (DEPTH, SC_CONV_WIDTH, D_SC), f32) * SC_CONV_WIDTH ** -0.5
    w_out = jax.random.normal(ks[11], (DEPTH, D_MIX, D_MODEL), f32) * D_MIX ** -0.5 * res_scale
    norm2_g = 1.0 + 0.02 * jax.random.normal(ks[12], (DEPTH, D_MODEL), f32)
    w_up = jax.random.normal(ks[13], (DEPTH, D_MODEL, 2 * D_FF), f32) * D_MODEL ** -0.5
    ffn_conv_w = jax.random.normal(ks[14], (DEPTH, FFN_CONV_WIDTH, 2 * D_FF), f32) * FFN_CONV_WIDTH ** -0.5
    w_down = jax.random.normal(ks[15], (DEPTH, D_FF, D_MODEL), f32) * D_FF ** -0.5 * res_scale
    final_g = 1.0 + 0.02 * jax.random.normal(ks[16], (D_MODEL,), f32)
    return {"x": x, "norm1_g": norm1_g, "w_in": w_in, "lru_conv_w": lru_conv_w,
            "lru_conv_b": lru_conv_b, "lru_wa": lru_wa, "lru_ba": lru_ba, "lru_wx": lru_wx,
            "lru_bx": lru_bx, "lru_lambda": lru_lambda, "sc_conv_w": sc_conv_w, "w_out": w_out,
            "norm2_g": norm2_g, "w_up": w_up, "ffn_conv_w": ffn_conv_w, "w_down": w_down,
            "final_g": final_g}


def _fwd_reference(x, norm1_g, w_in, lru_conv_w, lru_conv_b, lru_wa, lru_ba, lru_wx, lru_bx,
              lru_lambda, sc_conv_w, w_out, norm2_g, w_up, ffn_conv_w, w_down, final_g):
    splits = [D_LRU, 2 * D_LRU, 2 * D_LRU + D_SC, 2 * D_LRU + 2 * D_SC]
    for l in range(DEPTH):
        h = rms_norm(x, norm1_g[l])
        z = jnp.einsum('bsd,de->bse', h, w_in[l])
        lru_x, lru_gate, sc_b, sc_c, sc_x = jnp.split(z, splits, axis=-1)
        lru_x = causal_dwconv(lru_x, lru_conv_w[l]) + lru_conv_b[l]
        y_lru = rg_lru(lru_x, lru_wa[l], lru_ba[l], lru_wx[l], lru_bx[l], lru_lambda[l]) \
            * jax.nn.gelu(lru_gate)
        y_sc = sc_b * causal_dwconv(sc_c * sc_x, sc_conv_w[l])
        y_mix = jnp.concatenate([y_lru, y_sc], axis=-1)
        x = x + jnp.einsum('bse,ed->bsd', y_mix, w_out[l])
        h = rms_norm(x, norm2_g[l])
        u = causal_dwconv(jnp.einsum('bsd,df->bsf', h, w_up[l]), ffn_conv_w[l])
        gate, up = jnp.split(u, 2, axis=-1)
        x = x + jnp.einsum('bsf,fd->bsd', jax.nn.gelu(gate) * up, w_down[l])
    return rms_norm(x, final_g)


import jax as _jax
import jax.numpy as _jnp

TWIN_FORMAT = 'train_step'
FWD_PARAMS = ['x', 'norm1_g', 'w_in', 'lru_conv_w', 'lru_conv_b', 'lru_wa', 'lru_ba', 'lru_wx', 'lru_bx', 'lru_lambda', 'sc_conv_w', 'w_out', 'norm2_g', 'w_up', 'ffn_conv_w', 'w_down', 'final_g']
TWIN_WEIGHTS = ['norm1_g', 'w_in', 'lru_conv_w', 'lru_conv_b', 'lru_wa', 'lru_ba', 'lru_wx', 'lru_bx', 'lru_lambda', 'sc_conv_w', 'w_out', 'norm2_g', 'w_up', 'ffn_conv_w', 'w_down', 'final_g']
TWIN_DIFF_INPUT = 'x'
TWIN_INPUTS = ['x', 'norm1_g', 'w_in', 'lru_conv_w', 'lru_conv_b', 'lru_wa', 'lru_ba', 'lru_wx', 'lru_bx', 'lru_lambda', 'sc_conv_w', 'w_out', 'norm2_g', 'w_up', 'ffn_conv_w', 'w_down', 'final_g', 'loss_target', 'm_norm1_g', 'm_w_in', 'm_lru_conv_w', 'm_lru_conv_b', 'm_lru_wa', 'm_lru_ba', 'm_lru_wx', 'm_lru_bx', 'm_lru_lambda', 'm_sc_conv_w', 'm_w_out', 'm_norm2_g', 'm_w_up', 'm_ffn_conv_w', 'm_w_down', 'm_final_g', 'v_norm1_g', 'v_w_in', 'v_lru_conv_w', 'v_lru_conv_b', 'v_lru_wa', 'v_lru_ba', 'v_lru_wx', 'v_lru_bx', 'v_lru_lambda', 'v_sc_conv_w', 'v_w_out', 'v_norm2_g', 'v_w_up', 'v_ffn_conv_w', 'v_w_down', 'v_final_g']
TWIN_OUTPUTS = ['loss', 'grad_x', 'grad_norm1_g', 'grad_w_in', 'grad_lru_conv_w', 'grad_lru_conv_b', 'grad_lru_wa', 'grad_lru_ba', 'grad_lru_wx', 'grad_lru_bx', 'grad_lru_lambda', 'grad_sc_conv_w', 'grad_w_out', 'grad_norm2_g', 'grad_w_up', 'grad_ffn_conv_w', 'grad_w_down', 'grad_final_g', 'delta_norm1_g', 'delta_w_in', 'delta_lru_conv_w', 'delta_lru_conv_b', 'delta_lru_wa', 'delta_lru_ba', 'delta_lru_wx', 'delta_lru_bx', 'delta_lru_lambda', 'delta_sc_conv_w', 'delta_w_out', 'delta_norm2_g', 'delta_w_up', 'delta_ffn_conv_w', 'delta_w_down', 'delta_final_g', 'new_m_norm1_g', 'new_m_w_in', 'new_m_lru_conv_w', 'new_m_lru_conv_b', 'new_m_lru_wa', 'new_m_lru_ba', 'new_m_lru_wx', 'new_m_lru_bx', 'new_m_lru_lambda', 'new_m_sc_conv_w', 'new_m_w_out', 'new_m_norm2_g', 'new_m_w_up', 'new_m_ffn_conv_w', 'new_m_w_down', 'new_m_final_g', 'new_v_norm1_g', 'new_v_w_in', 'new_v_lru_conv_w', 'new_v_lru_conv_b', 'new_v_lru_wa', 'new_v_lru_ba', 'new_v_lru_wx', 'new_v_lru_bx', 'new_v_lru_lambda', 'new_v_sc_conv_w', 'new_v_w_out', 'new_v_norm2_g', 'new_v_w_up', 'new_v_ffn_conv_w', 'new_v_w_down', 'new_v_final_g']
TWIN_LEAF_KINDS = {'loss': 'loss', 'grad_x': 'grad_x', 'grad_norm1_g': 'grad_w', 'grad_w_in': 'grad_w', 'grad_lru_conv_w': 'grad_w', 'grad_lru_conv_b': 'grad_w', 'grad_lru_wa': 'grad_w', 'grad_lru_ba': 'grad_w', 'grad_lru_wx': 'grad_w', 'grad_lru_bx': 'grad_w', 'grad_lru_lambda': 'grad_w', 'grad_sc_conv_w': 'grad_w', 'grad_w_out': 'grad_w', 'grad_norm2_g': 'grad_w', 'grad_w_up': 'grad_w', 'grad_ffn_conv_w': 'grad_w', 'grad_w_down': 'grad_w', 'grad_final_g': 'grad_w', 'delta_norm1_g': 'delta_w', 'delta_w_in': 'delta_w', 'delta_lru_conv_w': 'delta_w', 'delta_lru_conv_b': 'delta_w', 'delta_lru_wa': 'delta_w', 'delta_lru_ba': 'delta_w', 'delta_lru_wx': 'delta_w', 'delta_lru_bx': 'delta_w', 'delta_lru_lambda': 'delta_w', 'delta_sc_conv_w': 'delta_w', 'delta_w_out': 'delta_w', 'delta_norm2_g': 'delta_w', 'delta_w_up': 'delta_w', 'delta_ffn_conv_w': 'delta_w', 'delta_w_down': 'delta_w', 'delta_final_g': 'delta_w', 'new_m_norm1_g': 'new_m', 'new_m_w_in': 'new_m', 'new_m_lru_conv_w': 'new_m', 'new_m_lru_conv_b': 'new_m', 'new_m_lru_wa': 'new_m', 'new_m_lru_ba': 'new_m', 'new_m_lru_wx': 'new_m', 'new_m_lru_bx': 'new_m', 'new_m_lru_lambda': 'new_m', 'new_m_sc_conv_w': 'new_m', 'new_m_w_out': 'new_m', 'new_m_norm2_g': 'new_m', 'new_m_w_up': 'new_m', 'new_m_ffn_conv_w': 'new_m', 'new_m_w_down': 'new_m', 'new_m_final_g': 'new_m', 'new_v_norm1_g': 'new_v', 'new_v_w_in': 'new_v', 'new_v_lru_conv_w': 'new_v', 'new_v_lru_conv_b': 'new_v', 'new_v_lru_wa': 'new_v', 'new_v_lru_ba': 'new_v', 'new_v_lru_wx': 'new_v', 'new_v_lru_bx': 'new_v', 'new_v_lru_lambda': 'new_v', 'new_v_sc_conv_w': 'new_v', 'new_v_w_out': 'new_v', 'new_v_norm2_g': 'new_v', 'new_v_w_up': 'new_v', 'new_v_ffn_conv_w': 'new_v', 'new_v_w_down': 'new_v', 'new_v_final_g': 'new_v'}


def _forward(args):
    return _fwd_reference(*[args[k] for k in FWD_PARAMS])


def _output_shape():
    out = _jax.eval_shape(lambda: _forward(_fwd_setup_inputs(0)))
    return out.shape, out.dtype

N_MICROBATCH = 1
ADAM_LR = 0.001
ADAM_B1 = 0.9
ADAM_B2 = 0.999
ADAM_EPS = 1e-08
ADAM_WD = 0.01
ADAM_STEP = 10
PER_EXAMPLE_BATCH_AXIS = {'x': 0, 'loss_target': 0}
SHARED_INPUTS = []
_WEIGHT_DTYPES = {'norm1_g': _jnp.float32, 'w_in': _jnp.float32, 'lru_conv_w': _jnp.float32, 'lru_conv_b': _jnp.float32, 'lru_wa': _jnp.float32, 'lru_ba': _jnp.float32, 'lru_wx': _jnp.float32, 'lru_bx': _jnp.float32, 'lru_lambda': _jnp.float32, 'sc_conv_w': _jnp.float32, 'w_out': _jnp.float32, 'norm2_g': _jnp.float32, 'w_up': _jnp.float32, 'ffn_conv_w': _jnp.float32, 'w_down': _jnp.float32, 'final_g': _jnp.float32}
MOMENT_SCALE = {'norm1_g': 1.044274e-01, 'w_in': 5.452906e-02, 'lru_conv_w': 3.830739e-02, 'lru_conv_b': 4.154163e-01, 'lru_wa': 1.415753e-02, 'lru_ba': 1.082436e-02, 'lru_wx': 2.559040e-02, 'lru_bx': 1.448412e-02, 'lru_lambda': 2.071311e-02, 'sc_conv_w': 7.442254e-02, 'w_out': 1.281441e-01, 'norm2_g': 8.084124e-02, 'w_up': 3.168222e-02, 'ffn_conv_w': 3.182043e-02, 'w_down': 1.080423e-01, 'final_g': 3.199303e+01}


def _to_microbatches(a, axis):
    t = _jnp.moveaxis(a, axis, 0)
    t = t.reshape((N_MICROBATCH, t.shape[0] // N_MICROBATCH) + t.shape[1:])
    return _jnp.moveaxis(t, 1, axis + 1)


def setup_inputs(seed: int = 0) -> dict:
    inp = _fwd_setup_inputs(seed)
    key = _jax.random.fold_in(_jax.random.key(seed), 7919)
    shape, _ = _output_shape()
    out = dict(inp)
    out["loss_target"] = _jax.random.normal(_jax.random.fold_in(key, 0), shape, _jnp.float32)
    for i, name in enumerate(TWIN_WEIGHTS):
        w = inp[name].astype(_jnp.float32)
        if MOMENT_SCALE is None:
            s = _jnp.sqrt(_jnp.mean(_jnp.square(w)) + 1e-30)
        else:
            s = MOMENT_SCALE[name]
        km, kv = _jax.random.split(_jax.random.fold_in(key, i + 1))
        out[name] = w
        out["m_" + name] = s * _jax.random.normal(km, w.shape, _jnp.float32)
        out["v_" + name] = (s * s) * _jax.random.uniform(kv, w.shape, _jnp.float32, 0.5, 1.5)
    if N_MICROBATCH > 1:
        for name, axis in PER_EXAMPLE_BATCH_AXIS.items():
            out[name] = _to_microbatches(out[name], axis)
    return {'x': out['x'], 'norm1_g': out['norm1_g'], 'w_in': out['w_in'], 'lru_conv_w': out['lru_conv_w'], 'lru_conv_b': out['lru_conv_b'], 'lru_wa': out['lru_wa'], 'lru_ba': out['lru_ba'], 'lru_wx': out['lru_wx'], 'lru_bx': out['lru_bx'], 'lru_lambda': out['lru_lambda'], 'sc_conv_w': out['sc_conv_w'], 'w_out': out['w_out'], 'norm2_g': out['norm2_g'], 'w_up': out['w_up'], 'ffn_conv_w': out['ffn_conv_w'], 'w_down': out['w_down'], 'final_g': out['final_g'], 'loss_target': out['loss_target'], 'm_norm1_g': out['m_norm1_g'], 'm_w_in': out['m_w_in'], 'm_lru_conv_w': out['m_lru_conv_w'], 'm_lru_conv_b': out['m_lru_conv_b'], 'm_lru_wa': out['m_lru_wa'], 'm_lru_ba': out['m_lru_ba'], 'm_lru_wx': out['m_lru_wx'], 'm_lru_bx': out['m_lru_bx'], 'm_lru_lambda': out['m_lru_lambda'], 'm_sc_conv_w': out['m_sc_conv_w'], 'm_w_out': out['m_w_out'], 'm_norm2_g': out['m_norm2_g'], 'm_w_up': out['m_w_up'], 'm_ffn_conv_w': out['m_ffn_conv_w'], 'm_w_down': out['m_w_down'], 'm_final_g': out['m_final_g'], 'v_norm1_g': out['v_norm1_g'], 'v_w_in': out['v_w_in'], 'v_lru_conv_w': out['v_lru_conv_w'], 'v_lru_conv_b': out['v_lru_conv_b'], 'v_lru_wa': out['v_lru_wa'], 'v_lru_ba': out['v_lru_ba'], 'v_lru_wx': out['v_lru_wx'], 'v_lru_bx': out['v_lru_bx'], 'v_lru_lambda': out['v_lru_lambda'], 'v_sc_conv_w': out['v_sc_conv_w'], 'v_w_out': out['v_w_out'], 'v_norm2_g': out['v_norm2_g'], 'v_w_up': out['v_w_up'], 'v_ffn_conv_w': out['v_ffn_conv_w'], 'v_w_down': out['v_w_down'], 'v_final_g': out['v_final_g']}


def _loss(weights, diff, rest, loss_target):
    with _jax.named_scope("forward"):
        args = {**rest, TWIN_DIFF_INPUT: diff, **{k: w.astype(_WEIGHT_DTYPES[k]) for k, w in weights.items()}}
        y = _forward(args)
    with _jax.named_scope("loss_head"):
        err = _jnp.square(y.astype(_jnp.float32) - loss_target)
        return 0.5 * _jnp.sum(_jnp.mean(err, axis=-1)) if err.ndim else 0.5 * err


def _adamw(w, g, m, v):
    m = ADAM_B1 * m + (1.0 - ADAM_B1) * g
    v = ADAM_B2 * v + (1.0 - ADAM_B2) * _jnp.square(g)
    m_hat = m / (1.0 - ADAM_B1 ** ADAM_STEP)
    v_hat = v / (1.0 - ADAM_B2 ** ADAM_STEP)
    delta = -ADAM_LR * (m_hat / (_jnp.sqrt(v_hat) + ADAM_EPS) + ADAM_WD * w)
    return delta, m, v


def reference(x, norm1_g, w_in, lru_conv_w, lru_conv_b, lru_wa, lru_ba, lru_wx, lru_bx, lru_lambda, sc_conv_w, w_out, norm2_g, w_up, ffn_conv_w, w_down, final_g, loss_target, m_norm1_g, m_w_in, m_lru_conv_w, m_lru_conv_b, m_lru_wa, m_lru_ba, m_lru_wx, m_lru_bx, m_lru_lambda, m_sc_conv_w, m_w_out, m_norm2_g, m_w_up, m_ffn_conv_w, m_w_down, m_final_g, v_norm1_g, v_w_in, v_lru_conv_w, v_lru_conv_b, v_lru_wa, v_lru_ba, v_lru_wx, v_lru_bx, v_lru_lambda, v_sc_conv_w, v_w_out, v_norm2_g, v_w_up, v_ffn_conv_w, v_w_down, v_final_g):
    given = dict(x=x, norm1_g=norm1_g, w_in=w_in, lru_conv_w=lru_conv_w, lru_conv_b=lru_conv_b, lru_wa=lru_wa, lru_ba=lru_ba, lru_wx=lru_wx, lru_bx=lru_bx, lru_lambda=lru_lambda, sc_conv_w=sc_conv_w, w_out=w_out, norm2_g=norm2_g, w_up=w_up, ffn_conv_w=ffn_conv_w, w_down=w_down, final_g=final_g, loss_target=loss_target, m_norm1_g=m_norm1_g, m_w_in=m_w_in, m_lru_conv_w=m_lru_conv_w, m_lru_conv_b=m_lru_conv_b, m_lru_wa=m_lru_wa, m_lru_ba=m_lru_ba, m_lru_wx=m_lru_wx, m_lru_bx=m_lru_bx, m_lru_lambda=m_lru_lambda, m_sc_conv_w=m_sc_conv_w, m_w_out=m_w_out, m_norm2_g=m_norm2_g, m_w_up=m_w_up, m_ffn_conv_w=m_ffn_conv_w, m_w_down=m_w_down, m_final_g=m_final_g, v_norm1_g=v_norm1_g, v_w_in=v_w_in, v_lru_conv_w=v_lru_conv_w, v_lru_conv_b=v_lru_conv_b, v_lru_wa=v_lru_wa, v_lru_ba=v_lru_ba, v_lru_wx=v_lru_wx, v_lru_bx=v_lru_bx, v_lru_lambda=v_lru_lambda, v_sc_conv_w=v_sc_conv_w, v_w_out=v_w_out, v_norm2_g=v_norm2_g, v_w_up=v_w_up, v_ffn_conv_w=v_ffn_conv_w, v_w_down=v_w_down, v_final_g=v_final_g)
    weights = {n: given[n] for n in TWIN_WEIGHTS}
    shared = {n: given[n] for n in SHARED_INPUTS}
    per_example = {n: given[n] for n in ['x']}
    grad_fn = _jax.value_and_grad(_loss, argnums=(0, 1))

    def one_microbatch(ex, loss_target):
        ex = dict(ex)
        diff = ex.pop(TWIN_DIFF_INPUT)
        return grad_fn(weights, diff, {**shared, **ex}, loss_target)

    if N_MICROBATCH == 1:
        loss, (grad_w, grad_x) = one_microbatch(per_example, given["loss_target"])
    else:
        def body(carry, xs):
            loss_sum, grad_sum = carry
            l_k, (gw_k, gx_k) = one_microbatch(xs[0], xs[1])
            with _jax.named_scope("update"):
                return (loss_sum + l_k, _jax.tree.map(_jnp.add, grad_sum, gw_k)), gx_k

        init = (_jnp.zeros((), _jnp.float32), _jax.tree.map(_jnp.zeros_like, weights))
        (loss, grad_w), grad_x = _jax.lax.scan(body, init, (per_example, given["loss_target"]))
    with _jax.named_scope("update"):
        delta_w, new_m, new_v = {}, {}, {}
        for n in TWIN_WEIGHTS:
            delta_w[n], new_m[n], new_v[n] = _adamw(weights[n], grad_w[n], given["m_" + n], given["v_" + n])
    return (loss, grad_x, *[grad_w[n] for n in TWIN_WEIGHTS], *[delta_w[n] for n in TWIN_WEIGHTS],
            *[new_m[n] for n in TWIN_WEIGHTS], *[new_v[n] for n in TWIN_WEIGHTS])
```

```python
import functools
import math

import jax
import jax.numpy as jnp
from jax import lax
from jax.experimental import pallas as pl
from jax.experimental.pallas import tpu as pltpu

F32 = jnp.float32
BF16 = jnp.bfloat16
MESH = pl.DeviceIdType.MESH

D_MODEL = 1024
D_LRU = 1024
D_SC = 512
D_MIX = D_LRU + D_SC
D_IN = 2 * D_LRU + 3 * D_SC
D_FF = 3072
N_CHIP = 4
RG_C = 8.0
EPS = 1e-6
ADAM_LR = 0.001
ADAM_B1 = 0.9
ADAM_B2 = 0.999
ADAM_EPS = 1e-08
ADAM_WD = 0.01
ADAM_STEP = 10

SUBLANES = 8
LANES = 128
VMEM_LIMIT = 56 * 1024 * 1024
GELU_C0 = math.sqrt(2.0 / math.pi)
GELU_C1 = 0.044715

REP_LAYER = 6 * 1024 + 2 * 16 * 64 * 64
REP_ROWS = (2 * REP_LAYER + 1024) // LANES
CONV_LAYER = 4 * 1024 + 2048 + 3 * 6144
CONV_ROWS = 2 * CONV_LAYER // LANES
SMALL_ROWS = REP_ROWS + CONV_ROWS + 8
SMALL_HALF = SMALL_ROWS // 2
CONV_PACK_ROWS = 96


def _cp(*sem):
    return pltpu.CompilerParams(dimension_semantics=sem, vmem_limit_bytes=VMEM_LIMIT)


def _sigmoid(v):
    return 1.0 / (1.0 + jnp.exp(-v))


def _gelu_parts(v):
    v2 = v * v
    t = jnp.tanh(GELU_C0 * v * (1.0 + GELU_C1 * v2))
    half = 0.5 * (1.0 + t)
    gel = v * half
    dgel = half + 0.5 * v * (1.0 - t * t) * (GELU_C0 * (1.0 + 3.0 * GELU_C1 * v2))
    return gel, dgel


def _gelu(v):
    t = jnp.tanh(GELU_C0 * v * (1.0 + GELU_C1 * (v * v)))
    return 0.5 * v * (1.0 + t)


def _neg_expm1(y, a):
    p = jnp.full_like(y, 1.0 / 5040.0)
    for coef in (1.0 / 720.0, 1.0 / 120.0, 1.0 / 24.0, 1.0 / 6.0, 0.5, 1.0):
        p = p * y + coef
    return jnp.where(y > -0.3, -(p * y), 1.0 - a * a)


def _softplus_neg(lam):
    nl = -lam
    e = jnp.exp(-jnp.abs(nl))
    u = 1.0 + e
    l1p = jnp.where(u == 1.0, e, jnp.log(u) * e / (u - 1.0))
    return jnp.maximum(nl, 0.0) + l1p


def _conv_taps(ext, taps, n_out):
    kw = len(taps)
    acc = taps[kw - 1] * ext[SUBLANES:SUBLANES + n_out]
    for k in range(kw - 1):
        acc = acc + taps[k] * pltpu.roll(ext, kw - 1 - k, axis=0)[SUBLANES:SUBLANES + n_out]
    return acc


def _conv_taps_t(ext, taps, n_out):
    kw = len(taps)
    n = ext.shape[0]
    acc = taps[kw - 1] * ext[0:n_out]
    for k in range(kw - 1):
        acc = acc + taps[k] * pltpu.roll(ext, n - (kw - 1 - k), axis=0)[0:n_out]
    return acc


def _scan8(a, b, carry, row):
    for s in (1, 2, 4):
        m = row >= s
        a_sh = jnp.where(m, pltpu.roll(a, s, axis=0), 1.0)
        b_sh = jnp.where(m, pltpu.roll(b, s, axis=0), 0.0)
        b = a * b_sh + b
        a = a * a_sh
    return a * carry + b


def _scan8_rev(a, b, carry, row):
    for s in (1, 2, 4):
        m = row < SUBLANES - s
        a_sh = jnp.where(m, pltpu.roll(a, SUBLANES - s, axis=0), 1.0)
        b_sh = jnp.where(m, pltpu.roll(b, SUBLANES - s, axis=0), 0.0)
        b = a * b_sh + b
        a = a * a_sh
    return a * carry + b


def _cast_bf16(w, name):
    nl, r, c = w.shape
    tr = 256 if r % 256 == 0 else r

    def body(w_ref, o_ref):
        o_ref[...] = w_ref[...].astype(BF16)

    return pl.pallas_call(
        body, name=name, grid=(nl, r // tr),
        in_specs=[pl.BlockSpec((None, tr, c), lambda l, i: (l, i, 0))],
        out_specs=pl.BlockSpec((None, tr, c), lambda l, i: (l, i, 0)),
        out_shape=jax.ShapeDtypeStruct(w.shape, BF16),
        compiler_params=_cp("parallel", "parallel"),
    )(w)


def _norm_mm(x, g, w, layer, name, planes=False, tm=512, tn=512):
    t_len, d = x.shape
    n = w.shape[2]
    nj = n // tn
    half = nj // 2

    def body(x_ref, g_ref, w_ref, z_ref, h_ref, hs):
        @pl.when(pl.program_id(1) == 0)
        def _():
            xv = x_ref[...]
            r = lax.rsqrt(jnp.mean(xv * xv, axis=-1, keepdims=True) + EPS)
            hv = ((xv * r) * g_ref[...]).astype(BF16)
            hs[...] = hv
            h_ref[...] = hv

        z_ref[...] = jnp.dot(hs[...], w_ref[...], preferred_element_type=F32)

    if planes:
        z_shape = jax.ShapeDtypeStruct((2, t_len, n // 2), F32)
        z_spec = pl.BlockSpec((None, tm, tn), lambda i, j: (j // half, i, j % half))
    else:
        z_shape = jax.ShapeDtypeStruct((t_len, n), F32)
        z_spec = pl.BlockSpec((tm, tn), lambda i, j: (i, j))
    return pl.pallas_call(
        body, name=name, grid=(t_len // tm, nj),
        in_specs=[pl.BlockSpec((tm, d), lambda i, j: (i, 0)),
                  pl.BlockSpec((1, d), lambda i, j: (0, 0)),
                  pl.BlockSpec((None, d, tn), lambda i, j: (layer, 0, j))],
        out_specs=[z_spec, pl.BlockSpec((tm, d), lambda i, j: (i, 0))],
        out_shape=[z_shape, jax.ShapeDtypeStruct((t_len, d), BF16)],
        scratch_shapes=[pltpu.VMEM((tm, d), BF16)],
        compiler_params=_cp("parallel", "arbitrary"),
    )(x, g, w)


def _mm_res(a, w, layer, res, name, tm=512):
    t_len, k = a.shape
    n = w.shape[2]

    def body(a_ref, w_ref, r_ref, o_ref):
        o_ref[...] = r_ref[...] + jnp.dot(a_ref[...], w_ref[...], preferred_element_type=F32)

    return pl.pallas_call(
        body, name=name, grid=(t_len // tm,),
        in_specs=[pl.BlockSpec((tm, k), lambda i: (i, 0)),
                  pl.BlockSpec((None, k, n), lambda i: (layer, 0, 0)),
                  pl.BlockSpec((tm, n), lambda i: (i, 0))],
        out_specs=pl.BlockSpec((tm, n), lambda i: (i, 0)),
        out_shape=jax.ShapeDtypeStruct((t_len, n), F32),
        compiler_params=_cp("parallel"),
    )(a, w, res)


def _mm_nt(a, w, layer, name, tm=512):
    t_len, k = a.shape
    n = w.shape[1]

    def body(a_ref, w_ref, o_ref):
        o_ref[...] = lax.dot_general(a_ref[...], w_ref[...], (((1,), (1,)), ((), ())),
                                     preferred_element_type=F32)

    return pl.pallas_call(
        body, name=name, grid=(t_len // tm,),
        in_specs=[pl.BlockSpec((tm, k), lambda i: (i, 0)),
                  pl.BlockSpec((None, n, k), lambda i: (layer, 0, 0))],
        out_specs=pl.BlockSpec((tm, n), lambda i: (i, 0)),
        out_shape=jax.ShapeDtypeStruct((t_len, n), F32),
        compiler_params=_cp("parallel"),
    )(a, w)


def _mm_nt_normbwd(dz, w, layer, x, g, dres, name, planes=False, tm=256, tn=512):
    t_len, d = x.shape
    n = w.shape[2]
    nk = n // tn
    half = nk // 2

    def body(dz_ref, w_ref, x_ref, g_ref, r_ref, dx_ref, dxb_ref, dg_ref, acc):
        i = pl.program_id(0)
        k = pl.program_id(1)

        @pl.when(k == 0)
        def _():
            acc[...] = jnp.zeros_like(acc)

        @pl.when((i == 0) & (k == 0))
        def _():
            dg_ref[...] = jnp.zeros_like(dg_ref)

        acc[...] += lax.dot_general(dz_ref[...], w_ref[...], (((1,), (1,)), ((), ())),
                                    preferred_element_type=F32)

        @pl.when(k == nk - 1)
        def _():
            xv = x_ref[...]
            r = lax.rsqrt(jnp.mean(xv * xv, axis=-1, keepdims=True) + EPS)
            xh = xv * r
            dh = acc[...]
            dhg = dh * g_ref[...]
            dx = r_ref[...] + r * (dhg - xh * jnp.mean(dhg * xh, axis=-1, keepdims=True))
            dx_ref[...] = dx
            dxb_ref[...] = dx.astype(BF16)
            dg_ref[0:1, :] += jnp.sum(dh * xh, axis=0, keepdims=True)

    if planes:
        dz_spec = pl.BlockSpec((None, tm, tn), lambda i, k: (k // half, i, k % half))
    else:
        dz_spec = pl.BlockSpec((tm, tn), lambda i, k: (i, k))
    return pl.pallas_call(
        body, name=name, grid=(t_len // tm, nk),
        in_specs=[dz_spec,
                  pl.BlockSpec((None, d, tn), lambda i, k: (layer, 0, k)),
                  pl.BlockSpec((tm, d), lambda i, k: (i, 0)),
                  pl.BlockSpec((1, d), lambda i, k: (0, 0)),
                  pl.BlockSpec((tm, d), lambda i, k: (i, 0))],
        out_specs=[pl.BlockSpec((tm, d), lambda i, k: (i, 0)),
                   pl.BlockSpec((tm, d), lambda i, k: (i, 0)),
                   pl.BlockSpec((SUBLANES, d), lambda i, k: (0, 0))],
        out_shape=[jax.ShapeDtypeStruct((t_len, d), F32),
                   jax.ShapeDtypeStruct((t_len, d), BF16),
                   jax.ShapeDtypeStruct((SUBLANES, d), F32)],
        scratch_shapes=[pltpu.VMEM((tm, d), F32)],
        compiler_params=_cp("arbitrary", "arbitrary"),
    )(dz, w, x, g, dres)


def _mm_tn(a, g, layer, prev, name, planes=False, tk=512, tn=512, tt=512):
    t_len, k = a.shape
    n = 2 * g.shape[2] if planes else g.shape[1]
    nn = n // tn
    half = nn // 2
    nt = t_len // tt

    def body(*refs):
        a_ref, g_ref = refs[0], refs[1]
        o_ref = refs[-1]

        @pl.when(pl.program_id(2) == 0)
        def _():
            o_ref[...] = jnp.zeros_like(o_ref)

        o_ref[...] += lax.dot_general(a_ref[...], g_ref[...], (((0,), (0,)), ((), ())),
                                      preferred_element_type=F32)

    if planes:
        g_spec = pl.BlockSpec((None, tt, tn), lambda i, j, t: (j // half, t, j % half))
    else:
        g_spec = pl.BlockSpec((tt, tn), lambda i, j, t: (t, j))
    in_specs = [pl.BlockSpec((tt, tk), lambda i, j, t: (t, i)), g_spec]
    args = [a, g]
    aliases = {}
    if prev is not None:
        in_specs.append(pl.BlockSpec(memory_space=pl.ANY))
        args.append(prev)
        aliases = {2: 0}
    return pl.pallas_call(
        body, name=name, grid=(k // tk, nn, nt),
        in_specs=in_specs,
        out_specs=pl.BlockSpec((None, tk, tn), lambda i, j, t: (layer, i, j)),
        out_shape=jax.ShapeDtypeStruct((2, k, n), F32),
        input_output_aliases=aliases,
        compiler_params=_cp("parallel", "parallel", "arbitrary"),
    )(*args)


def _lru_gates(rp, ip, lx, spn):
    r = _sigmoid(rp)
    i = _sigmoid(ip)
    la = r * spn
    a = jnp.exp(la)
    mult = jnp.sqrt(_neg_expm1(2.0 * la, a))
    return r, i, a, mult


def _mixer_fwd(z, cw8, vec8, wa_bd, wx_bd, scw8, name, tb=256):
    t_len = z.shape[0]
    grp = 2 * SUBLANES

    def body(z_ref, cw_ref, vec_ref, wa_ref, wx_ref, scw_ref, y_ref, h_ref,
             xhalo, phalo, hcar, lx_s, rp_s, ip_s):
        @pl.when(pl.program_id(0) == 0)
        def _():
            xhalo[...] = jnp.zeros_like(xhalo)
            phalo[...] = jnp.zeros_like(phalo)
            hcar[...] = jnp.zeros_like(hcar)

        cw = cw_ref[...]
        vec = vec_ref[...]
        xp = z_ref[:, 0:D_LRU]
        ext = jnp.concatenate([xhalo[...], xp], axis=0)
        lx = vec[0:1] + _conv_taps(ext, [cw[k:k + 1] for k in range(4)], tb)
        xhalo[...] = xp[tb - SUBLANES:]
        lx_s[...] = lx
        lxb = lx.astype(BF16)
        for q in range(4):
            sl = slice(q * 256, (q + 1) * 256)
            rp_s[:, sl] = jnp.dot(lxb[:, sl], wa_ref[q], preferred_element_type=F32) + vec[1:2, sl]
            ip_s[:, sl] = jnp.dot(lxb[:, sl], wx_ref[q], preferred_element_type=F32) + vec[2:3, sl]

        spn = jnp.broadcast_to(-RG_C * _softplus_neg(vec[3:4]), (SUBLANES, D_LRU))
        row = lax.broadcasted_iota(jnp.int32, (SUBLANES, D_LRU), 0)

        def step(ci, carry):
            o = pl.multiple_of(ci * grp, grp)
            ys = []
            for sub in range(2):
                oo = pl.multiple_of(o + sub * SUBLANES, SUBLANES)
                rows = pl.ds(oo, SUBLANES)
                lxv = lx_s[rows, :]
                _, i, a, mult = _lru_gates(rp_s[rows, :], ip_s[rows, :], lxv, spn)
                h = _scan8(a, mult * (i * lxv), carry, row)
                h_ref[rows, :] = h
                ys.append(h * _gelu(z_ref[rows, D_LRU:2 * D_LRU]))
                carry = jnp.broadcast_to(h[SUBLANES - 1:SUBLANES, :], (SUBLANES, D_LRU))
            y_ref[pl.ds(o, grp), 0:D_LRU] = jnp.concatenate(ys, axis=0).astype(BF16)
            return carry

        hcar[...] = lax.fori_loop(0, tb // grp, step, hcar[...])

        scw = scw_ref[...]
        p = z_ref[:, 2 * D_LRU + D_SC:2 * D_LRU + 2 * D_SC] * z_ref[:, 2 * D_LRU + 2 * D_SC:]
        pext = jnp.concatenate([phalo[...], p], axis=0)
        q = _conv_taps(pext, [scw[k:k + 1] for k in range(3)], tb)
        phalo[...] = p[tb - SUBLANES:]
        y_ref[:, D_LRU:] = (z_ref[:, 2 * D_LRU:2 * D_LRU + D_SC] * q).astype(BF16)

    const = lambda t: (0, 0)
    return pl.pallas_call(
        body, name=name, grid=(t_len // tb,),
        in_specs=[pl.BlockSpec((tb, D_IN), lambda t: (t, 0)),
                  pl.BlockSpec((SUBLANES, D_LRU), const),
                  pl.BlockSpec((SUBLANES, D_LRU), const),
                  pl.BlockSpec((4, 256, 256), lambda t: (0, 0, 0)),
                  pl.BlockSpec((4, 256, 256), lambda t: (0, 0, 0)),
                  pl.BlockSpec((SUBLANES, D_SC), const)],
        out_specs=[pl.BlockSpec((tb, D_MIX), lambda t: (t, 0)),
                   pl.BlockSpec((tb, D_LRU), lambda t: (t, 0))],
        out_shape=[jax.ShapeDtypeStruct((t_len, D_MIX), BF16),
                   jax.ShapeDtypeStruct((t_len, D_LRU), F32)],
        scratch_shapes=[pltpu.VMEM((SUBLANES, D_LRU), F32), pltpu.VMEM((SUBLANES, D_SC), F32),
                        pltpu.VMEM((SUBLANES, D_LRU), F32), pltpu.VMEM((tb, D_LRU), F32),
                        pltpu.VMEM((tb, D_LRU), F32), pltpu.VMEM((tb, D_LRU), F32)],
        compiler_params=_cp("arbitrary"),
    )(z, cw8, vec8, wa_bd, wx_bd, scw8)


def _mixer_bwd(z, h, dy, cw8, vec8, wa_bd, wx_bd, scw8, name, tb=128):
    t_len = z.shape[0]
    nb = t_len // tb
    hb = tb // SUBLANES
    grp = 2 * SUBLANES

    def body(z_ref, zh_ref, h_ref, hh_ref, dy_ref, cw_ref, vec_ref, wa_ref, wx_ref, scw_ref,
             dz_ref, dcw_ref, dvec_ref, dwa_ref, dwx_ref, dscw_ref,
             lx_s, lxb_s, rp_s, ip_s, drpb_s, dipb_s, dlx_s, hext_s, acc_s, acar, gcar, dqh):
        t = pl.program_id(0)
        first_block = t == nb - 1

        @pl.when(t == 0)
        def _():
            for ref in (dcw_ref, dvec_ref, dwa_ref, dwx_ref, dscw_ref, acc_s, acar, gcar, dqh):
                ref[...] = jnp.zeros_like(ref)
            dlx_s[tb:, :] = jnp.zeros((SUBLANES, D_LRU), F32)

        cw = cw_ref[...]
        vec = vec_ref[...]
        scw = scw_ref[...]
        ctaps = [cw[k:k + 1] for k in range(4)]
        staps = [scw[k:k + 1] for k in range(3)]
        keep = jnp.where(first_block, 0.0, 1.0)

        xp = z_ref[:, 0:D_LRU]
        xext = jnp.concatenate([zh_ref[:, 0:D_LRU] * keep, xp], axis=0)
        lx = vec[0:1] + _conv_taps(xext, ctaps, tb)
        lx_s[...] = lx
        lxb = lx.astype(BF16)
        lxb_s[...] = lxb
        for q in range(4):
            sl = slice(q * 256, (q + 1) * 256)
            rp_s[:, sl] = jnp.dot(lxb[:, sl], wa_ref[q], preferred_element_type=F32) + vec[1:2, sl]
            ip_s[:, sl] = jnp.dot(lxb[:, sl], wx_ref[q], preferred_element_type=F32) + vec[2:3, sl]
        hext_s[0:SUBLANES, :] = hh_ref[...] * keep
        hext_s[SUBLANES:, :] = h_ref[...]

        spn = jnp.broadcast_to(-RG_C * _softplus_neg(vec[3:4]), (SUBLANES, D_LRU))
        row = lax.broadcasted_iota(jnp.int32, (SUBLANES, D_LRU), 0)

        def step(ci, carry):
            a_next, g_next = carry
            o = pl.multiple_of((tb // grp - 1 - ci) * grp, grp)
            dgs, drs, dis = [None, None], [None, None], [None, None]
            for sub in (1, 0):
                oo = pl.multiple_of(o + sub * SUBLANES, SUBLANES)
                rows = pl.ds(oo, SUBLANES)
                lxv = lx_s[rows, :]
                r, i, a, mult = _lru_gates(rp_s[rows, :], ip_s[rows, :], lxv, spn)
                hv = hext_s[pl.ds(oo + SUBLANES, SUBLANES), :]
                hprev = pltpu.roll(hext_s[pl.ds(oo, 2 * SUBLANES), :], 1, axis=0)[SUBLANES:]
                gel, dgel = _gelu_parts(z_ref[rows, D_LRU:2 * D_LRU])
                dyl = dy_ref[rows, 0:D_LRU]
                a_up = jnp.where(row < SUBLANES - 1, pltpu.roll(a, SUBLANES - 1, axis=0), a_next)
                gg = _scan8_rev(a_up, dyl * gel, g_next, row)
                dgs[sub] = dyl * hv * dgel
                ilx = i * lxv
                dla = gg * hprev * a - (gg * ilx) * (a * a) / mult
                dlx_s[rows, :] = gg * mult * i
                drp = dla * spn * r * (1.0 - r)
                dip = gg * mult * lxv * i * (1.0 - i)
                drs[sub] = drp
                dis[sub] = dip
                acc_s[0] += drp
                acc_s[1] += dip
                acc_s[2] += dla * r
                a_next = jnp.broadcast_to(a[0:1, :], (SUBLANES, D_LRU))
                g_next = jnp.broadcast_to(gg[0:1, :], (SUBLANES, D_LRU))
            rows16 = pl.ds(o, grp)
            dz_ref[rows16, D_LRU:2 * D_LRU] = jnp.concatenate(dgs, axis=0).astype(BF16)
            drpb_s[rows16, :] = jnp.concatenate(drs, axis=0).astype(BF16)
            dipb_s[rows16, :] = jnp.concatenate(dis, axis=0).astype(BF16)
            return a_next, g_next

        a_c, g_c = lax.fori_loop(0, tb // grp, step, (acar[...], gcar[...]))
        acar[...] = a_c
        gcar[...] = g_c

        drpb = drpb_s[...]
        dipb = dipb_s[...]
        nt_dims = (((1,), (1,)), ((), ()))
        tn_dims = (((0,), (0,)), ((), ()))
        for q in range(4):
            sl = slice(q * 256, (q + 1) * 256)
            dlx_s[0:tb, sl] += (
                lax.dot_general(drpb[:, sl], wa_ref[q], nt_dims, preferred_element_type=F32)
                + lax.dot_general(dipb[:, sl], wx_ref[q], nt_dims, preferred_element_type=F32))
            dwa_ref[q] += lax.dot_general(lxb[:, sl], drpb[:, sl], tn_dims, preferred_element_type=F32)
            dwx_ref[q] += lax.dot_general(lxb[:, sl], dipb[:, sl], tn_dims, preferred_element_type=F32)

        dlx_ext = dlx_s[...]
        dlx = dlx_ext[0:tb]
        dz_ref[:, 0:D_LRU] = _conv_taps_t(dlx_ext, ctaps, tb).astype(BF16)
        dcw_ref[3:4, :] += jnp.sum(dlx * xp, axis=0, keepdims=True)
        for k in range(3):
            shifted = pltpu.roll(xext, 3 - k, axis=0)[SUBLANES:]
            dcw_ref[k:k + 1, :] += jnp.sum(dlx * shifted, axis=0, keepdims=True)
        dvec_ref[0:1, :] += jnp.sum(dlx, axis=0, keepdims=True)
        dlx_s[tb:, :] = dlx[0:SUBLANES]

        o_b, o_c, o_x = 2 * D_LRU, 2 * D_LRU + D_SC, 2 * D_LRU + 2 * D_SC
        sb = z_ref[:, o_b:o_c]
        scc = z_ref[:, o_c:o_x]
        sx = z_ref[:, o_x:]
        p = scc * sx
        pext = jnp.concatenate([zh_ref[:, o_c:o_x] * zh_ref[:, o_x:] * keep, p], axis=0)
        q = _conv_taps(pext, staps, tb)
        dys = dy_ref[:, D_LRU:]
        dq = dys * sb
        dp = _conv_taps_t(jnp.concatenate([dq, dqh[...]], axis=0), staps, tb)
        dscw_ref[2:3, :] += jnp.sum(dq * p, axis=0, keepdims=True)
        for k in range(2):
            shifted = pltpu.roll(pext, 2 - k, axis=0)[SUBLANES:]
            dscw_ref[k:k + 1, :] += jnp.sum(dq * shifted, axis=0, keepdims=True)
        dqh[...] = dq[0:SUBLANES]
        dz_ref[:, o_b:o_c] = (dys * q).astype(BF16)
        dz_ref[:, o_c:o_x] = (dp * sx).astype(BF16)
        dz_ref[:, o_x:] = (dp * scc).astype(BF16)

        @pl.when(first_block)
        def _():
            dvec_ref[1:2, :] = jnp.sum(acc_s[0], axis=0, keepdims=True)
            dvec_ref[2:3, :] = jnp.sum(acc_s[1], axis=0, keepdims=True)
            dvec_ref[3:4, :] = (jnp.sum(acc_s[2], axis=0, keepdims=True) * RG_C * _sigmoid(-vec[3:4]))

    blk = lambda t: (nb - 1 - t, 0)
    halo = lambda t: (jnp.maximum((nb - 1 - t) * hb - 1, 0), 0)
    const = lambda t: (0, 0)
    const3 = lambda t: (0, 0, 0)
    return pl.pallas_call(
        body, name=name, grid=(nb,),
        in_specs=[pl.BlockSpec((tb, D_IN), blk), pl.BlockSpec((SUBLANES, D_IN), halo),
                  pl.BlockSpec((tb, D_LRU), blk), pl.BlockSpec((SUBLANES, D_LRU), halo),
                  pl.BlockSpec((tb, D_MIX), blk),
                  pl.BlockSpec((SUBLANES, D_LRU), const), pl.BlockSpec((SUBLANES, D_LRU), const),
                  pl.BlockSpec((4, 256, 256), const3), pl.BlockSpec((4, 256, 256), const3),
                  pl.BlockSpec((SUBLANES, D_SC), const)],
        out_specs=[pl.BlockSpec((tb, D_IN), blk),
                   pl.BlockSpec((SUBLANES, D_LRU), const), pl.BlockSpec((SUBLANES, D_LRU), const),
                   pl.BlockSpec((4, 256, 256), const3), pl.BlockSpec((4, 256, 256), const3),
                   pl.BlockSpec((SUBLANES, D_SC), const)],
        out_shape=[jax.ShapeDtypeStruct((t_len, D_IN), BF16),
                   jax.ShapeDtypeStruct((SUBLANES, D_LRU), F32), jax.ShapeDtypeStruct((SUBLANES, D_LRU), F32),
                   jax.ShapeDtypeStruct((4, 256, 256), F32), jax.ShapeDtypeStruct((4, 256, 256), F32),
                   jax.ShapeDtypeStruct((SUBLANES, D_SC), F32)],
        scratch_shapes=[pltpu.VMEM((tb, D_LRU), F32), pltpu.VMEM((tb, D_LRU), BF16),
                        pltpu.VMEM((tb, D_LRU), F32), pltpu.VMEM((tb, D_LRU), F32),
                        pltpu.VMEM((tb, D_LRU), BF16), pltpu.VMEM((tb, D_LRU), BF16),
                        pltpu.VMEM((tb + SUBLANES, D_LRU), F32), pltpu.VMEM((tb + SUBLANES, D_LRU), F32),
                        pltpu.VMEM((3, SUBLANES, D_LRU), F32),
                        pltpu.VMEM((SUBLANES, D_LRU), F32), pltpu.VMEM((SUBLANES, D_LRU), F32),
                        pltpu.VMEM((SUBLANES, D_SC), F32)],
        compiler_params=_cp("arbitrary"),
    )(z, z, h, h, dy, cw8, vec8, wa_bd, wx_bd, scw8)


def _ffn_act(u, fw, name, tb=256, tn=512, rc=16):
    t_len = u.shape[1]
    hb = tb // SUBLANES

    def body(u_ref, uh_ref, fw_ref, o_ref, ext):
        keep = jnp.where(pl.program_id(0) == 0, 0.0, 1.0)
        ext[:, 0:SUBLANES, :] = uh_ref[...] * keep
        ext[:, SUBLANES:, :] = u_ref[...]
        fw_v = fw_ref[...]
        wg = [fw_v[0, k:k + 1, :] for k in range(3)]
        wu = [fw_v[1, k:k + 1, :] for k in range(3)]

        def chunk(ci, c):
            o = pl.multiple_of(ci * rc, rc)
            win = pl.ds(o, rc + SUBLANES)
            gate = _conv_taps(ext[0, win, :], wg, rc)
            up = _conv_taps(ext[1, win, :], wu, rc)
            o_ref[pl.ds(o, rc), :] = (_gelu(gate) * up).astype(BF16)
            return c

        lax.fori_loop(0, tb // rc, chunk, 0)

    return pl.pallas_call(
        body, name=name, grid=(t_len // tb, D_FF // tn),
        in_specs=[pl.BlockSpec((2, tb, tn), lambda i, j: (0, i, j)),
                  pl.BlockSpec((2, SUBLANES, tn), lambda i, j: (0, jnp.maximum(i * hb - 1, 0), j)),
                  pl.BlockSpec((2, SUBLANES, tn), lambda i, j: (0, 0, j))],
        out_specs=pl.BlockSpec((tb, tn), lambda i, j: (i, j)),
        out_shape=jax.ShapeDtypeStruct((t_len, D_FF), BF16),
        scratch_shapes=[pltpu.VMEM((2, tb + SUBLANES, tn), F32)],
        compiler_params=_cp("parallel", "parallel"),
    )(u, u, fw)


def _ffn_bwd(dact, u, fw, name, tb=256, tn=512, rc=16):
    t_len = u.shape[1]
    ni = t_len // tb
    hb = tb // SUBLANES
    last_halo = t_len // SUBLANES - 1

    def body(d_ref, dn_ref, u_ref, up_ref, un_ref, fw_ref, du_ref, dfw_ref, extu, extd, acc):
        i = pl.program_id(1)

        @pl.when(i == 0)
        def _():
            acc[...] = jnp.zeros_like(acc)

        keep_prev = jnp.where(i == 0, 0.0, 1.0)
        keep_next = jnp.where(i == ni - 1, 0.0, 1.0)
        extu[:, 0:SUBLANES, :] = up_ref[...] * keep_prev
        extu[:, SUBLANES:SUBLANES + tb, :] = u_ref[...]
        extu[:, SUBLANES + tb:, :] = un_ref[...]
        extd[0:tb, :] = d_ref[...]
        extd[tb:, :] = dn_ref[...] * keep_next
        fw_v = fw_ref[...]
        taps = [[fw_v[pln, k:k + 1, :] for k in range(3)] for pln in range(2)]
        m = rc + SUBLANES

        def chunk(ci, c):
            o = pl.multiple_of(ci * rc, rc)
            win = pl.ds(o, rc + 2 * SUBLANES)
            sh = [[pltpu.roll(extu[pln, win, :], 2 - k, axis=0)[SUBLANES:] for k in range(3)]
                  for pln in range(2)]
            gate = sum(taps[0][k] * sh[0][k] for k in range(3))
            up = sum(taps[1][k] * sh[1][k] for k in range(3))
            dv = extd[pl.ds(o, m), :]
            gel, dgel = _gelu_parts(gate)
            dpost = [dv * up * dgel, dv * gel]
            for pln in range(2):
                du_ref[pln, pl.ds(o, rc), :] = _conv_taps_t(dpost[pln], taps[pln], rc).astype(BF16)
                for k in range(3):
                    prod = dpost[pln][0:rc] * sh[pln][k][0:rc]
                    acc[3 * pln + k] += prod[0:SUBLANES] + prod[SUBLANES:]
            return c

        lax.fori_loop(0, tb // rc, chunk, 0)

        @pl.when(i == ni - 1)
        def _():
            dfw_ref[...] = jnp.zeros_like(dfw_ref)
            for pln in range(2):
                for k in range(3):
                    dfw_ref[pln, k:k + 1, :] = jnp.sum(acc[3 * pln + k], axis=0, keepdims=True)

    return pl.pallas_call(
        body, name=name, grid=(D_FF // tn, ni),
        in_specs=[pl.BlockSpec((tb, tn), lambda j, i: (i, j)),
                  pl.BlockSpec((SUBLANES, tn), lambda j, i: (jnp.minimum((i + 1) * hb, last_halo), j)),
                  pl.BlockSpec((2, tb, tn), lambda j, i: (0, i, j)),
                  pl.BlockSpec((2, SUBLANES, tn), lambda j, i: (0, jnp.maximum(i * hb - 1, 0), j)),
                  pl.BlockSpec((2, SUBLANES, tn), lambda j, i: (0, jnp.minimum((i + 1) * hb, last_halo), j)),
                  pl.BlockSpec((2, SUBLANES, tn), lambda j, i: (0, 0, j))],
        out_specs=[pl.BlockSpec((2, tb, tn), lambda j, i: (0, i, j)),
                   pl.BlockSpec((2, SUBLANES, tn), lambda j, i: (0, 0, j))],
        out_shape=[jax.ShapeDtypeStruct((2, t_len, D_FF), BF16),
                   jax.ShapeDtypeStruct((2, SUBLANES, D_FF), F32)],
        scratch_shapes=[pltpu.VMEM((2, tb + 2 * SUBLANES, tn), F32),
                        pltpu.VMEM((tb + SUBLANES, tn), F32),
                        pltpu.VMEM((6, SUBLANES, tn), F32)],
        compiler_params=_cp("parallel", "arbitrary"),
    )(dact, dact, u, u, u, fw)


def _loss_head(x, g, target, name, tb=256):
    t_len, d = x.shape

    def body(x_ref, g_ref, t_ref, dx_ref, dxb_ref, dg_ref, loss_ref):
        @pl.when(pl.program_id(0) == 0)
        def _():
            dg_ref[...] = jnp.zeros_like(dg_ref)
            loss_ref[...] = jnp.zeros_like(loss_ref)

        xv = x_ref[...]
        gv = g_ref[...]
        r = lax.rsqrt(jnp.mean(xv * xv, axis=-1, keepdims=True) + EPS)
        xh = xv * r
        err = xh * gv - t_ref[...]
        loss_ref[...] += (0.5 / d) * jnp.sum(jnp.sum(err * err, axis=-1, keepdims=True), axis=0, keepdims=True)
        dy = err * (1.0 / d)
        dyg = dy * gv
        dx = r * (dyg - xh * jnp.mean(dyg * xh, axis=-1, keepdims=True))
        dx_ref[...] = dx
        dxb_ref[...] = dx.astype(BF16)
        dg_ref[0:1, :] += jnp.sum(dy * xh, axis=0, keepdims=True)

    return pl.pallas_call(
        body, name=name, grid=(t_len // tb,),
        in_specs=[pl.BlockSpec((tb, d), lambda i: (i, 0)), pl.BlockSpec((1, d), lambda i: (0, 0)),
                  pl.BlockSpec((tb, d), lambda i: (i, 0))],
        out_specs=[pl.BlockSpec((tb, d), lambda i: (i, 0)), pl.BlockSpec((tb, d), lambda i: (i, 0)),
                   pl.BlockSpec((SUBLANES, d), lambda i: (0, 0)),
                   pl.BlockSpec((SUBLANES, LANES), lambda i: (0, 0))],
        out_shape=[jax.ShapeDtypeStruct((t_len, d), F32), jax.ShapeDtypeStruct((t_len, d), BF16),
                   jax.ShapeDtypeStruct((SUBLANES, d), F32), jax.ShapeDtypeStruct((SUBLANES, LANES), F32)],
        compiler_params=_cp("arbitrary"),
    )(x, g, target)


def _adamw(w, g, m, v, name):
    r, c = w.shape
    tr = 256 if r % 256 == 0 else r
    c1 = 1.0 / (1.0 - ADAM_B1 ** ADAM_STEP)
    c2 = 1.0 / (1.0 - ADAM_B2 ** ADAM_STEP)

    def body(w_ref, g_ref, m_ref, v_ref, d_ref, mo_ref, vo_ref):
        gv = g_ref[...]
        mn = ADAM_B1 * m_ref[...] + (1.0 - ADAM_B1) * gv
        vn = ADAM_B2 * v_ref[...] + (1.0 - ADAM_B2) * (gv * gv)
        d_ref[...] = -ADAM_LR * ((mn * c1) / (jnp.sqrt(vn * c2) + ADAM_EPS) + ADAM_WD * w_ref[...])
        mo_ref[...] = mn
        vo_ref[...] = vn

    spec = pl.BlockSpec((tr, c), lambda i: (i, 0))
    shape = jax.ShapeDtypeStruct((r, c), F32)
    return pl.pallas_call(
        body, name=name, grid=(r // tr,),
        in_specs=[spec] * 4, out_specs=[spec] * 3, out_shape=[shape] * 3,
        compiler_params=_cp("parallel"),
    )(w, g, m, v)


ANY = pl.BlockSpec(memory_space=pl.ANY)


def _place():
    x, y, c = lax.axis_index("x"), lax.axis_index("y"), lax.axis_index("c")
    chips = [(1 - x, y), (x, 1 - y), (1 - x, 1 - y)]
    return x, y, c, chips


def _all_gather_weights(s_in, s_out, s_up, s_down, s_conv):
    shards = [s_in, s_out, s_up, s_down]
    col_sharded = [True, False, True, False]
    full_shapes = [(2, D_MODEL, D_IN), (2, D_MIX, D_MODEL), (2, D_MODEL, 2 * D_FF), (2, D_FF, D_MODEL)]
    items = [(w, l) for l in range(2) for w in range(4)] + [(4, 0)]
    n_items = len(items)

    def body(*refs):
        srcs, fulls = refs[0:5], refs[5:10]
        send, recv, lsem = refs[10:13]
        x, y, c, chips = _place()
        me = 2 * x + y
        sib = (x, y, 1 - c)

        def shard_half(w, l, cc):
            s = srcs[w]
            if w == 4:
                return s.at[pl.ds(cc * (CONV_PACK_ROWS // 2), CONV_PACK_ROWS // 2), :]
            half = s.shape[1] // 2
            return s.at[l, pl.ds(cc * half, half), :]

        def region(w, l, k, cc):
            f = fulls[w]
            if w == 4:
                return f.at[k, pl.ds(cc * (CONV_PACK_ROWS // 2), CONV_PACK_ROWS // 2), :]
            rows, cols = shards[w].shape[1], shards[w].shape[2]
            if col_sharded[w]:
                return f.at[l, pl.ds(cc * (rows // 2), rows // 2), pl.ds(k * cols, cols)]
            return f.at[l, pl.ds(k * rows + cc * (rows // 2), rows // 2), :]

        def own_region(w, l):
            f = fulls[w]
            if w == 4:
                return f.at[me]
            rows, cols = shards[w].shape[1], shards[w].shape[2]
            if col_sharded[w]:
                return f.at[l, :, pl.ds(me * cols, cols)]
            return f.at[l, pl.ds(me * rows, rows), :]

        def remote(it, slot, src, dst, to):
            return pltpu.make_async_remote_copy(
                src_ref=src, dst_ref=dst, send_sem=send.at[it * 6 + slot], recv_sem=recv.at[it * 6 + slot],
                device_id=to, device_id_type=MESH)

        local = []
        for it, (w, l) in enumerate(items):
            src = srcs[w] if w == 4 else srcs[w].at[l]
            cp = pltpu.make_async_copy(src, own_region(w, l), lsem.at[it])
            cp.start()
            local.append(cp)
        sent = []
        for it, (w, l) in enumerate(items):
            for j, chip in enumerate(chips):
                cp = remote(it, j, shard_half(w, l, c), region(w, l, me, c), (*chip, c))
                cp.start()
                sent.append(cp)
        for it, (w, l) in enumerate(items):
            for j, (px, py) in enumerate(chips):
                got = region(w, l, 2 * px + py, c)
                remote(it, j, got, got, (px, py, c)).wait_recv()
                cp = remote(it, 3 + j, got, got, sib)
                cp.start()
                sent.append(cp)
        for it, (w, l) in enumerate(items):
            for j, (px, py) in enumerate(chips):
                got = region(w, l, 2 * px + py, 1 - c)
                remote(it, 3 + j, got, got, sib).wait_recv()
        for cp in sent:
            cp.wait_send()
        for cp in local:
            cp.wait()

    out_shape = [jax.ShapeDtypeStruct(s, BF16) for s in full_shapes]
    out_shape.append(jax.ShapeDtypeStruct((N_CHIP, CONV_PACK_ROWS, LANES), F32))
    return pl.pallas_call(
        body, name="ag_weights",
        in_specs=[ANY] * 5, out_specs=[ANY] * 5, out_shape=out_shape,
        scratch_shapes=[pltpu.SemaphoreType.DMA((n_items * 6,)), pltpu.SemaphoreType.DMA((n_items * 6,)),
                        pltpu.SemaphoreType.DMA((n_items,))],
    )(s_in, s_out, s_up, s_down, s_conv)


def _half(ref, l, axis, cc):
    if len(ref.shape) == 2:
        return ref.at[pl.ds(cc * (ref.shape[0] // 2), ref.shape[0] // 2), :]
    if axis == 1:
        return ref.at[l, pl.ds(cc * (ref.shape[1] // 2), ref.shape[1] // 2), :]
    return ref.at[l, :, pl.ds(cc * (ref.shape[2] // 2), ref.shape[2] // 2)]


GRAD_AXES = [1, 2, 1, 2, 1]


def _grad_items(arrs):
    return [(w, l) for w in range(4) for l in range(2)] + [(4, 0)]


def _pair_exchange(grads):
    items = _grad_items(grads)

    def land_shape(w):
        s = list(grads[w].shape)
        s[GRAD_AXES[w] if w < 4 else 0] //= 2
        return tuple(s)

    def body(*refs):
        srcs, lands = refs[0:5], refs[5:10]
        send, recv = refs[10:12]
        x, y, c, _ = _place()
        copies = []
        for it, (w, l) in enumerate(items):
            dst = lands[w] if w == 4 else lands[w].at[l]
            cp = pltpu.make_async_remote_copy(
                src_ref=_half(srcs[w], l, GRAD_AXES[w], 1 - c), dst_ref=dst,
                send_sem=send.at[it], recv_sem=recv.at[it], device_id=(x, y, 1 - c), device_id_type=MESH)
            cp.start()
            copies.append(cp)
        for cp in copies:
            cp.wait_send()
        for cp in copies:
            cp.wait_recv()

    return pl.pallas_call(
        body, name="rs_pair",
        in_specs=[ANY] * 5, out_specs=[ANY] * 5,
        out_shape=[jax.ShapeDtypeStruct(land_shape(w), F32) for w in range(5)],
        scratch_shapes=[pltpu.SemaphoreType.DMA((len(items),)), pltpu.SemaphoreType.DMA((len(items),))],
    )(*grads)


def _add_half(g, land, c_arr, axis, out_dtype, name):
    if g.ndim == 2:
        g3, land3 = g[None], land[None]
    else:
        g3, land3 = g, land
    nl, r, cdim = land3.shape
    tr = 256 if r % 256 == 0 else r
    nrb = r // tr

    def body(c_ref, g_ref, l_ref, o_ref):
        o_ref[...] = (g_ref[...] + l_ref[...]).astype(out_dtype)

    if axis == 1:
        g_spec = pl.BlockSpec((None, tr, cdim), lambda l, i, c_ref: (l, c_ref[0] * nrb + i, 0))
    else:
        g_spec = pl.BlockSpec((None, tr, cdim), lambda l, i, c_ref: (l, i, c_ref[0]))
    spec = pl.BlockSpec((None, tr, cdim), lambda l, i, c_ref: (l, i, 0))
    out = pl.pallas_call(
        body, name=name,
        grid_spec=pltpu.PrefetchScalarGridSpec(
            num_scalar_prefetch=1, grid=(nl, nrb), in_specs=[g_spec, spec], out_specs=spec),
        out_shape=jax.ShapeDtypeStruct(land3.shape, out_dtype),
        compiler_params=_cp("parallel", "parallel"),
    )(c_arr, g3, land3)
    return out[0] if g.ndim == 2 else out


def _scatter_to_owners(parts):
    items = _grad_items(parts)

    def piece_shape(w):
        s = list(parts[w].shape)
        if w == 4:
            return (N_CHIP, *s)
        s[3 - GRAD_AXES[w]] //= N_CHIP
        return (N_CHIP, *s)

    def body(*refs):
        srcs, slots = refs[0:5], refs[5:10]
        send, recv, lsem = refs[10:13]
        x, y, c, chips = _place()
        me = 2 * x + y

        def piece(w, l, k):
            s = srcs[w]
            if w == 4:
                return s
            if GRAD_AXES[w] == 1:
                n = s.shape[2] // N_CHIP
                return s.at[l, :, pl.ds(k * n, n)]
            n = s.shape[1] // N_CHIP
            return s.at[l, pl.ds(k * n, n), :]

        def slot(w, l, k):
            return slots[w].at[k] if w == 4 else slots[w].at[k, l]

        local, sent = [], []
        for it, (w, l) in enumerate(items):
            cp = pltpu.make_async_copy(piece(w, l, me), slot(w, l, me), lsem.at[it])
            cp.start()
            local.append(cp)
            for j, (px, py) in enumerate(chips):
                cp = pltpu.make_async_remote_copy(
                    src_ref=piece(w, l, 2 * px + py), dst_ref=slot(w, l, me),
                    send_sem=send.at[it * 3 + j], recv_sem=recv.at[it * 3 + j],
                    device_id=(px, py, c), device_id_type=MESH)
                cp.start()
                sent.append(cp)
        for it, (w, l) in enumerate(items):
            for j, (px, py) in enumerate(chips):
                got = slot(w, l, 2 * px + py)
                pltpu.make_async_remote_copy(
                    src_ref=got, dst_ref=got, send_sem=send.at[it * 3 + j], recv_sem=recv.at[it * 3 + j],
                    device_id=(px, py, c), device_id_type=MESH).wait_recv()
        for cp in sent:
            cp.wait_send()
        for cp in local:
            cp.wait()

    return pl.pallas_call(
        body, name="rs_scatter",
        in_specs=[ANY] * 5, out_specs=[ANY] * 5,
        out_shape=[jax.ShapeDtypeStruct(piece_shape(w), parts[w].dtype) for w in range(5)],
        scratch_shapes=[pltpu.SemaphoreType.DMA((len(items) * 3,)), pltpu.SemaphoreType.DMA((len(items) * 3,)),
                        pltpu.SemaphoreType.DMA((len(items),))],
    )(*parts)


def _sum_slots(slots, name):
    shape = slots.shape
    s3 = slots.reshape(N_CHIP, -1, shape[-1])
    r, cdim = s3.shape[1], s3.shape[2]
    tr = 256 if r % 256 == 0 else r

    def body(s_ref, o_ref):
        v = s_ref[...].astype(F32)
        o_ref[...] = (v[0] + v[1]) + (v[2] + v[3])

    out = pl.pallas_call(
        body, name=name, grid=(r // tr,),
        in_specs=[pl.BlockSpec((N_CHIP, tr, cdim), lambda i: (0, i, 0))],
        out_specs=pl.BlockSpec((tr, cdim), lambda i: (i, 0)),
        out_shape=jax.ShapeDtypeStruct((r, cdim), F32),
        compiler_params=_cp("parallel"),
    )(s3)
    return out.reshape(shape[1:])


def _pair_share(sums):
    items = _grad_items(sums)

    def full_shape(w):
        s = list(sums[w].shape)
        s[GRAD_AXES[w] if w < 4 else 0] *= 2
        return tuple(s)

    def body(*refs):
        srcs, fulls = refs[0:5], refs[5:10]
        send, recv, lsem = refs[10:13]
        x, y, c, _ = _place()
        local, sent = [], []
        for it, (w, l) in enumerate(items):
            src = srcs[w] if w == 4 else srcs[w].at[l]
            dst = _half(fulls[w], l, GRAD_AXES[w], c)
            cp = pltpu.make_async_copy(src, dst, lsem.at[it])
            cp.start()
            local.append(cp)
            cp = pltpu.make_async_remote_copy(
                src_ref=src, dst_ref=dst, send_sem=send.at[it], recv_sem=recv.at[it],
                device_id=(x, y, 1 - c), device_id_type=MESH)
            cp.start()
            sent.append(cp)
        for it, (w, l) in enumerate(items):
            got = _half(fulls[w], l, GRAD_AXES[w], 1 - c)
            pltpu.make_async_remote_copy(
                src_ref=got, dst_ref=got, send_sem=send.at[it], recv_sem=recv.at[it],
                device_id=(x, y, 1 - c), device_id_type=MESH).wait_recv()
        for cp in sent:
            cp.wait_send()
        for cp in local:
            cp.wait()

    return pl.pallas_call(
        body, name="rs_share",
        in_specs=[ANY] * 5, out_specs=[ANY] * 5,
        out_shape=[jax.ShapeDtypeStruct(full_shape(w), F32) for w in range(5)],
        scratch_shapes=[pltpu.SemaphoreType.DMA((len(items),)), pltpu.SemaphoreType.DMA((len(items),)),
                        pltpu.SemaphoreType.DMA((len(items),))],
    )(*sums)


def _block_diag(w):
    w4 = w.reshape(4, 4, 64, 64)
    eye = jnp.eye(4, dtype=w.dtype)[None, :, None, :, None]
    return (w4[:, :, :, None, :] * eye).reshape(4, 256, 256)


def _block_diag_extract(d):
    d5 = d.reshape(4, 4, 64, 4, 64)
    return jnp.stack([d5[:, hh, :, hh, :] for hh in range(4)], axis=1).reshape(16, 64, 64)


def _rows8(a):
    return jnp.pad(a, ((0, SUBLANES - a.shape[0]), (0, 0)))


def _pack_rep(norm1_g, conv_b, ba, bx, lam, norm2_g, wa, wx, final_g):
    parts = []
    for l in range(2):
        parts += [norm1_g[l], conv_b[l], ba[l], bx[l], lam[l], norm2_g[l], wa[l].reshape(-1), wx[l].reshape(-1)]
    parts.append(final_g)
    return jnp.concatenate(parts).reshape(REP_ROWS, LANES)


def _unpack_rep(buf):
    flat = buf.reshape(-1)
    out = {k: [] for k in ("norm1_g", "lru_conv_b", "lru_ba", "lru_bx", "lru_lambda", "norm2_g", "lru_wa", "lru_wx")}
    for l in range(2):
        o = l * REP_LAYER
        for i, k in enumerate(("norm1_g", "lru_conv_b", "lru_ba", "lru_bx", "lru_lambda", "norm2_g")):
            out[k].append(flat[o + i * 1024:o + (i + 1) * 1024])
        o += 6 * 1024
        out["lru_wa"].append(flat[o:o + 65536].reshape(16, 64, 64))
        out["lru_wx"].append(flat[o + 65536:o + 131072].reshape(16, 64, 64))
    res = {k: jnp.stack(v) for k, v in out.items()}
    res["final_g"] = flat[2 * REP_LAYER:2 * REP_LAYER + 1024]
    return res


def _pack_conv_shard(lru_cw, sc_cw, ffn_cw):
    return jnp.concatenate([lru_cw.reshape(16, LANES), jnp.pad(sc_cw.reshape(6, LANES), ((0, 2), (0, 0))),
                            ffn_cw.reshape(72, LANES)], axis=0)


def _unpack_conv_shard(buf):
    return (buf[0:16].reshape(2, 4, 256), buf[16:22].reshape(2, 3, 128), buf[24:96].reshape(2, 3, 1536))


def kernel(x, norm1_g, w_in, lru_conv_w, lru_conv_b, lru_wa, lru_ba, lru_wx, lru_bx, lru_lambda, sc_conv_w, w_out, norm2_g, w_up, ffn_conv_w, w_down, final_g, loss_target, m_norm1_g, m_w_in, m_lru_conv_w, m_lru_conv_b, m_lru_wa, m_lru_ba, m_lru_wx, m_lru_bx, m_lru_lambda, m_sc_conv_w, m_w_out, m_norm2_g, m_w_up, m_ffn_conv_w, m_w_down, m_final_g, v_norm1_g, v_w_in, v_lru_conv_w, v_lru_conv_b, v_lru_wa, v_lru_ba, v_lru_wx, v_lru_bx, v_lru_lambda, v_sc_conv_w, v_w_out, v_norm2_g, v_w_up, v_ffn_conv_w, v_w_down, v_final_g):
    me = 2 * lax.axis_index("x") + lax.axis_index("y")
    c_arr = lax.axis_index("c").astype(jnp.int32).reshape(1)

    s_conv = _pack_conv_shard(lru_conv_w, sc_conv_w, ffn_conv_w)
    wi, wo, wu, wd, convs = _all_gather_weights(
        _cast_bf16(w_in, "cast_w_in"), _cast_bf16(w_out, "cast_w_out"),
        _cast_bf16(w_up, "cast_w_up"), _cast_bf16(w_down, "cast_w_down"), s_conv)
    per_chip = [_unpack_conv_shard(convs[k]) for k in range(N_CHIP)]
    lru_cw = jnp.concatenate([p[0] for p in per_chip], axis=-1)
    sc_cw = jnp.concatenate([p[1] for p in per_chip], axis=-1)
    ffn_cw = jnp.concatenate([p[2] for p in per_chip], axis=-1)

    cw8 = [_rows8(lru_cw[l]) for l in range(2)]
    vec8 = [_rows8(jnp.stack([lru_conv_b[l], lru_ba[l], lru_bx[l], lru_lambda[l]])) for l in range(2)]
    wa_bd = [_block_diag(lru_wa[l]).astype(BF16) for l in range(2)]
    wx_bd = [_block_diag(lru_wx[l]).astype(BF16) for l in range(2)]
    scw8 = [_rows8(sc_cw[l]) for l in range(2)]
    fw8 = [jnp.pad(ffn_cw[l].reshape(3, 2, D_FF).transpose(1, 0, 2), ((0, 0), (0, 5), (0, 0))) for l in range(2)]

    xs = x[0]
    saved = []
    for l in range(2):
        z, h1 = _norm_mm(xs, norm1_g[l][None], wi, l, f"fwd_in_{l}")
        ymix, hst = _mixer_fwd(z, cw8[l], vec8[l], wa_bd[l], wx_bd[l], scw8[l], f"fwd_mixer_{l}")
        x2 = _mm_res(ymix, wo, l, xs, f"fwd_out_{l}")
        u, h2 = _norm_mm(x2, norm2_g[l][None], wu, l, f"fwd_up_{l}", planes=True)
        act = _ffn_act(u, fw8[l], f"fwd_act_{l}")
        x3 = _mm_res(act, wd, l, x2, f"fwd_down_{l}")
        saved.append((xs, h1, z, hst, ymix, x2, h2, u, act))
        xs = x3

    dx, dxb, dgf, loss_blk = _loss_head(xs, final_g[None], loss_target[0], "loss_head")
    loss = lax.psum(loss_blk[0, 0], ("x", "y", "c"))

    g_in = g_out = g_up = g_down = None
    small = [None, None]
    for l in (1, 0):
        x_in, h1, z, hst, ymix, x2, h2, u, act = saved[l]
        g_down = _mm_tn(act, dxb, l, g_down, f"bwd_wdown_{l}", tn=1024)
        dact = _mm_nt(dxb, wd, l, f"bwd_dact_{l}")
        du, dfw = _ffn_bwd(dact, u, fw8[l], f"bwd_act_{l}")
        g_up = _mm_tn(h2, du, l, g_up, f"bwd_wup_{l}", planes=True, tn=1024)
        dx2, dx2b, dg2 = _mm_nt_normbwd(du, wu, l, x2, norm2_g[l][None], dx, f"bwd_up_{l}", planes=True)
        g_out = _mm_tn(ymix, dx2b, l, g_out, f"bwd_wout_{l}", tn=1024)
        dymix = _mm_nt(dx2b, wo, l, f"bwd_dymix_{l}")
        dz, dcw, dvec, dwa, dwx, dscw = _mixer_bwd(z, hst, dymix, cw8[l], vec8[l], wa_bd[l], wx_bd[l], scw8[l],
                                                  f"bwd_mixer_{l}")
        g_in = _mm_tn(h1, dz, l, g_in, f"bwd_win_{l}", tn=1792)
        dx, dxb, dg1 = _mm_nt_normbwd(dz, wi, l, x_in, norm1_g[l][None], dx2, f"bwd_in_{l}")
        rep = [dg1[0], dvec[0], dvec[1], dvec[2], dvec[3], dg2[0],
               _block_diag_extract(dwa).reshape(-1), _block_diag_extract(dwx).reshape(-1)]
        conv = [dcw[0:4].reshape(-1), jnp.pad(dscw[0:3].reshape(-1), (0, 512)),
                dfw[:, 0:3, :].transpose(1, 0, 2).reshape(-1)]
        small[l] = (jnp.concatenate(rep), jnp.concatenate(conv))
    grad_x = dx[None]
    g_small = jnp.concatenate([small[0][0], small[1][0], dgf[0], small[0][1], small[1][1],
                               jnp.zeros((8 * LANES,), F32)]).reshape(SMALL_ROWS, LANES)

    grads = [g_in, g_out, g_up, g_down, g_small]
    lands = _pair_exchange(grads)
    parts = [_add_half(grads[w], lands[w], c_arr, GRAD_AXES[w], BF16 if w < 4 else F32, f"rs_add_{w}")
             for w in range(5)]
    slots = _scatter_to_owners(parts)
    sums = [_sum_slots(slots[w], f"rs_sum_{w}") for w in range(5)]
    gw_in, gw_out, gw_up, gw_down, gs = _pair_share(sums)

    g_rep = gs[0:REP_ROWS]
    g_conv = gs[REP_ROWS:REP_ROWS + CONV_ROWS].reshape(2, CONV_LAYER)
    g_lru_cw = lax.dynamic_slice_in_dim(g_conv[:, 0:4096].reshape(2, 4, 1024), me * 256, 256, axis=2)
    g_sc_cw = lax.dynamic_slice_in_dim(g_conv[:, 4096:4096 + 1536].reshape(2, 3, 512), me * 128, 128, axis=2)
    g_ffn_cw = lax.dynamic_slice_in_dim(g_conv[:, 6144:].reshape(2, 3, 6144), me * 1536, 1536, axis=2)

    def big(w, g, m, v, name):
        shape = w.shape
        two_d = lambda a: a.reshape(-1, shape[-1])
        return [o.reshape(shape) for o in _adamw(two_d(w), two_d(g), two_d(m), two_d(v), name)]

    upd = {"w_in": big(w_in, gw_in, m_w_in, v_w_in, "adamw_w_in"),
           "w_out": big(w_out, gw_out, m_w_out, v_w_out, "adamw_w_out"),
           "w_up": big(w_up, gw_up, m_w_up, v_w_up, "adamw_w_up"),
           "w_down": big(w_down, gw_down, m_w_down, v_w_down, "adamw_w_down")}
    rep_out = _adamw(
        _pack_rep(norm1_g, lru_conv_b, lru_ba, lru_bx, lru_lambda, norm2_g, lru_wa, lru_wx, final_g), g_rep,
        _pack_rep(m_norm1_g, m_lru_conv_b, m_lru_ba, m_lru_bx, m_lru_lambda, m_norm2_g, m_lru_wa, m_lru_wx, m_final_g),
        _pack_rep(v_norm1_g, v_lru_conv_b, v_lru_ba, v_lru_bx, v_lru_lambda, v_norm2_g, v_lru_wa, v_lru_wx, v_final_g),
        "adamw_rep")
    conv_out = _adamw(s_conv, _pack_conv_shard(g_lru_cw, g_sc_cw, g_ffn_cw),
                      _pack_conv_shard(m_lru_conv_w, m_sc_conv_w, m_ffn_conv_w),
                      _pack_conv_shard(v_lru_conv_w, v_sc_conv_w, v_ffn_conv_w), "adamw_conv")

    names = ["norm1_g", "w_in", "lru_conv_w", "lru_conv_b", "lru_wa", "lru_ba", "lru_wx", "lru_bx", "lru_lambda",
             "sc_conv_w", "w_out", "norm2_g", "w_up", "ffn_conv_w", "w_down", "final_g"]
    groups = []
    g_all = dict(_unpack_rep(g_rep))
    g_all.update(w_in=gw_in, w_out=gw_out, w_up=gw_up, w_down=gw_down,
                 lru_conv_w=g_lru_cw, sc_conv_w=g_sc_cw, ffn_conv_w=g_ffn_cw)
    groups.append(g_all)
    for i in range(3):
        d = dict(_unpack_rep(rep_out[i]))
        cl, cs, cf = _unpack_conv_shard(conv_out[i])
        d.update(lru_conv_w=cl, sc_conv_w=cs, ffn_conv_w=cf)
        d.update({k: v[i] for k, v in upd.items()})
        groups.append(d)
    return (loss, grad_x, *[grp[n] for grp in groups for n in names])
```

```python
import math

import jax
import jax.numpy as jnp
from jax import lax
from jax.experimental import pallas as pl
from jax.experimental.pallas import tpu as pltpu

F32 = jnp.float32
BF16 = jnp.bfloat16
MESH = pl.DeviceIdType.MESH

D_MODEL = 1024
D_LRU = 1024
D_SC = 512
D_MIX = D_LRU + D_SC
D_IN = 2 * D_LRU + 3 * D_SC
D_FF = 3072
N_CHIP = 4
RG_C = 8.0
EPS = 1e-6
ADAM_LR = 0.001
ADAM_B1 = 0.9
ADAM_B2 = 0.999
ADAM_EPS = 1e-08
ADAM_WD = 0.01
ADAM_STEP = 10

SUBLANES = 8
PACKED = 16
LANES = 128
VMEM_LIMIT = 56 * 1024 * 1024
GELU_C0 = math.sqrt(2.0 / math.pi)
GELU_C1 = 0.044715

REP_LAYER = 6 * 1024 + 2 * 16 * 64 * 64
REP_ROWS = (2 * REP_LAYER + 1024) // LANES
CONV_LAYER = 4 * 1024 + 2048 + 3 * 6144
CONV_ROWS = 2 * CONV_LAYER // LANES
SMALL_ROWS = REP_ROWS + CONV_ROWS + 8
CONV_PACK_ROWS = 96

ONCE = pl.Buffered(1)
ANY = pl.BlockSpec(memory_space=pl.ANY)


def _cp(*sem):
    return pltpu.CompilerParams(dimension_semantics=sem, vmem_limit_bytes=VMEM_LIMIT)


def _sigmoid(v):
    return 1.0 / (1.0 + jnp.exp(-v))


def _gelu_parts(v):
    v2 = v * v
    t = jnp.tanh(GELU_C0 * v * (1.0 + GELU_C1 * v2))
    half = 0.5 * (1.0 + t)
    gel = v * half
    dgel = half + 0.5 * v * (1.0 - t * t) * (GELU_C0 * (1.0 + 3.0 * GELU_C1 * v2))
    return gel, dgel


def _gelu(v):
    t = jnp.tanh(GELU_C0 * v * (1.0 + GELU_C1 * (v * v)))
    return 0.5 * v * (1.0 + t)


def _neg_expm1(y, a):
    p = jnp.full_like(y, 1.0 / 5040.0)
    for coef in (1.0 / 720.0, 1.0 / 120.0, 1.0 / 24.0, 1.0 / 6.0, 0.5, 1.0):
        p = p * y + coef
    return jnp.where(y > -0.3, -(p * y), 1.0 - a * a)


def _softplus_neg(lam):
    nl = -lam
    e = jnp.exp(-jnp.abs(nl))
    u = 1.0 + e
    l1p = jnp.where(u == 1.0, e, jnp.log(u) * e / (u - 1.0))
    return jnp.maximum(nl, 0.0) + l1p


def _conv_taps(ext, taps, n_out):
    kw = len(taps)
    acc = taps[kw - 1] * ext[SUBLANES:SUBLANES + n_out]
    for k in range(kw - 1):
        acc = acc + taps[k] * pltpu.roll(ext, kw - 1 - k, axis=0)[SUBLANES:SUBLANES + n_out]
    return acc


def _conv_taps_t(ext, taps, n_out):
    kw = len(taps)
    n = ext.shape[0]
    acc = taps[kw - 1] * ext[0:n_out]
    for k in range(kw - 1):
        acc = acc + taps[k] * pltpu.roll(ext, n - (kw - 1 - k), axis=0)[0:n_out]
    return acc


def _scan8(a, b, carry, row):
    for s in (1, 2, 4):
        m = row >= s
        a_sh = jnp.where(m, pltpu.roll(a, s, axis=0), 1.0)
        b_sh = jnp.where(m, pltpu.roll(b, s, axis=0), 0.0)
        b = a * b_sh + b
        a = a * a_sh
    return a * carry + b


def _scan8_rev(a, b, carry, row):
    for s in (1, 2, 4):
        m = row < SUBLANES - s
        a_sh = jnp.where(m, pltpu.roll(a, SUBLANES - s, axis=0), 1.0)
        b_sh = jnp.where(m, pltpu.roll(b, SUBLANES - s, axis=0), 0.0)
        b = a * b_sh + b
        a = a * a_sh
    return a * carry + b


def _cast_into_full(w, col_sharded, idx, name):
    nl, r, c = w.shape
    tr = 256 if r % 256 == 0 else r
    nrb = r // tr

    def body(idx_ref, w_ref, o_ref):
        o_ref[...] = w_ref[...].astype(BF16)

    if col_sharded:
        full = (nl, r, N_CHIP * c)
        o_spec = pl.BlockSpec((None, tr, c), lambda l, i, idx_ref: (l, i, idx_ref[1]))
    else:
        full = (nl, N_CHIP * r, c)
        o_spec = pl.BlockSpec((None, tr, c), lambda l, i, idx_ref: (l, idx_ref[1] * nrb + i, 0))
    return pl.pallas_call(
        body, name=name,
        grid_spec=pltpu.PrefetchScalarGridSpec(
            num_scalar_prefetch=1, grid=(nl, nrb),
            in_specs=[pl.BlockSpec((None, tr, c), lambda l, i, idx_ref: (l, i, 0))], out_specs=o_spec),
        out_shape=jax.ShapeDtypeStruct(full, BF16),
        compiler_params=_cp("parallel", "parallel"),
    )(idx, w)


def _norm_mm(x, g, w, layer, name, planes=False, tm=512, tn=512):
    t_len, d = x.shape
    n = w.shape[2]
    half = n // 2

    def body(x_ref, g_ref, w_ref, z_ref, h_ref):
        xv = x_ref[...]
        r = lax.rsqrt(jnp.mean(xv * xv, axis=-1, keepdims=True) + EPS)
        h_ref[...] = ((xv * r) * g_ref[...]).astype(BF16)
        for n0 in range(0, n, tn):
            blk = jnp.dot(h_ref[...], w_ref[:, n0:n0 + tn], preferred_element_type=F32).astype(BF16)
            if planes:
                z_ref[n0 // half, :, n0 % half:n0 % half + tn] = blk
            else:
                z_ref[:, n0:n0 + tn] = blk

    if planes:
        z_shape = jax.ShapeDtypeStruct((2, t_len, half), BF16)
        z_spec = pl.BlockSpec((2, tm, half), lambda i: (0, i, 0))
    else:
        z_shape = jax.ShapeDtypeStruct((t_len, n), BF16)
        z_spec = pl.BlockSpec((tm, n), lambda i: (i, 0))
    return pl.pallas_call(
        body, name=name, grid=(t_len // tm,),
        in_specs=[pl.BlockSpec((tm, d), lambda i: (i, 0)),
                  pl.BlockSpec((1, d), lambda i: (0, 0)),
                  pl.BlockSpec((None, d, n), lambda i: (layer, 0, 0), pipeline_mode=ONCE)],
        out_specs=[z_spec, pl.BlockSpec((tm, d), lambda i: (i, 0))],
        out_shape=[z_shape, jax.ShapeDtypeStruct((t_len, d), BF16)],
        compiler_params=_cp("parallel"),
    )(x, g, w)


def _mm_res(a, w, layer, res, name, tm=512):
    t_len, k = a.shape
    n = w.shape[2]

    def body(a_ref, w_ref, r_ref, o_ref):
        o_ref[...] = r_ref[...] + jnp.dot(a_ref[...], w_ref[...], preferred_element_type=F32)

    return pl.pallas_call(
        body, name=name, grid=(t_len // tm,),
        in_specs=[pl.BlockSpec((tm, k), lambda i: (i, 0)),
                  pl.BlockSpec((None, k, n), lambda i: (layer, 0, 0), pipeline_mode=ONCE),
                  pl.BlockSpec((tm, n), lambda i: (i, 0))],
        out_specs=pl.BlockSpec((tm, n), lambda i: (i, 0)),
        out_shape=jax.ShapeDtypeStruct((t_len, n), F32),
        compiler_params=_cp("parallel"),
    )(a, w, res)


def _mm_nt(a, w, layer, name, tm=512):
    t_len, k = a.shape
    n = w.shape[1]

    def body(a_ref, w_ref, o_ref):
        o_ref[...] = lax.dot_general(a_ref[...], w_ref[...], (((1,), (1,)), ((), ())),
                                     preferred_element_type=F32).astype(BF16)

    return pl.pallas_call(
        body, name=name, grid=(t_len // tm,),
        in_specs=[pl.BlockSpec((tm, k), lambda i: (i, 0)),
                  pl.BlockSpec((None, n, k), lambda i: (layer, 0, 0), pipeline_mode=ONCE)],
        out_specs=pl.BlockSpec((tm, n), lambda i: (i, 0)),
        out_shape=jax.ShapeDtypeStruct((t_len, n), BF16),
        compiler_params=_cp("parallel"),
    )(a, w)


def _mm_nt_normbwd(dz, w, layer, x, g, dres, name, planes=False, tm=512):
    t_len, d = x.shape
    n = w.shape[2]
    half = n // 2
    nt_dims = (((1,), (1,)), ((), ()))

    def body(dz_ref, w_ref, x_ref, g_ref, r_ref, dx_ref, dxb_ref, dg_ref):
        @pl.when(pl.program_id(0) == 0)
        def _():
            dg_ref[...] = jnp.zeros_like(dg_ref)

        if planes:
            dh = (lax.dot_general(dz_ref[0], w_ref[:, 0:half], nt_dims, preferred_element_type=F32)
                  + lax.dot_general(dz_ref[1], w_ref[:, half:], nt_dims, preferred_element_type=F32))
        else:
            dh = lax.dot_general(dz_ref[...], w_ref[...], nt_dims, preferred_element_type=F32)
        xv = x_ref[...]
        r = lax.rsqrt(jnp.mean(xv * xv, axis=-1, keepdims=True) + EPS)
        xh = xv * r
        dhg = dh * g_ref[...]
        dx = r_ref[...] + r * (dhg - xh * jnp.mean(dhg * xh, axis=-1, keepdims=True))
        dx_ref[...] = dx
        dxb_ref[...] = dx.astype(BF16)
        dg_ref[0:1, :] += jnp.sum(dh * xh, axis=0, keepdims=True)

    if planes:
        dz_spec = pl.BlockSpec((2, tm, half), lambda i: (0, i, 0))
    else:
        dz_spec = pl.BlockSpec((tm, n), lambda i: (i, 0))
    return pl.pallas_call(
        body, name=name, grid=(t_len // tm,),
        in_specs=[dz_spec,
                  pl.BlockSpec((None, d, n), lambda i: (layer, 0, 0), pipeline_mode=ONCE),
                  pl.BlockSpec((tm, d), lambda i: (i, 0)),
                  pl.BlockSpec((1, d), lambda i: (0, 0)),
                  pl.BlockSpec((tm, d), lambda i: (i, 0))],
        out_specs=[pl.BlockSpec((tm, d), lambda i: (i, 0)),
                   pl.BlockSpec((tm, d), lambda i: (i, 0)),
                   pl.BlockSpec((SUBLANES, d), lambda i: (0, 0))],
        out_shape=[jax.ShapeDtypeStruct((t_len, d), F32),
                   jax.ShapeDtypeStruct((t_len, d), BF16),
                   jax.ShapeDtypeStruct((SUBLANES, d), F32)],
        compiler_params=_cp("arbitrary"),
    )(dz, w, x, g, dres)


def _mm_tn(a, g, layer, prev, name, tk, tn, planes=False, tt=1024):
    t_len, k = a.shape
    n = 2 * g.shape[2] if planes else g.shape[1]
    nn = n // tn
    half = nn // 2
    tt = min(tt, t_len)
    nt = t_len // tt

    def body(*refs):
        a_ref, g_ref = refs[0], refs[1]
        o_ref = refs[-1]

        @pl.when(pl.program_id(2) == 0)
        def _():
            o_ref[...] = jnp.zeros_like(o_ref)

        o_ref[...] += lax.dot_general(a_ref[...], g_ref[...], (((0,), (0,)), ((), ())),
                                      preferred_element_type=F32)

    if planes:
        g_spec = pl.BlockSpec((None, tt, tn), lambda i, j, t: (j // half, t, j % half))
    else:
        g_spec = pl.BlockSpec((tt, tn), lambda i, j, t: (t, j))
    in_specs = [pl.BlockSpec((tt, tk), lambda i, j, t: (t, i)), g_spec]
    args = [a, g]
    aliases = {}
    if prev is not None:
        in_specs.append(ANY)
        args.append(prev)
        aliases = {2: 0}
    return pl.pallas_call(
        body, name=name, grid=(k // tk, nn, nt),
        in_specs=in_specs,
        out_specs=pl.BlockSpec((None, tk, tn), lambda i, j, t: (layer, i, j)),
        out_shape=jax.ShapeDtypeStruct((2, k, n), F32),
        input_output_aliases=aliases,
        compiler_params=_cp("parallel", "parallel", "arbitrary"),
    )(*args)


def _lru_gates(rp, ip, spn):
    r = _sigmoid(rp)
    i = _sigmoid(ip)
    la = r * spn
    a = jnp.exp(la)
    mult = jnp.sqrt(_neg_expm1(2.0 * la, a))
    return r, i, a, mult


def _mixer_fwd(z, cw8, vec8, wa_bd, wx_bd, scw8, name, tb=256):
    t_len = z.shape[0]

    def body(z_ref, cw_ref, vec_ref, wa_ref, wx_ref, scw_ref, y_ref, h_ref,
             xhalo, phalo, hcar, lx_s, rp_s, ip_s):
        @pl.when(pl.program_id(0) == 0)
        def _():
            xhalo[...] = jnp.zeros_like(xhalo)
            phalo[...] = jnp.zeros_like(phalo)
            hcar[...] = jnp.zeros_like(hcar)

        cw = cw_ref[...]
        vec = vec_ref[...]
        xp = z_ref[:, 0:D_LRU].astype(F32)
        ext = jnp.concatenate([xhalo[...], xp], axis=0)
        lx = vec[0:1] + _conv_taps(ext, [cw[k:k + 1] for k in range(4)], tb)
        xhalo[...] = xp[tb - SUBLANES:]
        lx_s[...] = lx
        lxb = lx.astype(BF16)
        for q in range(4):
            sl = slice(q * 256, (q + 1) * 256)
            rp_s[:, sl] = jnp.dot(lxb[:, sl], wa_ref[q], preferred_element_type=F32) + vec[1:2, sl]
            ip_s[:, sl] = jnp.dot(lxb[:, sl], wx_ref[q], preferred_element_type=F32) + vec[2:3, sl]

        spn = jnp.broadcast_to(-RG_C * _softplus_neg(vec[3:4]), (SUBLANES, D_LRU))
        row = lax.broadcasted_iota(jnp.int32, (SUBLANES, D_LRU), 0)

        def step(ci, carry):
            o = pl.multiple_of(ci * PACKED, PACKED)
            gate = z_ref[pl.ds(o, PACKED), D_LRU:2 * D_LRU].astype(F32)
            ys = []
            for sub in range(2):
                rows = pl.ds(pl.multiple_of(o + sub * SUBLANES, SUBLANES), SUBLANES)
                lxv = lx_s[rows, :]
                _, i, a, mult = _lru_gates(rp_s[rows, :], ip_s[rows, :], spn)
                h = _scan8(a, mult * (i * lxv), carry, row)
                h_ref[rows, :] = h
                ys.append(h * _gelu(gate[sub * SUBLANES:(sub + 1) * SUBLANES]))
                carry = jnp.broadcast_to(h[SUBLANES - 1:SUBLANES, :], (SUBLANES, D_LRU))
            y_ref[pl.ds(o, PACKED), 0:D_LRU] = jnp.concatenate(ys, axis=0).astype(BF16)
            return carry

        hcar[...] = lax.fori_loop(0, tb // PACKED, step, hcar[...])

        scw = scw_ref[...]
        o_b, o_c, o_x = 2 * D_LRU, 2 * D_LRU + D_SC, 2 * D_LRU + 2 * D_SC
        p = z_ref[:, o_c:o_x].astype(F32) * z_ref[:, o_x:].astype(F32)
        pext = jnp.concatenate([phalo[...], p], axis=0)
        q = _conv_taps(pext, [scw[k:k + 1] for k in range(3)], tb)
        phalo[...] = p[tb - SUBLANES:]
        y_ref[:, D_LRU:] = (z_ref[:, o_b:o_c].astype(F32) * q).astype(BF16)

    const = lambda t: (0, 0)
    return pl.pallas_call(
        body, name=name, grid=(t_len // tb,),
        in_specs=[pl.BlockSpec((tb, D_IN), lambda t: (t, 0)),
                  pl.BlockSpec((SUBLANES, D_LRU), const),
                  pl.BlockSpec((SUBLANES, D_LRU), const),
                  pl.BlockSpec((4, 256, 256), lambda t: (0, 0, 0)),
                  pl.BlockSpec((4, 256, 256), lambda t: (0, 0, 0)),
                  pl.BlockSpec((SUBLANES, D_SC), const)],
        out_specs=[pl.BlockSpec((tb, D_MIX), lambda t: (t, 0)),
                   pl.BlockSpec((tb, D_LRU), lambda t: (t, 0))],
        out_shape=[jax.ShapeDtypeStruct((t_len, D_MIX), BF16),
                   jax.ShapeDtypeStruct((t_len, D_LRU), F32)],
        scratch_shapes=[pltpu.VMEM((SUBLANES, D_LRU), F32), pltpu.VMEM((SUBLANES, D_SC), F32),
                        pltpu.VMEM((SUBLANES, D_LRU), F32), pltpu.VMEM((tb, D_LRU), F32),
                        pltpu.VMEM((tb, D_LRU), F32), pltpu.VMEM((tb, D_LRU), F32)],
        compiler_params=_cp("arbitrary"),
    )(z, cw8, vec8, wa_bd, wx_bd, scw8)


def _mixer_bwd(z, h, dy, cw8, vec8, wa_bd, wx_bd, scw8, name, tb=128):
    t_len = z.shape[0]
    nb = t_len // tb

    def body(z_ref, zh_ref, h_ref, hh_ref, dy_ref, cw_ref, vec_ref, wa_ref, wx_ref, scw_ref,
             dz_ref, dcw_ref, dvec_ref, dwa_ref, dwx_ref, dscw_ref,
             lx_s, rp_s, ip_s, drpb_s, dipb_s, dlx_s, hext_s, acc_s, acar, gcar, dqh):
        t = pl.program_id(0)
        first_block = t == nb - 1

        @pl.when(t == 0)
        def _():
            for ref in (dcw_ref, dvec_ref, dwa_ref, dwx_ref, dscw_ref, acc_s, acar, gcar, dqh):
                ref[...] = jnp.zeros_like(ref)
            dlx_s[tb:, :] = jnp.zeros((SUBLANES, D_LRU), F32)

        cw = cw_ref[...]
        vec = vec_ref[...]
        scw = scw_ref[...]
        ctaps = [cw[k:k + 1] for k in range(4)]
        staps = [scw[k:k + 1] for k in range(3)]
        keep = jnp.where(first_block, 0.0, 1.0)
        zh = zh_ref[...].astype(F32)[PACKED - SUBLANES:] * keep

        xp = z_ref[:, 0:D_LRU].astype(F32)
        xext = jnp.concatenate([zh[:, 0:D_LRU], xp], axis=0)
        lx = vec[0:1] + _conv_taps(xext, ctaps, tb)
        lx_s[...] = lx
        lxb = lx.astype(BF16)
        for q in range(4):
            sl = slice(q * 256, (q + 1) * 256)
            rp_s[:, sl] = jnp.dot(lxb[:, sl], wa_ref[q], preferred_element_type=F32) + vec[1:2, sl]
            ip_s[:, sl] = jnp.dot(lxb[:, sl], wx_ref[q], preferred_element_type=F32) + vec[2:3, sl]
        hext_s[0:SUBLANES, :] = hh_ref[...] * keep
        hext_s[SUBLANES:, :] = h_ref[...]

        spn = jnp.broadcast_to(-RG_C * _softplus_neg(vec[3:4]), (SUBLANES, D_LRU))
        row = lax.broadcasted_iota(jnp.int32, (SUBLANES, D_LRU), 0)

        def step(ci, carry):
            a_next, g_next = carry
            o = pl.multiple_of((tb // PACKED - 1 - ci) * PACKED, PACKED)
            rows16 = pl.ds(o, PACKED)
            gate16 = z_ref[rows16, D_LRU:2 * D_LRU].astype(F32)
            dyl16 = dy_ref[rows16, 0:D_LRU].astype(F32)
            dgs, drs, dis = [None, None], [None, None], [None, None]
            for sub in (1, 0):
                oo = pl.multiple_of(o + sub * SUBLANES, SUBLANES)
                rows = pl.ds(oo, SUBLANES)
                half = slice(sub * SUBLANES, (sub + 1) * SUBLANES)
                lxv = lx_s[rows, :]
                r, i, a, mult = _lru_gates(rp_s[rows, :], ip_s[rows, :], spn)
                hwin = hext_s[pl.ds(oo, 2 * SUBLANES), :]
                hv = hwin[SUBLANES:]
                hprev = pltpu.roll(hwin, 1, axis=0)[SUBLANES:]
                gel, dgel = _gelu_parts(gate16[half])
                dyl = dyl16[half]
                a_up = jnp.where(row < SUBLANES - 1, pltpu.roll(a, SUBLANES - 1, axis=0), a_next)
                gg = _scan8_rev(a_up, dyl * gel, g_next, row)
                dgs[sub] = dyl * hv * dgel
                ilx = i * lxv
                dla = gg * hprev * a - (gg * ilx) * (a * a) / mult
                dlx_s[rows, :] = gg * mult * i
                drp = dla * spn * r * (1.0 - r)
                dip = gg * mult * lxv * i * (1.0 - i)
                drs[sub] = drp
                dis[sub] = dip
                acc_s[0] += drp
                acc_s[1] += dip
                acc_s[2] += dla * r
                a_next = jnp.broadcast_to(a[0:1, :], (SUBLANES, D_LRU))
                g_next = jnp.broadcast_to(gg[0:1, :], (SUBLANES, D_LRU))
            dz_ref[rows16, D_LRU:2 * D_LRU] = jnp.concatenate(dgs, axis=0).astype(BF16)
            drpb_s[rows16, :] = jnp.concatenate(drs, axis=0).astype(BF16)
            dipb_s[rows16, :] = jnp.concatenate(dis, axis=0).astype(BF16)
            return a_next, g_next

        a_c, g_c = lax.fori_loop(0, tb // PACKED, step, (acar[...], gcar[...]))
        acar[...] = a_c
        gcar[...] = g_c

        drpb = drpb_s[...]
        dipb = dipb_s[...]
        nt_dims = (((1,), (1,)), ((), ()))
        tn_dims = (((0,), (0,)), ((), ()))
        for q in range(4):
            sl = slice(q * 256, (q + 1) * 256)
            dlx_s[0:tb, sl] += (
                lax.dot_general(drpb[:, sl], wa_ref[q], nt_dims, preferred_element_type=F32)
                + lax.dot_general(dipb[:, sl], wx_ref[q], nt_dims, preferred_element_type=F32))
            dwa_ref[q] += lax.dot_general(lxb[:, sl], drpb[:, sl], tn_dims, preferred_element_type=F32)
            dwx_ref[q] += lax.dot_general(lxb[:, sl], dipb[:, sl], tn_dims, preferred_element_type=F32)

        dlx_ext = dlx_s[...]
        dlx = dlx_ext[0:tb]
        dz_ref[:, 0:D_LRU] = _conv_taps_t(dlx_ext, ctaps, tb).astype(BF16)
        dcw_ref[3:4, :] += jnp.sum(dlx * xp, axis=0, keepdims=True)
        for k in range(3):
            shifted = pltpu.roll(xext, 3 - k, axis=0)[SUBLANES:]
            dcw_ref[k:k + 1, :] += jnp.sum(dlx * shifted, axis=0, keepdims=True)
        dvec_ref[0:1, :] += jnp.sum(dlx, axis=0, keepdims=True)
        dlx_s[tb:, :] = dlx[0:SUBLANES]

        o_b, o_c, o_x = 2 * D_LRU, 2 * D_LRU + D_SC, 2 * D_LRU + 2 * D_SC
        sb = z_ref[:, o_b:o_c].astype(F32)
        scc = z_ref[:, o_c:o_x].astype(F32)
        sx = z_ref[:, o_x:].astype(F32)
        p = scc * sx
        pext = jnp.concatenate([zh[:, o_c:o_x] * zh[:, o_x:], p], axis=0)
        q = _conv_taps(pext, staps, tb)
        dys = dy_ref[:, D_LRU:].astype(F32)
        dq = dys * sb
        dp = _conv_taps_t(jnp.concatenate([dq, dqh[...]], axis=0), staps, tb)
        dscw_ref[2:3, :] += jnp.sum(dq * p, axis=0, keepdims=True)
        for k in range(2):
            shifted = pltpu.roll(pext, 2 - k, axis=0)[SUBLANES:]
            dscw_ref[k:k + 1, :] += jnp.sum(dq * shifted, axis=0, keepdims=True)
        dqh[...] = dq[0:SUBLANES]
        dz_ref[:, o_b:o_c] = (dys * q).astype(BF16)
        dz_ref[:, o_c:o_x] = (dp * sx).astype(BF16)
        dz_ref[:, o_x:] = (dp * scc).astype(BF16)

        @pl.when(first_block)
        def _():
            dvec_ref[1:2, :] = jnp.sum(acc_s[0], axis=0, keepdims=True)
            dvec_ref[2:3, :] = jnp.sum(acc_s[1], axis=0, keepdims=True)
            dvec_ref[3:4, :] = (jnp.sum(acc_s[2], axis=0, keepdims=True) * RG_C * _sigmoid(-vec[3:4]))

    blk = lambda t: (nb - 1 - t, 0)
    halo8 = lambda t: (jnp.maximum((nb - 1 - t) * (tb // SUBLANES) - 1, 0), 0)
    halo16 = lambda t: (jnp.maximum((nb - 1 - t) * (tb // PACKED) - 1, 0), 0)
    const = lambda t: (0, 0)
    const3 = lambda t: (0, 0, 0)
    return pl.pallas_call(
        body, name=name, grid=(nb,),
        in_specs=[pl.BlockSpec((tb, D_IN), blk), pl.BlockSpec((PACKED, D_IN), halo16),
                  pl.BlockSpec((tb, D_LRU), blk), pl.BlockSpec((SUBLANES, D_LRU), halo8),
                  pl.BlockSpec((tb, D_MIX), blk),
                  pl.BlockSpec((SUBLANES, D_LRU), const), pl.BlockSpec((SUBLANES, D_LRU), const),
                  pl.BlockSpec((4, 256, 256), const3), pl.BlockSpec((4, 256, 256), const3),
                  pl.BlockSpec((SUBLANES, D_SC), const)],
        out_specs=[pl.BlockSpec((tb, D_IN), blk),
                   pl.BlockSpec((SUBLANES, D_LRU), const), pl.BlockSpec((SUBLANES, D_LRU), const),
                   pl.BlockSpec((4, 256, 256), const3), pl.BlockSpec((4, 256, 256), const3),
                   pl.BlockSpec((SUBLANES, D_SC), const)],
        out_shape=[jax.ShapeDtypeStruct((t_len, D_IN), BF16),
                   jax.ShapeDtypeStruct((SUBLANES, D_LRU), F32), jax.ShapeDtypeStruct((SUBLANES, D_LRU), F32),
                   jax.ShapeDtypeStruct((4, 256, 256), F32), jax.ShapeDtypeStruct((4, 256, 256), F32),
                   jax.ShapeDtypeStruct((SUBLANES, D_SC), F32)],
        scratch_shapes=[pltpu.VMEM((tb, D_LRU), F32),
                        pltpu.VMEM((tb, D_LRU), F32), pltpu.VMEM((tb, D_LRU), F32),
                        pltpu.VMEM((tb, D_LRU), BF16), pltpu.VMEM((tb, D_LRU), BF16),
                        pltpu.VMEM((tb + SUBLANES, D_LRU), F32), pltpu.VMEM((tb + SUBLANES, D_LRU), F32),
                        pltpu.VMEM((3, SUBLANES, D_LRU), F32),
                        pltpu.VMEM((SUBLANES, D_LRU), F32), pltpu.VMEM((SUBLANES, D_LRU), F32),
                        pltpu.VMEM((SUBLANES, D_SC), F32)],
        compiler_params=_cp("arbitrary"),
    )(z, z, h, h, dy, cw8, vec8, wa_bd, wx_bd, scw8)


def _ffn_act(u, fw, name, tb=256, tn=512, rc=16):
    t_len = u.shape[1]
    hb = tb // PACKED

    def body(u_ref, uh_ref, fw_ref, o_ref, ext):
        keep = jnp.where(pl.program_id(0) == 0, 0.0, 1.0)
        ext[:, 0:SUBLANES, :] = uh_ref[...].astype(F32)[:, PACKED - SUBLANES:, :] * keep
        ext[:, SUBLANES:, :] = u_ref[...].astype(F32)
        fw_v = fw_ref[...]
        wg = [fw_v[0, k:k + 1, :] for k in range(3)]
        wu = [fw_v[1, k:k + 1, :] for k in range(3)]

        def chunk(ci, c):
            o = pl.multiple_of(ci * rc, rc)
            win = pl.ds(o, rc + SUBLANES)
            gate = _conv_taps(ext[0, win, :], wg, rc)
            up = _conv_taps(ext[1, win, :], wu, rc)
            o_ref[pl.ds(o, rc), :] = (_gelu(gate) * up).astype(BF16)
            return c

        lax.fori_loop(0, tb // rc, chunk, 0)

    return pl.pallas_call(
        body, name=name, grid=(t_len // tb, D_FF // tn),
        in_specs=[pl.BlockSpec((2, tb, tn), lambda i, j: (0, i, j)),
                  pl.BlockSpec((2, PACKED, tn), lambda i, j: (0, jnp.maximum(i * hb - 1, 0), j)),
                  pl.BlockSpec((2, SUBLANES, tn), lambda i, j: (0, 0, j))],
        out_specs=pl.BlockSpec((tb, tn), lambda i, j: (i, j)),
        out_shape=jax.ShapeDtypeStruct((t_len, D_FF), BF16),
        scratch_shapes=[pltpu.VMEM((2, tb + SUBLANES, tn), F32)],
        compiler_params=_cp("parallel", "parallel"),
    )(u, u, fw)


def _ffn_bwd(dact, u, fw, name, tb=256, tn=512, rc=16):
    t_len = u.shape[1]
    ni = t_len // tb
    hb = tb // PACKED
    last_halo = t_len // PACKED - 1

    def body(d_ref, dn_ref, u_ref, up_ref, un_ref, fw_ref, du_ref, dfw_ref, extu, extd, acc):
        i = pl.program_id(1)

        @pl.when(i == 0)
        def _():
            acc[...] = jnp.zeros_like(acc)

        keep_prev = jnp.where(i == 0, 0.0, 1.0)
        keep_next = jnp.where(i == ni - 1, 0.0, 1.0)
        extu[:, 0:SUBLANES, :] = up_ref[...].astype(F32)[:, PACKED - SUBLANES:, :] * keep_prev
        extu[:, SUBLANES:SUBLANES + tb, :] = u_ref[...].astype(F32)
        extu[:, SUBLANES + tb:, :] = un_ref[...].astype(F32)[:, 0:SUBLANES, :]
        extd[0:tb, :] = d_ref[...].astype(F32)
        extd[tb:, :] = dn_ref[...].astype(F32)[0:SUBLANES] * keep_next
        fw_v = fw_ref[...]
        taps = [[fw_v[pln, k:k + 1, :] for k in range(3)] for pln in range(2)]
        m = rc + SUBLANES

        def chunk(ci, c):
            o = pl.multiple_of(ci * rc, rc)
            win = pl.ds(o, rc + 2 * SUBLANES)
            sh = []
            for pln in range(2):
                e = extu[pln, win, :]
                sh.append([pltpu.roll(e, 2, axis=0)[SUBLANES:], pltpu.roll(e, 1, axis=0)[SUBLANES:], e[SUBLANES:]])
            gate = sum(taps[0][k] * sh[0][k] for k in range(3))
            up = sum(taps[1][k] * sh[1][k] for k in range(3))
            dv = extd[pl.ds(o, m), :]
            gel, dgel = _gelu_parts(gate)
            dpost = [dv * up * dgel, dv * gel]
            for pln in range(2):
                du_ref[pln, pl.ds(o, rc), :] = _conv_taps_t(dpost[pln], taps[pln], rc).astype(BF16)
                for k in range(3):
                    prod = dpost[pln][0:rc] * sh[pln][k][0:rc]
                    acc[3 * pln + k] += prod[0:SUBLANES] + prod[SUBLANES:]
            return c

        lax.fori_loop(0, tb // rc, chunk, 0)

        @pl.when(i == ni - 1)
        def _():
            dfw_ref[...] = jnp.zeros_like(dfw_ref)
            for pln in range(2):
                for k in range(3):
                    dfw_ref[pln, k:k + 1, :] = jnp.sum(acc[3 * pln + k], axis=0, keepdims=True)

    return pl.pallas_call(
        body, name=name, grid=(D_FF // tn, ni),
        in_specs=[pl.BlockSpec((tb, tn), lambda j, i: (i, j)),
                  pl.BlockSpec((PACKED, tn), lambda j, i: (jnp.minimum((i + 1) * hb, last_halo), j)),
                  pl.BlockSpec((2, tb, tn), lambda j, i: (0, i, j)),
                  pl.BlockSpec((2, PACKED, tn), lambda j, i: (0, jnp.maximum(i * hb - 1, 0), j)),
                  pl.BlockSpec((2, PACKED, tn), lambda j, i: (0, jnp.minimum((i + 1) * hb, last_halo), j)),
                  pl.BlockSpec((2, SUBLANES, tn), lambda j, i: (0, 0, j))],
        out_specs=[pl.BlockSpec((2, tb, tn), lambda j, i: (0, i, j)),
                   pl.BlockSpec((2, SUBLANES, tn), lambda j, i: (0, 0, j))],
        out_shape=[jax.ShapeDtypeStruct((2, t_len, D_FF), BF16),
                   jax.ShapeDtypeStruct((2, SUBLANES, D_FF), F32)],
        scratch_shapes=[pltpu.VMEM((2, tb + 2 * SUBLANES, tn), F32),
                        pltpu.VMEM((tb + SUBLANES, tn), F32),
                        pltpu.VMEM((6, SUBLANES, tn), F32)],
        compiler_params=_cp("parallel", "arbitrary"),
    )(dact, dact, u, u, u, fw)


def _loss_head(x, g, target, name, tb=256):
    t_len, d = x.shape

    def body(x_ref, g_ref, t_ref, dx_ref, dxb_ref, dg_ref, loss_ref):
        @pl.when(pl.program_id(0) == 0)
        def _():
            dg_ref[...] = jnp.zeros_like(dg_ref)
            loss_ref[...] = jnp.zeros_like(loss_ref)

        xv = x_ref[...]
        gv = g_ref[...]
        r = lax.rsqrt(jnp.mean(xv * xv, axis=-1, keepdims=True) + EPS)
        xh = xv * r
        err = xh * gv - t_ref[...]
        loss_ref[...] += (0.5 / d) * jnp.sum(jnp.sum(err * err, axis=-1, keepdims=True), axis=0, keepdims=True)
        dy = err * (1.0 / d)
        dyg = dy * gv
        dx = r * (dyg - xh * jnp.mean(dyg * xh, axis=-1, keepdims=True))
        dx_ref[...] = dx
        dxb_ref[...] = dx.astype(BF16)
        dg_ref[0:1, :] += jnp.sum(dy * xh, axis=0, keepdims=True)

    return pl.pallas_call(
        body, name=name, grid=(t_len // tb,),
        in_specs=[pl.BlockSpec((tb, d), lambda i: (i, 0)), pl.BlockSpec((1, d), lambda i: (0, 0)),
                  pl.BlockSpec((tb, d), lambda i: (i, 0))],
        out_specs=[pl.BlockSpec((tb, d), lambda i: (i, 0)), pl.BlockSpec((tb, d), lambda i: (i, 0)),
                   pl.BlockSpec((SUBLANES, d), lambda i: (0, 0)),
                   pl.BlockSpec((SUBLANES, LANES), lambda i: (0, 0))],
        out_shape=[jax.ShapeDtypeStruct((t_len, d), F32), jax.ShapeDtypeStruct((t_len, d), BF16),
                   jax.ShapeDtypeStruct((SUBLANES, d), F32), jax.ShapeDtypeStruct((SUBLANES, LANES), F32)],
        compiler_params=_cp("arbitrary"),
    )(x, g, target)


def _adamw(w, g, m, v, name):
    r, c = w.shape
    tr = 256 if r % 256 == 0 else r
    c1 = 1.0 / (1.0 - ADAM_B1 ** ADAM_STEP)
    c2 = 1.0 / (1.0 - ADAM_B2 ** ADAM_STEP)

    def body(w_ref, g_ref, m_ref, v_ref, d_ref, mo_ref, vo_ref):
        gv = g_ref[...]
        mn = ADAM_B1 * m_ref[...] + (1.0 - ADAM_B1) * gv
        vn = ADAM_B2 * v_ref[...] + (1.0 - ADAM_B2) * (gv * gv)
        d_ref[...] = -ADAM_LR * ((mn * c1) / (jnp.sqrt(vn * c2) + ADAM_EPS) + ADAM_WD * w_ref[...])
        mo_ref[...] = mn
        vo_ref[...] = vn

    spec = pl.BlockSpec((tr, c), lambda i: (i, 0))
    shape = jax.ShapeDtypeStruct((r, c), F32)
    return pl.pallas_call(
        body, name=name, grid=(r // tr,),
        in_specs=[spec] * 4, out_specs=[spec] * 3, out_shape=[shape] * 3,
        compiler_params=_cp("parallel"),
    )(w, g, m, v)


def _place():
    x, y, c = lax.axis_index("x"), lax.axis_index("y"), lax.axis_index("c")
    chips = [(1 - x, y), (x, 1 - y), (1 - x, 1 - y)]
    return x, y, c, chips


def _all_gather_weights(f_in, f_out, f_up, f_down, f_conv):
    fulls_in = [f_in, f_out, f_up, f_down, f_conv]
    col_sharded = [True, False, True, False]
    items = [(w, l) for l in range(2) for w in range(4)] + [(4, 0)]
    n_items = len(items)

    def body(*refs):
        fulls = refs[5:10]
        send, recv = refs[10:12]
        x, y, c, chips = _place()
        me = 2 * x + y
        sib = (x, y, 1 - c)

        def region(w, l, k, cc):
            f = fulls[w]
            if w == 4:
                return f.at[k, pl.ds(cc * (CONV_PACK_ROWS // 2), CONV_PACK_ROWS // 2), :]
            if col_sharded[w]:
                rows, cols = f.shape[1], f.shape[2] // N_CHIP
                return f.at[l, pl.ds(cc * (rows // 2), rows // 2), pl.ds(k * cols, cols)]
            rows = f.shape[1] // N_CHIP
            return f.at[l, pl.ds(k * rows + cc * (rows // 2), rows // 2), :]

        def remote(it, slot, reg, to):
            return pltpu.make_async_remote_copy(
                src_ref=reg, dst_ref=reg, send_sem=send.at[it * 6 + slot], recv_sem=recv.at[it * 6 + slot],
                device_id=to, device_id_type=MESH)

        sent = []
        for it, (w, l) in enumerate(items):
            for j, chip in enumerate(chips):
                cp = remote(it, j, region(w, l, me, c), (*chip, c))
                cp.start()
                sent.append(cp)
        for it, (w, l) in enumerate(items):
            for j, (px, py) in enumerate(chips):
                got = region(w, l, 2 * px + py, c)
                remote(it, j, got, (px, py, c)).wait_recv()
                cp = remote(it, 3 + j, got, sib)
                cp.start()
                sent.append(cp)
        for it, (w, l) in enumerate(items):
            for j, (px, py) in enumerate(chips):
                remote(it, 3 + j, region(w, l, 2 * px + py, 1 - c), sib).wait_recv()
        for cp in sent:
            cp.wait_send()

    return pl.pallas_call(
        body, name="ag_weights",
        in_specs=[ANY] * 5, out_specs=[ANY] * 5,
        out_shape=[jax.ShapeDtypeStruct(f.shape, f.dtype) for f in fulls_in],
        input_output_aliases={i: i for i in range(5)},
        scratch_shapes=[pltpu.SemaphoreType.DMA((n_items * 6,)), pltpu.SemaphoreType.DMA((n_items * 6,))],
    )(*fulls_in)


def _half(ref, l, axis, cc):
    if len(ref.shape) == 2:
        return ref.at[pl.ds(cc * (ref.shape[0] // 2), ref.shape[0] // 2), :]
    if axis == 1:
        return ref.at[l, pl.ds(cc * (ref.shape[1] // 2), ref.shape[1] // 2), :]
    return ref.at[l, :, pl.ds(cc * (ref.shape[2] // 2), ref.shape[2] // 2)]


GRAD_AXES = [1, 2, 1, 2, 1]
GRAD_ITEMS = [(w, l) for w in range(4) for l in range(2)] + [(4, 0)]


def _pair_exchange(grads):
    def land_shape(w):
        s = list(grads[w].shape)
        s[GRAD_AXES[w] if w < 4 else 0] //= 2
        return tuple(s)

    def body(*refs):
        srcs, lands = refs[0:5], refs[5:10]
        send, recv = refs[10:12]
        x, y, c, _ = _place()
        copies = []
        for it, (w, l) in enumerate(GRAD_ITEMS):
            dst = lands[w] if w == 4 else lands[w].at[l]
            cp = pltpu.make_async_remote_copy(
                src_ref=_half(srcs[w], l, GRAD_AXES[w], 1 - c), dst_ref=dst,
                send_sem=send.at[it], recv_sem=recv.at[it], device_id=(x, y, 1 - c), device_id_type=MESH)
            cp.start()
            copies.append(cp)
        for cp in copies:
            cp.wait_send()
        for cp in copies:
            cp.wait_recv()

    n = len(GRAD_ITEMS)
    return pl.pallas_call(
        body, name="rs_pair",
        in_specs=[ANY] * 5, out_specs=[ANY] * 5,
        out_shape=[jax.ShapeDtypeStruct(land_shape(w), F32) for w in range(5)],
        scratch_shapes=[pltpu.SemaphoreType.DMA((n,)), pltpu.SemaphoreType.DMA((n,))],
    )(*grads)


def _pair_sum(g, land, idx, w, name):
    axis = GRAD_AXES[w]
    odt = BF16 if w < 4 else F32

    def body(idx_ref, g_ref, l_ref, p_ref, s_ref):
        v = (g_ref[...] + l_ref[...]).astype(odt)
        p_ref[...] = v
        if w == 4:
            s_ref[...] = v
        else:
            @pl.when(pl.program_id(2 if axis == 1 else 1) == idx_ref[1])
            def _():
                s_ref[...] = v

    if w == 4:
        g, land = g[None], land[None]
        r = land.shape[1]
        grid = (1,)
        g_spec = pl.BlockSpec((None, r, LANES), lambda i, idx_ref: (0, idx_ref[0], 0))
        spec = pl.BlockSpec((None, r, LANES), lambda i, idx_ref: (0, 0, 0))
        s_spec = pl.BlockSpec((None, r, LANES), lambda i, idx_ref: (idx_ref[1], 0, 0))
        s_shape = (N_CHIP, r, LANES)
    elif axis == 1:
        _, r, cdim = land.shape
        pc, tr = cdim // N_CHIP, 256
        nrb = r // tr
        grid = (2, nrb, N_CHIP)
        g_spec = pl.BlockSpec((None, tr, pc), lambda l, i, k, idx_ref: (l, idx_ref[0] * nrb + i, k))
        spec = pl.BlockSpec((None, tr, pc), lambda l, i, k, idx_ref: (l, i, k))
        s_spec = pl.BlockSpec((None, None, tr, pc), lambda l, i, k, idx_ref: (idx_ref[1], l, i, 0))
        s_shape = (N_CHIP, 2, r, pc)
    else:
        _, r, cdim = land.shape
        pr = r // N_CHIP
        grid = (2, N_CHIP)
        g_spec = pl.BlockSpec((None, pr, cdim), lambda l, k, idx_ref: (l, k, idx_ref[0]))
        spec = pl.BlockSpec((None, pr, cdim), lambda l, k, idx_ref: (l, k, 0))
        s_spec = pl.BlockSpec((None, None, pr, cdim), lambda l, k, idx_ref: (idx_ref[1], l, 0, 0))
        s_shape = (N_CHIP, 2, pr, cdim)
    part, slots = pl.pallas_call(
        body, name=name,
        grid_spec=pltpu.PrefetchScalarGridSpec(
            num_scalar_prefetch=1, grid=grid, in_specs=[g_spec, spec], out_specs=[spec, s_spec]),
        out_shape=[jax.ShapeDtypeStruct(land.shape, odt), jax.ShapeDtypeStruct(s_shape, odt)],
        compiler_params=_cp(*(["arbitrary"] * len(grid))),
    )(idx, g, land)
    return (part[0] if w == 4 else part), slots


def _scatter_to_owners(parts, slots):
    def body(*refs):
        srcs, out = refs[0:5], refs[10:15]
        send, recv = refs[15:17]
        x, y, c, chips = _place()
        me = 2 * x + y

        def piece(w, l, k):
            s = srcs[w]
            if w == 4:
                return s
            if GRAD_AXES[w] == 1:
                n = s.shape[2] // N_CHIP
                return s.at[l, :, pl.ds(k * n, n)]
            n = s.shape[1] // N_CHIP
            return s.at[l, pl.ds(k * n, n), :]

        def slot(w, l, k):
            return out[w].at[k] if w == 4 else out[w].at[k, l]

        sent = []
        for it, (w, l) in enumerate(GRAD_ITEMS):
            for j, (px, py) in enumerate(chips):
                cp = pltpu.make_async_remote_copy(
                    src_ref=piece(w, l, 2 * px + py), dst_ref=slot(w, l, me),
                    send_sem=send.at[it * 3 + j], recv_sem=recv.at[it * 3 + j],
                    device_id=(px, py, c), device_id_type=MESH)
                cp.start()
                sent.append(cp)
        for it, (w, l) in enumerate(GRAD_ITEMS):
            for j, (px, py) in enumerate(chips):
                got = slot(w, l, 2 * px + py)
                pltpu.make_async_remote_copy(
                    src_ref=got, dst_ref=got, send_sem=send.at[it * 3 + j], recv_sem=recv.at[it * 3 + j],
                    device_id=(px, py, c), device_id_type=MESH).wait_recv()
        for cp in sent:
            cp.wait_send()

    n = len(GRAD_ITEMS)
    return pl.pallas_call(
        body, name="rs_scatter",
        in_specs=[ANY] * 10, out_specs=[ANY] * 5,
        out_shape=[jax.ShapeDtypeStruct(s.shape, s.dtype) for s in slots],
        input_output_aliases={5 + i: i for i in range(5)},
        scratch_shapes=[pltpu.SemaphoreType.DMA((n * 3,)), pltpu.SemaphoreType.DMA((n * 3,))],
    )(*parts, *slots)


def _sum_slots(slots, idx, w, name):
    axis = GRAD_AXES[w]

    def body(idx_ref, s_ref, o_ref):
        v = s_ref[...].astype(F32)
        o_ref[...] = (v[0] + v[1]) + (v[2] + v[3])

    if w == 4:
        _, r, cdim = slots.shape
        grid = (1,)
        s_spec = pl.BlockSpec((N_CHIP, r, cdim), lambda i, idx_ref: (0, 0, 0))
        o_spec = pl.BlockSpec((r, cdim), lambda i, idx_ref: (idx_ref[0], 0))
        full = (2 * r, cdim)
    else:
        _, _, r, cdim = slots.shape
        tr = 256 if r % 256 == 0 else 384
        nrb = r // tr
        grid = (2, nrb)
        s_spec = pl.BlockSpec((N_CHIP, None, tr, cdim), lambda l, i, idx_ref: (0, l, i, 0))
        if axis == 1:
            o_spec = pl.BlockSpec((None, tr, cdim), lambda l, i, idx_ref: (l, idx_ref[0] * nrb + i, 0))
            full = (2, 2 * r, cdim)
        else:
            o_spec = pl.BlockSpec((None, tr, cdim), lambda l, i, idx_ref: (l, i, idx_ref[0]))
            full = (2, r, 2 * cdim)
    return pl.pallas_call(
        body, name=name,
        grid_spec=pltpu.PrefetchScalarGridSpec(
            num_scalar_prefetch=1, grid=grid, in_specs=[s_spec], out_specs=o_spec),
        out_shape=jax.ShapeDtypeStruct(full, F32),
        compiler_params=_cp(*(["parallel"] * len(grid))),
    )(idx, slots)


def _pair_share(fulls):
    def body(*refs):
        out = refs[5:10]
        send, recv = refs[10:12]
        x, y, c, _ = _place()
        sent = []
        for it, (w, l) in enumerate(GRAD_ITEMS):
            mine = _half(out[w], l, GRAD_AXES[w], c)
            cp = pltpu.make_async_remote_copy(
                src_ref=mine, dst_ref=mine, send_sem=send.at[it], recv_sem=recv.at[it],
                device_id=(x, y, 1 - c), device_id_type=MESH)
            cp.start()
            sent.append(cp)
        for it, (w, l) in enumerate(GRAD_ITEMS):
            got = _half(out[w], l, GRAD_AXES[w], 1 - c)
            pltpu.make_async_remote_copy(
                src_ref=got, dst_ref=got, send_sem=send.at[it], recv_sem=recv.at[it],
                device_id=(x, y, 1 - c), device_id_type=MESH).wait_recv()
        for cp in sent:
            cp.wait_send()

    n = len(GRAD_ITEMS)
    return pl.pallas_call(
        body, name="rs_share",
        in_specs=[ANY] * 5, out_specs=[ANY] * 5,
        out_shape=[jax.ShapeDtypeStruct(f.shape, f.dtype) for f in fulls],
        input_output_aliases={i: i for i in range(5)},
        scratch_shapes=[pltpu.SemaphoreType.DMA((n,)), pltpu.SemaphoreType.DMA((n,))],
    )(*fulls)


def _block_diag(w):
    w4 = w.reshape(4, 4, 64, 64)
    eye = jnp.eye(4, dtype=w.dtype)[None, :, None, :, None]
    return (w4[:, :, :, None, :] * eye).reshape(4, 256, 256)


def _block_diag_extract(d):
    d5 = d.reshape(4, 4, 64, 4, 64)
    return jnp.stack([d5[:, hh, :, hh, :] for hh in range(4)], axis=1).reshape(16, 64, 64)


def _rows8(a):
    return jnp.pad(a, ((0, SUBLANES - a.shape[0]), (0, 0)))


def _pack_rep(norm1_g, conv_b, ba, bx, lam, norm2_g, wa, wx, final_g):
    parts = []
    for l in range(2):
        parts += [norm1_g[l], conv_b[l], ba[l], bx[l], lam[l], norm2_g[l], wa[l].reshape(-1), wx[l].reshape(-1)]
    parts.append(final_g)
    return jnp.concatenate(parts).reshape(REP_ROWS, LANES)


def _unpack_rep(buf):
    flat = buf.reshape(-1)
    out = {k: [] for k in ("norm1_g", "lru_conv_b", "lru_ba", "lru_bx", "lru_lambda", "norm2_g", "lru_wa", "lru_wx")}
    for l in range(2):
        o = l * REP_LAYER
        for i, k in enumerate(("norm1_g", "lru_conv_b", "lru_ba", "lru_bx", "lru_lambda", "norm2_g")):
            out[k].append(flat[o + i * 1024:o + (i + 1) * 1024])
        o += 6 * 1024
        out["lru_wa"].append(flat[o:o + 65536].reshape(16, 64, 64))
        out["lru_wx"].append(flat[o + 65536:o + 131072].reshape(16, 64, 64))
    res = {k: jnp.stack(v) for k, v in out.items()}
    res["final_g"] = flat[2 * REP_LAYER:2 * REP_LAYER + 1024]
    return res


def _pack_conv_shard(lru_cw, sc_cw, ffn_cw):
    return jnp.concatenate([lru_cw.reshape(16, LANES), jnp.pad(sc_cw.reshape(6, LANES), ((0, 2), (0, 0))),
                            ffn_cw.reshape(72, LANES)], axis=0)


def _unpack_conv_shard(buf):
    return (buf[0:16].reshape(2, 4, 256), buf[16:22].reshape(2, 3, 128), buf[24:96].reshape(2, 3, 1536))


def kernel(x, norm1_g, w_in, lru_conv_w, lru_conv_b, lru_wa, lru_ba, lru_wx, lru_bx, lru_lambda, sc_conv_w, w_out, norm2_g, w_up, ffn_conv_w, w_down, final_g, loss_target, m_norm1_g, m_w_in, m_lru_conv_w, m_lru_conv_b, m_lru_wa, m_lru_ba, m_lru_wx, m_lru_bx, m_lru_lambda, m_sc_conv_w, m_w_out, m_norm2_g, m_w_up, m_ffn_conv_w, m_w_down, m_final_g, v_norm1_g, v_w_in, v_lru_conv_w, v_lru_conv_b, v_lru_wa, v_lru_ba, v_lru_wx, v_lru_bx, v_lru_lambda, v_sc_conv_w, v_w_out, v_norm2_g, v_w_up, v_ffn_conv_w, v_w_down, v_final_g):
    me = 2 * lax.axis_index("x") + lax.axis_index("y")
    idx = jnp.stack([lax.axis_index("c"), me]).astype(jnp.int32)

    s_conv = _pack_conv_shard(lru_conv_w, sc_conv_w, ffn_conv_w)
    conv_slots = lax.dynamic_update_slice(jnp.zeros((N_CHIP, CONV_PACK_ROWS, LANES), F32), s_conv[None], (me, 0, 0))
    wi, wo, wu, wd, convs = _all_gather_weights(
        _cast_into_full(w_in, True, idx, "cast_w_in"), _cast_into_full(w_out, False, idx, "cast_w_out"),
        _cast_into_full(w_up, True, idx, "cast_w_up"), _cast_into_full(w_down, False, idx, "cast_w_down"),
        conv_slots)
    per_chip = [_unpack_conv_shard(convs[k]) for k in range(N_CHIP)]
    lru_cw = jnp.concatenate([p[0] for p in per_chip], axis=-1)
    sc_cw = jnp.concatenate([p[1] for p in per_chip], axis=-1)
    ffn_cw = jnp.concatenate([p[2] for p in per_chip], axis=-1)

    cw8 = [_rows8(lru_cw[l]) for l in range(2)]
    vec8 = [_rows8(jnp.stack([lru_conv_b[l], lru_ba[l], lru_bx[l], lru_lambda[l]])) for l in range(2)]
    wa_bd = [_block_diag(lru_wa[l]).astype(BF16) for l in range(2)]
    wx_bd = [_block_diag(lru_wx[l]).astype(BF16) for l in range(2)]
    scw8 = [_rows8(sc_cw[l]) for l in range(2)]
    fw8 = [jnp.pad(ffn_cw[l].reshape(3, 2, D_FF).transpose(1, 0, 2), ((0, 0), (0, 5), (0, 0))) for l in range(2)]

    xs = x[0]
    saved = []
    for l in range(2):
        z, h1 = _norm_mm(xs, norm1_g[l][None], wi, l, f"fwd_in_{l}")
        ymix, hst = _mixer_fwd(z, cw8[l], vec8[l], wa_bd[l], wx_bd[l], scw8[l], f"fwd_mixer_{l}")
        x2 = _mm_res(ymix, wo, l, xs, f"fwd_out_{l}")
        u, h2 = _norm_mm(x2, norm2_g[l][None], wu, l, f"fwd_up_{l}", planes=True)
        act = _ffn_act(u, fw8[l], f"fwd_act_{l}")
        x3 = _mm_res(act, wd, l, x2, f"fwd_down_{l}")
        saved.append((xs, h1, z, hst, ymix, x2, h2, u, act))
        xs = x3

    dx, dxb, dgf, loss_blk = _loss_head(xs, final_g[None], loss_target[0], "loss_head")
    loss = lax.psum(loss_blk[0, 0], ("x", "y", "c"))

    g_in = g_out = g_up = g_down = None
    small = [None, None]
    for l in (1, 0):
        x_in, h1, z, hst, ymix, x2, h2, u, act = saved[l]
        g_down = _mm_tn(act, dxb, l, g_down, f"bwd_wdown_{l}", tk=1536, tn=1024)
        dact = _mm_nt(dxb, wd, l, f"bwd_dact_{l}")
        du, dfw = _ffn_bwd(dact, u, fw8[l], f"bwd_act_{l}")
        g_up = _mm_tn(h2, du, l, g_up, f"bwd_wup_{l}", tk=1024, tn=1536, planes=True)
        dx2, dx2b, dg2 = _mm_nt_normbwd(du, wu, l, x2, norm2_g[l][None], dx, f"bwd_up_{l}", planes=True)
        g_out = _mm_tn(ymix, dx2b, l, g_out, f"bwd_wout_{l}", tk=1536, tn=1024)
        dymix = _mm_nt(dx2b, wo, l, f"bwd_dymix_{l}")
        dz, dcw, dvec, dwa, dwx, dscw = _mixer_bwd(z, hst, dymix, cw8[l], vec8[l], wa_bd[l], wx_bd[l], scw8[l],
                                                  f"bwd_mixer_{l}")
        g_in = _mm_tn(h1, dz, l, g_in, f"bwd_win_{l}", tk=1024, tn=1792)
        dx, dxb, dg1 = _mm_nt_normbwd(dz, wi, l, x_in, norm1_g[l][None], dx2, f"bwd_in_{l}")
        rep = [dg1[0], dvec[0], dvec[1], dvec[2], dvec[3], dg2[0],
               _block_diag_extract(dwa).reshape(-1), _block_diag_extract(dwx).reshape(-1)]
        conv = [dcw[0:4].reshape(-1), jnp.pad(dscw[0:3].reshape(-1), (0, 512)),
                dfw[:, 0:3, :].transpose(1, 0, 2).reshape(-1)]
        small[l] = (jnp.concatenate(rep), jnp.concatenate(conv))
    grad_x = dx[None]
    g_small = jnp.concatenate([small[0][0], small[1][0], dgf[0], small[0][1], small[1][1],
                               jnp.zeros((8 * LANES,), F32)]).reshape(SMALL_ROWS, LANES)

    grads = [g_in, g_out, g_up, g_down, g_small]
    lands = _pair_exchange(grads)
    summed = [_pair_sum(grads[w], lands[w], idx, w, f"rs_add_{w}") for w in range(5)]
    slots = _scatter_to_owners([s[0] for s in summed], [s[1] for s in summed])
    gw_in, gw_out, gw_up, gw_down, gs = _pair_share(
        [_sum_slots(slots[w], idx, w, f"rs_sum_{w}") for w in range(5)])

    g_rep = gs[0:REP_ROWS]
    g_conv = gs[REP_ROWS:REP_ROWS + CONV_ROWS].reshape(2, CONV_LAYER)
    g_lru_cw = lax.dynamic_slice_in_dim(g_conv[:, 0:4096].reshape(2, 4, 1024), me * 256, 256, axis=2)
    g_sc_cw = lax.dynamic_slice_in_dim(g_conv[:, 4096:4096 + 1536].reshape(2, 3, 512), me * 128, 128, axis=2)
    g_ffn_cw = lax.dynamic_slice_in_dim(g_conv[:, 6144:].reshape(2, 3, 6144), me * 1536, 1536, axis=2)

    def big(w, g, m, v, name):
        shape = w.shape
        two_d = lambda a: a.reshape(-1, shape[-1])
        return [o.reshape(shape) for o in _adamw(two_d(w), two_d(g), two_d(m), two_d(v), name)]

    upd = {"w_in": big(w_in, gw_in, m_w_in, v_w_in, "adamw_w_in"),
           "w_out": big(w_out, gw_out, m_w_out, v_w_out, "adamw_w_out"),
           "w_up": big(w_up, gw_up, m_w_up, v_w_up, "adamw_w_up"),
           "w_down": big(w_down, gw_down, m_w_down, v_w_down, "adamw_w_down")}
    rep_out = _adamw(
        _pack_rep(norm1_g, lru_conv_b, lru_ba, lru_bx, lru_lambda, norm2_g, lru_wa, lru_wx, final_g), g_rep,
        _pack_rep(m_norm1_g, m_lru_conv_b, m_lru_ba, m_lru_bx, m_lru_lambda, m_norm2_g, m_lru_wa, m_lru_wx, m_final_g),
        _pack_rep(v_norm1_g, v_lru_conv_b, v_lru_ba, v_lru_bx, v_lru_lambda, v_norm2_g, v_lru_wa, v_lru_wx, v_final_g),
        "adamw_rep")
    conv_out = _adamw(s_conv, _pack_conv_shard(g_lru_cw, g_sc_cw, g_ffn_cw),
                      _pack_conv_shard(m_lru_conv_w, m_sc_conv_w, m_ffn_conv_w),
                      _pack_conv_shard(v_lru_conv_w, v_sc_conv_w, v_ffn_conv_w), "adamw_conv")

    names = ["norm1_g", "w_in", "lru_conv_w", "lru_conv_b", "lru_wa", "lru_ba", "lru_wx", "lru_bx", "lru_lambda",
             "sc_conv_w", "w_out", "norm2_g", "w_up", "ffn_conv_w", "w_down", "final_g"]
    groups = []
    g_all = dict(_unpack_rep(g_rep))
    g_all.update(w_in=gw_in, w_out=gw_out, w_up=gw_up, w_down=gw_down,
                 lru_conv_w=g_lru_cw, sc_conv_w=g_sc_cw, ffn_conv_w=g_ffn_cw)
    groups.append(g_all)
    for i in range(3):
        d = dict(_unpack_rep(rep_out[i]))
        cl, cs, cf = _unpack_conv_shard(conv_out[i])
        d.update(lru_conv_w=cl, sc_conv_w=cs, ffn_conv_w=cf)
        d.update({k: v[i] for k, v in upd.items()})
        groups.append(d)
    return (loss, grad_x, *[grp[n] for grp in groups for n in names])
```

```python
import dataclasses
import functools
import math
import operator
from typing import Any, Callable, Optional, Sequence

import jax
import jax.numpy as jnp
from jax import lax
from jax.experimental import pallas as pl
from jax.experimental.pallas import tpu as pltpu

F32 = jnp.float32
BF16 = jnp.bfloat16
MESH = pl.DeviceIdType.MESH

D_MODEL = 1024
D_LRU = 1024
D_SC = 512
D_MIX = D_LRU + D_SC
D_IN = 2 * D_LRU + 3 * D_SC
D_FF = 3072
N_CHIP = 4
RG_C = 8.0
EPS = 1e-6
ADAM_LR = 0.001
ADAM_B1 = 0.9
ADAM_B2 = 0.999
ADAM_EPS = 1e-08
ADAM_WD = 0.01
ADAM_STEP = 10

SUBLANES = 8
PACKED = 16
LANES = 128
VMEM_LIMIT = 56 * 1024 * 1024
GELU_C0 = math.sqrt(2.0 / math.pi)
GELU_C1 = 0.044715

REP_LAYER = 6 * 1024 + 2 * 16 * 64 * 64
REP_ROWS = (2 * REP_LAYER + 1024) // LANES
CONV_LAYER = 4 * 1024 + 2048 + 3 * 6144
CONV_ROWS = 2 * CONV_LAYER // LANES
SMALL_ROWS = REP_ROWS + CONV_ROWS + 8
CONV_PACK_ROWS = 96

W_IN, W_OUT, W_UP, W_DOWN, SMALL = range(5)
COL_SHARDED = {W_IN: True, W_OUT: False, W_UP: True, W_DOWN: False}

ONCE = pl.Buffered(1)
ANY = pl.BlockSpec(memory_space=pl.ANY)


def _cp(*sem):
    return pltpu.CompilerParams(dimension_semantics=sem, vmem_limit_bytes=VMEM_LIMIT)


@dataclasses.dataclass
class Comm:
    srcs: Sequence[Any]
    bufs: Sequence[Any]
    outs: Sequence[Any]
    n_sem: int
    start: Callable
    finish: Callable
    mid: Optional[Callable] = None
    mid_at: Optional[Sequence[int]] = None


def _pallas(body, *, name, grid, in_specs, out_specs, out_shape, args, sem, scratch_shapes=(), comm=None):
    if comm is None:
        res = pl.pallas_call(
            body, name=name, grid=grid, in_specs=list(in_specs), out_specs=list(out_specs),
            out_shape=list(out_shape), scratch_shapes=list(scratch_shapes), compiler_params=_cp(*sem))(*args)
        return tuple(res), ()
    n_in, n_out, n_scr = len(in_specs), len(out_specs), len(scratch_shapes)
    ns, nb, no = len(comm.srcs), len(comm.bufs), len(comm.outs)

    def carrier(*refs):
        p = 0
        main_in = refs[p:p + n_in]
        p += n_in
        c_src = refs[p:p + ns]
        p += ns + nb
        main_out = refs[p:p + n_out]
        p += n_out
        c_buf = refs[p:p + nb]
        p += nb
        c_out = refs[p:p + no]
        p += no
        scr = refs[p:p + n_scr]
        send, recv = refs[p + n_scr], refs[p + n_scr + 1]
        ids = [pl.program_id(a) for a in range(len(grid))]

        def at(steps):
            return functools.reduce(operator.and_, [i == s for i, s in zip(ids, steps)])

        @pl.when(at([0] * len(grid)))
        def _():
            comm.start(c_src, c_buf, c_out, send, recv)

        if comm.mid is not None:
            @pl.when(at(comm.mid_at))
            def _():
                comm.mid(c_src, c_buf, c_out, send, recv)

        body(*main_in, *main_out, *scr)

        @pl.when(at([g - 1 for g in grid]))
        def _():
            comm.finish(c_src, c_buf, c_out, send, recv)

    res = pl.pallas_call(
        carrier, name=name, grid=grid,
        in_specs=list(in_specs) + [ANY] * (ns + nb),
        out_specs=list(out_specs) + [ANY] * (nb + no),
        out_shape=list(out_shape) + [jax.ShapeDtypeStruct(b.shape, b.dtype) for b in comm.bufs] + list(comm.outs),
        input_output_aliases={n_in + ns + j: n_out + j for j in range(nb)},
        scratch_shapes=list(scratch_shapes) + [pltpu.SemaphoreType.DMA((comm.n_sem,)),
                                               pltpu.SemaphoreType.DMA((comm.n_sem,))],
        compiler_params=_cp(*(["arbitrary"] * len(grid))),
    )(*args, *comm.srcs, *comm.bufs)
    return tuple(res[:n_out]), tuple(res[n_out:])


def _comm_call(comm, name):
    ns, nb, no = len(comm.srcs), len(comm.bufs), len(comm.outs)

    def body(*refs):
        c_src = refs[0:ns]
        c_buf = refs[ns + nb:ns + 2 * nb]
        c_out = refs[ns + 2 * nb:ns + 2 * nb + no]
        send, recv = refs[ns + 2 * nb + no], refs[ns + 2 * nb + no + 1]
        comm.start(c_src, c_buf, c_out, send, recv)
        if comm.mid is not None:
            comm.mid(c_src, c_buf, c_out, send, recv)
        comm.finish(c_src, c_buf, c_out, send, recv)

    return tuple(pl.pallas_call(
        body, name=name,
        in_specs=[ANY] * (ns + nb), out_specs=[ANY] * (nb + no),
        out_shape=[jax.ShapeDtypeStruct(b.shape, b.dtype) for b in comm.bufs] + list(comm.outs),
        input_output_aliases={ns + j: j for j in range(nb)},
        scratch_shapes=[pltpu.SemaphoreType.DMA((comm.n_sem,)), pltpu.SemaphoreType.DMA((comm.n_sem,))],
    )(*comm.srcs, *comm.bufs))


def _sigmoid(v):
    return 1.0 / (1.0 + jnp.exp(-v))


def _gelu_parts(v):
    v2 = v * v
    t = jnp.tanh(GELU_C0 * v * (1.0 + GELU_C1 * v2))
    half = 0.5 * (1.0 + t)
    gel = v * half
    dgel = half + 0.5 * v * (1.0 - t * t) * (GELU_C0 * (1.0 + 3.0 * GELU_C1 * v2))
    return gel, dgel


def _gelu(v):
    t = jnp.tanh(GELU_C0 * v * (1.0 + GELU_C1 * (v * v)))
    return 0.5 * v * (1.0 + t)


def _neg_expm1(y, a):
    p = jnp.full_like(y, 1.0 / 5040.0)
    for coef in (1.0 / 720.0, 1.0 / 120.0, 1.0 / 24.0, 1.0 / 6.0, 0.5, 1.0):
        p = p * y + coef
    return jnp.where(y > -0.3, -(p * y), 1.0 - a * a)


def _softplus_neg(lam):
    nl = -lam
    e = jnp.exp(-jnp.abs(nl))
    u = 1.0 + e
    l1p = jnp.where(u == 1.0, e, jnp.log(u) * e / (u - 1.0))
    return jnp.maximum(nl, 0.0) + l1p


def _conv_taps(ext, taps, n_out):
    kw = len(taps)
    acc = taps[kw - 1] * ext[SUBLANES:SUBLANES + n_out]
    for k in range(kw - 1):
        acc = acc + taps[k] * pltpu.roll(ext, kw - 1 - k, axis=0)[SUBLANES:SUBLANES + n_out]
    return acc


def _conv_taps_t(ext, taps, n_out):
    kw = len(taps)
    n = ext.shape[0]
    acc = taps[kw - 1] * ext[0:n_out]
    for k in range(kw - 1):
        acc = acc + taps[k] * pltpu.roll(ext, n - (kw - 1 - k), axis=0)[0:n_out]
    return acc


def _scan8(a, b, carry, row):
    for s in (1, 2, 4):
        m = row >= s
        a_sh = jnp.where(m, pltpu.roll(a, s, axis=0), 1.0)
        b_sh = jnp.where(m, pltpu.roll(b, s, axis=0), 0.0)
        b = a * b_sh + b
        a = a * a_sh
    return a * carry + b


def _scan8_rev(a, b, carry, row):
    for s in (1, 2, 4):
        m = row < SUBLANES - s
        a_sh = jnp.where(m, pltpu.roll(a, SUBLANES - s, axis=0), 1.0)
        b_sh = jnp.where(m, pltpu.roll(b, SUBLANES - s, axis=0), 0.0)
        b = a * b_sh + b
        a = a * a_sh
    return a * carry + b


def _cast_into_full(w, kind, idx, name):
    nl, r, c = w.shape
    tr = 256 if r % 256 == 0 else r
    nrb = r // tr

    def body(idx_ref, w_ref, o0_ref, o1_ref):
        o0_ref[...] = w_ref[0].astype(BF16)
        o1_ref[...] = w_ref[1].astype(BF16)

    if COL_SHARDED[kind]:
        full = (r, N_CHIP * c)
        o_spec = pl.BlockSpec((tr, c), lambda i, idx_ref: (i, idx_ref[1]))
    else:
        full = (N_CHIP * r, c)
        o_spec = pl.BlockSpec((tr, c), lambda i, idx_ref: (idx_ref[1] * nrb + i, 0))
    return pl.pallas_call(
        body, name=name,
        grid_spec=pltpu.PrefetchScalarGridSpec(
            num_scalar_prefetch=1, grid=(nrb,),
            in_specs=[pl.BlockSpec((nl, tr, c), lambda i, idx_ref: (0, i, 0))], out_specs=[o_spec, o_spec]),
        out_shape=[jax.ShapeDtypeStruct(full, BF16)] * 2,
        compiler_params=_cp("parallel"),
    )(idx, w)


def _norm_mm(x, g, w, name, planes=False, tm=512, tn=512, comm=None):
    t_len, d = x.shape
    n = w.shape[1]
    half = n // 2

    def body(x_ref, g_ref, w_ref, z_ref, h_ref):
        xv = x_ref[...]
        r = lax.rsqrt(jnp.mean(xv * xv, axis=-1, keepdims=True) + EPS)
        h_ref[...] = ((xv * r) * g_ref[...]).astype(BF16)
        for n0 in range(0, n, tn):
            blk = jnp.dot(h_ref[...], w_ref[:, n0:n0 + tn], preferred_element_type=F32).astype(BF16)
            if planes:
                z_ref[n0 // half, :, n0 % half:n0 % half + tn] = blk
            else:
                z_ref[:, n0:n0 + tn] = blk

    if planes:
        z_shape = jax.ShapeDtypeStruct((2, t_len, half), BF16)
        z_spec = pl.BlockSpec((2, tm, half), lambda i: (0, i, 0))
    else:
        z_shape = jax.ShapeDtypeStruct((t_len, n), BF16)
        z_spec = pl.BlockSpec((tm, n), lambda i: (i, 0))
    return _pallas(
        body, name=name, grid=(t_len // tm,),
        in_specs=[pl.BlockSpec((tm, d), lambda i: (i, 0)),
                  pl.BlockSpec((1, d), lambda i: (0, 0)),
                  pl.BlockSpec((d, n), lambda i: (0, 0), pipeline_mode=ONCE)],
        out_specs=[z_spec, pl.BlockSpec((tm, d), lambda i: (i, 0))],
        out_shape=[z_shape, jax.ShapeDtypeStruct((t_len, d), BF16)],
        args=(x, g, w), sem=("parallel",), comm=comm)


def _mm_res(a, w, res, name, tm=512, comm=None):
    t_len, k = a.shape
    n = w.shape[1]

    def body(a_ref, w_ref, r_ref, o_ref):
        o_ref[...] = r_ref[...] + jnp.dot(a_ref[...], w_ref[...], preferred_element_type=F32)

    return _pallas(
        body, name=name, grid=(t_len // tm,),
        in_specs=[pl.BlockSpec((tm, k), lambda i: (i, 0)),
                  pl.BlockSpec((k, n), lambda i: (0, 0), pipeline_mode=ONCE),
                  pl.BlockSpec((tm, n), lambda i: (i, 0))],
        out_specs=[pl.BlockSpec((tm, n), lambda i: (i, 0))],
        out_shape=[jax.ShapeDtypeStruct((t_len, n), F32)],
        args=(a, w, res), sem=("parallel",), comm=comm)


def _mm_nt(a, w, name, tm=512, comm=None):
    t_len, k = a.shape
    n = w.shape[0]

    def body(a_ref, w_ref, o_ref):
        o_ref[...] = lax.dot_general(a_ref[...], w_ref[...], (((1,), (1,)), ((), ())),
                                     preferred_element_type=F32).astype(BF16)

    return _pallas(
        body, name=name, grid=(t_len // tm,),
        in_specs=[pl.BlockSpec((tm, k), lambda i: (i, 0)),
                  pl.BlockSpec((n, k), lambda i: (0, 0), pipeline_mode=ONCE)],
        out_specs=[pl.BlockSpec((tm, n), lambda i: (i, 0))],
        out_shape=[jax.ShapeDtypeStruct((t_len, n), BF16)],
        args=(a, w), sem=("parallel",), comm=comm)


def _mm_nt_normbwd(dz, w, x, g, dres, name, planes=False, tm=512, comm=None):
    t_len, d = x.shape
    n = w.shape[1]
    half = n // 2
    nt_dims = (((1,), (1,)), ((), ()))

    def body(dz_ref, w_ref, x_ref, g_ref, r_ref, dx_ref, dxb_ref, dg_ref):
        @pl.when(pl.program_id(0) == 0)
        def _():
            dg_ref[...] = jnp.zeros_like(dg_ref)

        if planes:
            dh = (lax.dot_general(dz_ref[0], w_ref[:, 0:half], nt_dims, preferred_element_type=F32)
                  + lax.dot_general(dz_ref[1], w_ref[:, half:], nt_dims, preferred_element_type=F32))
        else:
            dh = lax.dot_general(dz_ref[...], w_ref[...], nt_dims, preferred_element_type=F32)
        xv = x_ref[...]
        r = lax.rsqrt(jnp.mean(xv * xv, axis=-1, keepdims=True) + EPS)
        xh = xv * r
        dhg = dh * g_ref[...]
        dx = r_ref[...] + r * (dhg - xh * jnp.mean(dhg * xh, axis=-1, keepdims=True))
        dx_ref[...] = dx
        dxb_ref[...] = dx.astype(BF16)
        dg_ref[0:1, :] += jnp.sum(dh * xh, axis=0, keepdims=True)

    if planes:
        dz_spec = pl.BlockSpec((2, tm, half), lambda i: (0, i, 0))
    else:
        dz_spec = pl.BlockSpec((tm, n), lambda i: (i, 0))
    return _pallas(
        body, name=name, grid=(t_len // tm,),
        in_specs=[dz_spec,
                  pl.BlockSpec((d, n), lambda i: (0, 0), pipeline_mode=ONCE),
                  pl.BlockSpec((tm, d), lambda i: (i, 0)),
                  pl.BlockSpec((1, d), lambda i: (0, 0)),
                  pl.BlockSpec((tm, d), lambda i: (i, 0))],
        out_specs=[pl.BlockSpec((tm, d), lambda i: (i, 0)),
                   pl.BlockSpec((tm, d), lambda i: (i, 0)),
                   pl.BlockSpec((SUBLANES, d), lambda i: (0, 0))],
        out_shape=[jax.ShapeDtypeStruct((t_len, d), F32),
                   jax.ShapeDtypeStruct((t_len, d), BF16),
                   jax.ShapeDtypeStruct((SUBLANES, d), F32)],
        args=(dz, w, x, g, dres), sem=("arbitrary",), comm=comm)


def _mm_tn(a, g, name, tk, tn, planes=False, tt=1024, comm=None):
    t_len, k = a.shape
    n = 2 * g.shape[2] if planes else g.shape[1]
    nn = n // tn
    half = nn // 2
    tt = min(tt, t_len)

    def body(a_ref, g_ref, o_ref):
        @pl.when(pl.program_id(2) == 0)
        def _():
            o_ref[...] = jnp.zeros_like(o_ref)

        o_ref[...] += lax.dot_general(a_ref[...], g_ref[...], (((0,), (0,)), ((), ())),
                                      preferred_element_type=F32)

    if planes:
        g_spec = pl.BlockSpec((None, tt, tn), lambda i, j, t: (j // half, t, j % half))
    else:
        g_spec = pl.BlockSpec((tt, tn), lambda i, j, t: (t, j))
    return _pallas(
        body, name=name, grid=(k // tk, nn, t_len // tt),
        in_specs=[pl.BlockSpec((tt, tk), lambda i, j, t: (t, i)), g_spec],
        out_specs=[pl.BlockSpec((tk, tn), lambda i, j, t: (i, j))],
        out_shape=[jax.ShapeDtypeStruct((k, n), F32)],
        args=(a, g), sem=("parallel", "parallel", "arbitrary"), comm=comm)


def _lru_gates(rp, ip, spn):
    r = _sigmoid(rp)
    i = _sigmoid(ip)
    la = r * spn
    a = jnp.exp(la)
    mult = jnp.sqrt(_neg_expm1(2.0 * la, a))
    return r, i, a, mult


def _mixer_fwd(z, cw8, vec8, wa_bd, wx_bd, scw8, name, tb=256, comm=None):
    t_len = z.shape[0]

    def body(z_ref, cw_ref, vec_ref, wa_ref, wx_ref, scw_ref, y_ref, h_ref,
             xhalo, phalo, hcar, lx_s, rp_s, ip_s):
        @pl.when(pl.program_id(0) == 0)
        def _():
            xhalo[...] = jnp.zeros_like(xhalo)
            phalo[...] = jnp.zeros_like(phalo)
            hcar[...] = jnp.zeros_like(hcar)

        cw = cw_ref[...]
        vec = vec_ref[...]
        xp = z_ref[:, 0:D_LRU].astype(F32)
        ext = jnp.concatenate([xhalo[...], xp], axis=0)
        lx = vec[0:1] + _conv_taps(ext, [cw[k:k + 1] for k in range(4)], tb)
        xhalo[...] = xp[tb - SUBLANES:]
        lx_s[...] = lx
        lxb = lx.astype(BF16)
        for q in range(4):
            sl = slice(q * 256, (q + 1) * 256)
            rp_s[:, sl] = jnp.dot(lxb[:, sl], wa_ref[q], preferred_element_type=F32) + vec[1:2, sl]
            ip_s[:, sl] = jnp.dot(lxb[:, sl], wx_ref[q], preferred_element_type=F32) + vec[2:3, sl]

        spn = jnp.broadcast_to(-RG_C * _softplus_neg(vec[3:4]), (SUBLANES, D_LRU))
        row = lax.broadcasted_iota(jnp.int32, (SUBLANES, D_LRU), 0)

        def step(ci, carry):
            o = pl.multiple_of(ci * PACKED, PACKED)
            gate = z_ref[pl.ds(o, PACKED), D_LRU:2 * D_LRU].astype(F32)
            ys = []
            for sub in range(2):
                rows = pl.ds(pl.multiple_of(o + sub * SUBLANES, SUBLANES), SUBLANES)
                lxv = lx_s[rows, :]
                _, i, a, mult = _lru_gates(rp_s[rows, :], ip_s[rows, :], spn)
                h = _scan8(a, mult * (i * lxv), carry, row)
                h_ref[rows, :] = h
                ys.append(h * _gelu(gate[sub * SUBLANES:(sub + 1) * SUBLANES]))
                carry = jnp.broadcast_to(h[SUBLANES - 1:SUBLANES, :], (SUBLANES, D_LRU))
            y_ref[pl.ds(o, PACKED), 0:D_LRU] = jnp.concatenate(ys, axis=0).astype(BF16)
            return carry

        hcar[...] = lax.fori_loop(0, tb // PACKED, step, hcar[...])

        scw = scw_ref[...]
        o_b, o_c, o_x = 2 * D_LRU, 2 * D_LRU + D_SC, 2 * D_LRU + 2 * D_SC
        p = z_ref[:, o_c:o_x].astype(F32) * z_ref[:, o_x:].astype(F32)
        pext = jnp.concatenate([phalo[...], p], axis=0)
        q = _conv_taps(pext, [scw[k:k + 1] for k in range(3)], tb)
        phalo[...] = p[tb - SUBLANES:]
        y_ref[:, D_LRU:] = (z_ref[:, o_b:o_c].astype(F32) * q).astype(BF16)

    const = lambda t: (0, 0)
    return _pallas(
        body, name=name, grid=(t_len // tb,),
        in_specs=[pl.BlockSpec((tb, D_IN), lambda t: (t, 0)),
                  pl.BlockSpec((SUBLANES, D_LRU), const),
                  pl.BlockSpec((SUBLANES, D_LRU), const),
                  pl.BlockSpec((4, 256, 256), lambda t: (0, 0, 0)),
                  pl.BlockSpec((4, 256, 256), lambda t: (0, 0, 0)),
                  pl.BlockSpec((SUBLANES, D_SC), const)],
        out_specs=[pl.BlockSpec((tb, D_MIX), lambda t: (t, 0)),
                   pl.BlockSpec((tb, D_LRU), lambda t: (t, 0))],
        out_shape=[jax.ShapeDtypeStruct((t_len, D_MIX), BF16),
                   jax.ShapeDtypeStruct((t_len, D_LRU), F32)],
        scratch_shapes=[pltpu.VMEM((SUBLANES, D_LRU), F32), pltpu.VMEM((SUBLANES, D_SC), F32),
                        pltpu.VMEM((SUBLANES, D_LRU), F32), pltpu.VMEM((tb, D_LRU), F32),
                        pltpu.VMEM((tb, D_LRU), F32), pltpu.VMEM((tb, D_LRU), F32)],
        args=(z, cw8, vec8, wa_bd, wx_bd, scw8), sem=("arbitrary",), comm=comm)


def _mixer_bwd(z, h, dy, cw8, vec8, wa_bd, wx_bd, scw8, name, tb=128, comm=None):
    t_len = z.shape[0]
    nb = t_len // tb

    def body(z_ref, zh_ref, h_ref, hh_ref, dy_ref, cw_ref, vec_ref, wa_ref, wx_ref, scw_ref,
             dz_ref, dcw_ref, dvec_ref, dwa_ref, dwx_ref, dscw_ref,
             lx_s, rp_s, ip_s, drpb_s, dipb_s, dlx_s, hext_s, acc_s, acar, gcar, dqh):
        t = pl.program_id(0)
        first_block = t == nb - 1

        @pl.when(t == 0)
        def _():
            for ref in (dcw_ref, dvec_ref, dwa_ref, dwx_ref, dscw_ref, acc_s, acar, gcar, dqh):
                ref[...] = jnp.zeros_like(ref)
            dlx_s[tb:, :] = jnp.zeros((SUBLANES, D_LRU), F32)

        cw = cw_ref[...]
        vec = vec_ref[...]
        scw = scw_ref[...]
        ctaps = [cw[k:k + 1] for k in range(4)]
        staps = [scw[k:k + 1] for k in range(3)]
        keep = jnp.where(first_block, 0.0, 1.0)
        zh = zh_ref[...].astype(F32)[PACKED - SUBLANES:] * keep

        xp = z_ref[:, 0:D_LRU].astype(F32)
        xext = jnp.concatenate([zh[:, 0:D_LRU], xp], axis=0)
        lx = vec[0:1] + _conv_taps(xext, ctaps, tb)
        lx_s[...] = lx
        lxb = lx.astype(BF16)
        for q in range(4):
            sl = slice(q * 256, (q + 1) * 256)
            rp_s[:, sl] = jnp.dot(lxb[:, sl], wa_ref[q], preferred_element_type=F32) + vec[1:2, sl]
            ip_s[:, sl] = jnp.dot(lxb[:, sl], wx_ref[q], preferred_element_type=F32) + vec[2:3, sl]
        hext_s[0:SUBLANES, :] = hh_ref[...] * keep
        hext_s[SUBLANES:, :] = h_ref[...]

        spn = jnp.broadcast_to(-RG_C * _softplus_neg(vec[3:4]), (SUBLANES, D_LRU))
        row = lax.broadcasted_iota(jnp.int32, (SUBLANES, D_LRU), 0)

        def step(ci, carry):
            a_next, g_next = carry
            o = pl.multiple_of((tb // PACKED - 1 - ci) * PACKED, PACKED)
            rows16 = pl.ds(o, PACKED)
            gate16 = z_ref[rows16, D_LRU:2 * D_LRU].astype(F32)
            dyl16 = dy_ref[rows16, 0:D_LRU].astype(F32)
            dgs, drs, dis = [None, None], [None, None], [None, None]
            for sub in (1, 0):
                oo = pl.multiple_of(o + sub * SUBLANES, SUBLANES)
                rows = pl.ds(oo, SUBLANES)
                half = slice(sub * SUBLANES, (sub + 1) * SUBLANES)
                lxv = lx_s[rows, :]
                r, i, a, mult = _lru_gates(rp_s[rows, :], ip_s[rows, :], spn)
                hwin = hext_s[pl.ds(oo, 2 * SUBLANES), :]
                hv = hwin[SUBLANES:]
                hprev = pltpu.roll(hwin, 1, axis=0)[SUBLANES:]
                gel, dgel = _gelu_parts(gate16[half])
                dyl = dyl16[half]
                a_up = jnp.where(row < SUBLANES - 1, pltpu.roll(a, SUBLANES - 1, axis=0), a_next)
                gg = _scan8_rev(a_up, dyl * gel, g_next, row)
                dgs[sub] = dyl * hv * dgel
                ilx = i * lxv
                dla = gg * hprev * a - (gg * ilx) * (a * a) / mult
                dlx_s[rows, :] = gg * mult * i
                drp = dla * spn * r * (1.0 - r)
                dip = gg * mult * lxv * i * (1.0 - i)
                drs[sub] = drp
                dis[sub] = dip
                acc_s[0] += drp
                acc_s[1] += dip
                acc_s[2] += dla * r
                a_next = jnp.broadcast_to(a[0:1, :], (SUBLANES, D_LRU))
                g_next = jnp.broadcast_to(gg[0:1, :], (SUBLANES, D_LRU))
            dz_ref[rows16, D_LRU:2 * D_LRU] = jnp.concatenate(dgs, axis=0).astype(BF16)
            drpb_s[rows16, :] = jnp.concatenate(drs, axis=0).astype(BF16)
            dipb_s[rows16, :] = jnp.concatenate(dis, axis=0).astype(BF16)
            return a_next, g_next

        a_c, g_c = lax.fori_loop(0, tb // PACKED, step, (acar[...], gcar[...]))
        acar[...] = a_c
        gcar[...] = g_c

        drpb = drpb_s[...]
        dipb = dipb_s[...]
        nt_dims = (((1,), (1,)), ((), ()))
        tn_dims = (((0,), (0,)), ((), ()))
        for q in range(4):
            sl = slice(q * 256, (q + 1) * 256)
            dlx_s[0:tb, sl] += (
                lax.dot_general(drpb[:, sl], wa_ref[q], nt_dims, preferred_element_type=F32)
                + lax.dot_general(dipb[:, sl], wx_ref[q], nt_dims, preferred_element_type=F32))
            dwa_ref[q] += lax.dot_general(lxb[:, sl], drpb[:, sl], tn_dims, preferred_element_type=F32)
            dwx_ref[q] += lax.dot_general(lxb[:, sl], dipb[:, sl], tn_dims, preferred_element_type=F32)

        dlx_ext = dlx_s[...]
        dlx = dlx_ext[0:tb]
        dz_ref[:, 0:D_LRU] = _conv_taps_t(dlx_ext, ctaps, tb).astype(BF16)
        dcw_ref[3:4, :] += jnp.sum(dlx * xp, axis=0, keepdims=True)
        for k in range(3):
            shifted = pltpu.roll(xext, 3 - k, axis=0)[SUBLANES:]
            dcw_ref[k:k + 1, :] += jnp.sum(dlx * shifted, axis=0, keepdims=True)
        dvec_ref[0:1, :] += jnp.sum(dlx, axis=0, keepdims=True)
        dlx_s[tb:, :] = dlx[0:SUBLANES]

        o_b, o_c, o_x = 2 * D_LRU, 2 * D_LRU + D_SC, 2 * D_LRU + 2 * D_SC
        sb = z_ref[:, o_b:o_c].astype(F32)
        scc = z_ref[:, o_c:o_x].astype(F32)
        sx = z_ref[:, o_x:].astype(F32)
        p = scc * sx
        pext = jnp.concatenate([zh[:, o_c:o_x] * zh[:, o_x:], p], axis=0)
        q = _conv_taps(pext, staps, tb)
        dys = dy_ref[:, D_LRU:].astype(F32)
        dq = dys * sb
        dp = _conv_taps_t(jnp.concatenate([dq, dqh[...]], axis=0), staps, tb)
        dscw_ref[2:3, :] += jnp.sum(dq * p, axis=0, keepdims=True)
        for k in range(2):
            shifted = pltpu.roll(pext, 2 - k, axis=0)[SUBLANES:]
            dscw_ref[k:k + 1, :] += jnp.sum(dq * shifted, axis=0, keepdims=True)
        dqh[...] = dq[0:SUBLANES]
        dz_ref[:, o_b:o_c] = (dys * q).astype(BF16)
        dz_ref[:, o_c:o_x] = (dp * sx).astype(BF16)
        dz_ref[:, o_x:] = (dp * scc).astype(BF16)

        @pl.when(first_block)
        def _():
            dvec_ref[1:2, :] = jnp.sum(acc_s[0], axis=0, keepdims=True)
            dvec_ref[2:3, :] = jnp.sum(acc_s[1], axis=0, keepdims=True)
            dvec_ref[3:4, :] = (jnp.sum(acc_s[2], axis=0, keepdims=True) * RG_C * _sigmoid(-vec[3:4]))

    blk = lambda t: (nb - 1 - t, 0)
    halo8 = lambda t: (jnp.maximum((nb - 1 - t) * (tb // SUBLANES) - 1, 0), 0)
    halo16 = lambda t: (jnp.maximum((nb - 1 - t) * (tb // PACKED) - 1, 0), 0)
    const = lambda t: (0, 0)
    const3 = lambda t: (0, 0, 0)
    return _pallas(
        body, name=name, grid=(nb,),
        in_specs=[pl.BlockSpec((tb, D_IN), blk), pl.BlockSpec((PACKED, D_IN), halo16),
                  pl.BlockSpec((tb, D_LRU), blk), pl.BlockSpec((SUBLANES, D_LRU), halo8),
                  pl.BlockSpec((tb, D_MIX), blk),
                  pl.BlockSpec((SUBLANES, D_LRU), const), pl.BlockSpec((SUBLANES, D_LRU), const),
                  pl.BlockSpec((4, 256, 256), const3), pl.BlockSpec((4, 256, 256), const3),
                  pl.BlockSpec((SUBLANES, D_SC), const)],
        out_specs=[pl.BlockSpec((tb, D_IN), blk),
                   pl.BlockSpec((SUBLANES, D_LRU), const), pl.BlockSpec((SUBLANES, D_LRU), const),
                   pl.BlockSpec((4, 256, 256), const3), pl.BlockSpec((4, 256, 256), const3),
                   pl.BlockSpec((SUBLANES, D_SC), const)],
        out_shape=[jax.ShapeDtypeStruct((t_len, D_IN), BF16),
                   jax.ShapeDtypeStruct((SUBLANES, D_LRU), F32), jax.ShapeDtypeStruct((SUBLANES, D_LRU), F32),
                   jax.ShapeDtypeStruct((4, 256, 256), F32), jax.ShapeDtypeStruct((4, 256, 256), F32),
                   jax.ShapeDtypeStruct((SUBLANES, D_SC), F32)],
        scratch_shapes=[pltpu.VMEM((tb, D_LRU), F32),
                        pltpu.VMEM((tb, D_LRU), F32), pltpu.VMEM((tb, D_LRU), F32),
                        pltpu.VMEM((tb, D_LRU), BF16), pltpu.VMEM((tb, D_LRU), BF16),
                        pltpu.VMEM((tb + SUBLANES, D_LRU), F32), pltpu.VMEM((tb + SUBLANES, D_LRU), F32),
                        pltpu.VMEM((3, SUBLANES, D_LRU), F32),
                        pltpu.VMEM((SUBLANES, D_LRU), F32), pltpu.VMEM((SUBLANES, D_LRU), F32),
                        pltpu.VMEM((SUBLANES, D_SC), F32)],
        args=(z, z, h, h, dy, cw8, vec8, wa_bd, wx_bd, scw8), sem=("arbitrary",), comm=comm)


def _ffn_act(u, fw, name, tb=256, tn=512, rc=16, comm=None):
    t_len = u.shape[1]
    hb = tb // PACKED

    def body(u_ref, uh_ref, fw_ref, o_ref, ext):
        keep = jnp.where(pl.program_id(0) == 0, 0.0, 1.0)
        ext[:, 0:SUBLANES, :] = uh_ref[...].astype(F32)[:, PACKED - SUBLANES:, :] * keep
        ext[:, SUBLANES:, :] = u_ref[...].astype(F32)
        fw_v = fw_ref[...]
        wg = [fw_v[0, k:k + 1, :] for k in range(3)]
        wu = [fw_v[1, k:k + 1, :] for k in range(3)]

        def chunk(ci, c):
            o = pl.multiple_of(ci * rc, rc)
            win = pl.ds(o, rc + SUBLANES)
            gate = _conv_taps(ext[0, win, :], wg, rc)
            up = _conv_taps(ext[1, win, :], wu, rc)
            o_ref[pl.ds(o, rc), :] = (_gelu(gate) * up).astype(BF16)
            return c

        lax.fori_loop(0, tb // rc, chunk, 0)

    return _pallas(
        body, name=name, grid=(t_len // tb, D_FF // tn),
        in_specs=[pl.BlockSpec((2, tb, tn), lambda i, j: (0, i, j)),
                  pl.BlockSpec((2, PACKED, tn), lambda i, j: (0, jnp.maximum(i * hb - 1, 0), j)),
                  pl.BlockSpec((2, SUBLANES, tn), lambda i, j: (0, 0, j))],
        out_specs=[pl.BlockSpec((tb, tn), lambda i, j: (i, j))],
        out_shape=[jax.ShapeDtypeStruct((t_len, D_FF), BF16)],
        scratch_shapes=[pltpu.VMEM((2, tb + SUBLANES, tn), F32)],
        args=(u, u, fw), sem=("parallel", "parallel"), comm=comm)


def _ffn_bwd(dact, u, fw, name, tb=256, tn=512, rc=16, comm=None):
    t_len = u.shape[1]
    ni = t_len // tb
    hb = tb // PACKED
    last_halo = t_len // PACKED - 1

    def body(d_ref, dn_ref, u_ref, up_ref, un_ref, fw_ref, du_ref, dfw_ref, extu, extd, acc):
        i = pl.program_id(1)

        @pl.when(i == 0)
        def _():
            acc[...] = jnp.zeros_like(acc)

        keep_prev = jnp.where(i == 0, 0.0, 1.0)
        keep_next = jnp.where(i == ni - 1, 0.0, 1.0)
        extu[:, 0:SUBLANES, :] = up_ref[...].astype(F32)[:, PACKED - SUBLANES:, :] * keep_prev
        extu[:, SUBLANES:SUBLANES + tb, :] = u_ref[...].astype(F32)
        extu[:, SUBLANES + tb:, :] = un_ref[...].astype(F32)[:, 0:SUBLANES, :]
        extd[0:tb, :] = d_ref[...].astype(F32)
        extd[tb:, :] = dn_ref[...].astype(F32)[0:SUBLANES] * keep_next
        fw_v = fw_ref[...]
        taps = [[fw_v[pln, k:k + 1, :] for k in range(3)] for pln in range(2)]
        m = rc + SUBLANES

        def chunk(ci, c):
            o = pl.multiple_of(ci * rc, rc)
            win = pl.ds(o, rc + 2 * SUBLANES)
            sh = []
            for pln in range(2):
                e = extu[pln, win, :]
                sh.append([pltpu.roll(e, 2, axis=0)[SUBLANES:], pltpu.roll(e, 1, axis=0)[SUBLANES:], e[SUBLANES:]])
            gate = sum(taps[0][k] * sh[0][k] for k in range(3))
            up = sum(taps[1][k] * sh[1][k] for k in range(3))
            dv = extd[pl.ds(o, m), :]
            gel, dgel = _gelu_parts(gate)
            dpost = [dv * up * dgel, dv * gel]
            for pln in range(2):
                du_ref[pln, pl.ds(o, rc), :] = _conv_taps_t(dpost[pln], taps[pln], rc).astype(BF16)
                for k in range(3):
                    prod = dpost[pln][0:rc] * sh[pln][k][0:rc]
                    acc[3 * pln + k] += prod[0:SUBLANES] + prod[SUBLANES:]
            return c

        lax.fori_loop(0, tb // rc, chunk, 0)

        @pl.when(i == ni - 1)
        def _():
            dfw_ref[...] = jnp.zeros_like(dfw_ref)
            for pln in range(2):
                for k in range(3):
                    dfw_ref[pln, k:k + 1, :] = jnp.sum(acc[3 * pln + k], axis=0, keepdims=True)

    return _pallas(
        body, name=name, grid=(D_FF // tn, ni),
        in_specs=[pl.BlockSpec((tb, tn), lambda j, i: (i, j)),
                  pl.BlockSpec((PACKED, tn), lambda j, i: (jnp.minimum((i + 1) * hb, last_halo), j)),
                  pl.BlockSpec((2, tb, tn), lambda j, i: (0, i, j)),
                  pl.BlockSpec((2, PACKED, tn), lambda j, i: (0, jnp.maximum(i * hb - 1, 0), j)),
                  pl.BlockSpec((2, PACKED, tn), lambda j, i: (0, jnp.minimum((i + 1) * hb, last_halo), j)),
                  pl.BlockSpec((2, SUBLANES, tn), lambda j, i: (0, 0, j))],
        out_specs=[pl.BlockSpec((2, tb, tn), lambda j, i: (0, i, j)),
                   pl.BlockSpec((2, SUBLANES, tn), lambda j, i: (0, 0, j))],
        out_shape=[jax.ShapeDtypeStruct((2, t_len, D_FF), BF16),
                   jax.ShapeDtypeStruct((2, SUBLANES, D_FF), F32)],
        scratch_shapes=[pltpu.VMEM((2, tb + 2 * SUBLANES, tn), F32),
                        pltpu.VMEM((tb + SUBLANES, tn), F32),
                        pltpu.VMEM((6, SUBLANES, tn), F32)],
        args=(dact, dact, u, u, u, fw), sem=("parallel", "arbitrary"), comm=comm)


def _loss_head(x, g, target, name, tb=256):
    t_len, d = x.shape

    def body(x_ref, g_ref, t_ref, dx_ref, dxb_ref, dg_ref, loss_ref):
        @pl.when(pl.program_id(0) == 0)
        def _():
            dg_ref[...] = jnp.zeros_like(dg_ref)
            loss_ref[...] = jnp.zeros_like(loss_ref)

        xv = x_ref[...]
        gv = g_ref[...]
        r = lax.rsqrt(jnp.mean(xv * xv, axis=-1, keepdims=True) + EPS)
        xh = xv * r
        err = xh * gv - t_ref[...]
        loss_ref[...] += (0.5 / d) * jnp.sum(jnp.sum(err * err, axis=-1, keepdims=True), axis=0, keepdims=True)
        dy = err * (1.0 / d)
        dyg = dy * gv
        dx = r * (dyg - xh * jnp.mean(dyg * xh, axis=-1, keepdims=True))
        dx_ref[...] = dx
        dxb_ref[...] = dx.astype(BF16)
        dg_ref[0:1, :] += jnp.sum(dy * xh, axis=0, keepdims=True)

    return _pallas(
        body, name=name, grid=(t_len // tb,),
        in_specs=[pl.BlockSpec((tb, d), lambda i: (i, 0)), pl.BlockSpec((1, d), lambda i: (0, 0)),
                  pl.BlockSpec((tb, d), lambda i: (i, 0))],
        out_specs=[pl.BlockSpec((tb, d), lambda i: (i, 0)), pl.BlockSpec((tb, d), lambda i: (i, 0)),
                   pl.BlockSpec((SUBLANES, d), lambda i: (0, 0)),
                   pl.BlockSpec((SUBLANES, LANES), lambda i: (0, 0))],
        out_shape=[jax.ShapeDtypeStruct((t_len, d), F32), jax.ShapeDtypeStruct((t_len, d), BF16),
                   jax.ShapeDtypeStruct((SUBLANES, d), F32), jax.ShapeDtypeStruct((SUBLANES, LANES), F32)],
        args=(x, g, target), sem=("arbitrary",))[0]


def _adamw(w, g, m, v, name):
    r, c = w.shape
    tr = 256 if r % 256 == 0 else r
    c1 = 1.0 / (1.0 - ADAM_B1 ** ADAM_STEP)
    c2 = 1.0 / (1.0 - ADAM_B2 ** ADAM_STEP)

    def body(w_ref, g_ref, m_ref, v_ref, d_ref, mo_ref, vo_ref):
        gv = g_ref[...]
        mn = ADAM_B1 * m_ref[...] + (1.0 - ADAM_B1) * gv
        vn = ADAM_B2 * v_ref[...] + (1.0 - ADAM_B2) * (gv * gv)
        d_ref[...] = -ADAM_LR * ((mn * c1) / (jnp.sqrt(vn * c2) + ADAM_EPS) + ADAM_WD * w_ref[...])
        mo_ref[...] = mn
        vo_ref[...] = vn

    spec = pl.BlockSpec((tr, c), lambda i: (i, 0))
    shape = jax.ShapeDtypeStruct((r, c), F32)
    return _pallas(
        body, name=name, grid=(r // tr,),
        in_specs=[spec] * 4, out_specs=[spec] * 3, out_shape=[shape] * 3,
        args=(w, g, m, v), sem=("parallel",))[0]


def _place():
    x, y, c = lax.axis_index("x"), lax.axis_index("y"), lax.axis_index("c")
    chips = [(1 - x, y), (x, 1 - y), (1 - x, 1 - y)]
    return x, y, c, chips


def _remote(src, dst, send, recv, sem, to):
    return pltpu.make_async_remote_copy(
        src_ref=src, dst_ref=dst, send_sem=send.at[sem], recv_sem=recv.at[sem], device_id=to, device_id_type=MESH)


def _gather_plan(fulls, kinds, mid_at=None):
    def region(f, kind, k, cc):
        if kind == SMALL:
            return f.at[k, pl.ds(cc * (CONV_PACK_ROWS // 2), CONV_PACK_ROWS // 2), :]
        if COL_SHARDED[kind]:
            rows, cols = f.shape[0], f.shape[1] // N_CHIP
            return f.at[pl.ds(cc * (rows // 2), rows // 2), pl.ds(k * cols, cols)]
        rows = f.shape[0] // N_CHIP
        return f.at[pl.ds(k * rows + cc * (rows // 2), rows // 2), :]

    def first_hop(bufs, send, recv, it, j):
        x, y, c, chips = _place()
        reg = region(bufs[it], kinds[it], 2 * x + y, c)
        return _remote(reg, reg, send, recv, it * 6 + j, (*chips[j], c))

    def arrival(bufs, send, recv, it, j, second):
        x, y, c, chips = _place()
        px, py = chips[j]
        reg = region(bufs[it], kinds[it], 2 * px + py, 1 - c if second else c)
        to = (x, y, 1 - c) if second else (px, py, c)
        return _remote(reg, reg, send, recv, it * 6 + (3 + j if second else j), to)

    def forward(bufs, send, recv, it, j):
        x, y, c, chips = _place()
        px, py = chips[j]
        reg = region(bufs[it], kinds[it], 2 * px + py, c)
        return _remote(reg, reg, send, recv, it * 6 + 3 + j, (x, y, 1 - c))

    def start(srcs, bufs, outs, send, recv):
        for it in range(len(bufs)):
            for j in range(3):
                first_hop(bufs, send, recv, it, j).start()

    def mid(srcs, bufs, outs, send, recv):
        for it in range(len(bufs)):
            for j in range(3):
                arrival(bufs, send, recv, it, j, False).wait_recv()
                forward(bufs, send, recv, it, j).start()

    def finish(srcs, bufs, outs, send, recv):
        for it in range(len(bufs)):
            for j in range(3):
                arrival(bufs, send, recv, it, j, True).wait_recv()
        for it in range(len(bufs)):
            for j in range(3):
                first_hop(bufs, send, recv, it, j).wait_send()
                forward(bufs, send, recv, it, j).wait_send()

    return Comm(srcs=(), bufs=tuple(fulls), outs=(), n_sem=6 * len(fulls), start=start, mid=mid, finish=finish,
                mid_at=mid_at)


def _half_axis(kind):
    return 0 if kind == SMALL or COL_SHARDED[kind] else 1


def _half2(ref, kind, cc):
    if _half_axis(kind) == 0:
        return ref.at[pl.ds(cc * (ref.shape[0] // 2), ref.shape[0] // 2), :]
    return ref.at[:, pl.ds(cc * (ref.shape[1] // 2), ref.shape[1] // 2)]


def _pair_plan(grads, kinds):
    def land_shape(g, kind):
        s = list(g.shape)
        s[_half_axis(kind)] //= 2
        return jax.ShapeDtypeStruct(tuple(s), F32)

    def copy(srcs, outs, send, recv, it):
        x, y, c, _ = _place()
        return _remote(_half2(srcs[it], kinds[it], 1 - c), outs[it], send, recv, it, (x, y, 1 - c))

    def start(srcs, bufs, outs, send, recv):
        for it in range(len(srcs)):
            copy(srcs, outs, send, recv, it).start()

    def finish(srcs, bufs, outs, send, recv):
        for it in range(len(srcs)):
            copy(srcs, outs, send, recv, it).wait_send()
        for it in range(len(srcs)):
            copy(srcs, outs, send, recv, it).wait_recv()

    return Comm(srcs=tuple(grads), bufs=(), outs=tuple(land_shape(g, k) for g, k in zip(grads, kinds)),
                n_sem=len(grads), start=start, finish=finish)


def _scatter_plan(parts, slots, kinds):
    def piece(s, kind, k):
        if kind == SMALL:
            return s
        if COL_SHARDED[kind]:
            n = s.shape[1] // N_CHIP
            return s.at[:, pl.ds(k * n, n)]
        n = s.shape[0] // N_CHIP
        return s.at[pl.ds(k * n, n), :]

    def outbound(srcs, bufs, send, recv, it, j):
        x, y, c, chips = _place()
        px, py = chips[j]
        return _remote(piece(srcs[it], kinds[it], 2 * px + py), bufs[it].at[2 * x + y], send, recv, it * 3 + j,
                       (px, py, c))

    def inbound(bufs, send, recv, it, j):
        x, y, c, chips = _place()
        px, py = chips[j]
        got = bufs[it].at[2 * px + py]
        return _remote(got, got, send, recv, it * 3 + j, (px, py, c))

    def start(srcs, bufs, outs, send, recv):
        for it in range(len(srcs)):
            for j in range(3):
                outbound(srcs, bufs, send, recv, it, j).start()

    def finish(srcs, bufs, outs, send, recv):
        for it in range(len(srcs)):
            for j in range(3):
                inbound(bufs, send, recv, it, j).wait_recv()
        for it in range(len(srcs)):
            for j in range(3):
                outbound(srcs, bufs, send, recv, it, j).wait_send()

    return Comm(srcs=tuple(parts), bufs=tuple(slots), outs=(), n_sem=3 * len(parts), start=start, finish=finish)


def _share_plan(fulls, kinds, layer):
    def half(f, kind, cc):
        return _half2(f if kind == SMALL else f.at[layer], kind, cc)

    def copy(bufs, send, recv, it, cc):
        x, y, c, _ = _place()
        reg = half(bufs[it], kinds[it], c if cc == "mine" else 1 - c)
        return _remote(reg, reg, send, recv, it, (x, y, 1 - c))

    def start(srcs, bufs, outs, send, recv):
        for it in range(len(bufs)):
            copy(bufs, send, recv, it, "mine").start()

    def finish(srcs, bufs, outs, send, recv):
        for it in range(len(bufs)):
            copy(bufs, send, recv, it, "other").wait_recv()
        for it in range(len(bufs)):
            copy(bufs, send, recv, it, "mine").wait_send()

    return Comm(srcs=(), bufs=tuple(fulls), outs=(), n_sem=len(fulls), start=start, finish=finish)


def _pair_sum(g, land, idx, kind, name):
    odt = F32 if kind == SMALL else BF16
    r, cdim = land.shape

    def body(idx_ref, g_ref, l_ref, p_ref, s_ref):
        v = (g_ref[...] + l_ref[...]).astype(odt)
        p_ref[...] = v
        if kind == SMALL:
            s_ref[...] = v
        else:
            @pl.when(pl.program_id(1 if COL_SHARDED[kind] else 0) == idx_ref[1])
            def _():
                s_ref[...] = v

    if kind == SMALL:
        grid = (1,)
        g_spec = pl.BlockSpec((r, LANES), lambda i, idx_ref: (idx_ref[0], 0))
        spec = pl.BlockSpec((r, LANES), lambda i, idx_ref: (0, 0))
        s_spec = pl.BlockSpec((None, r, LANES), lambda i, idx_ref: (idx_ref[1], 0, 0))
        s_shape = (N_CHIP, r, LANES)
    elif COL_SHARDED[kind]:
        pc, tr = cdim // N_CHIP, 256
        nrb = r // tr
        grid = (nrb, N_CHIP)
        g_spec = pl.BlockSpec((tr, pc), lambda i, k, idx_ref: (idx_ref[0] * nrb + i, k))
        spec = pl.BlockSpec((tr, pc), lambda i, k, idx_ref: (i, k))
        s_spec = pl.BlockSpec((None, tr, pc), lambda i, k, idx_ref: (idx_ref[1], i, 0))
        s_shape = (N_CHIP, r, pc)
    else:
        pr = r // N_CHIP
        grid = (N_CHIP,)
        g_spec = pl.BlockSpec((pr, cdim), lambda k, idx_ref: (k, idx_ref[0]))
        spec = pl.BlockSpec((pr, cdim), lambda k, idx_ref: (k, 0))
        s_spec = pl.BlockSpec((None, pr, cdim), lambda k, idx_ref: (idx_ref[1], 0, 0))
        s_shape = (N_CHIP, pr, cdim)
    return pl.pallas_call(
        body, name=name,
        grid_spec=pltpu.PrefetchScalarGridSpec(
            num_scalar_prefetch=1, grid=grid, in_specs=[g_spec, spec], out_specs=[spec, s_spec]),
        out_shape=[jax.ShapeDtypeStruct(land.shape, odt), jax.ShapeDtypeStruct(s_shape, odt)],
        compiler_params=_cp(*(["arbitrary"] * len(grid))),
    )(idx, g, land)


def _sum_slots(slots, idx, kind, layer, prev, name):
    _, r, cdim = slots.shape

    def body(*refs):
        s_ref, o_ref = refs[1], refs[-1]
        v = s_ref[...].astype(F32)
        o_ref[...] = (v[0] + v[1]) + (v[2] + v[3])

    if kind == SMALL:
        grid = (1,)
        s_spec = pl.BlockSpec((N_CHIP, r, cdim), lambda i, idx_ref: (0, 0, 0))
        o_spec = pl.BlockSpec((r, cdim), lambda i, idx_ref: (idx_ref[0], 0))
        full = (2 * r, cdim)
    else:
        tr = 256 if r % 256 == 0 else 384
        nrb = r // tr
        grid = (nrb,)
        s_spec = pl.BlockSpec((N_CHIP, tr, cdim), lambda i, idx_ref: (0, i, 0))
        if COL_SHARDED[kind]:
            o_spec = pl.BlockSpec((None, tr, cdim), lambda i, idx_ref: (layer, idx_ref[0] * nrb + i, 0))
            full = (2, 2 * r, cdim)
        else:
            o_spec = pl.BlockSpec((None, tr, cdim), lambda i, idx_ref: (layer, i, idx_ref[0]))
            full = (2, r, 2 * cdim)
    in_specs, args, aliases = [s_spec], [idx, slots], {}
    if prev is not None:
        in_specs.append(ANY)
        args.append(prev)
        aliases = {2: 0}
    return pl.pallas_call(
        body, name=name,
        grid_spec=pltpu.PrefetchScalarGridSpec(
            num_scalar_prefetch=1, grid=grid, in_specs=in_specs, out_specs=o_spec),
        out_shape=jax.ShapeDtypeStruct(full, F32),
        input_output_aliases=aliases,
        compiler_params=_cp(*(["parallel"] * len(grid))),
    )(*args)


def _block_diag(w):
    w4 = w.reshape(4, 4, 64, 64)
    eye = jnp.eye(4, dtype=w.dtype)[None, :, None, :, None]
    return (w4[:, :, :, None, :] * eye).reshape(4, 256, 256)


def _block_diag_extract(d):
    d5 = d.reshape(4, 4, 64, 4, 64)
    return jnp.stack([d5[:, hh, :, hh, :] for hh in range(4)], axis=1).reshape(16, 64, 64)


def _rows8(a):
    return jnp.pad(a, ((0, SUBLANES - a.shape[0]), (0, 0)))


def _pack_rep(norm1_g, conv_b, ba, bx, lam, norm2_g, wa, wx, final_g):
    parts = []
    for l in range(2):
        parts += [norm1_g[l], conv_b[l], ba[l], bx[l], lam[l], norm2_g[l], wa[l].reshape(-1), wx[l].reshape(-1)]
    parts.append(final_g)
    return jnp.concatenate(parts).reshape(REP_ROWS, LANES)


def _unpack_rep(buf):
    flat = buf.reshape(-1)
    out = {k: [] for k in ("norm1_g", "lru_conv_b", "lru_ba", "lru_bx", "lru_lambda", "norm2_g", "lru_wa", "lru_wx")}
    for l in range(2):
        o = l * REP_LAYER
        for i, k in enumerate(("norm1_g", "lru_conv_b", "lru_ba", "lru_bx", "lru_lambda", "norm2_g")):
            out[k].append(flat[o + i * 1024:o + (i + 1) * 1024])
        o += 6 * 1024
        out["lru_wa"].append(flat[o:o + 65536].reshape(16, 64, 64))
        out["lru_wx"].append(flat[o + 65536:o + 131072].reshape(16, 64, 64))
    res = {k: jnp.stack(v) for k, v in out.items()}
    res["final_g"] = flat[2 * REP_LAYER:2 * REP_LAYER + 1024]
    return res


def _pack_conv_shard(lru_cw, sc_cw, ffn_cw):
    return jnp.concatenate([lru_cw.reshape(16, LANES), jnp.pad(sc_cw.reshape(6, LANES), ((0, 2), (0, 0))),
                            ffn_cw.reshape(72, LANES)], axis=0)


def _unpack_conv_shard(buf):
    return (buf[0:16].reshape(2, 4, 256), buf[16:22].reshape(2, 3, 128), buf[24:96].reshape(2, 3, 1536))


def kernel(x, norm1_g, w_in, lru_conv_w, lru_conv_b, lru_wa, lru_ba, lru_wx, lru_bx, lru_lambda, sc_conv_w, w_out, norm2_g, w_up, ffn_conv_w, w_down, final_g, loss_target, m_norm1_g, m_w_in, m_lru_conv_w, m_lru_conv_b, m_lru_wa, m_lru_ba, m_lru_wx, m_lru_bx, m_lru_lambda, m_sc_conv_w, m_w_out, m_norm2_g, m_w_up, m_ffn_conv_w, m_w_down, m_final_g, v_norm1_g, v_w_in, v_lru_conv_w, v_lru_conv_b, v_lru_wa, v_lru_ba, v_lru_wx, v_lru_bx, v_lru_lambda, v_sc_conv_w, v_w_out, v_norm2_g, v_w_up, v_ffn_conv_w, v_w_down, v_final_g):
    me = 2 * lax.axis_index("x") + lax.axis_index("y")
    idx = jnp.stack([lax.axis_index("c"), me]).astype(jnp.int32)
    t_len = x.shape[1]

    s_conv = _pack_conv_shard(lru_conv_w, sc_conv_w, ffn_conv_w)
    conv_slots = lax.dynamic_update_slice(jnp.zeros((N_CHIP, CONV_PACK_ROWS, LANES), F32), s_conv[None], (me, 0, 0))
    wi = list(_cast_into_full(w_in, W_IN, idx, "cast_w_in"))
    wo = list(_cast_into_full(w_out, W_OUT, idx, "cast_w_out"))
    wu = list(_cast_into_full(w_up, W_UP, idx, "cast_w_up"))
    wd = list(_cast_into_full(w_down, W_DOWN, idx, "cast_w_down"))
    wi[0], convs = _comm_call(_gather_plan([wi[0], conv_slots], [W_IN, SMALL]), "ag_first")
    per_chip = [_unpack_conv_shard(convs[k]) for k in range(N_CHIP)]
    lru_cw = jnp.concatenate([p[0] for p in per_chip], axis=-1)
    sc_cw = jnp.concatenate([p[1] for p in per_chip], axis=-1)
    ffn_cw = jnp.concatenate([p[2] for p in per_chip], axis=-1)

    cw8 = [_rows8(lru_cw[l]) for l in range(2)]
    vec8 = [_rows8(jnp.stack([lru_conv_b[l], lru_ba[l], lru_bx[l], lru_lambda[l]])) for l in range(2)]
    wa_bd = [_block_diag(lru_wa[l]).astype(BF16) for l in range(2)]
    wx_bd = [_block_diag(lru_wx[l]).astype(BF16) for l in range(2)]
    scw8 = [_rows8(sc_cw[l]) for l in range(2)]
    fw8 = [jnp.pad(ffn_cw[l].reshape(3, 2, D_FF).transpose(1, 0, 2), ((0, 0), (0, 5), (0, 0))) for l in range(2)]

    xs = x[0]
    saved = []
    for l in range(2):
        first = l == 0
        n512, n256 = t_len // 512, t_len // 256
        comm = _gather_plan([wo[0], wd[0]], [W_OUT, W_DOWN], mid_at=(max(n512 - 2, 0),)) if first else None
        (z, h1), got = _norm_mm(xs, norm1_g[l][None], wi[l], f"fwd_in_{l}", comm=comm)
        if first:
            wo[0], wd[0] = got
        comm = _gather_plan([wu[0]], [W_UP], mid_at=(max(n256 - 2, 0),)) if first else None
        (ymix, hst), got = _mixer_fwd(z, cw8[l], vec8[l], wa_bd[l], wx_bd[l], scw8[l], f"fwd_mixer_{l}", comm=comm)
        if first:
            wu[0], = got
        (x2,), _ = _mm_res(ymix, wo[l], xs, f"fwd_out_{l}")
        comm = _gather_plan([wi[1], wo[1]], [W_IN, W_OUT], mid_at=(n512 - 1,)) if first else None
        (u, h2), got = _norm_mm(x2, norm2_g[l][None], wu[l], f"fwd_up_{l}", planes=True, comm=comm)
        if first:
            wi[1], wo[1] = got
        comm = _gather_plan([wu[1], wd[1]], [W_UP, W_DOWN], mid_at=(n256 - 1, 0)) if first else None
        (act,), got = _ffn_act(u, fw8[l], f"fwd_act_{l}", comm=comm)
        if first:
            wu[1], wd[1] = got
        (x3,), _ = _mm_res(act, wd[l], x2, f"fwd_down_{l}")
        saved.append((xs, h1, z, hst, ymix, x2, h2, u, act))
        xs = x3

    dx, dxb, dgf, loss_blk = _loss_head(xs, final_g[None], loss_target[0], "loss_head")
    loss = lax.psum(loss_blk[0, 0], ("x", "y", "c"))

    kinds = [W_IN, W_OUT, W_UP, W_DOWN]
    grads = [None, None]
    small = [None, None]
    reduced = [None] * 4
    lands = parts = slots = None
    for l in (1, 0):
        x_in, h1, z, hst, ymix, x2, h2, u, act = saved[l]
        carry = l == 0
        comm = _pair_plan(grads[1], kinds) if carry else None
        (g_down,), got = _mm_tn(act, dxb, f"bwd_wdown_{l}", tk=1536, tn=1024, comm=comm)
        if carry:
            summed = [_pair_sum(grads[1][w], got[w], idx, kinds[w], f"rs_add1_{w}") for w in range(4)]
            parts, slots = [s[0] for s in summed], [s[1] for s in summed]
        (dact,), _ = _mm_nt(dxb, wd[l], f"bwd_dact_{l}")
        comm = _scatter_plan(parts, slots, kinds) if carry else None
        (du, dfw), got = _ffn_bwd(dact, u, fw8[l], f"bwd_act_{l}", comm=comm)
        if carry:
            reduced = [_sum_slots(got[w], idx, kinds[w], 1, None, f"rs_sum1_{w}") for w in range(4)]
        comm = _share_plan(reduced, kinds, 1) if carry else None
        (g_up,), got = _mm_tn(h2, du, f"bwd_wup_{l}", tk=1024, tn=1536, planes=True, comm=comm)
        if carry:
            reduced = list(got)
        dx2, dx2b, dg2 = _mm_nt_normbwd(du, wu[l], x2, norm2_g[l][None], dx, f"bwd_up_{l}", planes=True)[0]
        (g_out,), _ = _mm_tn(ymix, dx2b, f"bwd_wout_{l}", tk=1536, tn=1024)
        (dymix,), _ = _mm_nt(dx2b, wo[l], f"bwd_dymix_{l}")
        dz, dcw, dvec, dwa, dwx, dscw = _mixer_bwd(z, hst, dymix, cw8[l], vec8[l], wa_bd[l], wx_bd[l], scw8[l],
                                                  f"bwd_mixer_{l}")[0]
        (g_in,), _ = _mm_tn(h1, dz, f"bwd_win_{l}", tk=1024, tn=1792)
        dx, dxb, dg1 = _mm_nt_normbwd(dz, wi[l], x_in, norm1_g[l][None], dx2, f"bwd_in_{l}")[0]
        grads[l] = [g_in, g_out, g_up, g_down]
        rep = [dg1[0], dvec[0], dvec[1], dvec[2], dvec[3], dg2[0],
               _block_diag_extract(dwa).reshape(-1), _block_diag_extract(dwx).reshape(-1)]
        conv = [dcw[0:4].reshape(-1), jnp.pad(dscw[0:3].reshape(-1), (0, 512)),
                dfw[:, 0:3, :].transpose(1, 0, 2).reshape(-1)]
        small[l] = (jnp.concatenate(rep), jnp.concatenate(conv))
    grad_x = dx[None]
    g_small = jnp.concatenate([small[0][0], small[1][0], dgf[0], small[0][1], small[1][1],
                               jnp.zeros((8 * LANES,), F32)]).reshape(SMALL_ROWS, LANES)

    kinds5 = kinds + [SMALL]
    grads5 = grads[0] + [g_small]
    lands = _comm_call(_pair_plan(grads5, kinds5), "rs_pair0")
    summed = [_pair_sum(grads5[w], lands[w], idx, kinds5[w], f"rs_add0_{w}") for w in range(5)]
    slots = _comm_call(_scatter_plan([s[0] for s in summed], [s[1] for s in summed], kinds5), "rs_scatter0")
    reduced = [_sum_slots(slots[w], idx, kinds5[w], 0, reduced[w] if w < 4 else None, f"rs_sum0_{w}")
               for w in range(5)]
    gw_in, gw_out, gw_up, gw_down, gs = _comm_call(_share_plan(reduced, kinds5, 0), "rs_share0")

    g_rep = gs[0:REP_ROWS]
    g_conv = gs[REP_ROWS:REP_ROWS + CONV_ROWS].reshape(2, CONV_LAYER)
    g_lru_cw = lax.dynamic_slice_in_dim(g_conv[:, 0:4096].reshape(2, 4, 1024), me * 256, 256, axis=2)
    g_sc_cw = lax.dynamic_slice_in_dim(g_conv[:, 4096:4096 + 1536].reshape(2, 3, 512), me * 128, 128, axis=2)
    g_ffn_cw = lax.dynamic_slice_in_dim(g_conv[:, 6144:].reshape(2, 3, 6144), me * 1536, 1536, axis=2)

    def big(w, g, m, v, name):
        shape = w.shape
        two_d = lambda a: a.reshape(-1, shape[-1])
        return [o.reshape(shape) for o in _adamw(two_d(w), two_d(g), two_d(m), two_d(v), name)]

    upd = {"w_in": big(w_in, gw_in, m_w_in, v_w_in, "adamw_w_in"),
           "w_out": big(w_out, gw_out, m_w_out, v_w_out, "adamw_w_out"),
           "w_up": big(w_up, gw_up, m_w_up, v_w_up, "adamw_w_up"),
           "w_down": big(w_down, gw_down, m_w_down, v_w_down, "adamw_w_down")}
    rep_out = _adamw(
        _pack_rep(norm1_g, lru_conv_b, lru_ba, lru_bx, lru_lambda, norm2_g, lru_wa, lru_wx, final_g), g_rep,
        _pack_rep(m_norm1_g, m_lru_conv_b, m_lru_ba, m_lru_bx, m_lru_lambda, m_norm2_g, m_lru_wa, m_lru_wx, m_final_g),
        _pack_rep(v_norm1_g, v_lru_conv_b, v_lru_ba, v_lru_bx, v_lru_lambda, v_norm2_g, v_lru_wa, v_lru_wx, v_final_g),
        "adamw_rep")
    conv_out = _adamw(s_conv, _pack_conv_shard(g_lru_cw, g_sc_cw, g_ffn_cw),
                      _pack_conv_shard(m_lru_conv_w, m_sc_conv_w, m_ffn_conv_w),
                      _pack_conv_shard(v_lru_conv_w, v_sc_conv_w, v_ffn_conv_w), "adamw_conv")

    names = ["norm1_g", "w_in", "lru_conv_w", "lru_conv_b", "lru_wa", "lru_ba", "lru_wx", "lru_bx", "lru_lambda",
             "sc_conv_w", "w_out", "norm2_g", "w_up", "ffn_conv_w", "w_down", "final_g"]
    groups = []
    g_all = dict(_unpack_rep(g_rep))
    g_all.update(w_in=gw_in, w_out=gw_out, w_up=gw_up, w_down=gw_down,
                 lru_conv_w=g_lru_cw, sc_conv_w=g_sc_cw, ffn_conv_w=g_ffn_cw)
    groups.append(g_all)
    for i in range(3):
        d = dict(_unpack_rep(rep_out[i]))
        cl, cs, cf = _unpack_conv_shard(conv_out[i])
        d.update(lru_conv_w=cl, sc_conv_w=cs, ffn_conv_w=cf)
        d.update({k: v[i] for k, v in upd.items()})
        groups.append(d)
    return (loss, grad_x, *[grp[n] for grp in groups for n in names])
```

```python
import dataclasses
import functools
import math
import operator
from typing import Any, Callable, Optional, Sequence

import jax
import jax.numpy as jnp
from jax import lax
from jax.experimental import pallas as pl
from jax.experimental.pallas import tpu as pltpu

F32 = jnp.float32
BF16 = jnp.bfloat16
MESH = pl.DeviceIdType.MESH

D_MODEL = 1024
D_LRU = 1024
D_SC = 512
D_MIX = D_LRU + D_SC
D_IN = 2 * D_LRU + 3 * D_SC
D_FF = 3072
N_CHIP = 4
RG_C = 8.0
EPS = 1e-6
ADAM_LR = 0.001
ADAM_B1 = 0.9
ADAM_B2 = 0.999
ADAM_EPS = 1e-08
ADAM_WD = 0.01
ADAM_STEP = 10

SUBLANES = 8
PACKED = 16
LANES = 128
VMEM_LIMIT = 56 * 1024 * 1024
GELU_C0 = math.sqrt(2.0 / math.pi)
GELU_C1 = 0.044715

REP_LAYER = 6 * 1024 + 2 * 16 * 64 * 64
REP_ROWS = (2 * REP_LAYER + 1024) // LANES
CONV_LAYER = 4 * 1024 + 2048 + 3 * 6144
CONV_ROWS = 2 * CONV_LAYER // LANES
SMALL_ROWS = REP_ROWS + CONV_ROWS + 8
CONV_PACK_ROWS = 96

W_IN, W_OUT, W_UP, W_DOWN, SMALL = range(5)
COL_SHARDED = {W_IN: True, W_OUT: False, W_UP: True, W_DOWN: False}

ONCE = pl.Buffered(1)
ANY = pl.BlockSpec(memory_space=pl.ANY)


def _cp(*sem):
    return pltpu.CompilerParams(dimension_semantics=sem, vmem_limit_bytes=VMEM_LIMIT)


@dataclasses.dataclass
class Comm:
    srcs: Sequence[Any]
    bufs: Sequence[Any]
    outs: Sequence[Any]
    n_sem: int
    start: Callable
    finish: Callable
    mid: Optional[Callable] = None
    mid_at: Optional[Sequence[int]] = None


def _pallas(body, *, name, grid, in_specs, out_specs, out_shape, args, sem, scratch_shapes=(), comm=None):
    if comm is None:
        res = pl.pallas_call(
            body, name=name, grid=grid, in_specs=list(in_specs), out_specs=list(out_specs),
            out_shape=list(out_shape), scratch_shapes=list(scratch_shapes), compiler_params=_cp(*sem))(*args)
        return tuple(res), ()
    n_in, n_out, n_scr = len(in_specs), len(out_specs), len(scratch_shapes)
    ns, nb, no = len(comm.srcs), len(comm.bufs), len(comm.outs)

    def carrier(*refs):
        p = 0
        main_in = refs[p:p + n_in]
        p += n_in
        c_src = refs[p:p + ns]
        p += ns + nb
        main_out = refs[p:p + n_out]
        p += n_out
        c_buf = refs[p:p + nb]
        p += nb
        c_out = refs[p:p + no]
        p += no
        scr = refs[p:p + n_scr]
        send, recv = refs[p + n_scr], refs[p + n_scr + 1]
        ids = [pl.program_id(a) for a in range(len(grid))]

        def at(steps):
            return functools.reduce(operator.and_, [i == s for i, s in zip(ids, steps)])

        @pl.when(at([0] * len(grid)))
        def _():
            comm.start(c_src, c_buf, c_out, send, recv)

        if comm.mid is not None:
            @pl.when(at(comm.mid_at))
            def _():
                comm.mid(c_src, c_buf, c_out, send, recv)

        body(*main_in, *main_out, *scr)

        @pl.when(at([g - 1 for g in grid]))
        def _():
            comm.finish(c_src, c_buf, c_out, send, recv)

    res = pl.pallas_call(
        carrier, name=name, grid=grid,
        in_specs=list(in_specs) + [ANY] * (ns + nb),
        out_specs=list(out_specs) + [ANY] * (nb + no),
        out_shape=list(out_shape) + [jax.ShapeDtypeStruct(b.shape, b.dtype) for b in comm.bufs] + list(comm.outs),
        input_output_aliases={n_in + ns + j: n_out + j for j in range(nb)},
        scratch_shapes=list(scratch_shapes) + [pltpu.SemaphoreType.DMA((comm.n_sem,)),
                                               pltpu.SemaphoreType.DMA((comm.n_sem,))],
        compiler_params=_cp(*(["arbitrary"] * len(grid))),
    )(*args, *comm.srcs, *comm.bufs)
    return tuple(res[:n_out]), tuple(res[n_out:])


def _comm_call(comm, name):
    ns, nb, no = len(comm.srcs), len(comm.bufs), len(comm.outs)

    def body(*refs):
        c_src = refs[0:ns]
        c_buf = refs[ns + nb:ns + 2 * nb]
        c_out = refs[ns + 2 * nb:ns + 2 * nb + no]
        send, recv = refs[ns + 2 * nb + no], refs[ns + 2 * nb + no + 1]
        comm.start(c_src, c_buf, c_out, send, recv)
        if comm.mid is not None:
            comm.mid(c_src, c_buf, c_out, send, recv)
        comm.finish(c_src, c_buf, c_out, send, recv)

    return tuple(pl.pallas_call(
        body, name=name,
        in_specs=[ANY] * (ns + nb), out_specs=[ANY] * (nb + no),
        out_shape=[jax.ShapeDtypeStruct(b.shape, b.dtype) for b in comm.bufs] + list(comm.outs),
        input_output_aliases={ns + j: j for j in range(nb)},
        scratch_shapes=[pltpu.SemaphoreType.DMA((comm.n_sem,)), pltpu.SemaphoreType.DMA((comm.n_sem,))],
    )(*comm.srcs, *comm.bufs))


def _sigmoid(v):
    return 1.0 / (1.0 + jnp.exp(-v))


def _gelu_parts(v):
    v2 = v * v
    t = jnp.tanh(GELU_C0 * v * (1.0 + GELU_C1 * v2))
    half = 0.5 * (1.0 + t)
    gel = v * half
    dgel = half + 0.5 * v * (1.0 - t * t) * (GELU_C0 * (1.0 + 3.0 * GELU_C1 * v2))
    return gel, dgel


def _gelu(v):
    t = jnp.tanh(GELU_C0 * v * (1.0 + GELU_C1 * (v * v)))
    return 0.5 * v * (1.0 + t)


def _neg_expm1(y, a):
    p = jnp.full_like(y, 1.0 / 5040.0)
    for coef in (1.0 / 720.0, 1.0 / 120.0, 1.0 / 24.0, 1.0 / 6.0, 0.5, 1.0):
        p = p * y + coef
    return jnp.where(y > -0.3, -(p * y), 1.0 - a * a)


def _softplus_neg(lam):
    nl = -lam
    e = jnp.exp(-jnp.abs(nl))
    u = 1.0 + e
    l1p = jnp.where(u == 1.0, e, jnp.log(u) * e / (u - 1.0))
    return jnp.maximum(nl, 0.0) + l1p


def _conv_taps(ext, taps, n_out):
    kw = len(taps)
    acc = taps[kw - 1] * ext[SUBLANES:SUBLANES + n_out]
    for k in range(kw - 1):
        acc = acc + taps[k] * pltpu.roll(ext, kw - 1 - k, axis=0)[SUBLANES:SUBLANES + n_out]
    return acc


def _conv_taps_t(ext, taps, n_out):
    kw = len(taps)
    n = ext.shape[0]
    acc = taps[kw - 1] * ext[0:n_out]
    for k in range(kw - 1):
        acc = acc + taps[k] * pltpu.roll(ext, n - (kw - 1 - k), axis=0)[0:n_out]
    return acc


def _scan8(a, b, carry, row):
    for s in (1, 2, 4):
        m = row >= s
        a_sh = jnp.where(m, pltpu.roll(a, s, axis=0), 1.0)
        b_sh = jnp.where(m, pltpu.roll(b, s, axis=0), 0.0)
        b = a * b_sh + b
        a = a * a_sh
    return a * carry + b


def _scan8_rev(a, b, carry, row):
    for s in (1, 2, 4):
        m = row < SUBLANES - s
        a_sh = jnp.where(m, pltpu.roll(a, SUBLANES - s, axis=0), 1.0)
        b_sh = jnp.where(m, pltpu.roll(b, SUBLANES - s, axis=0), 0.0)
        b = a * b_sh + b
        a = a * a_sh
    return a * carry + b


def _cast_into_full(w, kind, idx, name):
    nl, r, c = w.shape
    tr = 256 if r % 256 == 0 else r
    nrb = r // tr

    def body(idx_ref, w_ref, o0_ref, o1_ref):
        o0_ref[...] = w_ref[0].astype(BF16)
        o1_ref[...] = w_ref[1].astype(BF16)

    if COL_SHARDED[kind]:
        full = (r, N_CHIP * c)
        o_spec = pl.BlockSpec((tr, c), lambda i, idx_ref: (i, idx_ref[1]))
    else:
        full = (N_CHIP * r, c)
        o_spec = pl.BlockSpec((tr, c), lambda i, idx_ref: (idx_ref[1] * nrb + i, 0))
    return pl.pallas_call(
        body, name=name,
        grid_spec=pltpu.PrefetchScalarGridSpec(
            num_scalar_prefetch=1, grid=(nrb,),
            in_specs=[pl.BlockSpec((nl, tr, c), lambda i, idx_ref: (0, i, 0))], out_specs=[o_spec, o_spec]),
        out_shape=[jax.ShapeDtypeStruct(full, BF16)] * 2,
        compiler_params=_cp("parallel"),
    )(idx, w)


def _norm_mm(x, g, w, name, planes=False, tm=512, tn=512, comm=None):
    t_len, d = x.shape
    n = w.shape[1]
    half = n // 2

    def body(x_ref, g_ref, w_ref, z_ref, h_ref):
        xv = x_ref[...]
        r = lax.rsqrt(jnp.mean(xv * xv, axis=-1, keepdims=True) + EPS)
        h_ref[...] = ((xv * r) * g_ref[...]).astype(BF16)
        for n0 in range(0, n, tn):
            blk = jnp.dot(h_ref[...], w_ref[:, n0:n0 + tn], preferred_element_type=F32).astype(BF16)
            if planes:
                z_ref[n0 // half, :, n0 % half:n0 % half + tn] = blk
            else:
                z_ref[:, n0:n0 + tn] = blk

    if planes:
        z_shape = jax.ShapeDtypeStruct((2, t_len, half), BF16)
        z_spec = pl.BlockSpec((2, tm, half), lambda i: (0, i, 0))
    else:
        z_shape = jax.ShapeDtypeStruct((t_len, n), BF16)
        z_spec = pl.BlockSpec((tm, n), lambda i: (i, 0))
    return _pallas(
        body, name=name, grid=(t_len // tm,),
        in_specs=[pl.BlockSpec((tm, d), lambda i: (i, 0)),
                  pl.BlockSpec((1, d), lambda i: (0, 0)),
                  pl.BlockSpec((d, n), lambda i: (0, 0), pipeline_mode=ONCE)],
        out_specs=[z_spec, pl.BlockSpec((tm, d), lambda i: (i, 0))],
        out_shape=[z_shape, jax.ShapeDtypeStruct((t_len, d), BF16)],
        args=(x, g, w), sem=("parallel",), comm=comm)


def _mm_res(a, w, res, name, tm=512, comm=None):
    t_len, k = a.shape
    n = w.shape[1]

    def body(a_ref, w_ref, r_ref, o_ref):
        o_ref[...] = r_ref[...] + jnp.dot(a_ref[...], w_ref[...], preferred_element_type=F32)

    return _pallas(
        body, name=name, grid=(t_len // tm,),
        in_specs=[pl.BlockSpec((tm, k), lambda i: (i, 0)),
                  pl.BlockSpec((k, n), lambda i: (0, 0), pipeline_mode=ONCE),
                  pl.BlockSpec((tm, n), lambda i: (i, 0))],
        out_specs=[pl.BlockSpec((tm, n), lambda i: (i, 0))],
        out_shape=[jax.ShapeDtypeStruct((t_len, n), F32)],
        args=(a, w, res), sem=("parallel",), comm=comm)


def _mm_nt(a, w, name, tm=512, comm=None):
    t_len, k = a.shape
    n = w.shape[0]

    def body(a_ref, w_ref, o_ref):
        o_ref[...] = lax.dot_general(a_ref[...], w_ref[...], (((1,), (1,)), ((), ())),
                                     preferred_element_type=F32).astype(BF16)

    return _pallas(
        body, name=name, grid=(t_len // tm,),
        in_specs=[pl.BlockSpec((tm, k), lambda i: (i, 0)),
                  pl.BlockSpec((n, k), lambda i: (0, 0), pipeline_mode=ONCE)],
        out_specs=[pl.BlockSpec((tm, n), lambda i: (i, 0))],
        out_shape=[jax.ShapeDtypeStruct((t_len, n), BF16)],
        args=(a, w), sem=("parallel",), comm=comm)


def _mm_nt_normbwd(dz, w, x, g, dres, name, planes=False, tm=512, comm=None):
    t_len, d = x.shape
    n = w.shape[1]
    half = n // 2
    nt_dims = (((1,), (1,)), ((), ()))

    def body(dz_ref, w_ref, x_ref, g_ref, r_ref, dx_ref, dxb_ref, dg_ref):
        @pl.when(pl.program_id(0) == 0)
        def _():
            dg_ref[...] = jnp.zeros_like(dg_ref)

        if planes:
            dh = (lax.dot_general(dz_ref[0], w_ref[:, 0:half], nt_dims, preferred_element_type=F32)
                  + lax.dot_general(dz_ref[1], w_ref[:, half:], nt_dims, preferred_element_type=F32))
        else:
            dh = lax.dot_general(dz_ref[...], w_ref[...], nt_dims, preferred_element_type=F32)
        xv = x_ref[...]
        r = lax.rsqrt(jnp.mean(xv * xv, axis=-1, keepdims=True) + EPS)
        xh = xv * r
        dhg = dh * g_ref[...]
        dx = r_ref[...] + r * (dhg - xh * jnp.mean(dhg * xh, axis=-1, keepdims=True))
        dx_ref[...] = dx
        dxb_ref[...] = dx.astype(BF16)
        dg_ref[0:1, :] += jnp.sum(dh * xh, axis=0, keepdims=True)

    if planes:
        dz_spec = pl.BlockSpec((2, tm, half), lambda i: (0, i, 0))
    else:
        dz_spec = pl.BlockSpec((tm, n), lambda i: (i, 0))
    return _pallas(
        body, name=name, grid=(t_len // tm,),
        in_specs=[dz_spec,
                  pl.BlockSpec((d, n), lambda i: (0, 0), pipeline_mode=ONCE),
                  pl.BlockSpec((tm, d), lambda i: (i, 0)),
                  pl.BlockSpec((1, d), lambda i: (0, 0)),
                  pl.BlockSpec((tm, d), lambda i: (i, 0))],
        out_specs=[pl.BlockSpec((tm, d), lambda i: (i, 0)),
                   pl.BlockSpec((tm, d), lambda i: (i, 0)),
                   pl.BlockSpec((SUBLANES, d), lambda i: (0, 0))],
        out_shape=[jax.ShapeDtypeStruct((t_len, d), F32),
                   jax.ShapeDtypeStruct((t_len, d), BF16),
                   jax.ShapeDtypeStruct((SUBLANES, d), F32)],
        args=(dz, w, x, g, dres), sem=("arbitrary",), comm=comm)


def _mm_tn(a, g, name, tk, tn, planes=False, tt=1024, comm=None):
    t_len, k = a.shape
    n = 2 * g.shape[2] if planes else g.shape[1]
    nn = n // tn
    half = nn // 2
    tt = min(tt, t_len)

    def body(a_ref, g_ref, o_ref):
        @pl.when(pl.program_id(2) == 0)
        def _():
            o_ref[...] = jnp.zeros_like(o_ref)

        o_ref[...] += lax.dot_general(a_ref[...], g_ref[...], (((0,), (0,)), ((), ())),
                                      preferred_element_type=F32)

    if planes:
        g_spec = pl.BlockSpec((None, tt, tn), lambda i, j, t: (j // half, t, j % half))
    else:
        g_spec = pl.BlockSpec((tt, tn), lambda i, j, t: (t, j))
    return _pallas(
        body, name=name, grid=(k // tk, nn, t_len // tt),
        in_specs=[pl.BlockSpec((tt, tk), lambda i, j, t: (t, i)), g_spec],
        out_specs=[pl.BlockSpec((tk, tn), lambda i, j, t: (i, j))],
        out_shape=[jax.ShapeDtypeStruct((k, n), F32)],
        args=(a, g), sem=("parallel", "parallel", "arbitrary"), comm=comm)


def _lru_gates(rp, ip, spn):
    r = _sigmoid(rp)
    i = _sigmoid(ip)
    la = r * spn
    a = jnp.exp(la)
    mult = jnp.sqrt(_neg_expm1(2.0 * la, a))
    return r, i, a, mult


def _mixer_fwd(z, cw8, vec8, wa_bd, wx_bd, scw8, name, tb=256, comm=None):
    t_len = z.shape[0]

    def body(z_ref, cw_ref, vec_ref, wa_ref, wx_ref, scw_ref, y_ref, h_ref,
             xhalo, phalo, hcar, lx_s, rp_s, ip_s):
        @pl.when(pl.program_id(0) == 0)
        def _():
            xhalo[...] = jnp.zeros_like(xhalo)
            phalo[...] = jnp.zeros_like(phalo)
            hcar[...] = jnp.zeros_like(hcar)

        cw = cw_ref[...]
        vec = vec_ref[...]
        xp = z_ref[:, 0:D_LRU].astype(F32)
        ext = jnp.concatenate([xhalo[...], xp], axis=0)
        lx = vec[0:1] + _conv_taps(ext, [cw[k:k + 1] for k in range(4)], tb)
        xhalo[...] = xp[tb - SUBLANES:]
        lx_s[...] = lx
        lxb = lx.astype(BF16)
        for q in range(4):
            sl = slice(q * 256, (q + 1) * 256)
            rp_s[:, sl] = jnp.dot(lxb[:, sl], wa_ref[q], preferred_element_type=F32) + vec[1:2, sl]
            ip_s[:, sl] = jnp.dot(lxb[:, sl], wx_ref[q], preferred_element_type=F32) + vec[2:3, sl]

        spn = jnp.broadcast_to(-RG_C * _softplus_neg(vec[3:4]), (SUBLANES, D_LRU))
        row = lax.broadcasted_iota(jnp.int32, (SUBLANES, D_LRU), 0)

        def step(ci, carry):
            o = pl.multiple_of(ci * PACKED, PACKED)
            gate = z_ref[pl.ds(o, PACKED), D_LRU:2 * D_LRU].astype(F32)
            ys = []
            for sub in range(2):
                rows = pl.ds(pl.multiple_of(o + sub * SUBLANES, SUBLANES), SUBLANES)
                lxv = lx_s[rows, :]
                _, i, a, mult = _lru_gates(rp_s[rows, :], ip_s[rows, :], spn)
                h = _scan8(a, mult * (i * lxv), carry, row)
                h_ref[rows, :] = h
                ys.append(h * _gelu(gate[sub * SUBLANES:(sub + 1) * SUBLANES]))
                carry = jnp.broadcast_to(h[SUBLANES - 1:SUBLANES, :], (SUBLANES, D_LRU))
            y_ref[pl.ds(o, PACKED), 0:D_LRU] = jnp.concatenate(ys, axis=0).astype(BF16)
            return carry

        hcar[...] = lax.fori_loop(0, tb // PACKED, step, hcar[...])

        scw = scw_ref[...]
        o_b, o_c, o_x = 2 * D_LRU, 2 * D_LRU + D_SC, 2 * D_LRU + 2 * D_SC
        p = z_ref[:, o_c:o_x].astype(F32) * z_ref[:, o_x:].astype(F32)
        pext = jnp.concatenate([phalo[...], p], axis=0)
        q = _conv_taps(pext, [scw[k:k + 1] for k in range(3)], tb)
        phalo[...] = p[tb - SUBLANES:]
        y_ref[:, D_LRU:] = (z_ref[:, o_b:o_c].astype(F32) * q).astype(BF16)

    const = lambda t: (0, 0)
    return _pallas(
        body, name=name, grid=(t_len // tb,),
        in_specs=[pl.BlockSpec((tb, D_IN), lambda t: (t, 0)),
                  pl.BlockSpec((SUBLANES, D_LRU), const),
                  pl.BlockSpec((SUBLANES, D_LRU), const),
                  pl.BlockSpec((4, 256, 256), lambda t: (0, 0, 0)),
                  pl.BlockSpec((4, 256, 256), lambda t: (0, 0, 0)),
                  pl.BlockSpec((SUBLANES, D_SC), const)],
        out_specs=[pl.BlockSpec((tb, D_MIX), lambda t: (t, 0)),
                   pl.BlockSpec((tb, D_LRU), lambda t: (t, 0))],
        out_shape=[jax.ShapeDtypeStruct((t_len, D_MIX), BF16),
                   jax.ShapeDtypeStruct((t_len, D_LRU), F32)],
        scratch_shapes=[pltpu.VMEM((SUBLANES, D_LRU), F32), pltpu.VMEM((SUBLANES, D_SC), F32),
                        pltpu.VMEM((SUBLANES, D_LRU), F32), pltpu.VMEM((tb, D_LRU), F32),
                        pltpu.VMEM((tb, D_LRU), F32), pltpu.VMEM((tb, D_LRU), F32)],
        args=(z, cw8, vec8, wa_bd, wx_bd, scw8), sem=("arbitrary",), comm=comm)


def _mixer_bwd(z, h, dy, cw8, vec8, wa_bd, wx_bd, scw8, name, tb=128, comm=None):
    t_len = z.shape[0]
    nb = t_len // tb

    def body(z_ref, zh_ref, h_ref, hh_ref, dy_ref, cw_ref, vec_ref, wa_ref, wx_ref, scw_ref,
             dz_ref, dcw_ref, dvec_ref, dwa_ref, dwx_ref, dscw_ref,
             lx_s, rp_s, ip_s, drpb_s, dipb_s, dlx_s, hext_s, acc_s, acar, gcar, dqh):
        t = pl.program_id(0)
        first_block = t == nb - 1

        @pl.when(t == 0)
        def _():
            for ref in (dcw_ref, dvec_ref, dwa_ref, dwx_ref, dscw_ref, acc_s, acar, gcar, dqh):
                ref[...] = jnp.zeros_like(ref)
            dlx_s[tb:, :] = jnp.zeros((SUBLANES, D_LRU), F32)

        cw = cw_ref[...]
        vec = vec_ref[...]
        scw = scw_ref[...]
        ctaps = [cw[k:k + 1] for k in range(4)]
        staps = [scw[k:k + 1] for k in range(3)]
        keep = jnp.where(first_block, 0.0, 1.0)
        zh = zh_ref[...].astype(F32)[PACKED - SUBLANES:] * keep

        xp = z_ref[:, 0:D_LRU].astype(F32)
        xext = jnp.concatenate([zh[:, 0:D_LRU], xp], axis=0)
        lx = vec[0:1] + _conv_taps(xext, ctaps, tb)
        lx_s[...] = lx
        lxb = lx.astype(BF16)
        for q in range(4):
            sl = slice(q * 256, (q + 1) * 256)
            rp_s[:, sl] = jnp.dot(lxb[:, sl], wa_ref[q], preferred_element_type=F32) + vec[1:2, sl]
            ip_s[:, sl] = jnp.dot(lxb[:, sl], wx_ref[q], preferred_element_type=F32) + vec[2:3, sl]
        hext_s[0:SUBLANES, :] = hh_ref[...] * keep
        hext_s[SUBLANES:, :] = h_ref[...]

        spn = jnp.broadcast_to(-RG_C * _softplus_neg(vec[3:4]), (SUBLANES, D_LRU))
        row = lax.broadcasted_iota(jnp.int32, (SUBLANES, D_LRU), 0)

        def step(ci, carry):
            a_next, g_next = carry
            o = pl.multiple_of((tb // PACKED - 1 - ci) * PACKED, PACKED)
            rows16 = pl.ds(o, PACKED)
            gate16 = z_ref[rows16, D_LRU:2 * D_LRU].astype(F32)
            dyl16 = dy_ref[rows16, 0:D_LRU].astype(F32)
            dgs, drs, dis = [None, None], [None, None], [None, None]
            for sub in (1, 0):
                oo = pl.multiple_of(o + sub * SUBLANES, SUBLANES)
                rows = pl.ds(oo, SUBLANES)
                half = slice(sub * SUBLANES, (sub + 1) * SUBLANES)
                lxv = lx_s[rows, :]
                r, i, a, mult = _lru_gates(rp_s[rows, :], ip_s[rows, :], spn)
                hwin = hext_s[pl.ds(oo, 2 * SUBLANES), :]
                hv = hwin[SUBLANES:]
                hprev = pltpu.roll(hwin, 1, axis=0)[SUBLANES:]
                gel, dgel = _gelu_parts(gate16[half])
                dyl = dyl16[half]
                a_up = jnp.where(row < SUBLANES - 1, pltpu.roll(a, SUBLANES - 1, axis=0), a_next)
                gg = _scan8_rev(a_up, dyl * gel, g_next, row)
                dgs[sub] = dyl * hv * dgel
                ilx = i * lxv
                dla = gg * hprev * a - (gg * ilx) * (a * a) / mult
                dlx_s[rows, :] = gg * mult * i
                drp = dla * spn * r * (1.0 - r)
                dip = gg * mult * lxv * i * (1.0 - i)
                drs[sub] = drp
                dis[sub] = dip
                acc_s[0] += drp
                acc_s[1] += dip
                acc_s[2] += dla * r
                a_next = jnp.broadcast_to(a[0:1, :], (SUBLANES, D_LRU))
                g_next = jnp.broadcast_to(gg[0:1, :], (SUBLANES, D_LRU))
            dz_ref[rows16, D_LRU:2 * D_LRU] = jnp.concatenate(dgs, axis=0).astype(BF16)
            drpb_s[rows16, :] = jnp.concatenate(drs, axis=0).astype(BF16)
            dipb_s[rows16, :] = jnp.concatenate(dis, axis=0).astype(BF16)
            return a_next, g_next

        a_c, g_c = lax.fori_loop(0, tb // PACKED, step, (acar[...], gcar[...]))
        acar[...] = a_c
        gcar[...] = g_c

        drpb = drpb_s[...]
        dipb = dipb_s[...]
        nt_dims = (((1,), (1,)), ((), ()))
        tn_dims = (((0,), (0,)), ((), ()))
        for q in range(4):
            sl = slice(q * 256, (q + 1) * 256)
            dlx_s[0:tb, sl] += (
                lax.dot_general(drpb[:, sl], wa_ref[q], nt_dims, preferred_element_type=F32)
                + lax.dot_general(dipb[:, sl], wx_ref[q], nt_dims, preferred_element_type=F32))
            dwa_ref[q] += lax.dot_general(lxb[:, sl], drpb[:, sl], tn_dims, preferred_element_type=F32)
            dwx_ref[q] += lax.dot_general(lxb[:, sl], dipb[:, sl], tn_dims, preferred_element_type=F32)

        dlx_ext = dlx_s[...]
        dlx = dlx_ext[0:tb]
        dz_ref[:, 0:D_LRU] = _conv_taps_t(dlx_ext, ctaps, tb).astype(BF16)
        dcw_ref[3:4, :] += jnp.sum(dlx * xp, axis=0, keepdims=True)
        for k in range(3):
            shifted = pltpu.roll(xext, 3 - k, axis=0)[SUBLANES:]
            dcw_ref[k:k + 1, :] += jnp.sum(dlx * shifted, axis=0, keepdims=True)
        dvec_ref[0:1, :] += jnp.sum(dlx, axis=0, keepdims=True)
        dlx_s[tb:, :] = dlx[0:SUBLANES]

        o_b, o_c, o_x = 2 * D_LRU, 2 * D_LRU + D_SC, 2 * D_LRU + 2 * D_SC
        sb = z_ref[:, o_b:o_c].astype(F32)
        scc = z_ref[:, o_c:o_x].astype(F32)
        sx = z_ref[:, o_x:].astype(F32)
        p = scc * sx
        pext = jnp.concatenate([zh[:, o_c:o_x] * zh[:, o_x:], p], axis=0)
        q = _conv_taps(pext, staps, tb)
        dys = dy_ref[:, D_LRU:].astype(F32)
        dq = dys * sb
        dp = _conv_taps_t(jnp.concatenate([dq, dqh[...]], axis=0), staps, tb)
        dscw_ref[2:3, :] += jnp.sum(dq * p, axis=0, keepdims=True)
        for k in range(2):
            shifted = pltpu.roll(pext, 2 - k, axis=0)[SUBLANES:]
            dscw_ref[k:k + 1, :] += jnp.sum(dq * shifted, axis=0, keepdims=True)
        dqh[...] = dq[0:SUBLANES]
        dz_ref[:, o_b:o_c] = (dys * q).astype(BF16)
        dz_ref[:, o_c:o_x] = (dp * sx).astype(BF16)
        dz_ref[:, o_x:] = (dp * scc).astype(BF16)

        @pl.when(first_block)
        def _():
            dvec_ref[1:2, :] = jnp.sum(acc_s[0], axis=0, keepdims=True)
            dvec_ref[2:3, :] = jnp.sum(acc_s[1], axis=0, keepdims=True)
            dvec_ref[3:4, :] = (jnp.sum(acc_s[2], axis=0, keepdims=True) * RG_C * _sigmoid(-vec[3:4]))

    blk = lambda t: (nb - 1 - t, 0)
    halo8 = lambda t: (jnp.maximum((nb - 1 - t) * (tb // SUBLANES) - 1, 0), 0)
    halo16 = lambda t: (jnp.maximum((nb - 1 - t) * (tb // PACKED) - 1, 0), 0)
    const = lambda t: (0, 0)
    const3 = lambda t: (0, 0, 0)
    return _pallas(
        body, name=name, grid=(nb,),
        in_specs=[pl.BlockSpec((tb, D_IN), blk), pl.BlockSpec((PACKED, D_IN), halo16),
                  pl.BlockSpec((tb, D_LRU), blk), pl.BlockSpec((SUBLANES, D_LRU), halo8),
                  pl.BlockSpec((tb, D_MIX), blk),
                  pl.BlockSpec((SUBLANES, D_LRU), const), pl.BlockSpec((SUBLANES, D_LRU), const),
                  pl.BlockSpec((4, 256, 256), const3), pl.BlockSpec((4, 256, 256), const3),
                  pl.BlockSpec((SUBLANES, D_SC), const)],
        out_specs=[pl.BlockSpec((tb, D_IN), blk),
                   pl.BlockSpec((SUBLANES, D_LRU), const), pl.BlockSpec((SUBLANES, D_LRU), const),
                   pl.BlockSpec((4, 256, 256), const3), pl.BlockSpec((4, 256, 256), const3),
                   pl.BlockSpec((SUBLANES, D_SC), const)],
        out_shape=[jax.ShapeDtypeStruct((t_len, D_IN), BF16),
                   jax.ShapeDtypeStruct((SUBLANES, D_LRU), F32), jax.ShapeDtypeStruct((SUBLANES, D_LRU), F32),
                   jax.ShapeDtypeStruct((4, 256, 256), F32), jax.ShapeDtypeStruct((4, 256, 256), F32),
                   jax.ShapeDtypeStruct((SUBLANES, D_SC), F32)],
        scratch_shapes=[pltpu.VMEM((tb, D_LRU), F32),
                        pltpu.VMEM((tb, D_LRU), F32), pltpu.VMEM((tb, D_LRU), F32),
                        pltpu.VMEM((tb, D_LRU), BF16), pltpu.VMEM((tb, D_LRU), BF16),
                        pltpu.VMEM((tb + SUBLANES, D_LRU), F32), pltpu.VMEM((tb + SUBLANES, D_LRU), F32),
                        pltpu.VMEM((3, SUBLANES, D_LRU), F32),
                        pltpu.VMEM((SUBLANES, D_LRU), F32), pltpu.VMEM((SUBLANES, D_LRU), F32),
                        pltpu.VMEM((SUBLANES, D_SC), F32)],
        args=(z, z, h, h, dy, cw8, vec8, wa_bd, wx_bd, scw8), sem=("arbitrary",), comm=comm)


def _ffn_act(u, fw, name, tb=256, tn=512, rc=16, comm=None):
    t_len = u.shape[1]
    hb = tb // PACKED

    def body(u_ref, uh_ref, fw_ref, o_ref, ext):
        keep = jnp.where(pl.program_id(0) == 0, 0.0, 1.0)
        ext[:, 0:SUBLANES, :] = uh_ref[...].astype(F32)[:, PACKED - SUBLANES:, :] * keep
        ext[:, SUBLANES:, :] = u_ref[...].astype(F32)
        fw_v = fw_ref[...]
        wg = [fw_v[0, k:k + 1, :] for k in range(3)]
        wu = [fw_v[1, k:k + 1, :] for k in range(3)]

        def chunk(ci, c):
            o = pl.multiple_of(ci * rc, rc)
            win = pl.ds(o, rc + SUBLANES)
            gate = _conv_taps(ext[0, win, :], wg, rc)
            up = _conv_taps(ext[1, win, :], wu, rc)
            o_ref[pl.ds(o, rc), :] = (_gelu(gate) * up).astype(BF16)
            return c

        lax.fori_loop(0, tb // rc, chunk, 0)

    return _pallas(
        body, name=name, grid=(t_len // tb, D_FF // tn),
        in_specs=[pl.BlockSpec((2, tb, tn), lambda i, j: (0, i, j)),
                  pl.BlockSpec((2, PACKED, tn), lambda i, j: (0, jnp.maximum(i * hb - 1, 0), j)),
                  pl.BlockSpec((2, SUBLANES, tn), lambda i, j: (0, 0, j))],
        out_specs=[pl.BlockSpec((tb, tn), lambda i, j: (i, j))],
        out_shape=[jax.ShapeDtypeStruct((t_len, D_FF), BF16)],
        scratch_shapes=[pltpu.VMEM((2, tb + SUBLANES, tn), F32)],
        args=(u, u, fw), sem=("parallel", "parallel"), comm=comm)


def _ffn_bwd(dact, u, fw, name, tb=256, tn=512, rc=16, comm=None):
    t_len = u.shape[1]
    ni = t_len // tb
    hb = tb // PACKED
    last_halo = t_len // PACKED - 1

    def body(d_ref, dn_ref, u_ref, up_ref, un_ref, fw_ref, du_ref, dfw_ref, extu, extd, acc):
        i = pl.program_id(1)

        @pl.when(i == 0)
        def _():
            acc[...] = jnp.zeros_like(acc)

        keep_prev = jnp.where(i == 0, 0.0, 1.0)
        keep_next = jnp.where(i == ni - 1, 0.0, 1.0)
        extu[:, 0:SUBLANES, :] = up_ref[...].astype(F32)[:, PACKED - SUBLANES:, :] * keep_prev
        extu[:, SUBLANES:SUBLANES + tb, :] = u_ref[...].astype(F32)
        extu[:, SUBLANES + tb:, :] = un_ref[...].astype(F32)[:, 0:SUBLANES, :]
        extd[0:tb, :] = d_ref[...].astype(F32)
        extd[tb:, :] = dn_ref[...].astype(F32)[0:SUBLANES] * keep_next
        fw_v = fw_ref[...]
        taps = [[fw_v[pln, k:k + 1, :] for k in range(3)] for pln in range(2)]
        m = rc + SUBLANES

        def chunk(ci, c):
            o = pl.multiple_of(ci * rc, rc)
            win = pl.ds(o, rc + 2 * SUBLANES)
            sh = []
            for pln in range(2):
                e = extu[pln, win, :]
                sh.append([pltpu.roll(e, 2, axis=0)[SUBLANES:], pltpu.roll(e, 1, axis=0)[SUBLANES:], e[SUBLANES:]])
            gate = sum(taps[0][k] * sh[0][k] for k in range(3))
            up = sum(taps[1][k] * sh[1][k] for k in range(3))
            dv = extd[pl.ds(o, m), :]
            gel, dgel = _gelu_parts(gate)
            dpost = [dv * up * dgel, dv * gel]
            for pln in range(2):
                du_ref[pln, pl.ds(o, rc), :] = _conv_taps_t(dpost[pln], taps[pln], rc).astype(BF16)
                for k in range(3):
                    prod = dpost[pln][0:rc] * sh[pln][k][0:rc]
                    acc[3 * pln + k] += prod[0:SUBLANES] + prod[SUBLANES:]
            return c

        lax.fori_loop(0, tb // rc, chunk, 0)

        @pl.when(i == ni - 1)
        def _():
            dfw_ref[...] = jnp.zeros_like(dfw_ref)
            for pln in range(2):
                for k in range(3):
                    dfw_ref[pln, k:k + 1, :] = jnp.sum(acc[3 * pln + k], axis=0, keepdims=True)

    return _pallas(
        body, name=name, grid=(D_FF // tn, ni),
        in_specs=[pl.BlockSpec((tb, tn), lambda j, i: (i, j)),
                  pl.BlockSpec((PACKED, tn), lambda j, i: (jnp.minimum((i + 1) * hb, last_halo), j)),
                  pl.BlockSpec((2, tb, tn), lambda j, i: (0, i, j)),
                  pl.BlockSpec((2, PACKED, tn), lambda j, i: (0, jnp.maximum(i * hb - 1, 0), j)),
                  pl.BlockSpec((2, PACKED, tn), lambda j, i: (0, jnp.minimum((i + 1) * hb, last_halo), j)),
                  pl.BlockSpec((2, SUBLANES, tn), lambda j, i: (0, 0, j))],
        out_specs=[pl.BlockSpec((2, tb, tn), lambda j, i: (0, i, j)),
                   pl.BlockSpec((2, SUBLANES, tn), lambda j, i: (0, 0, j))],
        out_shape=[jax.ShapeDtypeStruct((2, t_len, D_FF), BF16),
                   jax.ShapeDtypeStruct((2, SUBLANES, D_FF), F32)],
        scratch_shapes=[pltpu.VMEM((2, tb + 2 * SUBLANES, tn), F32),
                        pltpu.VMEM((tb + SUBLANES, tn), F32),
                        pltpu.VMEM((6, SUBLANES, tn), F32)],
        args=(dact, dact, u, u, u, fw), sem=("parallel", "arbitrary"), comm=comm)


def _loss_head(x, g, target, name, tb=256):
    t_len, d = x.shape

    def body(x_ref, g_ref, t_ref, dx_ref, dxb_ref, dg_ref, loss_ref):
        @pl.when(pl.program_id(0) == 0)
        def _():
            dg_ref[...] = jnp.zeros_like(dg_ref)
            loss_ref[...] = jnp.zeros_like(loss_ref)

        xv = x_ref[...]
        gv = g_ref[...]
        r = lax.rsqrt(jnp.mean(xv * xv, axis=-1, keepdims=True) + EPS)
        xh = xv * r
        err = xh * gv - t_ref[...]
        loss_ref[...] += (0.5 / d) * jnp.sum(jnp.sum(err * err, axis=-1, keepdims=True), axis=0, keepdims=True)
        dy = err * (1.0 / d)
        dyg = dy * gv
        dx = r * (dyg - xh * jnp.mean(dyg * xh, axis=-1, keepdims=True))
        dx_ref[...] = dx
        dxb_ref[...] = dx.astype(BF16)
        dg_ref[0:1, :] += jnp.sum(dy * xh, axis=0, keepdims=True)

    return _pallas(
        body, name=name, grid=(t_len // tb,),
        in_specs=[pl.BlockSpec((tb, d), lambda i: (i, 0)), pl.BlockSpec((1, d), lambda i: (0, 0)),
                  pl.BlockSpec((tb, d), lambda i: (i, 0))],
        out_specs=[pl.BlockSpec((tb, d), lambda i: (i, 0)), pl.BlockSpec((tb, d), lambda i: (i, 0)),
                   pl.BlockSpec((SUBLANES, d), lambda i: (0, 0)),
                   pl.BlockSpec((SUBLANES, LANES), lambda i: (0, 0))],
        out_shape=[jax.ShapeDtypeStruct((t_len, d), F32), jax.ShapeDtypeStruct((t_len, d), BF16),
                   jax.ShapeDtypeStruct((SUBLANES, d), F32), jax.ShapeDtypeStruct((SUBLANES, LANES), F32)],
        args=(x, g, target), sem=("arbitrary",))[0]


def _adamw(w, g, m, v, name):
    r, c = w.shape
    tr = 256 if r % 256 == 0 else r
    c1 = 1.0 / (1.0 - ADAM_B1 ** ADAM_STEP)
    c2 = 1.0 / (1.0 - ADAM_B2 ** ADAM_STEP)

    def body(w_ref, g_ref, m_ref, v_ref, d_ref, mo_ref, vo_ref):
        gv = g_ref[...]
        mn = ADAM_B1 * m_ref[...] + (1.0 - ADAM_B1) * gv
        vn = ADAM_B2 * v_ref[...] + (1.0 - ADAM_B2) * (gv * gv)
        d_ref[...] = -ADAM_LR * ((mn * c1) / (jnp.sqrt(vn * c2) + ADAM_EPS) + ADAM_WD * w_ref[...])
        mo_ref[...] = mn
        vo_ref[...] = vn

    spec = pl.BlockSpec((tr, c), lambda i: (i, 0))
    shape = jax.ShapeDtypeStruct((r, c), F32)
    return _pallas(
        body, name=name, grid=(r // tr,),
        in_specs=[spec] * 4, out_specs=[spec] * 3, out_shape=[shape] * 3,
        args=(w, g, m, v), sem=("parallel",))[0]


def _place():
    x, y, c = lax.axis_index("x"), lax.axis_index("y"), lax.axis_index("c")
    chips = [(1 - x, y), (x, 1 - y), (1 - x, 1 - y)]
    return x, y, c, chips


def _remote(src, dst, send, recv, sem, to):
    return pltpu.make_async_remote_copy(
        src_ref=src, dst_ref=dst, send_sem=send.at[sem], recv_sem=recv.at[sem], device_id=to, device_id_type=MESH)


def _gather_plan(fulls, kinds, mid_at=None):
    def region(f, kind, k, cc):
        if kind == SMALL:
            return f.at[k, pl.ds(cc * (CONV_PACK_ROWS // 2), CONV_PACK_ROWS // 2), :]
        if COL_SHARDED[kind]:
            rows, cols = f.shape[0], f.shape[1] // N_CHIP
            return f.at[pl.ds(cc * (rows // 2), rows // 2), pl.ds(k * cols, cols)]
        rows = f.shape[0] // N_CHIP
        return f.at[pl.ds(k * rows + cc * (rows // 2), rows // 2), :]

    def first_hop(bufs, send, recv, it, j):
        x, y, c, chips = _place()
        reg = region(bufs[it], kinds[it], 2 * x + y, c)
        return _remote(reg, reg, send, recv, it * 6 + j, (*chips[j], c))

    def arrival(bufs, send, recv, it, j, second):
        x, y, c, chips = _place()
        px, py = chips[j]
        reg = region(bufs[it], kinds[it], 2 * px + py, 1 - c if second else c)
        to = (x, y, 1 - c) if second else (px, py, c)
        return _remote(reg, reg, send, recv, it * 6 + (3 + j if second else j), to)

    def forward(bufs, send, recv, it, j):
        x, y, c, chips = _place()
        px, py = chips[j]
        reg = region(bufs[it], kinds[it], 2 * px + py, c)
        return _remote(reg, reg, send, recv, it * 6 + 3 + j, (x, y, 1 - c))

    def start(srcs, bufs, outs, send, recv):
        for it in range(len(bufs)):
            for j in range(3):
                first_hop(bufs, send, recv, it, j).start()

    def mid(srcs, bufs, outs, send, recv):
        for it in range(len(bufs)):
            for j in range(3):
                arrival(bufs, send, recv, it, j, False).wait_recv()
                forward(bufs, send, recv, it, j).start()

    def finish(srcs, bufs, outs, send, recv):
        for it in range(len(bufs)):
            for j in range(3):
                arrival(bufs, send, recv, it, j, True).wait_recv()
        for it in range(len(bufs)):
            for j in range(3):
                first_hop(bufs, send, recv, it, j).wait_send()
                forward(bufs, send, recv, it, j).wait_send()

    return Comm(srcs=(), bufs=tuple(fulls), outs=(), n_sem=6 * len(fulls), start=start, mid=mid, finish=finish,
                mid_at=mid_at)


def _half_axis(kind):
    return 0 if kind == SMALL or COL_SHARDED[kind] else 1


def _half2(ref, kind, cc):
    if _half_axis(kind) == 0:
        return ref.at[pl.ds(cc * (ref.shape[0] // 2), ref.shape[0] // 2), :]
    return ref.at[:, pl.ds(cc * (ref.shape[1] // 2), ref.shape[1] // 2)]


def _pair_plan(grads, kinds):
    def land_shape(g, kind):
        s = list(g.shape)
        s[_half_axis(kind)] //= 2
        return jax.ShapeDtypeStruct(tuple(s), F32)

    def copy(srcs, outs, send, recv, it):
        x, y, c, _ = _place()
        return _remote(_half2(srcs[it], kinds[it], 1 - c), outs[it], send, recv, it, (x, y, 1 - c))

    def start(srcs, bufs, outs, send, recv):
        for it in range(len(srcs)):
            copy(srcs, outs, send, recv, it).start()

    def finish(srcs, bufs, outs, send, recv):
        for it in range(len(srcs)):
            copy(srcs, outs, send, recv, it).wait_send()
        for it in range(len(srcs)):
            copy(srcs, outs, send, recv, it).wait_recv()

    return Comm(srcs=tuple(grads), bufs=(), outs=tuple(land_shape(g, k) for g, k in zip(grads, kinds)),
                n_sem=len(grads), start=start, finish=finish)


def _scatter_plan(parts, slots, kinds):
    def piece(s, kind, k):
        if kind == SMALL:
            return s
        if COL_SHARDED[kind]:
            n = s.shape[1] // N_CHIP
            return s.at[:, pl.ds(k * n, n)]
        n = s.shape[0] // N_CHIP
        return s.at[pl.ds(k * n, n), :]

    def outbound(srcs, bufs, send, recv, it, j):
        x, y, c, chips = _place()
        px, py = chips[j]
        return _remote(piece(srcs[it], kinds[it], 2 * px + py), bufs[it].at[2 * x + y], send, recv, it * 3 + j,
                       (px, py, c))

    def inbound(bufs, send, recv, it, j):
        x, y, c, chips = _place()
        px, py = chips[j]
        got = bufs[it].at[2 * px + py]
        return _remote(got, got, send, recv, it * 3 + j, (px, py, c))

    def start(srcs, bufs, outs, send, recv):
        for it in range(len(srcs)):
            for j in range(3):
                outbound(srcs, bufs, send, recv, it, j).start()

    def finish(srcs, bufs, outs, send, recv):
        for it in range(len(srcs)):
            for j in range(3):
                inbound(bufs, send, recv, it, j).wait_recv()
        for it in range(len(srcs)):
            for j in range(3):
                outbound(srcs, bufs, send, recv, it, j).wait_send()

    return Comm(srcs=tuple(parts), bufs=tuple(slots), outs=(), n_sem=3 * len(parts), start=start, finish=finish)


def _share_plan(fulls, kinds, layer):
    def half(f, kind, cc):
        return _half2(f if kind == SMALL else f.at[layer], kind, cc)

    def copy(bufs, send, recv, it, cc):
        x, y, c, _ = _place()
        reg = half(bufs[it], kinds[it], c if cc == "mine" else 1 - c)
        return _remote(reg, reg, send, recv, it, (x, y, 1 - c))

    def start(srcs, bufs, outs, send, recv):
        for it in range(len(bufs)):
            copy(bufs, send, recv, it, "mine").start()

    def finish(srcs, bufs, outs, send, recv):
        for it in range(len(bufs)):
            copy(bufs, send, recv, it, "other").wait_recv()
        for it in range(len(bufs)):
            copy(bufs, send, recv, it, "mine").wait_send()

    return Comm(srcs=(), bufs=tuple(fulls), outs=(), n_sem=len(fulls), start=start, finish=finish)


def _pair_sum(g, land, idx, kind, name):
    odt = F32 if kind == SMALL else BF16
    r, cdim = land.shape

    def body(idx_ref, g_ref, l_ref, p_ref, s_ref):
        v = (g_ref[...] + l_ref[...]).astype(odt)
        p_ref[...] = v
        if kind == SMALL:
            s_ref[...] = v
        else:
            @pl.when(pl.program_id(1 if COL_SHARDED[kind] else 0) == idx_ref[1])
            def _():
                s_ref[...] = v

    if kind == SMALL:
        grid = (1,)
        g_spec = pl.BlockSpec((r, LANES), lambda i, idx_ref: (idx_ref[0], 0))
        spec = pl.BlockSpec((r, LANES), lambda i, idx_ref: (0, 0))
        s_spec = pl.BlockSpec((None, r, LANES), lambda i, idx_ref: (idx_ref[1], 0, 0))
        s_shape = (N_CHIP, r, LANES)
    elif COL_SHARDED[kind]:
        pc, tr = cdim // N_CHIP, 256
        nrb = r // tr
        grid = (nrb, N_CHIP)
        g_spec = pl.BlockSpec((tr, pc), lambda i, k, idx_ref: (idx_ref[0] * nrb + i, k))
        spec = pl.BlockSpec((tr, pc), lambda i, k, idx_ref: (i, k))
        s_spec = pl.BlockSpec((None, tr, pc), lambda i, k, idx_ref: (idx_ref[1], i, 0))
        s_shape = (N_CHIP, r, pc)
    else:
        pr = r // N_CHIP
        grid = (N_CHIP,)
        g_spec = pl.BlockSpec((pr, cdim), lambda k, idx_ref: (k, idx_ref[0]))
        spec = pl.BlockSpec((pr, cdim), lambda k, idx_ref: (k, 0))
        s_spec = pl.BlockSpec((None, pr, cdim), lambda k, idx_ref: (idx_ref[1], 0, 0))
        s_shape = (N_CHIP, pr, cdim)
    return pl.pallas_call(
        body, name=name,
        grid_spec=pltpu.PrefetchScalarGridSpec(
            num_scalar_prefetch=1, grid=grid, in_specs=[g_spec, spec], out_specs=[spec, s_spec]),
        out_shape=[jax.ShapeDtypeStruct(land.shape, odt), jax.ShapeDtypeStruct(s_shape, odt)],
        compiler_params=_cp(*(["arbitrary"] * len(grid))),
    )(idx, g, land)


def _sum_slots(slots, idx, kind, layer, prev, name):
    _, r, cdim = slots.shape

    def body(*refs):
        s_ref, o_ref = refs[1], refs[-1]
        v = s_ref[...].astype(F32)
        o_ref[...] = (v[0] + v[1]) + (v[2] + v[3])

    if kind == SMALL:
        grid = (1,)
        s_spec = pl.BlockSpec((N_CHIP, r, cdim), lambda i, idx_ref: (0, 0, 0))
        o_spec = pl.BlockSpec((r, cdim), lambda i, idx_ref: (idx_ref[0], 0))
        full = (2 * r, cdim)
    else:
        tr = 256 if r % 256 == 0 else 384
        nrb = r // tr
        grid = (nrb,)
        s_spec = pl.BlockSpec((N_CHIP, tr, cdim), lambda i, idx_ref: (0, i, 0))
        if COL_SHARDED[kind]:
            o_spec = pl.BlockSpec((None, tr, cdim), lambda i, idx_ref: (layer, idx_ref[0] * nrb + i, 0))
            full = (2, 2 * r, cdim)
        else:
            o_spec = pl.BlockSpec((None, tr, cdim), lambda i, idx_ref: (layer, i, idx_ref[0]))
            full = (2, r, 2 * cdim)
    in_specs, args, aliases = [s_spec], [idx, slots], {}
    if prev is not None:
        in_specs.append(ANY)
        args.append(prev)
        aliases = {2: 0}
    return pl.pallas_call(
        body, name=name,
        grid_spec=pltpu.PrefetchScalarGridSpec(
            num_scalar_prefetch=1, grid=grid, in_specs=in_specs, out_specs=o_spec),
        out_shape=jax.ShapeDtypeStruct(full, F32),
        input_output_aliases=aliases,
        compiler_params=_cp(*(["parallel"] * len(grid))),
    )(*args)


def _block_diag(w):
    w4 = w.reshape(4, 4, 64, 64)
    eye = jnp.eye(4, dtype=w.dtype)[None, :, None, :, None]
    return (w4[:, :, :, None, :] * eye).reshape(4, 256, 256)


def _block_diag_extract(d):
    d5 = d.reshape(4, 4, 64, 4, 64)
    return jnp.stack([d5[:, hh, :, hh, :] for hh in range(4)], axis=1).reshape(16, 64, 64)


def _rows8(a):
    return jnp.pad(a, ((0, SUBLANES - a.shape[0]), (0, 0)))


def _pack_rep(norm1_g, conv_b, ba, bx, lam, norm2_g, wa, wx, final_g):
    parts = []
    for l in range(2):
        parts += [norm1_g[l], conv_b[l], ba[l], bx[l], lam[l], norm2_g[l], wa[l].reshape(-1), wx[l].reshape(-1)]
    parts.append(final_g)
    return jnp.concatenate(parts).reshape(REP_ROWS, LANES)


def _unpack_rep(buf):
    flat = buf.reshape(-1)
    out = {k: [] for k in ("norm1_g", "lru_conv_b", "lru_ba", "lru_bx", "lru_lambda", "norm2_g", "lru_wa", "lru_wx")}
    for l in range(2):
        o = l * REP_LAYER
        for i, k in enumerate(("norm1_g", "lru_conv_b", "lru_ba", "lru_bx", "lru_lambda", "norm2_g")):
            out[k].append(flat[o + i * 1024:o + (i + 1) * 1024])
        o += 6 * 1024
        out["lru_wa"].append(flat[o:o + 65536].reshape(16, 64, 64))
        out["lru_wx"].append(flat[o + 65536:o + 131072].reshape(16, 64, 64))
    res = {k: jnp.stack(v) for k, v in out.items()}
    res["final_g"] = flat[2 * REP_LAYER:2 * REP_LAYER + 1024]
    return res


def _pack_conv_shard(lru_cw, sc_cw, ffn_cw):
    return jnp.concatenate([lru_cw.reshape(16, LANES), jnp.pad(sc_cw.reshape(6, LANES), ((0, 2), (0, 0))),
                            ffn_cw.reshape(72, LANES)], axis=0)


def _unpack_conv_shard(buf):
    return (buf[0:16].reshape(2, 4, 256), buf[16:22].reshape(2, 3, 128), buf[24:96].reshape(2, 3, 1536))


def kernel(x, norm1_g, w_in, lru_conv_w, lru_conv_b, lru_wa, lru_ba, lru_wx, lru_bx, lru_lambda, sc_conv_w, w_out, norm2_g, w_up, ffn_conv_w, w_down, final_g, loss_target, m_norm1_g, m_w_in, m_lru_conv_w, m_lru_conv_b, m_lru_wa, m_lru_ba, m_lru_wx, m_lru_bx, m_lru_lambda, m_sc_conv_w, m_w_out, m_norm2_g, m_w_up, m_ffn_conv_w, m_w_down, m_final_g, v_norm1_g, v_w_in, v_lru_conv_w, v_lru_conv_b, v_lru_wa, v_lru_ba, v_lru_wx, v_lru_bx, v_lru_lambda, v_sc_conv_w, v_w_out, v_norm2_g, v_w_up, v_ffn_conv_w, v_w_down, v_final_g):
    me = 2 * lax.axis_index("x") + lax.axis_index("y")
    idx = jnp.stack([lax.axis_index("c"), me]).astype(jnp.int32)
    t_len = x.shape[1]

    s_conv = _pack_conv_shard(lru_conv_w, sc_conv_w, ffn_conv_w)
    conv_slots = lax.dynamic_update_slice(jnp.zeros((N_CHIP, CONV_PACK_ROWS, LANES), F32), s_conv[None], (me, 0, 0))
    wi = list(_cast_into_full(w_in, W_IN, idx, "cast_w_in"))
    wo = list(_cast_into_full(w_out, W_OUT, idx, "cast_w_out"))
    wu = list(_cast_into_full(w_up, W_UP, idx, "cast_w_up"))
    wd = list(_cast_into_full(w_down, W_DOWN, idx, "cast_w_down"))
    wi[0], convs = _comm_call(_gather_plan([wi[0], conv_slots], [W_IN, SMALL]), "ag_first")
    per_chip = [_unpack_conv_shard(convs[k]) for k in range(N_CHIP)]
    lru_cw = jnp.concatenate([p[0] for p in per_chip], axis=-1)
    sc_cw = jnp.concatenate([p[1] for p in per_chip], axis=-1)
    ffn_cw = jnp.concatenate([p[2] for p in per_chip], axis=-1)

    cw8 = [_rows8(lru_cw[l]) for l in range(2)]
    vec8 = [_rows8(jnp.stack([lru_conv_b[l], lru_ba[l], lru_bx[l], lru_lambda[l]])) for l in range(2)]
    wa_bd = [_block_diag(lru_wa[l]).astype(BF16) for l in range(2)]
    wx_bd = [_block_diag(lru_wx[l]).astype(BF16) for l in range(2)]
    scw8 = [_rows8(sc_cw[l]) for l in range(2)]
    fw8 = [jnp.pad(ffn_cw[l].reshape(3, 2, D_FF).transpose(1, 0, 2), ((0, 0), (0, 5), (0, 0))) for l in range(2)]

    xs = x[0]
    saved = []
    for l in range(2):
        first = l == 0
        n512, n256 = t_len // 512, t_len // 256
        comm = _gather_plan([wo[0]], [W_OUT], mid_at=(max(n512 - 2, 0),)) if first else None
        (z, h1), got = _norm_mm(xs, norm1_g[l][None], wi[l], f"fwd_in_{l}", comm=comm)
        if first:
            wo[0], = got
        comm = _gather_plan([wu[0]], [W_UP], mid_at=(max(n256 - 2, 0),)) if first else None
        (ymix, hst), got = _mixer_fwd(z, cw8[l], vec8[l], wa_bd[l], wx_bd[l], scw8[l], f"fwd_mixer_{l}", comm=comm)
        if first:
            wu[0], = got
        (x2,), _ = _mm_res(ymix, wo[l], xs, f"fwd_out_{l}")
        comm = _gather_plan([wd[0], wi[1]], [W_DOWN, W_IN], mid_at=(n512 - 1,)) if first else None
        (u, h2), got = _norm_mm(x2, norm2_g[l][None], wu[l], f"fwd_up_{l}", planes=True, comm=comm)
        if first:
            wd[0], wi[1] = got
        comm = _gather_plan([wo[1], wu[1]], [W_OUT, W_UP], mid_at=(n256 - 1, 0)) if first else None
        (act,), got = _ffn_act(u, fw8[l], f"fwd_act_{l}", comm=comm)
        if first:
            wo[1], wu[1] = got
        comm = _gather_plan([wd[1]], [W_DOWN], mid_at=(n512 - 1,)) if first else None
        (x3,), got = _mm_res(act, wd[l], x2, f"fwd_down_{l}", comm=comm)
        if first:
            wd[1], = got
        saved.append((xs, h1, z, hst, ymix, x2, h2, u, act))
        xs = x3

    dx, dxb, dgf, loss_blk = _loss_head(xs, final_g[None], loss_target[0], "loss_head")
    loss = lax.psum(loss_blk[0, 0], ("x", "y", "c"))

    kinds = [W_IN, W_OUT, W_UP, W_DOWN]
    grads = [None, None]
    small = [None, None]
    reduced = [None] * 4
    lands = parts = slots = None
    for l in (1, 0):
        x_in, h1, z, hst, ymix, x2, h2, u, act = saved[l]
        carry = l == 0
        comm = _pair_plan(grads[1], kinds) if carry else None
        (g_down,), got = _mm_tn(act, dxb, f"bwd_wdown_{l}", tk=1536, tn=1024, comm=comm)
        if carry:
            summed = [_pair_sum(grads[1][w], got[w], idx, kinds[w], f"rs_add1_{w}") for w in range(4)]
            parts, slots = [s[0] for s in summed], [s[1] for s in summed]
        (dact,), _ = _mm_nt(dxb, wd[l], f"bwd_dact_{l}")
        comm = _scatter_plan(parts, slots, kinds) if carry else None
        (du, dfw), got = _ffn_bwd(dact, u, fw8[l], f"bwd_act_{l}", comm=comm)
        if carry:
            reduced = [_sum_slots(got[w], idx, kinds[w], 1, None, f"rs_sum1_{w}") for w in range(4)]
        comm = _share_plan(reduced, kinds, 1) if carry else None
        (g_up,), got = _mm_tn(h2, du, f"bwd_wup_{l}", tk=1024, tn=1536, planes=True, comm=comm)
        if carry:
            reduced = list(got)
        comm = _pair_plan([g_up, g_down], [W_UP, W_DOWN]) if carry else None
        (dx2, dx2b, dg2), got = _mm_nt_normbwd(du, wu[l], x2, norm2_g[l][None], dx, f"bwd_up_{l}", planes=True,
                                               comm=comm)
        if carry:
            sum_up = _pair_sum(g_up, got[0], idx, W_UP, "rs_add0_2")
            sum_down = _pair_sum(g_down, got[1], idx, W_DOWN, "rs_add0_3")
        (g_out,), _ = _mm_tn(ymix, dx2b, f"bwd_wout_{l}", tk=1536, tn=1024)
        comm = _pair_plan([g_out], [W_OUT]) if carry else None
        (dymix,), got = _mm_nt(dx2b, wo[l], f"bwd_dymix_{l}", comm=comm)
        if carry:
            sum_out = _pair_sum(g_out, got[0], idx, W_OUT, "rs_add0_1")
            comm = _scatter_plan([sum_out[0], sum_up[0], sum_down[0]], [sum_out[1], sum_up[1], sum_down[1]],
                                 [W_OUT, W_UP, W_DOWN])
        (dz, dcw, dvec, dwa, dwx, dscw), got = _mixer_bwd(z, hst, dymix, cw8[l], vec8[l], wa_bd[l], wx_bd[l],
                                                        scw8[l], f"bwd_mixer_{l}", comm=comm)
        if carry:
            for w, s in zip((W_OUT, W_UP, W_DOWN), got):
                reduced[w] = _sum_slots(s, idx, w, 0, reduced[w], f"rs_sum0_{w}")
        (g_in,), _ = _mm_tn(h1, dz, f"bwd_win_{l}", tk=1024, tn=1792)
        comm = _pair_plan([g_in], [W_IN]) if carry else None
        (dx, dxb, dg1), got = _mm_nt_normbwd(dz, wi[l], x_in, norm1_g[l][None], dx2, f"bwd_in_{l}", comm=comm)
        if carry:
            sum_in = _pair_sum(g_in, got[0], idx, W_IN, "rs_add0_0")
        grads[l] = [g_in, g_out, g_up, g_down]
        rep = [dg1[0], dvec[0], dvec[1], dvec[2], dvec[3], dg2[0],
               _block_diag_extract(dwa).reshape(-1), _block_diag_extract(dwx).reshape(-1)]
        conv = [dcw[0:4].reshape(-1), jnp.pad(dscw[0:3].reshape(-1), (0, 512)),
                dfw[:, 0:3, :].transpose(1, 0, 2).reshape(-1)]
        small[l] = (jnp.concatenate(rep), jnp.concatenate(conv))
    grad_x = dx[None]
    g_small = jnp.concatenate([small[0][0], small[1][0], dgf[0], small[0][1], small[1][1],
                               jnp.zeros((8 * LANES,), F32)]).reshape(SMALL_ROWS, LANES)

    land_small, = _comm_call(_pair_plan([g_small], [SMALL]), "rs_pair_small")
    sum_small = _pair_sum(g_small, land_small, idx, SMALL, "rs_add0_4")
    slot_in, slot_small = _comm_call(
        _scatter_plan([sum_in[0], sum_small[0]], [sum_in[1], sum_small[1]], [W_IN, SMALL]), "rs_scatter_last")
    reduced[W_IN] = _sum_slots(slot_in, idx, W_IN, 0, reduced[W_IN], "rs_sum0_0")
    reduced.append(_sum_slots(slot_small, idx, SMALL, 0, None, "rs_sum0_4"))
    gw_in, gw_out, gw_up, gw_down, gs = _comm_call(_share_plan(reduced, kinds + [SMALL], 0), "rs_share0")

    g_rep = gs[0:REP_ROWS]
    g_conv = gs[REP_ROWS:REP_ROWS + CONV_ROWS].reshape(2, CONV_LAYER)
    g_lru_cw = lax.dynamic_slice_in_dim(g_conv[:, 0:4096].reshape(2, 4, 1024), me * 256, 256, axis=2)
    g_sc_cw = lax.dynamic_slice_in_dim(g_conv[:, 4096:4096 + 1536].reshape(2, 3, 512), me * 128, 128, axis=2)
    g_ffn_cw = lax.dynamic_slice_in_dim(g_conv[:, 6144:].reshape(2, 3, 6144), me * 1536, 1536, axis=2)

    def big(w, g, m, v, name):
        shape = w.shape
        two_d = lambda a: a.reshape(-1, shape[-1])
        return [o.reshape(shape) for o in _adamw(two_d(w), two_d(g), two_d(m), two_d(v), name)]

    upd = {"w_in": big(w_in, gw_in, m_w_in, v_w_in, "adamw_w_in"),
           "w_out": big(w_out, gw_out, m_w_out, v_w_out, "adamw_w_out"),
           "w_up": big(w_up, gw_up, m_w_up, v_w_up, "adamw_w_up"),
           "w_down": big(w_down, gw_down, m_w_down, v_w_down, "adamw_w_down")}
    rep_out = _adamw(
        _pack_rep(norm1_g, lru_conv_b, lru_ba, lru_bx, lru_lambda, norm2_g, lru_wa, lru_wx, final_g), g_rep,
        _pack_rep(m_norm1_g, m_lru_conv_b, m_lru_ba, m_lru_bx, m_lru_lambda, m_norm2_g, m_lru_wa, m_lru_wx, m_final_g),
        _pack_rep(v_norm1_g, v_lru_conv_b, v_lru_ba, v_lru_bx, v_lru_lambda, v_norm2_g, v_lru_wa, v_lru_wx, v_final_g),
        "adamw_rep")
    conv_out = _adamw(s_conv, _pack_conv_shard(g_lru_cw, g_sc_cw, g_ffn_cw),
                      _pack_conv_shard(m_lru_conv_w, m_sc_conv_w, m_ffn_conv_w),
                      _pack_conv_shard(v_lru_conv_w, v_sc_conv_w, v_ffn_conv_w), "adamw_conv")

    names = ["norm1_g", "w_in", "lru_conv_w", "lru_conv_b", "lru_wa", "lru_ba", "lru_wx", "lru_bx", "lru_lambda",
             "sc_conv_w", "w_out", "norm2_g", "w_up", "ffn_conv_w", "w_down", "final_g"]
    groups = []
    g_all = dict(_unpack_rep(g_rep))
    g_all.update(w_in=gw_in, w_out=gw_out, w_up=gw_up, w_down=gw_down,
                 lru_conv_w=g_lru_cw, sc_conv_w=g_sc_cw, ffn_conv_w=g_ffn_cw)
    groups.append(g_all)
    for i in range(3):
        d = dict(_unpack_rep(rep_out[i]))
        cl, cs, cf = _unpack_conv_shard(conv_out[i])
        d.update(lru_conv_w=cl, sc_conv_w=cs, ffn_conv_w=cf)
        d.update({k: v[i] for k, v in upd.items()})
        groups.append(d)
    return (loss, grad_x, *[grp[n] for grp in groups for n in names])
```

```python
import dataclasses
import functools
import math
import operator
from typing import Any, Callable, Optional, Sequence

import jax
import jax.numpy as jnp
from jax import lax
from jax.experimental import pallas as pl
from jax.experimental.pallas import tpu as pltpu

F32 = jnp.float32
BF16 = jnp.bfloat16
MESH = pl.DeviceIdType.MESH

D_MODEL = 1024
D_LRU = 1024
D_SC = 512
D_MIX = D_LRU + D_SC
D_IN = 2 * D_LRU + 3 * D_SC
D_FF = 3072
N_CHIP = 4
RG_C = 8.0
EPS = 1e-6
ADAM_LR = 0.001
ADAM_B1 = 0.9
ADAM_B2 = 0.999
ADAM_EPS = 1e-08
ADAM_WD = 0.01
ADAM_STEP = 10

SUBLANES = 8
PACKED = 16
LANES = 128
VMEM_LIMIT = 56 * 1024 * 1024
GELU_C0 = math.sqrt(2.0 / math.pi)
GELU_C1 = 0.044715

REP_LAYER = 6 * 1024 + 2 * 16 * 64 * 64
REP_ROWS = (2 * REP_LAYER + 1024) // LANES
CONV_LAYER = 4 * 1024 + 2048 + 3 * 6144
CONV_ROWS = 2 * CONV_LAYER // LANES
SMALL_ROWS = REP_ROWS + CONV_ROWS + 8
CONV_PACK_ROWS = 96

W_IN, W_OUT, W_UP, W_DOWN, SMALL = range(5)
COL_SHARDED = {W_IN: True, W_OUT: False, W_UP: True, W_DOWN: False}

ONCE = pl.Buffered(1)
ANY = pl.BlockSpec(memory_space=pl.ANY)


def _cp(*sem):
    return pltpu.CompilerParams(dimension_semantics=sem, vmem_limit_bytes=VMEM_LIMIT)


@dataclasses.dataclass
class Comm:
    srcs: Sequence[Any]
    bufs: Sequence[Any]
    outs: Sequence[Any]
    n_sem: int
    start: Callable
    finish: Callable
    mid: Optional[Callable] = None
    mid_at: Optional[Sequence[int]] = None


def _pallas(body, *, name, grid, in_specs, out_specs, out_shape, args, sem, scratch_shapes=(), comm=None):
    if comm is None:
        res = pl.pallas_call(
            body, name=name, grid=grid, in_specs=list(in_specs), out_specs=list(out_specs),
            out_shape=list(out_shape), scratch_shapes=list(scratch_shapes), compiler_params=_cp(*sem))(*args)
        return tuple(res), ()
    n_in, n_out, n_scr = len(in_specs), len(out_specs), len(scratch_shapes)
    ns, nb, no = len(comm.srcs), len(comm.bufs), len(comm.outs)

    def carrier(*refs):
        p = 0
        main_in = refs[p:p + n_in]
        p += n_in
        c_src = refs[p:p + ns]
        p += ns + nb
        main_out = refs[p:p + n_out]
        p += n_out
        c_buf = refs[p:p + nb]
        p += nb
        c_out = refs[p:p + no]
        p += no
        scr = refs[p:p + n_scr]
        send, recv = refs[p + n_scr], refs[p + n_scr + 1]
        ids = [pl.program_id(a) for a in range(len(grid))]

        def at(steps):
            return functools.reduce(operator.and_, [i == s for i, s in zip(ids, steps)])

        @pl.when(at([0] * len(grid)))
        def _():
            comm.start(c_src, c_buf, c_out, send, recv)

        if comm.mid is not None:
            @pl.when(at(comm.mid_at))
            def _():
                comm.mid(c_src, c_buf, c_out, send, recv)

        body(*main_in, *main_out, *scr)

        @pl.when(at([g - 1 for g in grid]))
        def _():
            comm.finish(c_src, c_buf, c_out, send, recv)

    res = pl.pallas_call(
        carrier, name=name, grid=grid,
        in_specs=list(in_specs) + [ANY] * (ns + nb),
        out_specs=list(out_specs) + [ANY] * (nb + no),
        out_shape=list(out_shape) + [jax.ShapeDtypeStruct(b.shape, b.dtype) for b in comm.bufs] + list(comm.outs),
        input_output_aliases={n_in + ns + j: n_out + j for j in range(nb)},
        scratch_shapes=list(scratch_shapes) + [pltpu.SemaphoreType.DMA((comm.n_sem,)),
                                               pltpu.SemaphoreType.DMA((comm.n_sem,))],
        compiler_params=_cp(*(["arbitrary"] * len(grid))),
    )(*args, *comm.srcs, *comm.bufs)
    return tuple(res[:n_out]), tuple(res[n_out:])


def _comm_call(comm, name):
    ns, nb, no = len(comm.srcs), len(comm.bufs), len(comm.outs)

    def body(*refs):
        c_src = refs[0:ns]
        c_buf = refs[ns + nb:ns + 2 * nb]
        c_out = refs[ns + 2 * nb:ns + 2 * nb + no]
        send, recv = refs[ns + 2 * nb + no], refs[ns + 2 * nb + no + 1]
        comm.start(c_src, c_buf, c_out, send, recv)
        if comm.mid is not None:
            comm.mid(c_src, c_buf, c_out, send, recv)
        comm.finish(c_src, c_buf, c_out, send, recv)

    return tuple(pl.pallas_call(
        body, name=name,
        in_specs=[ANY] * (ns + nb), out_specs=[ANY] * (nb + no),
        out_shape=[jax.ShapeDtypeStruct(b.shape, b.dtype) for b in comm.bufs] + list(comm.outs),
        input_output_aliases={ns + j: j for j in range(nb)},
        scratch_shapes=[pltpu.SemaphoreType.DMA((comm.n_sem,)), pltpu.SemaphoreType.DMA((comm.n_sem,))],
    )(*comm.srcs, *comm.bufs))


def _sigmoid(v):
    return 1.0 / (1.0 + jnp.exp(-v))


def _sigmoid_tanh(v):
    return 0.5 + 0.5 * jnp.tanh(0.5 * v)


def _gelu_parts(v):
    v2 = v * v
    t = jnp.tanh(GELU_C0 * v * (1.0 + GELU_C1 * v2))
    half = 0.5 * (1.0 + t)
    gel = v * half
    dgel = half + 0.5 * v * (1.0 - t * t) * (GELU_C0 * (1.0 + 3.0 * GELU_C1 * v2))
    return gel, dgel


def _gelu(v):
    t = jnp.tanh(GELU_C0 * v * (1.0 + GELU_C1 * (v * v)))
    return 0.5 * v * (1.0 + t)


def _neg_expm1(y, a):
    p = jnp.full_like(y, 1.0 / 5040.0)
    for coef in (1.0 / 720.0, 1.0 / 120.0, 1.0 / 24.0, 1.0 / 6.0, 0.5, 1.0):
        p = p * y + coef
    return jnp.where(y > -0.3, -(p * y), 1.0 - a * a)


def _softplus_neg(lam):
    nl = -lam
    e = jnp.exp(-jnp.abs(nl))
    u = 1.0 + e
    l1p = jnp.where(u == 1.0, e, jnp.log(u) * e / (u - 1.0))
    return jnp.maximum(nl, 0.0) + l1p


def _conv_taps(ext, taps, n_out):
    kw = len(taps)
    acc = taps[kw - 1] * ext[SUBLANES:SUBLANES + n_out]
    for k in range(kw - 1):
        acc = acc + taps[k] * pltpu.roll(ext, kw - 1 - k, axis=0)[SUBLANES:SUBLANES + n_out]
    return acc


def _conv_taps_t(ext, taps, n_out):
    kw = len(taps)
    n = ext.shape[0]
    acc = taps[kw - 1] * ext[0:n_out]
    for k in range(kw - 1):
        acc = acc + taps[k] * pltpu.roll(ext, n - (kw - 1 - k), axis=0)[0:n_out]
    return acc


def _scan8(a, b, carry, row):
    for s in (1, 2, 4):
        m = row >= s
        a_sh = jnp.where(m, pltpu.roll(a, s, axis=0), 1.0)
        b_sh = jnp.where(m, pltpu.roll(b, s, axis=0), 0.0)
        b = a * b_sh + b
        a = a * a_sh
    return a * carry + b


def _scan8_rev(a, b, carry, row):
    for s in (1, 2, 4):
        m = row < SUBLANES - s
        a_sh = jnp.where(m, pltpu.roll(a, SUBLANES - s, axis=0), 1.0)
        b_sh = jnp.where(m, pltpu.roll(b, SUBLANES - s, axis=0), 0.0)
        b = a * b_sh + b
        a = a * a_sh
    return a * carry + b


def _cast_into_full(w, kind, idx, name):
    nl, r, c = w.shape
    tr = 256 if r % 256 == 0 else r
    nrb = r // tr

    def body(idx_ref, w_ref, o0_ref, o1_ref):
        o0_ref[...] = w_ref[0].astype(BF16)
        o1_ref[...] = w_ref[1].astype(BF16)

    if COL_SHARDED[kind]:
        full = (r, N_CHIP * c)
        o_spec = pl.BlockSpec((tr, c), lambda i, idx_ref: (i, idx_ref[1]))
    else:
        full = (N_CHIP * r, c)
        o_spec = pl.BlockSpec((tr, c), lambda i, idx_ref: (idx_ref[1] * nrb + i, 0))
    return pl.pallas_call(
        body, name=name,
        grid_spec=pltpu.PrefetchScalarGridSpec(
            num_scalar_prefetch=1, grid=(nrb,),
            in_specs=[pl.BlockSpec((nl, tr, c), lambda i, idx_ref: (0, i, 0))], out_specs=[o_spec, o_spec]),
        out_shape=[jax.ShapeDtypeStruct(full, BF16)] * 2,
        compiler_params=_cp("parallel"),
    )(idx, w)


def _norm_mm(x, g, w, name, planes=False, tm=512, tn=512, comm=None):
    t_len, d = x.shape
    n = w.shape[1]
    half = n // 2

    def body(x_ref, g_ref, w_ref, z_ref, h_ref):
        xv = x_ref[...]
        r = lax.rsqrt(jnp.mean(xv * xv, axis=-1, keepdims=True) + EPS)
        h_ref[...] = ((xv * r) * g_ref[...]).astype(BF16)
        for n0 in range(0, n, tn):
            blk = jnp.dot(h_ref[...], w_ref[:, n0:n0 + tn], preferred_element_type=F32).astype(BF16)
            if planes:
                z_ref[n0 // half, :, n0 % half:n0 % half + tn] = blk
            else:
                z_ref[:, n0:n0 + tn] = blk

    if planes:
        z_shape = jax.ShapeDtypeStruct((2, t_len, half), BF16)
        z_spec = pl.BlockSpec((2, tm, half), lambda i: (0, i, 0))
    else:
        z_shape = jax.ShapeDtypeStruct((t_len, n), BF16)
        z_spec = pl.BlockSpec((tm, n), lambda i: (i, 0))
    return _pallas(
        body, name=name, grid=(t_len // tm,),
        in_specs=[pl.BlockSpec((tm, d), lambda i: (i, 0)),
                  pl.BlockSpec((1, d), lambda i: (0, 0)),
                  pl.BlockSpec((d, n), lambda i: (0, 0), pipeline_mode=ONCE)],
        out_specs=[z_spec, pl.BlockSpec((tm, d), lambda i: (i, 0))],
        out_shape=[z_shape, jax.ShapeDtypeStruct((t_len, d), BF16)],
        args=(x, g, w), sem=("parallel",), comm=comm)


def _mm_res(a, w, res, name, tm=512, comm=None):
    t_len, k = a.shape
    n = w.shape[1]

    def body(a_ref, w_ref, r_ref, o_ref):
        o_ref[...] = r_ref[...] + jnp.dot(a_ref[...], w_ref[...], preferred_element_type=F32)

    return _pallas(
        body, name=name, grid=(t_len // tm,),
        in_specs=[pl.BlockSpec((tm, k), lambda i: (i, 0)),
                  pl.BlockSpec((k, n), lambda i: (0, 0), pipeline_mode=ONCE),
                  pl.BlockSpec((tm, n), lambda i: (i, 0))],
        out_specs=[pl.BlockSpec((tm, n), lambda i: (i, 0))],
        out_shape=[jax.ShapeDtypeStruct((t_len, n), F32)],
        args=(a, w, res), sem=("parallel",), comm=comm)


def _mm_nt(a, w, name, tm=512, comm=None):
    t_len, k = a.shape
    n = w.shape[0]

    def body(a_ref, w_ref, o_ref):
        o_ref[...] = lax.dot_general(a_ref[...], w_ref[...], (((1,), (1,)), ((), ())),
                                     preferred_element_type=F32).astype(BF16)

    return _pallas(
        body, name=name, grid=(t_len // tm,),
        in_specs=[pl.BlockSpec((tm, k), lambda i: (i, 0)),
                  pl.BlockSpec((n, k), lambda i: (0, 0), pipeline_mode=ONCE)],
        out_specs=[pl.BlockSpec((tm, n), lambda i: (i, 0))],
        out_shape=[jax.ShapeDtypeStruct((t_len, n), BF16)],
        args=(a, w), sem=("parallel",), comm=comm)


def _mm_nt_normbwd(dz, w, x, g, dres, name, planes=False, tm=512, comm=None):
    t_len, d = x.shape
    n = w.shape[1]
    half = n // 2
    nt_dims = (((1,), (1,)), ((), ()))

    def body(dz_ref, w_ref, x_ref, g_ref, r_ref, dx_ref, dxb_ref, dg_ref):
        @pl.when(pl.program_id(0) == 0)
        def _():
            dg_ref[...] = jnp.zeros_like(dg_ref)

        if planes:
            dh = (lax.dot_general(dz_ref[0], w_ref[:, 0:half], nt_dims, preferred_element_type=F32)
                  + lax.dot_general(dz_ref[1], w_ref[:, half:], nt_dims, preferred_element_type=F32))
        else:
            dh = lax.dot_general(dz_ref[...], w_ref[...], nt_dims, preferred_element_type=F32)
        xv = x_ref[...]
        r = lax.rsqrt(jnp.mean(xv * xv, axis=-1, keepdims=True) + EPS)
        xh = xv * r
        dhg = dh * g_ref[...]
        dx = r_ref[...] + r * (dhg - xh * jnp.mean(dhg * xh, axis=-1, keepdims=True))
        dx_ref[...] = dx
        dxb_ref[...] = dx.astype(BF16)
        dg_ref[0:1, :] += jnp.sum(dh * xh, axis=0, keepdims=True)

    if planes:
        dz_spec = pl.BlockSpec((2, tm, half), lambda i: (0, i, 0))
    else:
        dz_spec = pl.BlockSpec((tm, n), lambda i: (i, 0))
    return _pallas(
        body, name=name, grid=(t_len // tm,),
        in_specs=[dz_spec,
                  pl.BlockSpec((d, n), lambda i: (0, 0), pipeline_mode=ONCE),
                  pl.BlockSpec((tm, d), lambda i: (i, 0)),
                  pl.BlockSpec((1, d), lambda i: (0, 0)),
                  pl.BlockSpec((tm, d), lambda i: (i, 0))],
        out_specs=[pl.BlockSpec((tm, d), lambda i: (i, 0)),
                   pl.BlockSpec((tm, d), lambda i: (i, 0)),
                   pl.BlockSpec((SUBLANES, d), lambda i: (0, 0))],
        out_shape=[jax.ShapeDtypeStruct((t_len, d), F32),
                   jax.ShapeDtypeStruct((t_len, d), BF16),
                   jax.ShapeDtypeStruct((SUBLANES, d), F32)],
        args=(dz, w, x, g, dres), sem=("arbitrary",), comm=comm)


def _mm_tn(a, g, name, tk, tn, planes=False, tt=1024, comm=None):
    t_len, k = a.shape
    n = 2 * g.shape[2] if planes else g.shape[1]
    nn = n // tn
    half = nn // 2
    tt = min(tt, t_len)

    def body(a_ref, g_ref, o_ref):
        @pl.when(pl.program_id(2) == 0)
        def _():
            o_ref[...] = jnp.zeros_like(o_ref)

        o_ref[...] += lax.dot_general(a_ref[...], g_ref[...], (((0,), (0,)), ((), ())),
                                      preferred_element_type=F32)

    if planes:
        g_spec = pl.BlockSpec((None, tt, tn), lambda i, j, t: (j // half, t, j % half))
    else:
        g_spec = pl.BlockSpec((tt, tn), lambda i, j, t: (t, j))
    return _pallas(
        body, name=name, grid=(k // tk, nn, t_len // tt),
        in_specs=[pl.BlockSpec((tt, tk), lambda i, j, t: (t, i)), g_spec],
        out_specs=[pl.BlockSpec((tk, tn), lambda i, j, t: (i, j))],
        out_shape=[jax.ShapeDtypeStruct((k, n), F32)],
        args=(a, g), sem=("parallel", "parallel", "arbitrary"), comm=comm)


def _lru_gates(rp, ip, spn):
    r = _sigmoid(rp)
    i = _sigmoid_tanh(ip)
    la = r * spn
    a = jnp.exp(la)
    mult = jnp.sqrt(_neg_expm1(2.0 * la, a))
    return r, i, a, mult


def _mixer_fwd(z, cw8, vec8, wa_bd, wx_bd, scw8, name, tb=256, comm=None):
    t_len = z.shape[0]

    def body(z_ref, cw_ref, vec_ref, wa_ref, wx_ref, scw_ref, y_ref, h_ref,
             xhalo, phalo, hcar, lx_s, rp_s, ip_s):
        @pl.when(pl.program_id(0) == 0)
        def _():
            xhalo[...] = jnp.zeros_like(xhalo)
            phalo[...] = jnp.zeros_like(phalo)
            hcar[...] = jnp.zeros_like(hcar)

        cw = cw_ref[...]
        vec = vec_ref[...]
        xp = z_ref[:, 0:D_LRU].astype(F32)
        ext = jnp.concatenate([xhalo[...], xp], axis=0)
        lx = vec[0:1] + _conv_taps(ext, [cw[k:k + 1] for k in range(4)], tb)
        xhalo[...] = xp[tb - SUBLANES:]
        lx_s[...] = lx
        lxb = lx.astype(BF16)
        for q in range(4):
            sl = slice(q * 256, (q + 1) * 256)
            rp_s[:, sl] = jnp.dot(lxb[:, sl], wa_ref[q], preferred_element_type=F32) + vec[1:2, sl]
            ip_s[:, sl] = jnp.dot(lxb[:, sl], wx_ref[q], preferred_element_type=F32) + vec[2:3, sl]

        spn = jnp.broadcast_to(-RG_C * _softplus_neg(vec[3:4]), (SUBLANES, D_LRU))
        row = lax.broadcasted_iota(jnp.int32, (SUBLANES, D_LRU), 0)

        def step(ci, carry):
            o = pl.multiple_of(ci * PACKED, PACKED)
            gate = z_ref[pl.ds(o, PACKED), D_LRU:2 * D_LRU].astype(F32)
            ys = []
            for sub in range(2):
                rows = pl.ds(pl.multiple_of(o + sub * SUBLANES, SUBLANES), SUBLANES)
                lxv = lx_s[rows, :]
                _, i, a, mult = _lru_gates(rp_s[rows, :], ip_s[rows, :], spn)
                h = _scan8(a, mult * (i * lxv), carry, row)
                h_ref[rows, :] = h
                ys.append(h * _gelu(gate[sub * SUBLANES:(sub + 1) * SUBLANES]))
                carry = jnp.broadcast_to(h[SUBLANES - 1:SUBLANES, :], (SUBLANES, D_LRU))
            y_ref[pl.ds(o, PACKED), 0:D_LRU] = jnp.concatenate(ys, axis=0).astype(BF16)
            return carry

        hcar[...] = lax.fori_loop(0, tb // PACKED, step, hcar[...])

        scw = scw_ref[...]
        o_b, o_c, o_x = 2 * D_LRU, 2 * D_LRU + D_SC, 2 * D_LRU + 2 * D_SC
        p = z_ref[:, o_c:o_x].astype(F32) * z_ref[:, o_x:].astype(F32)
        pext = jnp.concatenate([phalo[...], p], axis=0)
        q = _conv_taps(pext, [scw[k:k + 1] for k in range(3)], tb)
        phalo[...] = p[tb - SUBLANES:]
        y_ref[:, D_LRU:] = (z_ref[:, o_b:o_c].astype(F32) * q).astype(BF16)

    const = lambda t: (0, 0)
    return _pallas(
        body, name=name, grid=(t_len // tb,),
        in_specs=[pl.BlockSpec((tb, D_IN), lambda t: (t, 0)),
                  pl.BlockSpec((SUBLANES, D_LRU), const),
                  pl.BlockSpec((SUBLANES, D_LRU), const),
                  pl.BlockSpec((4, 256, 256), lambda t: (0, 0, 0)),
                  pl.BlockSpec((4, 256, 256), lambda t: (0, 0, 0)),
                  pl.BlockSpec((SUBLANES, D_SC), const)],
        out_specs=[pl.BlockSpec((tb, D_MIX), lambda t: (t, 0)),
                   pl.BlockSpec((tb, D_LRU), lambda t: (t, 0))],
        out_shape=[jax.ShapeDtypeStruct((t_len, D_MIX), BF16),
                   jax.ShapeDtypeStruct((t_len, D_LRU), F32)],
        scratch_shapes=[pltpu.VMEM((SUBLANES, D_LRU), F32), pltpu.VMEM((SUBLANES, D_SC), F32),
                        pltpu.VMEM((SUBLANES, D_LRU), F32), pltpu.VMEM((tb, D_LRU), F32),
                        pltpu.VMEM((tb, D_LRU), F32), pltpu.VMEM((tb, D_LRU), F32)],
        args=(z, cw8, vec8, wa_bd, wx_bd, scw8), sem=("arbitrary",), comm=comm)


def _mixer_bwd(z, h, dy, cw8, vec8, wa_bd, wx_bd, scw8, name, tb=256, comm=None):
    t_len = z.shape[0]
    nb = t_len // tb

    def body(z_ref, zh_ref, h_ref, hh_ref, dy_ref, cw_ref, vec_ref, wa_ref, wx_ref, scw_ref,
             dz_ref, dcw_ref, dvec_ref, dwa_ref, dwx_ref, dscw_ref,
             lx_s, rp_s, ip_s, drpb_s, dipb_s, dlx_s, hext_s, acc_s, acar, gcar, dqh):
        t = pl.program_id(0)
        first_block = t == nb - 1

        @pl.when(t == 0)
        def _():
            for ref in (dcw_ref, dvec_ref, dwa_ref, dwx_ref, dscw_ref, acc_s, acar, gcar, dqh):
                ref[...] = jnp.zeros_like(ref)
            dlx_s[tb:, :] = jnp.zeros((SUBLANES, D_LRU), F32)

        cw = cw_ref[...]
        vec = vec_ref[...]
        scw = scw_ref[...]
        ctaps = [cw[k:k + 1] for k in range(4)]
        staps = [scw[k:k + 1] for k in range(3)]
        keep = jnp.where(first_block, 0.0, 1.0)
        zh = zh_ref[...].astype(F32)[PACKED - SUBLANES:] * keep

        xp = z_ref[:, 0:D_LRU].astype(F32)
        xext = jnp.concatenate([zh[:, 0:D_LRU], xp], axis=0)
        lx = vec[0:1] + _conv_taps(xext, ctaps, tb)
        lx_s[...] = lx
        lxb = lx.astype(BF16)
        for q in range(4):
            sl = slice(q * 256, (q + 1) * 256)
            rp_s[:, sl] = jnp.dot(lxb[:, sl], wa_ref[q], preferred_element_type=F32) + vec[1:2, sl]
            ip_s[:, sl] = jnp.dot(lxb[:, sl], wx_ref[q], preferred_element_type=F32) + vec[2:3, sl]
        hext_s[0:SUBLANES, :] = hh_ref[...] * keep
        hext_s[SUBLANES:, :] = h_ref[...]

        spn = jnp.broadcast_to(-RG_C * _softplus_neg(vec[3:4]), (SUBLANES, D_LRU))
        row = lax.broadcasted_iota(jnp.int32, (SUBLANES, D_LRU), 0)

        def step(ci, carry):
            a_next, g_next = carry
            o = pl.multiple_of((tb // PACKED - 1 - ci) * PACKED, PACKED)
            rows16 = pl.ds(o, PACKED)
            gate16 = z_ref[rows16, D_LRU:2 * D_LRU].astype(F32)
            dyl16 = dy_ref[rows16, 0:D_LRU].astype(F32)
            dgs, drs, dis = [None, None], [None, None], [None, None]
            for sub in (1, 0):
                oo = pl.multiple_of(o + sub * SUBLANES, SUBLANES)
                rows = pl.ds(oo, SUBLANES)
                half = slice(sub * SUBLANES, (sub + 1) * SUBLANES)
                lxv = lx_s[rows, :]
                r, i, a, mult = _lru_gates(rp_s[rows, :], ip_s[rows, :], spn)
                hwin = hext_s[pl.ds(oo, 2 * SUBLANES), :]
                hv = hwin[SUBLANES:]
                hprev = pltpu.roll(hwin, 1, axis=0)[SUBLANES:]
                gel, dgel = _gelu_parts(gate16[half])
                dyl = dyl16[half]
                a_up = jnp.where(row < SUBLANES - 1, pltpu.roll(a, SUBLANES - 1, axis=0), a_next)
                gg = _scan8_rev(a_up, dyl * gel, g_next, row)
                dgs[sub] = dyl * hv * dgel
                ilx = i * lxv
                dla = gg * hprev * a - (gg * ilx) * (a * a) / mult
                dlx_s[rows, :] = gg * mult * i
                drp = dla * spn * r * (1.0 - r)
                dip = gg * mult * lxv * i * (1.0 - i)
                drs[sub] = drp
                dis[sub] = dip
                acc_s[0] += drp
                acc_s[1] += dip
                acc_s[2] += dla * r
                a_next = jnp.broadcast_to(a[0:1, :], (SUBLANES, D_LRU))
                g_next = jnp.broadcast_to(gg[0:1, :], (SUBLANES, D_LRU))
            dz_ref[rows16, D_LRU:2 * D_LRU] = jnp.concatenate(dgs, axis=0).astype(BF16)
            drpb_s[rows16, :] = jnp.concatenate(drs, axis=0).astype(BF16)
            dipb_s[rows16, :] = jnp.concatenate(dis, axis=0).astype(BF16)
            return a_next, g_next

        a_c, g_c = lax.fori_loop(0, tb // PACKED, step, (acar[...], gcar[...]))
        acar[...] = a_c
        gcar[...] = g_c

        drpb = drpb_s[...]
        dipb = dipb_s[...]
        nt_dims = (((1,), (1,)), ((), ()))
        tn_dims = (((0,), (0,)), ((), ()))
        for q in range(4):
            sl = slice(q * 256, (q + 1) * 256)
            dlx_s[0:tb, sl] += (
                lax.dot_general(drpb[:, sl], wa_ref[q], nt_dims, preferred_element_type=F32)
                + lax.dot_general(dipb[:, sl], wx_ref[q], nt_dims, preferred_element_type=F32))
            dwa_ref[q] += lax.dot_general(lxb[:, sl], drpb[:, sl], tn_dims, preferred_element_type=F32)
            dwx_ref[q] += lax.dot_general(lxb[:, sl], dipb[:, sl], tn_dims, preferred_element_type=F32)

        dlx_ext = dlx_s[...]
        dlx = dlx_ext[0:tb]
        dz_ref[:, 0:D_LRU] = _conv_taps_t(dlx_ext, ctaps, tb).astype(BF16)
        dcw_ref[3:4, :] += jnp.sum(dlx * xp, axis=0, keepdims=True)
        for k in range(3):
            shifted = pltpu.roll(xext, 3 - k, axis=0)[SUBLANES:]
            dcw_ref[k:k + 1, :] += jnp.sum(dlx * shifted, axis=0, keepdims=True)
        dvec_ref[0:1, :] += jnp.sum(dlx, axis=0, keepdims=True)
        dlx_s[tb:, :] = dlx[0:SUBLANES]

        o_b, o_c, o_x = 2 * D_LRU, 2 * D_LRU + D_SC, 2 * D_LRU + 2 * D_SC
        sb = z_ref[:, o_b:o_c].astype(F32)
        scc = z_ref[:, o_c:o_x].astype(F32)
        sx = z_ref[:, o_x:].astype(F32)
        p = scc * sx
        pext = jnp.concatenate([zh[:, o_c:o_x] * zh[:, o_x:], p], axis=0)
        q = _conv_taps(pext, staps, tb)
        dys = dy_ref[:, D_LRU:].astype(F32)
        dq = dys * sb
        dp = _conv_taps_t(jnp.concatenate([dq, dqh[...]], axis=0), staps, tb)
        dscw_ref[2:3, :] += jnp.sum(dq * p, axis=0, keepdims=True)
        for k in range(2):
            shifted = pltpu.roll(pext, 2 - k, axis=0)[SUBLANES:]
            dscw_ref[k:k + 1, :] += jnp.sum(dq * shifted, axis=0, keepdims=True)
        dqh[...] = dq[0:SUBLANES]
        dz_ref[:, o_b:o_c] = (dys * q).astype(BF16)
        dz_ref[:, o_c:o_x] = (dp * sx).astype(BF16)
        dz_ref[:, o_x:] = (dp * scc).astype(BF16)

        @pl.when(first_block)
        def _():
            dvec_ref[1:2, :] = jnp.sum(acc_s[0], axis=0, keepdims=True)
            dvec_ref[2:3, :] = jnp.sum(acc_s[1], axis=0, keepdims=True)
            dvec_ref[3:4, :] = (jnp.sum(acc_s[2], axis=0, keepdims=True) * RG_C * _sigmoid(-vec[3:4]))

    blk = lambda t: (nb - 1 - t, 0)
    halo8 = lambda t: (jnp.maximum((nb - 1 - t) * (tb // SUBLANES) - 1, 0), 0)
    halo16 = lambda t: (jnp.maximum((nb - 1 - t) * (tb // PACKED) - 1, 0), 0)
    const = lambda t: (0, 0)
    const3 = lambda t: (0, 0, 0)
    return _pallas(
        body, name=name, grid=(nb,),
        in_specs=[pl.BlockSpec((tb, D_IN), blk), pl.BlockSpec((PACKED, D_IN), halo16),
                  pl.BlockSpec((tb, D_LRU), blk), pl.BlockSpec((SUBLANES, D_LRU), halo8),
                  pl.BlockSpec((tb, D_MIX), blk),
                  pl.BlockSpec((SUBLANES, D_LRU), const), pl.BlockSpec((SUBLANES, D_LRU), const),
                  pl.BlockSpec((4, 256, 256), const3), pl.BlockSpec((4, 256, 256), const3),
                  pl.BlockSpec((SUBLANES, D_SC), const)],
        out_specs=[pl.BlockSpec((tb, D_IN), blk),
                   pl.BlockSpec((SUBLANES, D_LRU), const), pl.BlockSpec((SUBLANES, D_LRU), const),
                   pl.BlockSpec((4, 256, 256), const3), pl.BlockSpec((4, 256, 256), const3),
                   pl.BlockSpec((SUBLANES, D_SC), const)],
        out_shape=[jax.ShapeDtypeStruct((t_len, D_IN), BF16),
                   jax.ShapeDtypeStruct((SUBLANES, D_LRU), F32), jax.ShapeDtypeStruct((SUBLANES, D_LRU), F32),
                   jax.ShapeDtypeStruct((4, 256, 256), F32), jax.ShapeDtypeStruct((4, 256, 256), F32),
                   jax.ShapeDtypeStruct((SUBLANES, D_SC), F32)],
        scratch_shapes=[pltpu.VMEM((tb, D_LRU), F32),
                        pltpu.VMEM((tb, D_LRU), F32), pltpu.VMEM((tb, D_LRU), F32),
                        pltpu.VMEM((tb, D_LRU), BF16), pltpu.VMEM((tb, D_LRU), BF16),
                        pltpu.VMEM((tb + SUBLANES, D_LRU), F32), pltpu.VMEM((tb + SUBLANES, D_LRU), F32),
                        pltpu.VMEM((3, SUBLANES, D_LRU), F32),
                        pltpu.VMEM((SUBLANES, D_LRU), F32), pltpu.VMEM((SUBLANES, D_LRU), F32),
                        pltpu.VMEM((SUBLANES, D_SC), F32)],
        args=(z, z, h, h, dy, cw8, vec8, wa_bd, wx_bd, scw8), sem=("arbitrary",), comm=comm)


def _ffn_act(u, fw, name, tb=256, tn=512, rc=64, comm=None):
    t_len = u.shape[1]
    hb = tb // PACKED

    def body(u_ref, uh_ref, fw_ref, o_ref, ext):
        keep = jnp.where(pl.program_id(0) == 0, 0.0, 1.0)
        ext[:, 0:SUBLANES, :] = uh_ref[...].astype(F32)[:, PACKED - SUBLANES:, :] * keep
        ext[:, SUBLANES:, :] = u_ref[...].astype(F32)
        fw_v = fw_ref[...]

        for lb in range(tn // LANES):
            lanes = slice(lb * LANES, (lb + 1) * LANES)
            wg = [fw_v[0, k:k + 1, lanes] for k in range(3)]
            wu = [fw_v[1, k:k + 1, lanes] for k in range(3)]

            def chunk(ci, c, lanes=lanes, wg=wg, wu=wu):
                o = pl.multiple_of(ci * rc, rc)
                win = pl.ds(o, rc + SUBLANES)
                gate = _conv_taps(ext[0, win, lanes], wg, rc)
                up = _conv_taps(ext[1, win, lanes], wu, rc)
                o_ref[pl.ds(o, rc), lanes] = (_gelu(gate) * up).astype(BF16)
                return c

            lax.fori_loop(0, tb // rc, chunk, 0)

    return _pallas(
        body, name=name, grid=(t_len // tb, D_FF // tn),
        in_specs=[pl.BlockSpec((2, tb, tn), lambda i, j: (0, i, j)),
                  pl.BlockSpec((2, PACKED, tn), lambda i, j: (0, jnp.maximum(i * hb - 1, 0), j)),
                  pl.BlockSpec((2, SUBLANES, tn), lambda i, j: (0, 0, j))],
        out_specs=[pl.BlockSpec((tb, tn), lambda i, j: (i, j))],
        out_shape=[jax.ShapeDtypeStruct((t_len, D_FF), BF16)],
        scratch_shapes=[pltpu.VMEM((2, tb + SUBLANES, tn), F32)],
        args=(u, u, fw), sem=("parallel", "parallel"), comm=comm)


def _ffn_bwd(dact, u, fw, name, tb=256, tn=512, rc=32, comm=None):
    t_len = u.shape[1]
    ni = t_len // tb
    hb = tb // PACKED
    last_halo = t_len // PACKED - 1

    def body(d_ref, dn_ref, u_ref, up_ref, un_ref, fw_ref, du_ref, dfw_ref, extu, extd, acc):
        i = pl.program_id(1)

        @pl.when(i == 0)
        def _():
            acc[...] = jnp.zeros_like(acc)

        keep_prev = jnp.where(i == 0, 0.0, 1.0)
        keep_next = jnp.where(i == ni - 1, 0.0, 1.0)
        extu[:, 0:SUBLANES, :] = up_ref[...].astype(F32)[:, PACKED - SUBLANES:, :] * keep_prev
        extu[:, SUBLANES:SUBLANES + tb, :] = u_ref[...].astype(F32)
        extu[:, SUBLANES + tb:, :] = un_ref[...].astype(F32)[:, 0:SUBLANES, :]
        extd[0:tb, :] = d_ref[...].astype(F32)
        extd[tb:, :] = dn_ref[...].astype(F32)[0:SUBLANES] * keep_next
        fw_v = fw_ref[...]
        m = rc + SUBLANES

        for lb in range(tn // LANES):
            lanes = slice(lb * LANES, (lb + 1) * LANES)
            taps = [[fw_v[pln, k:k + 1, lanes] for k in range(3)] for pln in range(2)]

            def chunk(ci, c, lanes=lanes, taps=taps):
                o = pl.multiple_of(ci * rc, rc)
                win = pl.ds(o, rc + 2 * SUBLANES)
                sh = []
                for pln in range(2):
                    e = extu[pln, win, lanes]
                    sh.append([pltpu.roll(e, 2, axis=0)[SUBLANES:], pltpu.roll(e, 1, axis=0)[SUBLANES:],
                               e[SUBLANES:]])
                gate = sum(taps[0][k] * sh[0][k] for k in range(3))
                up = sum(taps[1][k] * sh[1][k] for k in range(3))
                dv = extd[pl.ds(o, m), lanes]
                gel, dgel = _gelu_parts(gate)
                dpost = [dv * up * dgel, dv * gel]
                for pln in range(2):
                    du_ref[pln, pl.ds(o, rc), lanes] = _conv_taps_t(dpost[pln], taps[pln], rc).astype(BF16)
                    for k in range(3):
                        prod = dpost[pln][0:rc] * sh[pln][k][0:rc]
                        acc[3 * pln + k, :, lanes] += sum(
                            prod[s:s + SUBLANES] for s in range(0, rc, SUBLANES))
                return c

            lax.fori_loop(0, tb // rc, chunk, 0)

        @pl.when(i == ni - 1)
        def _():
            dfw_ref[...] = jnp.zeros_like(dfw_ref)
            for pln in range(2):
                for k in range(3):
                    dfw_ref[pln, k:k + 1, :] = jnp.sum(acc[3 * pln + k], axis=0, keepdims=True)

    return _pallas(
        body, name=name, grid=(D_FF // tn, ni),
        in_specs=[pl.BlockSpec((tb, tn), lambda j, i: (i, j)),
                  pl.BlockSpec((PACKED, tn), lambda j, i: (jnp.minimum((i + 1) * hb, last_halo), j)),
                  pl.BlockSpec((2, tb, tn), lambda j, i: (0, i, j)),
                  pl.BlockSpec((2, PACKED, tn), lambda j, i: (0, jnp.maximum(i * hb - 1, 0), j)),
                  pl.BlockSpec((2, PACKED, tn), lambda j, i: (0, jnp.minimum((i + 1) * hb, last_halo), j)),
                  pl.BlockSpec((2, SUBLANES, tn), lambda j, i: (0, 0, j))],
        out_specs=[pl.BlockSpec((2, tb, tn), lambda j, i: (0, i, j)),
                   pl.BlockSpec((2, SUBLANES, tn), lambda j, i: (0, 0, j))],
        out_shape=[jax.ShapeDtypeStruct((2, t_len, D_FF), BF16),
                   jax.ShapeDtypeStruct((2, SUBLANES, D_FF), F32)],
        scratch_shapes=[pltpu.VMEM((2, tb + 2 * SUBLANES, tn), F32),
                        pltpu.VMEM((tb + SUBLANES, tn), F32),
                        pltpu.VMEM((6, SUBLANES, tn), F32)],
        args=(dact, dact, u, u, u, fw), sem=("parallel", "arbitrary"), comm=comm)


def _loss_head(x, g, target, name, tb=256):
    t_len, d = x.shape

    def body(x_ref, g_ref, t_ref, dx_ref, dxb_ref, dg_ref, loss_ref):
        @pl.when(pl.program_id(0) == 0)
        def _():
            dg_ref[...] = jnp.zeros_like(dg_ref)
            loss_ref[...] = jnp.zeros_like(loss_ref)

        xv = x_ref[...]
        gv = g_ref[...]
        r = lax.rsqrt(jnp.mean(xv * xv, axis=-1, keepdims=True) + EPS)
        xh = xv * r
        err = xh * gv - t_ref[...]
        loss_ref[...] += (0.5 / d) * jnp.sum(jnp.sum(err * err, axis=-1, keepdims=True), axis=0, keepdims=True)
        dy = err * (1.0 / d)
        dyg = dy * gv
        dx = r * (dyg - xh * jnp.mean(dyg * xh, axis=-1, keepdims=True))
        dx_ref[...] = dx
        dxb_ref[...] = dx.astype(BF16)
        dg_ref[0:1, :] += jnp.sum(dy * xh, axis=0, keepdims=True)

    return _pallas(
        body, name=name, grid=(t_len // tb,),
        in_specs=[pl.BlockSpec((tb, d), lambda i: (i, 0)), pl.BlockSpec((1, d), lambda i: (0, 0)),
                  pl.BlockSpec((tb, d), lambda i: (i, 0))],
        out_specs=[pl.BlockSpec((tb, d), lambda i: (i, 0)), pl.BlockSpec((tb, d), lambda i: (i, 0)),
                   pl.BlockSpec((SUBLANES, d), lambda i: (0, 0)),
                   pl.BlockSpec((SUBLANES, LANES), lambda i: (0, 0))],
        out_shape=[jax.ShapeDtypeStruct((t_len, d), F32), jax.ShapeDtypeStruct((t_len, d), BF16),
                   jax.ShapeDtypeStruct((SUBLANES, d), F32), jax.ShapeDtypeStruct((SUBLANES, LANES), F32)],
        args=(x, g, target), sem=("arbitrary",))[0]


def _adamw(w, g, m, v, name):
    r, c = w.shape
    tr = 256 if r % 256 == 0 else r
    c1 = 1.0 / (1.0 - ADAM_B1 ** ADAM_STEP)
    c2 = 1.0 / (1.0 - ADAM_B2 ** ADAM_STEP)

    def body(w_ref, g_ref, m_ref, v_ref, d_ref, mo_ref, vo_ref):
        gv = g_ref[...]
        mn = ADAM_B1 * m_ref[...] + (1.0 - ADAM_B1) * gv
        vn = ADAM_B2 * v_ref[...] + (1.0 - ADAM_B2) * (gv * gv)
        d_ref[...] = -ADAM_LR * ((mn * c1) / (jnp.sqrt(vn * c2) + ADAM_EPS) + ADAM_WD * w_ref[...])
        mo_ref[...] = mn
        vo_ref[...] = vn

    spec = pl.BlockSpec((tr, c), lambda i: (i, 0))
    shape = jax.ShapeDtypeStruct((r, c), F32)
    return _pallas(
        body, name=name, grid=(r // tr,),
        in_specs=[spec] * 4, out_specs=[spec] * 3, out_shape=[shape] * 3,
        args=(w, g, m, v), sem=("parallel",))[0]


def _place():
    x, y, c = lax.axis_index("x"), lax.axis_index("y"), lax.axis_index("c")
    chips = [(1 - x, y), (x, 1 - y), (1 - x, 1 - y)]
    return x, y, c, chips


def _remote(src, dst, send, recv, sem, to):
    return pltpu.make_async_remote_copy(
        src_ref=src, dst_ref=dst, send_sem=send.at[sem], recv_sem=recv.at[sem], device_id=to, device_id_type=MESH)


def _gather_plan(fulls, kinds, mid_at=None):
    def region(f, kind, k, cc):
        if kind == SMALL:
            return f.at[k, pl.ds(cc * (CONV_PACK_ROWS // 2), CONV_PACK_ROWS // 2), :]
        if COL_SHARDED[kind]:
            rows, cols = f.shape[0], f.shape[1] // N_CHIP
            return f.at[pl.ds(cc * (rows // 2), rows // 2), pl.ds(k * cols, cols)]
        rows = f.shape[0] // N_CHIP
        return f.at[pl.ds(k * rows + cc * (rows // 2), rows // 2), :]

    def first_hop(bufs, send, recv, it, j):
        x, y, c, chips = _place()
        reg = region(bufs[it], kinds[it], 2 * x + y, c)
        return _remote(reg, reg, send, recv, it * 6 + j, (*chips[j], c))

    def arrival(bufs, send, recv, it, j, second):
        x, y, c, chips = _place()
        px, py = chips[j]
        reg = region(bufs[it], kinds[it], 2 * px + py, 1 - c if second else c)
        to = (x, y, 1 - c) if second else (px, py, c)
        return _remote(reg, reg, send, recv, it * 6 + (3 + j if second else j), to)

    def forward(bufs, send, recv, it, j):
        x, y, c, chips = _place()
        px, py = chips[j]
        reg = region(bufs[it], kinds[it], 2 * px + py, c)
        return _remote(reg, reg, send, recv, it * 6 + 3 + j, (x, y, 1 - c))

    def start(srcs, bufs, outs, send, recv):
        for it in range(len(bufs)):
            for j in range(3):
                first_hop(bufs, send, recv, it, j).start()

    def mid(srcs, bufs, outs, send, recv):
        for it in range(len(bufs)):
            for j in range(3):
                arrival(bufs, send, recv, it, j, False).wait_recv()
                forward(bufs, send, recv, it, j).start()

    def finish(srcs, bufs, outs, send, recv):
        for it in range(len(bufs)):
            for j in range(3):
                arrival(bufs, send, recv, it, j, True).wait_recv()
        for it in range(len(bufs)):
            for j in range(3):
                first_hop(bufs, send, recv, it, j).wait_send()
                forward(bufs, send, recv, it, j).wait_send()

    return Comm(srcs=(), bufs=tuple(fulls), outs=(), n_sem=6 * len(fulls), start=start, mid=mid, finish=finish,
                mid_at=mid_at)


def _half_axis(kind):
    return 0 if kind == SMALL or COL_SHARDED[kind] else 1


def _half2(ref, kind, cc):
    if _half_axis(kind) == 0:
        return ref.at[pl.ds(cc * (ref.shape[0] // 2), ref.shape[0] // 2), :]
    return ref.at[:, pl.ds(cc * (ref.shape[1] // 2), ref.shape[1] // 2)]


def _pair_plan(grads, kinds):
    def land_shape(g, kind):
        s = list(g.shape)
        s[_half_axis(kind)] //= 2
        return jax.ShapeDtypeStruct(tuple(s), F32)

    def copy(srcs, outs, send, recv, it):
        x, y, c, _ = _place()
        return _remote(_half2(srcs[it], kinds[it], 1 - c), outs[it], send, recv, it, (x, y, 1 - c))

    def start(srcs, bufs, outs, send, recv):
        for it in range(len(srcs)):
            copy(srcs, outs, send, recv, it).start()

    def finish(srcs, bufs, outs, send, recv):
        for it in range(len(srcs)):
            copy(srcs, outs, send, recv, it).wait_send()
        for it in range(len(srcs)):
            copy(srcs, outs, send, recv, it).wait_recv()

    return Comm(srcs=tuple(grads), bufs=(), outs=tuple(land_shape(g, k) for g, k in zip(grads, kinds)),
                n_sem=len(grads), start=start, finish=finish)


def _scatter_plan(parts, slots, kinds):
    def piece(s, kind, k):
        if kind == SMALL:
            return s
        if COL_SHARDED[kind]:
            n = s.shape[1] // N_CHIP
            return s.at[:, pl.ds(k * n, n)]
        n = s.shape[0] // N_CHIP
        return s.at[pl.ds(k * n, n), :]

    def outbound(srcs, bufs, send, recv, it, j):
        x, y, c, chips = _place()
        px, py = chips[j]
        return _remote(piece(srcs[it], kinds[it], 2 * px + py), bufs[it].at[2 * x + y], send, recv, it * 3 + j,
                       (px, py, c))

    def inbound(bufs, send, recv, it, j):
        x, y, c, chips = _place()
        px, py = chips[j]
        got = bufs[it].at[2 * px + py]
        return _remote(got, got, send, recv, it * 3 + j, (px, py, c))

    def start(srcs, bufs, outs, send, recv):
        for it in range(len(srcs)):
            for j in range(3):
                outbound(srcs, bufs, send, recv, it, j).start()

    def finish(srcs, bufs, outs, send, recv):
        for it in range(len(srcs)):
            for j in range(3):
                inbound(bufs, send, recv, it, j).wait_recv()
        for it in range(len(srcs)):
            for j in range(3):
                outbound(srcs, bufs, send, recv, it, j).wait_send()

    return Comm(srcs=tuple(parts), bufs=tuple(slots), outs=(), n_sem=3 * len(parts), start=start, finish=finish)


def _share_plan(fulls, kinds, layer):
    def half(f, kind, cc):
        return _half2(f if kind == SMALL else f.at[layer], kind, cc)

    def copy(bufs, send, recv, it, cc):
        x, y, c, _ = _place()
        reg = half(bufs[it], kinds[it], c if cc == "mine" else 1 - c)
        return _remote(reg, reg, send, recv, it, (x, y, 1 - c))

    def start(srcs, bufs, outs, send, recv):
        for it in range(len(bufs)):
            copy(bufs, send, recv, it, "mine").start()

    def finish(srcs, bufs, outs, send, recv):
        for it in range(len(bufs)):
            copy(bufs, send, recv, it, "other").wait_recv()
        for it in range(len(bufs)):
            copy(bufs, send, recv, it, "mine").wait_send()

    return Comm(srcs=(), bufs=tuple(fulls), outs=(), n_sem=len(fulls), start=start, finish=finish)


def _pair_sum(g, land, idx, kind, name):
    odt = F32 if kind == SMALL else BF16
    r, cdim = land.shape

    def body(idx_ref, g_ref, l_ref, p_ref, s_ref):
        v = (g_ref[...] + l_ref[...]).astype(odt)
        p_ref[...] = v
        if kind == SMALL:
            s_ref[...] = v
        else:
            @pl.when(pl.program_id(1 if COL_SHARDED[kind] else 0) == idx_ref[1])
            def _():
                s_ref[...] = v

    if kind == SMALL:
        grid = (1,)
        g_spec = pl.BlockSpec((r, LANES), lambda i, idx_ref: (idx_ref[0], 0))
        spec = pl.BlockSpec((r, LANES), lambda i, idx_ref: (0, 0))
        s_spec = pl.BlockSpec((None, r, LANES), lambda i, idx_ref: (idx_ref[1], 0, 0))
        s_shape = (N_CHIP, r, LANES)
    elif COL_SHARDED[kind]:
        pc, tr = cdim // N_CHIP, 256
        nrb = r // tr
        grid = (nrb, N_CHIP)
        g_spec = pl.BlockSpec((tr, pc), lambda i, k, idx_ref: (idx_ref[0] * nrb + i, k))
        spec = pl.BlockSpec((tr, pc), lambda i, k, idx_ref: (i, k))
        s_spec = pl.BlockSpec((None, tr, pc), lambda i, k, idx_ref: (idx_ref[1], i, 0))
        s_shape = (N_CHIP, r, pc)
    else:
        pr = r // N_CHIP
        grid = (N_CHIP,)
        g_spec = pl.BlockSpec((pr, cdim), lambda k, idx_ref: (k, idx_ref[0]))
        spec = pl.BlockSpec((pr, cdim), lambda k, idx_ref: (k, 0))
        s_spec = pl.BlockSpec((None, pr, cdim), lambda k, idx_ref: (idx_ref[1], 0, 0))
        s_shape = (N_CHIP, pr, cdim)
    return pl.pallas_call(
        body, name=name,
        grid_spec=pltpu.PrefetchScalarGridSpec(
            num_scalar_prefetch=1, grid=grid, in_specs=[g_spec, spec], out_specs=[spec, s_spec]),
        out_shape=[jax.ShapeDtypeStruct(land.shape, odt), jax.ShapeDtypeStruct(s_shape, odt)],
        compiler_params=_cp(*(["arbitrary"] * len(grid))),
    )(idx, g, land)


def _sum_slots(slots, idx, kind, layer, prev, name):
    _, r, cdim = slots.shape

    def body(*refs):
        s_ref, o_ref = refs[1], refs[-1]
        v = s_ref[...].astype(F32)
        o_ref[...] = (v[0] + v[1]) + (v[2] + v[3])

    if kind == SMALL:
        grid = (1,)
        s_spec = pl.BlockSpec((N_CHIP, r, cdim), lambda i, idx_ref: (0, 0, 0))
        o_spec = pl.BlockSpec((r, cdim), lambda i, idx_ref: (idx_ref[0], 0))
        full = (2 * r, cdim)
    else:
        tr = 256 if r % 256 == 0 else 384
        nrb = r // tr
        grid = (nrb,)
        s_spec = pl.BlockSpec((N_CHIP, tr, cdim), lambda i, idx_ref: (0, i, 0))
        if COL_SHARDED[kind]:
            o_spec = pl.BlockSpec((None, tr, cdim), lambda i, idx_ref: (layer, idx_ref[0] * nrb + i, 0))
            full = (2, 2 * r, cdim)
        else:
            o_spec = pl.BlockSpec((None, tr, cdim), lambda i, idx_ref: (layer, i, idx_ref[0]))
            full = (2, r, 2 * cdim)
    in_specs, args, aliases = [s_spec], [idx, slots], {}
    if prev is not None:
        in_specs.append(ANY)
        args.append(prev)
        aliases = {2: 0}
    return pl.pallas_call(
        body, name=name,
        grid_spec=pltpu.PrefetchScalarGridSpec(
            num_scalar_prefetch=1, grid=grid, in_specs=in_specs, out_specs=o_spec),
        out_shape=jax.ShapeDtypeStruct(full, F32),
        input_output_aliases=aliases,
        compiler_params=_cp(*(["parallel"] * len(grid))),
    )(*args)


def _block_diag(w):
    w4 = w.reshape(4, 4, 64, 64)
    eye = jnp.eye(4, dtype=w.dtype)[None, :, None, :, None]
    return (w4[:, :, :, None, :] * eye).reshape(4, 256, 256)


def _block_diag_extract(d):
    d5 = d.reshape(4, 4, 64, 4, 64)
    return jnp.stack([d5[:, hh, :, hh, :] for hh in range(4)], axis=1).reshape(16, 64, 64)


def _rows8(a):
    return jnp.pad(a, ((0, SUBLANES - a.shape[0]), (0, 0)))


def _pack_rep(norm1_g, conv_b, ba, bx, lam, norm2_g, wa, wx, final_g):
    parts = []
    for l in range(2):
        parts += [norm1_g[l], conv_b[l], ba[l], bx[l], lam[l], norm2_g[l], wa[l].reshape(-1), wx[l].reshape(-1)]
    parts.append(final_g)
    return jnp.concatenate(parts).reshape(REP_ROWS, LANES)


def _unpack_rep(buf):
    flat = buf.reshape(-1)
    out = {k: [] for k in ("norm1_g", "lru_conv_b", "lru_ba", "lru_bx", "lru_lambda", "norm2_g", "lru_wa", "lru_wx")}
    for l in range(2):
        o = l * REP_LAYER
        for i, k in enumerate(("norm1_g", "lru_conv_b", "lru_ba", "lru_bx", "lru_lambda", "norm2_g")):
            out[k].append(flat[o + i * 1024:o + (i + 1) * 1024])
        o += 6 * 1024
        out["lru_wa"].append(flat[o:o + 65536].reshape(16, 64, 64))
        out["lru_wx"].append(flat[o + 65536:o + 131072].reshape(16, 64, 64))
    res = {k: jnp.stack(v) for k, v in out.items()}
    res["final_g"] = flat[2 * REP_LAYER:2 * REP_LAYER + 1024]
    return res


def _pack_conv_shard(lru_cw, sc_cw, ffn_cw):
    return jnp.concatenate([lru_cw.reshape(16, LANES), jnp.pad(sc_cw.reshape(6, LANES), ((0, 2), (0, 0))),
                            ffn_cw.reshape(72, LANES)], axis=0)


def _unpack_conv_shard(buf):
    return (buf[0:16].reshape(2, 4, 256), buf[16:22].reshape(2, 3, 128), buf[24:96].reshape(2, 3, 1536))


def kernel(x, norm1_g, w_in, lru_conv_w, lru_conv_b, lru_wa, lru_ba, lru_wx, lru_bx, lru_lambda, sc_conv_w, w_out, norm2_g, w_up, ffn_conv_w, w_down, final_g, loss_target, m_norm1_g, m_w_in, m_lru_conv_w, m_lru_conv_b, m_lru_wa, m_lru_ba, m_lru_wx, m_lru_bx, m_lru_lambda, m_sc_conv_w, m_w_out, m_norm2_g, m_w_up, m_ffn_conv_w, m_w_down, m_final_g, v_norm1_g, v_w_in, v_lru_conv_w, v_lru_conv_b, v_lru_wa, v_lru_ba, v_lru_wx, v_lru_bx, v_lru_lambda, v_sc_conv_w, v_w_out, v_norm2_g, v_w_up, v_ffn_conv_w, v_w_down, v_final_g):
    me = 2 * lax.axis_index("x") + lax.axis_index("y")
    idx = jnp.stack([lax.axis_index("c"), me]).astype(jnp.int32)
    t_len = x.shape[1]

    s_conv = _pack_conv_shard(lru_conv_w, sc_conv_w, ffn_conv_w)
    conv_slots = lax.dynamic_update_slice(jnp.zeros((N_CHIP, CONV_PACK_ROWS, LANES), F32), s_conv[None], (me, 0, 0))
    wi = list(_cast_into_full(w_in, W_IN, idx, "cast_w_in"))
    wo = list(_cast_into_full(w_out, W_OUT, idx, "cast_w_out"))
    wu = list(_cast_into_full(w_up, W_UP, idx, "cast_w_up"))
    wd = list(_cast_into_full(w_down, W_DOWN, idx, "cast_w_down"))
    wi[0], convs = _comm_call(_gather_plan([wi[0], conv_slots], [W_IN, SMALL]), "ag_first")
    per_chip = [_unpack_conv_shard(convs[k]) for k in range(N_CHIP)]
    lru_cw = jnp.concatenate([p[0] for p in per_chip], axis=-1)
    sc_cw = jnp.concatenate([p[1] for p in per_chip], axis=-1)
    ffn_cw = jnp.concatenate([p[2] for p in per_chip], axis=-1)

    cw8 = [_rows8(lru_cw[l]) for l in range(2)]
    vec8 = [_rows8(jnp.stack([lru_conv_b[l], lru_ba[l], lru_bx[l], lru_lambda[l]])) for l in range(2)]
    wa_bd = [_block_diag(lru_wa[l]).astype(BF16) for l in range(2)]
    wx_bd = [_block_diag(lru_wx[l]).astype(BF16) for l in range(2)]
    scw8 = [_rows8(sc_cw[l]) for l in range(2)]
    fw8 = [jnp.pad(ffn_cw[l].reshape(3, 2, D_FF).transpose(1, 0, 2), ((0, 0), (0, 5), (0, 0))) for l in range(2)]

    xs = x[0]
    saved = []
    for l in range(2):
        first = l == 0
        n512, n256 = t_len // 512, t_len // 256
        comm = _gather_plan([wo[0]], [W_OUT], mid_at=(max(n512 - 2, 0),)) if first else None
        (z, h1), got = _norm_mm(xs, norm1_g[l][None], wi[l], f"fwd_in_{l}", comm=comm)
        if first:
            wo[0], = got
        comm = _gather_plan([wu[0]], [W_UP], mid_at=(max(n256 - 2, 0),)) if first else None
        (ymix, hst), got = _mixer_fwd(z, cw8[l], vec8[l], wa_bd[l], wx_bd[l], scw8[l], f"fwd_mixer_{l}", comm=comm)
        if first:
            wu[0], = got
        (x2,), _ = _mm_res(ymix, wo[l], xs, f"fwd_out_{l}")
        comm = _gather_plan([wd[0], wi[1]], [W_DOWN, W_IN], mid_at=(n512 - 1,)) if first else None
        (u, h2), got = _norm_mm(x2, norm2_g[l][None], wu[l], f"fwd_up_{l}", planes=True, comm=comm)
        if first:
            wd[0], wi[1] = got
        comm = _gather_plan([wo[1], wu[1]], [W_OUT, W_UP], mid_at=(n256 - 1, 0)) if first else None
        (act,), got = _ffn_act(u, fw8[l], f"fwd_act_{l}", comm=comm)
        if first:
            wo[1], wu[1] = got
        comm = _gather_plan([wd[1]], [W_DOWN], mid_at=(n512 - 1,)) if first else None
        (x3,), got = _mm_res(act, wd[l], x2, f"fwd_down_{l}", comm=comm)
        if first:
            wd[1], = got
        saved.append((xs, h1, z, hst, ymix, x2, h2, u, act))
        xs = x3

    dx, dxb, dgf, loss_blk = _loss_head(xs, final_g[None], loss_target[0], "loss_head")
    loss = lax.psum(loss_blk[0, 0], ("x", "y", "c"))

    kinds = [W_IN, W_OUT, W_UP, W_DOWN]
    grads = [None, None]
    small = [None, None]
    reduced = [None] * 4
    lands = parts = slots = None
    for l in (1, 0):
        x_in, h1, z, hst, ymix, x2, h2, u, act = saved[l]
        carry = l == 0
        comm = _pair_plan(grads[1], kinds) if carry else None
        (g_down,), got = _mm_tn(act, dxb, f"bwd_wdown_{l}", tk=1536, tn=1024, comm=comm)
        if carry:
            summed = [_pair_sum(grads[1][w], got[w], idx, kinds[w], f"rs_add1_{w}") for w in range(4)]
            parts, slots = [s[0] for s in summed], [s[1] for s in summed]
        (dact,), _ = _mm_nt(dxb, wd[l], f"bwd_dact_{l}")
        comm = _scatter_plan(parts, slots, kinds) if carry else None
        (du, dfw), got = _ffn_bwd(dact, u, fw8[l], f"bwd_act_{l}", comm=comm)
        if carry:
            reduced = [_sum_slots(got[w], idx, kinds[w], 1, None, f"rs_sum1_{w}") for w in range(4)]
        comm = _share_plan(reduced, kinds, 1) if carry else None
        (g_up,), got = _mm_tn(h2, du, f"bwd_wup_{l}", tk=1024, tn=1536, planes=True, comm=comm)
        if carry:
            reduced = list(got)
        comm = _pair_plan([g_up, g_down], [W_UP, W_DOWN]) if carry else None
        (dx2, dx2b, dg2), got = _mm_nt_normbwd(du, wu[l], x2, norm2_g[l][None], dx, f"bwd_up_{l}", planes=True,
                                               comm=comm)
        if carry:
            sum_up = _pair_sum(g_up, got[0], idx, W_UP, "rs_add0_2")
            sum_down = _pair_sum(g_down, got[1], idx, W_DOWN, "rs_add0_3")
        (g_out,), _ = _mm_tn(ymix, dx2b, f"bwd_wout_{l}", tk=1536, tn=1024)
        comm = _pair_plan([g_out], [W_OUT]) if carry else None
        (dymix,), got = _mm_nt(dx2b, wo[l], f"bwd_dymix_{l}", comm=comm)
        if carry:
            sum_out = _pair_sum(g_out, got[0], idx, W_OUT, "rs_add0_1")
            comm = _scatter_plan([sum_out[0], sum_up[0], sum_down[0]], [sum_out[1], sum_up[1], sum_down[1]],
                                 [W_OUT, W_UP, W_DOWN])
        (dz, dcw, dvec, dwa, dwx, dscw), got = _mixer_bwd(z, hst, dymix, cw8[l], vec8[l], wa_bd[l], wx_bd[l],
                                                        scw8[l], f"bwd_mixer_{l}", comm=comm)
        if carry:
            for w, s in zip((W_OUT, W_UP, W_DOWN), got):
                reduced[w] = _sum_slots(s, idx, w, 0, reduced[w], f"rs_sum0_{w}")
        (g_in,), _ = _mm_tn(h1, dz, f"bwd_win_{l}", tk=1024, tn=1792)
        comm = _pair_plan([g_in], [W_IN]) if carry else None
        (dx, dxb, dg1), got = _mm_nt_normbwd(dz, wi[l], x_in, norm1_g[l][None], dx2, f"bwd_in_{l}", comm=comm)
        if carry:
            sum_in = _pair_sum(g_in, got[0], idx, W_IN, "rs_add0_0")
        grads[l] = [g_in, g_out, g_up, g_down]
        rep = [dg1[0], dvec[0], dvec[1], dvec[2], dvec[3], dg2[0],
               _block_diag_extract(dwa).reshape(-1), _block_diag_extract(dwx).reshape(-1)]
        conv = [dcw[0:4].reshape(-1), jnp.pad(dscw[0:3].reshape(-1), (0, 512)),
                dfw[:, 0:3, :].transpose(1, 0, 2).reshape(-1)]
        small[l] = (jnp.concatenate(rep), jnp.concatenate(conv))
    grad_x = dx[None]
    g_small = jnp.concatenate([small[0][0], small[1][0], dgf[0], small[0][1], small[1][1],
                               jnp.zeros((8 * LANES,), F32)]).reshape(SMALL_ROWS, LANES)

    land_small, = _comm_call(_pair_plan([g_small], [SMALL]), "rs_pair_small")
    sum_small = _pair_sum(g_small, land_small, idx, SMALL, "rs_add0_4")
    slot_in, slot_small = _comm_call(
        _scatter_plan([sum_in[0], sum_small[0]], [sum_in[1], sum_small[1]], [W_IN, SMALL]), "rs_scatter_last")
    reduced[W_IN] = _sum_slots(slot_in, idx, W_IN, 0, reduced[W_IN], "rs_sum0_0")
    reduced.append(_sum_slots(slot_small, idx, SMALL, 0, None, "rs_sum0_4"))
    gw_in, gw_out, gw_up, gw_down, gs = _comm_call(_share_plan(reduced, kinds + [SMALL], 0), "rs_share0")

    g_rep = gs[0:REP_ROWS]
    g_conv = gs[REP_ROWS:REP_ROWS + CONV_ROWS].reshape(2, CONV_LAYER)
    g_lru_cw = lax.dynamic_slice_in_dim(g_conv[:, 0:4096].reshape(2, 4, 1024), me * 256, 256, axis=2)
    g_sc_cw = lax.dynamic_slice_in_dim(g_conv[:, 4096:4096 + 1536].reshape(2, 3, 512), me * 128, 128, axis=2)
    g_ffn_cw = lax.dynamic_slice_in_dim(g_conv[:, 6144:].reshape(2, 3, 6144), me * 1536, 1536, axis=2)

    def big(w, g, m, v, name):
        shape = w.shape
        two_d = lambda a: a.reshape(-1, shape[-1])
        return [o.reshape(shape) for o in _adamw(two_d(w), two_d(g), two_d(m), two_d(v), name)]

    upd = {"w_in": big(w_in, gw_in, m_w_in, v_w_in, "adamw_w_in"),
           "w_out": big(w_out, gw_out, m_w_out, v_w_out, "adamw_w_out"),
           "w_up": big(w_up, gw_up, m_w_up, v_w_up, "adamw_w_up"),
           "w_down": big(w_down, gw_down, m_w_down, v_w_down, "adamw_w_down")}
    rep_out = _adamw(
        _pack_rep(norm1_g, lru_conv_b, lru_ba, lru_bx, lru_lambda, norm2_g, lru_wa, lru_wx, final_g), g_rep,
        _pack_rep(m_norm1_g, m_lru_conv_b, m_lru_ba, m_lru_bx, m_lru_lambda, m_norm2_g, m_lru_wa, m_lru_wx, m_final_g),
        _pack_rep(v_norm1_g, v_lru_conv_b, v_lru_ba, v_lru_bx, v_lru_lambda, v_norm2_g, v_lru_wa, v_lru_wx, v_final_g),
        "adamw_rep")
    conv_out = _adamw(s_conv, _pack_conv_shard(g_lru_cw, g_sc_cw, g_ffn_cw),
                      _pack_conv_shard(m_lru_conv_w, m_sc_conv_w, m_ffn_conv_w),
                      _pack_conv_shard(v_lru_conv_w, v_sc_conv_w, v_ffn_conv_w), "adamw_conv")

    names = ["norm1_g", "w_in", "lru_conv_w", "lru_conv_b", "lru_wa", "lru_ba", "lru_wx", "lru_bx", "lru_lambda",
             "sc_conv_w", "w_out", "norm2_g", "w_up", "ffn_conv_w", "w_down", "final_g"]
    groups = []
    g_all = dict(_unpack_rep(g_rep))
    g_all.update(w_in=gw_in, w_out=gw_out, w_up=gw_up, w_down=gw_down,
                 lru_conv_w=g_lru_cw, sc_conv_w=g_sc_cw, ffn_conv_w=g_ffn_cw)
    groups.append(g_all)
    for i in range(3):
        d = dict(_unpack_rep(rep_out[i]))
        cl, cs, cf = _unpack_conv_shard(conv_out[i])
        d.update(lru_conv_w=cl, sc_conv_w=cs, ffn_conv_w=cf)
        d.update({k: v[i] for k, v in upd.items()})
        groups.append(d)
    return (loss, grad_x, *[grp[n] for grp in groups for n in names])
```

```python
import dataclasses
import functools
import math
import operator
from typing import Any, Callable, Optional, Sequence

import jax
import jax.numpy as jnp
from jax import lax
from jax.experimental import pallas as pl
from jax.experimental.pallas import tpu as pltpu

F32 = jnp.float32
BF16 = jnp.bfloat16
MESH = pl.DeviceIdType.MESH

D_MODEL = 1024
D_LRU = 1024
D_SC = 512
D_MIX = D_LRU + D_SC
D_IN = 2 * D_LRU + 3 * D_SC
D_FF = 3072
N_CHIP = 4
RG_C = 8.0
EPS = 1e-6
ADAM_LR = 0.001
ADAM_B1 = 0.9
ADAM_B2 = 0.999
ADAM_EPS = 1e-08
ADAM_WD = 0.01
ADAM_STEP = 10

SUBLANES = 8
PACKED = 16
LANES = 128
VMEM_LIMIT = 56 * 1024 * 1024
GELU_C0 = math.sqrt(2.0 / math.pi)
GELU_C1 = 0.044715

REP_LAYER = 6 * 1024 + 2 * 16 * 64 * 64
REP_ROWS = (2 * REP_LAYER + 1024) // LANES
CONV_LAYER = 4 * 1024 + 2048 + 3 * 6144
CONV_ROWS = 2 * CONV_LAYER // LANES
SMALL_ROWS = REP_ROWS + CONV_ROWS + 8
CONV_PACK_ROWS = 96

W_IN, W_OUT, W_UP, W_DOWN, SMALL = range(5)
COL_SHARDED = {W_IN: True, W_OUT: False, W_UP: True, W_DOWN: False}

ONCE = pl.Buffered(1)
ANY = pl.BlockSpec(memory_space=pl.ANY)


def _cp(*sem):
    return pltpu.CompilerParams(dimension_semantics=sem, vmem_limit_bytes=VMEM_LIMIT)


@dataclasses.dataclass
class Comm:
    srcs: Sequence[Any]
    bufs: Sequence[Any]
    outs: Sequence[Any]
    n_sem: int
    start: Callable
    finish: Callable
    mid: Optional[Callable] = None
    mid_at: Optional[Sequence[int]] = None


def _pallas(body, *, name, grid, in_specs, out_specs, out_shape, args, sem, scratch_shapes=(), comm=None):
    if comm is None:
        res = pl.pallas_call(
            body, name=name, grid=grid, in_specs=list(in_specs), out_specs=list(out_specs),
            out_shape=list(out_shape), scratch_shapes=list(scratch_shapes), compiler_params=_cp(*sem))(*args)
        return tuple(res), ()
    n_in, n_out, n_scr = len(in_specs), len(out_specs), len(scratch_shapes)
    ns, nb, no = len(comm.srcs), len(comm.bufs), len(comm.outs)

    def carrier(*refs):
        p = 0
        main_in = refs[p:p + n_in]
        p += n_in
        c_src = refs[p:p + ns]
        p += ns + nb
        main_out = refs[p:p + n_out]
        p += n_out
        c_buf = refs[p:p + nb]
        p += nb
        c_out = refs[p:p + no]
        p += no
        scr = refs[p:p + n_scr]
        send, recv = refs[p + n_scr], refs[p + n_scr + 1]
        ids = [pl.program_id(a) for a in range(len(grid))]

        def at(steps):
            return functools.reduce(operator.and_, [i == s for i, s in zip(ids, steps)])

        @pl.when(at([0] * len(grid)))
        def _():
            comm.start(c_src, c_buf, c_out, send, recv)

        if comm.mid is not None:
            @pl.when(at(comm.mid_at))
            def _():
                comm.mid(c_src, c_buf, c_out, send, recv)

        body(*main_in, *main_out, *scr)

        @pl.when(at([g - 1 for g in grid]))
        def _():
            comm.finish(c_src, c_buf, c_out, send, recv)

    res = pl.pallas_call(
        carrier, name=name, grid=grid,
        in_specs=list(in_specs) + [ANY] * (ns + nb),
        out_specs=list(out_specs) + [ANY] * (nb + no),
        out_shape=list(out_shape) + [jax.ShapeDtypeStruct(b.shape, b.dtype) for b in comm.bufs] + list(comm.outs),
        input_output_aliases={n_in + ns + j: n_out + j for j in range(nb)},
        scratch_shapes=list(scratch_shapes) + [pltpu.SemaphoreType.DMA((comm.n_sem,)),
                                               pltpu.SemaphoreType.DMA((comm.n_sem,))],
        compiler_params=_cp(*(["arbitrary"] * len(grid))),
    )(*args, *comm.srcs, *comm.bufs)
    return tuple(res[:n_out]), tuple(res[n_out:])


def _comm_call(comm, name):
    ns, nb, no = len(comm.srcs), len(comm.bufs), len(comm.outs)

    def body(*refs):
        c_src = refs[0:ns]
        c_buf = refs[ns + nb:ns + 2 * nb]
        c_out = refs[ns + 2 * nb:ns + 2 * nb + no]
        send, recv = refs[ns + 2 * nb + no], refs[ns + 2 * nb + no + 1]
        comm.start(c_src, c_buf, c_out, send, recv)
        if comm.mid is not None:
            comm.mid(c_src, c_buf, c_out, send, recv)
        comm.finish(c_src, c_buf, c_out, send, recv)

    return tuple(pl.pallas_call(
        body, name=name,
        in_specs=[ANY] * (ns + nb), out_specs=[ANY] * (nb + no),
        out_shape=[jax.ShapeDtypeStruct(b.shape, b.dtype) for b in comm.bufs] + list(comm.outs),
        input_output_aliases={ns + j: j for j in range(nb)},
        scratch_shapes=[pltpu.SemaphoreType.DMA((comm.n_sem,)), pltpu.SemaphoreType.DMA((comm.n_sem,))],
    )(*comm.srcs, *comm.bufs))


def _sigmoid(v):
    return 1.0 / (1.0 + jnp.exp(-v))


def _sigmoid_tanh(v):
    return 0.5 + 0.5 * jnp.tanh(0.5 * v)


def _gelu_parts(v):
    v2 = v * v
    t = jnp.tanh(GELU_C0 * v * (1.0 + GELU_C1 * v2))
    half = 0.5 * (1.0 + t)
    gel = v * half
    dgel = half + 0.5 * v * (1.0 - t * t) * (GELU_C0 * (1.0 + 3.0 * GELU_C1 * v2))
    return gel, dgel


def _gelu(v):
    t = jnp.tanh(GELU_C0 * v * (1.0 + GELU_C1 * (v * v)))
    return 0.5 * v * (1.0 + t)


def _neg_expm1(y, a):
    p = jnp.full_like(y, 1.0 / 120.0)
    for coef in (1.0 / 24.0, 1.0 / 6.0, 0.5, 1.0):
        p = p * y + coef
    return jnp.where(y > -0.1, -(p * y), 1.0 - a * a)


def _softplus_neg(lam):
    nl = -lam
    e = jnp.exp(-jnp.abs(nl))
    u = 1.0 + e
    l1p = jnp.where(u == 1.0, e, jnp.log(u) * e / (u - 1.0))
    return jnp.maximum(nl, 0.0) + l1p


def _conv_taps(ext, taps, n_out):
    kw = len(taps)
    acc = taps[kw - 1] * ext[SUBLANES:SUBLANES + n_out]
    for k in range(kw - 1):
        acc = acc + taps[k] * pltpu.roll(ext, kw - 1 - k, axis=0)[SUBLANES:SUBLANES + n_out]
    return acc


def _conv_taps_t(ext, taps, n_out):
    kw = len(taps)
    n = ext.shape[0]
    acc = taps[kw - 1] * ext[0:n_out]
    for k in range(kw - 1):
        acc = acc + taps[k] * pltpu.roll(ext, n - (kw - 1 - k), axis=0)[0:n_out]
    return acc


def _scan8(a, b, carry, row):
    for s in (1, 2, 4):
        m = row >= s
        a_sh = jnp.where(m, pltpu.roll(a, s, axis=0), 1.0)
        b_sh = jnp.where(m, pltpu.roll(b, s, axis=0), 0.0)
        b = a * b_sh + b
        a = a * a_sh
    return a * carry + b


def _scan8_rev(a, b, carry, row):
    for s in (1, 2, 4):
        m = row < SUBLANES - s
        a_sh = jnp.where(m, pltpu.roll(a, SUBLANES - s, axis=0), 1.0)
        b_sh = jnp.where(m, pltpu.roll(b, SUBLANES - s, axis=0), 0.0)
        b = a * b_sh + b
        a = a * a_sh
    return a * carry + b


def _cast_into_full(w, kind, idx, name):
    nl, r, c = w.shape
    tr = 256 if r % 256 == 0 else r
    nrb = r // tr

    def body(idx_ref, w_ref, o0_ref, o1_ref):
        o0_ref[...] = w_ref[0].astype(BF16)
        o1_ref[...] = w_ref[1].astype(BF16)

    if COL_SHARDED[kind]:
        full = (r, N_CHIP * c)
        o_spec = pl.BlockSpec((tr, c), lambda i, idx_ref: (i, idx_ref[1]))
    else:
        full = (N_CHIP * r, c)
        o_spec = pl.BlockSpec((tr, c), lambda i, idx_ref: (idx_ref[1] * nrb + i, 0))
    return pl.pallas_call(
        body, name=name,
        grid_spec=pltpu.PrefetchScalarGridSpec(
            num_scalar_prefetch=1, grid=(nrb,),
            in_specs=[pl.BlockSpec((nl, tr, c), lambda i, idx_ref: (0, i, 0))], out_specs=[o_spec, o_spec]),
        out_shape=[jax.ShapeDtypeStruct(full, BF16)] * 2,
        compiler_params=_cp("parallel"),
    )(idx, w)


def _norm_mm(x, g, w, name, planes=False, tm=512, tn=512, comm=None):
    t_len, d = x.shape
    n = w.shape[1]
    half = n // 2

    def body(x_ref, g_ref, w_ref, z_ref, h_ref):
        xv = x_ref[...]
        r = lax.rsqrt(jnp.mean(xv * xv, axis=-1, keepdims=True) + EPS)
        h_ref[...] = ((xv * r) * g_ref[...]).astype(BF16)
        for n0 in range(0, n, tn):
            blk = jnp.dot(h_ref[...], w_ref[:, n0:n0 + tn], preferred_element_type=F32).astype(BF16)
            if planes:
                z_ref[n0 // half, :, n0 % half:n0 % half + tn] = blk
            else:
                z_ref[:, n0:n0 + tn] = blk

    if planes:
        z_shape = jax.ShapeDtypeStruct((2, t_len, half), BF16)
        z_spec = pl.BlockSpec((2, tm, half), lambda i: (0, i, 0))
    else:
        z_shape = jax.ShapeDtypeStruct((t_len, n), BF16)
        z_spec = pl.BlockSpec((tm, n), lambda i: (i, 0))
    return _pallas(
        body, name=name, grid=(t_len // tm,),
        in_specs=[pl.BlockSpec((tm, d), lambda i: (i, 0)),
                  pl.BlockSpec((1, d), lambda i: (0, 0)),
                  pl.BlockSpec((d, n), lambda i: (0, 0), pipeline_mode=ONCE)],
        out_specs=[z_spec, pl.BlockSpec((tm, d), lambda i: (i, 0))],
        out_shape=[z_shape, jax.ShapeDtypeStruct((t_len, d), BF16)],
        args=(x, g, w), sem=("parallel",), comm=comm)


def _mm_res(a, w, res, name, tm=512, comm=None):
    t_len, k = a.shape
    n = w.shape[1]

    def body(a_ref, w_ref, r_ref, o_ref):
        o_ref[...] = r_ref[...] + jnp.dot(a_ref[...], w_ref[...], preferred_element_type=F32)

    return _pallas(
        body, name=name, grid=(t_len // tm,),
        in_specs=[pl.BlockSpec((tm, k), lambda i: (i, 0)),
                  pl.BlockSpec((k, n), lambda i: (0, 0), pipeline_mode=ONCE),
                  pl.BlockSpec((tm, n), lambda i: (i, 0))],
        out_specs=[pl.BlockSpec((tm, n), lambda i: (i, 0))],
        out_shape=[jax.ShapeDtypeStruct((t_len, n), F32)],
        args=(a, w, res), sem=("parallel",), comm=comm)


def _mm_nt(a, w, name, tm=512, comm=None):
    t_len, k = a.shape
    n = w.shape[0]

    def body(a_ref, w_ref, o_ref):
        o_ref[...] = lax.dot_general(a_ref[...], w_ref[...], (((1,), (1,)), ((), ())),
                                     preferred_element_type=F32).astype(BF16)

    return _pallas(
        body, name=name, grid=(t_len // tm,),
        in_specs=[pl.BlockSpec((tm, k), lambda i: (i, 0)),
                  pl.BlockSpec((n, k), lambda i: (0, 0), pipeline_mode=ONCE)],
        out_specs=[pl.BlockSpec((tm, n), lambda i: (i, 0))],
        out_shape=[jax.ShapeDtypeStruct((t_len, n), BF16)],
        args=(a, w), sem=("parallel",), comm=comm)


def _mm_nt_normbwd(dz, w, x, g, dres, name, planes=False, tm=512, comm=None):
    t_len, d = x.shape
    n = w.shape[1]
    half = n // 2
    nt_dims = (((1,), (1,)), ((), ()))

    def body(dz_ref, w_ref, x_ref, g_ref, r_ref, dx_ref, dxb_ref, dg_ref):
        @pl.when(pl.program_id(0) == 0)
        def _():
            dg_ref[...] = jnp.zeros_like(dg_ref)

        if planes:
            dh = (lax.dot_general(dz_ref[0], w_ref[:, 0:half], nt_dims, preferred_element_type=F32)
                  + lax.dot_general(dz_ref[1], w_ref[:, half:], nt_dims, preferred_element_type=F32))
        else:
            dh = lax.dot_general(dz_ref[...], w_ref[...], nt_dims, preferred_element_type=F32)
        xv = x_ref[...]
        r = lax.rsqrt(jnp.mean(xv * xv, axis=-1, keepdims=True) + EPS)
        xh = xv * r
        dhg = dh * g_ref[...]
        dx = r_ref[...] + r * (dhg - xh * jnp.mean(dhg * xh, axis=-1, keepdims=True))
        dx_ref[...] = dx
        dxb_ref[...] = dx.astype(BF16)
        dg_ref[0:1, :] += jnp.sum(dh * xh, axis=0, keepdims=True)

    if planes:
        dz_spec = pl.BlockSpec((2, tm, half), lambda i: (0, i, 0))
    else:
        dz_spec = pl.BlockSpec((tm, n), lambda i: (i, 0))
    return _pallas(
        body, name=name, grid=(t_len // tm,),
        in_specs=[dz_spec,
                  pl.BlockSpec((d, n), lambda i: (0, 0), pipeline_mode=ONCE),
                  pl.BlockSpec((tm, d), lambda i: (i, 0)),
                  pl.BlockSpec((1, d), lambda i: (0, 0)),
                  pl.BlockSpec((tm, d), lambda i: (i, 0))],
        out_specs=[pl.BlockSpec((tm, d), lambda i: (i, 0)),
                   pl.BlockSpec((tm, d), lambda i: (i, 0)),
                   pl.BlockSpec((SUBLANES, d), lambda i: (0, 0))],
        out_shape=[jax.ShapeDtypeStruct((t_len, d), F32),
                   jax.ShapeDtypeStruct((t_len, d), BF16),
                   jax.ShapeDtypeStruct((SUBLANES, d), F32)],
        args=(dz, w, x, g, dres), sem=("arbitrary",), comm=comm)


def _mm_tn(a, g, name, tk, tn, planes=False, tt=1024, comm=None):
    t_len, k = a.shape
    n = 2 * g.shape[2] if planes else g.shape[1]
    nn = n // tn
    half = nn // 2
    tt = min(tt, t_len)

    def body(a_ref, g_ref, o_ref):
        @pl.when(pl.program_id(2) == 0)
        def _():
            o_ref[...] = jnp.zeros_like(o_ref)

        o_ref[...] += lax.dot_general(a_ref[...], g_ref[...], (((0,), (0,)), ((), ())),
                                      preferred_element_type=F32)

    if planes:
        g_spec = pl.BlockSpec((None, tt, tn), lambda i, j, t: (j // half, t, j % half))
    else:
        g_spec = pl.BlockSpec((tt, tn), lambda i, j, t: (t, j))
    return _pallas(
        body, name=name, grid=(k // tk, nn, t_len // tt),
        in_specs=[pl.BlockSpec((tt, tk), lambda i, j, t: (t, i)), g_spec],
        out_specs=[pl.BlockSpec((tk, tn), lambda i, j, t: (i, j))],
        out_shape=[jax.ShapeDtypeStruct((k, n), F32)],
        args=(a, g), sem=("parallel", "parallel", "arbitrary"), comm=comm)


def _lru_gates(rp, ip, spn):
    r = _sigmoid(rp)
    i = _sigmoid_tanh(ip)
    la = r * spn
    a = jnp.exp(la)
    mult = jnp.sqrt(_neg_expm1(2.0 * la, a))
    return r, i, a, mult


def _mixer_fwd(z, cw8, vec8, wa_bd, wx_bd, scw8, name, tb=256, comm=None):
    t_len = z.shape[0]

    def body(z_ref, cw_ref, vec_ref, wa_ref, wx_ref, scw_ref, y_ref, h_ref,
             xhalo, phalo, hcar, lx_s, rp_s, ip_s):
        @pl.when(pl.program_id(0) == 0)
        def _():
            xhalo[...] = jnp.zeros_like(xhalo)
            phalo[...] = jnp.zeros_like(phalo)
            hcar[...] = jnp.zeros_like(hcar)

        cw = cw_ref[...]
        vec = vec_ref[...]
        xp = z_ref[:, 0:D_LRU].astype(F32)
        ext = jnp.concatenate([xhalo[...], xp], axis=0)
        lx = vec[0:1] + _conv_taps(ext, [cw[k:k + 1] for k in range(4)], tb)
        xhalo[...] = xp[tb - SUBLANES:]
        lx_s[...] = lx
        lxb = lx.astype(BF16)
        for q in range(4):
            sl = slice(q * 256, (q + 1) * 256)
            rp_s[:, sl] = jnp.dot(lxb[:, sl], wa_ref[q], preferred_element_type=F32) + vec[1:2, sl]
            ip_s[:, sl] = jnp.dot(lxb[:, sl], wx_ref[q], preferred_element_type=F32) + vec[2:3, sl]

        spn = jnp.broadcast_to(-RG_C * _softplus_neg(vec[3:4]), (SUBLANES, D_LRU))
        row = lax.broadcasted_iota(jnp.int32, (SUBLANES, D_LRU), 0)

        def step(ci, carry):
            o = pl.multiple_of(ci * PACKED, PACKED)
            gate = z_ref[pl.ds(o, PACKED), D_LRU:2 * D_LRU].astype(F32)
            ys = []
            for sub in range(2):
                rows = pl.ds(pl.multiple_of(o + sub * SUBLANES, SUBLANES), SUBLANES)
                lxv = lx_s[rows, :]
                _, i, a, mult = _lru_gates(rp_s[rows, :], ip_s[rows, :], spn)
                h = _scan8(a, mult * (i * lxv), carry, row)
                h_ref[rows, :] = h
                ys.append(h * _gelu(gate[sub * SUBLANES:(sub + 1) * SUBLANES]))
                carry = jnp.broadcast_to(h[SUBLANES - 1:SUBLANES, :], (SUBLANES, D_LRU))
            y_ref[pl.ds(o, PACKED), 0:D_LRU] = jnp.concatenate(ys, axis=0).astype(BF16)
            return carry

        hcar[...] = lax.fori_loop(0, tb // PACKED, step, hcar[...])

        scw = scw_ref[...]
        o_b, o_c, o_x = 2 * D_LRU, 2 * D_LRU + D_SC, 2 * D_LRU + 2 * D_SC
        p = z_ref[:, o_c:o_x].astype(F32) * z_ref[:, o_x:].astype(F32)
        pext = jnp.concatenate([phalo[...], p], axis=0)
        q = _conv_taps(pext, [scw[k:k + 1] for k in range(3)], tb)
        phalo[...] = p[tb - SUBLANES:]
        y_ref[:, D_LRU:] = (z_ref[:, o_b:o_c].astype(F32) * q).astype(BF16)

    const = lambda t: (0, 0)
    return _pallas(
        body, name=name, grid=(t_len // tb,),
        in_specs=[pl.BlockSpec((tb, D_IN), lambda t: (t, 0)),
                  pl.BlockSpec((SUBLANES, D_LRU), const),
                  pl.BlockSpec((SUBLANES, D_LRU), const),
                  pl.BlockSpec((4, 256, 256), lambda t: (0, 0, 0)),
                  pl.BlockSpec((4, 256, 256), lambda t: (0, 0, 0)),
                  pl.BlockSpec((SUBLANES, D_SC), const)],
        out_specs=[pl.BlockSpec((tb, D_MIX), lambda t: (t, 0)),
                   pl.BlockSpec((tb, D_LRU), lambda t: (t, 0))],
        out_shape=[jax.ShapeDtypeStruct((t_len, D_MIX), BF16),
                   jax.ShapeDtypeStruct((t_len, D_LRU), F32)],
        scratch_shapes=[pltpu.VMEM((SUBLANES, D_LRU), F32), pltpu.VMEM((SUBLANES, D_SC), F32),
                        pltpu.VMEM((SUBLANES, D_LRU), F32), pltpu.VMEM((tb, D_LRU), F32),
                        pltpu.VMEM((tb, D_LRU), F32), pltpu.VMEM((tb, D_LRU), F32)],
        args=(z, cw8, vec8, wa_bd, wx_bd, scw8), sem=("arbitrary",), comm=comm)


def _mixer_bwd(z, h, dy, cw8, vec8, wa_bd, wx_bd, scw8, name, tb=256, comm=None):
    t_len = z.shape[0]
    nb = t_len // tb

    def body(z_ref, zh_ref, h_ref, hh_ref, dy_ref, cw_ref, vec_ref, wa_ref, wx_ref, scw_ref,
             dz_ref, dcw_ref, dvec_ref, dwa_ref, dwx_ref, dscw_ref,
             lx_s, rp_s, ip_s, drpb_s, dipb_s, dlx_s, hext_s, acc_s, acar, gcar, dqh):
        t = pl.program_id(0)
        first_block = t == nb - 1

        @pl.when(t == 0)
        def _():
            for ref in (dcw_ref, dvec_ref, dwa_ref, dwx_ref, dscw_ref, acc_s, acar, gcar, dqh):
                ref[...] = jnp.zeros_like(ref)
            dlx_s[tb:, :] = jnp.zeros((SUBLANES, D_LRU), F32)

        cw = cw_ref[...]
        vec = vec_ref[...]
        scw = scw_ref[...]
        ctaps = [cw[k:k + 1] for k in range(4)]
        staps = [scw[k:k + 1] for k in range(3)]
        keep = jnp.where(first_block, 0.0, 1.0)
        zh = zh_ref[...].astype(F32)[PACKED - SUBLANES:] * keep

        xp = z_ref[:, 0:D_LRU].astype(F32)
        xext = jnp.concatenate([zh[:, 0:D_LRU], xp], axis=0)
        lx = vec[0:1] + _conv_taps(xext, ctaps, tb)
        lx_s[...] = lx
        lxb = lx.astype(BF16)
        for q in range(4):
            sl = slice(q * 256, (q + 1) * 256)
            rp_s[:, sl] = jnp.dot(lxb[:, sl], wa_ref[q], preferred_element_type=F32) + vec[1:2, sl]
            ip_s[:, sl] = jnp.dot(lxb[:, sl], wx_ref[q], preferred_element_type=F32) + vec[2:3, sl]
        hext_s[0:SUBLANES, :] = hh_ref[...] * keep
        hext_s[SUBLANES:, :] = h_ref[...]

        spn = jnp.broadcast_to(-RG_C * _softplus_neg(vec[3:4]), (SUBLANES, D_LRU))
        row = lax.broadcasted_iota(jnp.int32, (SUBLANES, D_LRU), 0)

        def step(ci, carry):
            a_next, g_next = carry
            o = pl.multiple_of((tb // PACKED - 1 - ci) * PACKED, PACKED)
            rows16 = pl.ds(o, PACKED)
            gate16 = z_ref[rows16, D_LRU:2 * D_LRU].astype(F32)
            dyl16 = dy_ref[rows16, 0:D_LRU].astype(F32)
            dgs, drs, dis = [None, None], [None, None], [None, None]
            for sub in (1, 0):
                oo = pl.multiple_of(o + sub * SUBLANES, SUBLANES)
                rows = pl.ds(oo, SUBLANES)
                half = slice(sub * SUBLANES, (sub + 1) * SUBLANES)
                lxv = lx_s[rows, :]
                r, i, a, mult = _lru_gates(rp_s[rows, :], ip_s[rows, :], spn)
                hwin = hext_s[pl.ds(oo, 2 * SUBLANES), :]
                hv = hwin[SUBLANES:]
                hprev = pltpu.roll(hwin, 1, axis=0)[SUBLANES:]
                gel, dgel = _gelu_parts(gate16[half])
                dyl = dyl16[half]
                a_up = jnp.where(row < SUBLANES - 1, pltpu.roll(a, SUBLANES - 1, axis=0), a_next)
                gg = _scan8_rev(a_up, dyl * gel, g_next, row)
                dgs[sub] = dyl * hv * dgel
                ilx = i * lxv
                dla = gg * hprev * a - (gg * ilx) * (a * a) / mult
                dlx_s[rows, :] = gg * mult * i
                drp = dla * spn * r * (1.0 - r)
                dip = gg * mult * lxv * i * (1.0 - i)
                drs[sub] = drp
                dis[sub] = dip
                acc_s[0] += drp
                acc_s[1] += dip
                acc_s[2] += dla * r
                a_next = jnp.broadcast_to(a[0:1, :], (SUBLANES, D_LRU))
                g_next = jnp.broadcast_to(gg[0:1, :], (SUBLANES, D_LRU))
            dz_ref[rows16, D_LRU:2 * D_LRU] = jnp.concatenate(dgs, axis=0).astype(BF16)
            drpb_s[rows16, :] = jnp.concatenate(drs, axis=0).astype(BF16)
            dipb_s[rows16, :] = jnp.concatenate(dis, axis=0).astype(BF16)
            return a_next, g_next

        a_c, g_c = lax.fori_loop(0, tb // PACKED, step, (acar[...], gcar[...]))
        acar[...] = a_c
        gcar[...] = g_c

        drpb = drpb_s[...]
        dipb = dipb_s[...]
        nt_dims = (((1,), (1,)), ((), ()))
        tn_dims = (((0,), (0,)), ((), ()))
        for q in range(4):
            sl = slice(q * 256, (q + 1) * 256)
            dlx_s[0:tb, sl] += (
                lax.dot_general(drpb[:, sl], wa_ref[q], nt_dims, preferred_element_type=F32)
                + lax.dot_general(dipb[:, sl], wx_ref[q], nt_dims, preferred_element_type=F32))
            dwa_ref[q] += lax.dot_general(lxb[:, sl], drpb[:, sl], tn_dims, preferred_element_type=F32)
            dwx_ref[q] += lax.dot_general(lxb[:, sl], dipb[:, sl], tn_dims, preferred_element_type=F32)

        dlx_ext = dlx_s[...]
        dlx = dlx_ext[0:tb]
        dz_ref[:, 0:D_LRU] = _conv_taps_t(dlx_ext, ctaps, tb).astype(BF16)
        dcw_ref[3:4, :] += jnp.sum(dlx * xp, axis=0, keepdims=True)
        for k in range(3):
            shifted = pltpu.roll(xext, 3 - k, axis=0)[SUBLANES:]
            dcw_ref[k:k + 1, :] += jnp.sum(dlx * shifted, axis=0, keepdims=True)
        dvec_ref[0:1, :] += jnp.sum(dlx, axis=0, keepdims=True)
        dlx_s[tb:, :] = dlx[0:SUBLANES]

        o_b, o_c, o_x = 2 * D_LRU, 2 * D_LRU + D_SC, 2 * D_LRU + 2 * D_SC
        sb = z_ref[:, o_b:o_c].astype(F32)
        scc = z_ref[:, o_c:o_x].astype(F32)
        sx = z_ref[:, o_x:].astype(F32)
        p = scc * sx
        pext = jnp.concatenate([zh[:, o_c:o_x] * zh[:, o_x:], p], axis=0)
        q = _conv_taps(pext, staps, tb)
        dys = dy_ref[:, D_LRU:].astype(F32)
        dq = dys * sb
        dp = _conv_taps_t(jnp.concatenate([dq, dqh[...]], axis=0), staps, tb)
        dscw_ref[2:3, :] += jnp.sum(dq * p, axis=0, keepdims=True)
        for k in range(2):
            shifted = pltpu.roll(pext, 2 - k, axis=0)[SUBLANES:]
            dscw_ref[k:k + 1, :] += jnp.sum(dq * shifted, axis=0, keepdims=True)
        dqh[...] = dq[0:SUBLANES]
        dz_ref[:, o_b:o_c] = (dys * q).astype(BF16)
        dz_ref[:, o_c:o_x] = (dp * sx).astype(BF16)
        dz_ref[:, o_x:] = (dp * scc).astype(BF16)

        @pl.when(first_block)
        def _():
            dvec_ref[1:2, :] = jnp.sum(acc_s[0], axis=0, keepdims=True)
            dvec_ref[2:3, :] = jnp.sum(acc_s[1], axis=0, keepdims=True)
            dvec_ref[3:4, :] = (jnp.sum(acc_s[2], axis=0, keepdims=True) * RG_C * _sigmoid(-vec[3:4]))

    blk = lambda t: (nb - 1 - t, 0)
    halo8 = lambda t: (jnp.maximum((nb - 1 - t) * (tb // SUBLANES) - 1, 0), 0)
    halo16 = lambda t: (jnp.maximum((nb - 1 - t) * (tb // PACKED) - 1, 0), 0)
    const = lambda t: (0, 0)
    const3 = lambda t: (0, 0, 0)
    return _pallas(
        body, name=name, grid=(nb,),
        in_specs=[pl.BlockSpec((tb, D_IN), blk), pl.BlockSpec((PACKED, D_IN), halo16),
                  pl.BlockSpec((tb, D_LRU), blk), pl.BlockSpec((SUBLANES, D_LRU), halo8),
                  pl.BlockSpec((tb, D_MIX), blk),
                  pl.BlockSpec((SUBLANES, D_LRU), const), pl.BlockSpec((SUBLANES, D_LRU), const),
                  pl.BlockSpec((4, 256, 256), const3), pl.BlockSpec((4, 256, 256), const3),
                  pl.BlockSpec((SUBLANES, D_SC), const)],
        out_specs=[pl.BlockSpec((tb, D_IN), blk),
                   pl.BlockSpec((SUBLANES, D_LRU), const), pl.BlockSpec((SUBLANES, D_LRU), const),
                   pl.BlockSpec((4, 256, 256), const3), pl.BlockSpec((4, 256, 256), const3),
                   pl.BlockSpec((SUBLANES, D_SC), const)],
        out_shape=[jax.ShapeDtypeStruct((t_len, D_IN), BF16),
                   jax.ShapeDtypeStruct((SUBLANES, D_LRU), F32), jax.ShapeDtypeStruct((SUBLANES, D_LRU), F32),
                   jax.ShapeDtypeStruct((4, 256, 256), F32), jax.ShapeDtypeStruct((4, 256, 256), F32),
                   jax.ShapeDtypeStruct((SUBLANES, D_SC), F32)],
        scratch_shapes=[pltpu.VMEM((tb, D_LRU), F32),
                        pltpu.VMEM((tb, D_LRU), F32), pltpu.VMEM((tb, D_LRU), F32),
                        pltpu.VMEM((tb, D_LRU), BF16), pltpu.VMEM((tb, D_LRU), BF16),
                        pltpu.VMEM((tb + SUBLANES, D_LRU), F32), pltpu.VMEM((tb + SUBLANES, D_LRU), F32),
                        pltpu.VMEM((3, SUBLANES, D_LRU), F32),
                        pltpu.VMEM((SUBLANES, D_LRU), F32), pltpu.VMEM((SUBLANES, D_LRU), F32),
                        pltpu.VMEM((SUBLANES, D_SC), F32)],
        args=(z, z, h, h, dy, cw8, vec8, wa_bd, wx_bd, scw8), sem=("arbitrary",), comm=comm)


def _ffn_act(u, fw, name, tb=256, tn=1024, rc=64, comm=None):
    t_len = u.shape[1]
    hb = tb // PACKED

    def body(u_ref, uh_ref, fw_ref, o_ref, ext):
        keep = jnp.where(pl.program_id(0) == 0, 0.0, 1.0)
        ext[:, 0:SUBLANES, :] = uh_ref[...].astype(F32)[:, PACKED - SUBLANES:, :] * keep
        ext[:, SUBLANES:, :] = u_ref[...].astype(F32)
        fw_v = fw_ref[...]

        for lb in range(tn // LANES):
            lanes = slice(lb * LANES, (lb + 1) * LANES)
            wg = [fw_v[0, k:k + 1, lanes] for k in range(3)]
            wu = [fw_v[1, k:k + 1, lanes] for k in range(3)]

            def chunk(ci, c, lanes=lanes, wg=wg, wu=wu):
                o = pl.multiple_of(ci * rc, rc)
                win = pl.ds(o, rc + SUBLANES)
                gate = _conv_taps(ext[0, win, lanes], wg, rc)
                up = _conv_taps(ext[1, win, lanes], wu, rc)
                o_ref[pl.ds(o, rc), lanes] = (_gelu(gate) * up).astype(BF16)
                return c

            lax.fori_loop(0, tb // rc, chunk, 0)

    return _pallas(
        body, name=name, grid=(t_len // tb, D_FF // tn),
        in_specs=[pl.BlockSpec((2, tb, tn), lambda i, j: (0, i, j)),
                  pl.BlockSpec((2, PACKED, tn), lambda i, j: (0, jnp.maximum(i * hb - 1, 0), j)),
                  pl.BlockSpec((2, SUBLANES, tn), lambda i, j: (0, 0, j))],
        out_specs=[pl.BlockSpec((tb, tn), lambda i, j: (i, j))],
        out_shape=[jax.ShapeDtypeStruct((t_len, D_FF), BF16)],
        scratch_shapes=[pltpu.VMEM((2, tb + SUBLANES, tn), F32)],
        args=(u, u, fw), sem=("parallel", "parallel"), comm=comm)


def _ffn_bwd(dact, u, fw, name, tb=256, tn=1024, rc=32, comm=None):
    t_len = u.shape[1]
    ni = t_len // tb
    hb = tb // PACKED
    last_halo = t_len // PACKED - 1

    def body(d_ref, dn_ref, u_ref, up_ref, un_ref, fw_ref, du_ref, dfw_ref, extu, extd, acc):
        i = pl.program_id(1)

        @pl.when(i == 0)
        def _():
            acc[...] = jnp.zeros_like(acc)

        keep_prev = jnp.where(i == 0, 0.0, 1.0)
        keep_next = jnp.where(i == ni - 1, 0.0, 1.0)
        extu[:, 0:SUBLANES, :] = up_ref[...].astype(F32)[:, PACKED - SUBLANES:, :] * keep_prev
        extu[:, SUBLANES:SUBLANES + tb, :] = u_ref[...].astype(F32)
        extu[:, SUBLANES + tb:, :] = un_ref[...].astype(F32)[:, 0:SUBLANES, :]
        extd[0:tb, :] = d_ref[...].astype(F32)
        extd[tb:, :] = dn_ref[...].astype(F32)[0:SUBLANES] * keep_next
        fw_v = fw_ref[...]
        m = rc + SUBLANES

        for lb in range(tn // LANES):
            lanes = slice(lb * LANES, (lb + 1) * LANES)
            taps = [[fw_v[pln, k:k + 1, lanes] for k in range(3)] for pln in range(2)]

            def chunk(ci, c, lanes=lanes, taps=taps):
                o = pl.multiple_of(ci * rc, rc)
                win = pl.ds(o, rc + 2 * SUBLANES)
                sh = []
                for pln in range(2):
                    e = extu[pln, win, lanes]
                    sh.append([pltpu.roll(e, 2, axis=0)[SUBLANES:], pltpu.roll(e, 1, axis=0)[SUBLANES:],
                               e[SUBLANES:]])
                gate = sum(taps[0][k] * sh[0][k] for k in range(3))
                up = sum(taps[1][k] * sh[1][k] for k in range(3))
                dv = extd[pl.ds(o, m), lanes]
                gel, dgel = _gelu_parts(gate)
                dpost = [dv * up * dgel, dv * gel]
                for pln in range(2):
                    du_ref[pln, pl.ds(o, rc), lanes] = _conv_taps_t(dpost[pln], taps[pln], rc).astype(BF16)
                    for k in range(3):
                        prod = dpost[pln][0:rc] * sh[pln][k][0:rc]
                        acc[3 * pln + k, :, lanes] += sum(
                            prod[s:s + SUBLANES] for s in range(0, rc, SUBLANES))
                return c

            lax.fori_loop(0, tb // rc, chunk, 0)

        @pl.when(i == ni - 1)
        def _():
            dfw_ref[...] = jnp.zeros_like(dfw_ref)
            for pln in range(2):
                for k in range(3):
                    dfw_ref[pln, k:k + 1, :] = jnp.sum(acc[3 * pln + k], axis=0, keepdims=True)

    return _pallas(
        body, name=name, grid=(D_FF // tn, ni),
        in_specs=[pl.BlockSpec((tb, tn), lambda j, i: (i, j)),
                  pl.BlockSpec((PACKED, tn), lambda j, i: (jnp.minimum((i + 1) * hb, last_halo), j)),
                  pl.BlockSpec((2, tb, tn), lambda j, i: (0, i, j)),
                  pl.BlockSpec((2, PACKED, tn), lambda j, i: (0, jnp.maximum(i * hb - 1, 0), j)),
                  pl.BlockSpec((2, PACKED, tn), lambda j, i: (0, jnp.minimum((i + 1) * hb, last_halo), j)),
                  pl.BlockSpec((2, SUBLANES, tn), lambda j, i: (0, 0, j))],
        out_specs=[pl.BlockSpec((2, tb, tn), lambda j, i: (0, i, j)),
                   pl.BlockSpec((2, SUBLANES, tn), lambda j, i: (0, 0, j))],
        out_shape=[jax.ShapeDtypeStruct((2, t_len, D_FF), BF16),
                   jax.ShapeDtypeStruct((2, SUBLANES, D_FF), F32)],
        scratch_shapes=[pltpu.VMEM((2, tb + 2 * SUBLANES, tn), F32),
                        pltpu.VMEM((tb + SUBLANES, tn), F32),
                        pltpu.VMEM((6, SUBLANES, tn), F32)],
        args=(dact, dact, u, u, u, fw), sem=("parallel", "arbitrary"), comm=comm)


def _loss_head(x, g, target, name, tb=256):
    t_len, d = x.shape

    def body(x_ref, g_ref, t_ref, dx_ref, dxb_ref, dg_ref, loss_ref):
        @pl.when(pl.program_id(0) == 0)
        def _():
            dg_ref[...] = jnp.zeros_like(dg_ref)
            loss_ref[...] = jnp.zeros_like(loss_ref)

        xv = x_ref[...]
        gv = g_ref[...]
        r = lax.rsqrt(jnp.mean(xv * xv, axis=-1, keepdims=True) + EPS)
        xh = xv * r
        err = xh * gv - t_ref[...]
        loss_ref[...] += (0.5 / d) * jnp.sum(jnp.sum(err * err, axis=-1, keepdims=True), axis=0, keepdims=True)
        dy = err * (1.0 / d)
        dyg = dy * gv
        dx = r * (dyg - xh * jnp.mean(dyg * xh, axis=-1, keepdims=True))
        dx_ref[...] = dx
        dxb_ref[...] = dx.astype(BF16)
        dg_ref[0:1, :] += jnp.sum(dy * xh, axis=0, keepdims=True)

    return _pallas(
        body, name=name, grid=(t_len // tb,),
        in_specs=[pl.BlockSpec((tb, d), lambda i: (i, 0)), pl.BlockSpec((1, d), lambda i: (0, 0)),
                  pl.BlockSpec((tb, d), lambda i: (i, 0))],
        out_specs=[pl.BlockSpec((tb, d), lambda i: (i, 0)), pl.BlockSpec((tb, d), lambda i: (i, 0)),
                   pl.BlockSpec((SUBLANES, d), lambda i: (0, 0)),
                   pl.BlockSpec((SUBLANES, LANES), lambda i: (0, 0))],
        out_shape=[jax.ShapeDtypeStruct((t_len, d), F32), jax.ShapeDtypeStruct((t_len, d), BF16),
                   jax.ShapeDtypeStruct((SUBLANES, d), F32), jax.ShapeDtypeStruct((SUBLANES, LANES), F32)],
        args=(x, g, target), sem=("arbitrary",))[0]


def _adamw(w, g, m, v, name):
    r, c = w.shape
    tr = 256 if r % 256 == 0 else r
    c1 = 1.0 / (1.0 - ADAM_B1 ** ADAM_STEP)
    c2 = 1.0 / (1.0 - ADAM_B2 ** ADAM_STEP)

    def body(w_ref, g_ref, m_ref, v_ref, d_ref, mo_ref, vo_ref):
        gv = g_ref[...]
        mn = ADAM_B1 * m_ref[...] + (1.0 - ADAM_B1) * gv
        vn = ADAM_B2 * v_ref[...] + (1.0 - ADAM_B2) * (gv * gv)
        d_ref[...] = -ADAM_LR * ((mn * c1) / (jnp.sqrt(vn * c2) + ADAM_EPS) + ADAM_WD * w_ref[...])
        mo_ref[...] = mn
        vo_ref[...] = vn

    spec = pl.BlockSpec((tr, c), lambda i: (i, 0))
    shape = jax.ShapeDtypeStruct((r, c), F32)
    return _pallas(
        body, name=name, grid=(r // tr,),
        in_specs=[spec] * 4, out_specs=[spec] * 3, out_shape=[shape] * 3,
        args=(w, g, m, v), sem=("parallel",))[0]


def _place():
    x, y, c = lax.axis_index("x"), lax.axis_index("y"), lax.axis_index("c")
    chips = [(1 - x, y), (x, 1 - y), (1 - x, 1 - y)]
    return x, y, c, chips


def _remote(src, dst, send, recv, sem, to):
    return pltpu.make_async_remote_copy(
        src_ref=src, dst_ref=dst, send_sem=send.at[sem], recv_sem=recv.at[sem], device_id=to, device_id_type=MESH)


def _gather_plan(fulls, kinds, mid_at=None, parts=None):
    parts = parts or [(0, 1)] * len(fulls)

    def region(it, f, k, cc):
        kind = kinds[it]
        p, n = parts[it]
        if kind == SMALL:
            return f.at[k, pl.ds(cc * (CONV_PACK_ROWS // 2), CONV_PACK_ROWS // 2), :]
        if COL_SHARDED[kind]:
            rows, cols = f.shape[0] // (2 * n), f.shape[1] // N_CHIP
            return f.at[pl.ds((cc * n + p) * rows, rows), pl.ds(k * cols, cols)]
        assert n == 1
        rows = f.shape[0] // N_CHIP
        return f.at[pl.ds(k * rows + cc * (rows // 2), rows // 2), :]

    def first_hop(bufs, send, recv, it, j):
        x, y, c, chips = _place()
        reg = region(it, bufs[it], 2 * x + y, c)
        return _remote(reg, reg, send, recv, it * 6 + j, (*chips[j], c))

    def arrival(bufs, send, recv, it, j, second):
        x, y, c, chips = _place()
        px, py = chips[j]
        reg = region(it, bufs[it], 2 * px + py, 1 - c if second else c)
        to = (x, y, 1 - c) if second else (px, py, c)
        return _remote(reg, reg, send, recv, it * 6 + (3 + j if second else j), to)

    def forward(bufs, send, recv, it, j):
        x, y, c, chips = _place()
        px, py = chips[j]
        reg = region(it, bufs[it], 2 * px + py, c)
        return _remote(reg, reg, send, recv, it * 6 + 3 + j, (x, y, 1 - c))

    def start(srcs, bufs, outs, send, recv):
        for it in range(len(bufs)):
            for j in range(3):
                first_hop(bufs, send, recv, it, j).start()

    def mid(srcs, bufs, outs, send, recv):
        for it in range(len(bufs)):
            for j in range(3):
                arrival(bufs, send, recv, it, j, False).wait_recv()
                forward(bufs, send, recv, it, j).start()

    def finish(srcs, bufs, outs, send, recv):
        for it in range(len(bufs)):
            for j in range(3):
                arrival(bufs, send, recv, it, j, True).wait_recv()
        for it in range(len(bufs)):
            for j in range(3):
                first_hop(bufs, send, recv, it, j).wait_send()
                forward(bufs, send, recv, it, j).wait_send()

    return Comm(srcs=(), bufs=tuple(fulls), outs=(), n_sem=6 * len(fulls), start=start, mid=mid, finish=finish,
                mid_at=mid_at)


def _half_axis(kind):
    return 0 if kind == SMALL or COL_SHARDED[kind] else 1


def _half2(ref, kind, cc):
    if _half_axis(kind) == 0:
        return ref.at[pl.ds(cc * (ref.shape[0] // 2), ref.shape[0] // 2), :]
    return ref.at[:, pl.ds(cc * (ref.shape[1] // 2), ref.shape[1] // 2)]


def _pair_plan(grads, kinds):
    def land_shape(g, kind):
        s = list(g.shape)
        s[_half_axis(kind)] //= 2
        return jax.ShapeDtypeStruct(tuple(s), F32)

    def copy(srcs, outs, send, recv, it):
        x, y, c, _ = _place()
        return _remote(_half2(srcs[it], kinds[it], 1 - c), outs[it], send, recv, it, (x, y, 1 - c))

    def start(srcs, bufs, outs, send, recv):
        for it in range(len(srcs)):
            copy(srcs, outs, send, recv, it).start()

    def finish(srcs, bufs, outs, send, recv):
        for it in range(len(srcs)):
            copy(srcs, outs, send, recv, it).wait_send()
        for it in range(len(srcs)):
            copy(srcs, outs, send, recv, it).wait_recv()

    return Comm(srcs=tuple(grads), bufs=(), outs=tuple(land_shape(g, k) for g, k in zip(grads, kinds)),
                n_sem=len(grads), start=start, finish=finish)


def _scatter_plan(parts, slots, kinds):
    def piece(s, kind, k):
        if kind == SMALL:
            return s
        if COL_SHARDED[kind]:
            n = s.shape[1] // N_CHIP
            return s.at[:, pl.ds(k * n, n)]
        n = s.shape[0] // N_CHIP
        return s.at[pl.ds(k * n, n), :]

    def outbound(srcs, bufs, send, recv, it, j):
        x, y, c, chips = _place()
        px, py = chips[j]
        return _remote(piece(srcs[it], kinds[it], 2 * px + py), bufs[it].at[2 * x + y], send, recv, it * 3 + j,
                       (px, py, c))

    def inbound(bufs, send, recv, it, j):
        x, y, c, chips = _place()
        px, py = chips[j]
        got = bufs[it].at[2 * px + py]
        return _remote(got, got, send, recv, it * 3 + j, (px, py, c))

    def start(srcs, bufs, outs, send, recv):
        for it in range(len(srcs)):
            for j in range(3):
                outbound(srcs, bufs, send, recv, it, j).start()

    def finish(srcs, bufs, outs, send, recv):
        for it in range(len(srcs)):
            for j in range(3):
                inbound(bufs, send, recv, it, j).wait_recv()
        for it in range(len(srcs)):
            for j in range(3):
                outbound(srcs, bufs, send, recv, it, j).wait_send()

    return Comm(srcs=tuple(parts), bufs=tuple(slots), outs=(), n_sem=3 * len(parts), start=start, finish=finish)


def _share_plan(fulls, kinds, layer):
    def half(f, kind, cc):
        return _half2(f if kind == SMALL else f.at[layer], kind, cc)

    def copy(bufs, send, recv, it, cc):
        x, y, c, _ = _place()
        reg = half(bufs[it], kinds[it], c if cc == "mine" else 1 - c)
        return _remote(reg, reg, send, recv, it, (x, y, 1 - c))

    def start(srcs, bufs, outs, send, recv):
        for it in range(len(bufs)):
            copy(bufs, send, recv, it, "mine").start()

    def finish(srcs, bufs, outs, send, recv):
        for it in range(len(bufs)):
            copy(bufs, send, recv, it, "other").wait_recv()
        for it in range(len(bufs)):
            copy(bufs, send, recv, it, "mine").wait_send()

    return Comm(srcs=(), bufs=tuple(fulls), outs=(), n_sem=len(fulls), start=start, finish=finish)


def _pair_sum(g, land, idx, kind, name):
    odt = F32 if kind == SMALL else BF16
    r, cdim = land.shape

    def body(idx_ref, g_ref, l_ref, p_ref, s_ref):
        v = (g_ref[...] + l_ref[...]).astype(odt)
        p_ref[...] = v
        if kind == SMALL:
            s_ref[...] = v
        else:
            @pl.when(pl.program_id(1 if COL_SHARDED[kind] else 0) == idx_ref[1])
            def _():
                s_ref[...] = v

    if kind == SMALL:
        grid = (1,)
        g_spec = pl.BlockSpec((r, LANES), lambda i, idx_ref: (idx_ref[0], 0))
        spec = pl.BlockSpec((r, LANES), lambda i, idx_ref: (0, 0))
        s_spec = pl.BlockSpec((None, r, LANES), lambda i, idx_ref: (idx_ref[1], 0, 0))
        s_shape = (N_CHIP, r, LANES)
    elif COL_SHARDED[kind]:
        pc, tr = cdim // N_CHIP, 256
        nrb = r // tr
        grid = (nrb, N_CHIP)
        g_spec = pl.BlockSpec((tr, pc), lambda i, k, idx_ref: (idx_ref[0] * nrb + i, k))
        spec = pl.BlockSpec((tr, pc), lambda i, k, idx_ref: (i, k))
        s_spec = pl.BlockSpec((None, tr, pc), lambda i, k, idx_ref: (idx_ref[1], i, 0))
        s_shape = (N_CHIP, r, pc)
    else:
        pr = r // N_CHIP
        grid = (N_CHIP,)
        g_spec = pl.BlockSpec((pr, cdim), lambda k, idx_ref: (k, idx_ref[0]))
        spec = pl.BlockSpec((pr, cdim), lambda k, idx_ref: (k, 0))
        s_spec = pl.BlockSpec((None, pr, cdim), lambda k, idx_ref: (idx_ref[1], 0, 0))
        s_shape = (N_CHIP, pr, cdim)
    return pl.pallas_call(
        body, name=name,
        grid_spec=pltpu.PrefetchScalarGridSpec(
            num_scalar_prefetch=1, grid=grid, in_specs=[g_spec, spec], out_specs=[spec, s_spec]),
        out_shape=[jax.ShapeDtypeStruct(land.shape, odt), jax.ShapeDtypeStruct(s_shape, odt)],
        compiler_params=_cp(*(["arbitrary"] * len(grid))),
    )(idx, g, land)


def _sum_slots(slots, idx, kind, layer, prev, name):
    _, r, cdim = slots.shape

    def body(*refs):
        s_ref, o_ref = refs[1], refs[-1]
        v = s_ref[...].astype(F32)
        o_ref[...] = (v[0] + v[1]) + (v[2] + v[3])

    if kind == SMALL:
        grid = (1,)
        s_spec = pl.BlockSpec((N_CHIP, r, cdim), lambda i, idx_ref: (0, 0, 0))
        o_spec = pl.BlockSpec((r, cdim), lambda i, idx_ref: (idx_ref[0], 0))
        full = (2 * r, cdim)
    else:
        tr = 256 if r % 256 == 0 else 384
        nrb = r // tr
        grid = (nrb,)
        s_spec = pl.BlockSpec((N_CHIP, tr, cdim), lambda i, idx_ref: (0, i, 0))
        if COL_SHARDED[kind]:
            o_spec = pl.BlockSpec((None, tr, cdim), lambda i, idx_ref: (layer, idx_ref[0] * nrb + i, 0))
            full = (2, 2 * r, cdim)
        else:
            o_spec = pl.BlockSpec((None, tr, cdim), lambda i, idx_ref: (layer, i, idx_ref[0]))
            full = (2, r, 2 * cdim)
    in_specs, args, aliases = [s_spec], [idx, slots], {}
    if prev is not None:
        in_specs.append(ANY)
        args.append(prev)
        aliases = {2: 0}
    return pl.pallas_call(
        body, name=name,
        grid_spec=pltpu.PrefetchScalarGridSpec(
            num_scalar_prefetch=1, grid=grid, in_specs=in_specs, out_specs=o_spec),
        out_shape=jax.ShapeDtypeStruct(full, F32),
        input_output_aliases=aliases,
        compiler_params=_cp(*(["parallel"] * len(grid))),
    )(*args)


def _block_diag(w):
    w4 = w.reshape(4, 4, 64, 64)
    eye = jnp.eye(4, dtype=w.dtype)[None, :, None, :, None]
    return (w4[:, :, :, None, :] * eye).reshape(4, 256, 256)


def _block_diag_extract(d):
    d5 = d.reshape(4, 4, 64, 4, 64)
    return jnp.stack([d5[:, hh, :, hh, :] for hh in range(4)], axis=1).reshape(16, 64, 64)


def _rows8(a):
    return jnp.pad(a, ((0, SUBLANES - a.shape[0]), (0, 0)))


def _pack_rep(norm1_g, conv_b, ba, bx, lam, norm2_g, wa, wx, final_g):
    parts = []
    for l in range(2):
        parts += [norm1_g[l], conv_b[l], ba[l], bx[l], lam[l], norm2_g[l], wa[l].reshape(-1), wx[l].reshape(-1)]
    parts.append(final_g)
    return jnp.concatenate(parts).reshape(REP_ROWS, LANES)


def _unpack_rep(buf):
    flat = buf.reshape(-1)
    out = {k: [] for k in ("norm1_g", "lru_conv_b", "lru_ba", "lru_bx", "lru_lambda", "norm2_g", "lru_wa", "lru_wx")}
    for l in range(2):
        o = l * REP_LAYER
        for i, k in enumerate(("norm1_g", "lru_conv_b", "lru_ba", "lru_bx", "lru_lambda", "norm2_g")):
            out[k].append(flat[o + i * 1024:o + (i + 1) * 1024])
        o += 6 * 1024
        out["lru_wa"].append(flat[o:o + 65536].reshape(16, 64, 64))
        out["lru_wx"].append(flat[o + 65536:o + 131072].reshape(16, 64, 64))
    res = {k: jnp.stack(v) for k, v in out.items()}
    res["final_g"] = flat[2 * REP_LAYER:2 * REP_LAYER + 1024]
    return res


def _pack_conv_shard(lru_cw, sc_cw, ffn_cw):
    return jnp.concatenate([lru_cw.reshape(16, LANES), jnp.pad(sc_cw.reshape(6, LANES), ((0, 2), (0, 0))),
                            ffn_cw.reshape(72, LANES)], axis=0)


def _unpack_conv_shard(buf):
    return (buf[0:16].reshape(2, 4, 256), buf[16:22].reshape(2, 3, 128), buf[24:96].reshape(2, 3, 1536))


def kernel(x, norm1_g, w_in, lru_conv_w, lru_conv_b, lru_wa, lru_ba, lru_wx, lru_bx, lru_lambda, sc_conv_w, w_out, norm2_g, w_up, ffn_conv_w, w_down, final_g, loss_target, m_norm1_g, m_w_in, m_lru_conv_w, m_lru_conv_b, m_lru_wa, m_lru_ba, m_lru_wx, m_lru_bx, m_lru_lambda, m_sc_conv_w, m_w_out, m_norm2_g, m_w_up, m_ffn_conv_w, m_w_down, m_final_g, v_norm1_g, v_w_in, v_lru_conv_w, v_lru_conv_b, v_lru_wa, v_lru_ba, v_lru_wx, v_lru_bx, v_lru_lambda, v_sc_conv_w, v_w_out, v_norm2_g, v_w_up, v_ffn_conv_w, v_w_down, v_final_g):
    me = 2 * lax.axis_index("x") + lax.axis_index("y")
    idx = jnp.stack([lax.axis_index("c"), me]).astype(jnp.int32)
    t_len = x.shape[1]

    s_conv = _pack_conv_shard(lru_conv_w, sc_conv_w, ffn_conv_w)
    conv_slots = lax.dynamic_update_slice(jnp.zeros((N_CHIP, CONV_PACK_ROWS, LANES), F32), s_conv[None], (me, 0, 0))
    wi = list(_cast_into_full(w_in, W_IN, idx, "cast_w_in"))
    wo = list(_cast_into_full(w_out, W_OUT, idx, "cast_w_out"))
    wu = list(_cast_into_full(w_up, W_UP, idx, "cast_w_up"))
    wd = list(_cast_into_full(w_down, W_DOWN, idx, "cast_w_down"))
    wi[0], convs = _comm_call(_gather_plan([wi[0], conv_slots], [W_IN, SMALL]), "ag_first")
    per_chip = [_unpack_conv_shard(convs[k]) for k in range(N_CHIP)]
    lru_cw = jnp.concatenate([p[0] for p in per_chip], axis=-1)
    sc_cw = jnp.concatenate([p[1] for p in per_chip], axis=-1)
    ffn_cw = jnp.concatenate([p[2] for p in per_chip], axis=-1)

    cw8 = [_rows8(lru_cw[l]) for l in range(2)]
    vec8 = [_rows8(jnp.stack([lru_conv_b[l], lru_ba[l], lru_bx[l], lru_lambda[l]])) for l in range(2)]
    wa_bd = [_block_diag(lru_wa[l]).astype(BF16) for l in range(2)]
    wx_bd = [_block_diag(lru_wx[l]).astype(BF16) for l in range(2)]
    scw8 = [_rows8(sc_cw[l]) for l in range(2)]
    fw8 = [jnp.pad(ffn_cw[l].reshape(3, 2, D_FF).transpose(1, 0, 2), ((0, 0), (0, 5), (0, 0))) for l in range(2)]

    xs = x[0]
    saved = []
    n512, n256 = t_len // 512, t_len // 256
    whole, lower, upper = (0, 1), (0, 2), (1, 2)
    carried_by = {
        "fwd_in_0": ([(wu, 0, W_UP, lower)], (max(n512 - 2, 0),)),
        "fwd_mixer_0": ([(wu, 0, W_UP, upper), (wo, 0, W_OUT, whole)], (max(n256 - 3, 0),)),
        "fwd_up_0": ([(wd, 0, W_DOWN, whole)], (max(n512 - 2, 0),)),
        "fwd_act_0": ([(wi, 1, W_IN, whole), (wo, 1, W_OUT, whole), (wu, 1, W_UP, lower)], (n256 - 1, 0)),
        "fwd_down_0": ([(wu, 1, W_UP, upper)], (n512 - 1,)),
        "fwd_in_1": ([(wd, 1, W_DOWN, whole)], (n512 - 1,)),
    }

    def carried(name):
        if name not in carried_by:
            return None, lambda got: None
        items, mid_at = carried_by[name]

        def store(got):
            for (lst, i, _, _), arr in zip(items, got):
                lst[i] = arr

        return _gather_plan([lst[i] for lst, i, _, _ in items], [k for _, _, k, _ in items], mid_at=mid_at,
                            parts=[p for _, _, _, p in items]), store

    for l in range(2):
        comm, store = carried(f"fwd_in_{l}")
        (z, h1), got = _norm_mm(xs, norm1_g[l][None], wi[l], f"fwd_in_{l}", comm=comm)
        store(got)
        comm, store = carried(f"fwd_mixer_{l}")
        (ymix, hst), got = _mixer_fwd(z, cw8[l], vec8[l], wa_bd[l], wx_bd[l], scw8[l], f"fwd_mixer_{l}", comm=comm)
        store(got)
        (x2,), _ = _mm_res(ymix, wo[l], xs, f"fwd_out_{l}")
        comm, store = carried(f"fwd_up_{l}")
        (u, h2), got = _norm_mm(x2, norm2_g[l][None], wu[l], f"fwd_up_{l}", planes=True, comm=comm)
        store(got)
        comm, store = carried(f"fwd_act_{l}")
        (act,), got = _ffn_act(u, fw8[l], f"fwd_act_{l}", comm=comm)
        store(got)
        comm, store = carried(f"fwd_down_{l}")
        (x3,), got = _mm_res(act, wd[l], x2, f"fwd_down_{l}", comm=comm)
        store(got)
        saved.append((xs, h1, z, hst, ymix, x2, h2, u, act))
        xs = x3

    dx, dxb, dgf, loss_blk = _loss_head(xs, final_g[None], loss_target[0], "loss_head")
    loss = lax.psum(loss_blk[0, 0], ("x", "y", "c"))

    kinds = [W_IN, W_OUT, W_UP, W_DOWN]
    grads = [None, None]
    small = [None, None]
    reduced = [None] * 4
    lands = parts = slots = None
    for l in (1, 0):
        x_in, h1, z, hst, ymix, x2, h2, u, act = saved[l]
        carry = l == 0
        comm = _pair_plan(grads[1], kinds) if carry else None
        (g_down,), got = _mm_tn(act, dxb, f"bwd_wdown_{l}", tk=1536, tn=1024, comm=comm)
        if carry:
            summed = [_pair_sum(grads[1][w], got[w], idx, kinds[w], f"rs_add1_{w}") for w in range(4)]
            parts, slots = [s[0] for s in summed], [s[1] for s in summed]
        (dact,), _ = _mm_nt(dxb, wd[l], f"bwd_dact_{l}")
        comm = _scatter_plan(parts, slots, kinds) if carry else None
        (du, dfw), got = _ffn_bwd(dact, u, fw8[l], f"bwd_act_{l}", comm=comm)
        if carry:
            reduced = [_sum_slots(got[w], idx, kinds[w], 1, None, f"rs_sum1_{w}") for w in range(4)]
        comm = _share_plan(reduced, kinds, 1) if carry else None
        (g_up,), got = _mm_tn(h2, du, f"bwd_wup_{l}", tk=1024, tn=1536, planes=True, comm=comm)
        if carry:
            reduced = list(got)
        comm = _pair_plan([g_up, g_down], [W_UP, W_DOWN]) if carry else None
        (dx2, dx2b, dg2), got = _mm_nt_normbwd(du, wu[l], x2, norm2_g[l][None], dx, f"bwd_up_{l}", planes=True,
                                               comm=comm)
        if carry:
            sum_up = _pair_sum(g_up, got[0], idx, W_UP, "rs_add0_2")
            sum_down = _pair_sum(g_down, got[1], idx, W_DOWN, "rs_add0_3")
        (g_out,), _ = _mm_tn(ymix, dx2b, f"bwd_wout_{l}", tk=1536, tn=1024)
        comm = _pair_plan([g_out], [W_OUT]) if carry else None
        (dymix,), got = _mm_nt(dx2b, wo[l], f"bwd_dymix_{l}", comm=comm)
        if carry:
            sum_out = _pair_sum(g_out, got[0], idx, W_OUT, "rs_add0_1")
            comm = _scatter_plan([sum_out[0], sum_up[0], sum_down[0]], [sum_out[1], sum_up[1], sum_down[1]],
                                 [W_OUT, W_UP, W_DOWN])
        (dz, dcw, dvec, dwa, dwx, dscw), got = _mixer_bwd(z, hst, dymix, cw8[l], vec8[l], wa_bd[l], wx_bd[l],
                                                        scw8[l], f"bwd_mixer_{l}", comm=comm)
        if carry:
            for w, s in zip((W_OUT, W_UP, W_DOWN), got):
                reduced[w] = _sum_slots(s, idx, w, 0, reduced[w], f"rs_sum0_{w}")
        (g_in,), _ = _mm_tn(h1, dz, f"bwd_win_{l}", tk=1024, tn=1792)
        comm = _pair_plan([g_in], [W_IN]) if carry else None
        (dx, dxb, dg1), got = _mm_nt_normbwd(dz, wi[l], x_in, norm1_g[l][None], dx2, f"bwd_in_{l}", comm=comm)
        if carry:
            sum_in = _pair_sum(g_in, got[0], idx, W_IN, "rs_add0_0")
        grads[l] = [g_in, g_out, g_up, g_down]
        rep = [dg1[0], dvec[0], dvec[1], dvec[2], dvec[3], dg2[0],
               _block_diag_extract(dwa).reshape(-1), _block_diag_extract(dwx).reshape(-1)]
        conv = [dcw[0:4].reshape(-1), jnp.pad(dscw[0:3].reshape(-1), (0, 512)),
                dfw[:, 0:3, :].transpose(1, 0, 2).reshape(-1)]
        small[l] = (jnp.concatenate(rep), jnp.concatenate(conv))
    grad_x = dx[None]
    g_small = jnp.concatenate([small[0][0], small[1][0], dgf[0], small[0][1], small[1][1],
                               jnp.zeros((8 * LANES,), F32)]).reshape(SMALL_ROWS, LANES)

    land_small, = _comm_call(_pair_plan([g_small], [SMALL]), "rs_pair_small")
    sum_small = _pair_sum(g_small, land_small, idx, SMALL, "rs_add0_4")
    slot_in, slot_small = _comm_call(
        _scatter_plan([sum_in[0], sum_small[0]], [sum_in[1], sum_small[1]], [W_IN, SMALL]), "rs_scatter_last")
    reduced[W_IN] = _sum_slots(slot_in, idx, W_IN, 0, reduced[W_IN], "rs_sum0_0")
    reduced.append(_sum_slots(slot_small, idx, SMALL, 0, None, "rs_sum0_4"))
    gw_in, gw_out, gw_up, gw_down, gs = _comm_call(_share_plan(reduced, kinds + [SMALL], 0), "rs_share0")

    g_rep = gs[0:REP_ROWS]
    g_conv = gs[REP_ROWS:REP_ROWS + CONV_ROWS].reshape(2, CONV_LAYER)
    g_lru_cw = lax.dynamic_slice_in_dim(g_conv[:, 0:4096].reshape(2, 4, 1024), me * 256, 256, axis=2)
    g_sc_cw = lax.dynamic_slice_in_dim(g_conv[:, 4096:4096 + 1536].reshape(2, 3, 512), me * 128, 128, axis=2)
    g_ffn_cw = lax.dynamic_slice_in_dim(g_conv[:, 6144:].reshape(2, 3, 6144), me * 1536, 1536, axis=2)

    def big(w, g, m, v, name):
        shape = w.shape
        two_d = lambda a: a.reshape(-1, shape[-1])
        return [o.reshape(shape) for o in _adamw(two_d(w), two_d(g), two_d(m), two_d(v), name)]

    upd = {"w_in": big(w_in, gw_in, m_w_in, v_w_in, "adamw_w_in"),
           "w_out": big(w_out, gw_out, m_w_out, v_w_out, "adamw_w_out"),
           "w_up": big(w_up, gw_up, m_w_up, v_w_up, "adamw_w_up"),
           "w_down": big(w_down, gw_down, m_w_down, v_w_down, "adamw_w_down")}
    rep_out = _adamw(
        _pack_rep(norm1_g, lru_conv_b, lru_ba, lru_bx, lru_lambda, norm2_g, lru_wa, lru_wx, final_g), g_rep,
        _pack_rep(m_norm1_g, m_lru_conv_b, m_lru_ba, m_lru_bx, m_lru_lambda, m_norm2_g, m_lru_wa, m_lru_wx, m_final_g),
        _pack_rep(v_norm1_g, v_lru_conv_b, v_lru_ba, v_lru_bx, v_lru_lambda, v_norm2_g, v_lru_wa, v_lru_wx, v_final_g),
        "adamw_rep")
    conv_out = _adamw(s_conv, _pack_conv_shard(g_lru_cw, g_sc_cw, g_ffn_cw),
                      _pack_conv_shard(m_lru_conv_w, m_sc_conv_w, m_ffn_conv_w),
                      _pack_conv_shard(v_lru_conv_w, v_sc_conv_w, v_ffn_conv_w), "adamw_conv")

    names = ["norm1_g", "w_in", "lru_conv_w", "lru_conv_b", "lru_wa", "lru_ba", "lru_wx", "lru_bx", "lru_lambda",
             "sc_conv_w", "w_out", "norm2_g", "w_up", "ffn_conv_w", "w_down", "final_g"]
    groups = []
    g_all = dict(_unpack_rep(g_rep))
    g_all.update(w_in=gw_in, w_out=gw_out, w_up=gw_up, w_down=gw_down,
                 lru_conv_w=g_lru_cw, sc_conv_w=g_sc_cw, ffn_conv_w=g_ffn_cw)
    groups.append(g_all)
    for i in range(3):
        d = dict(_unpack_rep(rep_out[i]))
        cl, cs, cf = _unpack_conv_shard(conv_out[i])
        d.update(lru_conv_w=cl, sc_conv_w=cs, ffn_conv_w=cf)
        d.update({k: v[i] for k, v in upd.items()})
        groups.append(d)
    return (loss, grad_x, *[grp[n] for grp in groups for n in names])
```

```python
import dataclasses
import functools
import math
import operator
from typing import Any, Callable, Optional, Sequence

import jax
import jax.numpy as jnp
from jax import lax
from jax.experimental import pallas as pl
from jax.experimental.pallas import tpu as pltpu

F32 = jnp.float32
BF16 = jnp.bfloat16
MESH = pl.DeviceIdType.MESH

D_MODEL = 1024
D_LRU = 1024
D_SC = 512
D_MIX = D_LRU + D_SC
D_IN = 2 * D_LRU + 3 * D_SC
D_FF = 3072
N_CHIP = 4
RG_C = 8.0
EPS = 1e-6
ADAM_LR = 0.001
ADAM_B1 = 0.9
ADAM_B2 = 0.999
ADAM_EPS = 1e-08
ADAM_WD = 0.01
ADAM_STEP = 10

SUBLANES = 8
PACKED = 16
LANES = 128
VMEM_LIMIT = 56 * 1024 * 1024
GELU_C0 = math.sqrt(2.0 / math.pi)
GELU_C1 = 0.044715

REP_LAYER = 6 * 1024 + 2 * 16 * 64 * 64
REP_ROWS = (2 * REP_LAYER + 1024) // LANES
CONV_LAYER = 4 * 1024 + 2048 + 3 * 6144
CONV_ROWS = 2 * CONV_LAYER // LANES
SMALL_ROWS = REP_ROWS + CONV_ROWS + 8
CONV_PACK_ROWS = 96

W_IN, W_OUT, W_UP, W_DOWN, SMALL = range(5)
COL_SHARDED = {W_IN: True, W_OUT: False, W_UP: True, W_DOWN: False}

ONCE = pl.Buffered(1)
ANY = pl.BlockSpec(memory_space=pl.ANY)


def _cp(*sem):
    return pltpu.CompilerParams(dimension_semantics=sem, vmem_limit_bytes=VMEM_LIMIT)


@dataclasses.dataclass
class Comm:
    srcs: Sequence[Any]
    bufs: Sequence[Any]
    outs: Sequence[Any]
    n_sem: int
    start: Callable
    finish: Callable
    mid: Optional[Callable] = None
    mid_at: Optional[Sequence[int]] = None


def _pallas(body, *, name, grid, in_specs, out_specs, out_shape, args, sem, scratch_shapes=(), comm=None):
    if comm is None:
        res = pl.pallas_call(
            body, name=name, grid=grid, in_specs=list(in_specs), out_specs=list(out_specs),
            out_shape=list(out_shape), scratch_shapes=list(scratch_shapes), compiler_params=_cp(*sem))(*args)
        return tuple(res), ()
    n_in, n_out, n_scr = len(in_specs), len(out_specs), len(scratch_shapes)
    ns, nb, no = len(comm.srcs), len(comm.bufs), len(comm.outs)

    def carrier(*refs):
        p = 0
        main_in = refs[p:p + n_in]
        p += n_in
        c_src = refs[p:p + ns]
        p += ns + nb
        main_out = refs[p:p + n_out]
        p += n_out
        c_buf = refs[p:p + nb]
        p += nb
        c_out = refs[p:p + no]
        p += no
        scr = refs[p:p + n_scr]
        send, recv = refs[p + n_scr], refs[p + n_scr + 1]
        ids = [pl.program_id(a) for a in range(len(grid))]

        def at(steps):
            return functools.reduce(operator.and_, [i == s for i, s in zip(ids, steps)])

        @pl.when(at([0] * len(grid)))
        def _():
            comm.start(c_src, c_buf, c_out, send, recv)

        if comm.mid is not None:
            @pl.when(at(comm.mid_at))
            def _():
                comm.mid(c_src, c_buf, c_out, send, recv)

        body(*main_in, *main_out, *scr)

        @pl.when(at([g - 1 for g in grid]))
        def _():
            comm.finish(c_src, c_buf, c_out, send, recv)

    res = pl.pallas_call(
        carrier, name=name, grid=grid,
        in_specs=list(in_specs) + [ANY] * (ns + nb),
        out_specs=list(out_specs) + [ANY] * (nb + no),
        out_shape=list(out_shape) + [jax.ShapeDtypeStruct(b.shape, b.dtype) for b in comm.bufs] + list(comm.outs),
        input_output_aliases={n_in + ns + j: n_out + j for j in range(nb)},
        scratch_shapes=list(scratch_shapes) + [pltpu.SemaphoreType.DMA((comm.n_sem,)),
                                               pltpu.SemaphoreType.DMA((comm.n_sem,))],
        compiler_params=_cp(*(["arbitrary"] * len(grid))),
    )(*args, *comm.srcs, *comm.bufs)
    return tuple(res[:n_out]), tuple(res[n_out:])


def _comm_call(comm, name):
    ns, nb, no = len(comm.srcs), len(comm.bufs), len(comm.outs)

    def body(*refs):
        c_src = refs[0:ns]
        c_buf = refs[ns + nb:ns + 2 * nb]
        c_out = refs[ns + 2 * nb:ns + 2 * nb + no]
        send, recv = refs[ns + 2 * nb + no], refs[ns + 2 * nb + no + 1]
        comm.start(c_src, c_buf, c_out, send, recv)
        if comm.mid is not None:
            comm.mid(c_src, c_buf, c_out, send, recv)
        comm.finish(c_src, c_buf, c_out, send, recv)

    return tuple(pl.pallas_call(
        body, name=name,
        in_specs=[ANY] * (ns + nb), out_specs=[ANY] * (nb + no),
        out_shape=[jax.ShapeDtypeStruct(b.shape, b.dtype) for b in comm.bufs] + list(comm.outs),
        input_output_aliases={ns + j: j for j in range(nb)},
        scratch_shapes=[pltpu.SemaphoreType.DMA((comm.n_sem,)), pltpu.SemaphoreType.DMA((comm.n_sem,))],
    )(*comm.srcs, *comm.bufs))


def _sigmoid(v):
    return 1.0 / (1.0 + jnp.exp(-v))


def _sigmoid_tanh(v):
    return 0.5 + 0.5 * jnp.tanh(0.5 * v)


def _gelu_parts(v):
    v2 = v * v
    t = jnp.tanh(GELU_C0 * v * (1.0 + GELU_C1 * v2))
    half = 0.5 * (1.0 + t)
    gel = v * half
    dgel = half + 0.5 * v * (1.0 - t * t) * (GELU_C0 * (1.0 + 3.0 * GELU_C1 * v2))
    return gel, dgel


def _gelu(v):
    t = jnp.tanh(GELU_C0 * v * (1.0 + GELU_C1 * (v * v)))
    return 0.5 * v * (1.0 + t)


def _neg_expm1(y, a):
    p = jnp.full_like(y, 1.0 / 120.0)
    for coef in (1.0 / 24.0, 1.0 / 6.0, 0.5, 1.0):
        p = p * y + coef
    return jnp.where(y > -0.1, -(p * y), 1.0 - a * a)


def _softplus_neg(lam):
    nl = -lam
    e = jnp.exp(-jnp.abs(nl))
    u = 1.0 + e
    l1p = jnp.where(u == 1.0, e, jnp.log(u) * e / (u - 1.0))
    return jnp.maximum(nl, 0.0) + l1p


def _conv_taps(ext, taps, n_out):
    kw = len(taps)
    acc = taps[kw - 1] * ext[SUBLANES:SUBLANES + n_out]
    for k in range(kw - 1):
        acc = acc + taps[k] * pltpu.roll(ext, kw - 1 - k, axis=0)[SUBLANES:SUBLANES + n_out]
    return acc


def _conv_taps_t(ext, taps, n_out):
    kw = len(taps)
    n = ext.shape[0]
    acc = taps[kw - 1] * ext[0:n_out]
    for k in range(kw - 1):
        acc = acc + taps[k] * pltpu.roll(ext, n - (kw - 1 - k), axis=0)[0:n_out]
    return acc


def _scan8(a, b, carry, row):
    for s in (1, 2, 4):
        m = row >= s
        a_sh = jnp.where(m, pltpu.roll(a, s, axis=0), 1.0)
        b_sh = jnp.where(m, pltpu.roll(b, s, axis=0), 0.0)
        b = a * b_sh + b
        a = a * a_sh
    return a * carry + b


def _scan8_rev(a, b, carry, row):
    for s in (1, 2, 4):
        m = row < SUBLANES - s
        a_sh = jnp.where(m, pltpu.roll(a, SUBLANES - s, axis=0), 1.0)
        b_sh = jnp.where(m, pltpu.roll(b, SUBLANES - s, axis=0), 0.0)
        b = a * b_sh + b
        a = a * a_sh
    return a * carry + b


def _cast_into_full(w, kind, idx, name):
    nl, r, c = w.shape
    tr = 256 if r % 256 == 0 else r
    nrb = r // tr

    def body(idx_ref, w_ref, o0_ref, o1_ref):
        o0_ref[...] = w_ref[0].astype(BF16)
        o1_ref[...] = w_ref[1].astype(BF16)

    if COL_SHARDED[kind]:
        full = (r, N_CHIP * c)
        o_spec = pl.BlockSpec((tr, c), lambda i, idx_ref: (i, idx_ref[1]))
    else:
        full = (N_CHIP * r, c)
        o_spec = pl.BlockSpec((tr, c), lambda i, idx_ref: (idx_ref[1] * nrb + i, 0))
    return pl.pallas_call(
        body, name=name,
        grid_spec=pltpu.PrefetchScalarGridSpec(
            num_scalar_prefetch=1, grid=(nrb,),
            in_specs=[pl.BlockSpec((nl, tr, c), lambda i, idx_ref: (0, i, 0))], out_specs=[o_spec, o_spec]),
        out_shape=[jax.ShapeDtypeStruct(full, BF16)] * 2,
        compiler_params=_cp("parallel"),
    )(idx, w)


def _norm_mm(x, g, w, name, planes=False, tm=512, tn=512, comm=None):
    t_len, d = x.shape
    n = w.shape[1]
    half = n // 2

    def body(x_ref, g_ref, w_ref, z_ref, h_ref):
        xv = x_ref[...]
        r = lax.rsqrt(jnp.mean(xv * xv, axis=-1, keepdims=True) + EPS)
        h_ref[...] = ((xv * r) * g_ref[...]).astype(BF16)
        for n0 in range(0, n, tn):
            blk = jnp.dot(h_ref[...], w_ref[:, n0:n0 + tn], preferred_element_type=F32).astype(BF16)
            if planes:
                z_ref[n0 // half, :, n0 % half:n0 % half + tn] = blk
            else:
                z_ref[:, n0:n0 + tn] = blk

    if planes:
        z_shape = jax.ShapeDtypeStruct((2, t_len, half), BF16)
        z_spec = pl.BlockSpec((2, tm, half), lambda i: (0, i, 0))
    else:
        z_shape = jax.ShapeDtypeStruct((t_len, n), BF16)
        z_spec = pl.BlockSpec((tm, n), lambda i: (i, 0))
    return _pallas(
        body, name=name, grid=(t_len // tm,),
        in_specs=[pl.BlockSpec((tm, d), lambda i: (i, 0)),
                  pl.BlockSpec((1, d), lambda i: (0, 0)),
                  pl.BlockSpec((d, n), lambda i: (0, 0), pipeline_mode=ONCE)],
        out_specs=[z_spec, pl.BlockSpec((tm, d), lambda i: (i, 0))],
        out_shape=[z_shape, jax.ShapeDtypeStruct((t_len, d), BF16)],
        args=(x, g, w), sem=("parallel",), comm=comm)


def _mm_res(a, w, res, name, tm=512, comm=None):
    t_len, k = a.shape
    n = w.shape[1]

    def body(a_ref, w_ref, r_ref, o_ref):
        o_ref[...] = r_ref[...] + jnp.dot(a_ref[...], w_ref[...], preferred_element_type=F32)

    return _pallas(
        body, name=name, grid=(t_len // tm,),
        in_specs=[pl.BlockSpec((tm, k), lambda i: (i, 0)),
                  pl.BlockSpec((k, n), lambda i: (0, 0), pipeline_mode=ONCE),
                  pl.BlockSpec((tm, n), lambda i: (i, 0))],
        out_specs=[pl.BlockSpec((tm, n), lambda i: (i, 0))],
        out_shape=[jax.ShapeDtypeStruct((t_len, n), F32)],
        args=(a, w, res), sem=("parallel",), comm=comm)


def _mm_nt(a, w, name, tm=512, comm=None):
    t_len, k = a.shape
    n = w.shape[0]

    def body(a_ref, w_ref, o_ref):
        o_ref[...] = lax.dot_general(a_ref[...], w_ref[...], (((1,), (1,)), ((), ())),
                                     preferred_element_type=F32).astype(BF16)

    return _pallas(
        body, name=name, grid=(t_len // tm,),
        in_specs=[pl.BlockSpec((tm, k), lambda i: (i, 0)),
                  pl.BlockSpec((n, k), lambda i: (0, 0), pipeline_mode=ONCE)],
        out_specs=[pl.BlockSpec((tm, n), lambda i: (i, 0))],
        out_shape=[jax.ShapeDtypeStruct((t_len, n), BF16)],
        args=(a, w), sem=("parallel",), comm=comm)


def _mm_nt_normbwd(dz, w, x, g, dres, name, planes=False, tm=512, comm=None):
    t_len, d = x.shape
    n = w.shape[1]
    half = n // 2
    nt_dims = (((1,), (1,)), ((), ()))

    def body(dz_ref, w_ref, x_ref, g_ref, r_ref, dx_ref, dxb_ref, dg_ref):
        @pl.when(pl.program_id(0) == 0)
        def _():
            dg_ref[...] = jnp.zeros_like(dg_ref)

        if planes:
            dh = (lax.dot_general(dz_ref[0], w_ref[:, 0:half], nt_dims, preferred_element_type=F32)
                  + lax.dot_general(dz_ref[1], w_ref[:, half:], nt_dims, preferred_element_type=F32))
        else:
            dh = lax.dot_general(dz_ref[...], w_ref[...], nt_dims, preferred_element_type=F32)
        xv = x_ref[...]
        r = lax.rsqrt(jnp.mean(xv * xv, axis=-1, keepdims=True) + EPS)
        xh = xv * r
        dhg = dh * g_ref[...]
        dx = r_ref[...] + r * (dhg - xh * jnp.mean(dhg * xh, axis=-1, keepdims=True))
        dx_ref[...] = dx
        dxb_ref[...] = dx.astype(BF16)
        dg_ref[0:1, :] += jnp.sum(dh * xh, axis=0, keepdims=True)

    if planes:
        dz_spec = pl.BlockSpec((2, tm, half), lambda i: (0, i, 0))
    else:
        dz_spec = pl.BlockSpec((tm, n), lambda i: (i, 0))
    return _pallas(
        body, name=name, grid=(t_len // tm,),
        in_specs=[dz_spec,
                  pl.BlockSpec((d, n), lambda i: (0, 0), pipeline_mode=ONCE),
                  pl.BlockSpec((tm, d), lambda i: (i, 0)),
                  pl.BlockSpec((1, d), lambda i: (0, 0)),
                  pl.BlockSpec((tm, d), lambda i: (i, 0))],
        out_specs=[pl.BlockSpec((tm, d), lambda i: (i, 0)),
                   pl.BlockSpec((tm, d), lambda i: (i, 0)),
                   pl.BlockSpec((SUBLANES, d), lambda i: (0, 0))],
        out_shape=[jax.ShapeDtypeStruct((t_len, d), F32),
                   jax.ShapeDtypeStruct((t_len, d), BF16),
                   jax.ShapeDtypeStruct((SUBLANES, d), F32)],
        args=(dz, w, x, g, dres), sem=("arbitrary",), comm=comm)


def _mm_tn(a, g, name, tk, tn, planes=False, tt=1024, comm=None):
    t_len, k = a.shape
    n = 2 * g.shape[2] if planes else g.shape[1]
    nn = n // tn
    half = nn // 2
    tt = min(tt, t_len)

    def body(a_ref, g_ref, o_ref):
        @pl.when(pl.program_id(2) == 0)
        def _():
            o_ref[...] = jnp.zeros_like(o_ref)

        o_ref[...] += lax.dot_general(a_ref[...], g_ref[...], (((0,), (0,)), ((), ())),
                                      preferred_element_type=F32)

    if planes:
        g_spec = pl.BlockSpec((None, tt, tn), lambda i, j, t: (j // half, t, j % half))
    else:
        g_spec = pl.BlockSpec((tt, tn), lambda i, j, t: (t, j))
    return _pallas(
        body, name=name, grid=(k // tk, nn, t_len // tt),
        in_specs=[pl.BlockSpec((tt, tk), lambda i, j, t: (t, i)), g_spec],
        out_specs=[pl.BlockSpec((tk, tn), lambda i, j, t: (i, j))],
        out_shape=[jax.ShapeDtypeStruct((k, n), F32)],
        args=(a, g), sem=("parallel", "parallel", "arbitrary"), comm=comm)


def _lru_gates(rp, ip, spn):
    r = _sigmoid(rp)
    i = _sigmoid_tanh(ip)
    la = r * spn
    a = jnp.exp(la)
    mult = jnp.sqrt(_neg_expm1(2.0 * la, a))
    return r, i, a, mult


def _mixer_fwd(z, prm, gates, layer, name, tb=256, comm=None):
    t_len = z.shape[0]

    def body(z_ref, p_ref, g_ref, y_ref, h_ref, xhalo, phalo, hcar, lx_s, rp_s, ip_s):
        @pl.when(pl.program_id(0) == 0)
        def _():
            xhalo[...] = jnp.zeros_like(xhalo)
            phalo[...] = jnp.zeros_like(phalo)
            hcar[...] = jnp.zeros_like(hcar)

        prm_v = p_ref[...]
        cw = prm_v[0:4]
        vec = prm_v[4:8]
        xp = z_ref[:, 0:D_LRU].astype(F32)
        ext = jnp.concatenate([xhalo[...], xp], axis=0)
        lx = vec[0:1] + _conv_taps(ext, [cw[k:k + 1] for k in range(4)], tb)
        xhalo[...] = xp[tb - SUBLANES:]
        lx_s[...] = lx
        lxb = lx.astype(BF16)
        for q in range(4):
            sl = slice(q * 256, (q + 1) * 256)
            rp_s[:, sl] = jnp.dot(lxb[:, sl], g_ref[q], preferred_element_type=F32) + vec[1:2, sl]
            ip_s[:, sl] = jnp.dot(lxb[:, sl], g_ref[4 + q], preferred_element_type=F32) + vec[2:3, sl]

        spn = jnp.broadcast_to(-RG_C * _softplus_neg(vec[3:4]), (SUBLANES, D_LRU))
        row = lax.broadcasted_iota(jnp.int32, (SUBLANES, D_LRU), 0)

        def step(ci, carry):
            o = pl.multiple_of(ci * PACKED, PACKED)
            gate = z_ref[pl.ds(o, PACKED), D_LRU:2 * D_LRU].astype(F32)
            ys = []
            for sub in range(2):
                rows = pl.ds(pl.multiple_of(o + sub * SUBLANES, SUBLANES), SUBLANES)
                lxv = lx_s[rows, :]
                _, i, a, mult = _lru_gates(rp_s[rows, :], ip_s[rows, :], spn)
                h = _scan8(a, mult * (i * lxv), carry, row)
                h_ref[rows, :] = h
                ys.append(h * _gelu(gate[sub * SUBLANES:(sub + 1) * SUBLANES]))
                carry = jnp.broadcast_to(h[SUBLANES - 1:SUBLANES, :], (SUBLANES, D_LRU))
            y_ref[pl.ds(o, PACKED), 0:D_LRU] = jnp.concatenate(ys, axis=0).astype(BF16)
            return carry

        hcar[...] = lax.fori_loop(0, tb // PACKED, step, hcar[...])

        scw = prm_v[8:11, 0:D_SC]
        o_b, o_c, o_x = 2 * D_LRU, 2 * D_LRU + D_SC, 2 * D_LRU + 2 * D_SC
        p = z_ref[:, o_c:o_x].astype(F32) * z_ref[:, o_x:].astype(F32)
        pext = jnp.concatenate([phalo[...], p], axis=0)
        q = _conv_taps(pext, [scw[k:k + 1] for k in range(3)], tb)
        phalo[...] = p[tb - SUBLANES:]
        y_ref[:, D_LRU:] = (z_ref[:, o_b:o_c].astype(F32) * q).astype(BF16)

    return _pallas(
        body, name=name, grid=(t_len // tb,),
        in_specs=[pl.BlockSpec((tb, D_IN), lambda t: (t, 0)),
                  pl.BlockSpec((None, 2 * SUBLANES, D_LRU), lambda t: (layer, 0, 0)),
                  pl.BlockSpec((None, 8, 256, 256), lambda t: (layer, 0, 0, 0))],
        out_specs=[pl.BlockSpec((tb, D_MIX), lambda t: (t, 0)),
                   pl.BlockSpec((tb, D_LRU), lambda t: (t, 0))],
        out_shape=[jax.ShapeDtypeStruct((t_len, D_MIX), BF16),
                   jax.ShapeDtypeStruct((t_len, D_LRU), F32)],
        scratch_shapes=[pltpu.VMEM((SUBLANES, D_LRU), F32), pltpu.VMEM((SUBLANES, D_SC), F32),
                        pltpu.VMEM((SUBLANES, D_LRU), F32), pltpu.VMEM((tb, D_LRU), F32),
                        pltpu.VMEM((tb, D_LRU), F32), pltpu.VMEM((tb, D_LRU), F32)],
        args=(z, prm, gates), sem=("arbitrary",), comm=comm)


def _mixer_bwd(z, h, dy, prm, gates, layer, name, tb=256, comm=None):
    t_len = z.shape[0]
    nb = t_len // tb

    def body(z_ref, zh_ref, h_ref, hh_ref, dy_ref, p_ref, g_ref, dz_ref, dp_ref, dg_ref,
             lx_s, rp_s, ip_s, drpb_s, dipb_s, dlx_s, hext_s, acc_s, acar, gcar, dqh):
        t = pl.program_id(0)
        first_block = t == nb - 1

        @pl.when(t == 0)
        def _():
            for ref in (dp_ref, dg_ref, acc_s, acar, gcar, dqh):
                ref[...] = jnp.zeros_like(ref)
            dlx_s[tb:, :] = jnp.zeros((SUBLANES, D_LRU), F32)

        prm_v = p_ref[...]
        cw = prm_v[0:4]
        vec = prm_v[4:8]
        scw = prm_v[8:11, 0:D_SC]
        wa_ref = [g_ref.at[q] for q in range(4)]
        wx_ref = [g_ref.at[4 + q] for q in range(4)]
        dwa_ref = [dg_ref.at[q] for q in range(4)]
        dwx_ref = [dg_ref.at[4 + q] for q in range(4)]
        ctaps = [cw[k:k + 1] for k in range(4)]
        staps = [scw[k:k + 1] for k in range(3)]
        keep = jnp.where(first_block, 0.0, 1.0)
        zh = zh_ref[...].astype(F32)[PACKED - SUBLANES:] * keep

        xp = z_ref[:, 0:D_LRU].astype(F32)
        xext = jnp.concatenate([zh[:, 0:D_LRU], xp], axis=0)
        lx = vec[0:1] + _conv_taps(xext, ctaps, tb)
        lx_s[...] = lx
        lxb = lx.astype(BF16)
        for q in range(4):
            sl = slice(q * 256, (q + 1) * 256)
            rp_s[:, sl] = jnp.dot(lxb[:, sl], wa_ref[q][...], preferred_element_type=F32) + vec[1:2, sl]
            ip_s[:, sl] = jnp.dot(lxb[:, sl], wx_ref[q][...], preferred_element_type=F32) + vec[2:3, sl]
        hext_s[0:SUBLANES, :] = hh_ref[...] * keep
        hext_s[SUBLANES:, :] = h_ref[...]

        spn = jnp.broadcast_to(-RG_C * _softplus_neg(vec[3:4]), (SUBLANES, D_LRU))
        row = lax.broadcasted_iota(jnp.int32, (SUBLANES, D_LRU), 0)

        def step(ci, carry):
            a_next, g_next = carry
            o = pl.multiple_of((tb // PACKED - 1 - ci) * PACKED, PACKED)
            rows16 = pl.ds(o, PACKED)
            gate16 = z_ref[rows16, D_LRU:2 * D_LRU].astype(F32)
            dyl16 = dy_ref[rows16, 0:D_LRU].astype(F32)
            dgs, drs, dis = [None, None], [None, None], [None, None]
            for sub in (1, 0):
                oo = pl.multiple_of(o + sub * SUBLANES, SUBLANES)
                rows = pl.ds(oo, SUBLANES)
                half = slice(sub * SUBLANES, (sub + 1) * SUBLANES)
                lxv = lx_s[rows, :]
                r, i, a, mult = _lru_gates(rp_s[rows, :], ip_s[rows, :], spn)
                hwin = hext_s[pl.ds(oo, 2 * SUBLANES), :]
                hv = hwin[SUBLANES:]
                hprev = pltpu.roll(hwin, 1, axis=0)[SUBLANES:]
                gel, dgel = _gelu_parts(gate16[half])
                dyl = dyl16[half]
                a_up = jnp.where(row < SUBLANES - 1, pltpu.roll(a, SUBLANES - 1, axis=0), a_next)
                gg = _scan8_rev(a_up, dyl * gel, g_next, row)
                dgs[sub] = dyl * hv * dgel
                ilx = i * lxv
                dla = gg * hprev * a - (gg * ilx) * (a * a) / mult
                dlx_s[rows, :] = gg * mult * i
                drp = dla * spn * r * (1.0 - r)
                dip = gg * mult * lxv * i * (1.0 - i)
                drs[sub] = drp
                dis[sub] = dip
                acc_s[0] += drp
                acc_s[1] += dip
                acc_s[2] += dla * r
                a_next = jnp.broadcast_to(a[0:1, :], (SUBLANES, D_LRU))
                g_next = jnp.broadcast_to(gg[0:1, :], (SUBLANES, D_LRU))
            dz_ref[rows16, D_LRU:2 * D_LRU] = jnp.concatenate(dgs, axis=0).astype(BF16)
            drpb_s[rows16, :] = jnp.concatenate(drs, axis=0).astype(BF16)
            dipb_s[rows16, :] = jnp.concatenate(dis, axis=0).astype(BF16)
            return a_next, g_next

        a_c, g_c = lax.fori_loop(0, tb // PACKED, step, (acar[...], gcar[...]))
        acar[...] = a_c
        gcar[...] = g_c

        drpb = drpb_s[...]
        dipb = dipb_s[...]
        nt_dims = (((1,), (1,)), ((), ()))
        tn_dims = (((0,), (0,)), ((), ()))
        for q in range(4):
            sl = slice(q * 256, (q + 1) * 256)
            dlx_s[0:tb, sl] += (
                lax.dot_general(drpb[:, sl], wa_ref[q][...], nt_dims, preferred_element_type=F32)
                + lax.dot_general(dipb[:, sl], wx_ref[q][...], nt_dims, preferred_element_type=F32))
            dwa_ref[q][...] += lax.dot_general(lxb[:, sl], drpb[:, sl], tn_dims, preferred_element_type=F32)
            dwx_ref[q][...] += lax.dot_general(lxb[:, sl], dipb[:, sl], tn_dims, preferred_element_type=F32)

        dlx_ext = dlx_s[...]
        dlx = dlx_ext[0:tb]
        dz_ref[:, 0:D_LRU] = _conv_taps_t(dlx_ext, ctaps, tb).astype(BF16)
        dp_ref[3:4, :] += jnp.sum(dlx * xp, axis=0, keepdims=True)
        for k in range(3):
            shifted = pltpu.roll(xext, 3 - k, axis=0)[SUBLANES:]
            dp_ref[k:k + 1, :] += jnp.sum(dlx * shifted, axis=0, keepdims=True)
        dp_ref[4:5, :] += jnp.sum(dlx, axis=0, keepdims=True)
        dlx_s[tb:, :] = dlx[0:SUBLANES]

        o_b, o_c, o_x = 2 * D_LRU, 2 * D_LRU + D_SC, 2 * D_LRU + 2 * D_SC
        sb = z_ref[:, o_b:o_c].astype(F32)
        scc = z_ref[:, o_c:o_x].astype(F32)
        sx = z_ref[:, o_x:].astype(F32)
        p = scc * sx
        pext = jnp.concatenate([zh[:, o_c:o_x] * zh[:, o_x:], p], axis=0)
        q = _conv_taps(pext, staps, tb)
        dys = dy_ref[:, D_LRU:].astype(F32)
        dq = dys * sb
        dp = _conv_taps_t(jnp.concatenate([dq, dqh[...]], axis=0), staps, tb)
        dp_ref[10:11, 0:D_SC] += jnp.sum(dq * p, axis=0, keepdims=True)
        for k in range(2):
            shifted = pltpu.roll(pext, 2 - k, axis=0)[SUBLANES:]
            dp_ref[8 + k:9 + k, 0:D_SC] += jnp.sum(dq * shifted, axis=0, keepdims=True)
        dqh[...] = dq[0:SUBLANES]
        dz_ref[:, o_b:o_c] = (dys * q).astype(BF16)
        dz_ref[:, o_c:o_x] = (dp * sx).astype(BF16)
        dz_ref[:, o_x:] = (dp * scc).astype(BF16)

        @pl.when(first_block)
        def _():
            dp_ref[5:6, :] = jnp.sum(acc_s[0], axis=0, keepdims=True)
            dp_ref[6:7, :] = jnp.sum(acc_s[1], axis=0, keepdims=True)
            dp_ref[7:8, :] = (jnp.sum(acc_s[2], axis=0, keepdims=True) * RG_C * _sigmoid(-vec[3:4]))

    blk = lambda t: (nb - 1 - t, 0)
    halo8 = lambda t: (jnp.maximum((nb - 1 - t) * (tb // SUBLANES) - 1, 0), 0)
    halo16 = lambda t: (jnp.maximum((nb - 1 - t) * (tb // PACKED) - 1, 0), 0)
    return _pallas(
        body, name=name, grid=(nb,),
        in_specs=[pl.BlockSpec((tb, D_IN), blk), pl.BlockSpec((PACKED, D_IN), halo16),
                  pl.BlockSpec((tb, D_LRU), blk), pl.BlockSpec((SUBLANES, D_LRU), halo8),
                  pl.BlockSpec((tb, D_MIX), blk),
                  pl.BlockSpec((None, 2 * SUBLANES, D_LRU), lambda t: (layer, 0, 0)),
                  pl.BlockSpec((None, 8, 256, 256), lambda t: (layer, 0, 0, 0))],
        out_specs=[pl.BlockSpec((tb, D_IN), blk),
                   pl.BlockSpec((2 * SUBLANES, D_LRU), lambda t: (0, 0)),
                   pl.BlockSpec((8, 256, 256), lambda t: (0, 0, 0))],
        out_shape=[jax.ShapeDtypeStruct((t_len, D_IN), BF16),
                   jax.ShapeDtypeStruct((2 * SUBLANES, D_LRU), F32),
                   jax.ShapeDtypeStruct((8, 256, 256), F32)],
        scratch_shapes=[pltpu.VMEM((tb, D_LRU), F32),
                        pltpu.VMEM((tb, D_LRU), F32), pltpu.VMEM((tb, D_LRU), F32),
                        pltpu.VMEM((tb, D_LRU), BF16), pltpu.VMEM((tb, D_LRU), BF16),
                        pltpu.VMEM((tb + SUBLANES, D_LRU), F32), pltpu.VMEM((tb + SUBLANES, D_LRU), F32),
                        pltpu.VMEM((3, SUBLANES, D_LRU), F32),
                        pltpu.VMEM((SUBLANES, D_LRU), F32), pltpu.VMEM((SUBLANES, D_LRU), F32),
                        pltpu.VMEM((SUBLANES, D_SC), F32)],
        args=(z, z, h, h, dy, prm, gates), sem=("arbitrary",), comm=comm)


def _ffn_act(u, fw, layer, name, tb=256, tn=1024, rc=64, comm=None):
    t_len = u.shape[1]
    hb = tb // PACKED

    def body(u_ref, uh_ref, fw_ref, o_ref, ext):
        keep = jnp.where(pl.program_id(0) == 0, 0.0, 1.0)
        ext[:, 0:SUBLANES, :] = uh_ref[...].astype(F32)[:, PACKED - SUBLANES:, :] * keep
        ext[:, SUBLANES:, :] = u_ref[...].astype(F32)
        fw_v = fw_ref[...]

        for lb in range(tn // LANES):
            lanes = slice(lb * LANES, (lb + 1) * LANES)
            wg = [fw_v[0, k:k + 1, lanes] for k in range(3)]
            wu = [fw_v[1, k:k + 1, lanes] for k in range(3)]

            def chunk(ci, c, lanes=lanes, wg=wg, wu=wu):
                o = pl.multiple_of(ci * rc, rc)
                win = pl.ds(o, rc + SUBLANES)
                gate = _conv_taps(ext[0, win, lanes], wg, rc)
                up = _conv_taps(ext[1, win, lanes], wu, rc)
                o_ref[pl.ds(o, rc), lanes] = (_gelu(gate) * up).astype(BF16)
                return c

            lax.fori_loop(0, tb // rc, chunk, 0)

    return _pallas(
        body, name=name, grid=(t_len // tb, D_FF // tn),
        in_specs=[pl.BlockSpec((2, tb, tn), lambda i, j: (0, i, j)),
                  pl.BlockSpec((2, PACKED, tn), lambda i, j: (0, jnp.maximum(i * hb - 1, 0), j)),
                  pl.BlockSpec((None, 2, SUBLANES, tn), lambda i, j: (layer, 0, 0, j))],
        out_specs=[pl.BlockSpec((tb, tn), lambda i, j: (i, j))],
        out_shape=[jax.ShapeDtypeStruct((t_len, D_FF), BF16)],
        scratch_shapes=[pltpu.VMEM((2, tb + SUBLANES, tn), F32)],
        args=(u, u, fw), sem=("parallel", "parallel"), comm=comm)


def _ffn_bwd(dact, u, fw, layer, name, tb=256, tn=1024, rc=32, comm=None):
    t_len = u.shape[1]
    ni = t_len // tb
    hb = tb // PACKED
    last_halo = t_len // PACKED - 1

    def body(d_ref, dn_ref, u_ref, up_ref, un_ref, fw_ref, du_ref, dfw_ref, extu, extd, acc):
        i = pl.program_id(1)

        @pl.when(i == 0)
        def _():
            acc[...] = jnp.zeros_like(acc)

        keep_prev = jnp.where(i == 0, 0.0, 1.0)
        keep_next = jnp.where(i == ni - 1, 0.0, 1.0)
        extu[:, 0:SUBLANES, :] = up_ref[...].astype(F32)[:, PACKED - SUBLANES:, :] * keep_prev
        extu[:, SUBLANES:SUBLANES + tb, :] = u_ref[...].astype(F32)
        extu[:, SUBLANES + tb:, :] = un_ref[...].astype(F32)[:, 0:SUBLANES, :]
        extd[0:tb, :] = d_ref[...].astype(F32)
        extd[tb:, :] = dn_ref[...].astype(F32)[0:SUBLANES] * keep_next
        fw_v = fw_ref[...]
        m = rc + SUBLANES

        for lb in range(tn // LANES):
            lanes = slice(lb * LANES, (lb + 1) * LANES)
            taps = [[fw_v[pln, k:k + 1, lanes] for k in range(3)] for pln in range(2)]

            def chunk(ci, c, lanes=lanes, taps=taps):
                o = pl.multiple_of(ci * rc, rc)
                win = pl.ds(o, rc + 2 * SUBLANES)
                sh = []
                for pln in range(2):
                    e = extu[pln, win, lanes]
                    sh.append([pltpu.roll(e, 2, axis=0)[SUBLANES:], pltpu.roll(e, 1, axis=0)[SUBLANES:],
                               e[SUBLANES:]])
                gate = sum(taps[0][k] * sh[0][k] for k in range(3))
                up = sum(taps[1][k] * sh[1][k] for k in range(3))
                dv = extd[pl.ds(o, m), lanes]
                gel, dgel = _gelu_parts(gate)
                dpost = [dv * up * dgel, dv * gel]
                for pln in range(2):
                    du_ref[pln, pl.ds(o, rc), lanes] = _conv_taps_t(dpost[pln], taps[pln], rc).astype(BF16)
                    for k in range(3):
                        prod = dpost[pln][0:rc] * sh[pln][k][0:rc]
                        acc[3 * pln + k, :, lanes] += sum(
                            prod[s:s + SUBLANES] for s in range(0, rc, SUBLANES))
                return c

            lax.fori_loop(0, tb // rc, chunk, 0)

        @pl.when(i == ni - 1)
        def _():
            dfw_ref[...] = jnp.zeros_like(dfw_ref)
            for pln in range(2):
                for k in range(3):
                    dfw_ref[pln, k:k + 1, :] = jnp.sum(acc[3 * pln + k], axis=0, keepdims=True)

    return _pallas(
        body, name=name, grid=(D_FF // tn, ni),
        in_specs=[pl.BlockSpec((tb, tn), lambda j, i: (i, j)),
                  pl.BlockSpec((PACKED, tn), lambda j, i: (jnp.minimum((i + 1) * hb, last_halo), j)),
                  pl.BlockSpec((2, tb, tn), lambda j, i: (0, i, j)),
                  pl.BlockSpec((2, PACKED, tn), lambda j, i: (0, jnp.maximum(i * hb - 1, 0), j)),
                  pl.BlockSpec((2, PACKED, tn), lambda j, i: (0, jnp.minimum((i + 1) * hb, last_halo), j)),
                  pl.BlockSpec((None, 2, SUBLANES, tn), lambda j, i: (layer, 0, 0, j))],
        out_specs=[pl.BlockSpec((2, tb, tn), lambda j, i: (0, i, j)),
                   pl.BlockSpec((2, SUBLANES, tn), lambda j, i: (0, 0, j))],
        out_shape=[jax.ShapeDtypeStruct((2, t_len, D_FF), BF16),
                   jax.ShapeDtypeStruct((2, SUBLANES, D_FF), F32)],
        scratch_shapes=[pltpu.VMEM((2, tb + 2 * SUBLANES, tn), F32),
                        pltpu.VMEM((tb + SUBLANES, tn), F32),
                        pltpu.VMEM((6, SUBLANES, tn), F32)],
        args=(dact, dact, u, u, u, fw), sem=("parallel", "arbitrary"), comm=comm)


def _loss_head(x, g, target, name, tb=256):
    t_len, d = x.shape

    def body(x_ref, g_ref, t_ref, dx_ref, dxb_ref, dg_ref, loss_ref):
        @pl.when(pl.program_id(0) == 0)
        def _():
            dg_ref[...] = jnp.zeros_like(dg_ref)
            loss_ref[...] = jnp.zeros_like(loss_ref)

        xv = x_ref[...]
        gv = g_ref[...]
        r = lax.rsqrt(jnp.mean(xv * xv, axis=-1, keepdims=True) + EPS)
        xh = xv * r
        err = xh * gv - t_ref[...]
        loss_ref[...] += (0.5 / d) * jnp.sum(jnp.sum(err * err, axis=-1, keepdims=True), axis=0, keepdims=True)
        dy = err * (1.0 / d)
        dyg = dy * gv
        dx = r * (dyg - xh * jnp.mean(dyg * xh, axis=-1, keepdims=True))
        dx_ref[...] = dx
        dxb_ref[...] = dx.astype(BF16)
        dg_ref[0:1, :] += jnp.sum(dy * xh, axis=0, keepdims=True)

    return _pallas(
        body, name=name, grid=(t_len // tb,),
        in_specs=[pl.BlockSpec((tb, d), lambda i: (i, 0)), pl.BlockSpec((1, d), lambda i: (0, 0)),
                  pl.BlockSpec((tb, d), lambda i: (i, 0))],
        out_specs=[pl.BlockSpec((tb, d), lambda i: (i, 0)), pl.BlockSpec((tb, d), lambda i: (i, 0)),
                   pl.BlockSpec((SUBLANES, d), lambda i: (0, 0)),
                   pl.BlockSpec((SUBLANES, LANES), lambda i: (0, 0))],
        out_shape=[jax.ShapeDtypeStruct((t_len, d), F32), jax.ShapeDtypeStruct((t_len, d), BF16),
                   jax.ShapeDtypeStruct((SUBLANES, d), F32), jax.ShapeDtypeStruct((SUBLANES, LANES), F32)],
        args=(x, g, target), sem=("arbitrary",))[0]


def _adamw(w, g, m, v, name):
    r, c = w.shape
    tr = 256 if r % 256 == 0 else r
    c1 = 1.0 / (1.0 - ADAM_B1 ** ADAM_STEP)
    c2 = 1.0 / (1.0 - ADAM_B2 ** ADAM_STEP)

    def body(w_ref, g_ref, m_ref, v_ref, d_ref, mo_ref, vo_ref):
        gv = g_ref[...]
        mn = ADAM_B1 * m_ref[...] + (1.0 - ADAM_B1) * gv
        vn = ADAM_B2 * v_ref[...] + (1.0 - ADAM_B2) * (gv * gv)
        d_ref[...] = -ADAM_LR * ((mn * c1) / (jnp.sqrt(vn * c2) + ADAM_EPS) + ADAM_WD * w_ref[...])
        mo_ref[...] = mn
        vo_ref[...] = vn

    spec = pl.BlockSpec((tr, c), lambda i: (i, 0))
    shape = jax.ShapeDtypeStruct((r, c), F32)
    return _pallas(
        body, name=name, grid=(r // tr,),
        in_specs=[spec] * 4, out_specs=[spec] * 3, out_shape=[shape] * 3,
        args=(w, g, m, v), sem=("parallel",))[0]


def _place():
    x, y, c = lax.axis_index("x"), lax.axis_index("y"), lax.axis_index("c")
    chips = [(1 - x, y), (x, 1 - y), (1 - x, 1 - y)]
    return x, y, c, chips


def _remote(src, dst, send, recv, sem, to):
    return pltpu.make_async_remote_copy(
        src_ref=src, dst_ref=dst, send_sem=send.at[sem], recv_sem=recv.at[sem], device_id=to, device_id_type=MESH)


def _gather_plan(fulls, kinds, mid_at=None, parts=None):
    parts = parts or [(0, 1)] * len(fulls)

    def region(it, f, k, cc):
        kind = kinds[it]
        p, n = parts[it]
        if kind == SMALL:
            return f.at[k, pl.ds(cc * (CONV_PACK_ROWS // 2), CONV_PACK_ROWS // 2), :]
        if COL_SHARDED[kind]:
            rows, cols = f.shape[0] // (2 * n), f.shape[1] // N_CHIP
            return f.at[pl.ds((cc * n + p) * rows, rows), pl.ds(k * cols, cols)]
        assert n == 1
        rows = f.shape[0] // N_CHIP
        return f.at[pl.ds(k * rows + cc * (rows // 2), rows // 2), :]

    def first_hop(bufs, send, recv, it, j):
        x, y, c, chips = _place()
        reg = region(it, bufs[it], 2 * x + y, c)
        return _remote(reg, reg, send, recv, it * 6 + j, (*chips[j], c))

    def arrival(bufs, send, recv, it, j, second):
        x, y, c, chips = _place()
        px, py = chips[j]
        reg = region(it, bufs[it], 2 * px + py, 1 - c if second else c)
        to = (x, y, 1 - c) if second else (px, py, c)
        return _remote(reg, reg, send, recv, it * 6 + (3 + j if second else j), to)

    def forward(bufs, send, recv, it, j):
        x, y, c, chips = _place()
        px, py = chips[j]
        reg = region(it, bufs[it], 2 * px + py, c)
        return _remote(reg, reg, send, recv, it * 6 + 3 + j, (x, y, 1 - c))

    def start(srcs, bufs, outs, send, recv):
        for it in range(len(bufs)):
            for j in range(3):
                first_hop(bufs, send, recv, it, j).start()

    def mid(srcs, bufs, outs, send, recv):
        for it in range(len(bufs)):
            for j in range(3):
                arrival(bufs, send, recv, it, j, False).wait_recv()
                forward(bufs, send, recv, it, j).start()

    def finish(srcs, bufs, outs, send, recv):
        for it in range(len(bufs)):
            for j in range(3):
                arrival(bufs, send, recv, it, j, True).wait_recv()
        for it in range(len(bufs)):
            for j in range(3):
                first_hop(bufs, send, recv, it, j).wait_send()
                forward(bufs, send, recv, it, j).wait_send()

    return Comm(srcs=(), bufs=tuple(fulls), outs=(), n_sem=6 * len(fulls), start=start, mid=mid, finish=finish,
                mid_at=mid_at)


def _half_axis(kind):
    return 0 if kind == SMALL or COL_SHARDED[kind] else 1


def _half2(ref, kind, cc):
    if _half_axis(kind) == 0:
        return ref.at[pl.ds(cc * (ref.shape[0] // 2), ref.shape[0] // 2), :]
    return ref.at[:, pl.ds(cc * (ref.shape[1] // 2), ref.shape[1] // 2)]


def _pair_plan(grads, kinds):
    def land_shape(g, kind):
        s = list(g.shape)
        s[_half_axis(kind)] //= 2
        return jax.ShapeDtypeStruct(tuple(s), F32)

    def copy(srcs, outs, send, recv, it):
        x, y, c, _ = _place()
        return _remote(_half2(srcs[it], kinds[it], 1 - c), outs[it], send, recv, it, (x, y, 1 - c))

    def start(srcs, bufs, outs, send, recv):
        for it in range(len(srcs)):
            copy(srcs, outs, send, recv, it).start()

    def finish(srcs, bufs, outs, send, recv):
        for it in range(len(srcs)):
            copy(srcs, outs, send, recv, it).wait_send()
        for it in range(len(srcs)):
            copy(srcs, outs, send, recv, it).wait_recv()

    return Comm(srcs=tuple(grads), bufs=(), outs=tuple(land_shape(g, k) for g, k in zip(grads, kinds)),
                n_sem=len(grads), start=start, finish=finish)


def _scatter_plan(parts, slots, kinds):
    def piece(s, kind, k):
        if kind == SMALL:
            return s
        if COL_SHARDED[kind]:
            n = s.shape[1] // N_CHIP
            return s.at[:, pl.ds(k * n, n)]
        n = s.shape[0] // N_CHIP
        return s.at[pl.ds(k * n, n), :]

    def outbound(srcs, bufs, send, recv, it, j):
        x, y, c, chips = _place()
        px, py = chips[j]
        return _remote(piece(srcs[it], kinds[it], 2 * px + py), bufs[it].at[2 * x + y], send, recv, it * 3 + j,
                       (px, py, c))

    def inbound(bufs, send, recv, it, j):
        x, y, c, chips = _place()
        px, py = chips[j]
        got = bufs[it].at[2 * px + py]
        return _remote(got, got, send, recv, it * 3 + j, (px, py, c))

    def start(srcs, bufs, outs, send, recv):
        for it in range(len(srcs)):
            for j in range(3):
                outbound(srcs, bufs, send, recv, it, j).start()

    def finish(srcs, bufs, outs, send, recv):
        for it in range(len(srcs)):
            for j in range(3):
                inbound(bufs, send, recv, it, j).wait_recv()
        for it in range(len(srcs)):
            for j in range(3):
                outbound(srcs, bufs, send, recv, it, j).wait_send()

    return Comm(srcs=tuple(parts), bufs=tuple(slots), outs=(), n_sem=3 * len(parts), start=start, finish=finish)


def _share_plan(fulls, kinds, layer):
    def half(f, kind, cc):
        return _half2(f if kind == SMALL else f.at[layer], kind, cc)

    def copy(bufs, send, recv, it, cc):
        x, y, c, _ = _place()
        reg = half(bufs[it], kinds[it], c if cc == "mine" else 1 - c)
        return _remote(reg, reg, send, recv, it, (x, y, 1 - c))

    def start(srcs, bufs, outs, send, recv):
        for it in range(len(bufs)):
            copy(bufs, send, recv, it, "mine").start()

    def finish(srcs, bufs, outs, send, recv):
        for it in range(len(bufs)):
            copy(bufs, send, recv, it, "other").wait_recv()
        for it in range(len(bufs)):
            copy(bufs, send, recv, it, "mine").wait_send()

    return Comm(srcs=(), bufs=tuple(fulls), outs=(), n_sem=len(fulls), start=start, finish=finish)


def _pair_sum(g, land, idx, kind, name):
    odt = F32 if kind == SMALL else BF16
    r, cdim = land.shape

    def body(idx_ref, g_ref, l_ref, p_ref, s_ref):
        v = (g_ref[...] + l_ref[...]).astype(odt)
        p_ref[...] = v
        if kind == SMALL:
            s_ref[...] = v
        else:
            @pl.when(pl.program_id(1 if COL_SHARDED[kind] else 0) == idx_ref[1])
            def _():
                s_ref[...] = v

    if kind == SMALL:
        grid = (1,)
        g_spec = pl.BlockSpec((r, LANES), lambda i, idx_ref: (idx_ref[0], 0))
        spec = pl.BlockSpec((r, LANES), lambda i, idx_ref: (0, 0))
        s_spec = pl.BlockSpec((None, r, LANES), lambda i, idx_ref: (idx_ref[1], 0, 0))
        s_shape = (N_CHIP, r, LANES)
    elif COL_SHARDED[kind]:
        pc, tr = cdim // N_CHIP, 256
        nrb = r // tr
        grid = (nrb, N_CHIP)
        g_spec = pl.BlockSpec((tr, pc), lambda i, k, idx_ref: (idx_ref[0] * nrb + i, k))
        spec = pl.BlockSpec((tr, pc), lambda i, k, idx_ref: (i, k))
        s_spec = pl.BlockSpec((None, tr, pc), lambda i, k, idx_ref: (idx_ref[1], i, 0))
        s_shape = (N_CHIP, r, pc)
    else:
        pr = r // N_CHIP
        grid = (N_CHIP,)
        g_spec = pl.BlockSpec((pr, cdim), lambda k, idx_ref: (k, idx_ref[0]))
        spec = pl.BlockSpec((pr, cdim), lambda k, idx_ref: (k, 0))
        s_spec = pl.BlockSpec((None, pr, cdim), lambda k, idx_ref: (idx_ref[1], 0, 0))
        s_shape = (N_CHIP, pr, cdim)
    return pl.pallas_call(
        body, name=name,
        grid_spec=pltpu.PrefetchScalarGridSpec(
            num_scalar_prefetch=1, grid=grid, in_specs=[g_spec, spec], out_specs=[spec, s_spec]),
        out_shape=[jax.ShapeDtypeStruct(land.shape, odt), jax.ShapeDtypeStruct(s_shape, odt)],
        compiler_params=_cp(*(["arbitrary"] * len(grid))),
    )(idx, g, land)


def _sum_slots(slots, idx, kind, layer, prev, name):
    _, r, cdim = slots.shape

    def body(*refs):
        s_ref, o_ref = refs[1], refs[-1]
        v = s_ref[...].astype(F32)
        o_ref[...] = (v[0] + v[1]) + (v[2] + v[3])

    if kind == SMALL:
        grid = (1,)
        s_spec = pl.BlockSpec((N_CHIP, r, cdim), lambda i, idx_ref: (0, 0, 0))
        o_spec = pl.BlockSpec((r, cdim), lambda i, idx_ref: (idx_ref[0], 0))
        full = (2 * r, cdim)
    else:
        tr = 256 if r % 256 == 0 else 384
        nrb = r // tr
        grid = (nrb,)
        s_spec = pl.BlockSpec((N_CHIP, tr, cdim), lambda i, idx_ref: (0, i, 0))
        if COL_SHARDED[kind]:
            o_spec = pl.BlockSpec((None, tr, cdim), lambda i, idx_ref: (layer, idx_ref[0] * nrb + i, 0))
            full = (2, 2 * r, cdim)
        else:
            o_spec = pl.BlockSpec((None, tr, cdim), lambda i, idx_ref: (layer, i, idx_ref[0]))
            full = (2, r, 2 * cdim)
    in_specs, args, aliases = [s_spec], [idx, slots], {}
    if prev is not None:
        in_specs.append(ANY)
        args.append(prev)
        aliases = {2: 0}
    return pl.pallas_call(
        body, name=name,
        grid_spec=pltpu.PrefetchScalarGridSpec(
            num_scalar_prefetch=1, grid=grid, in_specs=in_specs, out_specs=o_spec),
        out_shape=jax.ShapeDtypeStruct(full, F32),
        input_output_aliases=aliases,
        compiler_params=_cp(*(["parallel"] * len(grid))),
    )(*args)


def _block_diag(w):
    w4 = w.reshape(2, 4, 4, 64, 64)
    eye = jnp.eye(4, dtype=w.dtype)[None, None, :, None, :, None]
    return (w4[:, :, :, :, None, :] * eye).reshape(2, 4, 256, 256)


def _block_diag_extract(d):
    d5 = d.reshape(4, 4, 64, 4, 64)
    return jnp.stack([d5[:, hh, :, hh, :] for hh in range(4)], axis=1).reshape(-1)


REP_NAMES = ("norm1_g", "lru_conv_b", "lru_ba", "lru_bx", "lru_lambda", "norm2_g", "lru_wa", "lru_wx")


def _pack_rep(norm1_g, conv_b, ba, bx, lam, norm2_g, wa, wx, final_g):
    parts = [a.reshape(-1) for a in (norm1_g, conv_b, ba, bx, lam, norm2_g, wa, wx, final_g)]
    return jnp.concatenate(parts).reshape(REP_ROWS, LANES)


def _unpack_rep(buf):
    flat = buf.reshape(-1)
    res, o = {}, 0
    for k in REP_NAMES:
        shape = (2, 16, 64, 64) if k in ("lru_wa", "lru_wx") else (2, 1024)
        n = math.prod(shape)
        res[k] = flat[o:o + n].reshape(shape)
        o += n
    res["final_g"] = flat[o:o + 1024]
    return res


def _pack_conv_shard(lru_cw, sc_cw, ffn_cw):
    return jnp.concatenate([lru_cw.reshape(16, LANES), jnp.pad(sc_cw.reshape(6, LANES), ((0, 2), (0, 0))),
                            ffn_cw.reshape(72, LANES)], axis=0)


def _unpack_conv_shard(buf):
    return (buf[0:16].reshape(2, 4, 256), buf[16:22].reshape(2, 3, 128), buf[24:96].reshape(2, 3, 1536))


def kernel(x, norm1_g, w_in, lru_conv_w, lru_conv_b, lru_wa, lru_ba, lru_wx, lru_bx, lru_lambda, sc_conv_w, w_out, norm2_g, w_up, ffn_conv_w, w_down, final_g, loss_target, m_norm1_g, m_w_in, m_lru_conv_w, m_lru_conv_b, m_lru_wa, m_lru_ba, m_lru_wx, m_lru_bx, m_lru_lambda, m_sc_conv_w, m_w_out, m_norm2_g, m_w_up, m_ffn_conv_w, m_w_down, m_final_g, v_norm1_g, v_w_in, v_lru_conv_w, v_lru_conv_b, v_lru_wa, v_lru_ba, v_lru_wx, v_lru_bx, v_lru_lambda, v_sc_conv_w, v_w_out, v_norm2_g, v_w_up, v_ffn_conv_w, v_w_down, v_final_g):
    me = 2 * lax.axis_index("x") + lax.axis_index("y")
    idx = jnp.stack([lax.axis_index("c"), me]).astype(jnp.int32)
    t_len = x.shape[1]

    s_conv = _pack_conv_shard(lru_conv_w, sc_conv_w, ffn_conv_w)
    conv_slots = lax.dynamic_update_slice(jnp.zeros((N_CHIP, CONV_PACK_ROWS, LANES), F32), s_conv[None], (me, 0, 0))
    wi = list(_cast_into_full(w_in, W_IN, idx, "cast_w_in"))
    wo = list(_cast_into_full(w_out, W_OUT, idx, "cast_w_out"))
    wu = list(_cast_into_full(w_up, W_UP, idx, "cast_w_up"))
    wd = list(_cast_into_full(w_down, W_DOWN, idx, "cast_w_down"))
    wi[0], convs = _comm_call(_gather_plan([wi[0], conv_slots], [W_IN, SMALL]), "ag_first")
    per_chip = [_unpack_conv_shard(convs[k]) for k in range(N_CHIP)]
    lru_cw = jnp.concatenate([p[0] for p in per_chip], axis=-1)
    sc_cw = jnp.concatenate([p[1] for p in per_chip], axis=-1)
    ffn_cw = jnp.concatenate([p[2] for p in per_chip], axis=-1)

    prm = jnp.concatenate(
        [lru_cw, jnp.stack([lru_conv_b, lru_ba, lru_bx, lru_lambda], axis=1),
         jnp.pad(sc_cw, ((0, 0), (0, 0), (0, D_LRU - D_SC))), jnp.zeros((2, 5, D_LRU), F32)], axis=1)
    gates = jnp.concatenate([_block_diag(lru_wa), _block_diag(lru_wx)], axis=1).astype(BF16)
    fw8 = jnp.pad(ffn_cw.reshape(2, 3, 2, D_FF).transpose(0, 2, 1, 3), ((0, 0), (0, 0), (0, 5), (0, 0)))

    xs = x[0]
    saved = []
    n512, n256 = t_len // 512, t_len // 256
    whole, lower, upper = (0, 1), (0, 2), (1, 2)
    carried_by = {
        "fwd_in_0": ([(wu, 0, W_UP, lower)], (max(n512 - 2, 0),)),
        "fwd_mixer_0": ([(wu, 0, W_UP, upper), (wo, 0, W_OUT, whole)], (max(n256 - 3, 0),)),
        "fwd_up_0": ([(wd, 0, W_DOWN, whole)], (max(n512 - 2, 0),)),
        "fwd_act_0": ([(wi, 1, W_IN, whole), (wo, 1, W_OUT, whole), (wu, 1, W_UP, (0, 4))], (n256 - 2, 0)),
        "fwd_down_0": ([(wu, 1, W_UP, (1, 4))], (max(n512 - 2, 0),)),
        "fwd_in_1": ([(wu, 1, W_UP, (2, 4))], (max(n512 - 2, 0),)),
        "fwd_mixer_1": ([(wu, 1, W_UP, (3, 4)), (wd, 1, W_DOWN, whole)], (max(n256 - 3, 0),)),
    }

    def carried(name):
        if name not in carried_by:
            return None, lambda got: None
        items, mid_at = carried_by[name]

        def store(got):
            for (lst, i, _, _), arr in zip(items, got):
                lst[i] = arr

        return _gather_plan([lst[i] for lst, i, _, _ in items], [k for _, _, k, _ in items], mid_at=mid_at,
                            parts=[p for _, _, _, p in items]), store

    for l in range(2):
        comm, store = carried(f"fwd_in_{l}")
        (z, h1), got = _norm_mm(xs, norm1_g[l][None], wi[l], f"fwd_in_{l}", comm=comm)
        store(got)
        comm, store = carried(f"fwd_mixer_{l}")
        (ymix, hst), got = _mixer_fwd(z, prm, gates, l, f"fwd_mixer_{l}", comm=comm)
        store(got)
        (x2,), _ = _mm_res(ymix, wo[l], xs, f"fwd_out_{l}")
        comm, store = carried(f"fwd_up_{l}")
        (u, h2), got = _norm_mm(x2, norm2_g[l][None], wu[l], f"fwd_up_{l}", planes=True, comm=comm)
        store(got)
        comm, store = carried(f"fwd_act_{l}")
        (act,), got = _ffn_act(u, fw8, l, f"fwd_act_{l}", comm=comm)
        store(got)
        comm, store = carried(f"fwd_down_{l}")
        (x3,), got = _mm_res(act, wd[l], x2, f"fwd_down_{l}", comm=comm)
        store(got)
        saved.append((xs, h1, z, hst, ymix, x2, h2, u, act))
        xs = x3

    dx, dxb, dgf, loss_blk = _loss_head(xs, final_g[None], loss_target[0], "loss_head")
    loss = lax.psum(loss_blk[0, 0], ("x", "y", "c"))

    kinds = [W_IN, W_OUT, W_UP, W_DOWN]
    grads = [None, None]
    small = [None, None]
    reduced = [None] * 4
    summed1 = [None] * 4
    parts = slots = None
    for l in (1, 0):
        x_in, h1, z, hst, ymix, x2, h2, u, act = saved[l]
        carry = l == 0
        comm = _pair_plan([grads[1][W_IN]], [W_IN]) if carry else None
        (g_down,), got = _mm_tn(act, dxb, f"bwd_wdown_{l}", tk=1536, tn=1024, comm=comm)
        if carry:
            summed1[W_IN] = _pair_sum(grads[1][W_IN], got[0], idx, W_IN, "rs_add1_0")
            parts, slots = [s[0] for s in summed1], [s[1] for s in summed1]
        (dact,), _ = _mm_nt(dxb, wd[l], f"bwd_dact_{l}")
        comm = _scatter_plan(parts, slots, kinds) if carry else None
        (du, dfw), got = _ffn_bwd(dact, u, fw8, l, f"bwd_act_{l}", comm=comm)
        if carry:
            reduced = [_sum_slots(got[w], idx, kinds[w], 1, None, f"rs_sum1_{w}") for w in range(4)]
        comm = _share_plan(reduced, kinds, 1) if carry else None
        (g_up,), got = _mm_tn(h2, du, f"bwd_wup_{l}", tk=1024, tn=1536, planes=True, comm=comm)
        if carry:
            reduced = list(got)
        comm = _pair_plan([g_up, g_down], [W_UP, W_DOWN]) if carry else None
        (dx2, dx2b, dg2), got = _mm_nt_normbwd(du, wu[l], x2, norm2_g[l][None], dx, f"bwd_up_{l}", planes=True,
                                               comm=comm)
        if carry:
            sum_up = _pair_sum(g_up, got[0], idx, W_UP, "rs_add0_2")
            sum_down = _pair_sum(g_down, got[1], idx, W_DOWN, "rs_add0_3")
        (g_out,), _ = _mm_tn(ymix, dx2b, f"bwd_wout_{l}", tk=1536, tn=1024)
        comm = _pair_plan([g_out], [W_OUT]) if carry else None
        (dymix,), got = _mm_nt(dx2b, wo[l], f"bwd_dymix_{l}", comm=comm)
        trio = (W_OUT, W_UP, W_DOWN)
        if carry:
            sum_out = _pair_sum(g_out, got[0], idx, W_OUT, "rs_add0_1")
            comm = _scatter_plan([sum_out[0], sum_up[0], sum_down[0]], [sum_out[1], sum_up[1], sum_down[1]], trio)
        else:
            comm = _pair_plan([g_out, g_up, g_down], trio)
        (dz, dprm, dgates), got = _mixer_bwd(z, hst, dymix, prm, gates, l, f"bwd_mixer_{l}", comm=comm)
        if carry:
            for w, s in zip(trio, got):
                reduced[w] = _sum_slots(s, idx, w, 0, reduced[w], f"rs_sum0_{w}")
        else:
            for w, g, land in zip(trio, (g_out, g_up, g_down), got):
                summed1[w] = _pair_sum(g, land, idx, w, f"rs_add1_{w}")
        (g_in,), _ = _mm_tn(h1, dz, f"bwd_win_{l}", tk=1024, tn=1792)
        comm = _pair_plan([g_in], [W_IN]) if carry else None
        (dx, dxb, dg1), got = _mm_nt_normbwd(dz, wi[l], x_in, norm1_g[l][None], dx2, f"bwd_in_{l}", comm=comm)
        if carry:
            sum_in = _pair_sum(g_in, got[0], idx, W_IN, "rs_add0_0")
        grads[l] = [g_in, g_out, g_up, g_down]
        rep = dict(zip(REP_NAMES, [dg1[0], dprm[4], dprm[5], dprm[6], dprm[7], dg2[0],
                                   _block_diag_extract(dgates[0:4]), _block_diag_extract(dgates[4:8])]))
        conv = [dprm[0:4].reshape(-1), jnp.pad(dprm[8:11, 0:D_SC].reshape(-1), (0, 512)),
                dfw[:, 0:3, :].transpose(1, 0, 2).reshape(-1)]
        small[l] = (rep, conv)
    grad_x = dx[None]
    g_small = jnp.concatenate(
        [small[l][0][k] for k in REP_NAMES for l in range(2)] + [dgf[0]] + small[0][1] + small[1][1]
        + [jnp.zeros((8 * LANES,), F32)]).reshape(SMALL_ROWS, LANES)

    land_small, = _comm_call(_pair_plan([g_small], [SMALL]), "rs_pair_small")
    sum_small = _pair_sum(g_small, land_small, idx, SMALL, "rs_add0_4")
    slot_in, slot_small = _comm_call(
        _scatter_plan([sum_in[0], sum_small[0]], [sum_in[1], sum_small[1]], [W_IN, SMALL]), "rs_scatter_last")
    reduced[W_IN] = _sum_slots(slot_in, idx, W_IN, 0, reduced[W_IN], "rs_sum0_0")
    reduced.append(_sum_slots(slot_small, idx, SMALL, 0, None, "rs_sum0_4"))
    gw_in, gw_out, gw_up, gw_down, gs = _comm_call(_share_plan(reduced, kinds + [SMALL], 0), "rs_share0")

    g_rep = gs[0:REP_ROWS]
    g_conv = gs[REP_ROWS:REP_ROWS + CONV_ROWS].reshape(2, CONV_LAYER)
    g_lru_cw = lax.dynamic_slice_in_dim(g_conv[:, 0:4096].reshape(2, 4, 1024), me * 256, 256, axis=2)
    g_sc_cw = lax.dynamic_slice_in_dim(g_conv[:, 4096:4096 + 1536].reshape(2, 3, 512), me * 128, 128, axis=2)
    g_ffn_cw = lax.dynamic_slice_in_dim(g_conv[:, 6144:].reshape(2, 3, 6144), me * 1536, 1536, axis=2)

    def big(w, g, m, v, name):
        shape = w.shape
        two_d = lambda a: a.reshape(-1, shape[-1])
        return [o.reshape(shape) for o in _adamw(two_d(w), two_d(g), two_d(m), two_d(v), name)]

    upd = {"w_in": big(w_in, gw_in, m_w_in, v_w_in, "adamw_w_in"),
           "w_out": big(w_out, gw_out, m_w_out, v_w_out, "adamw_w_out"),
           "w_up": big(w_up, gw_up, m_w_up, v_w_up, "adamw_w_up"),
           "w_down": big(w_down, gw_down, m_w_down, v_w_down, "adamw_w_down")}
    rep_out = _adamw(
        _pack_rep(norm1_g, lru_conv_b, lru_ba, lru_bx, lru_lambda, norm2_g, lru_wa, lru_wx, final_g), g_rep,
        _pack_rep(m_norm1_g, m_lru_conv_b, m_lru_ba, m_lru_bx, m_lru_lambda, m_norm2_g, m_lru_wa, m_lru_wx, m_final_g),
        _pack_rep(v_norm1_g, v_lru_conv_b, v_lru_ba, v_lru_bx, v_lru_lambda, v_norm2_g, v_lru_wa, v_lru_wx, v_final_g),
        "adamw_rep")
    conv_out = _adamw(s_conv, _pack_conv_shard(g_lru_cw, g_sc_cw, g_ffn_cw),
                      _pack_conv_shard(m_lru_conv_w, m_sc_conv_w, m_ffn_conv_w),
                      _pack_conv_shard(v_lru_conv_w, v_sc_conv_w, v_ffn_conv_w), "adamw_conv")

    names = ["norm1_g", "w_in", "lru_conv_w", "lru_conv_b", "lru_wa", "lru_ba", "lru_wx", "lru_bx", "lru_lambda",
             "sc_conv_w", "w_out", "norm2_g", "w_up", "ffn_conv_w", "w_down", "final_g"]
    groups = []
    g_all = dict(_unpack_rep(g_rep))
    g_all.update(w_in=gw_in, w_out=gw_out, w_up=gw_up, w_down=gw_down,
                 lru_conv_w=g_lru_cw, sc_conv_w=g_sc_cw, ffn_conv_w=g_ffn_cw)
    groups.append(g_all)
    for i in range(3):
        d = dict(_unpack_rep(rep_out[i]))
        cl, cs, cf = _unpack_conv_shard(conv_out[i])
        d.update(lru_conv_w=cl, sc_conv_w=cs, ffn_conv_w=cf)
        d.update({k: v[i] for k, v in upd.items()})
        groups.append(d)
    return (loss, grad_x, *[grp[n] for grp in groups for n in names])
```

```python
import dataclasses
import functools
import math
import operator
from typing import Any, Callable, Optional, Sequence

import jax
import jax.numpy as jnp
from jax import lax
from jax.experimental import pallas as pl
from jax.experimental.pallas import tpu as pltpu

F32 = jnp.float32
BF16 = jnp.bfloat16
MESH = pl.DeviceIdType.MESH

D_MODEL = 1024
D_LRU = 1024
D_SC = 512
D_MIX = D_LRU + D_SC
D_IN = 2 * D_LRU + 3 * D_SC
D_FF = 3072
N_CHIP = 4
RG_C = 8.0
EPS = 1e-6
ADAM_LR = 0.001
ADAM_B1 = 0.9
ADAM_B2 = 0.999
ADAM_EPS = 1e-08
ADAM_WD = 0.01
ADAM_STEP = 10

SUBLANES = 8
PACKED = 16
LANES = 128
VMEM_LIMIT = 56 * 1024 * 1024
GELU_C0 = math.sqrt(2.0 / math.pi)
GELU_C1 = 0.044715

REP_LAYER = 6 * 1024 + 2 * 16 * 64 * 64
REP_ROWS = (2 * REP_LAYER + 1024) // LANES
CONV_LAYER = 4 * 1024 + 2048 + 3 * 6144
CONV_ROWS = 2 * CONV_LAYER // LANES
SMALL_ROWS = REP_ROWS + CONV_ROWS + 8
CONV_PACK_ROWS = 96

W_IN, W_OUT, W_UP, W_DOWN, SMALL = range(5)
COL_SHARDED = {W_IN: True, W_OUT: False, W_UP: True, W_DOWN: False}

ONCE = pl.Buffered(1)
ANY = pl.BlockSpec(memory_space=pl.ANY)


def _cp(*sem):
    return pltpu.CompilerParams(dimension_semantics=sem, vmem_limit_bytes=VMEM_LIMIT)


@dataclasses.dataclass
class Comm:
    srcs: Sequence[Any]
    bufs: Sequence[Any]
    outs: Sequence[Any]
    n_sem: int
    start: Callable
    finish: Callable
    mid: Optional[Callable] = None
    mid_at: Optional[Sequence[int]] = None


def _pallas(body, *, name, grid, in_specs, out_specs, out_shape, args, sem, scratch_shapes=(), comm=None):
    if comm is None:
        res = pl.pallas_call(
            body, name=name, grid=grid, in_specs=list(in_specs), out_specs=list(out_specs),
            out_shape=list(out_shape), scratch_shapes=list(scratch_shapes), compiler_params=_cp(*sem))(*args)
        return tuple(res), ()
    n_in, n_out, n_scr = len(in_specs), len(out_specs), len(scratch_shapes)
    ns, nb, no = len(comm.srcs), len(comm.bufs), len(comm.outs)

    def carrier(*refs):
        p = 0
        main_in = refs[p:p + n_in]
        p += n_in
        c_src = refs[p:p + ns]
        p += ns + nb
        main_out = refs[p:p + n_out]
        p += n_out
        c_buf = refs[p:p + nb]
        p += nb
        c_out = refs[p:p + no]
        p += no
        scr = refs[p:p + n_scr]
        send, recv = refs[p + n_scr], refs[p + n_scr + 1]
        ids = [pl.program_id(a) for a in range(len(grid))]

        def at(steps):
            return functools.reduce(operator.and_, [i == s for i, s in zip(ids, steps)])

        @pl.when(at([0] * len(grid)))
        def _():
            comm.start(c_src, c_buf, c_out, send, recv)

        if comm.mid is not None:
            @pl.when(at(comm.mid_at))
            def _():
                comm.mid(c_src, c_buf, c_out, send, recv)

        body(*main_in, *main_out, *scr)

        @pl.when(at([g - 1 for g in grid]))
        def _():
            comm.finish(c_src, c_buf, c_out, send, recv)

    res = pl.pallas_call(
        carrier, name=name, grid=grid,
        in_specs=list(in_specs) + [ANY] * (ns + nb),
        out_specs=list(out_specs) + [ANY] * (nb + no),
        out_shape=list(out_shape) + [jax.ShapeDtypeStruct(b.shape, b.dtype) for b in comm.bufs] + list(comm.outs),
        input_output_aliases={n_in + ns + j: n_out + j for j in range(nb)},
        scratch_shapes=list(scratch_shapes) + [pltpu.SemaphoreType.DMA((comm.n_sem,)),
                                               pltpu.SemaphoreType.DMA((comm.n_sem,))],
        compiler_params=_cp(*(["arbitrary"] * len(grid))),
    )(*args, *comm.srcs, *comm.bufs)
    return tuple(res[:n_out]), tuple(res[n_out:])


def _comm_call(comm, name):
    ns, nb, no = len(comm.srcs), len(comm.bufs), len(comm.outs)

    def body(*refs):
        c_src = refs[0:ns]
        c_buf = refs[ns + nb:ns + 2 * nb]
        c_out = refs[ns + 2 * nb:ns + 2 * nb + no]
        send, recv = refs[ns + 2 * nb + no], refs[ns + 2 * nb + no + 1]
        comm.start(c_src, c_buf, c_out, send, recv)
        if comm.mid is not None:
            comm.mid(c_src, c_buf, c_out, send, recv)
        comm.finish(c_src, c_buf, c_out, send, recv)

    return tuple(pl.pallas_call(
        body, name=name,
        in_specs=[ANY] * (ns + nb), out_specs=[ANY] * (nb + no),
        out_shape=[jax.ShapeDtypeStruct(b.shape, b.dtype) for b in comm.bufs] + list(comm.outs),
        input_output_aliases={ns + j: j for j in range(nb)},
        scratch_shapes=[pltpu.SemaphoreType.DMA((comm.n_sem,)), pltpu.SemaphoreType.DMA((comm.n_sem,))],
    )(*comm.srcs, *comm.bufs))


def _sigmoid(v):
    return 1.0 / (1.0 + jnp.exp(-v))


def _sigmoid_tanh(v):
    return 0.5 + 0.5 * jnp.tanh(0.5 * v)


def _gelu_parts(v):
    v2 = v * v
    t = jnp.tanh(GELU_C0 * v * (1.0 + GELU_C1 * v2))
    half = 0.5 * (1.0 + t)
    gel = v * half
    dgel = half + 0.5 * v * (1.0 - t * t) * (GELU_C0 * (1.0 + 3.0 * GELU_C1 * v2))
    return gel, dgel


def _gelu(v):
    t = jnp.tanh(GELU_C0 * v * (1.0 + GELU_C1 * (v * v)))
    return 0.5 * v * (1.0 + t)


def _neg_expm1(y, a):
    p = jnp.full_like(y, 1.0 / 120.0)
    for coef in (1.0 / 24.0, 1.0 / 6.0, 0.5, 1.0):
        p = p * y + coef
    return jnp.where(y > -0.1, -(p * y), 1.0 - a * a)


def _softplus_neg(lam):
    nl = -lam
    e = jnp.exp(-jnp.abs(nl))
    u = 1.0 + e
    l1p = jnp.where(u == 1.0, e, jnp.log(u) * e / (u - 1.0))
    return jnp.maximum(nl, 0.0) + l1p


def _conv_taps(ext, taps, n_out):
    kw = len(taps)
    acc = taps[kw - 1] * ext[SUBLANES:SUBLANES + n_out]
    for k in range(kw - 1):
        acc = acc + taps[k] * pltpu.roll(ext, kw - 1 - k, axis=0)[SUBLANES:SUBLANES + n_out]
    return acc


def _conv_taps_t(ext, taps, n_out):
    kw = len(taps)
    n = ext.shape[0]
    acc = taps[kw - 1] * ext[0:n_out]
    for k in range(kw - 1):
        acc = acc + taps[k] * pltpu.roll(ext, n - (kw - 1 - k), axis=0)[0:n_out]
    return acc


def _scan8(a, b, carry, row):
    for s in (1, 2, 4):
        m = row >= s
        a_sh = jnp.where(m, pltpu.roll(a, s, axis=0), 1.0)
        b_sh = jnp.where(m, pltpu.roll(b, s, axis=0), 0.0)
        b = a * b_sh + b
        a = a * a_sh
    return a * carry + b


def _scan8_rev(a, b, carry, row):
    for s in (1, 2, 4):
        m = row < SUBLANES - s
        a_sh = jnp.where(m, pltpu.roll(a, SUBLANES - s, axis=0), 1.0)
        b_sh = jnp.where(m, pltpu.roll(b, SUBLANES - s, axis=0), 0.0)
        b = a * b_sh + b
        a = a * a_sh
    return a * carry + b


def _cast_into_full(w, kind, idx, name):
    nl, r, c = w.shape
    tr = 256 if r % 256 == 0 else r
    nrb = r // tr

    def body(idx_ref, w_ref, o0_ref, o1_ref):
        o0_ref[...] = w_ref[0].astype(BF16)
        o1_ref[...] = w_ref[1].astype(BF16)

    if COL_SHARDED[kind]:
        full = (r, N_CHIP * c)
        o_spec = pl.BlockSpec((tr, c), lambda i, idx_ref: (i, idx_ref[1]))
    else:
        full = (N_CHIP * r, c)
        o_spec = pl.BlockSpec((tr, c), lambda i, idx_ref: (idx_ref[1] * nrb + i, 0))
    return pl.pallas_call(
        body, name=name,
        grid_spec=pltpu.PrefetchScalarGridSpec(
            num_scalar_prefetch=1, grid=(nrb,),
            in_specs=[pl.BlockSpec((nl, tr, c), lambda i, idx_ref: (0, i, 0))], out_specs=[o_spec, o_spec]),
        out_shape=[jax.ShapeDtypeStruct(full, BF16)] * 2,
        compiler_params=_cp("parallel"),
    )(idx, w)


def _norm_mm(x, g, w, name, planes=False, tm=512, tn=512, comm=None):
    t_len, d = x.shape
    n = w.shape[1]
    half = n // 2

    def body(x_ref, g_ref, w_ref, z_ref, h_ref):
        xv = x_ref[...]
        r = lax.rsqrt(jnp.mean(xv * xv, axis=-1, keepdims=True) + EPS)
        h_ref[...] = ((xv * r) * g_ref[...]).astype(BF16)
        for n0 in range(0, n, tn):
            blk = jnp.dot(h_ref[...], w_ref[:, n0:n0 + tn], preferred_element_type=F32).astype(BF16)
            if planes:
                z_ref[n0 // half, :, n0 % half:n0 % half + tn] = blk
            else:
                z_ref[:, n0:n0 + tn] = blk

    if planes:
        z_shape = jax.ShapeDtypeStruct((2, t_len, half), BF16)
        z_spec = pl.BlockSpec((2, tm, half), lambda i: (0, i, 0))
    else:
        z_shape = jax.ShapeDtypeStruct((t_len, n), BF16)
        z_spec = pl.BlockSpec((tm, n), lambda i: (i, 0))
    return _pallas(
        body, name=name, grid=(t_len // tm,),
        in_specs=[pl.BlockSpec((tm, d), lambda i: (i, 0)),
                  pl.BlockSpec((1, d), lambda i: (0, 0)),
                  pl.BlockSpec((d, n), lambda i: (0, 0), pipeline_mode=ONCE)],
        out_specs=[z_spec, pl.BlockSpec((tm, d), lambda i: (i, 0))],
        out_shape=[z_shape, jax.ShapeDtypeStruct((t_len, d), BF16)],
        args=(x, g, w), sem=("parallel",), comm=comm)


def _mm_res(a, w, res, name, tm=512, comm=None):
    t_len, k = a.shape
    n = w.shape[1]

    def body(a_ref, w_ref, r_ref, o_ref):
        o_ref[...] = r_ref[...] + jnp.dot(a_ref[...], w_ref[...], preferred_element_type=F32)

    return _pallas(
        body, name=name, grid=(t_len // tm,),
        in_specs=[pl.BlockSpec((tm, k), lambda i: (i, 0)),
                  pl.BlockSpec((k, n), lambda i: (0, 0), pipeline_mode=ONCE),
                  pl.BlockSpec((tm, n), lambda i: (i, 0))],
        out_specs=[pl.BlockSpec((tm, n), lambda i: (i, 0))],
        out_shape=[jax.ShapeDtypeStruct((t_len, n), F32)],
        args=(a, w, res), sem=("parallel",), comm=comm)


def _mm_nt(a, w, name, tm=512, comm=None):
    t_len, k = a.shape
    n = w.shape[0]

    def body(a_ref, w_ref, o_ref):
        o_ref[...] = lax.dot_general(a_ref[...], w_ref[...], (((1,), (1,)), ((), ())),
                                     preferred_element_type=F32).astype(BF16)

    return _pallas(
        body, name=name, grid=(t_len // tm,),
        in_specs=[pl.BlockSpec((tm, k), lambda i: (i, 0)),
                  pl.BlockSpec((n, k), lambda i: (0, 0), pipeline_mode=ONCE)],
        out_specs=[pl.BlockSpec((tm, n), lambda i: (i, 0))],
        out_shape=[jax.ShapeDtypeStruct((t_len, n), BF16)],
        args=(a, w), sem=("parallel",), comm=comm)


def _mm_nt_normbwd(dz, w, x, g, dres, name, planes=False, tm=512, comm=None):
    t_len, d = x.shape
    n = w.shape[1]
    half = n // 2
    nt_dims = (((1,), (1,)), ((), ()))

    def body(dz_ref, w_ref, x_ref, g_ref, r_ref, dx_ref, dxb_ref, dg_ref):
        @pl.when(pl.program_id(0) == 0)
        def _():
            dg_ref[...] = jnp.zeros_like(dg_ref)

        if planes:
            dh = (lax.dot_general(dz_ref[0], w_ref[:, 0:half], nt_dims, preferred_element_type=F32)
                  + lax.dot_general(dz_ref[1], w_ref[:, half:], nt_dims, preferred_element_type=F32))
        else:
            dh = lax.dot_general(dz_ref[...], w_ref[...], nt_dims, preferred_element_type=F32)
        xv = x_ref[...]
        r = lax.rsqrt(jnp.mean(xv * xv, axis=-1, keepdims=True) + EPS)
        xh = xv * r
        dhg = dh * g_ref[...]
        dx = r_ref[...] + r * (dhg - xh * jnp.mean(dhg * xh, axis=-1, keepdims=True))
        dx_ref[...] = dx
        dxb_ref[...] = dx.astype(BF16)
        dg_ref[0:1, :] += jnp.sum(dh * xh, axis=0, keepdims=True)

    if planes:
        dz_spec = pl.BlockSpec((2, tm, half), lambda i: (0, i, 0))
    else:
        dz_spec = pl.BlockSpec((tm, n), lambda i: (i, 0))
    return _pallas(
        body, name=name, grid=(t_len // tm,),
        in_specs=[dz_spec,
                  pl.BlockSpec((d, n), lambda i: (0, 0), pipeline_mode=ONCE),
                  pl.BlockSpec((tm, d), lambda i: (i, 0)),
                  pl.BlockSpec((1, d), lambda i: (0, 0)),
                  pl.BlockSpec((tm, d), lambda i: (i, 0))],
        out_specs=[pl.BlockSpec((tm, d), lambda i: (i, 0)),
                   pl.BlockSpec((tm, d), lambda i: (i, 0)),
                   pl.BlockSpec((SUBLANES, d), lambda i: (0, 0))],
        out_shape=[jax.ShapeDtypeStruct((t_len, d), F32),
                   jax.ShapeDtypeStruct((t_len, d), BF16),
                   jax.ShapeDtypeStruct((SUBLANES, d), F32)],
        args=(dz, w, x, g, dres), sem=("arbitrary",), comm=comm)


def _mm_tn(a, g, name, tk, tn, planes=False, tt=1024, comm=None):
    t_len, k = a.shape
    n = 2 * g.shape[2] if planes else g.shape[1]
    nn = n // tn
    half = nn // 2
    tt = min(tt, t_len)

    def body(a_ref, g_ref, o_ref):
        @pl.when(pl.program_id(2) == 0)
        def _():
            o_ref[...] = jnp.zeros_like(o_ref)

        o_ref[...] += lax.dot_general(a_ref[...], g_ref[...], (((0,), (0,)), ((), ())),
                                      preferred_element_type=F32)

    if planes:
        g_spec = pl.BlockSpec((None, tt, tn), lambda i, j, t: (j // half, t, j % half))
    else:
        g_spec = pl.BlockSpec((tt, tn), lambda i, j, t: (t, j))
    return _pallas(
        body, name=name, grid=(k // tk, nn, t_len // tt),
        in_specs=[pl.BlockSpec((tt, tk), lambda i, j, t: (t, i)), g_spec],
        out_specs=[pl.BlockSpec((tk, tn), lambda i, j, t: (i, j))],
        out_shape=[jax.ShapeDtypeStruct((k, n), F32)],
        args=(a, g), sem=("parallel", "parallel", "arbitrary"), comm=comm)


def _lru_gates(rp, ip, spn):
    r = _sigmoid(rp)
    i = _sigmoid_tanh(ip)
    la = r * spn
    a = jnp.exp(la)
    mult = jnp.sqrt(_neg_expm1(2.0 * la, a))
    return r, i, a, mult


def _mixer_fwd(z, prm, gates, layer, name, tb=256, comm=None):
    t_len = z.shape[0]

    def body(z_ref, p_ref, g_ref, y_ref, h_ref, xhalo, phalo, hcar, lx_s, rp_s, ip_s):
        @pl.when(pl.program_id(0) == 0)
        def _():
            xhalo[...] = jnp.zeros_like(xhalo)
            phalo[...] = jnp.zeros_like(phalo)
            hcar[...] = jnp.zeros_like(hcar)

        prm_v = p_ref[...]
        cw = prm_v[0:4]
        vec = prm_v[4:8]
        xp = z_ref[:, 0:D_LRU].astype(F32)
        ext = jnp.concatenate([xhalo[...], xp], axis=0)
        lx = vec[0:1] + _conv_taps(ext, [cw[k:k + 1] for k in range(4)], tb)
        xhalo[...] = xp[tb - SUBLANES:]
        lx_s[...] = lx
        lxb = lx.astype(BF16)
        for q in range(4):
            sl = slice(q * 256, (q + 1) * 256)
            rp_s[:, sl] = jnp.dot(lxb[:, sl], g_ref[q], preferred_element_type=F32) + vec[1:2, sl]
            ip_s[:, sl] = jnp.dot(lxb[:, sl], g_ref[4 + q], preferred_element_type=F32) + vec[2:3, sl]

        spn = jnp.broadcast_to(-RG_C * _softplus_neg(vec[3:4]), (SUBLANES, D_LRU))
        row = lax.broadcasted_iota(jnp.int32, (SUBLANES, D_LRU), 0)

        def step(ci, carry):
            o = pl.multiple_of(ci * PACKED, PACKED)
            gate = z_ref[pl.ds(o, PACKED), D_LRU:2 * D_LRU].astype(F32)
            ys = []
            for sub in range(2):
                rows = pl.ds(pl.multiple_of(o + sub * SUBLANES, SUBLANES), SUBLANES)
                lxv = lx_s[rows, :]
                _, i, a, mult = _lru_gates(rp_s[rows, :], ip_s[rows, :], spn)
                h = _scan8(a, mult * (i * lxv), carry, row)
                h_ref[rows, :] = h
                ys.append(h * _gelu(gate[sub * SUBLANES:(sub + 1) * SUBLANES]))
                carry = jnp.broadcast_to(h[SUBLANES - 1:SUBLANES, :], (SUBLANES, D_LRU))
            y_ref[pl.ds(o, PACKED), 0:D_LRU] = jnp.concatenate(ys, axis=0).astype(BF16)
            return carry

        hcar[...] = lax.fori_loop(0, tb // PACKED, step, hcar[...])

        scw = prm_v[8:11, 0:D_SC]
        o_b, o_c, o_x = 2 * D_LRU, 2 * D_LRU + D_SC, 2 * D_LRU + 2 * D_SC
        p = z_ref[:, o_c:o_x].astype(F32) * z_ref[:, o_x:].astype(F32)
        pext = jnp.concatenate([phalo[...], p], axis=0)
        q = _conv_taps(pext, [scw[k:k + 1] for k in range(3)], tb)
        phalo[...] = p[tb - SUBLANES:]
        y_ref[:, D_LRU:] = (z_ref[:, o_b:o_c].astype(F32) * q).astype(BF16)

    return _pallas(
        body, name=name, grid=(t_len // tb,),
        in_specs=[pl.BlockSpec((tb, D_IN), lambda t: (t, 0)),
                  pl.BlockSpec((None, 2 * SUBLANES, D_LRU), lambda t: (layer, 0, 0)),
                  pl.BlockSpec((None, 8, 256, 256), lambda t: (layer, 0, 0, 0))],
        out_specs=[pl.BlockSpec((tb, D_MIX), lambda t: (t, 0)),
                   pl.BlockSpec((tb, D_LRU), lambda t: (t, 0))],
        out_shape=[jax.ShapeDtypeStruct((t_len, D_MIX), BF16),
                   jax.ShapeDtypeStruct((t_len, D_LRU), F32)],
        scratch_shapes=[pltpu.VMEM((SUBLANES, D_LRU), F32), pltpu.VMEM((SUBLANES, D_SC), F32),
                        pltpu.VMEM((SUBLANES, D_LRU), F32), pltpu.VMEM((tb, D_LRU), F32),
                        pltpu.VMEM((tb, D_LRU), F32), pltpu.VMEM((tb, D_LRU), F32)],
        args=(z, prm, gates), sem=("arbitrary",), comm=comm)


def _mixer_bwd(z, h, dy, prm, gates, layer, name, tb=256, comm=None):
    t_len = z.shape[0]
    nb = t_len // tb

    def body(z_ref, zh_ref, h_ref, hh_ref, dy_ref, p_ref, g_ref, dz_ref, dp_ref, dg_ref,
             lx_s, rp_s, ip_s, drpb_s, dipb_s, dlx_s, hext_s, acc_s, acar, gcar, dqh):
        t = pl.program_id(0)
        first_block = t == nb - 1

        @pl.when(t == 0)
        def _():
            for ref in (dp_ref, dg_ref, acc_s, acar, gcar, dqh):
                ref[...] = jnp.zeros_like(ref)
            dlx_s[tb:, :] = jnp.zeros((SUBLANES, D_LRU), F32)

        prm_v = p_ref[...]
        cw = prm_v[0:4]
        vec = prm_v[4:8]
        scw = prm_v[8:11, 0:D_SC]
        wa_ref = [g_ref.at[q] for q in range(4)]
        wx_ref = [g_ref.at[4 + q] for q in range(4)]
        dwa_ref = [dg_ref.at[q] for q in range(4)]
        dwx_ref = [dg_ref.at[4 + q] for q in range(4)]
        ctaps = [cw[k:k + 1] for k in range(4)]
        staps = [scw[k:k + 1] for k in range(3)]
        keep = jnp.where(first_block, 0.0, 1.0)
        zh = zh_ref[...].astype(F32)[PACKED - SUBLANES:] * keep

        xp = z_ref[:, 0:D_LRU].astype(F32)
        xext = jnp.concatenate([zh[:, 0:D_LRU], xp], axis=0)
        lx = vec[0:1] + _conv_taps(xext, ctaps, tb)
        lx_s[...] = lx
        lxb = lx.astype(BF16)
        for q in range(4):
            sl = slice(q * 256, (q + 1) * 256)
            rp_s[:, sl] = jnp.dot(lxb[:, sl], wa_ref[q][...], preferred_element_type=F32) + vec[1:2, sl]
            ip_s[:, sl] = jnp.dot(lxb[:, sl], wx_ref[q][...], preferred_element_type=F32) + vec[2:3, sl]
        hext_s[0:SUBLANES, :] = hh_ref[...] * keep
        hext_s[SUBLANES:, :] = h_ref[...]

        spn = jnp.broadcast_to(-RG_C * _softplus_neg(vec[3:4]), (SUBLANES, D_LRU))
        row = lax.broadcasted_iota(jnp.int32, (SUBLANES, D_LRU), 0)

        def step(ci, carry):
            a_next, g_next = carry
            o = pl.multiple_of((tb // PACKED - 1 - ci) * PACKED, PACKED)
            rows16 = pl.ds(o, PACKED)
            gate16 = z_ref[rows16, D_LRU:2 * D_LRU].astype(F32)
            dyl16 = dy_ref[rows16, 0:D_LRU].astype(F32)
            dgs, drs, dis = [None, None], [None, None], [None, None]
            for sub in (1, 0):
                oo = pl.multiple_of(o + sub * SUBLANES, SUBLANES)
                rows = pl.ds(oo, SUBLANES)
                half = slice(sub * SUBLANES, (sub + 1) * SUBLANES)
                lxv = lx_s[rows, :]
                r, i, a, mult = _lru_gates(rp_s[rows, :], ip_s[rows, :], spn)
                hwin = hext_s[pl.ds(oo, 2 * SUBLANES), :]
                hv = hwin[SUBLANES:]
                hprev = pltpu.roll(hwin, 1, axis=0)[SUBLANES:]
                gel, dgel = _gelu_parts(gate16[half])
                dyl = dyl16[half]
                a_up = jnp.where(row < SUBLANES - 1, pltpu.roll(a, SUBLANES - 1, axis=0), a_next)
                gg = _scan8_rev(a_up, dyl * gel, g_next, row)
                dgs[sub] = dyl * hv * dgel
                ilx = i * lxv
                dla = gg * hprev * a - (gg * ilx) * (a * a) / mult
                dlx_s[rows, :] = gg * mult * i
                drp = dla * spn * r * (1.0 - r)
                dip = gg * mult * lxv * i * (1.0 - i)
                drs[sub] = drp
                dis[sub] = dip
                acc_s[0] += drp
                acc_s[1] += dip
                acc_s[2] += dla * r
                a_next = jnp.broadcast_to(a[0:1, :], (SUBLANES, D_LRU))
                g_next = jnp.broadcast_to(gg[0:1, :], (SUBLANES, D_LRU))
            dz_ref[rows16, D_LRU:2 * D_LRU] = jnp.concatenate(dgs, axis=0).astype(BF16)
            drpb_s[rows16, :] = jnp.concatenate(drs, axis=0).astype(BF16)
            dipb_s[rows16, :] = jnp.concatenate(dis, axis=0).astype(BF16)
            return a_next, g_next

        a_c, g_c = lax.fori_loop(0, tb // PACKED, step, (acar[...], gcar[...]))
        acar[...] = a_c
        gcar[...] = g_c

        drpb = drpb_s[...]
        dipb = dipb_s[...]
        nt_dims = (((1,), (1,)), ((), ()))
        tn_dims = (((0,), (0,)), ((), ()))
        for q in range(4):
            sl = slice(q * 256, (q + 1) * 256)
            dlx_s[0:tb, sl] += (
                lax.dot_general(drpb[:, sl], wa_ref[q][...], nt_dims, preferred_element_type=F32)
                + lax.dot_general(dipb[:, sl], wx_ref[q][...], nt_dims, preferred_element_type=F32))
            dwa_ref[q][...] += lax.dot_general(lxb[:, sl], drpb[:, sl], tn_dims, preferred_element_type=F32)
            dwx_ref[q][...] += lax.dot_general(lxb[:, sl], dipb[:, sl], tn_dims, preferred_element_type=F32)

        dlx_ext = dlx_s[...]
        dlx = dlx_ext[0:tb]
        dz_ref[:, 0:D_LRU] = _conv_taps_t(dlx_ext, ctaps, tb).astype(BF16)
        dp_ref[3:4, :] += jnp.sum(dlx * xp, axis=0, keepdims=True)
        for k in range(3):
            shifted = pltpu.roll(xext, 3 - k, axis=0)[SUBLANES:]
            dp_ref[k:k + 1, :] += jnp.sum(dlx * shifted, axis=0, keepdims=True)
        dp_ref[4:5, :] += jnp.sum(dlx, axis=0, keepdims=True)
        dlx_s[tb:, :] = dlx[0:SUBLANES]

        o_b, o_c, o_x = 2 * D_LRU, 2 * D_LRU + D_SC, 2 * D_LRU + 2 * D_SC
        sb = z_ref[:, o_b:o_c].astype(F32)
        scc = z_ref[:, o_c:o_x].astype(F32)
        sx = z_ref[:, o_x:].astype(F32)
        p = scc * sx
        pext = jnp.concatenate([zh[:, o_c:o_x] * zh[:, o_x:], p], axis=0)
        q = _conv_taps(pext, staps, tb)
        dys = dy_ref[:, D_LRU:].astype(F32)
        dq = dys * sb
        dp = _conv_taps_t(jnp.concatenate([dq, dqh[...]], axis=0), staps, tb)
        dp_ref[10:11, 0:D_SC] += jnp.sum(dq * p, axis=0, keepdims=True)
        for k in range(2):
            shifted = pltpu.roll(pext, 2 - k, axis=0)[SUBLANES:]
            dp_ref[8 + k:9 + k, 0:D_SC] += jnp.sum(dq * shifted, axis=0, keepdims=True)
        dqh[...] = dq[0:SUBLANES]
        dz_ref[:, o_b:o_c] = (dys * q).astype(BF16)
        dz_ref[:, o_c:o_x] = (dp * sx).astype(BF16)
        dz_ref[:, o_x:] = (dp * scc).astype(BF16)

        @pl.when(first_block)
        def _():
            dp_ref[5:6, :] = jnp.sum(acc_s[0], axis=0, keepdims=True)
            dp_ref[6:7, :] = jnp.sum(acc_s[1], axis=0, keepdims=True)
            dp_ref[7:8, :] = (jnp.sum(acc_s[2], axis=0, keepdims=True) * RG_C * _sigmoid(-vec[3:4]))

    blk = lambda t: (nb - 1 - t, 0)
    halo8 = lambda t: (jnp.maximum((nb - 1 - t) * (tb // SUBLANES) - 1, 0), 0)
    halo16 = lambda t: (jnp.maximum((nb - 1 - t) * (tb // PACKED) - 1, 0), 0)
    return _pallas(
        body, name=name, grid=(nb,),
        in_specs=[pl.BlockSpec((tb, D_IN), blk), pl.BlockSpec((PACKED, D_IN), halo16),
                  pl.BlockSpec((tb, D_LRU), blk), pl.BlockSpec((SUBLANES, D_LRU), halo8),
                  pl.BlockSpec((tb, D_MIX), blk),
                  pl.BlockSpec((None, 2 * SUBLANES, D_LRU), lambda t: (layer, 0, 0)),
                  pl.BlockSpec((None, 8, 256, 256), lambda t: (layer, 0, 0, 0))],
        out_specs=[pl.BlockSpec((tb, D_IN), blk),
                   pl.BlockSpec((2 * SUBLANES, D_LRU), lambda t: (0, 0)),
                   pl.BlockSpec((8, 256, 256), lambda t: (0, 0, 0))],
        out_shape=[jax.ShapeDtypeStruct((t_len, D_IN), BF16),
                   jax.ShapeDtypeStruct((2 * SUBLANES, D_LRU), F32),
                   jax.ShapeDtypeStruct((8, 256, 256), F32)],
        scratch_shapes=[pltpu.VMEM((tb, D_LRU), F32),
                        pltpu.VMEM((tb, D_LRU), F32), pltpu.VMEM((tb, D_LRU), F32),
                        pltpu.VMEM((tb, D_LRU), BF16), pltpu.VMEM((tb, D_LRU), BF16),
                        pltpu.VMEM((tb + SUBLANES, D_LRU), F32), pltpu.VMEM((tb + SUBLANES, D_LRU), F32),
                        pltpu.VMEM((3, SUBLANES, D_LRU), F32),
                        pltpu.VMEM((SUBLANES, D_LRU), F32), pltpu.VMEM((SUBLANES, D_LRU), F32),
                        pltpu.VMEM((SUBLANES, D_SC), F32)],
        args=(z, z, h, h, dy, prm, gates), sem=("arbitrary",), comm=comm)


def _ffn_act(u, fw, layer, name, tb=512, tn=1024, rc=64, comm=None):
    t_len = u.shape[1]
    hb = tb // PACKED

    def body(u_ref, uh_ref, fw_ref, o_ref, ext):
        keep = jnp.where(pl.program_id(0) == 0, 0.0, 1.0)
        ext[:, 0:SUBLANES, :] = uh_ref[...].astype(F32)[:, PACKED - SUBLANES:, :] * keep
        ext[:, SUBLANES:, :] = u_ref[...].astype(F32)
        fw_v = fw_ref[...]

        for lb in range(tn // LANES):
            lanes = slice(lb * LANES, (lb + 1) * LANES)
            wg = [fw_v[0, k:k + 1, lanes] for k in range(3)]
            wu = [fw_v[1, k:k + 1, lanes] for k in range(3)]

            def chunk(ci, c, lanes=lanes, wg=wg, wu=wu):
                o = pl.multiple_of(ci * rc, rc)
                win = pl.ds(o, rc + SUBLANES)
                gate = _conv_taps(ext[0, win, lanes], wg, rc)
                up = _conv_taps(ext[1, win, lanes], wu, rc)
                o_ref[pl.ds(o, rc), lanes] = (_gelu(gate) * up).astype(BF16)
                return c

            lax.fori_loop(0, tb // rc, chunk, 0)

    return _pallas(
        body, name=name, grid=(t_len // tb, D_FF // tn),
        in_specs=[pl.BlockSpec((2, tb, tn), lambda i, j: (0, i, j)),
                  pl.BlockSpec((2, PACKED, tn), lambda i, j: (0, jnp.maximum(i * hb - 1, 0), j)),
                  pl.BlockSpec((None, 2, SUBLANES, tn), lambda i, j: (layer, 0, 0, j))],
        out_specs=[pl.BlockSpec((tb, tn), lambda i, j: (i, j))],
        out_shape=[jax.ShapeDtypeStruct((t_len, D_FF), BF16)],
        scratch_shapes=[pltpu.VMEM((2, tb + SUBLANES, tn), F32)],
        args=(u, u, fw), sem=("parallel", "parallel"), comm=comm)


def _ffn_bwd(dact, u, fw, layer, name, tb=512, tn=1024, rc=32, comm=None):
    t_len = u.shape[1]
    ni = t_len // tb
    hb = tb // PACKED
    last_halo = t_len // PACKED - 1

    def body(d_ref, dn_ref, u_ref, up_ref, un_ref, fw_ref, du_ref, dfw_ref, extu, extd, acc):
        i = pl.program_id(1)

        @pl.when(i == 0)
        def _():
            acc[...] = jnp.zeros_like(acc)

        keep_prev = jnp.where(i == 0, 0.0, 1.0)
        keep_next = jnp.where(i == ni - 1, 0.0, 1.0)
        extu[:, 0:SUBLANES, :] = up_ref[...].astype(F32)[:, PACKED - SUBLANES:, :] * keep_prev
        extu[:, SUBLANES:SUBLANES + tb, :] = u_ref[...].astype(F32)
        extu[:, SUBLANES + tb:, :] = un_ref[...].astype(F32)[:, 0:SUBLANES, :]
        extd[0:tb, :] = d_ref[...].astype(F32)
        extd[tb:, :] = dn_ref[...].astype(F32)[0:SUBLANES] * keep_next
        fw_v = fw_ref[...]
        m = rc + SUBLANES

        for lb in range(tn // LANES):
            lanes = slice(lb * LANES, (lb + 1) * LANES)
            taps = [[fw_v[pln, k:k + 1, lanes] for k in range(3)] for pln in range(2)]

            def chunk(ci, c, lanes=lanes, taps=taps):
                o = pl.multiple_of(ci * rc, rc)
                win = pl.ds(o, rc + 2 * SUBLANES)
                sh = []
                for pln in range(2):
                    e = extu[pln, win, lanes]
                    sh.append([pltpu.roll(e, 2, axis=0)[SUBLANES:], pltpu.roll(e, 1, axis=0)[SUBLANES:],
                               e[SUBLANES:]])
                gate = sum(taps[0][k] * sh[0][k] for k in range(3))
                up = sum(taps[1][k] * sh[1][k] for k in range(3))
                dv = extd[pl.ds(o, m), lanes]
                gel, dgel = _gelu_parts(gate)
                dpost = [dv * up * dgel, dv * gel]
                for pln in range(2):
                    du_ref[pln, pl.ds(o, rc), lanes] = _conv_taps_t(dpost[pln], taps[pln], rc).astype(BF16)
                    for k in range(3):
                        prod = dpost[pln][0:rc] * sh[pln][k][0:rc]
                        acc[3 * pln + k, :, lanes] += sum(
                            prod[s:s + SUBLANES] for s in range(0, rc, SUBLANES))
                return c

            lax.fori_loop(0, tb // rc, chunk, 0)

        @pl.when(i == ni - 1)
        def _():
            dfw_ref[...] = jnp.zeros_like(dfw_ref)
            for pln in range(2):
                for k in range(3):
                    dfw_ref[pln, k:k + 1, :] = jnp.sum(acc[3 * pln + k], axis=0, keepdims=True)

    return _pallas(
        body, name=name, grid=(D_FF // tn, ni),
        in_specs=[pl.BlockSpec((tb, tn), lambda j, i: (i, j)),
                  pl.BlockSpec((PACKED, tn), lambda j, i: (jnp.minimum((i + 1) * hb, last_halo), j)),
                  pl.BlockSpec((2, tb, tn), lambda j, i: (0, i, j)),
                  pl.BlockSpec((2, PACKED, tn), lambda j, i: (0, jnp.maximum(i * hb - 1, 0), j)),
                  pl.BlockSpec((2, PACKED, tn), lambda j, i: (0, jnp.minimum((i + 1) * hb, last_halo), j)),
                  pl.BlockSpec((None, 2, SUBLANES, tn), lambda j, i: (layer, 0, 0, j))],
        out_specs=[pl.BlockSpec((2, tb, tn), lambda j, i: (0, i, j)),
                   pl.BlockSpec((2, SUBLANES, tn), lambda j, i: (0, 0, j))],
        out_shape=[jax.ShapeDtypeStruct((2, t_len, D_FF), BF16),
                   jax.ShapeDtypeStruct((2, SUBLANES, D_FF), F32)],
        scratch_shapes=[pltpu.VMEM((2, tb + 2 * SUBLANES, tn), F32),
                        pltpu.VMEM((tb + SUBLANES, tn), F32),
                        pltpu.VMEM((6, SUBLANES, tn), F32)],
        args=(dact, dact, u, u, u, fw), sem=("parallel", "arbitrary"), comm=comm)


def _loss_head(x, g, target, name, tb=256):
    t_len, d = x.shape

    def body(x_ref, g_ref, t_ref, dx_ref, dxb_ref, dg_ref, loss_ref):
        @pl.when(pl.program_id(0) == 0)
        def _():
            dg_ref[...] = jnp.zeros_like(dg_ref)
            loss_ref[...] = jnp.zeros_like(loss_ref)

        xv = x_ref[...]
        gv = g_ref[...]
        r = lax.rsqrt(jnp.mean(xv * xv, axis=-1, keepdims=True) + EPS)
        xh = xv * r
        err = xh * gv - t_ref[...]
        loss_ref[...] += (0.5 / d) * jnp.sum(jnp.sum(err * err, axis=-1, keepdims=True), axis=0, keepdims=True)
        dy = err * (1.0 / d)
        dyg = dy * gv
        dx = r * (dyg - xh * jnp.mean(dyg * xh, axis=-1, keepdims=True))
        dx_ref[...] = dx
        dxb_ref[...] = dx.astype(BF16)
        dg_ref[0:1, :] += jnp.sum(dy * xh, axis=0, keepdims=True)

    return _pallas(
        body, name=name, grid=(t_len // tb,),
        in_specs=[pl.BlockSpec((tb, d), lambda i: (i, 0)), pl.BlockSpec((1, d), lambda i: (0, 0)),
                  pl.BlockSpec((tb, d), lambda i: (i, 0))],
        out_specs=[pl.BlockSpec((tb, d), lambda i: (i, 0)), pl.BlockSpec((tb, d), lambda i: (i, 0)),
                   pl.BlockSpec((SUBLANES, d), lambda i: (0, 0)),
                   pl.BlockSpec((SUBLANES, LANES), lambda i: (0, 0))],
        out_shape=[jax.ShapeDtypeStruct((t_len, d), F32), jax.ShapeDtypeStruct((t_len, d), BF16),
                   jax.ShapeDtypeStruct((SUBLANES, d), F32), jax.ShapeDtypeStruct((SUBLANES, LANES), F32)],
        args=(x, g, target), sem=("arbitrary",))[0]


def _adamw(w, g, m, v, name, emit_grad=False):
    r, c = w.shape
    tr = 256 if r % 256 == 0 else r
    c1 = 1.0 / (1.0 - ADAM_B1 ** ADAM_STEP)
    c2 = 1.0 / (1.0 - ADAM_B2 ** ADAM_STEP)

    def body(w_ref, g_ref, m_ref, v_ref, d_ref, mo_ref, vo_ref, *go_ref):
        gv = g_ref[...]
        mn = ADAM_B1 * m_ref[...] + (1.0 - ADAM_B1) * gv
        vn = ADAM_B2 * v_ref[...] + (1.0 - ADAM_B2) * (gv * gv)
        d_ref[...] = -ADAM_LR * ((mn * c1) / (jnp.sqrt(vn * c2) + ADAM_EPS) + ADAM_WD * w_ref[...])
        mo_ref[...] = mn
        vo_ref[...] = vn
        if emit_grad:
            go_ref[0][...] = gv

    spec = pl.BlockSpec((tr, c), lambda i: (i, 0))
    shape = jax.ShapeDtypeStruct((r, c), F32)
    n_out = 4 if emit_grad else 3
    return _pallas(
        body, name=name, grid=(r // tr,),
        in_specs=[spec] * 4, out_specs=[spec] * n_out, out_shape=[shape] * n_out,
        args=(w, g, m, v), sem=("parallel",))[0]


def _place():
    x, y, c = lax.axis_index("x"), lax.axis_index("y"), lax.axis_index("c")
    chips = [(1 - x, y), (x, 1 - y), (1 - x, 1 - y)]
    return x, y, c, chips


def _remote(src, dst, send, recv, sem, to):
    return pltpu.make_async_remote_copy(
        src_ref=src, dst_ref=dst, send_sem=send.at[sem], recv_sem=recv.at[sem], device_id=to, device_id_type=MESH)


def _gather_plan(fulls, kinds, mid_at=None, parts=None):
    parts = parts or [(0, 1)] * len(fulls)

    def region(it, f, k, cc):
        kind = kinds[it]
        p, n = parts[it]
        if kind == SMALL:
            return f.at[k, pl.ds(cc * (CONV_PACK_ROWS // 2), CONV_PACK_ROWS // 2), :]
        if COL_SHARDED[kind]:
            rows, cols = f.shape[0] // (2 * n), f.shape[1] // N_CHIP
            return f.at[pl.ds((cc * n + p) * rows, rows), pl.ds(k * cols, cols)]
        assert n == 1
        rows = f.shape[0] // N_CHIP
        return f.at[pl.ds(k * rows + cc * (rows // 2), rows // 2), :]

    def first_hop(bufs, send, recv, it, j):
        x, y, c, chips = _place()
        reg = region(it, bufs[it], 2 * x + y, c)
        return _remote(reg, reg, send, recv, it * 6 + j, (*chips[j], c))

    def arrival(bufs, send, recv, it, j, second):
        x, y, c, chips = _place()
        px, py = chips[j]
        reg = region(it, bufs[it], 2 * px + py, 1 - c if second else c)
        to = (x, y, 1 - c) if second else (px, py, c)
        return _remote(reg, reg, send, recv, it * 6 + (3 + j if second else j), to)

    def forward(bufs, send, recv, it, j):
        x, y, c, chips = _place()
        px, py = chips[j]
        reg = region(it, bufs[it], 2 * px + py, c)
        return _remote(reg, reg, send, recv, it * 6 + 3 + j, (x, y, 1 - c))

    def start(srcs, bufs, outs, send, recv):
        for it in range(len(bufs)):
            for j in range(3):
                first_hop(bufs, send, recv, it, j).start()

    def mid(srcs, bufs, outs, send, recv):
        for it in range(len(bufs)):
            for j in range(3):
                arrival(bufs, send, recv, it, j, False).wait_recv()
                forward(bufs, send, recv, it, j).start()

    def finish(srcs, bufs, outs, send, recv):
        for it in range(len(bufs)):
            for j in range(3):
                arrival(bufs, send, recv, it, j, True).wait_recv()
        for it in range(len(bufs)):
            for j in range(3):
                first_hop(bufs, send, recv, it, j).wait_send()
                forward(bufs, send, recv, it, j).wait_send()

    return Comm(srcs=(), bufs=tuple(fulls), outs=(), n_sem=6 * len(fulls), start=start, mid=mid, finish=finish,
                mid_at=mid_at)


def _half_axis(kind):
    return 0 if kind == SMALL or COL_SHARDED[kind] else 1


def _half2(ref, kind, cc):
    if _half_axis(kind) == 0:
        return ref.at[pl.ds(cc * (ref.shape[0] // 2), ref.shape[0] // 2), :]
    return ref.at[:, pl.ds(cc * (ref.shape[1] // 2), ref.shape[1] // 2)]


def _pair_plan(grads, kinds):
    def land_shape(g, kind):
        s = list(g.shape)
        s[_half_axis(kind)] //= 2
        return jax.ShapeDtypeStruct(tuple(s), F32)

    def copy(srcs, outs, send, recv, it):
        x, y, c, _ = _place()
        return _remote(_half2(srcs[it], kinds[it], 1 - c), outs[it], send, recv, it, (x, y, 1 - c))

    def start(srcs, bufs, outs, send, recv):
        for it in range(len(srcs)):
            copy(srcs, outs, send, recv, it).start()

    def finish(srcs, bufs, outs, send, recv):
        for it in range(len(srcs)):
            copy(srcs, outs, send, recv, it).wait_send()
        for it in range(len(srcs)):
            copy(srcs, outs, send, recv, it).wait_recv()

    return Comm(srcs=tuple(grads), bufs=(), outs=tuple(land_shape(g, k) for g, k in zip(grads, kinds)),
                n_sem=len(grads), start=start, finish=finish)


def _scatter_plan(parts, slots, kinds):
    def piece(s, kind, k):
        if kind == SMALL:
            return s
        if COL_SHARDED[kind]:
            n = s.shape[1] // N_CHIP
            return s.at[:, pl.ds(k * n, n)]
        n = s.shape[0] // N_CHIP
        return s.at[pl.ds(k * n, n), :]

    def outbound(srcs, bufs, send, recv, it, j):
        x, y, c, chips = _place()
        px, py = chips[j]
        return _remote(piece(srcs[it], kinds[it], 2 * px + py), bufs[it].at[2 * x + y], send, recv, it * 3 + j,
                       (px, py, c))

    def inbound(bufs, send, recv, it, j):
        x, y, c, chips = _place()
        px, py = chips[j]
        got = bufs[it].at[2 * px + py]
        return _remote(got, got, send, recv, it * 3 + j, (px, py, c))

    def start(srcs, bufs, outs, send, recv):
        for it in range(len(srcs)):
            for j in range(3):
                outbound(srcs, bufs, send, recv, it, j).start()

    def finish(srcs, bufs, outs, send, recv):
        for it in range(len(srcs)):
            for j in range(3):
                inbound(bufs, send, recv, it, j).wait_recv()
        for it in range(len(srcs)):
            for j in range(3):
                outbound(srcs, bufs, send, recv, it, j).wait_send()

    return Comm(srcs=tuple(parts), bufs=tuple(slots), outs=(), n_sem=3 * len(parts), start=start, finish=finish)


def _share_plan(fulls, kinds, layer):
    def half(f, kind, cc):
        return _half2(f if kind == SMALL else f.at[layer], kind, cc)

    def copy(bufs, send, recv, it, cc):
        x, y, c, _ = _place()
        reg = half(bufs[it], kinds[it], c if cc == "mine" else 1 - c)
        return _remote(reg, reg, send, recv, it, (x, y, 1 - c))

    def start(srcs, bufs, outs, send, recv):
        for it in range(len(bufs)):
            copy(bufs, send, recv, it, "mine").start()

    def finish(srcs, bufs, outs, send, recv):
        for it in range(len(bufs)):
            copy(bufs, send, recv, it, "other").wait_recv()
        for it in range(len(bufs)):
            copy(bufs, send, recv, it, "mine").wait_send()

    return Comm(srcs=(), bufs=tuple(fulls), outs=(), n_sem=len(fulls), start=start, finish=finish)


def _pair_sum(g, land, idx, kind, name):
    odt = F32 if kind == SMALL else BF16
    r, cdim = land.shape

    def body(idx_ref, g_ref, l_ref, p_ref, s_ref):
        v = (g_ref[...] + l_ref[...]).astype(odt)
        p_ref[...] = v
        if kind == SMALL:
            s_ref[...] = v
        else:
            @pl.when(pl.program_id(1 if COL_SHARDED[kind] else 0) == idx_ref[1])
            def _():
                s_ref[...] = v

    if kind == SMALL:
        grid = (1,)
        g_spec = pl.BlockSpec((r, LANES), lambda i, idx_ref: (idx_ref[0], 0))
        spec = pl.BlockSpec((r, LANES), lambda i, idx_ref: (0, 0))
        s_spec = pl.BlockSpec((None, r, LANES), lambda i, idx_ref: (idx_ref[1], 0, 0))
        s_shape = (N_CHIP, r, LANES)
    elif COL_SHARDED[kind]:
        pc, tr = cdim // N_CHIP, 256
        nrb = r // tr
        grid = (nrb, N_CHIP)
        g_spec = pl.BlockSpec((tr, pc), lambda i, k, idx_ref: (idx_ref[0] * nrb + i, k))
        spec = pl.BlockSpec((tr, pc), lambda i, k, idx_ref: (i, k))
        s_spec = pl.BlockSpec((None, tr, pc), lambda i, k, idx_ref: (idx_ref[1], i, 0))
        s_shape = (N_CHIP, r, pc)
    else:
        pr = r // N_CHIP
        grid = (N_CHIP,)
        g_spec = pl.BlockSpec((pr, cdim), lambda k, idx_ref: (k, idx_ref[0]))
        spec = pl.BlockSpec((pr, cdim), lambda k, idx_ref: (k, 0))
        s_spec = pl.BlockSpec((None, pr, cdim), lambda k, idx_ref: (idx_ref[1], 0, 0))
        s_shape = (N_CHIP, pr, cdim)
    return pl.pallas_call(
        body, name=name,
        grid_spec=pltpu.PrefetchScalarGridSpec(
            num_scalar_prefetch=1, grid=grid, in_specs=[g_spec, spec], out_specs=[spec, s_spec]),
        out_shape=[jax.ShapeDtypeStruct(land.shape, odt), jax.ShapeDtypeStruct(s_shape, odt)],
        compiler_params=_cp(*(["arbitrary"] * len(grid))),
    )(idx, g, land)


def _sum_slots(slots, idx, kind, layer, prev, name):
    _, r, cdim = slots.shape

    def body(*refs):
        s_ref, o_ref = refs[1], refs[-1]
        v = s_ref[...].astype(F32)
        o_ref[...] = (v[0] + v[1]) + (v[2] + v[3])

    if kind == SMALL:
        grid = (1,)
        s_spec = pl.BlockSpec((N_CHIP, r, cdim), lambda i, idx_ref: (0, 0, 0))
        o_spec = pl.BlockSpec((r, cdim), lambda i, idx_ref: (idx_ref[0], 0))
        full = (2 * r, cdim)
    else:
        tr = 256 if r % 256 == 0 else 384
        nrb = r // tr
        grid = (nrb,)
        s_spec = pl.BlockSpec((N_CHIP, tr, cdim), lambda i, idx_ref: (0, i, 0))
        if COL_SHARDED[kind]:
            o_spec = pl.BlockSpec((None, tr, cdim), lambda i, idx_ref: (layer, idx_ref[0] * nrb + i, 0))
            full = (2, 2 * r, cdim)
        else:
            o_spec = pl.BlockSpec((None, tr, cdim), lambda i, idx_ref: (layer, i, idx_ref[0]))
            full = (2, r, 2 * cdim)
    in_specs, args, aliases = [s_spec], [idx, slots], {}
    if prev is not None:
        in_specs.append(ANY)
        args.append(prev)
        aliases = {2: 0}
    return pl.pallas_call(
        body, name=name,
        grid_spec=pltpu.PrefetchScalarGridSpec(
            num_scalar_prefetch=1, grid=grid, in_specs=in_specs, out_specs=o_spec),
        out_shape=jax.ShapeDtypeStruct(full, F32),
        input_output_aliases=aliases,
        compiler_params=_cp(*(["parallel"] * len(grid))),
    )(*args)


def _block_diag(w):
    w4 = w.reshape(2, 4, 4, 64, 64)
    eye = jnp.eye(4, dtype=w.dtype)[None, None, :, None, :, None]
    return (w4[:, :, :, :, None, :] * eye).reshape(2, 4, 256, 256)


def _block_diag_extract(d):
    d5 = d.reshape(4, 4, 64, 4, 64)
    return jnp.stack([d5[:, hh, :, hh, :] for hh in range(4)], axis=1).reshape(-1)


REP_NAMES = ("norm1_g", "lru_conv_b", "lru_ba", "lru_bx", "lru_lambda", "norm2_g", "lru_wa", "lru_wx")


def _pack_rep(norm1_g, conv_b, ba, bx, lam, norm2_g, wa, wx, final_g):
    parts = [a.reshape(-1) for a in (norm1_g, conv_b, ba, bx, lam, norm2_g, wa, wx, final_g)]
    return jnp.concatenate(parts).reshape(REP_ROWS, LANES)


def _unpack_rep(buf):
    flat = buf.reshape(-1)
    res, o = {}, 0
    for k in REP_NAMES:
        shape = (2, 16, 64, 64) if k in ("lru_wa", "lru_wx") else (2, 1024)
        n = math.prod(shape)
        res[k] = flat[o:o + n].reshape(shape)
        o += n
    res["final_g"] = flat[o:o + 1024]
    return res


def _pack_conv_shard(lru_cw, sc_cw, ffn_cw):
    return jnp.concatenate([lru_cw.reshape(16, LANES), jnp.pad(sc_cw.reshape(6, LANES), ((0, 2), (0, 0))),
                            ffn_cw.reshape(72, LANES)], axis=0)


def _unpack_conv_shard(buf):
    return (buf[0:16].reshape(2, 4, 256), buf[16:22].reshape(2, 3, 128), buf[24:96].reshape(2, 3, 1536))


def kernel(x, norm1_g, w_in, lru_conv_w, lru_conv_b, lru_wa, lru_ba, lru_wx, lru_bx, lru_lambda, sc_conv_w, w_out, norm2_g, w_up, ffn_conv_w, w_down, final_g, loss_target, m_norm1_g, m_w_in, m_lru_conv_w, m_lru_conv_b, m_lru_wa, m_lru_ba, m_lru_wx, m_lru_bx, m_lru_lambda, m_sc_conv_w, m_w_out, m_norm2_g, m_w_up, m_ffn_conv_w, m_w_down, m_final_g, v_norm1_g, v_w_in, v_lru_conv_w, v_lru_conv_b, v_lru_wa, v_lru_ba, v_lru_wx, v_lru_bx, v_lru_lambda, v_sc_conv_w, v_w_out, v_norm2_g, v_w_up, v_ffn_conv_w, v_w_down, v_final_g):
    me = 2 * lax.axis_index("x") + lax.axis_index("y")
    idx = jnp.stack([lax.axis_index("c"), me]).astype(jnp.int32)
    t_len = x.shape[1]

    s_conv = _pack_conv_shard(lru_conv_w, sc_conv_w, ffn_conv_w)
    conv_slots = lax.dynamic_update_slice(jnp.zeros((N_CHIP, CONV_PACK_ROWS, LANES), F32), s_conv[None], (me, 0, 0))
    wi = list(_cast_into_full(w_in, W_IN, idx, "cast_w_in"))
    wo = list(_cast_into_full(w_out, W_OUT, idx, "cast_w_out"))
    wu = list(_cast_into_full(w_up, W_UP, idx, "cast_w_up"))
    wd = list(_cast_into_full(w_down, W_DOWN, idx, "cast_w_down"))
    wi[0], convs = _comm_call(_gather_plan([wi[0], conv_slots], [W_IN, SMALL]), "ag_first")
    per_chip = [_unpack_conv_shard(convs[k]) for k in range(N_CHIP)]
    lru_cw = jnp.concatenate([p[0] for p in per_chip], axis=-1)
    sc_cw = jnp.concatenate([p[1] for p in per_chip], axis=-1)
    ffn_cw = jnp.concatenate([p[2] for p in per_chip], axis=-1)

    prm = jnp.concatenate(
        [lru_cw, jnp.stack([lru_conv_b, lru_ba, lru_bx, lru_lambda], axis=1),
         jnp.pad(sc_cw, ((0, 0), (0, 0), (0, D_LRU - D_SC))), jnp.zeros((2, 5, D_LRU), F32)], axis=1)
    gates = jnp.concatenate([_block_diag(lru_wa), _block_diag(lru_wx)], axis=1).astype(BF16)
    fw8 = jnp.pad(ffn_cw.reshape(2, 3, 2, D_FF).transpose(0, 2, 1, 3), ((0, 0), (0, 0), (0, 5), (0, 0)))

    xs = x[0]
    saved = []
    n512, n256 = t_len // 512, t_len // 256
    whole, lower, upper = (0, 1), (0, 2), (1, 2)
    carried_by = {
        "fwd_in_0": ([(wu, 0, W_UP, lower)], (max(n512 - 2, 0),)),
        "fwd_mixer_0": ([(wu, 0, W_UP, upper), (wo, 0, W_OUT, whole)], (max(n256 - 3, 0),)),
        "fwd_up_0": ([(wd, 0, W_DOWN, whole)], (max(n512 - 2, 0),)),
        "fwd_act_0": ([(wi, 1, W_IN, whole), (wo, 1, W_OUT, whole), (wu, 1, W_UP, (0, 4))], (n512 - 1, 0)),
        "fwd_down_0": ([(wu, 1, W_UP, (1, 4))], (max(n512 - 2, 0),)),
        "fwd_in_1": ([(wu, 1, W_UP, (2, 4))], (max(n512 - 2, 0),)),
        "fwd_mixer_1": ([(wu, 1, W_UP, (3, 4)), (wd, 1, W_DOWN, whole)], (max(n256 - 3, 0),)),
    }

    def carried(name):
        if name not in carried_by:
            return None, lambda got: None
        items, mid_at = carried_by[name]

        def store(got):
            for (lst, i, _, _), arr in zip(items, got):
                lst[i] = arr

        return _gather_plan([lst[i] for lst, i, _, _ in items], [k for _, _, k, _ in items], mid_at=mid_at,
                            parts=[p for _, _, _, p in items]), store

    for l in range(2):
        comm, store = carried(f"fwd_in_{l}")
        (z, h1), got = _norm_mm(xs, norm1_g[l][None], wi[l], f"fwd_in_{l}", comm=comm)
        store(got)
        comm, store = carried(f"fwd_mixer_{l}")
        (ymix, hst), got = _mixer_fwd(z, prm, gates, l, f"fwd_mixer_{l}", comm=comm)
        store(got)
        (x2,), _ = _mm_res(ymix, wo[l], xs, f"fwd_out_{l}")
        comm, store = carried(f"fwd_up_{l}")
        (u, h2), got = _norm_mm(x2, norm2_g[l][None], wu[l], f"fwd_up_{l}", planes=True, comm=comm)
        store(got)
        comm, store = carried(f"fwd_act_{l}")
        (act,), got = _ffn_act(u, fw8, l, f"fwd_act_{l}", comm=comm)
        store(got)
        comm, store = carried(f"fwd_down_{l}")
        (x3,), got = _mm_res(act, wd[l], x2, f"fwd_down_{l}", comm=comm)
        store(got)
        saved.append((xs, h1, z, hst, ymix, x2, h2, u, act))
        xs = x3

    dx, dxb, dgf, loss_blk = _loss_head(xs, final_g[None], loss_target[0], "loss_head")

    kinds = [W_IN, W_OUT, W_UP, W_DOWN]
    grads = [None, None]
    small = [None, None]
    reduced = [None] * 4
    summed1 = [None] * 4
    parts = slots = None
    for l in (1, 0):
        x_in, h1, z, hst, ymix, x2, h2, u, act = saved[l]
        carry = l == 0
        comm = _pair_plan([grads[1][W_IN]], [W_IN]) if carry else None
        (g_down,), got = _mm_tn(act, dxb, f"bwd_wdown_{l}", tk=1536, tn=1024, comm=comm)
        if carry:
            summed1[W_IN] = _pair_sum(grads[1][W_IN], got[0], idx, W_IN, "rs_add1_0")
            parts, slots = [s[0] for s in summed1], [s[1] for s in summed1]
        (dact,), _ = _mm_nt(dxb, wd[l], f"bwd_dact_{l}")
        comm = _scatter_plan(parts, slots, kinds) if carry else None
        (du, dfw), got = _ffn_bwd(dact, u, fw8, l, f"bwd_act_{l}", comm=comm)
        if carry:
            reduced = [_sum_slots(got[w], idx, kinds[w], 1, None, f"rs_sum1_{w}") for w in range(4)]
        comm = _share_plan(reduced, kinds, 1) if carry else None
        (g_up,), got = _mm_tn(h2, du, f"bwd_wup_{l}", tk=1024, tn=1536, planes=True, comm=comm)
        if carry:
            reduced = list(got)
        comm = _pair_plan([g_up, g_down], [W_UP, W_DOWN]) if carry else None
        (dx2, dx2b, dg2), got = _mm_nt_normbwd(du, wu[l], x2, norm2_g[l][None], dx, f"bwd_up_{l}", planes=True,
                                               comm=comm)
        if carry:
            sum_up = _pair_sum(g_up, got[0], idx, W_UP, "rs_add0_2")
            sum_down = _pair_sum(g_down, got[1], idx, W_DOWN, "rs_add0_3")
        (g_out,), _ = _mm_tn(ymix, dx2b, f"bwd_wout_{l}", tk=1536, tn=1024)
        comm = _pair_plan([g_out], [W_OUT]) if carry else None
        (dymix,), got = _mm_nt(dx2b, wo[l], f"bwd_dymix_{l}", comm=comm)
        trio = (W_OUT, W_UP, W_DOWN)
        if carry:
            sum_out = _pair_sum(g_out, got[0], idx, W_OUT, "rs_add0_1")
            comm = _scatter_plan([sum_out[0], sum_up[0], sum_down[0]], [sum_out[1], sum_up[1], sum_down[1]], trio)
        else:
            comm = _pair_plan([g_out, g_up, g_down], trio)
        (dz, dprm, dgates), got = _mixer_bwd(z, hst, dymix, prm, gates, l, f"bwd_mixer_{l}", comm=comm)
        if carry:
            for w, s in zip(trio, got):
                reduced[w] = _sum_slots(s, idx, w, 0, reduced[w], f"rs_sum0_{w}")
        else:
            for w, g, land in zip(trio, (g_out, g_up, g_down), got):
                summed1[w] = _pair_sum(g, land, idx, w, f"rs_add1_{w}")
        (g_in,), _ = _mm_tn(h1, dz, f"bwd_win_{l}", tk=1024, tn=1792)
        comm = _pair_plan([g_in], [W_IN]) if carry else None
        (dx, dxb, dg1), got = _mm_nt_normbwd(dz, wi[l], x_in, norm1_g[l][None], dx2, f"bwd_in_{l}", comm=comm)
        if carry:
            sum_in = _pair_sum(g_in, got[0], idx, W_IN, "rs_add0_0")
        grads[l] = [g_in, g_out, g_up, g_down]
        rep = dict(zip(REP_NAMES, [dg1[0], dprm[4], dprm[5], dprm[6], dprm[7], dg2[0],
                                   _block_diag_extract(dgates[0:4]), _block_diag_extract(dgates[4:8])]))
        conv = [dprm[0:4].reshape(-1), jnp.pad(dprm[8:11, 0:D_SC].reshape(-1), (0, 512)),
                dfw[:, 0:3, :].transpose(1, 0, 2).reshape(-1)]
        small[l] = (rep, conv)
    grad_x = dx[None]
    g_small = jnp.concatenate(
        [small[l][0][k] for k in REP_NAMES for l in range(2)] + [dgf[0]] + small[0][1] + small[1][1]
        + [loss_blk.reshape(-1)]).reshape(SMALL_ROWS, LANES)

    land_small, = _comm_call(_pair_plan([g_small], [SMALL]), "rs_pair_small")
    sum_small = _pair_sum(g_small, land_small, idx, SMALL, "rs_add0_4")
    slot_in, slot_small = _comm_call(
        _scatter_plan([sum_in[0], sum_small[0]], [sum_in[1], sum_small[1]], [W_IN, SMALL]), "rs_scatter_last")
    reduced[W_IN] = _sum_slots(slot_in, idx, W_IN, 0, reduced[W_IN], "rs_sum0_0")
    reduced.append(_sum_slots(slot_small, idx, SMALL, 0, None, "rs_sum0_4"))
    gw_in, gw_out, gw_up, gw_down, gs = _comm_call(_share_plan(reduced, kinds + [SMALL], 0), "rs_share0")

    loss = gs[REP_ROWS + CONV_ROWS, 0]
    g_rep = gs[0:REP_ROWS]
    g_conv = gs[REP_ROWS:REP_ROWS + CONV_ROWS].reshape(2, CONV_LAYER)
    g_lru_cw = lax.dynamic_slice_in_dim(g_conv[:, 0:4096].reshape(2, 4, 1024), me * 256, 256, axis=2)
    g_sc_cw = lax.dynamic_slice_in_dim(g_conv[:, 4096:4096 + 1536].reshape(2, 3, 512), me * 128, 128, axis=2)
    g_ffn_cw = lax.dynamic_slice_in_dim(g_conv[:, 6144:].reshape(2, 3, 6144), me * 1536, 1536, axis=2)

    def big(w, g, m, v, name):
        shape = w.shape
        two_d = lambda a: a.reshape(-1, shape[-1])
        return [o.reshape(shape) for o in _adamw(two_d(w), two_d(g), two_d(m), two_d(v), name, emit_grad=True)]

    upd = {"w_in": big(w_in, gw_in, m_w_in, v_w_in, "adamw_w_in"),
           "w_out": big(w_out, gw_out, m_w_out, v_w_out, "adamw_w_out"),
           "w_up": big(w_up, gw_up, m_w_up, v_w_up, "adamw_w_up"),
           "w_down": big(w_down, gw_down, m_w_down, v_w_down, "adamw_w_down")}
    rep_out = _adamw(
        _pack_rep(norm1_g, lru_conv_b, lru_ba, lru_bx, lru_lambda, norm2_g, lru_wa, lru_wx, final_g), g_rep,
        _pack_rep(m_norm1_g, m_lru_conv_b, m_lru_ba, m_lru_bx, m_lru_lambda, m_norm2_g, m_lru_wa, m_lru_wx, m_final_g),
        _pack_rep(v_norm1_g, v_lru_conv_b, v_lru_ba, v_lru_bx, v_lru_lambda, v_norm2_g, v_lru_wa, v_lru_wx, v_final_g),
        "adamw_rep")
    conv_out = _adamw(s_conv, _pack_conv_shard(g_lru_cw, g_sc_cw, g_ffn_cw),
                      _pack_conv_shard(m_lru_conv_w, m_sc_conv_w, m_ffn_conv_w),
                      _pack_conv_shard(v_lru_conv_w, v_sc_conv_w, v_ffn_conv_w), "adamw_conv")

    names = ["norm1_g", "w_in", "lru_conv_w", "lru_conv_b", "lru_wa", "lru_ba", "lru_wx", "lru_bx", "lru_lambda",
             "sc_conv_w", "w_out", "norm2_g", "w_up", "ffn_conv_w", "w_down", "final_g"]
    groups = []
    g_all = dict(_unpack_rep(g_rep))
    g_all.update({k: v[3] for k, v in upd.items()})
    g_all.update(lru_conv_w=g_lru_cw, sc_conv_w=g_sc_cw, ffn_conv_w=g_ffn_cw)
    groups.append(g_all)
    for i in range(3):
        d = dict(_unpack_rep(rep_out[i]))
        cl, cs, cf = _unpack_conv_shard(conv_out[i])
        d.update(lru_conv_w=cl, sc_conv_w=cs, ffn_conv_w=cf)
        d.update({k: v[i] for k, v in upd.items()})
        groups.append(d)
    return (loss, grad_x, *[grp[n] for grp in groups for n in names])
```

```python
import dataclasses
import functools
import math
import operator
from typing import Any, Callable, Optional, Sequence

import jax
import jax.numpy as jnp
from jax import lax
from jax.experimental import pallas as pl
from jax.experimental.pallas import tpu as pltpu

F32 = jnp.float32
BF16 = jnp.bfloat16
MESH = pl.DeviceIdType.MESH

D_MODEL = 1024
D_LRU = 1024
D_SC = 512
D_MIX = D_LRU + D_SC
D_IN = 2 * D_LRU + 3 * D_SC
D_FF = 3072
N_CHIP = 4
RG_C = 8.0
EPS = 1e-6
ADAM_LR = 0.001
ADAM_B1 = 0.9
ADAM_B2 = 0.999
ADAM_EPS = 1e-08
ADAM_WD = 0.01
ADAM_STEP = 10

SUBLANES = 8
PACKED = 16
LANES = 128
VMEM_LIMIT = 56 * 1024 * 1024
GELU_C0 = math.sqrt(2.0 / math.pi)
GELU_C1 = 0.044715

REP_LAYER = 6 * 1024 + 2 * 16 * 64 * 64
REP_ROWS = (2 * REP_LAYER + 1024) // LANES
CONV_LAYER = 4 * 1024 + 2048 + 3 * 6144
CONV_ROWS = 2 * CONV_LAYER // LANES
SMALL_ROWS = REP_ROWS + CONV_ROWS + 8
CONV_PACK_ROWS = 96

W_IN, W_OUT, W_UP, W_DOWN, SMALL = range(5)
COL_SHARDED = {W_IN: True, W_OUT: False, W_UP: True, W_DOWN: False}

ONCE = pl.Buffered(1)
ANY = pl.BlockSpec(memory_space=pl.ANY)


def _cp(*sem):
    return pltpu.CompilerParams(dimension_semantics=sem, vmem_limit_bytes=VMEM_LIMIT)


@dataclasses.dataclass
class Comm:
    srcs: Sequence[Any]
    bufs: Sequence[Any]
    outs: Sequence[Any]
    n_sem: int
    start: Callable
    finish: Callable
    mid: Optional[Callable] = None
    mid_at: Optional[Sequence[int]] = None


def _pallas(body, *, name, grid, in_specs, out_specs, out_shape, args, sem, scratch_shapes=(), comm=None):
    if comm is None:
        res = pl.pallas_call(
            body, name=name, grid=grid, in_specs=list(in_specs), out_specs=list(out_specs),
            out_shape=list(out_shape), scratch_shapes=list(scratch_shapes), compiler_params=_cp(*sem))(*args)
        return tuple(res), ()
    n_in, n_out, n_scr = len(in_specs), len(out_specs), len(scratch_shapes)
    ns, nb, no = len(comm.srcs), len(comm.bufs), len(comm.outs)

    def carrier(*refs):
        p = 0
        main_in = refs[p:p + n_in]
        p += n_in
        c_src = refs[p:p + ns]
        p += ns + nb
        main_out = refs[p:p + n_out]
        p += n_out
        c_buf = refs[p:p + nb]
        p += nb
        c_out = refs[p:p + no]
        p += no
        scr = refs[p:p + n_scr]
        send, recv = refs[p + n_scr], refs[p + n_scr + 1]
        ids = [pl.program_id(a) for a in range(len(grid))]

        def at(steps):
            return functools.reduce(operator.and_, [i == s for i, s in zip(ids, steps)])

        @pl.when(at([0] * len(grid)))
        def _():
            comm.start(c_src, c_buf, c_out, send, recv)

        if comm.mid is not None:
            @pl.when(at(comm.mid_at))
            def _():
                comm.mid(c_src, c_buf, c_out, send, recv)

        body(*main_in, *main_out, *scr)

        @pl.when(at([g - 1 for g in grid]))
        def _():
            comm.finish(c_src, c_buf, c_out, send, recv)

    res = pl.pallas_call(
        carrier, name=name, grid=grid,
        in_specs=list(in_specs) + [ANY] * (ns + nb),
        out_specs=list(out_specs) + [ANY] * (nb + no),
        out_shape=list(out_shape) + [jax.ShapeDtypeStruct(b.shape, b.dtype) for b in comm.bufs] + list(comm.outs),
        input_output_aliases={n_in + ns + j: n_out + j for j in range(nb)},
        scratch_shapes=list(scratch_shapes) + [pltpu.SemaphoreType.DMA((comm.n_sem,)),
                                               pltpu.SemaphoreType.DMA((comm.n_sem,))],
        compiler_params=_cp(*(["arbitrary"] * len(grid))),
    )(*args, *comm.srcs, *comm.bufs)
    return tuple(res[:n_out]), tuple(res[n_out:])


def _comm_call(comm, name):
    ns, nb, no = len(comm.srcs), len(comm.bufs), len(comm.outs)

    def body(*refs):
        c_src = refs[0:ns]
        c_buf = refs[ns + nb:ns + 2 * nb]
        c_out = refs[ns + 2 * nb:ns + 2 * nb + no]
        send, recv = refs[ns + 2 * nb + no], refs[ns + 2 * nb + no + 1]
        comm.start(c_src, c_buf, c_out, send, recv)
        if comm.mid is not None:
            comm.mid(c_src, c_buf, c_out, send, recv)
        comm.finish(c_src, c_buf, c_out, send, recv)

    return tuple(pl.pallas_call(
        body, name=name,
        in_specs=[ANY] * (ns + nb), out_specs=[ANY] * (nb + no),
        out_shape=[jax.ShapeDtypeStruct(b.shape, b.dtype) for b in comm.bufs] + list(comm.outs),
        input_output_aliases={ns + j: j for j in range(nb)},
        scratch_shapes=[pltpu.SemaphoreType.DMA((comm.n_sem,)), pltpu.SemaphoreType.DMA((comm.n_sem,))],
    )(*comm.srcs, *comm.bufs))


def _sigmoid(v):
    return 1.0 / (1.0 + jnp.exp(-v))


def _sigmoid_tanh(v):
    return 0.5 + 0.5 * jnp.tanh(0.5 * v)


def _gelu_parts(v):
    v2 = v * v
    t = jnp.tanh(GELU_C0 * v * (1.0 + GELU_C1 * v2))
    half = 0.5 * (1.0 + t)
    gel = v * half
    dgel = half + 0.5 * v * (1.0 - t * t) * (GELU_C0 * (1.0 + 3.0 * GELU_C1 * v2))
    return gel, dgel


def _gelu(v):
    t = jnp.tanh(GELU_C0 * v * (1.0 + GELU_C1 * (v * v)))
    return 0.5 * v * (1.0 + t)


def _neg_expm1(y, a):
    p = jnp.full_like(y, 1.0 / 120.0)
    for coef in (1.0 / 24.0, 1.0 / 6.0, 0.5, 1.0):
        p = p * y + coef
    return jnp.where(y > -0.1, -(p * y), 1.0 - a * a)


def _softplus_neg(lam):
    nl = -lam
    e = jnp.exp(-jnp.abs(nl))
    u = 1.0 + e
    l1p = jnp.where(u == 1.0, e, jnp.log(u) * e / (u - 1.0))
    return jnp.maximum(nl, 0.0) + l1p


def _conv_taps(ext, taps, n_out):
    kw = len(taps)
    acc = taps[kw - 1] * ext[SUBLANES:SUBLANES + n_out]
    for k in range(kw - 1):
        acc = acc + taps[k] * pltpu.roll(ext, kw - 1 - k, axis=0)[SUBLANES:SUBLANES + n_out]
    return acc


def _conv_taps_t(ext, taps, n_out):
    kw = len(taps)
    n = ext.shape[0]
    acc = taps[kw - 1] * ext[0:n_out]
    for k in range(kw - 1):
        acc = acc + taps[k] * pltpu.roll(ext, n - (kw - 1 - k), axis=0)[0:n_out]
    return acc


def _scan8(a, b, carry, row):
    for s in (1, 2, 4):
        m = row >= s
        a_sh = jnp.where(m, pltpu.roll(a, s, axis=0), 1.0)
        b_sh = jnp.where(m, pltpu.roll(b, s, axis=0), 0.0)
        b = a * b_sh + b
        a = a * a_sh
    return a * carry + b


def _scan8_rev(a, b, carry, row):
    for s in (1, 2, 4):
        m = row < SUBLANES - s
        a_sh = jnp.where(m, pltpu.roll(a, SUBLANES - s, axis=0), 1.0)
        b_sh = jnp.where(m, pltpu.roll(b, SUBLANES - s, axis=0), 0.0)
        b = a * b_sh + b
        a = a * a_sh
    return a * carry + b


def _cast_into_full(w, kind, idx, name):
    nl, r, c = w.shape
    tr = 256 if r % 256 == 0 else r
    nrb = r // tr

    def body(idx_ref, w_ref, o0_ref, o1_ref):
        o0_ref[...] = w_ref[0].astype(BF16)
        o1_ref[...] = w_ref[1].astype(BF16)

    if COL_SHARDED[kind]:
        full = (r, N_CHIP * c)
        o_spec = pl.BlockSpec((tr, c), lambda i, idx_ref: (i, idx_ref[1]))
    else:
        full = (N_CHIP * r, c)
        o_spec = pl.BlockSpec((tr, c), lambda i, idx_ref: (idx_ref[1] * nrb + i, 0))
    return pl.pallas_call(
        body, name=name,
        grid_spec=pltpu.PrefetchScalarGridSpec(
            num_scalar_prefetch=1, grid=(nrb,),
            in_specs=[pl.BlockSpec((nl, tr, c), lambda i, idx_ref: (0, i, 0))], out_specs=[o_spec, o_spec]),
        out_shape=[jax.ShapeDtypeStruct(full, BF16)] * 2,
        compiler_params=_cp("parallel"),
    )(idx, w)


def _norm_mm(x, g, w, name, planes=False, tm=512, tn=512, comm=None):
    t_len, d = x.shape
    n = w.shape[1]
    half = n // 2

    def body(x_ref, g_ref, w_ref, z_ref, h_ref):
        xv = x_ref[...]
        r = lax.rsqrt(jnp.mean(xv * xv, axis=-1, keepdims=True) + EPS)
        h_ref[...] = ((xv * r) * g_ref[...]).astype(BF16)
        for n0 in range(0, n, tn):
            blk = jnp.dot(h_ref[...], w_ref[:, n0:n0 + tn], preferred_element_type=F32).astype(BF16)
            if planes:
                z_ref[n0 // half, :, n0 % half:n0 % half + tn] = blk
            else:
                z_ref[:, n0:n0 + tn] = blk

    if planes:
        z_shape = jax.ShapeDtypeStruct((2, t_len, half), BF16)
        z_spec = pl.BlockSpec((2, tm, half), lambda i: (0, i, 0))
    else:
        z_shape = jax.ShapeDtypeStruct((t_len, n), BF16)
        z_spec = pl.BlockSpec((tm, n), lambda i: (i, 0))
    return _pallas(
        body, name=name, grid=(t_len // tm,),
        in_specs=[pl.BlockSpec((tm, d), lambda i: (i, 0)),
                  pl.BlockSpec((1, d), lambda i: (0, 0)),
                  pl.BlockSpec((d, n), lambda i: (0, 0), pipeline_mode=ONCE)],
        out_specs=[z_spec, pl.BlockSpec((tm, d), lambda i: (i, 0))],
        out_shape=[z_shape, jax.ShapeDtypeStruct((t_len, d), BF16)],
        args=(x, g, w), sem=("parallel",), comm=comm)


def _mm_res(a, w, res, name, tm=512, comm=None):
    t_len, k = a.shape
    n = w.shape[1]

    def body(a_ref, w_ref, r_ref, o_ref):
        o_ref[...] = r_ref[...] + jnp.dot(a_ref[...], w_ref[...], preferred_element_type=F32)

    return _pallas(
        body, name=name, grid=(t_len // tm,),
        in_specs=[pl.BlockSpec((tm, k), lambda i: (i, 0)),
                  pl.BlockSpec((k, n), lambda i: (0, 0), pipeline_mode=ONCE),
                  pl.BlockSpec((tm, n), lambda i: (i, 0))],
        out_specs=[pl.BlockSpec((tm, n), lambda i: (i, 0))],
        out_shape=[jax.ShapeDtypeStruct((t_len, n), F32)],
        args=(a, w, res), sem=("parallel",), comm=comm)


def _mm_nt(a, w, name, tm=512, comm=None):
    t_len, k = a.shape
    n = w.shape[0]

    def body(a_ref, w_ref, o_ref):
        o_ref[...] = lax.dot_general(a_ref[...], w_ref[...], (((1,), (1,)), ((), ())),
                                     preferred_element_type=F32).astype(BF16)

    return _pallas(
        body, name=name, grid=(t_len // tm,),
        in_specs=[pl.BlockSpec((tm, k), lambda i: (i, 0)),
                  pl.BlockSpec((n, k), lambda i: (0, 0), pipeline_mode=ONCE)],
        out_specs=[pl.BlockSpec((tm, n), lambda i: (i, 0))],
        out_shape=[jax.ShapeDtypeStruct((t_len, n), BF16)],
        args=(a, w), sem=("parallel",), comm=comm)


def _mm_nt_normbwd(dz, w, x, g, dres, name, planes=False, tm=512, comm=None):
    t_len, d = x.shape
    n = w.shape[1]
    half = n // 2
    nt_dims = (((1,), (1,)), ((), ()))

    def body(dz_ref, w_ref, x_ref, g_ref, r_ref, dx_ref, dxb_ref, dg_ref):
        @pl.when(pl.program_id(0) == 0)
        def _():
            dg_ref[...] = jnp.zeros_like(dg_ref)

        if planes:
            dh = (lax.dot_general(dz_ref[0], w_ref[:, 0:half], nt_dims, preferred_element_type=F32)
                  + lax.dot_general(dz_ref[1], w_ref[:, half:], nt_dims, preferred_element_type=F32))
        else:
            dh = lax.dot_general(dz_ref[...], w_ref[...], nt_dims, preferred_element_type=F32)
        xv = x_ref[...]
        r = lax.rsqrt(jnp.mean(xv * xv, axis=-1, keepdims=True) + EPS)
        xh = xv * r
        dhg = dh * g_ref[...]
        dx = r_ref[...] + r * (dhg - xh * jnp.mean(dhg * xh, axis=-1, keepdims=True))
        dx_ref[...] = dx
        dxb_ref[...] = dx.astype(BF16)
        dg_ref[0:1, :] += jnp.sum(dh * xh, axis=0, keepdims=True)

    if planes:
        dz_spec = pl.BlockSpec((2, tm, half), lambda i: (0, i, 0))
    else:
        dz_spec = pl.BlockSpec((tm, n), lambda i: (i, 0))
    return _pallas(
        body, name=name, grid=(t_len // tm,),
        in_specs=[dz_spec,
                  pl.BlockSpec((d, n), lambda i: (0, 0), pipeline_mode=ONCE),
                  pl.BlockSpec((tm, d), lambda i: (i, 0)),
                  pl.BlockSpec((1, d), lambda i: (0, 0)),
                  pl.BlockSpec((tm, d), lambda i: (i, 0))],
        out_specs=[pl.BlockSpec((tm, d), lambda i: (i, 0)),
                   pl.BlockSpec((tm, d), lambda i: (i, 0)),
                   pl.BlockSpec((SUBLANES, d), lambda i: (0, 0))],
        out_shape=[jax.ShapeDtypeStruct((t_len, d), F32),
                   jax.ShapeDtypeStruct((t_len, d), BF16),
                   jax.ShapeDtypeStruct((SUBLANES, d), F32)],
        args=(dz, w, x, g, dres), sem=("arbitrary",), comm=comm)


def _mm_tn(a, g, name, tk, tn, planes=False, tt=1024, comm=None):
    t_len, k = a.shape
    n = 2 * g.shape[2] if planes else g.shape[1]
    nn = n // tn
    half = nn // 2
    tt = min(tt, t_len)

    def body(a_ref, g_ref, o_ref):
        @pl.when(pl.program_id(2) == 0)
        def _():
            o_ref[...] = jnp.zeros_like(o_ref)

        o_ref[...] += lax.dot_general(a_ref[...], g_ref[...], (((0,), (0,)), ((), ())),
                                      preferred_element_type=F32)

    if planes:
        g_spec = pl.BlockSpec((None, tt, tn), lambda i, j, t: (j // half, t, j % half))
    else:
        g_spec = pl.BlockSpec((tt, tn), lambda i, j, t: (t, j))
    return _pallas(
        body, name=name, grid=(k // tk, nn, t_len // tt),
        in_specs=[pl.BlockSpec((tt, tk), lambda i, j, t: (t, i)), g_spec],
        out_specs=[pl.BlockSpec((tk, tn), lambda i, j, t: (i, j))],
        out_shape=[jax.ShapeDtypeStruct((k, n), F32)],
        args=(a, g), sem=("parallel", "parallel", "arbitrary"), comm=comm)


def _lru_gates(rp, ip, spn):
    r = _sigmoid(rp)
    i = _sigmoid_tanh(ip)
    la = r * spn
    a = jnp.exp(la)
    mult = jnp.sqrt(_neg_expm1(2.0 * la, a))
    return r, i, a, mult


def _mixer_fwd(z, prm, gates, layer, name, tb=256, comm=None):
    t_len = z.shape[0]

    def body(z_ref, p_ref, g_ref, y_ref, h_ref, xhalo, phalo, hcar, lx_s, rp_s, ip_s):
        @pl.when(pl.program_id(0) == 0)
        def _():
            xhalo[...] = jnp.zeros_like(xhalo)
            phalo[...] = jnp.zeros_like(phalo)
            hcar[...] = jnp.zeros_like(hcar)

        prm_v = p_ref[...]
        cw = prm_v[0:4]
        vec = prm_v[4:8]
        xp = z_ref[:, 0:D_LRU].astype(F32)
        ext = jnp.concatenate([xhalo[...], xp], axis=0)
        lx = vec[0:1] + _conv_taps(ext, [cw[k:k + 1] for k in range(4)], tb)
        xhalo[...] = xp[tb - SUBLANES:]
        lx_s[...] = lx
        lxb = lx.astype(BF16)
        for q in range(4):
            sl = slice(q * 256, (q + 1) * 256)
            rp_s[:, sl] = jnp.dot(lxb[:, sl], g_ref[q], preferred_element_type=F32) + vec[1:2, sl]
            ip_s[:, sl] = jnp.dot(lxb[:, sl], g_ref[4 + q], preferred_element_type=F32) + vec[2:3, sl]

        spn = jnp.broadcast_to(-RG_C * _softplus_neg(vec[3:4]), (SUBLANES, D_LRU))
        row = lax.broadcasted_iota(jnp.int32, (SUBLANES, D_LRU), 0)

        def step(ci, carry):
            o = pl.multiple_of(ci * PACKED, PACKED)
            gate = z_ref[pl.ds(o, PACKED), D_LRU:2 * D_LRU].astype(F32)
            ys = []
            for sub in range(2):
                rows = pl.ds(pl.multiple_of(o + sub * SUBLANES, SUBLANES), SUBLANES)
                lxv = lx_s[rows, :]
                _, i, a, mult = _lru_gates(rp_s[rows, :], ip_s[rows, :], spn)
                h = _scan8(a, mult * (i * lxv), carry, row)
                h_ref[rows, :] = h
                ys.append(h * _gelu(gate[sub * SUBLANES:(sub + 1) * SUBLANES]))
                carry = jnp.broadcast_to(h[SUBLANES - 1:SUBLANES, :], (SUBLANES, D_LRU))
            y_ref[pl.ds(o, PACKED), 0:D_LRU] = jnp.concatenate(ys, axis=0).astype(BF16)
            return carry

        hcar[...] = lax.fori_loop(0, tb // PACKED, step, hcar[...])

        scw = prm_v[8:11, 0:D_SC]
        o_b, o_c, o_x = 2 * D_LRU, 2 * D_LRU + D_SC, 2 * D_LRU + 2 * D_SC
        p = z_ref[:, o_c:o_x].astype(F32) * z_ref[:, o_x:].astype(F32)
        pext = jnp.concatenate([phalo[...], p], axis=0)
        q = _conv_taps(pext, [scw[k:k + 1] for k in range(3)], tb)
        phalo[...] = p[tb - SUBLANES:]
        y_ref[:, D_LRU:] = (z_ref[:, o_b:o_c].astype(F32) * q).astype(BF16)

    return _pallas(
        body, name=name, grid=(t_len // tb,),
        in_specs=[pl.BlockSpec((tb, D_IN), lambda t: (t, 0)),
                  pl.BlockSpec((None, 2 * SUBLANES, D_LRU), lambda t: (layer, 0, 0)),
                  pl.BlockSpec((None, 8, 256, 256), lambda t: (layer, 0, 0, 0))],
        out_specs=[pl.BlockSpec((tb, D_MIX), lambda t: (t, 0)),
                   pl.BlockSpec((tb, D_LRU), lambda t: (t, 0))],
        out_shape=[jax.ShapeDtypeStruct((t_len, D_MIX), BF16),
                   jax.ShapeDtypeStruct((t_len, D_LRU), F32)],
        scratch_shapes=[pltpu.VMEM((SUBLANES, D_LRU), F32), pltpu.VMEM((SUBLANES, D_SC), F32),
                        pltpu.VMEM((SUBLANES, D_LRU), F32), pltpu.VMEM((tb, D_LRU), F32),
                        pltpu.VMEM((tb, D_LRU), F32), pltpu.VMEM((tb, D_LRU), F32)],
        args=(z, prm, gates), sem=("arbitrary",), comm=comm)


def _mixer_bwd(z, h, dy, prm, gates, layer, name, tb=256, comm=None):
    t_len = z.shape[0]
    nb = t_len // tb

    def body(z_ref, zh_ref, h_ref, hh_ref, dy_ref, p_ref, g_ref, dz_ref, dp_ref, dg_ref,
             lx_s, rp_s, ip_s, drpb_s, dipb_s, dlx_s, hext_s, acc_s, acar, gcar, dqh):
        t = pl.program_id(0)
        first_block = t == nb - 1

        @pl.when(t == 0)
        def _():
            for ref in (dp_ref, dg_ref, acc_s, acar, gcar, dqh):
                ref[...] = jnp.zeros_like(ref)
            dlx_s[tb:, :] = jnp.zeros((SUBLANES, D_LRU), F32)

        prm_v = p_ref[...]
        cw = prm_v[0:4]
        vec = prm_v[4:8]
        scw = prm_v[8:11, 0:D_SC]
        wa_ref = [g_ref.at[q] for q in range(4)]
        wx_ref = [g_ref.at[4 + q] for q in range(4)]
        dwa_ref = [dg_ref.at[q] for q in range(4)]
        dwx_ref = [dg_ref.at[4 + q] for q in range(4)]
        ctaps = [cw[k:k + 1] for k in range(4)]
        staps = [scw[k:k + 1] for k in range(3)]
        keep = jnp.where(first_block, 0.0, 1.0)
        zh = zh_ref[...].astype(F32)[PACKED - SUBLANES:] * keep

        xp = z_ref[:, 0:D_LRU].astype(F32)
        xext = jnp.concatenate([zh[:, 0:D_LRU], xp], axis=0)
        lx = vec[0:1] + _conv_taps(xext, ctaps, tb)
        lx_s[...] = lx
        lxb = lx.astype(BF16)
        for q in range(4):
            sl = slice(q * 256, (q + 1) * 256)
            rp_s[:, sl] = jnp.dot(lxb[:, sl], wa_ref[q][...], preferred_element_type=F32) + vec[1:2, sl]
            ip_s[:, sl] = jnp.dot(lxb[:, sl], wx_ref[q][...], preferred_element_type=F32) + vec[2:3, sl]
        hext_s[0:SUBLANES, :] = hh_ref[...] * keep
        hext_s[SUBLANES:, :] = h_ref[...]

        spn = jnp.broadcast_to(-RG_C * _softplus_neg(vec[3:4]), (SUBLANES, D_LRU))
        row = lax.broadcasted_iota(jnp.int32, (SUBLANES, D_LRU), 0)

        def step(ci, carry):
            a_next, g_next = carry
            o = pl.multiple_of((tb // PACKED - 1 - ci) * PACKED, PACKED)
            rows16 = pl.ds(o, PACKED)
            gate16 = z_ref[rows16, D_LRU:2 * D_LRU].astype(F32)
            dyl16 = dy_ref[rows16, 0:D_LRU].astype(F32)
            dgs, drs, dis = [None, None], [None, None], [None, None]
            for sub in (1, 0):
                oo = pl.multiple_of(o + sub * SUBLANES, SUBLANES)
                rows = pl.ds(oo, SUBLANES)
                half = slice(sub * SUBLANES, (sub + 1) * SUBLANES)
                lxv = lx_s[rows, :]
                r, i, a, mult = _lru_gates(rp_s[rows, :], ip_s[rows, :], spn)
                hwin = hext_s[pl.ds(oo, 2 * SUBLANES), :]
                hv = hwin[SUBLANES:]
                hprev = pltpu.roll(hwin, 1, axis=0)[SUBLANES:]
                gel, dgel = _gelu_parts(gate16[half])
                dyl = dyl16[half]
                a_up = jnp.where(row < SUBLANES - 1, pltpu.roll(a, SUBLANES - 1, axis=0), a_next)
                gg = _scan8_rev(a_up, dyl * gel, g_next, row)
                dgs[sub] = dyl * hv * dgel
                ilx = i * lxv
                dla = gg * hprev * a - (gg * ilx) * (a * a) / mult
                dlx_s[rows, :] = gg * mult * i
                drp = dla * spn * r * (1.0 - r)
                dip = gg * mult * lxv * i * (1.0 - i)
                drs[sub] = drp
                dis[sub] = dip
                acc_s[0] += drp
                acc_s[1] += dip
                acc_s[2] += dla * r
                a_next = jnp.broadcast_to(a[0:1, :], (SUBLANES, D_LRU))
                g_next = jnp.broadcast_to(gg[0:1, :], (SUBLANES, D_LRU))
            dz_ref[rows16, D_LRU:2 * D_LRU] = jnp.concatenate(dgs, axis=0).astype(BF16)
            drpb_s[rows16, :] = jnp.concatenate(drs, axis=0).astype(BF16)
            dipb_s[rows16, :] = jnp.concatenate(dis, axis=0).astype(BF16)
            return a_next, g_next

        a_c, g_c = lax.fori_loop(0, tb // PACKED, step, (acar[...], gcar[...]))
        acar[...] = a_c
        gcar[...] = g_c

        drpb = drpb_s[...]
        dipb = dipb_s[...]
        nt_dims = (((1,), (1,)), ((), ()))
        tn_dims = (((0,), (0,)), ((), ()))
        for q in range(4):
            sl = slice(q * 256, (q + 1) * 256)
            dlx_s[0:tb, sl] += (
                lax.dot_general(drpb[:, sl], wa_ref[q][...], nt_dims, preferred_element_type=F32)
                + lax.dot_general(dipb[:, sl], wx_ref[q][...], nt_dims, preferred_element_type=F32))
            dwa_ref[q][...] += lax.dot_general(lxb[:, sl], drpb[:, sl], tn_dims, preferred_element_type=F32)
            dwx_ref[q][...] += lax.dot_general(lxb[:, sl], dipb[:, sl], tn_dims, preferred_element_type=F32)

        dlx_ext = dlx_s[...]
        dlx = dlx_ext[0:tb]
        dz_ref[:, 0:D_LRU] = _conv_taps_t(dlx_ext, ctaps, tb).astype(BF16)
        dp_ref[3:4, :] += jnp.sum(dlx * xp, axis=0, keepdims=True)
        for k in range(3):
            shifted = pltpu.roll(xext, 3 - k, axis=0)[SUBLANES:]
            dp_ref[k:k + 1, :] += jnp.sum(dlx * shifted, axis=0, keepdims=True)
        dp_ref[4:5, :] += jnp.sum(dlx, axis=0, keepdims=True)
        dlx_s[tb:, :] = dlx[0:SUBLANES]

        o_b, o_c, o_x = 2 * D_LRU, 2 * D_LRU + D_SC, 2 * D_LRU + 2 * D_SC
        sb = z_ref[:, o_b:o_c].astype(F32)
        scc = z_ref[:, o_c:o_x].astype(F32)
        sx = z_ref[:, o_x:].astype(F32)
        p = scc * sx
        pext = jnp.concatenate([zh[:, o_c:o_x] * zh[:, o_x:], p], axis=0)
        q = _conv_taps(pext, staps, tb)
        dys = dy_ref[:, D_LRU:].astype(F32)
        dq = dys * sb
        dp = _conv_taps_t(jnp.concatenate([dq, dqh[...]], axis=0), staps, tb)
        dp_ref[10:11, 0:D_SC] += jnp.sum(dq * p, axis=0, keepdims=True)
        for k in range(2):
            shifted = pltpu.roll(pext, 2 - k, axis=0)[SUBLANES:]
            dp_ref[8 + k:9 + k, 0:D_SC] += jnp.sum(dq * shifted, axis=0, keepdims=True)
        dqh[...] = dq[0:SUBLANES]
        dz_ref[:, o_b:o_c] = (dys * q).astype(BF16)
        dz_ref[:, o_c:o_x] = (dp * sx).astype(BF16)
        dz_ref[:, o_x:] = (dp * scc).astype(BF16)

        @pl.when(first_block)
        def _():
            dp_ref[5:6, :] = jnp.sum(acc_s[0], axis=0, keepdims=True)
            dp_ref[6:7, :] = jnp.sum(acc_s[1], axis=0, keepdims=True)
            dp_ref[7:8, :] = (jnp.sum(acc_s[2], axis=0, keepdims=True) * RG_C * _sigmoid(-vec[3:4]))

    blk = lambda t: (nb - 1 - t, 0)
    halo8 = lambda t: (jnp.maximum((nb - 1 - t) * (tb // SUBLANES) - 1, 0), 0)
    halo16 = lambda t: (jnp.maximum((nb - 1 - t) * (tb // PACKED) - 1, 0), 0)
    return _pallas(
        body, name=name, grid=(nb,),
        in_specs=[pl.BlockSpec((tb, D_IN), blk), pl.BlockSpec((PACKED, D_IN), halo16),
                  pl.BlockSpec((tb, D_LRU), blk), pl.BlockSpec((SUBLANES, D_LRU), halo8),
                  pl.BlockSpec((tb, D_MIX), blk),
                  pl.BlockSpec((None, 2 * SUBLANES, D_LRU), lambda t: (layer, 0, 0)),
                  pl.BlockSpec((None, 8, 256, 256), lambda t: (layer, 0, 0, 0))],
        out_specs=[pl.BlockSpec((tb, D_IN), blk),
                   pl.BlockSpec((2 * SUBLANES, D_LRU), lambda t: (0, 0)),
                   pl.BlockSpec((8, 256, 256), lambda t: (0, 0, 0))],
        out_shape=[jax.ShapeDtypeStruct((t_len, D_IN), BF16),
                   jax.ShapeDtypeStruct((2 * SUBLANES, D_LRU), F32),
                   jax.ShapeDtypeStruct((8, 256, 256), F32)],
        scratch_shapes=[pltpu.VMEM((tb, D_LRU), F32),
                        pltpu.VMEM((tb, D_LRU), F32), pltpu.VMEM((tb, D_LRU), F32),
                        pltpu.VMEM((tb, D_LRU), BF16), pltpu.VMEM((tb, D_LRU), BF16),
                        pltpu.VMEM((tb + SUBLANES, D_LRU), F32), pltpu.VMEM((tb + SUBLANES, D_LRU), F32),
                        pltpu.VMEM((3, SUBLANES, D_LRU), F32),
                        pltpu.VMEM((SUBLANES, D_LRU), F32), pltpu.VMEM((SUBLANES, D_LRU), F32),
                        pltpu.VMEM((SUBLANES, D_SC), F32)],
        args=(z, z, h, h, dy, prm, gates), sem=("arbitrary",), comm=comm)


def _ffn_act(u, fw, layer, name, tb=512, tn=1024, rc=64, comm=None):
    t_len = u.shape[1]
    hb = tb // PACKED

    def body(u_ref, uh_ref, fw_ref, o_ref, ext):
        keep = jnp.where(pl.program_id(0) == 0, 0.0, 1.0)
        ext[:, 0:SUBLANES, :] = uh_ref[...].astype(F32)[:, PACKED - SUBLANES:, :] * keep
        ext[:, SUBLANES:, :] = u_ref[...].astype(F32)
        fw_v = fw_ref[...]

        for lb in range(tn // LANES):
            lanes = slice(lb * LANES, (lb + 1) * LANES)
            wg = [fw_v[0, k:k + 1, lanes] for k in range(3)]
            wu = [fw_v[1, k:k + 1, lanes] for k in range(3)]

            def chunk(ci, c, lanes=lanes, wg=wg, wu=wu):
                o = pl.multiple_of(ci * rc, rc)
                win = pl.ds(o, rc + SUBLANES)
                gate = _conv_taps(ext[0, win, lanes], wg, rc)
                up = _conv_taps(ext[1, win, lanes], wu, rc)
                o_ref[pl.ds(o, rc), lanes] = (_gelu(gate) * up).astype(BF16)
                return c

            lax.fori_loop(0, tb // rc, chunk, 0)

    return _pallas(
        body, name=name, grid=(t_len // tb, D_FF // tn),
        in_specs=[pl.BlockSpec((2, tb, tn), lambda i, j: (0, i, j)),
                  pl.BlockSpec((2, PACKED, tn), lambda i, j: (0, jnp.maximum(i * hb - 1, 0), j)),
                  pl.BlockSpec((None, 2, SUBLANES, tn), lambda i, j: (layer, 0, 0, j))],
        out_specs=[pl.BlockSpec((tb, tn), lambda i, j: (i, j))],
        out_shape=[jax.ShapeDtypeStruct((t_len, D_FF), BF16)],
        scratch_shapes=[pltpu.VMEM((2, tb + SUBLANES, tn), F32)],
        args=(u, u, fw), sem=("parallel", "parallel"), comm=comm)


def _ffn_bwd(dact, u, fw, layer, name, tb=512, tn=1024, rc=32, comm=None):
    t_len = u.shape[1]
    ni = t_len // tb
    hb = tb // PACKED
    last_halo = t_len // PACKED - 1

    def body(d_ref, dn_ref, u_ref, up_ref, un_ref, fw_ref, du_ref, dfw_ref, extu, extd, acc):
        i = pl.program_id(1)

        @pl.when(i == 0)
        def _():
            acc[...] = jnp.zeros_like(acc)

        keep_prev = jnp.where(i == 0, 0.0, 1.0)
        keep_next = jnp.where(i == ni - 1, 0.0, 1.0)
        extu[:, 0:SUBLANES, :] = up_ref[...].astype(F32)[:, PACKED - SUBLANES:, :] * keep_prev
        extu[:, SUBLANES:SUBLANES + tb, :] = u_ref[...].astype(F32)
        extu[:, SUBLANES + tb:, :] = un_ref[...].astype(F32)[:, 0:SUBLANES, :]
        extd[0:tb, :] = d_ref[...].astype(F32)
        extd[tb:, :] = dn_ref[...].astype(F32)[0:SUBLANES] * keep_next
        fw_v = fw_ref[...]
        m = rc + SUBLANES

        for lb in range(tn // LANES):
            lanes = slice(lb * LANES, (lb + 1) * LANES)
            taps = [[fw_v[pln, k:k + 1, lanes] for k in range(3)] for pln in range(2)]

            def chunk(ci, c, lanes=lanes, taps=taps):
                o = pl.multiple_of(ci * rc, rc)
                win = pl.ds(o, rc + 2 * SUBLANES)
                sh = []
                for pln in range(2):
                    e = extu[pln, win, lanes]
                    sh.append([pltpu.roll(e, 2, axis=0)[SUBLANES:], pltpu.roll(e, 1, axis=0)[SUBLANES:],
                               e[SUBLANES:]])
                gate = sum(taps[0][k] * sh[0][k] for k in range(3))
                up = sum(taps[1][k] * sh[1][k] for k in range(3))
                dv = extd[pl.ds(o, m), lanes]
                gel, dgel = _gelu_parts(gate)
                dpost = [dv * up * dgel, dv * gel]
                for pln in range(2):
                    du_ref[pln, pl.ds(o, rc), lanes] = _conv_taps_t(dpost[pln], taps[pln], rc).astype(BF16)
                    for k in range(3):
                        prod = dpost[pln][0:rc] * sh[pln][k][0:rc]
                        acc[3 * pln + k, :, lanes] += sum(
                            prod[s:s + SUBLANES] for s in range(0, rc, SUBLANES))
                return c

            lax.fori_loop(0, tb // rc, chunk, 0)

        @pl.when(i == ni - 1)
        def _():
            dfw_ref[...] = jnp.zeros_like(dfw_ref)
            for pln in range(2):
                for k in range(3):
                    dfw_ref[pln, k:k + 1, :] = jnp.sum(acc[3 * pln + k], axis=0, keepdims=True)

    return _pallas(
        body, name=name, grid=(D_FF // tn, ni),
        in_specs=[pl.BlockSpec((tb, tn), lambda j, i: (i, j)),
                  pl.BlockSpec((PACKED, tn), lambda j, i: (jnp.minimum((i + 1) * hb, last_halo), j)),
                  pl.BlockSpec((2, tb, tn), lambda j, i: (0, i, j)),
                  pl.BlockSpec((2, PACKED, tn), lambda j, i: (0, jnp.maximum(i * hb - 1, 0), j)),
                  pl.BlockSpec((2, PACKED, tn), lambda j, i: (0, jnp.minimum((i + 1) * hb, last_halo), j)),
                  pl.BlockSpec((None, 2, SUBLANES, tn), lambda j, i: (layer, 0, 0, j))],
        out_specs=[pl.BlockSpec((2, tb, tn), lambda j, i: (0, i, j)),
                   pl.BlockSpec((2, SUBLANES, tn), lambda j, i: (0, 0, j))],
        out_shape=[jax.ShapeDtypeStruct((2, t_len, D_FF), BF16),
                   jax.ShapeDtypeStruct((2, SUBLANES, D_FF), F32)],
        scratch_shapes=[pltpu.VMEM((2, tb + 2 * SUBLANES, tn), F32),
                        pltpu.VMEM((tb + SUBLANES, tn), F32),
                        pltpu.VMEM((6, SUBLANES, tn), F32)],
        args=(dact, dact, u, u, u, fw), sem=("parallel", "arbitrary"), comm=comm)


def _loss_head(x, g, target, name, tb=256):
    t_len, d = x.shape

    def body(x_ref, g_ref, t_ref, dx_ref, dxb_ref, dg_ref, loss_ref):
        @pl.when(pl.program_id(0) == 0)
        def _():
            dg_ref[...] = jnp.zeros_like(dg_ref)
            loss_ref[...] = jnp.zeros_like(loss_ref)

        xv = x_ref[...]
        gv = g_ref[...]
        r = lax.rsqrt(jnp.mean(xv * xv, axis=-1, keepdims=True) + EPS)
        xh = xv * r
        err = xh * gv - t_ref[...]
        loss_ref[...] += (0.5 / d) * jnp.sum(jnp.sum(err * err, axis=-1, keepdims=True), axis=0, keepdims=True)
        dy = err * (1.0 / d)
        dyg = dy * gv
        dx = r * (dyg - xh * jnp.mean(dyg * xh, axis=-1, keepdims=True))
        dx_ref[...] = dx
        dxb_ref[...] = dx.astype(BF16)
        dg_ref[0:1, :] += jnp.sum(dy * xh, axis=0, keepdims=True)

    return _pallas(
        body, name=name, grid=(t_len // tb,),
        in_specs=[pl.BlockSpec((tb, d), lambda i: (i, 0)), pl.BlockSpec((1, d), lambda i: (0, 0)),
                  pl.BlockSpec((tb, d), lambda i: (i, 0))],
        out_specs=[pl.BlockSpec((tb, d), lambda i: (i, 0)), pl.BlockSpec((tb, d), lambda i: (i, 0)),
                   pl.BlockSpec((SUBLANES, d), lambda i: (0, 0)),
                   pl.BlockSpec((SUBLANES, LANES), lambda i: (0, 0))],
        out_shape=[jax.ShapeDtypeStruct((t_len, d), F32), jax.ShapeDtypeStruct((t_len, d), BF16),
                   jax.ShapeDtypeStruct((SUBLANES, d), F32), jax.ShapeDtypeStruct((SUBLANES, LANES), F32)],
        args=(x, g, target), sem=("arbitrary",))[0]


def _adamw(w, g, m, v, name, emit_grad=False, comm=None):
    r, c = w.shape
    tr = 256 if r % 256 == 0 else r
    c1 = 1.0 / (1.0 - ADAM_B1 ** ADAM_STEP)
    c2 = 1.0 / (1.0 - ADAM_B2 ** ADAM_STEP)

    def body(w_ref, g_ref, m_ref, v_ref, d_ref, mo_ref, vo_ref, *go_ref):
        gv = g_ref[...]
        mn = ADAM_B1 * m_ref[...] + (1.0 - ADAM_B1) * gv
        vn = ADAM_B2 * v_ref[...] + (1.0 - ADAM_B2) * (gv * gv)
        d_ref[...] = -ADAM_LR * ((mn * c1) / (jnp.sqrt(vn * c2) + ADAM_EPS) + ADAM_WD * w_ref[...])
        mo_ref[...] = mn
        vo_ref[...] = vn
        if emit_grad:
            go_ref[0][...] = gv

    spec = pl.BlockSpec((tr, c), lambda i: (i, 0))
    shape = jax.ShapeDtypeStruct((r, c), F32)
    n_out = 4 if emit_grad else 3
    return _pallas(
        body, name=name, grid=(r // tr,),
        in_specs=[spec] * 4, out_specs=[spec] * n_out, out_shape=[shape] * n_out,
        args=(w, g, m, v), sem=("parallel",), comm=comm)


def _place():
    x, y, c = lax.axis_index("x"), lax.axis_index("y"), lax.axis_index("c")
    chips = [(1 - x, y), (x, 1 - y), (1 - x, 1 - y)]
    return x, y, c, chips


def _remote(src, dst, send, recv, sem, to):
    return pltpu.make_async_remote_copy(
        src_ref=src, dst_ref=dst, send_sem=send.at[sem], recv_sem=recv.at[sem], device_id=to, device_id_type=MESH)


def _gather_plan(fulls, kinds, mid_at=None, parts=None):
    parts = parts or [(0, 1)] * len(fulls)

    def region(it, f, k, cc):
        kind = kinds[it]
        p, n = parts[it][0:2]
        count = parts[it][2] if len(parts[it]) > 2 else 1
        if kind == SMALL:
            return f.at[k, pl.ds(cc * (CONV_PACK_ROWS // 2), CONV_PACK_ROWS // 2), :]
        if COL_SHARDED[kind]:
            rows, cols = f.shape[0] // (2 * n), f.shape[1] // N_CHIP
            return f.at[pl.ds((cc * n + p) * rows, count * rows), pl.ds(k * cols, cols)]
        assert n == 1
        rows = f.shape[0] // N_CHIP
        return f.at[pl.ds(k * rows + cc * (rows // 2), rows // 2), :]

    def first_hop(bufs, send, recv, it, j):
        x, y, c, chips = _place()
        reg = region(it, bufs[it], 2 * x + y, c)
        return _remote(reg, reg, send, recv, it * 6 + j, (*chips[j], c))

    def arrival(bufs, send, recv, it, j, second):
        x, y, c, chips = _place()
        px, py = chips[j]
        reg = region(it, bufs[it], 2 * px + py, 1 - c if second else c)
        to = (x, y, 1 - c) if second else (px, py, c)
        return _remote(reg, reg, send, recv, it * 6 + (3 + j if second else j), to)

    def forward(bufs, send, recv, it, j):
        x, y, c, chips = _place()
        px, py = chips[j]
        reg = region(it, bufs[it], 2 * px + py, c)
        return _remote(reg, reg, send, recv, it * 6 + 3 + j, (x, y, 1 - c))

    def start(srcs, bufs, outs, send, recv):
        for it in range(len(bufs)):
            for j in range(3):
                first_hop(bufs, send, recv, it, j).start()

    def mid(srcs, bufs, outs, send, recv):
        for it in range(len(bufs)):
            for j in range(3):
                arrival(bufs, send, recv, it, j, False).wait_recv()
                forward(bufs, send, recv, it, j).start()

    def finish(srcs, bufs, outs, send, recv):
        for it in range(len(bufs)):
            for j in range(3):
                arrival(bufs, send, recv, it, j, True).wait_recv()
        for it in range(len(bufs)):
            for j in range(3):
                first_hop(bufs, send, recv, it, j).wait_send()
                forward(bufs, send, recv, it, j).wait_send()

    return Comm(srcs=(), bufs=tuple(fulls), outs=(), n_sem=6 * len(fulls), start=start, mid=mid, finish=finish,
                mid_at=mid_at)


def _half_axis(kind):
    return 0 if kind == SMALL or COL_SHARDED[kind] else 1


def _half2(ref, kind, cc):
    if _half_axis(kind) == 0:
        return ref.at[pl.ds(cc * (ref.shape[0] // 2), ref.shape[0] // 2), :]
    return ref.at[:, pl.ds(cc * (ref.shape[1] // 2), ref.shape[1] // 2)]


def _pair_plan(grads, kinds):
    def land_shape(g, kind):
        s = list(g.shape)
        s[_half_axis(kind)] //= 2
        return jax.ShapeDtypeStruct(tuple(s), F32)

    def copy(srcs, outs, send, recv, it):
        x, y, c, _ = _place()
        return _remote(_half2(srcs[it], kinds[it], 1 - c), outs[it], send, recv, it, (x, y, 1 - c))

    def start(srcs, bufs, outs, send, recv):
        for it in range(len(srcs)):
            copy(srcs, outs, send, recv, it).start()

    def finish(srcs, bufs, outs, send, recv):
        for it in range(len(srcs)):
            copy(srcs, outs, send, recv, it).wait_send()
        for it in range(len(srcs)):
            copy(srcs, outs, send, recv, it).wait_recv()

    return Comm(srcs=tuple(grads), bufs=(), outs=tuple(land_shape(g, k) for g, k in zip(grads, kinds)),
                n_sem=len(grads), start=start, finish=finish)


def _scatter_plan(parts, slots, kinds):
    def piece(s, kind, k):
        if kind == SMALL:
            return s
        if COL_SHARDED[kind]:
            n = s.shape[1] // N_CHIP
            return s.at[:, pl.ds(k * n, n)]
        n = s.shape[0] // N_CHIP
        return s.at[pl.ds(k * n, n), :]

    def outbound(srcs, bufs, send, recv, it, j):
        x, y, c, chips = _place()
        px, py = chips[j]
        return _remote(piece(srcs[it], kinds[it], 2 * px + py), bufs[it].at[2 * x + y], send, recv, it * 3 + j,
                       (px, py, c))

    def inbound(bufs, send, recv, it, j):
        x, y, c, chips = _place()
        px, py = chips[j]
        got = bufs[it].at[2 * px + py]
        return _remote(got, got, send, recv, it * 3 + j, (px, py, c))

    def start(srcs, bufs, outs, send, recv):
        for it in range(len(srcs)):
            for j in range(3):
                outbound(srcs, bufs, send, recv, it, j).start()

    def finish(srcs, bufs, outs, send, recv):
        for it in range(len(srcs)):
            for j in range(3):
                inbound(bufs, send, recv, it, j).wait_recv()
        for it in range(len(srcs)):
            for j in range(3):
                outbound(srcs, bufs, send, recv, it, j).wait_send()

    return Comm(srcs=tuple(parts), bufs=tuple(slots), outs=(), n_sem=3 * len(parts), start=start, finish=finish)


def _share_plan(fulls, kinds, layer):
    def half(f, kind, cc):
        return _half2(f if kind == SMALL else f.at[layer], kind, cc)

    def copy(bufs, send, recv, it, cc):
        x, y, c, _ = _place()
        reg = half(bufs[it], kinds[it], c if cc == "mine" else 1 - c)
        return _remote(reg, reg, send, recv, it, (x, y, 1 - c))

    def start(srcs, bufs, outs, send, recv):
        for it in range(len(bufs)):
            copy(bufs, send, recv, it, "mine").start()

    def finish(srcs, bufs, outs, send, recv):
        for it in range(len(bufs)):
            copy(bufs, send, recv, it, "other").wait_recv()
        for it in range(len(bufs)):
            copy(bufs, send, recv, it, "mine").wait_send()

    return Comm(srcs=(), bufs=tuple(fulls), outs=(), n_sem=len(fulls), start=start, finish=finish)


def _pair_sum(g, land, idx, kind, name):
    odt = F32 if kind == SMALL else BF16
    r, cdim = land.shape

    def body(idx_ref, g_ref, l_ref, p_ref, s_ref):
        v = (g_ref[...] + l_ref[...]).astype(odt)
        p_ref[...] = v
        if kind == SMALL:
            s_ref[...] = v
        else:
            @pl.when(pl.program_id(1 if COL_SHARDED[kind] else 0) == idx_ref[1])
            def _():
                s_ref[...] = v

    if kind == SMALL:
        grid = (1,)
        g_spec = pl.BlockSpec((r, LANES), lambda i, idx_ref: (idx_ref[0], 0))
        spec = pl.BlockSpec((r, LANES), lambda i, idx_ref: (0, 0))
        s_spec = pl.BlockSpec((None, r, LANES), lambda i, idx_ref: (idx_ref[1], 0, 0))
        s_shape = (N_CHIP, r, LANES)
    elif COL_SHARDED[kind]:
        pc, tr = cdim // N_CHIP, 256
        nrb = r // tr
        grid = (nrb, N_CHIP)
        g_spec = pl.BlockSpec((tr, pc), lambda i, k, idx_ref: (idx_ref[0] * nrb + i, k))
        spec = pl.BlockSpec((tr, pc), lambda i, k, idx_ref: (i, k))
        s_spec = pl.BlockSpec((None, tr, pc), lambda i, k, idx_ref: (idx_ref[1], i, 0))
        s_shape = (N_CHIP, r, pc)
    else:
        pr = r // N_CHIP
        grid = (N_CHIP,)
        g_spec = pl.BlockSpec((pr, cdim), lambda k, idx_ref: (k, idx_ref[0]))
        spec = pl.BlockSpec((pr, cdim), lambda k, idx_ref: (k, 0))
        s_spec = pl.BlockSpec((None, pr, cdim), lambda k, idx_ref: (idx_ref[1], 0, 0))
        s_shape = (N_CHIP, pr, cdim)
    return pl.pallas_call(
        body, name=name,
        grid_spec=pltpu.PrefetchScalarGridSpec(
            num_scalar_prefetch=1, grid=grid, in_specs=[g_spec, spec], out_specs=[spec, s_spec]),
        out_shape=[jax.ShapeDtypeStruct(land.shape, odt), jax.ShapeDtypeStruct(s_shape, odt)],
        compiler_params=_cp(*(["arbitrary"] * len(grid))),
    )(idx, g, land)


def _sum_slots(slots, idx, kind, layer, prev, name):
    _, r, cdim = slots.shape

    def body(*refs):
        s_ref, o_ref = refs[1], refs[-1]
        v = s_ref[...].astype(F32)
        o_ref[...] = (v[0] + v[1]) + (v[2] + v[3])

    if kind == SMALL:
        grid = (1,)
        s_spec = pl.BlockSpec((N_CHIP, r, cdim), lambda i, idx_ref: (0, 0, 0))
        o_spec = pl.BlockSpec((r, cdim), lambda i, idx_ref: (idx_ref[0], 0))
        full = (2 * r, cdim)
    else:
        tr = 256 if r % 256 == 0 else 384
        nrb = r // tr
        grid = (nrb,)
        s_spec = pl.BlockSpec((N_CHIP, tr, cdim), lambda i, idx_ref: (0, i, 0))
        if COL_SHARDED[kind]:
            o_spec = pl.BlockSpec((None, tr, cdim), lambda i, idx_ref: (layer, idx_ref[0] * nrb + i, 0))
            full = (2, 2 * r, cdim)
        else:
            o_spec = pl.BlockSpec((None, tr, cdim), lambda i, idx_ref: (layer, i, idx_ref[0]))
            full = (2, r, 2 * cdim)
    in_specs, args, aliases = [s_spec], [idx, slots], {}
    if prev is not None:
        in_specs.append(ANY)
        args.append(prev)
        aliases = {2: 0}
    return pl.pallas_call(
        body, name=name,
        grid_spec=pltpu.PrefetchScalarGridSpec(
            num_scalar_prefetch=1, grid=grid, in_specs=in_specs, out_specs=o_spec),
        out_shape=jax.ShapeDtypeStruct(full, F32),
        input_output_aliases=aliases,
        compiler_params=_cp(*(["parallel"] * len(grid))),
    )(*args)


def _block_diag(w):
    w4 = w.reshape(2, 4, 4, 64, 64)
    eye = jnp.eye(4, dtype=w.dtype)[None, None, :, None, :, None]
    return (w4[:, :, :, :, None, :] * eye).reshape(2, 4, 256, 256)


def _block_diag_extract(d):
    d5 = d.reshape(4, 4, 64, 4, 64)
    return jnp.stack([d5[:, hh, :, hh, :] for hh in range(4)], axis=1).reshape(-1)


REP_NAMES = ("norm1_g", "lru_conv_b", "lru_ba", "lru_bx", "lru_lambda", "norm2_g", "lru_wa", "lru_wx")


def _pack_rep(norm1_g, conv_b, ba, bx, lam, norm2_g, wa, wx, final_g):
    parts = [a.reshape(-1) for a in (norm1_g, conv_b, ba, bx, lam, norm2_g, wa, wx, final_g)]
    return jnp.concatenate(parts).reshape(REP_ROWS, LANES)


def _unpack_rep(buf):
    flat = buf.reshape(-1)
    res, o = {}, 0
    for k in REP_NAMES:
        shape = (2, 16, 64, 64) if k in ("lru_wa", "lru_wx") else (2, 1024)
        n = math.prod(shape)
        res[k] = flat[o:o + n].reshape(shape)
        o += n
    res["final_g"] = flat[o:o + 1024]
    return res


def _pack_conv_shard(lru_cw, sc_cw, ffn_cw):
    return jnp.concatenate([lru_cw.reshape(16, LANES), jnp.pad(sc_cw.reshape(6, LANES), ((0, 2), (0, 0))),
                            ffn_cw.reshape(72, LANES)], axis=0)


def _unpack_conv_shard(buf):
    return (buf[0:16].reshape(2, 4, 256), buf[16:22].reshape(2, 3, 128), buf[24:96].reshape(2, 3, 1536))


def kernel(x, norm1_g, w_in, lru_conv_w, lru_conv_b, lru_wa, lru_ba, lru_wx, lru_bx, lru_lambda, sc_conv_w, w_out, norm2_g, w_up, ffn_conv_w, w_down, final_g, loss_target, m_norm1_g, m_w_in, m_lru_conv_w, m_lru_conv_b, m_lru_wa, m_lru_ba, m_lru_wx, m_lru_bx, m_lru_lambda, m_sc_conv_w, m_w_out, m_norm2_g, m_w_up, m_ffn_conv_w, m_w_down, m_final_g, v_norm1_g, v_w_in, v_lru_conv_w, v_lru_conv_b, v_lru_wa, v_lru_ba, v_lru_wx, v_lru_bx, v_lru_lambda, v_sc_conv_w, v_w_out, v_norm2_g, v_w_up, v_ffn_conv_w, v_w_down, v_final_g):
    me = 2 * lax.axis_index("x") + lax.axis_index("y")
    idx = jnp.stack([lax.axis_index("c"), me]).astype(jnp.int32)
    t_len = x.shape[1]

    s_conv = _pack_conv_shard(lru_conv_w, sc_conv_w, ffn_conv_w)
    conv_slots = lax.dynamic_update_slice(jnp.zeros((N_CHIP, CONV_PACK_ROWS, LANES), F32), s_conv[None], (me, 0, 0))
    wi = list(_cast_into_full(w_in, W_IN, idx, "cast_w_in"))
    wo = list(_cast_into_full(w_out, W_OUT, idx, "cast_w_out"))
    wu = list(_cast_into_full(w_up, W_UP, idx, "cast_w_up"))
    wd = list(_cast_into_full(w_down, W_DOWN, idx, "cast_w_down"))
    wi[0], convs = _comm_call(_gather_plan([wi[0], conv_slots], [W_IN, SMALL]), "ag_first")
    per_chip = [_unpack_conv_shard(convs[k]) for k in range(N_CHIP)]
    lru_cw = jnp.concatenate([p[0] for p in per_chip], axis=-1)
    sc_cw = jnp.concatenate([p[1] for p in per_chip], axis=-1)
    ffn_cw = jnp.concatenate([p[2] for p in per_chip], axis=-1)

    prm = jnp.concatenate(
        [lru_cw, jnp.stack([lru_conv_b, lru_ba, lru_bx, lru_lambda], axis=1),
         jnp.pad(sc_cw, ((0, 0), (0, 0), (0, D_LRU - D_SC))), jnp.zeros((2, 5, D_LRU), F32)], axis=1)
    gates = jnp.concatenate([_block_diag(lru_wa), _block_diag(lru_wx)], axis=1).astype(BF16)
    fw8 = jnp.pad(ffn_cw.reshape(2, 3, 2, D_FF).transpose(0, 2, 1, 3), ((0, 0), (0, 0), (0, 5), (0, 0)))

    xs = x[0]
    saved = []
    n512, n256 = t_len // 512, t_len // 256
    whole, lower, upper = (0, 1), (0, 2), (1, 2)
    carried_by = {
        "fwd_in_0": ([(wu, 0, W_UP, (0, 4))], (max(n512 - 3, 0),)),
        "fwd_mixer_0": ([(wu, 0, W_UP, (1, 4, 2)), (wo, 0, W_OUT, whole)], (max(n256 - 3, 0),)),
        "fwd_out_0": ([(wu, 0, W_UP, (3, 4))], (max(n512 - 2, 0),)),
        "fwd_up_0": ([(wd, 0, W_DOWN, whole)], (max(n512 - 2, 0),)),
        "fwd_act_0": ([(wi, 1, W_IN, whole), (wo, 1, W_OUT, whole)], (max(n512 - 2, 0), 0)),
        "fwd_down_0": ([(wu, 1, W_UP, (0, 4))], (max(n512 - 3, 0),)),
        "fwd_in_1": ([(wu, 1, W_UP, (1, 4))], (max(n512 - 3, 0),)),
        "fwd_mixer_1": ([(wu, 1, W_UP, (2, 4, 2))], (max(n256 - 4, 0),)),
        "fwd_act_1": ([(wd, 1, W_DOWN, whole)], (max(n512 - 3, 0), 0)),
    }

    def carried(name):
        if name not in carried_by:
            return None, lambda got: None
        items, mid_at = carried_by[name]

        def store(got):
            for (lst, i, _, _), arr in zip(items, got):
                lst[i] = arr

        return _gather_plan([lst[i] for lst, i, _, _ in items], [k for _, _, k, _ in items], mid_at=mid_at,
                            parts=[p for _, _, _, p in items]), store

    for l in range(2):
        comm, store = carried(f"fwd_in_{l}")
        (z, h1), got = _norm_mm(xs, norm1_g[l][None], wi[l], f"fwd_in_{l}", comm=comm)
        store(got)
        comm, store = carried(f"fwd_mixer_{l}")
        (ymix, hst), got = _mixer_fwd(z, prm, gates, l, f"fwd_mixer_{l}", comm=comm)
        store(got)
        comm, store = carried(f"fwd_out_{l}")
        (x2,), got = _mm_res(ymix, wo[l], xs, f"fwd_out_{l}", comm=comm)
        store(got)
        comm, store = carried(f"fwd_up_{l}")
        (u, h2), got = _norm_mm(x2, norm2_g[l][None], wu[l], f"fwd_up_{l}", planes=True, comm=comm)
        store(got)
        comm, store = carried(f"fwd_act_{l}")
        (act,), got = _ffn_act(u, fw8, l, f"fwd_act_{l}", comm=comm)
        store(got)
        comm, store = carried(f"fwd_down_{l}")
        (x3,), got = _mm_res(act, wd[l], x2, f"fwd_down_{l}", comm=comm)
        store(got)
        saved.append((xs, h1, z, hst, ymix, x2, h2, u, act))
        xs = x3

    dx, dxb, dgf, loss_blk = _loss_head(xs, final_g[None], loss_target[0], "loss_head")

    kinds = [W_IN, W_OUT, W_UP, W_DOWN]
    grads = [None, None]
    small = [None, None]
    reduced = [None] * 4
    summed1 = [None] * 4
    parts = slots = None
    for l in (1, 0):
        x_in, h1, z, hst, ymix, x2, h2, u, act = saved[l]
        carry = l == 0
        comm = _pair_plan([grads[1][W_IN]], [W_IN]) if carry else None
        (g_down,), got = _mm_tn(act, dxb, f"bwd_wdown_{l}", tk=1536, tn=1024, comm=comm)
        if carry:
            summed1[W_IN] = _pair_sum(grads[1][W_IN], got[0], idx, W_IN, "rs_add1_0")
            parts, slots = [s[0] for s in summed1], [s[1] for s in summed1]
        (dact,), _ = _mm_nt(dxb, wd[l], f"bwd_dact_{l}")
        comm = _scatter_plan(parts, slots, kinds) if carry else None
        (du, dfw), got = _ffn_bwd(dact, u, fw8, l, f"bwd_act_{l}", comm=comm)
        if carry:
            reduced = [_sum_slots(got[w], idx, kinds[w], 1, None, f"rs_sum1_{w}") for w in range(4)]
        comm = _share_plan(reduced, kinds, 1) if carry else None
        (g_up,), got = _mm_tn(h2, du, f"bwd_wup_{l}", tk=1024, tn=1536, planes=True, comm=comm)
        if carry:
            reduced = list(got)
        comm = _pair_plan([g_up, g_down], [W_UP, W_DOWN]) if carry else None
        (dx2, dx2b, dg2), got = _mm_nt_normbwd(du, wu[l], x2, norm2_g[l][None], dx, f"bwd_up_{l}", planes=True,
                                               comm=comm)
        if carry:
            sum_up = _pair_sum(g_up, got[0], idx, W_UP, "rs_add0_2")
            sum_down = _pair_sum(g_down, got[1], idx, W_DOWN, "rs_add0_3")
        (g_out,), _ = _mm_tn(ymix, dx2b, f"bwd_wout_{l}", tk=1536, tn=1024)
        comm = _pair_plan([g_out], [W_OUT]) if carry else None
        (dymix,), got = _mm_nt(dx2b, wo[l], f"bwd_dymix_{l}", comm=comm)
        trio = (W_OUT, W_UP, W_DOWN)
        if carry:
            sum_out = _pair_sum(g_out, got[0], idx, W_OUT, "rs_add0_1")
            comm = _scatter_plan([sum_out[0], sum_up[0], sum_down[0]], [sum_out[1], sum_up[1], sum_down[1]], trio)
        else:
            comm = _pair_plan([g_out, g_up, g_down], trio)
        (dz, dprm, dgates), got = _mixer_bwd(z, hst, dymix, prm, gates, l, f"bwd_mixer_{l}", comm=comm)
        if carry:
            for w, s in zip(trio, got):
                reduced[w] = _sum_slots(s, idx, w, 0, reduced[w], f"rs_sum0_{w}")
        else:
            for w, g, land in zip(trio, (g_out, g_up, g_down), got):
                summed1[w] = _pair_sum(g, land, idx, w, f"rs_add1_{w}")
        comm = _share_plan([reduced[w] for w in trio], trio, 0) if carry else None
        (g_in,), got = _mm_tn(h1, dz, f"bwd_win_{l}", tk=1024, tn=1792, comm=comm)
        if carry:
            for w, full in zip(trio, got):
                reduced[w] = full
        comm = _pair_plan([g_in], [W_IN]) if carry else None
        (dx, dxb, dg1), got = _mm_nt_normbwd(dz, wi[l], x_in, norm1_g[l][None], dx2, f"bwd_in_{l}", comm=comm)
        if carry:
            sum_in = _pair_sum(g_in, got[0], idx, W_IN, "rs_add0_0")
        grads[l] = [g_in, g_out, g_up, g_down]
        rep = dict(zip(REP_NAMES, [dg1[0], dprm[4], dprm[5], dprm[6], dprm[7], dg2[0],
                                   _block_diag_extract(dgates[0:4]), _block_diag_extract(dgates[4:8])]))
        conv = [dprm[0:4].reshape(-1), jnp.pad(dprm[8:11, 0:D_SC].reshape(-1), (0, 512)),
                dfw[:, 0:3, :].transpose(1, 0, 2).reshape(-1)]
        small[l] = (rep, conv)
    grad_x = dx[None]
    g_small = jnp.concatenate(
        [small[l][0][k] for k in REP_NAMES for l in range(2)] + [dgf[0]] + small[0][1] + small[1][1]
        + [loss_blk.reshape(-1)]).reshape(SMALL_ROWS, LANES)

    def big(w, g, m, v, name, comm=None):
        shape = w.shape
        two_d = lambda a: a.reshape(-1, shape[-1])
        outs, got = _adamw(two_d(w), two_d(g), two_d(m), two_d(v), name, emit_grad=True, comm=comm)
        return [o.reshape(shape) for o in outs], got

    land_small, = _comm_call(_pair_plan([g_small], [SMALL]), "rs_pair_small")
    sum_small = _pair_sum(g_small, land_small, idx, SMALL, "rs_add0_4")
    upd = {}
    upd["w_up"], (slot_in,) = big(w_up, reduced[W_UP], m_w_up, v_w_up, "adamw_w_up",
                                  comm=_scatter_plan([sum_in[0]], [sum_in[1]], [W_IN]))
    upd["w_down"], (slot_small,) = big(w_down, reduced[W_DOWN], m_w_down, v_w_down, "adamw_w_down",
                                       comm=_scatter_plan([sum_small[0]], [sum_small[1]], [SMALL]))
    upd["w_out"], _ = big(w_out, reduced[W_OUT], m_w_out, v_w_out, "adamw_w_out")
    gw_in, gs = _comm_call(
        _share_plan([_sum_slots(slot_in, idx, W_IN, 0, reduced[W_IN], "rs_sum0_0"),
                     _sum_slots(slot_small, idx, SMALL, 0, None, "rs_sum0_4")], [W_IN, SMALL], 0), "rs_share0")
    upd["w_in"], _ = big(w_in, gw_in, m_w_in, v_w_in, "adamw_w_in")

    loss = gs[REP_ROWS + CONV_ROWS, 0]
    g_rep = gs[0:REP_ROWS]
    g_conv = gs[REP_ROWS:REP_ROWS + CONV_ROWS].reshape(2, CONV_LAYER)
    g_lru_cw = lax.dynamic_slice_in_dim(g_conv[:, 0:4096].reshape(2, 4, 1024), me * 256, 256, axis=2)
    g_sc_cw = lax.dynamic_slice_in_dim(g_conv[:, 4096:4096 + 1536].reshape(2, 3, 512), me * 128, 128, axis=2)
    g_ffn_cw = lax.dynamic_slice_in_dim(g_conv[:, 6144:].reshape(2, 3, 6144), me * 1536, 1536, axis=2)

    rep_out, _ = _adamw(
        _pack_rep(norm1_g, lru_conv_b, lru_ba, lru_bx, lru_lambda, norm2_g, lru_wa, lru_wx, final_g), g_rep,
        _pack_rep(m_norm1_g, m_lru_conv_b, m_lru_ba, m_lru_bx, m_lru_lambda, m_norm2_g, m_lru_wa, m_lru_wx, m_final_g),
        _pack_rep(v_norm1_g, v_lru_conv_b, v_lru_ba, v_lru_bx, v_lru_lambda, v_norm2_g, v_lru_wa, v_lru_wx, v_final_g),
        "adamw_rep")
    conv_out, _ = _adamw(s_conv, _pack_conv_shard(g_lru_cw, g_sc_cw, g_ffn_cw),
                         _pack_conv_shard(m_lru_conv_w, m_sc_conv_w, m_ffn_conv_w),
                         _pack_conv_shard(v_lru_conv_w, v_sc_conv_w, v_ffn_conv_w), "adamw_conv")

    names = ["norm1_g", "w_in", "lru_conv_w", "lru_conv_b", "lru_wa", "lru_ba", "lru_wx", "lru_bx", "lru_lambda",
             "sc_conv_w", "w_out", "norm2_g", "w_up", "ffn_conv_w", "w_down", "final_g"]
    groups = []
    g_all = dict(_unpack_rep(g_rep))
    g_all.update({k: v[3] for k, v in upd.items()})
    g_all.update(lru_conv_w=g_lru_cw, sc_conv_w=g_sc_cw, ffn_conv_w=g_ffn_cw)
    groups.append(g_all)
    for i in range(3):
        d = dict(_unpack_rep(rep_out[i]))
        cl, cs, cf = _unpack_conv_shard(conv_out[i])
        d.update(lru_conv_w=cl, sc_conv_w=cs, ffn_conv_w=cf)
        d.update({k: v[i] for k, v in upd.items()})
        groups.append(d)
    return (loss, grad_x, *[grp[n] for grp in groups for n in names])
```

```python
import dataclasses
import functools
import math
import operator
from typing import Any, Callable, Optional, Sequence

import jax
import jax.numpy as jnp
from jax import lax
from jax.experimental import pallas as pl
from jax.experimental.pallas import tpu as pltpu

F32 = jnp.float32
BF16 = jnp.bfloat16
MESH = pl.DeviceIdType.MESH

D_MODEL = 1024
D_LRU = 1024
D_SC = 512
D_MIX = D_LRU + D_SC
D_IN = 2 * D_LRU + 3 * D_SC
D_FF = 3072
N_CHIP = 4
RG_C = 8.0
EPS = 1e-6
ADAM_LR = 0.001
ADAM_B1 = 0.9
ADAM_B2 = 0.999
ADAM_EPS = 1e-08
ADAM_WD = 0.01
ADAM_STEP = 10

SUBLANES = 8
PACKED = 16
LANES = 128
VMEM_LIMIT = 56 * 1024 * 1024
GELU_C0 = math.sqrt(2.0 / math.pi)
GELU_C1 = 0.044715

REP_LAYER = 6 * 1024 + 2 * 16 * 64 * 64
REP_ROWS = (2 * REP_LAYER + 1024) // LANES
CONV_LAYER = 4 * 1024 + 2048 + 3 * 6144
CONV_ROWS = 2 * CONV_LAYER // LANES
SMALL_ROWS = REP_ROWS + CONV_ROWS + 8
CONV_PACK_ROWS = 96

W_IN, W_OUT, W_UP, W_DOWN, SMALL = range(5)
COL_SHARDED = {W_IN: True, W_OUT: False, W_UP: True, W_DOWN: False}

ONCE = pl.Buffered(1)
ANY = pl.BlockSpec(memory_space=pl.ANY)


def _cp(*sem):
    return pltpu.CompilerParams(dimension_semantics=sem, vmem_limit_bytes=VMEM_LIMIT)


@dataclasses.dataclass
class Comm:
    srcs: Sequence[Any]
    bufs: Sequence[Any]
    outs: Sequence[Any]
    n_sem: int
    start: Callable
    finish: Callable
    mid: Optional[Callable] = None
    mid_at: Optional[Sequence[int]] = None


def _pallas(body, *, name, grid, in_specs, out_specs, out_shape, args, sem, scratch_shapes=(), comm=None):
    if comm is None:
        res = pl.pallas_call(
            body, name=name, grid=grid, in_specs=list(in_specs), out_specs=list(out_specs),
            out_shape=list(out_shape), scratch_shapes=list(scratch_shapes), compiler_params=_cp(*sem))(*args)
        return tuple(res), ()
    n_in, n_out, n_scr = len(in_specs), len(out_specs), len(scratch_shapes)
    ns, nb, no = len(comm.srcs), len(comm.bufs), len(comm.outs)

    def carrier(*refs):
        p = 0
        main_in = refs[p:p + n_in]
        p += n_in
        c_src = refs[p:p + ns]
        p += ns + nb
        main_out = refs[p:p + n_out]
        p += n_out
        c_buf = refs[p:p + nb]
        p += nb
        c_out = refs[p:p + no]
        p += no
        scr = refs[p:p + n_scr]
        send, recv = refs[p + n_scr], refs[p + n_scr + 1]
        ids = [pl.program_id(a) for a in range(len(grid))]

        def at(steps):
            return functools.reduce(operator.and_, [i == s for i, s in zip(ids, steps)])

        @pl.when(at([0] * len(grid)))
        def _():
            comm.start(c_src, c_buf, c_out, send, recv)

        if comm.mid is not None:
            @pl.when(at(comm.mid_at))
            def _():
                comm.mid(c_src, c_buf, c_out, send, recv)

        body(*main_in, *main_out, *scr)

        @pl.when(at([g - 1 for g in grid]))
        def _():
            comm.finish(c_src, c_buf, c_out, send, recv)

    res = pl.pallas_call(
        carrier, name=name, grid=grid,
        in_specs=list(in_specs) + [ANY] * (ns + nb),
        out_specs=list(out_specs) + [ANY] * (nb + no),
        out_shape=list(out_shape) + [jax.ShapeDtypeStruct(b.shape, b.dtype) for b in comm.bufs] + list(comm.outs),
        input_output_aliases={n_in + ns + j: n_out + j for j in range(nb)},
        scratch_shapes=list(scratch_shapes) + [pltpu.SemaphoreType.DMA((comm.n_sem,)),
                                               pltpu.SemaphoreType.DMA((comm.n_sem,))],
        compiler_params=_cp(*(["arbitrary"] * len(grid))),
    )(*args, *comm.srcs, *comm.bufs)
    return tuple(res[:n_out]), tuple(res[n_out:])


def _comm_call(comm, name):
    ns, nb, no = len(comm.srcs), len(comm.bufs), len(comm.outs)

    def body(*refs):
        c_src = refs[0:ns]
        c_buf = refs[ns + nb:ns + 2 * nb]
        c_out = refs[ns + 2 * nb:ns + 2 * nb + no]
        send, recv = refs[ns + 2 * nb + no], refs[ns + 2 * nb + no + 1]
        comm.start(c_src, c_buf, c_out, send, recv)
        if comm.mid is not None:
            comm.mid(c_src, c_buf, c_out, send, recv)
        comm.finish(c_src, c_buf, c_out, send, recv)

    return tuple(pl.pallas_call(
        body, name=name,
        in_specs=[ANY] * (ns + nb), out_specs=[ANY] * (nb + no),
        out_shape=[jax.ShapeDtypeStruct(b.shape, b.dtype) for b in comm.bufs] + list(comm.outs),
        input_output_aliases={ns + j: j for j in range(nb)},
        scratch_shapes=[pltpu.SemaphoreType.DMA((comm.n_sem,)), pltpu.SemaphoreType.DMA((comm.n_sem,))],
    )(*comm.srcs, *comm.bufs))


def _sigmoid(v):
    return 1.0 / (1.0 + jnp.exp(-v))


def _sigmoid_tanh(v):
    return 0.5 + 0.5 * jnp.tanh(0.5 * v)


def _gelu_parts(v):
    v2 = v * v
    t = jnp.tanh(GELU_C0 * v * (1.0 + GELU_C1 * v2))
    half = 0.5 * (1.0 + t)
    gel = v * half
    dgel = half + 0.5 * v * (1.0 - t * t) * (GELU_C0 * (1.0 + 3.0 * GELU_C1 * v2))
    return gel, dgel


def _gelu(v):
    t = jnp.tanh(GELU_C0 * v * (1.0 + GELU_C1 * (v * v)))
    return 0.5 * v * (1.0 + t)


def _neg_expm1(y, a):
    p = jnp.full_like(y, 1.0 / 120.0)
    for coef in (1.0 / 24.0, 1.0 / 6.0, 0.5, 1.0):
        p = p * y + coef
    return jnp.where(y > -0.1, -(p * y), 1.0 - a * a)


def _softplus_neg(lam):
    nl = -lam
    e = jnp.exp(-jnp.abs(nl))
    u = 1.0 + e
    l1p = jnp.where(u == 1.0, e, jnp.log(u) * e / (u - 1.0))
    return jnp.maximum(nl, 0.0) + l1p


def _conv_taps(ext, taps, n_out):
    kw = len(taps)
    acc = taps[kw - 1] * ext[SUBLANES:SUBLANES + n_out]
    for k in range(kw - 1):
        acc = acc + taps[k] * pltpu.roll(ext, kw - 1 - k, axis=0)[SUBLANES:SUBLANES + n_out]
    return acc


def _conv_taps_t(ext, taps, n_out):
    kw = len(taps)
    n = ext.shape[0]
    acc = taps[kw - 1] * ext[0:n_out]
    for k in range(kw - 1):
        acc = acc + taps[k] * pltpu.roll(ext, n - (kw - 1 - k), axis=0)[0:n_out]
    return acc


def _scan8(a, b, carry, row):
    for s in (1, 2, 4):
        m = row >= s
        a_sh = jnp.where(m, pltpu.roll(a, s, axis=0), 1.0)
        b_sh = jnp.where(m, pltpu.roll(b, s, axis=0), 0.0)
        b = a * b_sh + b
        a = a * a_sh
    return a * carry + b


def _scan8_rev(a, b, carry, row):
    for s in (1, 2, 4):
        m = row < SUBLANES - s
        a_sh = jnp.where(m, pltpu.roll(a, SUBLANES - s, axis=0), 1.0)
        b_sh = jnp.where(m, pltpu.roll(b, SUBLANES - s, axis=0), 0.0)
        b = a * b_sh + b
        a = a * a_sh
    return a * carry + b


def _cast_into_full(w, kind, idx, name):
    nl, r, c = w.shape
    tr = 256 if r % 256 == 0 else r
    nrb = r // tr

    def body(idx_ref, w_ref, o0_ref, o1_ref):
        o0_ref[...] = w_ref[0].astype(BF16)
        o1_ref[...] = w_ref[1].astype(BF16)

    if COL_SHARDED[kind]:
        full = (r, N_CHIP * c)
        o_spec = pl.BlockSpec((tr, c), lambda i, idx_ref: (i, idx_ref[1]))
    else:
        full = (N_CHIP * r, c)
        o_spec = pl.BlockSpec((tr, c), lambda i, idx_ref: (idx_ref[1] * nrb + i, 0))
    return pl.pallas_call(
        body, name=name,
        grid_spec=pltpu.PrefetchScalarGridSpec(
            num_scalar_prefetch=1, grid=(nrb,),
            in_specs=[pl.BlockSpec((nl, tr, c), lambda i, idx_ref: (0, i, 0))], out_specs=[o_spec, o_spec]),
        out_shape=[jax.ShapeDtypeStruct(full, BF16)] * 2,
        compiler_params=_cp("parallel"),
    )(idx, w)


def _norm_mm(x, g, w, name, planes=False, tm=512, tn=512, comm=None):
    t_len, d = x.shape
    n = w.shape[1]
    half = n // 2

    def body(x_ref, g_ref, w_ref, z_ref, h_ref):
        xv = x_ref[...]
        r = lax.rsqrt(jnp.mean(xv * xv, axis=-1, keepdims=True) + EPS)
        h_ref[...] = ((xv * r) * g_ref[...]).astype(BF16)
        for n0 in range(0, n, tn):
            blk = jnp.dot(h_ref[...], w_ref[:, n0:n0 + tn], preferred_element_type=F32).astype(BF16)
            if planes:
                z_ref[n0 // half, :, n0 % half:n0 % half + tn] = blk
            else:
                z_ref[:, n0:n0 + tn] = blk

    if planes:
        z_shape = jax.ShapeDtypeStruct((2, t_len, half), BF16)
        z_spec = pl.BlockSpec((2, tm, half), lambda i: (0, i, 0))
    else:
        z_shape = jax.ShapeDtypeStruct((t_len, n), BF16)
        z_spec = pl.BlockSpec((tm, n), lambda i: (i, 0))
    return _pallas(
        body, name=name, grid=(t_len // tm,),
        in_specs=[pl.BlockSpec((tm, d), lambda i: (i, 0)),
                  pl.BlockSpec((1, d), lambda i: (0, 0)),
                  pl.BlockSpec((d, n), lambda i: (0, 0), pipeline_mode=ONCE)],
        out_specs=[z_spec, pl.BlockSpec((tm, d), lambda i: (i, 0))],
        out_shape=[z_shape, jax.ShapeDtypeStruct((t_len, d), BF16)],
        args=(x, g, w), sem=("parallel",), comm=comm)


def _mm_res(a, w, res, name, tm=512, comm=None):
    t_len, k = a.shape
    n = w.shape[1]

    def body(a_ref, w_ref, r_ref, o_ref):
        o_ref[...] = r_ref[...] + jnp.dot(a_ref[...], w_ref[...], preferred_element_type=F32)

    return _pallas(
        body, name=name, grid=(t_len // tm,),
        in_specs=[pl.BlockSpec((tm, k), lambda i: (i, 0)),
                  pl.BlockSpec((k, n), lambda i: (0, 0), pipeline_mode=ONCE),
                  pl.BlockSpec((tm, n), lambda i: (i, 0))],
        out_specs=[pl.BlockSpec((tm, n), lambda i: (i, 0))],
        out_shape=[jax.ShapeDtypeStruct((t_len, n), F32)],
        args=(a, w, res), sem=("parallel",), comm=comm)


def _mm_nt(a, w, name, tm=512, comm=None):
    t_len, k = a.shape
    n = w.shape[0]

    def body(a_ref, w_ref, o_ref):
        o_ref[...] = lax.dot_general(a_ref[...], w_ref[...], (((1,), (1,)), ((), ())),
                                     preferred_element_type=F32).astype(BF16)

    return _pallas(
        body, name=name, grid=(t_len // tm,),
        in_specs=[pl.BlockSpec((tm, k), lambda i: (i, 0)),
                  pl.BlockSpec((n, k), lambda i: (0, 0), pipeline_mode=ONCE)],
        out_specs=[pl.BlockSpec((tm, n), lambda i: (i, 0))],
        out_shape=[jax.ShapeDtypeStruct((t_len, n), BF16)],
        args=(a, w), sem=("parallel",), comm=comm)


def _mm_nt_normbwd(dz, w, x, g, dres, name, planes=False, tm=512, comm=None):
    t_len, d = x.shape
    n = w.shape[1]
    half = n // 2
    nt_dims = (((1,), (1,)), ((), ()))

    def body(dz_ref, w_ref, x_ref, g_ref, r_ref, dx_ref, dxb_ref, dg_ref):
        @pl.when(pl.program_id(0) == 0)
        def _():
            dg_ref[...] = jnp.zeros_like(dg_ref)

        if planes:
            dh = (lax.dot_general(dz_ref[0], w_ref[:, 0:half], nt_dims, preferred_element_type=F32)
                  + lax.dot_general(dz_ref[1], w_ref[:, half:], nt_dims, preferred_element_type=F32))
        else:
            dh = lax.dot_general(dz_ref[...], w_ref[...], nt_dims, preferred_element_type=F32)
        xv = x_ref[...]
        r = lax.rsqrt(jnp.mean(xv * xv, axis=-1, keepdims=True) + EPS)
        xh = xv * r
        dhg = dh * g_ref[...]
        dx = r_ref[...] + r * (dhg - xh * jnp.mean(dhg * xh, axis=-1, keepdims=True))
        dx_ref[...] = dx
        dxb_ref[...] = dx.astype(BF16)
        dg_ref[0:1, :] += jnp.sum(dh * xh, axis=0, keepdims=True)

    if planes:
        dz_spec = pl.BlockSpec((2, tm, half), lambda i: (0, i, 0))
    else:
        dz_spec = pl.BlockSpec((tm, n), lambda i: (i, 0))
    return _pallas(
        body, name=name, grid=(t_len // tm,),
        in_specs=[dz_spec,
                  pl.BlockSpec((d, n), lambda i: (0, 0), pipeline_mode=ONCE),
                  pl.BlockSpec((tm, d), lambda i: (i, 0)),
                  pl.BlockSpec((1, d), lambda i: (0, 0)),
                  pl.BlockSpec((tm, d), lambda i: (i, 0))],
        out_specs=[pl.BlockSpec((tm, d), lambda i: (i, 0)),
                   pl.BlockSpec((tm, d), lambda i: (i, 0)),
                   pl.BlockSpec((SUBLANES, d), lambda i: (0, 0))],
        out_shape=[jax.ShapeDtypeStruct((t_len, d), F32),
                   jax.ShapeDtypeStruct((t_len, d), BF16),
                   jax.ShapeDtypeStruct((SUBLANES, d), F32)],
        args=(dz, w, x, g, dres), sem=("arbitrary",), comm=comm)


def _mm_tn(a, g, name, tk, tn, planes=False, tt=1024, comm=None):
    t_len, k = a.shape
    n = 2 * g.shape[2] if planes else g.shape[1]
    nn = n // tn
    half = nn // 2
    tt = min(tt, t_len)

    def body(a_ref, g_ref, o_ref):
        @pl.when(pl.program_id(2) == 0)
        def _():
            o_ref[...] = jnp.zeros_like(o_ref)

        o_ref[...] += lax.dot_general(a_ref[...], g_ref[...], (((0,), (0,)), ((), ())),
                                      preferred_element_type=F32)

    if planes:
        g_spec = pl.BlockSpec((None, tt, tn), lambda i, j, t: (j // half, t, j % half))
    else:
        g_spec = pl.BlockSpec((tt, tn), lambda i, j, t: (t, j))
    return _pallas(
        body, name=name, grid=(k // tk, nn, t_len // tt),
        in_specs=[pl.BlockSpec((tt, tk), lambda i, j, t: (t, i)), g_spec],
        out_specs=[pl.BlockSpec((tk, tn), lambda i, j, t: (i, j))],
        out_shape=[jax.ShapeDtypeStruct((k, n), F32)],
        args=(a, g), sem=("parallel", "parallel", "arbitrary"), comm=comm)


def _lru_gates(rp, ip, spn):
    r = _sigmoid(rp)
    i = _sigmoid_tanh(ip)
    la = r * spn
    a = jnp.exp(la)
    mult = jnp.sqrt(_neg_expm1(2.0 * la, a))
    return r, i, a, mult


def _mixer_fwd(z, prm, gates, layer, name, tb=256, comm=None):
    t_len = z.shape[0]

    def body(z_ref, p_ref, g_ref, y_ref, h_ref, xhalo, phalo, hcar, lx_s, rp_s, ip_s):
        @pl.when(pl.program_id(0) == 0)
        def _():
            xhalo[...] = jnp.zeros_like(xhalo)
            phalo[...] = jnp.zeros_like(phalo)
            hcar[...] = jnp.zeros_like(hcar)

        prm_v = p_ref[...]
        cw = prm_v[0:4]
        vec = prm_v[4:8]
        xp = z_ref[:, 0:D_LRU].astype(F32)
        ext = jnp.concatenate([xhalo[...], xp], axis=0)
        lx = vec[0:1] + _conv_taps(ext, [cw[k:k + 1] for k in range(4)], tb)
        xhalo[...] = xp[tb - SUBLANES:]
        lx_s[...] = lx
        lxb = lx.astype(BF16)
        for q in range(4):
            sl = slice(q * 256, (q + 1) * 256)
            rp_s[:, sl] = jnp.dot(lxb[:, sl], g_ref[q], preferred_element_type=F32) + vec[1:2, sl]
            ip_s[:, sl] = jnp.dot(lxb[:, sl], g_ref[4 + q], preferred_element_type=F32) + vec[2:3, sl]

        spn = jnp.broadcast_to(-RG_C * _softplus_neg(vec[3:4]), (SUBLANES, D_LRU))
        row = lax.broadcasted_iota(jnp.int32, (SUBLANES, D_LRU), 0)

        def step(ci, carry):
            o = pl.multiple_of(ci * PACKED, PACKED)
            gate = z_ref[pl.ds(o, PACKED), D_LRU:2 * D_LRU].astype(F32)
            ys = []
            for sub in range(2):
                rows = pl.ds(pl.multiple_of(o + sub * SUBLANES, SUBLANES), SUBLANES)
                lxv = lx_s[rows, :]
                _, i, a, mult = _lru_gates(rp_s[rows, :], ip_s[rows, :], spn)
                h = _scan8(a, mult * (i * lxv), carry, row)
                h_ref[rows, :] = h
                ys.append(h * _gelu(gate[sub * SUBLANES:(sub + 1) * SUBLANES]))
                carry = jnp.broadcast_to(h[SUBLANES - 1:SUBLANES, :], (SUBLANES, D_LRU))
            y_ref[pl.ds(o, PACKED), 0:D_LRU] = jnp.concatenate(ys, axis=0).astype(BF16)
            return carry

        hcar[...] = lax.fori_loop(0, tb // PACKED, step, hcar[...])

        scw = prm_v[8:11, 0:D_SC]
        o_b, o_c, o_x = 2 * D_LRU, 2 * D_LRU + D_SC, 2 * D_LRU + 2 * D_SC
        p = z_ref[:, o_c:o_x].astype(F32) * z_ref[:, o_x:].astype(F32)
        pext = jnp.concatenate([phalo[...], p], axis=0)
        q = _conv_taps(pext, [scw[k:k + 1] for k in range(3)], tb)
        phalo[...] = p[tb - SUBLANES:]
        y_ref[:, D_LRU:] = (z_ref[:, o_b:o_c].astype(F32) * q).astype(BF16)

    return _pallas(
        body, name=name, grid=(t_len // tb,),
        in_specs=[pl.BlockSpec((tb, D_IN), lambda t: (t, 0)),
                  pl.BlockSpec((None, 2 * SUBLANES, D_LRU), lambda t: (layer, 0, 0)),
                  pl.BlockSpec((None, 8, 256, 256), lambda t: (layer, 0, 0, 0))],
        out_specs=[pl.BlockSpec((tb, D_MIX), lambda t: (t, 0)),
                   pl.BlockSpec((tb, D_LRU), lambda t: (t, 0))],
        out_shape=[jax.ShapeDtypeStruct((t_len, D_MIX), BF16),
                   jax.ShapeDtypeStruct((t_len, D_LRU), F32)],
        scratch_shapes=[pltpu.VMEM((SUBLANES, D_LRU), F32), pltpu.VMEM((SUBLANES, D_SC), F32),
                        pltpu.VMEM((SUBLANES, D_LRU), F32), pltpu.VMEM((tb, D_LRU), F32),
                        pltpu.VMEM((tb, D_LRU), F32), pltpu.VMEM((tb, D_LRU), F32)],
        args=(z, prm, gates), sem=("arbitrary",), comm=comm)


def _mixer_bwd(z, h, dy, prm, gates, layer, name, tb=256, comm=None):
    t_len = z.shape[0]
    nb = t_len // tb

    def body(z_ref, zh_ref, h_ref, hh_ref, dy_ref, p_ref, g_ref, dz_ref, dp_ref, dg_ref,
             lx_s, rp_s, ip_s, drpb_s, dipb_s, dlx_s, hext_s, acc_s, acar, gcar, dqh):
        t = pl.program_id(0)
        first_block = t == nb - 1

        @pl.when(t == 0)
        def _():
            for ref in (dp_ref, dg_ref, acc_s, acar, gcar, dqh):
                ref[...] = jnp.zeros_like(ref)
            dlx_s[tb:, :] = jnp.zeros((SUBLANES, D_LRU), F32)

        prm_v = p_ref[...]
        cw = prm_v[0:4]
        vec = prm_v[4:8]
        scw = prm_v[8:11, 0:D_SC]
        wa_ref = [g_ref.at[q] for q in range(4)]
        wx_ref = [g_ref.at[4 + q] for q in range(4)]
        dwa_ref = [dg_ref.at[q] for q in range(4)]
        dwx_ref = [dg_ref.at[4 + q] for q in range(4)]
        ctaps = [cw[k:k + 1] for k in range(4)]
        staps = [scw[k:k + 1] for k in range(3)]
        keep = jnp.where(first_block, 0.0, 1.0)
        zh = zh_ref[...].astype(F32)[PACKED - SUBLANES:] * keep

        xp = z_ref[:, 0:D_LRU].astype(F32)
        xext = jnp.concatenate([zh[:, 0:D_LRU], xp], axis=0)
        lx = vec[0:1] + _conv_taps(xext, ctaps, tb)
        lx_s[...] = lx
        lxb = lx.astype(BF16)
        for q in range(4):
            sl = slice(q * 256, (q + 1) * 256)
            rp_s[:, sl] = jnp.dot(lxb[:, sl], wa_ref[q][...], preferred_element_type=F32) + vec[1:2, sl]
            ip_s[:, sl] = jnp.dot(lxb[:, sl], wx_ref[q][...], preferred_element_type=F32) + vec[2:3, sl]
        hext_s[0:SUBLANES, :] = hh_ref[...] * keep
        hext_s[SUBLANES:, :] = h_ref[...]

        spn = jnp.broadcast_to(-RG_C * _softplus_neg(vec[3:4]), (SUBLANES, D_LRU))
        row = lax.broadcasted_iota(jnp.int32, (SUBLANES, D_LRU), 0)

        def step(ci, carry):
            a_next, g_next = carry
            o = pl.multiple_of((tb // PACKED - 1 - ci) * PACKED, PACKED)
            rows16 = pl.ds(o, PACKED)
            gate16 = z_ref[rows16, D_LRU:2 * D_LRU].astype(F32)
            dyl16 = dy_ref[rows16, 0:D_LRU].astype(F32)
            dgs, drs, dis = [None, None], [None, None], [None, None]
            for sub in (1, 0):
                oo = pl.multiple_of(o + sub * SUBLANES, SUBLANES)
                rows = pl.ds(oo, SUBLANES)
                half = slice(sub * SUBLANES, (sub + 1) * SUBLANES)
                lxv = lx_s[rows, :]
                r, i, a, mult = _lru_gates(rp_s[rows, :], ip_s[rows, :], spn)
                hwin = hext_s[pl.ds(oo, 2 * SUBLANES), :]
                hv = hwin[SUBLANES:]
                hprev = pltpu.roll(hwin, 1, axis=0)[SUBLANES:]
                gel, dgel = _gelu_parts(gate16[half])
                dyl = dyl16[half]
                a_up = jnp.where(row < SUBLANES - 1, pltpu.roll(a, SUBLANES - 1, axis=0), a_next)
                gg = _scan8_rev(a_up, dyl * gel, g_next, row)
                dgs[sub] = dyl * hv * dgel
                ilx = i * lxv
                dla = gg * hprev * a - (gg * ilx) * (a * a) / mult
                dlx_s[rows, :] = gg * mult * i
                drp = dla * spn * r * (1.0 - r)
                dip = gg * mult * lxv * i * (1.0 - i)
                drs[sub] = drp
                dis[sub] = dip
                acc_s[0] += drp
                acc_s[1] += dip
                acc_s[2] += dla * r
                a_next = jnp.broadcast_to(a[0:1, :], (SUBLANES, D_LRU))
                g_next = jnp.broadcast_to(gg[0:1, :], (SUBLANES, D_LRU))
            dz_ref[rows16, D_LRU:2 * D_LRU] = jnp.concatenate(dgs, axis=0).astype(BF16)
            drpb_s[rows16, :] = jnp.concatenate(drs, axis=0).astype(BF16)
            dipb_s[rows16, :] = jnp.concatenate(dis, axis=0).astype(BF16)
            return a_next, g_next

        a_c, g_c = lax.fori_loop(0, tb // PACKED, step, (acar[...], gcar[...]))
        acar[...] = a_c
        gcar[...] = g_c

        drpb = drpb_s[...]
        dipb = dipb_s[...]
        nt_dims = (((1,), (1,)), ((), ()))
        tn_dims = (((0,), (0,)), ((), ()))
        for q in range(4):
            sl = slice(q * 256, (q + 1) * 256)
            dlx_s[0:tb, sl] += (
                lax.dot_general(drpb[:, sl], wa_ref[q][...], nt_dims, preferred_element_type=F32)
                + lax.dot_general(dipb[:, sl], wx_ref[q][...], nt_dims, preferred_element_type=F32))
            dwa_ref[q][...] += lax.dot_general(lxb[:, sl], drpb[:, sl], tn_dims, preferred_element_type=F32)
            dwx_ref[q][...] += lax.dot_general(lxb[:, sl], dipb[:, sl], tn_dims, preferred_element_type=F32)

        dlx_ext = dlx_s[...]
        dlx = dlx_ext[0:tb]
        dz_ref[:, 0:D_LRU] = _conv_taps_t(dlx_ext, ctaps, tb).astype(BF16)
        dp_ref[3:4, :] += jnp.sum(dlx * xp, axis=0, keepdims=True)
        for k in range(3):
            shifted = pltpu.roll(xext, 3 - k, axis=0)[SUBLANES:]
            dp_ref[k:k + 1, :] += jnp.sum(dlx * shifted, axis=0, keepdims=True)
        dp_ref[4:5, :] += jnp.sum(dlx, axis=0, keepdims=True)
        dlx_s[tb:, :] = dlx[0:SUBLANES]

        o_b, o_c, o_x = 2 * D_LRU, 2 * D_LRU + D_SC, 2 * D_LRU + 2 * D_SC
        sb = z_ref[:, o_b:o_c].astype(F32)
        scc = z_ref[:, o_c:o_x].astype(F32)
        sx = z_ref[:, o_x:].astype(F32)
        p = scc * sx
        pext = jnp.concatenate([zh[:, o_c:o_x] * zh[:, o_x:], p], axis=0)
        q = _conv_taps(pext, staps, tb)
        dys = dy_ref[:, D_LRU:].astype(F32)
        dq = dys * sb
        dp = _conv_taps_t(jnp.concatenate([dq, dqh[...]], axis=0), staps, tb)
        dp_ref[10:11, 0:D_SC] += jnp.sum(dq * p, axis=0, keepdims=True)
        for k in range(2):
            shifted = pltpu.roll(pext, 2 - k, axis=0)[SUBLANES:]
            dp_ref[8 + k:9 + k, 0:D_SC] += jnp.sum(dq * shifted, axis=0, keepdims=True)
        dqh[...] = dq[0:SUBLANES]
        dz_ref[:, o_b:o_c] = (dys * q).astype(BF16)
        dz_ref[:, o_c:o_x] = (dp * sx).astype(BF16)
        dz_ref[:, o_x:] = (dp * scc).astype(BF16)

        @pl.when(first_block)
        def _():
            dp_ref[5:6, :] = jnp.sum(acc_s[0], axis=0, keepdims=True)
            dp_ref[6:7, :] = jnp.sum(acc_s[1], axis=0, keepdims=True)
            dp_ref[7:8, :] = (jnp.sum(acc_s[2], axis=0, keepdims=True) * RG_C * _sigmoid(-vec[3:4]))

    blk = lambda t: (nb - 1 - t, 0)
    halo8 = lambda t: (jnp.maximum((nb - 1 - t) * (tb // SUBLANES) - 1, 0), 0)
    halo16 = lambda t: (jnp.maximum((nb - 1 - t) * (tb // PACKED) - 1, 0), 0)
    return _pallas(
        body, name=name, grid=(nb,),
        in_specs=[pl.BlockSpec((tb, D_IN), blk), pl.BlockSpec((PACKED, D_IN), halo16),
                  pl.BlockSpec((tb, D_LRU), blk), pl.BlockSpec((SUBLANES, D_LRU), halo8),
                  pl.BlockSpec((tb, D_MIX), blk),
                  pl.BlockSpec((None, 2 * SUBLANES, D_LRU), lambda t: (layer, 0, 0)),
                  pl.BlockSpec((None, 8, 256, 256), lambda t: (layer, 0, 0, 0))],
        out_specs=[pl.BlockSpec((tb, D_IN), blk),
                   pl.BlockSpec((2 * SUBLANES, D_LRU), lambda t: (0, 0)),
                   pl.BlockSpec((8, 256, 256), lambda t: (0, 0, 0))],
        out_shape=[jax.ShapeDtypeStruct((t_len, D_IN), BF16),
                   jax.ShapeDtypeStruct((2 * SUBLANES, D_LRU), F32),
                   jax.ShapeDtypeStruct((8, 256, 256), F32)],
        scratch_shapes=[pltpu.VMEM((tb, D_LRU), F32),
                        pltpu.VMEM((tb, D_LRU), F32), pltpu.VMEM((tb, D_LRU), F32),
                        pltpu.VMEM((tb, D_LRU), BF16), pltpu.VMEM((tb, D_LRU), BF16),
                        pltpu.VMEM((tb + SUBLANES, D_LRU), F32), pltpu.VMEM((tb + SUBLANES, D_LRU), F32),
                        pltpu.VMEM((3, SUBLANES, D_LRU), F32),
                        pltpu.VMEM((SUBLANES, D_LRU), F32), pltpu.VMEM((SUBLANES, D_LRU), F32),
                        pltpu.VMEM((SUBLANES, D_SC), F32)],
        args=(z, z, h, h, dy, prm, gates), sem=("arbitrary",), comm=comm)


def _ffn_act(u, fw, layer, name, tb=512, tn=1024, rc=64, comm=None):
    t_len = u.shape[1]
    hb = tb // PACKED

    def body(u_ref, uh_ref, fw_ref, o_ref, ext):
        keep = jnp.where(pl.program_id(0) == 0, 0.0, 1.0)
        ext[:, 0:SUBLANES, :] = uh_ref[...].astype(F32)[:, PACKED - SUBLANES:, :] * keep
        ext[:, SUBLANES:, :] = u_ref[...].astype(F32)
        fw_v = fw_ref[...]

        for lb in range(tn // LANES):
            lanes = slice(lb * LANES, (lb + 1) * LANES)
            wg = [fw_v[0, k:k + 1, lanes] for k in range(3)]
            wu = [fw_v[1, k:k + 1, lanes] for k in range(3)]

            def chunk(ci, c, lanes=lanes, wg=wg, wu=wu):
                o = pl.multiple_of(ci * rc, rc)
                win = pl.ds(o, rc + SUBLANES)
                gate = _conv_taps(ext[0, win, lanes], wg, rc)
                up = _conv_taps(ext[1, win, lanes], wu, rc)
                o_ref[pl.ds(o, rc), lanes] = (_gelu(gate) * up).astype(BF16)
                return c

            lax.fori_loop(0, tb // rc, chunk, 0)

    return _pallas(
        body, name=name, grid=(t_len // tb, D_FF // tn),
        in_specs=[pl.BlockSpec((2, tb, tn), lambda i, j: (0, i, j)),
                  pl.BlockSpec((2, PACKED, tn), lambda i, j: (0, jnp.maximum(i * hb - 1, 0), j)),
                  pl.BlockSpec((None, 2, SUBLANES, tn), lambda i, j: (layer, 0, 0, j))],
        out_specs=[pl.BlockSpec((tb, tn), lambda i, j: (i, j))],
        out_shape=[jax.ShapeDtypeStruct((t_len, D_FF), BF16)],
        scratch_shapes=[pltpu.VMEM((2, tb + SUBLANES, tn), F32)],
        args=(u, u, fw), sem=("parallel", "parallel"), comm=comm)


def _ffn_bwd(dact, u, fw, layer, name, tb=512, tn=1024, rc=32, comm=None):
    t_len = u.shape[1]
    ni = t_len // tb
    hb = tb // PACKED
    last_halo = t_len // PACKED - 1

    def body(d_ref, dn_ref, u_ref, up_ref, un_ref, fw_ref, du_ref, dfw_ref, extu, extd, acc):
        i = pl.program_id(1)

        @pl.when(i == 0)
        def _():
            acc[...] = jnp.zeros_like(acc)

        keep_prev = jnp.where(i == 0, 0.0, 1.0)
        keep_next = jnp.where(i == ni - 1, 0.0, 1.0)
        extu[:, 0:SUBLANES, :] = up_ref[...].astype(F32)[:, PACKED - SUBLANES:, :] * keep_prev
        extu[:, SUBLANES:SUBLANES + tb, :] = u_ref[...].astype(F32)
        extu[:, SUBLANES + tb:, :] = un_ref[...].astype(F32)[:, 0:SUBLANES, :]
        extd[0:tb, :] = d_ref[...].astype(F32)
        extd[tb:, :] = dn_ref[...].astype(F32)[0:SUBLANES] * keep_next
        fw_v = fw_ref[...]
        m = rc + SUBLANES

        for lb in range(tn // LANES):
            lanes = slice(lb * LANES, (lb + 1) * LANES)
            taps = [[fw_v[pln, k:k + 1, lanes] for k in range(3)] for pln in range(2)]

            def chunk(ci, c, lanes=lanes, taps=taps):
                o = pl.multiple_of(ci * rc, rc)
                win = pl.ds(o, rc + 2 * SUBLANES)
                sh = []
                for pln in range(2):
                    e = extu[pln, win, lanes]
                    sh.append([pltpu.roll(e, 2, axis=0)[SUBLANES:], pltpu.roll(e, 1, axis=0)[SUBLANES:],
                               e[SUBLANES:]])
                gate = sum(taps[0][k] * sh[0][k] for k in range(3))
                up = sum(taps[1][k] * sh[1][k] for k in range(3))
                dv = extd[pl.ds(o, m), lanes]
                gel, dgel = _gelu_parts(gate)
                dpost = [dv * up * dgel, dv * gel]
                for pln in range(2):
                    du_ref[pln, pl.ds(o, rc), lanes] = _conv_taps_t(dpost[pln], taps[pln], rc).astype(BF16)
                    for k in range(3):
                        prod = dpost[pln][0:rc] * sh[pln][k][0:rc]
                        acc[3 * pln + k, :, lanes] += sum(
                            prod[s:s + SUBLANES] for s in range(0, rc, SUBLANES))
                return c

            lax.fori_loop(0, tb // rc, chunk, 0)

        @pl.when(i == ni - 1)
        def _():
            dfw_ref[...] = jnp.zeros_like(dfw_ref)
            for pln in range(2):
                for k in range(3):
                    dfw_ref[pln, k:k + 1, :] = jnp.sum(acc[3 * pln + k], axis=0, keepdims=True)

    return _pallas(
        body, name=name, grid=(D_FF // tn, ni),
        in_specs=[pl.BlockSpec((tb, tn), lambda j, i: (i, j)),
                  pl.BlockSpec((PACKED, tn), lambda j, i: (jnp.minimum((i + 1) * hb, last_halo), j)),
                  pl.BlockSpec((2, tb, tn), lambda j, i: (0, i, j)),
                  pl.BlockSpec((2, PACKED, tn), lambda j, i: (0, jnp.maximum(i * hb - 1, 0), j)),
                  pl.BlockSpec((2, PACKED, tn), lambda j, i: (0, jnp.minimum((i + 1) * hb, last_halo), j)),
                  pl.BlockSpec((None, 2, SUBLANES, tn), lambda j, i: (layer, 0, 0, j))],
        out_specs=[pl.BlockSpec((2, tb, tn), lambda j, i: (0, i, j)),
                   pl.BlockSpec((2, SUBLANES, tn), lambda j, i: (0, 0, j))],
        out_shape=[jax.ShapeDtypeStruct((2, t_len, D_FF), BF16),
                   jax.ShapeDtypeStruct((2, SUBLANES, D_FF), F32)],
        scratch_shapes=[pltpu.VMEM((2, tb + 2 * SUBLANES, tn), F32),
                        pltpu.VMEM((tb + SUBLANES, tn), F32),
                        pltpu.VMEM((6, SUBLANES, tn), F32)],
        args=(dact, dact, u, u, u, fw), sem=("parallel", "arbitrary"), comm=comm)


def _loss_head(x, g, target, name, tb=256):
    t_len, d = x.shape

    def body(x_ref, g_ref, t_ref, dx_ref, dxb_ref, dg_ref, loss_ref):
        @pl.when(pl.program_id(0) == 0)
        def _():
            dg_ref[...] = jnp.zeros_like(dg_ref)
            loss_ref[...] = jnp.zeros_like(loss_ref)

        xv = x_ref[...]
        gv = g_ref[...]
        r = lax.rsqrt(jnp.mean(xv * xv, axis=-1, keepdims=True) + EPS)
        xh = xv * r
        err = xh * gv - t_ref[...]
        loss_ref[...] += (0.5 / d) * jnp.sum(jnp.sum(err * err, axis=-1, keepdims=True), axis=0, keepdims=True)
        dy = err * (1.0 / d)
        dyg = dy * gv
        dx = r * (dyg - xh * jnp.mean(dyg * xh, axis=-1, keepdims=True))
        dx_ref[...] = dx
        dxb_ref[...] = dx.astype(BF16)
        dg_ref[0:1, :] += jnp.sum(dy * xh, axis=0, keepdims=True)

    return _pallas(
        body, name=name, grid=(t_len // tb,),
        in_specs=[pl.BlockSpec((tb, d), lambda i: (i, 0)), pl.BlockSpec((1, d), lambda i: (0, 0)),
                  pl.BlockSpec((tb, d), lambda i: (i, 0))],
        out_specs=[pl.BlockSpec((tb, d), lambda i: (i, 0)), pl.BlockSpec((tb, d), lambda i: (i, 0)),
                   pl.BlockSpec((SUBLANES, d), lambda i: (0, 0)),
                   pl.BlockSpec((SUBLANES, LANES), lambda i: (0, 0))],
        out_shape=[jax.ShapeDtypeStruct((t_len, d), F32), jax.ShapeDtypeStruct((t_len, d), BF16),
                   jax.ShapeDtypeStruct((SUBLANES, d), F32), jax.ShapeDtypeStruct((SUBLANES, LANES), F32)],
        args=(x, g, target), sem=("arbitrary",))[0]


def _adamw(w, g, m, v, name, emit_grad=False, comm=None):
    r, c = w.shape
    tr = 256 if r % 256 == 0 else r
    c1 = 1.0 / (1.0 - ADAM_B1 ** ADAM_STEP)
    c2 = 1.0 / (1.0 - ADAM_B2 ** ADAM_STEP)

    def body(w_ref, g_ref, m_ref, v_ref, d_ref, mo_ref, vo_ref, *go_ref):
        gv = g_ref[...]
        mn = ADAM_B1 * m_ref[...] + (1.0 - ADAM_B1) * gv
        vn = ADAM_B2 * v_ref[...] + (1.0 - ADAM_B2) * (gv * gv)
        d_ref[...] = -ADAM_LR * ((mn * c1) / (jnp.sqrt(vn * c2) + ADAM_EPS) + ADAM_WD * w_ref[...])
        mo_ref[...] = mn
        vo_ref[...] = vn
        if emit_grad:
            go_ref[0][...] = gv

    spec = pl.BlockSpec((tr, c), lambda i: (i, 0))
    shape = jax.ShapeDtypeStruct((r, c), F32)
    n_out = 4 if emit_grad else 3
    return _pallas(
        body, name=name, grid=(r // tr,),
        in_specs=[spec] * 4, out_specs=[spec] * n_out, out_shape=[shape] * n_out,
        args=(w, g, m, v), sem=("parallel",), comm=comm)


def _place():
    x, y, c = lax.axis_index("x"), lax.axis_index("y"), lax.axis_index("c")
    chips = [(1 - x, y), (x, 1 - y), (1 - x, 1 - y)]
    return x, y, c, chips


def _remote(src, dst, send, recv, sem, to):
    return pltpu.make_async_remote_copy(
        src_ref=src, dst_ref=dst, send_sem=send.at[sem], recv_sem=recv.at[sem], device_id=to, device_id_type=MESH)


def _gather_plan(fulls, kinds, mid_at=None, parts=None):
    parts = parts or [(0, 1)] * len(fulls)

    def region(it, f, k, cc):
        kind = kinds[it]
        p, n = parts[it][0:2]
        count = parts[it][2] if len(parts[it]) > 2 else 1
        if kind == SMALL:
            return f.at[k, pl.ds(cc * (CONV_PACK_ROWS // 2), CONV_PACK_ROWS // 2), :]
        if COL_SHARDED[kind]:
            rows, cols = f.shape[0] // (2 * n), f.shape[1] // N_CHIP
            return f.at[pl.ds((cc * n + p) * rows, count * rows), pl.ds(k * cols, cols)]
        assert n == 1
        rows = f.shape[0] // N_CHIP
        return f.at[pl.ds(k * rows + cc * (rows // 2), rows // 2), :]

    def first_hop(bufs, send, recv, it, j):
        x, y, c, chips = _place()
        reg = region(it, bufs[it], 2 * x + y, c)
        return _remote(reg, reg, send, recv, it * 6 + j, (*chips[j], c))

    def arrival(bufs, send, recv, it, j, second):
        x, y, c, chips = _place()
        px, py = chips[j]
        reg = region(it, bufs[it], 2 * px + py, 1 - c if second else c)
        to = (x, y, 1 - c) if second else (px, py, c)
        return _remote(reg, reg, send, recv, it * 6 + (3 + j if second else j), to)

    def forward(bufs, send, recv, it, j):
        x, y, c, chips = _place()
        px, py = chips[j]
        reg = region(it, bufs[it], 2 * px + py, c)
        return _remote(reg, reg, send, recv, it * 6 + 3 + j, (x, y, 1 - c))

    def start(srcs, bufs, outs, send, recv):
        for it in range(len(bufs)):
            for j in range(3):
                first_hop(bufs, send, recv, it, j).start()

    def mid(srcs, bufs, outs, send, recv):
        for it in range(len(bufs)):
            for j in range(3):
                arrival(bufs, send, recv, it, j, False).wait_recv()
                forward(bufs, send, recv, it, j).start()

    def finish(srcs, bufs, outs, send, recv):
        for it in range(len(bufs)):
            for j in range(3):
                arrival(bufs, send, recv, it, j, True).wait_recv()
        for it in range(len(bufs)):
            for j in range(3):
                first_hop(bufs, send, recv, it, j).wait_send()
                forward(bufs, send, recv, it, j).wait_send()

    return Comm(srcs=(), bufs=tuple(fulls), outs=(), n_sem=6 * len(fulls), start=start, mid=mid, finish=finish,
                mid_at=mid_at)


def _half_axis(kind):
    return 0 if kind == SMALL or COL_SHARDED[kind] else 1


def _half2(ref, kind, cc):
    if _half_axis(kind) == 0:
        return ref.at[pl.ds(cc * (ref.shape[0] // 2), ref.shape[0] // 2), :]
    return ref.at[:, pl.ds(cc * (ref.shape[1] // 2), ref.shape[1] // 2)]


def _pair_plan(grads, kinds):
    def land_shape(g, kind):
        s = list(g.shape)
        s[_half_axis(kind)] //= 2
        return jax.ShapeDtypeStruct(tuple(s), F32)

    def copy(srcs, outs, send, recv, it):
        x, y, c, _ = _place()
        return _remote(_half2(srcs[it], kinds[it], 1 - c), outs[it], send, recv, it, (x, y, 1 - c))

    def start(srcs, bufs, outs, send, recv):
        for it in range(len(srcs)):
            copy(srcs, outs, send, recv, it).start()

    def finish(srcs, bufs, outs, send, recv):
        for it in range(len(srcs)):
            copy(srcs, outs, send, recv, it).wait_send()
        for it in range(len(srcs)):
            copy(srcs, outs, send, recv, it).wait_recv()

    return Comm(srcs=tuple(grads), bufs=(), outs=tuple(land_shape(g, k) for g, k in zip(grads, kinds)),
                n_sem=len(grads), start=start, finish=finish)


def _scatter_plan(parts, slots, kinds):
    def piece(s, kind, k):
        if kind == SMALL:
            return s
        if COL_SHARDED[kind]:
            n = s.shape[1] // N_CHIP
            return s.at[:, pl.ds(k * n, n)]
        n = s.shape[0] // N_CHIP
        return s.at[pl.ds(k * n, n), :]

    def outbound(srcs, bufs, send, recv, it, j):
        x, y, c, chips = _place()
        px, py = chips[j]
        return _remote(piece(srcs[it], kinds[it], 2 * px + py), bufs[it].at[2 * x + y], send, recv, it * 3 + j,
                       (px, py, c))

    def inbound(bufs, send, recv, it, j):
        x, y, c, chips = _place()
        px, py = chips[j]
        got = bufs[it].at[2 * px + py]
        return _remote(got, got, send, recv, it * 3 + j, (px, py, c))

    def start(srcs, bufs, outs, send, recv):
        for it in range(len(srcs)):
            for j in range(3):
                outbound(srcs, bufs, send, recv, it, j).start()

    def finish(srcs, bufs, outs, send, recv):
        for it in range(len(srcs)):
            for j in range(3):
                inbound(bufs, send, recv, it, j).wait_recv()
        for it in range(len(srcs)):
            for j in range(3):
                outbound(srcs, bufs, send, recv, it, j).wait_send()

    return Comm(srcs=tuple(parts), bufs=tuple(slots), outs=(), n_sem=3 * len(parts), start=start, finish=finish)


def _share_plan(fulls, kinds, layer):
    def half(f, kind, cc):
        return _half2(f if kind == SMALL else f.at[layer], kind, cc)

    def copy(bufs, send, recv, it, cc):
        x, y, c, _ = _place()
        reg = half(bufs[it], kinds[it], c if cc == "mine" else 1 - c)
        return _remote(reg, reg, send, recv, it, (x, y, 1 - c))

    def start(srcs, bufs, outs, send, recv):
        for it in range(len(bufs)):
            copy(bufs, send, recv, it, "mine").start()

    def finish(srcs, bufs, outs, send, recv):
        for it in range(len(bufs)):
            copy(bufs, send, recv, it, "other").wait_recv()
        for it in range(len(bufs)):
            copy(bufs, send, recv, it, "mine").wait_send()

    return Comm(srcs=(), bufs=tuple(fulls), outs=(), n_sem=len(fulls), start=start, finish=finish)


def _pair_sum(g, land, idx, kind, name):
    odt = F32 if kind == SMALL else BF16
    r, cdim = land.shape

    def body(idx_ref, g_ref, l_ref, p_ref, s_ref):
        v = (g_ref[...] + l_ref[...]).astype(odt)
        p_ref[...] = v
        if kind == SMALL:
            s_ref[...] = v
        else:
            @pl.when(pl.program_id(1 if COL_SHARDED[kind] else 0) == idx_ref[1])
            def _():
                s_ref[...] = v

    if kind == SMALL:
        grid = (1,)
        g_spec = pl.BlockSpec((r, LANES), lambda i, idx_ref: (idx_ref[0], 0))
        spec = pl.BlockSpec((r, LANES), lambda i, idx_ref: (0, 0))
        s_spec = pl.BlockSpec((None, r, LANES), lambda i, idx_ref: (idx_ref[1], 0, 0))
        s_shape = (N_CHIP, r, LANES)
    elif COL_SHARDED[kind]:
        pc, tr = cdim // N_CHIP, 256
        nrb = r // tr
        grid = (nrb, N_CHIP)
        g_spec = pl.BlockSpec((tr, pc), lambda i, k, idx_ref: (idx_ref[0] * nrb + i, k))
        spec = pl.BlockSpec((tr, pc), lambda i, k, idx_ref: (i, k))
        s_spec = pl.BlockSpec((None, tr, pc), lambda i, k, idx_ref: (idx_ref[1], i, 0))
        s_shape = (N_CHIP, r, pc)
    else:
        pr = r // N_CHIP
        grid = (N_CHIP,)
        g_spec = pl.BlockSpec((pr, cdim), lambda k, idx_ref: (k, idx_ref[0]))
        spec = pl.BlockSpec((pr, cdim), lambda k, idx_ref: (k, 0))
        s_spec = pl.BlockSpec((None, pr, cdim), lambda k, idx_ref: (idx_ref[1], 0, 0))
        s_shape = (N_CHIP, pr, cdim)
    return pl.pallas_call(
        body, name=name,
        grid_spec=pltpu.PrefetchScalarGridSpec(
            num_scalar_prefetch=1, grid=grid, in_specs=[g_spec, spec], out_specs=[spec, s_spec]),
        out_shape=[jax.ShapeDtypeStruct(land.shape, odt), jax.ShapeDtypeStruct(s_shape, odt)],
        compiler_params=_cp(*(["arbitrary"] * len(grid))),
    )(idx, g, land)


def _sum_slots(slots, idx, kind, layer, prev, name):
    _, r, cdim = slots.shape

    def body(*refs):
        s_ref, o_ref = refs[1], refs[-1]
        v = s_ref[...].astype(F32)
        o_ref[...] = (v[0] + v[1]) + (v[2] + v[3])

    if kind == SMALL:
        grid = (1,)
        s_spec = pl.BlockSpec((N_CHIP, r, cdim), lambda i, idx_ref: (0, 0, 0))
        o_spec = pl.BlockSpec((r, cdim), lambda i, idx_ref: (idx_ref[0], 0))
        full = (2 * r, cdim)
    else:
        tr = 256 if r % 256 == 0 else 384
        nrb = r // tr
        grid = (nrb,)
        s_spec = pl.BlockSpec((N_CHIP, tr, cdim), lambda i, idx_ref: (0, i, 0))
        if COL_SHARDED[kind]:
            o_spec = pl.BlockSpec((None, tr, cdim), lambda i, idx_ref: (layer, idx_ref[0] * nrb + i, 0))
            full = (2, 2 * r, cdim)
        else:
            o_spec = pl.BlockSpec((None, tr, cdim), lambda i, idx_ref: (layer, i, idx_ref[0]))
            full = (2, r, 2 * cdim)
    in_specs, args, aliases = [s_spec], [idx, slots], {}
    if prev is not None:
        in_specs.append(ANY)
        args.append(prev)
        aliases = {2: 0}
    return pl.pallas_call(
        body, name=name,
        grid_spec=pltpu.PrefetchScalarGridSpec(
            num_scalar_prefetch=1, grid=grid, in_specs=in_specs, out_specs=o_spec),
        out_shape=jax.ShapeDtypeStruct(full, F32),
        input_output_aliases=aliases,
        compiler_params=_cp(*(["parallel"] * len(grid))),
    )(*args)


def _block_diag(w):
    w4 = w.reshape(2, 4, 4, 64, 64)
    eye = jnp.eye(4, dtype=w.dtype)[None, None, :, None, :, None]
    return (w4[:, :, :, :, None, :] * eye).reshape(2, 4, 256, 256)


def _block_diag_extract(d):
    d5 = d.reshape(4, 4, 64, 4, 64)
    return jnp.stack([d5[:, hh, :, hh, :] for hh in range(4)], axis=1).reshape(-1)


REP_NAMES = ("norm1_g", "lru_conv_b", "lru_ba", "lru_bx", "lru_lambda", "norm2_g", "lru_wa", "lru_wx")


def _pack_rep(norm1_g, conv_b, ba, bx, lam, norm2_g, wa, wx, final_g):
    parts = [a.reshape(-1) for a in (norm1_g, conv_b, ba, bx, lam, norm2_g, wa, wx, final_g)]
    return jnp.concatenate(parts).reshape(REP_ROWS, LANES)


def _unpack_rep(buf):
    flat = buf.reshape(-1)
    res, o = {}, 0
    for k in REP_NAMES:
        shape = (2, 16, 64, 64) if k in ("lru_wa", "lru_wx") else (2, 1024)
        n = math.prod(shape)
        res[k] = flat[o:o + n].reshape(shape)
        o += n
    res["final_g"] = flat[o:o + 1024]
    return res


def _pack_conv_shard(lru_cw, sc_cw, ffn_cw):
    return jnp.concatenate([lru_cw.reshape(16, LANES), jnp.pad(sc_cw.reshape(6, LANES), ((0, 2), (0, 0))),
                            ffn_cw.reshape(72, LANES)], axis=0)


def _unpack_conv_shard(buf):
    return (buf[0:16].reshape(2, 4, 256), buf[16:22].reshape(2, 3, 128), buf[24:96].reshape(2, 3, 1536))


def kernel(x, norm1_g, w_in, lru_conv_w, lru_conv_b, lru_wa, lru_ba, lru_wx, lru_bx, lru_lambda, sc_conv_w, w_out, norm2_g, w_up, ffn_conv_w, w_down, final_g, loss_target, m_norm1_g, m_w_in, m_lru_conv_w, m_lru_conv_b, m_lru_wa, m_lru_ba, m_lru_wx, m_lru_bx, m_lru_lambda, m_sc_conv_w, m_w_out, m_norm2_g, m_w_up, m_ffn_conv_w, m_w_down, m_final_g, v_norm1_g, v_w_in, v_lru_conv_w, v_lru_conv_b, v_lru_wa, v_lru_ba, v_lru_wx, v_lru_bx, v_lru_lambda, v_sc_conv_w, v_w_out, v_norm2_g, v_w_up, v_ffn_conv_w, v_w_down, v_final_g):
    me = 2 * lax.axis_index("x") + lax.axis_index("y")
    idx = jnp.stack([lax.axis_index("c"), me]).astype(jnp.int32)
    t_len = x.shape[1]

    s_conv = _pack_conv_shard(lru_conv_w, sc_conv_w, ffn_conv_w)
    conv_slots = lax.dynamic_update_slice(jnp.zeros((N_CHIP, CONV_PACK_ROWS, LANES), F32), s_conv[None], (me, 0, 0))
    wi = list(_cast_into_full(w_in, W_IN, idx, "cast_w_in"))
    wo = list(_cast_into_full(w_out, W_OUT, idx, "cast_w_out"))
    wu = list(_cast_into_full(w_up, W_UP, idx, "cast_w_up"))
    wd = list(_cast_into_full(w_down, W_DOWN, idx, "cast_w_down"))
    wi[0], convs = _comm_call(_gather_plan([wi[0], conv_slots], [W_IN, SMALL]), "ag_first")
    per_chip = [_unpack_conv_shard(convs[k]) for k in range(N_CHIP)]
    lru_cw = jnp.concatenate([p[0] for p in per_chip], axis=-1)
    sc_cw = jnp.concatenate([p[1] for p in per_chip], axis=-1)
    ffn_cw = jnp.concatenate([p[2] for p in per_chip], axis=-1)

    prm = jnp.concatenate(
        [lru_cw, jnp.stack([lru_conv_b, lru_ba, lru_bx, lru_lambda], axis=1),
         jnp.pad(sc_cw, ((0, 0), (0, 0), (0, D_LRU - D_SC))), jnp.zeros((2, 5, D_LRU), F32)], axis=1)
    gates = jnp.concatenate([_block_diag(lru_wa), _block_diag(lru_wx)], axis=1).astype(BF16)
    fw8 = jnp.pad(ffn_cw.reshape(2, 3, 2, D_FF).transpose(0, 2, 1, 3), ((0, 0), (0, 0), (0, 5), (0, 0)))

    xs = x[0]
    saved = []
    n512, n256 = t_len // 512, t_len // 256
    whole, lower, upper = (0, 1), (0, 2), (1, 2)
    carried_by = {
        "fwd_in_0": ([(wu, 0, W_UP, (0, 4))], (max(n512 - 3, 0),)),
        "fwd_mixer_0": ([(wu, 0, W_UP, (1, 4, 2)), (wo, 0, W_OUT, whole)], (max(n256 - 3, 0),)),
        "fwd_out_0": ([(wu, 0, W_UP, (3, 4))], (max(n512 - 2, 0),)),
        "fwd_up_0": ([(wd, 0, W_DOWN, whole)], (max(n512 - 2, 0),)),
        "fwd_act_0": ([(wi, 1, W_IN, whole), (wo, 1, W_OUT, whole)], (max(n512 - 2, 0), 0)),
        "fwd_down_0": ([(wu, 1, W_UP, (0, 4))], (max(n512 - 3, 0),)),
        "fwd_in_1": ([(wu, 1, W_UP, (1, 4))], (max(n512 - 3, 0),)),
        "fwd_mixer_1": ([(wu, 1, W_UP, (2, 4, 2))], (max(n256 - 4, 0),)),
        "fwd_act_1": ([(wd, 1, W_DOWN, whole)], (max(n512 - 3, 0), 0)),
    }

    def carried(name):
        if name not in carried_by:
            return None, lambda got: None
        items, mid_at = carried_by[name]

        def store(got):
            for (lst, i, _, _), arr in zip(items, got):
                lst[i] = arr

        return _gather_plan([lst[i] for lst, i, _, _ in items], [k for _, _, k, _ in items], mid_at=mid_at,
                            parts=[p for _, _, _, p in items]), store

    for l in range(2):
        comm, store = carried(f"fwd_in_{l}")
        (z, h1), got = _norm_mm(xs, norm1_g[l][None], wi[l], f"fwd_in_{l}", comm=comm)
        store(got)
        comm, store = carried(f"fwd_mixer_{l}")
        (ymix, hst), got = _mixer_fwd(z, prm, gates, l, f"fwd_mixer_{l}", comm=comm)
        store(got)
        comm, store = carried(f"fwd_out_{l}")
        (x2,), got = _mm_res(ymix, wo[l], xs, f"fwd_out_{l}", comm=comm)
        store(got)
        comm, store = carried(f"fwd_up_{l}")
        (u, h2), got = _norm_mm(x2, norm2_g[l][None], wu[l], f"fwd_up_{l}", planes=True, comm=comm)
        store(got)
        comm, store = carried(f"fwd_act_{l}")
        (act,), got = _ffn_act(u, fw8, l, f"fwd_act_{l}", comm=comm)
        store(got)
        comm, store = carried(f"fwd_down_{l}")
        (x3,), got = _mm_res(act, wd[l], x2, f"fwd_down_{l}", comm=comm)
        store(got)
        saved.append((xs, h1, z, hst, ymix, x2, h2, u, act))
        xs = x3

    dx, dxb, dgf, loss_blk = _loss_head(xs, final_g[None], loss_target[0], "loss_head")

    kinds = [W_IN, W_OUT, W_UP, W_DOWN]
    grads = [None, None]
    small = [None, None]
    reduced = [None] * 4
    summed1 = [None] * 4
    parts = slots = None
    for l in (1, 0):
        x_in, h1, z, hst, ymix, x2, h2, u, act = saved[l]
        carry = l == 0
        comm = _pair_plan([grads[1][W_IN]], [W_IN]) if carry else None
        (g_down,), got = _mm_tn(act, dxb, f"bwd_wdown_{l}", tk=1536, tn=1024, comm=comm)
        if carry:
            summed1[W_IN] = _pair_sum(grads[1][W_IN], got[0], idx, W_IN, "rs_add1_0")
            parts, slots = [s[0] for s in summed1], [s[1] for s in summed1]
        (dact,), _ = _mm_nt(dxb, wd[l], f"bwd_dact_{l}")
        comm = _scatter_plan(parts, slots, kinds) if carry else None
        (du, dfw), got = _ffn_bwd(dact, u, fw8, l, f"bwd_act_{l}", comm=comm)
        if carry:
            reduced = [_sum_slots(got[w], idx, kinds[w], 1, None, f"rs_sum1_{w}") for w in range(4)]
        comm = _share_plan(reduced, kinds, 1) if carry else None
        (g_up,), got = _mm_tn(h2, du, f"bwd_wup_{l}", tk=1024, tn=1536, planes=True, comm=comm)
        if carry:
            reduced = list(got)
        comm = _pair_plan([g_up, g_down], [W_UP, W_DOWN]) if carry else None
        (dx2, dx2b, dg2), got = _mm_nt_normbwd(du, wu[l], x2, norm2_g[l][None], dx, f"bwd_up_{l}", planes=True,
                                               comm=comm)
        if carry:
            sum_up = _pair_sum(g_up, got[0], idx, W_UP, "rs_add0_2")
            sum_down = _pair_sum(g_down, got[1], idx, W_DOWN, "rs_add0_3")
        (g_out,), _ = _mm_tn(ymix, dx2b, f"bwd_wout_{l}", tk=1536, tn=1024)
        comm = _pair_plan([g_out], [W_OUT]) if carry else None
        (dymix,), got = _mm_nt(dx2b, wo[l], f"bwd_dymix_{l}", comm=comm)
        trio = (W_OUT, W_UP, W_DOWN)
        if carry:
            sum_out = _pair_sum(g_out, got[0], idx, W_OUT, "rs_add0_1")
            comm = _scatter_plan([sum_out[0], sum_up[0], sum_down[0]], [sum_out[1], sum_up[1], sum_down[1]], trio)
        else:
            comm = _pair_plan([g_out, g_up, g_down], trio)
        (dz, dprm, dgates), got = _mixer_bwd(z, hst, dymix, prm, gates, l, f"bwd_mixer_{l}", comm=comm)
        if carry:
            for w, s in zip(trio, got):
                reduced[w] = _sum_slots(s, idx, w, 0, reduced[w], f"rs_sum0_{w}")
        else:
            for w, g, land in zip(trio, (g_out, g_up, g_down), got):
                summed1[w] = _pair_sum(g, land, idx, w, f"rs_add1_{w}")
        comm = _share_plan([reduced[w] for w in trio], trio, 0) if carry else None
        (g_in,), got = _mm_tn(h1, dz, f"bwd_win_{l}", tk=1024, tn=1792, comm=comm)
        if carry:
            for w, full in zip(trio, got):
                reduced[w] = full
        (dx, dxb, dg1), _ = _mm_nt_normbwd(dz, wi[l], x_in, norm1_g[l][None], dx2, f"bwd_in_{l}")
        grads[l] = [g_in, g_out, g_up, g_down]
        rep = dict(zip(REP_NAMES, [dg1[0], dprm[4], dprm[5], dprm[6], dprm[7], dg2[0],
                                   _block_diag_extract(dgates[0:4]), _block_diag_extract(dgates[4:8])]))
        conv = [dprm[0:4].reshape(-1), jnp.pad(dprm[8:11, 0:D_SC].reshape(-1), (0, 512)),
                dfw[:, 0:3, :].transpose(1, 0, 2).reshape(-1)]
        small[l] = (rep, conv)
    grad_x = dx[None]
    g_small = jnp.concatenate(
        [small[l][0][k] for k in REP_NAMES for l in range(2)] + [dgf[0]] + small[0][1] + small[1][1]
        + [loss_blk.reshape(-1)]).reshape(SMALL_ROWS, LANES)

    def big(w, g, m, v, name):
        shape = w.shape
        two_d = lambda a: a.reshape(-1, shape[-1])
        outs, _ = _adamw(two_d(w), two_d(g), two_d(m), two_d(v), name, emit_grad=True)
        return [o.reshape(shape) for o in outs]

    land_in, land_small = _comm_call(_pair_plan([grads[0][W_IN], g_small], [W_IN, SMALL]), "rs_pair_last")
    sum_in = _pair_sum(grads[0][W_IN], land_in, idx, W_IN, "rs_add0_0")
    sum_small = _pair_sum(g_small, land_small, idx, SMALL, "rs_add0_4")
    slot_in, slot_small = _comm_call(
        _scatter_plan([sum_in[0], sum_small[0]], [sum_in[1], sum_small[1]], [W_IN, SMALL]), "rs_scatter_last")
    gw_in, gs = _comm_call(
        _share_plan([_sum_slots(slot_in, idx, W_IN, 0, reduced[W_IN], "rs_sum0_0"),
                     _sum_slots(slot_small, idx, SMALL, 0, None, "rs_sum0_4")], [W_IN, SMALL], 0), "rs_share0")
    upd = {"w_up": big(w_up, reduced[W_UP], m_w_up, v_w_up, "adamw_w_up"),
           "w_down": big(w_down, reduced[W_DOWN], m_w_down, v_w_down, "adamw_w_down"),
           "w_out": big(w_out, reduced[W_OUT], m_w_out, v_w_out, "adamw_w_out"),
           "w_in": big(w_in, gw_in, m_w_in, v_w_in, "adamw_w_in")}

    loss = gs[REP_ROWS + CONV_ROWS, 0]
    g_rep = gs[0:REP_ROWS]
    g_conv = gs[REP_ROWS:REP_ROWS + CONV_ROWS].reshape(2, CONV_LAYER)
    g_lru_cw = lax.dynamic_slice_in_dim(g_conv[:, 0:4096].reshape(2, 4, 1024), me * 256, 256, axis=2)
    g_sc_cw = lax.dynamic_slice_in_dim(g_conv[:, 4096:4096 + 1536].reshape(2, 3, 512), me * 128, 128, axis=2)
    g_ffn_cw = lax.dynamic_slice_in_dim(g_conv[:, 6144:].reshape(2, 3, 6144), me * 1536, 1536, axis=2)

    rep_out, _ = _adamw(
        _pack_rep(norm1_g, lru_conv_b, lru_ba, lru_bx, lru_lambda, norm2_g, lru_wa, lru_wx, final_g), g_rep,
        _pack_rep(m_norm1_g, m_lru_conv_b, m_lru_ba, m_lru_bx, m_lru_lambda, m_norm2_g, m_lru_wa, m_lru_wx, m_final_g),
        _pack_rep(v_norm1_g, v_lru_conv_b, v_lru_ba, v_lru_bx, v_lru_lambda, v_norm2_g, v_lru_wa, v_lru_wx, v_final_g),
        "adamw_rep")
    conv_out, _ = _adamw(s_conv, _pack_conv_shard(g_lru_cw, g_sc_cw, g_ffn_cw),
                         _pack_conv_shard(m_lru_conv_w, m_sc_conv_w, m_ffn_conv_w),
                         _pack_conv_shard(v_lru_conv_w, v_sc_conv_w, v_ffn_conv_w), "adamw_conv")

    names = ["norm1_g", "w_in", "lru_conv_w", "lru_conv_b", "lru_wa", "lru_ba", "lru_wx", "lru_bx", "lru_lambda",
             "sc_conv_w", "w_out", "norm2_g", "w_up", "ffn_conv_w", "w_down", "final_g"]
    groups = []
    g_all = dict(_unpack_rep(g_rep))
    g_all.update({k: v[3] for k, v in upd.items()})
    g_all.update(lru_conv_w=g_lru_cw, sc_conv_w=g_sc_cw, ffn_conv_w=g_ffn_cw)
    groups.append(g_all)
    for i in range(3):
        d = dict(_unpack_rep(rep_out[i]))
        cl, cs, cf = _unpack_conv_shard(conv_out[i])
        d.update(lru_conv_w=cl, sc_conv_w=cs, ffn_conv_w=cf)
        d.update({k: v[i] for k, v in upd.items()})
        groups.append(d)
    return (loss, grad_x, *[grp[n] for grp in groups for n in names])
```

```python
import dataclasses
import functools
import math
import operator
from typing import Any, Callable, Optional, Sequence

import jax
import jax.numpy as jnp
from jax import lax
from jax.experimental import pallas as pl
from jax.experimental.pallas import tpu as pltpu

F32 = jnp.float32
BF16 = jnp.bfloat16
MESH = pl.DeviceIdType.MESH

D_MODEL = 1024
D_LRU = 1024
D_SC = 512
D_MIX = D_LRU + D_SC
D_IN = 2 * D_LRU + 3 * D_SC
D_FF = 3072
N_CHIP = 4
RG_C = 8.0
EPS = 1e-6
ADAM_LR = 0.001
ADAM_B1 = 0.9
ADAM_B2 = 0.999
ADAM_EPS = 1e-08
ADAM_WD = 0.01
ADAM_STEP = 10

SUBLANES = 8
PACKED = 16
LANES = 128
VMEM_LIMIT = 56 * 1024 * 1024
GELU_C0 = math.sqrt(2.0 / math.pi)
GELU_C1 = 0.044715

REP_LAYER = 6 * 1024 + 2 * 16 * 64 * 64
REP_ROWS = (2 * REP_LAYER + 1024) // LANES
CONV_LAYER = 4 * 1024 + 2048 + 3 * 6144
CONV_ROWS = 2 * CONV_LAYER // LANES
SMALL_ROWS = REP_ROWS + CONV_ROWS + 8
CONV_PACK_ROWS = 96

W_IN, W_OUT, W_UP, W_DOWN, SMALL = range(5)
COL_SHARDED = {W_IN: True, W_OUT: False, W_UP: True, W_DOWN: False}

ONCE = pl.Buffered(1)
ANY = pl.BlockSpec(memory_space=pl.ANY)


def _cp(*sem):
    return pltpu.CompilerParams(dimension_semantics=sem, vmem_limit_bytes=VMEM_LIMIT)


@dataclasses.dataclass
class Comm:
    srcs: Sequence[Any]
    bufs: Sequence[Any]
    outs: Sequence[Any]
    n_sem: int
    start: Callable
    finish: Callable
    mid: Optional[Callable] = None
    mid_at: Optional[Sequence[int]] = None


def _pallas(body, *, name, grid, in_specs, out_specs, out_shape, args, sem, scratch_shapes=(), comm=None):
    if comm is None:
        res = pl.pallas_call(
            body, name=name, grid=grid, in_specs=list(in_specs), out_specs=list(out_specs),
            out_shape=list(out_shape), scratch_shapes=list(scratch_shapes), compiler_params=_cp(*sem))(*args)
        return tuple(res), ()
    n_in, n_out, n_scr = len(in_specs), len(out_specs), len(scratch_shapes)
    ns, nb, no = len(comm.srcs), len(comm.bufs), len(comm.outs)

    def carrier(*refs):
        p = 0
        main_in = refs[p:p + n_in]
        p += n_in
        c_src = refs[p:p + ns]
        p += ns + nb
        main_out = refs[p:p + n_out]
        p += n_out
        c_buf = refs[p:p + nb]
        p += nb
        c_out = refs[p:p + no]
        p += no
        scr = refs[p:p + n_scr]
        send, recv = refs[p + n_scr], refs[p + n_scr + 1]
        ids = [pl.program_id(a) for a in range(len(grid))]

        def at(steps):
            return functools.reduce(operator.and_, [i == s for i, s in zip(ids, steps)])

        @pl.when(at([0] * len(grid)))
        def _():
            comm.start(c_src, c_buf, c_out, send, recv)

        if comm.mid is not None:
            @pl.when(at(comm.mid_at))
            def _():
                comm.mid(c_src, c_buf, c_out, send, recv)

        body(*main_in, *main_out, *scr)

        @pl.when(at([g - 1 for g in grid]))
        def _():
            comm.finish(c_src, c_buf, c_out, send, recv)

    res = pl.pallas_call(
        carrier, name=name, grid=grid,
        in_specs=list(in_specs) + [ANY] * (ns + nb),
        out_specs=list(out_specs) + [ANY] * (nb + no),
        out_shape=list(out_shape) + [jax.ShapeDtypeStruct(b.shape, b.dtype) for b in comm.bufs] + list(comm.outs),
        input_output_aliases={n_in + ns + j: n_out + j for j in range(nb)},
        scratch_shapes=list(scratch_shapes) + [pltpu.SemaphoreType.DMA((comm.n_sem,)),
                                               pltpu.SemaphoreType.DMA((comm.n_sem,))],
        compiler_params=_cp(*(["arbitrary"] * len(grid))),
    )(*args, *comm.srcs, *comm.bufs)
    return tuple(res[:n_out]), tuple(res[n_out:])


def _comm_call(comm, name):
    ns, nb, no = len(comm.srcs), len(comm.bufs), len(comm.outs)

    def body(*refs):
        c_src = refs[0:ns]
        c_buf = refs[ns + nb:ns + 2 * nb]
        c_out = refs[ns + 2 * nb:ns + 2 * nb + no]
        send, recv = refs[ns + 2 * nb + no], refs[ns + 2 * nb + no + 1]
        comm.start(c_src, c_buf, c_out, send, recv)
        if comm.mid is not None:
            comm.mid(c_src, c_buf, c_out, send, recv)
        comm.finish(c_src, c_buf, c_out, send, recv)

    return tuple(pl.pallas_call(
        body, name=name,
        in_specs=[ANY] * (ns + nb), out_specs=[ANY] * (nb + no),
        out_shape=[jax.ShapeDtypeStruct(b.shape, b.dtype) for b in comm.bufs] + list(comm.outs),
        input_output_aliases={ns + j: j for j in range(nb)},
        scratch_shapes=[pltpu.SemaphoreType.DMA((comm.n_sem,)), pltpu.SemaphoreType.DMA((comm.n_sem,))],
    )(*comm.srcs, *comm.bufs))


def _sigmoid(v):
    return 1.0 / (1.0 + jnp.exp(-v))


def _sigmoid_tanh(v):
    return 0.5 + 0.5 * jnp.tanh(0.5 * v)


def _gelu_parts(v):
    v2 = v * v
    t = jnp.tanh(GELU_C0 * v * (1.0 + GELU_C1 * v2))
    half = 0.5 * (1.0 + t)
    gel = v * half
    dgel = half + 0.5 * v * (1.0 - t * t) * (GELU_C0 * (1.0 + 3.0 * GELU_C1 * v2))
    return gel, dgel


def _gelu(v):
    t = jnp.tanh(GELU_C0 * v * (1.0 + GELU_C1 * (v * v)))
    return 0.5 * v * (1.0 + t)


def _neg_expm1(y, a):
    p = jnp.full_like(y, 1.0 / 120.0)
    for coef in (1.0 / 24.0, 1.0 / 6.0, 0.5, 1.0):
        p = p * y + coef
    return jnp.where(y > -0.1, -(p * y), 1.0 - a * a)


def _softplus_neg(lam):
    nl = -lam
    e = jnp.exp(-jnp.abs(nl))
    u = 1.0 + e
    l1p = jnp.where(u == 1.0, e, jnp.log(u) * e / (u - 1.0))
    return jnp.maximum(nl, 0.0) + l1p


def _conv_taps(ext, taps, n_out):
    kw = len(taps)
    acc = taps[kw - 1] * ext[SUBLANES:SUBLANES + n_out]
    for k in range(kw - 1):
        acc = acc + taps[k] * pltpu.roll(ext, kw - 1 - k, axis=0)[SUBLANES:SUBLANES + n_out]
    return acc


def _conv_taps_t(ext, taps, n_out):
    kw = len(taps)
    n = ext.shape[0]
    acc = taps[kw - 1] * ext[0:n_out]
    for k in range(kw - 1):
        acc = acc + taps[k] * pltpu.roll(ext, n - (kw - 1 - k), axis=0)[0:n_out]
    return acc


def _scan8(a, b, carry, row):
    for s in (1, 2, 4):
        m = row >= s
        a_sh = jnp.where(m, pltpu.roll(a, s, axis=0), 1.0)
        b_sh = jnp.where(m, pltpu.roll(b, s, axis=0), 0.0)
        b = a * b_sh + b
        a = a * a_sh
    return a * carry + b


def _scan8_rev(a, b, carry, row):
    for s in (1, 2, 4):
        m = row < SUBLANES - s
        a_sh = jnp.where(m, pltpu.roll(a, SUBLANES - s, axis=0), 1.0)
        b_sh = jnp.where(m, pltpu.roll(b, SUBLANES - s, axis=0), 0.0)
        b = a * b_sh + b
        a = a * a_sh
    return a * carry + b


def _cast_into_full(w, kind, idx, name):
    nl, r, c = w.shape
    tr = 256 if r % 256 == 0 else r
    nrb = r // tr

    def body(idx_ref, w_ref, o0_ref, o1_ref):
        o0_ref[...] = w_ref[0].astype(BF16)
        o1_ref[...] = w_ref[1].astype(BF16)

    if COL_SHARDED[kind]:
        full = (r, N_CHIP * c)
        o_spec = pl.BlockSpec((tr, c), lambda i, idx_ref: (i, idx_ref[1]))
    else:
        full = (N_CHIP * r, c)
        o_spec = pl.BlockSpec((tr, c), lambda i, idx_ref: (idx_ref[1] * nrb + i, 0))
    return pl.pallas_call(
        body, name=name,
        grid_spec=pltpu.PrefetchScalarGridSpec(
            num_scalar_prefetch=1, grid=(nrb,),
            in_specs=[pl.BlockSpec((nl, tr, c), lambda i, idx_ref: (0, i, 0))], out_specs=[o_spec, o_spec]),
        out_shape=[jax.ShapeDtypeStruct(full, BF16)] * 2,
        compiler_params=_cp("parallel"),
    )(idx, w)


def _norm_mm(x, g, w, name, planes=False, tm=512, tn=512, comm=None):
    t_len, d = x.shape
    n = w.shape[1]
    half = n // 2

    def body(x_ref, g_ref, w_ref, z_ref, h_ref):
        xv = x_ref[...]
        r = lax.rsqrt(jnp.mean(xv * xv, axis=-1, keepdims=True) + EPS)
        h_ref[...] = ((xv * r) * g_ref[...]).astype(BF16)
        for n0 in range(0, n, tn):
            blk = jnp.dot(h_ref[...], w_ref[:, n0:n0 + tn], preferred_element_type=F32).astype(BF16)
            if planes:
                z_ref[n0 // half, :, n0 % half:n0 % half + tn] = blk
            else:
                z_ref[:, n0:n0 + tn] = blk

    if planes:
        z_shape = jax.ShapeDtypeStruct((2, t_len, half), BF16)
        z_spec = pl.BlockSpec((2, tm, half), lambda i: (0, i, 0))
    else:
        z_shape = jax.ShapeDtypeStruct((t_len, n), BF16)
        z_spec = pl.BlockSpec((tm, n), lambda i: (i, 0))
    return _pallas(
        body, name=name, grid=(t_len // tm,),
        in_specs=[pl.BlockSpec((tm, d), lambda i: (i, 0)),
                  pl.BlockSpec((1, d), lambda i: (0, 0)),
                  pl.BlockSpec((d, n), lambda i: (0, 0), pipeline_mode=ONCE)],
        out_specs=[z_spec, pl.BlockSpec((tm, d), lambda i: (i, 0))],
        out_shape=[z_shape, jax.ShapeDtypeStruct((t_len, d), BF16)],
        args=(x, g, w), sem=("parallel",), comm=comm)


def _mm_res(a, w, res, name, tm=512, comm=None):
    t_len, k = a.shape
    n = w.shape[1]

    def body(a_ref, w_ref, r_ref, o_ref):
        o_ref[...] = r_ref[...] + jnp.dot(a_ref[...], w_ref[...], preferred_element_type=F32)

    return _pallas(
        body, name=name, grid=(t_len // tm,),
        in_specs=[pl.BlockSpec((tm, k), lambda i: (i, 0)),
                  pl.BlockSpec((k, n), lambda i: (0, 0), pipeline_mode=ONCE),
                  pl.BlockSpec((tm, n), lambda i: (i, 0))],
        out_specs=[pl.BlockSpec((tm, n), lambda i: (i, 0))],
        out_shape=[jax.ShapeDtypeStruct((t_len, n), F32)],
        args=(a, w, res), sem=("parallel",), comm=comm)


def _mm_nt(a, w, name, tm=512, comm=None):
    t_len, k = a.shape
    n = w.shape[0]

    def body(a_ref, w_ref, o_ref):
        o_ref[...] = lax.dot_general(a_ref[...], w_ref[...], (((1,), (1,)), ((), ())),
                                     preferred_element_type=F32).astype(BF16)

    return _pallas(
        body, name=name, grid=(t_len // tm,),
        in_specs=[pl.BlockSpec((tm, k), lambda i: (i, 0)),
                  pl.BlockSpec((n, k), lambda i: (0, 0), pipeline_mode=ONCE)],
        out_specs=[pl.BlockSpec((tm, n), lambda i: (i, 0))],
        out_shape=[jax.ShapeDtypeStruct((t_len, n), BF16)],
        args=(a, w), sem=("parallel",), comm=comm)


def _mm_nt_normbwd(dz, w, x, g, dres, name, planes=False, tm=512, comm=None):
    t_len, d = x.shape
    n = w.shape[1]
    half = n // 2
    nt_dims = (((1,), (1,)), ((), ()))

    def body(dz_ref, w_ref, x_ref, g_ref, r_ref, dx_ref, dxb_ref, dg_ref):
        @pl.when(pl.program_id(0) == 0)
        def _():
            dg_ref[...] = jnp.zeros_like(dg_ref)

        if planes:
            dh = (lax.dot_general(dz_ref[0], w_ref[:, 0:half], nt_dims, preferred_element_type=F32)
                  + lax.dot_general(dz_ref[1], w_ref[:, half:], nt_dims, preferred_element_type=F32))
        else:
            dh = lax.dot_general(dz_ref[...], w_ref[...], nt_dims, preferred_element_type=F32)
        xv = x_ref[...]
        r = lax.rsqrt(jnp.mean(xv * xv, axis=-1, keepdims=True) + EPS)
        xh = xv * r
        dhg = dh * g_ref[...]
        dx = r_ref[...] + r * (dhg - xh * jnp.mean(dhg * xh, axis=-1, keepdims=True))
        dx_ref[...] = dx
        dxb_ref[...] = dx.astype(BF16)
        dg_ref[0:1, :] += jnp.sum(dh * xh, axis=0, keepdims=True)

    if planes:
        dz_spec = pl.BlockSpec((2, tm, half), lambda i: (0, i, 0))
    else:
        dz_spec = pl.BlockSpec((tm, n), lambda i: (i, 0))
    return _pallas(
        body, name=name, grid=(t_len // tm,),
        in_specs=[dz_spec,
                  pl.BlockSpec((d, n), lambda i: (0, 0), pipeline_mode=ONCE),
                  pl.BlockSpec((tm, d), lambda i: (i, 0)),
                  pl.BlockSpec((1, d), lambda i: (0, 0)),
                  pl.BlockSpec((tm, d), lambda i: (i, 0))],
        out_specs=[pl.BlockSpec((tm, d), lambda i: (i, 0)),
                   pl.BlockSpec((tm, d), lambda i: (i, 0)),
                   pl.BlockSpec((SUBLANES, d), lambda i: (0, 0))],
        out_shape=[jax.ShapeDtypeStruct((t_len, d), F32),
                   jax.ShapeDtypeStruct((t_len, d), BF16),
                   jax.ShapeDtypeStruct((SUBLANES, d), F32)],
        args=(dz, w, x, g, dres), sem=("arbitrary",), comm=comm)


def _mm_tn(a, g, name, tk, tn, planes=False, tt=1024, comm=None):
    t_len, k = a.shape
    n = 2 * g.shape[2] if planes else g.shape[1]
    nn = n // tn
    half = nn // 2
    tt = min(tt, t_len)

    def body(a_ref, g_ref, o_ref):
        @pl.when(pl.program_id(2) == 0)
        def _():
            o_ref[...] = jnp.zeros_like(o_ref)

        o_ref[...] += lax.dot_general(a_ref[...], g_ref[...], (((0,), (0,)), ((), ())),
                                      preferred_element_type=F32)

    if planes:
        g_spec = pl.BlockSpec((None, tt, tn), lambda i, j, t: (j // half, t, j % half))
    else:
        g_spec = pl.BlockSpec((tt, tn), lambda i, j, t: (t, j))
    return _pallas(
        body, name=name, grid=(k // tk, nn, t_len // tt),
        in_specs=[pl.BlockSpec((tt, tk), lambda i, j, t: (t, i)), g_spec],
        out_specs=[pl.BlockSpec((tk, tn), lambda i, j, t: (i, j))],
        out_shape=[jax.ShapeDtypeStruct((k, n), F32)],
        args=(a, g), sem=("parallel", "parallel", "arbitrary"), comm=comm)


def _lru_gates(rp, ip, spn):
    r = _sigmoid(rp)
    i = _sigmoid_tanh(ip)
    la = r * spn
    a = jnp.exp(la)
    mult = jnp.sqrt(_neg_expm1(2.0 * la, a))
    return r, i, a, mult


def _mixer_fwd(z, prm, gates, layer, name, tb=256, comm=None):
    t_len = z.shape[0]

    def body(z_ref, p_ref, g_ref, y_ref, h_ref, xhalo, phalo, hcar, lx_s, rp_s, ip_s):
        @pl.when(pl.program_id(0) == 0)
        def _():
            xhalo[...] = jnp.zeros_like(xhalo)
            phalo[...] = jnp.zeros_like(phalo)
            hcar[...] = jnp.zeros_like(hcar)

        prm_v = p_ref[...]
        cw = prm_v[0:4]
        vec = prm_v[4:8]
        xp = z_ref[:, 0:D_LRU].astype(F32)
        ext = jnp.concatenate([xhalo[...], xp], axis=0)
        lx = vec[0:1] + _conv_taps(ext, [cw[k:k + 1] for k in range(4)], tb)
        xhalo[...] = xp[tb - SUBLANES:]
        lx_s[...] = lx
        lxb = lx.astype(BF16)
        for q in range(4):
            sl = slice(q * 256, (q + 1) * 256)
            rp_s[:, sl] = jnp.dot(lxb[:, sl], g_ref[q], preferred_element_type=F32) + vec[1:2, sl]
            ip_s[:, sl] = jnp.dot(lxb[:, sl], g_ref[4 + q], preferred_element_type=F32) + vec[2:3, sl]

        spn = jnp.broadcast_to(-RG_C * _softplus_neg(vec[3:4]), (SUBLANES, D_LRU))
        row = lax.broadcasted_iota(jnp.int32, (SUBLANES, D_LRU), 0)

        def step(ci, carry):
            o = pl.multiple_of(ci * PACKED, PACKED)
            gate = z_ref[pl.ds(o, PACKED), D_LRU:2 * D_LRU].astype(F32)
            ys = []
            for sub in range(2):
                rows = pl.ds(pl.multiple_of(o + sub * SUBLANES, SUBLANES), SUBLANES)
                lxv = lx_s[rows, :]
                _, i, a, mult = _lru_gates(rp_s[rows, :], ip_s[rows, :], spn)
                h = _scan8(a, mult * (i * lxv), carry, row)
                h_ref[rows, :] = h
                ys.append(h * _gelu(gate[sub * SUBLANES:(sub + 1) * SUBLANES]))
                carry = jnp.broadcast_to(h[SUBLANES - 1:SUBLANES, :], (SUBLANES, D_LRU))
            y_ref[pl.ds(o, PACKED), 0:D_LRU] = jnp.concatenate(ys, axis=0).astype(BF16)
            return carry

        hcar[...] = lax.fori_loop(0, tb // PACKED, step, hcar[...])

        scw = prm_v[8:11, 0:D_SC]
        o_b, o_c, o_x = 2 * D_LRU, 2 * D_LRU + D_SC, 2 * D_LRU + 2 * D_SC
        p = z_ref[:, o_c:o_x].astype(F32) * z_ref[:, o_x:].astype(F32)
        pext = jnp.concatenate([phalo[...], p], axis=0)
        q = _conv_taps(pext, [scw[k:k + 1] for k in range(3)], tb)
        phalo[...] = p[tb - SUBLANES:]
        y_ref[:, D_LRU:] = (z_ref[:, o_b:o_c].astype(F32) * q).astype(BF16)

    return _pallas(
        body, name=name, grid=(t_len // tb,),
        in_specs=[pl.BlockSpec((tb, D_IN), lambda t: (t, 0)),
                  pl.BlockSpec((None, 2 * SUBLANES, D_LRU), lambda t: (layer, 0, 0)),
                  pl.BlockSpec((None, 8, 256, 256), lambda t: (layer, 0, 0, 0))],
        out_specs=[pl.BlockSpec((tb, D_MIX), lambda t: (t, 0)),
                   pl.BlockSpec((tb, D_LRU), lambda t: (t, 0))],
        out_shape=[jax.ShapeDtypeStruct((t_len, D_MIX), BF16),
                   jax.ShapeDtypeStruct((t_len, D_LRU), F32)],
        scratch_shapes=[pltpu.VMEM((SUBLANES, D_LRU), F32), pltpu.VMEM((SUBLANES, D_SC), F32),
                        pltpu.VMEM((SUBLANES, D_LRU), F32), pltpu.VMEM((tb, D_LRU), F32),
                        pltpu.VMEM((tb, D_LRU), F32), pltpu.VMEM((tb, D_LRU), F32)],
        args=(z, prm, gates), sem=("arbitrary",), comm=comm)


def _mixer_bwd(z, h, dy, prm, gates, layer, name, tb=256, comm=None):
    t_len = z.shape[0]
    nb = t_len // tb

    def body(z_ref, zh_ref, h_ref, hh_ref, dy_ref, p_ref, g_ref, dz_ref, dp_ref, dg_ref,
             lx_s, rp_s, ip_s, drpb_s, dipb_s, dlx_s, hext_s, acc_s, acar, gcar, dqh):
        t = pl.program_id(0)
        first_block = t == nb - 1

        @pl.when(t == 0)
        def _():
            for ref in (dp_ref, dg_ref, acc_s, acar, gcar, dqh):
                ref[...] = jnp.zeros_like(ref)
            dlx_s[tb:, :] = jnp.zeros((SUBLANES, D_LRU), F32)

        prm_v = p_ref[...]
        cw = prm_v[0:4]
        vec = prm_v[4:8]
        scw = prm_v[8:11, 0:D_SC]
        wa_ref = [g_ref.at[q] for q in range(4)]
        wx_ref = [g_ref.at[4 + q] for q in range(4)]
        dwa_ref = [dg_ref.at[q] for q in range(4)]
        dwx_ref = [dg_ref.at[4 + q] for q in range(4)]
        ctaps = [cw[k:k + 1] for k in range(4)]
        staps = [scw[k:k + 1] for k in range(3)]
        keep = jnp.where(first_block, 0.0, 1.0)
        zh = zh_ref[...].astype(F32)[PACKED - SUBLANES:] * keep

        xp = z_ref[:, 0:D_LRU].astype(F32)
        xext = jnp.concatenate([zh[:, 0:D_LRU], xp], axis=0)
        lx = vec[0:1] + _conv_taps(xext, ctaps, tb)
        lx_s[...] = lx
        lxb = lx.astype(BF16)
        for q in range(4):
            sl = slice(q * 256, (q + 1) * 256)
            rp_s[:, sl] = jnp.dot(lxb[:, sl], wa_ref[q][...], preferred_element_type=F32) + vec[1:2, sl]
            ip_s[:, sl] = jnp.dot(lxb[:, sl], wx_ref[q][...], preferred_element_type=F32) + vec[2:3, sl]
        hext_s[0:SUBLANES, :] = hh_ref[...] * keep
        hext_s[SUBLANES:, :] = h_ref[...]

        spn = jnp.broadcast_to(-RG_C * _softplus_neg(vec[3:4]), (SUBLANES, D_LRU))
        row = lax.broadcasted_iota(jnp.int32, (SUBLANES, D_LRU), 0)

        def step(ci, carry):
            a_next, g_next = carry
            o = pl.multiple_of((tb // PACKED - 1 - ci) * PACKED, PACKED)
            rows16 = pl.ds(o, PACKED)
            gate16 = z_ref[rows16, D_LRU:2 * D_LRU].astype(F32)
            dyl16 = dy_ref[rows16, 0:D_LRU].astype(F32)
            dgs, drs, dis = [None, None], [None, None], [None, None]
            for sub in (1, 0):
                oo = pl.multiple_of(o + sub * SUBLANES, SUBLANES)
                rows = pl.ds(oo, SUBLANES)
                half = slice(sub * SUBLANES, (sub + 1) * SUBLANES)
                lxv = lx_s[rows, :]
                r, i, a, mult = _lru_gates(rp_s[rows, :], ip_s[rows, :], spn)
                hwin = hext_s[pl.ds(oo, 2 * SUBLANES), :]
                hv = hwin[SUBLANES:]
                hprev = pltpu.roll(hwin, 1, axis=0)[SUBLANES:]
                gel, dgel = _gelu_parts(gate16[half])
                dyl = dyl16[half]
                a_up = jnp.where(row < SUBLANES - 1, pltpu.roll(a, SUBLANES - 1, axis=0), a_next)
                gg = _scan8_rev(a_up, dyl * gel, g_next, row)
                dgs[sub] = dyl * hv * dgel
                ilx = i * lxv
                dla = gg * hprev * a - (gg * ilx) * (a * a) / mult
                dlx_s[rows, :] = gg * mult * i
                drp = dla * spn * r * (1.0 - r)
                dip = gg * mult * lxv * i * (1.0 - i)
                drs[sub] = drp
                dis[sub] = dip
                acc_s[0] += drp
                acc_s[1] += dip
                acc_s[2] += dla * r
                a_next = jnp.broadcast_to(a[0:1, :], (SUBLANES, D_LRU))
                g_next = jnp.broadcast_to(gg[0:1, :], (SUBLANES, D_LRU))
            dz_ref[rows16, D_LRU:2 * D_LRU] = jnp.concatenate(dgs, axis=0).astype(BF16)
            drpb_s[rows16, :] = jnp.concatenate(drs, axis=0).astype(BF16)
            dipb_s[rows16, :] = jnp.concatenate(dis, axis=0).astype(BF16)
            return a_next, g_next

        a_c, g_c = lax.fori_loop(0, tb // PACKED, step, (acar[...], gcar[...]))
        acar[...] = a_c
        gcar[...] = g_c

        drpb = drpb_s[...]
        dipb = dipb_s[...]
        nt_dims = (((1,), (1,)), ((), ()))
        tn_dims = (((0,), (0,)), ((), ()))
        for q in range(4):
            sl = slice(q * 256, (q + 1) * 256)
            dlx_s[0:tb, sl] += (
                lax.dot_general(drpb[:, sl], wa_ref[q][...], nt_dims, preferred_element_type=F32)
                + lax.dot_general(dipb[:, sl], wx_ref[q][...], nt_dims, preferred_element_type=F32))
            dwa_ref[q][...] += lax.dot_general(lxb[:, sl], drpb[:, sl], tn_dims, preferred_element_type=F32)
            dwx_ref[q][...] += lax.dot_general(lxb[:, sl], dipb[:, sl], tn_dims, preferred_element_type=F32)

        dlx_ext = dlx_s[...]
        dlx = dlx_ext[0:tb]
        dz_ref[:, 0:D_LRU] = _conv_taps_t(dlx_ext, ctaps, tb).astype(BF16)
        dp_ref[3:4, :] += jnp.sum(dlx * xp, axis=0, keepdims=True)
        for k in range(3):
            shifted = pltpu.roll(xext, 3 - k, axis=0)[SUBLANES:]
            dp_ref[k:k + 1, :] += jnp.sum(dlx * shifted, axis=0, keepdims=True)
        dp_ref[4:5, :] += jnp.sum(dlx, axis=0, keepdims=True)
        dlx_s[tb:, :] = dlx[0:SUBLANES]

        o_b, o_c, o_x = 2 * D_LRU, 2 * D_LRU + D_SC, 2 * D_LRU + 2 * D_SC
        sb = z_ref[:, o_b:o_c].astype(F32)
        scc = z_ref[:, o_c:o_x].astype(F32)
        sx = z_ref[:, o_x:].astype(F32)
        p = scc * sx
        pext = jnp.concatenate([zh[:, o_c:o_x] * zh[:, o_x:], p], axis=0)
        q = _conv_taps(pext, staps, tb)
        dys = dy_ref[:, D_LRU:].astype(F32)
        dq = dys * sb
        dp = _conv_taps_t(jnp.concatenate([dq, dqh[...]], axis=0), staps, tb)
        dp_ref[10:11, 0:D_SC] += jnp.sum(dq * p, axis=0, keepdims=True)
        for k in range(2):
            shifted = pltpu.roll(pext, 2 - k, axis=0)[SUBLANES:]
            dp_ref[8 + k:9 + k, 0:D_SC] += jnp.sum(dq * shifted, axis=0, keepdims=True)
        dqh[...] = dq[0:SUBLANES]
        dz_ref[:, o_b:o_c] = (dys * q).astype(BF16)
        dz_ref[:, o_c:o_x] = (dp * sx).astype(BF16)
        dz_ref[:, o_x:] = (dp * scc).astype(BF16)

        @pl.when(first_block)
        def _():
            dp_ref[5:6, :] = jnp.sum(acc_s[0], axis=0, keepdims=True)
            dp_ref[6:7, :] = jnp.sum(acc_s[1], axis=0, keepdims=True)
            dp_ref[7:8, :] = (jnp.sum(acc_s[2], axis=0, keepdims=True) * RG_C * _sigmoid(-vec[3:4]))

    blk = lambda t: (nb - 1 - t, 0)
    halo8 = lambda t: (jnp.maximum((nb - 1 - t) * (tb // SUBLANES) - 1, 0), 0)
    halo16 = lambda t: (jnp.maximum((nb - 1 - t) * (tb // PACKED) - 1, 0), 0)
    return _pallas(
        body, name=name, grid=(nb,),
        in_specs=[pl.BlockSpec((tb, D_IN), blk), pl.BlockSpec((PACKED, D_IN), halo16),
                  pl.BlockSpec((tb, D_LRU), blk), pl.BlockSpec((SUBLANES, D_LRU), halo8),
                  pl.BlockSpec((tb, D_MIX), blk),
                  pl.BlockSpec((None, 2 * SUBLANES, D_LRU), lambda t: (layer, 0, 0)),
                  pl.BlockSpec((None, 8, 256, 256), lambda t: (layer, 0, 0, 0))],
        out_specs=[pl.BlockSpec((tb, D_IN), blk),
                   pl.BlockSpec((2 * SUBLANES, D_LRU), lambda t: (0, 0)),
                   pl.BlockSpec((8, 256, 256), lambda t: (0, 0, 0))],
        out_shape=[jax.ShapeDtypeStruct((t_len, D_IN), BF16),
                   jax.ShapeDtypeStruct((2 * SUBLANES, D_LRU), F32),
                   jax.ShapeDtypeStruct((8, 256, 256), F32)],
        scratch_shapes=[pltpu.VMEM((tb, D_LRU), F32),
                        pltpu.VMEM((tb, D_LRU), F32), pltpu.VMEM((tb, D_LRU), F32),
                        pltpu.VMEM((tb, D_LRU), BF16), pltpu.VMEM((tb, D_LRU), BF16),
                        pltpu.VMEM((tb + SUBLANES, D_LRU), F32), pltpu.VMEM((tb + SUBLANES, D_LRU), F32),
                        pltpu.VMEM((3, SUBLANES, D_LRU), F32),
                        pltpu.VMEM((SUBLANES, D_LRU), F32), pltpu.VMEM((SUBLANES, D_LRU), F32),
                        pltpu.VMEM((SUBLANES, D_SC), F32)],
        args=(z, z, h, h, dy, prm, gates), sem=("arbitrary",), comm=comm)


def _ffn_act(u, fw, layer, name, tb=512, tn=1024, rc=64, comm=None):
    t_len = u.shape[1]
    hb = tb // PACKED

    def body(u_ref, uh_ref, fw_ref, o_ref, fg_ref, fu_ref, ext):
        keep = jnp.where(pl.program_id(0) == 0, 0.0, 1.0)
        ext[:, 0:SUBLANES, :] = uh_ref[...].astype(F32)[:, PACKED - SUBLANES:, :] * keep
        ext[:, SUBLANES:, :] = u_ref[...].astype(F32)
        fw_v = fw_ref[...]

        for lb in range(tn // LANES):
            lanes = slice(lb * LANES, (lb + 1) * LANES)
            wg = [fw_v[0, k:k + 1, lanes] for k in range(3)]
            wu = [fw_v[1, k:k + 1, lanes] for k in range(3)]

            def chunk(ci, c, lanes=lanes, wg=wg, wu=wu):
                o = pl.multiple_of(ci * rc, rc)
                win = pl.ds(o, rc + SUBLANES)
                gate = _conv_taps(ext[0, win, lanes], wg, rc)
                up = _conv_taps(ext[1, win, lanes], wu, rc)
                gel, dgel = _gelu_parts(gate)
                rows = pl.ds(o, rc)
                o_ref[rows, lanes] = (gel * up).astype(BF16)
                fg_ref[rows, lanes] = (up * dgel).astype(BF16)
                fu_ref[rows, lanes] = gel.astype(BF16)
                return c

            lax.fori_loop(0, tb // rc, chunk, 0)

    spec = pl.BlockSpec((tb, tn), lambda i, j: (i, j))
    shape = jax.ShapeDtypeStruct((t_len, D_FF), BF16)
    return _pallas(
        body, name=name, grid=(t_len // tb, D_FF // tn),
        in_specs=[pl.BlockSpec((2, tb, tn), lambda i, j: (0, i, j)),
                  pl.BlockSpec((2, PACKED, tn), lambda i, j: (0, jnp.maximum(i * hb - 1, 0), j)),
                  pl.BlockSpec((None, 2, SUBLANES, tn), lambda i, j: (layer, 0, 0, j))],
        out_specs=[spec] * 3, out_shape=[shape] * 3,
        scratch_shapes=[pltpu.VMEM((2, tb + SUBLANES, tn), F32)],
        args=(u, u, fw), sem=("parallel", "parallel"), comm=comm)


def _ffn_bwd(dact, fg, fu, u, fw, layer, name, tb=512, tn=1024, rc=32, comm=None):
    t_len = u.shape[1]
    ni = t_len // tb
    hb = tb // PACKED
    last_halo = t_len // PACKED - 1

    def body(d_ref, dn_ref, fg_ref, fgn_ref, fu_ref, fun_ref, u_ref, up_ref, fw_ref, du_ref, dfw_ref,
             extu, extp, acc):
        i = pl.program_id(1)

        @pl.when(i == 0)
        def _():
            acc[...] = jnp.zeros_like(acc)

        keep_prev = jnp.where(i == 0, 0.0, 1.0)
        keep_next = jnp.where(i == ni - 1, 0.0, 1.0)
        extu[:, 0:SUBLANES, :] = up_ref[...].astype(F32)[:, PACKED - SUBLANES:, :] * keep_prev
        extu[:, SUBLANES:, :] = u_ref[...].astype(F32)
        dv = d_ref[...].astype(F32)
        dn = dn_ref[...].astype(F32)[0:SUBLANES] * keep_next
        extp[0, 0:tb, :] = dv * fg_ref[...].astype(F32)
        extp[0, tb:, :] = dn * fgn_ref[...].astype(F32)[0:SUBLANES]
        extp[1, 0:tb, :] = dv * fu_ref[...].astype(F32)
        extp[1, tb:, :] = dn * fun_ref[...].astype(F32)[0:SUBLANES]
        fw_v = fw_ref[...]
        m = rc + SUBLANES

        for lb in range(tn // LANES):
            lanes = slice(lb * LANES, (lb + 1) * LANES)
            taps = [[fw_v[pln, k:k + 1, lanes] for k in range(3)] for pln in range(2)]

            def chunk(ci, c, lanes=lanes, taps=taps):
                o = pl.multiple_of(ci * rc, rc)
                for pln in range(2):
                    e = extu[pln, pl.ds(o, m), lanes]
                    sh = [pltpu.roll(e, 2, axis=0)[SUBLANES:], pltpu.roll(e, 1, axis=0)[SUBLANES:], e[SUBLANES:]]
                    dpost = extp[pln, pl.ds(o, m), lanes]
                    du_ref[pln, pl.ds(o, rc), lanes] = _conv_taps_t(dpost, taps[pln], rc).astype(BF16)
                    for k in range(3):
                        prod = dpost[0:rc] * sh[k]
                        acc[3 * pln + k, :, lanes] += sum(
                            prod[s:s + SUBLANES] for s in range(0, rc, SUBLANES))
                return c

            lax.fori_loop(0, tb // rc, chunk, 0)

        @pl.when(i == ni - 1)
        def _():
            dfw_ref[...] = jnp.zeros_like(dfw_ref)
            for pln in range(2):
                for k in range(3):
                    dfw_ref[pln, k:k + 1, :] = jnp.sum(acc[3 * pln + k], axis=0, keepdims=True)

    main = pl.BlockSpec((tb, tn), lambda j, i: (i, j))
    nxt = pl.BlockSpec((PACKED, tn), lambda j, i: (jnp.minimum((i + 1) * hb, last_halo), j))
    return _pallas(
        body, name=name, grid=(D_FF // tn, ni),
        in_specs=[main, nxt, main, nxt, main, nxt,
                  pl.BlockSpec((2, tb, tn), lambda j, i: (0, i, j)),
                  pl.BlockSpec((2, PACKED, tn), lambda j, i: (0, jnp.maximum(i * hb - 1, 0), j)),
                  pl.BlockSpec((None, 2, SUBLANES, tn), lambda j, i: (layer, 0, 0, j))],
        out_specs=[pl.BlockSpec((2, tb, tn), lambda j, i: (0, i, j)),
                   pl.BlockSpec((2, SUBLANES, tn), lambda j, i: (0, 0, j))],
        out_shape=[jax.ShapeDtypeStruct((2, t_len, D_FF), BF16),
                   jax.ShapeDtypeStruct((2, SUBLANES, D_FF), F32)],
        scratch_shapes=[pltpu.VMEM((2, tb + SUBLANES, tn), F32),
                        pltpu.VMEM((2, tb + SUBLANES, tn), F32),
                        pltpu.VMEM((6, SUBLANES, tn), F32)],
        args=(dact, dact, fg, fg, fu, fu, u, u, fw), sem=("parallel", "arbitrary"), comm=comm)


def _loss_head(x, g, target, name, tb=256):
    t_len, d = x.shape

    def body(x_ref, g_ref, t_ref, dx_ref, dxb_ref, dg_ref, loss_ref):
        @pl.when(pl.program_id(0) == 0)
        def _():
            dg_ref[...] = jnp.zeros_like(dg_ref)
            loss_ref[...] = jnp.zeros_like(loss_ref)

        xv = x_ref[...]
        gv = g_ref[...]
        r = lax.rsqrt(jnp.mean(xv * xv, axis=-1, keepdims=True) + EPS)
        xh = xv * r
        err = xh * gv - t_ref[...]
        loss_ref[...] += (0.5 / d) * jnp.sum(jnp.sum(err * err, axis=-1, keepdims=True), axis=0, keepdims=True)
        dy = err * (1.0 / d)
        dyg = dy * gv
        dx = r * (dyg - xh * jnp.mean(dyg * xh, axis=-1, keepdims=True))
        dx_ref[...] = dx
        dxb_ref[...] = dx.astype(BF16)
        dg_ref[0:1, :] += jnp.sum(dy * xh, axis=0, keepdims=True)

    return _pallas(
        body, name=name, grid=(t_len // tb,),
        in_specs=[pl.BlockSpec((tb, d), lambda i: (i, 0)), pl.BlockSpec((1, d), lambda i: (0, 0)),
                  pl.BlockSpec((tb, d), lambda i: (i, 0))],
        out_specs=[pl.BlockSpec((tb, d), lambda i: (i, 0)), pl.BlockSpec((tb, d), lambda i: (i, 0)),
                   pl.BlockSpec((SUBLANES, d), lambda i: (0, 0)),
                   pl.BlockSpec((SUBLANES, LANES), lambda i: (0, 0))],
        out_shape=[jax.ShapeDtypeStruct((t_len, d), F32), jax.ShapeDtypeStruct((t_len, d), BF16),
                   jax.ShapeDtypeStruct((SUBLANES, d), F32), jax.ShapeDtypeStruct((SUBLANES, LANES), F32)],
        args=(x, g, target), sem=("arbitrary",))[0]


def _adamw(w, g, m, v, name, emit_grad=False, comm=None):
    r, c = w.shape
    tr = 256 if r % 256 == 0 else r
    c1 = 1.0 / (1.0 - ADAM_B1 ** ADAM_STEP)
    c2 = 1.0 / (1.0 - ADAM_B2 ** ADAM_STEP)

    def body(w_ref, g_ref, m_ref, v_ref, d_ref, mo_ref, vo_ref, *go_ref):
        gv = g_ref[...]
        mn = ADAM_B1 * m_ref[...] + (1.0 - ADAM_B1) * gv
        vn = ADAM_B2 * v_ref[...] + (1.0 - ADAM_B2) * (gv * gv)
        d_ref[...] = -ADAM_LR * ((mn * c1) / (jnp.sqrt(vn * c2) + ADAM_EPS) + ADAM_WD * w_ref[...])
        mo_ref[...] = mn
        vo_ref[...] = vn
        if emit_grad:
            go_ref[0][...] = gv

    spec = pl.BlockSpec((tr, c), lambda i: (i, 0))
    shape = jax.ShapeDtypeStruct((r, c), F32)
    n_out = 4 if emit_grad else 3
    return _pallas(
        body, name=name, grid=(r // tr,),
        in_specs=[spec] * 4, out_specs=[spec] * n_out, out_shape=[shape] * n_out,
        args=(w, g, m, v), sem=("parallel",), comm=comm)


def _place():
    x, y, c = lax.axis_index("x"), lax.axis_index("y"), lax.axis_index("c")
    chips = [(1 - x, y), (x, 1 - y), (1 - x, 1 - y)]
    return x, y, c, chips


def _remote(src, dst, send, recv, sem, to):
    return pltpu.make_async_remote_copy(
        src_ref=src, dst_ref=dst, send_sem=send.at[sem], recv_sem=recv.at[sem], device_id=to, device_id_type=MESH)


def _gather_plan(fulls, kinds, mid_at=None, parts=None):
    parts = parts or [(0, 1)] * len(fulls)

    def region(it, f, k, cc):
        kind = kinds[it]
        p, n = parts[it][0:2]
        count = parts[it][2] if len(parts[it]) > 2 else 1
        if kind == SMALL:
            return f.at[k, pl.ds(cc * (CONV_PACK_ROWS // 2), CONV_PACK_ROWS // 2), :]
        if COL_SHARDED[kind]:
            rows, cols = f.shape[0] // (2 * n), f.shape[1] // N_CHIP
            return f.at[pl.ds((cc * n + p) * rows, count * rows), pl.ds(k * cols, cols)]
        assert n == 1
        rows = f.shape[0] // N_CHIP
        return f.at[pl.ds(k * rows + cc * (rows // 2), rows // 2), :]

    def first_hop(bufs, send, recv, it, j):
        x, y, c, chips = _place()
        reg = region(it, bufs[it], 2 * x + y, c)
        return _remote(reg, reg, send, recv, it * 6 + j, (*chips[j], c))

    def arrival(bufs, send, recv, it, j, second):
        x, y, c, chips = _place()
        px, py = chips[j]
        reg = region(it, bufs[it], 2 * px + py, 1 - c if second else c)
        to = (x, y, 1 - c) if second else (px, py, c)
        return _remote(reg, reg, send, recv, it * 6 + (3 + j if second else j), to)

    def forward(bufs, send, recv, it, j):
        x, y, c, chips = _place()
        px, py = chips[j]
        reg = region(it, bufs[it], 2 * px + py, c)
        return _remote(reg, reg, send, recv, it * 6 + 3 + j, (x, y, 1 - c))

    def start(srcs, bufs, outs, send, recv):
        for it in range(len(bufs)):
            for j in range(3):
                first_hop(bufs, send, recv, it, j).start()

    def mid(srcs, bufs, outs, send, recv):
        for it in range(len(bufs)):
            for j in range(3):
                arrival(bufs, send, recv, it, j, False).wait_recv()
                forward(bufs, send, recv, it, j).start()

    def finish(srcs, bufs, outs, send, recv):
        for it in range(len(bufs)):
            for j in range(3):
                arrival(bufs, send, recv, it, j, True).wait_recv()
        for it in range(len(bufs)):
            for j in range(3):
                first_hop(bufs, send, recv, it, j).wait_send()
                forward(bufs, send, recv, it, j).wait_send()

    return Comm(srcs=(), bufs=tuple(fulls), outs=(), n_sem=6 * len(fulls), start=start, mid=mid, finish=finish,
                mid_at=mid_at)


def _half_axis(kind):
    return 0 if kind == SMALL or COL_SHARDED[kind] else 1


def _half2(ref, kind, cc):
    if _half_axis(kind) == 0:
        return ref.at[pl.ds(cc * (ref.shape[0] // 2), ref.shape[0] // 2), :]
    return ref.at[:, pl.ds(cc * (ref.shape[1] // 2), ref.shape[1] // 2)]


def _pair_plan(grads, kinds):
    def land_shape(g, kind):
        s = list(g.shape)
        s[_half_axis(kind)] //= 2
        return jax.ShapeDtypeStruct(tuple(s), F32)

    def copy(srcs, outs, send, recv, it):
        x, y, c, _ = _place()
        return _remote(_half2(srcs[it], kinds[it], 1 - c), outs[it], send, recv, it, (x, y, 1 - c))

    def start(srcs, bufs, outs, send, recv):
        for it in range(len(srcs)):
            copy(srcs, outs, send, recv, it).start()

    def finish(srcs, bufs, outs, send, recv):
        for it in range(len(srcs)):
            copy(srcs, outs, send, recv, it).wait_send()
        for it in range(len(srcs)):
            copy(srcs, outs, send, recv, it).wait_recv()

    return Comm(srcs=tuple(grads), bufs=(), outs=tuple(land_shape(g, k) for g, k in zip(grads, kinds)),
                n_sem=len(grads), start=start, finish=finish)


def _scatter_plan(parts, slots, kinds):
    def piece(s, kind, k):
        if kind == SMALL:
            return s
        if COL_SHARDED[kind]:
            n = s.shape[1] // N_CHIP
            return s.at[:, pl.ds(k * n, n)]
        n = s.shape[0] // N_CHIP
        return s.at[pl.ds(k * n, n), :]

    def outbound(srcs, bufs, send, recv, it, j):
        x, y, c, chips = _place()
        px, py = chips[j]
        return _remote(piece(srcs[it], kinds[it], 2 * px + py), bufs[it].at[2 * x + y], send, recv, it * 3 + j,
                       (px, py, c))

    def inbound(bufs, send, recv, it, j):
        x, y, c, chips = _place()
        px, py = chips[j]
        got = bufs[it].at[2 * px + py]
        return _remote(got, got, send, recv, it * 3 + j, (px, py, c))

    def start(srcs, bufs, outs, send, recv):
        for it in range(len(srcs)):
            for j in range(3):
                outbound(srcs, bufs, send, recv, it, j).start()

    def finish(srcs, bufs, outs, send, recv):
        for it in range(len(srcs)):
            for j in range(3):
                inbound(bufs, send, recv, it, j).wait_recv()
        for it in range(len(srcs)):
            for j in range(3):
                outbound(srcs, bufs, send, recv, it, j).wait_send()

    return Comm(srcs=tuple(parts), bufs=tuple(slots), outs=(), n_sem=3 * len(parts), start=start, finish=finish)


def _share_plan(fulls, kinds, layer):
    def half(f, kind, cc):
        return _half2(f if kind == SMALL else f.at[layer], kind, cc)

    def copy(bufs, send, recv, it, cc):
        x, y, c, _ = _place()
        reg = half(bufs[it], kinds[it], c if cc == "mine" else 1 - c)
        return _remote(reg, reg, send, recv, it, (x, y, 1 - c))

    def start(srcs, bufs, outs, send, recv):
        for it in range(len(bufs)):
            copy(bufs, send, recv, it, "mine").start()

    def finish(srcs, bufs, outs, send, recv):
        for it in range(len(bufs)):
            copy(bufs, send, recv, it, "other").wait_recv()
        for it in range(len(bufs)):
            copy(bufs, send, recv, it, "mine").wait_send()

    return Comm(srcs=(), bufs=tuple(fulls), outs=(), n_sem=len(fulls), start=start, finish=finish)


def _pair_sum(g, land, idx, kind, name):
    odt = F32 if kind == SMALL else BF16
    r, cdim = land.shape

    def body(idx_ref, g_ref, l_ref, p_ref, s_ref):
        v = (g_ref[...] + l_ref[...]).astype(odt)
        p_ref[...] = v
        if kind == SMALL:
            s_ref[...] = v
        else:
            @pl.when(pl.program_id(1 if COL_SHARDED[kind] else 0) == idx_ref[1])
            def _():
                s_ref[...] = v

    if kind == SMALL:
        grid = (1,)
        g_spec = pl.BlockSpec((r, LANES), lambda i, idx_ref: (idx_ref[0], 0))
        spec = pl.BlockSpec((r, LANES), lambda i, idx_ref: (0, 0))
        s_spec = pl.BlockSpec((None, r, LANES), lambda i, idx_ref: (idx_ref[1], 0, 0))
        s_shape = (N_CHIP, r, LANES)
    elif COL_SHARDED[kind]:
        pc, tr = cdim // N_CHIP, 256
        nrb = r // tr
        grid = (nrb, N_CHIP)
        g_spec = pl.BlockSpec((tr, pc), lambda i, k, idx_ref: (idx_ref[0] * nrb + i, k))
        spec = pl.BlockSpec((tr, pc), lambda i, k, idx_ref: (i, k))
        s_spec = pl.BlockSpec((None, tr, pc), lambda i, k, idx_ref: (idx_ref[1], i, 0))
        s_shape = (N_CHIP, r, pc)
    else:
        pr = r // N_CHIP
        grid = (N_CHIP,)
        g_spec = pl.BlockSpec((pr, cdim), lambda k, idx_ref: (k, idx_ref[0]))
        spec = pl.BlockSpec((pr, cdim), lambda k, idx_ref: (k, 0))
        s_spec = pl.BlockSpec((None, pr, cdim), lambda k, idx_ref: (idx_ref[1], 0, 0))
        s_shape = (N_CHIP, pr, cdim)
    return pl.pallas_call(
        body, name=name,
        grid_spec=pltpu.PrefetchScalarGridSpec(
            num_scalar_prefetch=1, grid=grid, in_specs=[g_spec, spec], out_specs=[spec, s_spec]),
        out_shape=[jax.ShapeDtypeStruct(land.shape, odt), jax.ShapeDtypeStruct(s_shape, odt)],
        compiler_params=_cp(*(["arbitrary"] * len(grid))),
    )(idx, g, land)


def _sum_slots(slots, idx, kind, layer, prev, name):
    _, r, cdim = slots.shape

    def body(*refs):
        s_ref, o_ref = refs[1], refs[-1]
        v = s_ref[...].astype(F32)
        o_ref[...] = (v[0] + v[1]) + (v[2] + v[3])

    if kind == SMALL:
        grid = (1,)
        s_spec = pl.BlockSpec((N_CHIP, r, cdim), lambda i, idx_ref: (0, 0, 0))
        o_spec = pl.BlockSpec((r, cdim), lambda i, idx_ref: (idx_ref[0], 0))
        full = (2 * r, cdim)
    else:
        tr = 256 if r % 256 == 0 else 384
        nrb = r // tr
        grid = (nrb,)
        s_spec = pl.BlockSpec((N_CHIP, tr, cdim), lambda i, idx_ref: (0, i, 0))
        if COL_SHARDED[kind]:
            o_spec = pl.BlockSpec((None, tr, cdim), lambda i, idx_ref: (layer, idx_ref[0] * nrb + i, 0))
            full = (2, 2 * r, cdim)
        else:
            o_spec = pl.BlockSpec((None, tr, cdim), lambda i, idx_ref: (layer, i, idx_ref[0]))
            full = (2, r, 2 * cdim)
    in_specs, args, aliases = [s_spec], [idx, slots], {}
    if prev is not None:
        in_specs.append(ANY)
        args.append(prev)
        aliases = {2: 0}
    return pl.pallas_call(
        body, name=name,
        grid_spec=pltpu.PrefetchScalarGridSpec(
            num_scalar_prefetch=1, grid=grid, in_specs=in_specs, out_specs=o_spec),
        out_shape=jax.ShapeDtypeStruct(full, F32),
        input_output_aliases=aliases,
        compiler_params=_cp(*(["parallel"] * len(grid))),
    )(*args)


def _block_diag(w):
    w4 = w.reshape(2, 4, 4, 64, 64)
    eye = jnp.eye(4, dtype=w.dtype)[None, None, :, None, :, None]
    return (w4[:, :, :, :, None, :] * eye).reshape(2, 4, 256, 256)


def _block_diag_extract(d):
    d5 = d.reshape(4, 4, 64, 4, 64)
    return jnp.stack([d5[:, hh, :, hh, :] for hh in range(4)], axis=1).reshape(-1)


REP_NAMES = ("norm1_g", "lru_conv_b", "lru_ba", "lru_bx", "lru_lambda", "norm2_g", "lru_wa", "lru_wx")


def _pack_rep(norm1_g, conv_b, ba, bx, lam, norm2_g, wa, wx, final_g):
    parts = [a.reshape(-1) for a in (norm1_g, conv_b, ba, bx, lam, norm2_g, wa, wx, final_g)]
    return jnp.concatenate(parts).reshape(REP_ROWS, LANES)


def _unpack_rep(buf):
    flat = buf.reshape(-1)
    res, o = {}, 0
    for k in REP_NAMES:
        shape = (2, 16, 64, 64) if k in ("lru_wa", "lru_wx") else (2, 1024)
        n = math.prod(shape)
        res[k] = flat[o:o + n].reshape(shape)
        o += n
    res["final_g"] = flat[o:o + 1024]
    return res


def _pack_conv_shard(lru_cw, sc_cw, ffn_cw):
    return jnp.concatenate([lru_cw.reshape(16, LANES), jnp.pad(sc_cw.reshape(6, LANES), ((0, 2), (0, 0))),
                            ffn_cw.reshape(72, LANES)], axis=0)


def _unpack_conv_shard(buf):
    return (buf[0:16].reshape(2, 4, 256), buf[16:22].reshape(2, 3, 128), buf[24:96].reshape(2, 3, 1536))


def kernel(x, norm1_g, w_in, lru_conv_w, lru_conv_b, lru_wa, lru_ba, lru_wx, lru_bx, lru_lambda, sc_conv_w, w_out, norm2_g, w_up, ffn_conv_w, w_down, final_g, loss_target, m_norm1_g, m_w_in, m_lru_conv_w, m_lru_conv_b, m_lru_wa, m_lru_ba, m_lru_wx, m_lru_bx, m_lru_lambda, m_sc_conv_w, m_w_out, m_norm2_g, m_w_up, m_ffn_conv_w, m_w_down, m_final_g, v_norm1_g, v_w_in, v_lru_conv_w, v_lru_conv_b, v_lru_wa, v_lru_ba, v_lru_wx, v_lru_bx, v_lru_lambda, v_sc_conv_w, v_w_out, v_norm2_g, v_w_up, v_ffn_conv_w, v_w_down, v_final_g):
    me = 2 * lax.axis_index("x") + lax.axis_index("y")
    idx = jnp.stack([lax.axis_index("c"), me]).astype(jnp.int32)
    t_len = x.shape[1]

    s_conv = _pack_conv_shard(lru_conv_w, sc_conv_w, ffn_conv_w)
    conv_slots = lax.dynamic_update_slice(jnp.zeros((N_CHIP, CONV_PACK_ROWS, LANES), F32), s_conv[None], (me, 0, 0))
    wi = list(_cast_into_full(w_in, W_IN, idx, "cast_w_in"))
    wo = list(_cast_into_full(w_out, W_OUT, idx, "cast_w_out"))
    wu = list(_cast_into_full(w_up, W_UP, idx, "cast_w_up"))
    wd = list(_cast_into_full(w_down, W_DOWN, idx, "cast_w_down"))
    wi[0], convs = _comm_call(_gather_plan([wi[0], conv_slots], [W_IN, SMALL]), "ag_first")
    per_chip = [_unpack_conv_shard(convs[k]) for k in range(N_CHIP)]
    lru_cw = jnp.concatenate([p[0] for p in per_chip], axis=-1)
    sc_cw = jnp.concatenate([p[1] for p in per_chip], axis=-1)
    ffn_cw = jnp.concatenate([p[2] for p in per_chip], axis=-1)

    prm = jnp.concatenate(
        [lru_cw, jnp.stack([lru_conv_b, lru_ba, lru_bx, lru_lambda], axis=1),
         jnp.pad(sc_cw, ((0, 0), (0, 0), (0, D_LRU - D_SC))), jnp.zeros((2, 5, D_LRU), F32)], axis=1)
    gates = jnp.concatenate([_block_diag(lru_wa), _block_diag(lru_wx)], axis=1).astype(BF16)
    fw8 = jnp.pad(ffn_cw.reshape(2, 3, 2, D_FF).transpose(0, 2, 1, 3), ((0, 0), (0, 0), (0, 5), (0, 0)))

    xs = x[0]
    saved = []
    n512, n256 = t_len // 512, t_len // 256
    whole, lower, upper = (0, 1), (0, 2), (1, 2)
    carried_by = {
        "fwd_in_0": ([(wu, 0, W_UP, (0, 4))], (max(n512 - 3, 0),)),
        "fwd_mixer_0": ([(wu, 0, W_UP, (1, 4, 2)), (wo, 0, W_OUT, whole)], (max(n256 - 3, 0),)),
        "fwd_out_0": ([(wu, 0, W_UP, (3, 4))], (max(n512 - 2, 0),)),
        "fwd_up_0": ([(wd, 0, W_DOWN, whole)], (max(n512 - 2, 0),)),
        "fwd_act_0": ([(wi, 1, W_IN, whole), (wo, 1, W_OUT, whole)], (max(n512 - 2, 0), 0)),
        "fwd_down_0": ([(wu, 1, W_UP, (0, 4))], (max(n512 - 3, 0),)),
        "fwd_in_1": ([(wu, 1, W_UP, (1, 4))], (max(n512 - 3, 0),)),
        "fwd_mixer_1": ([(wu, 1, W_UP, (2, 4, 2))], (max(n256 - 4, 0),)),
        "fwd_act_1": ([(wd, 1, W_DOWN, whole)], (max(n512 - 3, 0), 0)),
    }

    def carried(name):
        if name not in carried_by:
            return None, lambda got: None
        items, mid_at = carried_by[name]

        def store(got):
            for (lst, i, _, _), arr in zip(items, got):
                lst[i] = arr

        return _gather_plan([lst[i] for lst, i, _, _ in items], [k for _, _, k, _ in items], mid_at=mid_at,
                            parts=[p for _, _, _, p in items]), store

    for l in range(2):
        comm, store = carried(f"fwd_in_{l}")
        (z, h1), got = _norm_mm(xs, norm1_g[l][None], wi[l], f"fwd_in_{l}", comm=comm)
        store(got)
        comm, store = carried(f"fwd_mixer_{l}")
        (ymix, hst), got = _mixer_fwd(z, prm, gates, l, f"fwd_mixer_{l}", comm=comm)
        store(got)
        comm, store = carried(f"fwd_out_{l}")
        (x2,), got = _mm_res(ymix, wo[l], xs, f"fwd_out_{l}", comm=comm)
        store(got)
        comm, store = carried(f"fwd_up_{l}")
        (u, h2), got = _norm_mm(x2, norm2_g[l][None], wu[l], f"fwd_up_{l}", planes=True, comm=comm)
        store(got)
        comm, store = carried(f"fwd_act_{l}")
        (act, fg, fu), got = _ffn_act(u, fw8, l, f"fwd_act_{l}", comm=comm)
        store(got)
        comm, store = carried(f"fwd_down_{l}")
        (x3,), got = _mm_res(act, wd[l], x2, f"fwd_down_{l}", comm=comm)
        store(got)
        saved.append((xs, h1, z, hst, ymix, x2, h2, u, act, fg, fu))
        xs = x3

    dx, dxb, dgf, loss_blk = _loss_head(xs, final_g[None], loss_target[0], "loss_head")

    kinds = [W_IN, W_OUT, W_UP, W_DOWN]
    grads = [None, None]
    small = [None, None]
    reduced = [None] * 4
    summed1 = [None] * 4
    parts = slots = None
    for l in (1, 0):
        x_in, h1, z, hst, ymix, x2, h2, u, act, fg, fu = saved[l]
        carry = l == 0
        comm = _pair_plan([grads[1][W_IN]], [W_IN]) if carry else None
        (g_down,), got = _mm_tn(act, dxb, f"bwd_wdown_{l}", tk=1536, tn=1024, comm=comm)
        if carry:
            summed1[W_IN] = _pair_sum(grads[1][W_IN], got[0], idx, W_IN, "rs_add1_0")
            parts, slots = [s[0] for s in summed1], [s[1] for s in summed1]
        (dact,), _ = _mm_nt(dxb, wd[l], f"bwd_dact_{l}")
        comm = _scatter_plan(parts, slots, kinds) if carry else None
        (du, dfw), got = _ffn_bwd(dact, fg, fu, u, fw8, l, f"bwd_act_{l}", comm=comm)
        if carry:
            reduced = [_sum_slots(got[w], idx, kinds[w], 1, None, f"rs_sum1_{w}") for w in range(4)]
        comm = _share_plan(reduced, kinds, 1) if carry else None
        (g_up,), got = _mm_tn(h2, du, f"bwd_wup_{l}", tk=1024, tn=1536, planes=True, comm=comm)
        if carry:
            reduced = list(got)
        comm = _pair_plan([g_up, g_down], [W_UP, W_DOWN]) if carry else None
        (dx2, dx2b, dg2), got = _mm_nt_normbwd(du, wu[l], x2, norm2_g[l][None], dx, f"bwd_up_{l}", planes=True,
                                               comm=comm)
        if carry:
            sum_up = _pair_sum(g_up, got[0], idx, W_UP, "rs_add0_2")
            sum_down = _pair_sum(g_down, got[1], idx, W_DOWN, "rs_add0_3")
        (g_out,), _ = _mm_tn(ymix, dx2b, f"bwd_wout_{l}", tk=1536, tn=1024)
        comm = _pair_plan([g_out], [W_OUT]) if carry else None
        (dymix,), got = _mm_nt(dx2b, wo[l], f"bwd_dymix_{l}", comm=comm)
        trio = (W_OUT, W_UP, W_DOWN)
        if carry:
            sum_out = _pair_sum(g_out, got[0], idx, W_OUT, "rs_add0_1")
            comm = _scatter_plan([sum_out[0], sum_up[0], sum_down[0]], [sum_out[1], sum_up[1], sum_down[1]], trio)
        else:
            comm = _pair_plan([g_out, g_up, g_down], trio)
        (dz, dprm, dgates), got = _mixer_bwd(z, hst, dymix, prm, gates, l, f"bwd_mixer_{l}", comm=comm)
        if carry:
            for w, s in zip(trio, got):
                reduced[w] = _sum_slots(s, idx, w, 0, reduced[w], f"rs_sum0_{w}")
        else:
            for w, g, land in zip(trio, (g_out, g_up, g_down), got):
                summed1[w] = _pair_sum(g, land, idx, w, f"rs_add1_{w}")
        comm = _share_plan([reduced[w] for w in trio], trio, 0) if carry else None
        (g_in,), got = _mm_tn(h1, dz, f"bwd_win_{l}", tk=1024, tn=1792, comm=comm)
        if carry:
            for w, full in zip(trio, got):
                reduced[w] = full
        (dx, dxb, dg1), _ = _mm_nt_normbwd(dz, wi[l], x_in, norm1_g[l][None], dx2, f"bwd_in_{l}")
        grads[l] = [g_in, g_out, g_up, g_down]
        rep = dict(zip(REP_NAMES, [dg1[0], dprm[4], dprm[5], dprm[6], dprm[7], dg2[0],
                                   _block_diag_extract(dgates[0:4]), _block_diag_extract(dgates[4:8])]))
        conv = [dprm[0:4].reshape(-1), jnp.pad(dprm[8:11, 0:D_SC].reshape(-1), (0, 512)),
                dfw[:, 0:3, :].transpose(1, 0, 2).reshape(-1)]
        small[l] = (rep, conv)
    grad_x = dx[None]
    g_small = jnp.concatenate(
        [small[l][0][k] for k in REP_NAMES for l in range(2)] + [dgf[0]] + small[0][1] + small[1][1]
        + [loss_blk.reshape(-1)]).reshape(SMALL_ROWS, LANES)

    def big(w, g, m, v, name):
        shape = w.shape
        two_d = lambda a: a.reshape(-1, shape[-1])
        outs, _ = _adamw(two_d(w), two_d(g), two_d(m), two_d(v), name, emit_grad=True)
        return [o.reshape(shape) for o in outs]

    land_in, land_small = _comm_call(_pair_plan([grads[0][W_IN], g_small], [W_IN, SMALL]), "rs_pair_last")
    sum_in = _pair_sum(grads[0][W_IN], land_in, idx, W_IN, "rs_add0_0")
    sum_small = _pair_sum(g_small, land_small, idx, SMALL, "rs_add0_4")
    slot_in, slot_small = _comm_call(
        _scatter_plan([sum_in[0], sum_small[0]], [sum_in[1], sum_small[1]], [W_IN, SMALL]), "rs_scatter_last")
    gw_in, gs = _comm_call(
        _share_plan([_sum_slots(slot_in, idx, W_IN, 0, reduced[W_IN], "rs_sum0_0"),
                     _sum_slots(slot_small, idx, SMALL, 0, None, "rs_sum0_4")], [W_IN, SMALL], 0), "rs_share0")
    upd = {"w_up": big(w_up, reduced[W_UP], m_w_up, v_w_up, "adamw_w_up"),
           "w_down": big(w_down, reduced[W_DOWN], m_w_down, v_w_down, "adamw_w_down"),
           "w_out": big(w_out, reduced[W_OUT], m_w_out, v_w_out, "adamw_w_out"),
           "w_in": big(w_in, gw_in, m_w_in, v_w_in, "adamw_w_in")}

    loss = gs[REP_ROWS + CONV_ROWS, 0]
    g_rep = gs[0:REP_ROWS]
    g_conv = gs[REP_ROWS:REP_ROWS + CONV_ROWS].reshape(2, CONV_LAYER)
    g_lru_cw = lax.dynamic_slice_in_dim(g_conv[:, 0:4096].reshape(2, 4, 1024), me * 256, 256, axis=2)
    g_sc_cw = lax.dynamic_slice_in_dim(g_conv[:, 4096:4096 + 1536].reshape(2, 3, 512), me * 128, 128, axis=2)
    g_ffn_cw = lax.dynamic_slice_in_dim(g_conv[:, 6144:].reshape(2, 3, 6144), me * 1536, 1536, axis=2)

    rep_out, _ = _adamw(
        _pack_rep(norm1_g, lru_conv_b, lru_ba, lru_bx, lru_lambda, norm2_g, lru_wa, lru_wx, final_g), g_rep,
        _pack_rep(m_norm1_g, m_lru_conv_b, m_lru_ba, m_lru_bx, m_lru_lambda, m_norm2_g, m_lru_wa, m_lru_wx, m_final_g),
        _pack_rep(v_norm1_g, v_lru_conv_b, v_lru_ba, v_lru_bx, v_lru_lambda, v_norm2_g, v_lru_wa, v_lru_wx, v_final_g),
        "adamw_rep")
    conv_out, _ = _adamw(s_conv, _pack_conv_shard(g_lru_cw, g_sc_cw, g_ffn_cw),
                         _pack_conv_shard(m_lru_conv_w, m_sc_conv_w, m_ffn_conv_w),
                         _pack_conv_shard(v_lru_conv_w, v_sc_conv_w, v_ffn_conv_w), "adamw_conv")

    names = ["norm1_g", "w_in", "lru_conv_w", "lru_conv_b", "lru_wa", "lru_ba", "lru_wx", "lru_bx", "lru_lambda",
             "sc_conv_w", "w_out", "norm2_g", "w_up", "ffn_conv_w", "w_down", "final_g"]
    groups = []
    g_all = dict(_unpack_rep(g_rep))
    g_all.update({k: v[3] for k, v in upd.items()})
    g_all.update(lru_conv_w=g_lru_cw, sc_conv_w=g_sc_cw, ffn_conv_w=g_ffn_cw)
    groups.append(g_all)
    for i in range(3):
        d = dict(_unpack_rep(rep_out[i]))
        cl, cs, cf = _unpack_conv_shard(conv_out[i])
        d.update(lru_conv_w=cl, sc_conv_w=cs, ffn_conv_w=cf)
        d.update({k: v[i] for k, v in upd.items()})
        groups.append(d)
    return (loss, grad_x, *[grp[n] for grp in groups for n in names])
```

```python
import dataclasses
import functools
import math
import operator
from typing import Any, Callable, Optional, Sequence

import jax
import jax.numpy as jnp
from jax import lax
from jax.experimental import pallas as pl
from jax.experimental.pallas import tpu as pltpu

F32 = jnp.float32
BF16 = jnp.bfloat16
MESH = pl.DeviceIdType.MESH

D_MODEL = 1024
D_LRU = 1024
D_SC = 512
D_MIX = D_LRU + D_SC
D_IN = 2 * D_LRU + 3 * D_SC
D_FF = 3072
N_CHIP = 4
RG_C = 8.0
EPS = 1e-6
ADAM_LR = 0.001
ADAM_B1 = 0.9
ADAM_B2 = 0.999
ADAM_EPS = 1e-08
ADAM_WD = 0.01
ADAM_STEP = 10

SUBLANES = 8
PACKED = 16
LANES = 128
VMEM_LIMIT = 56 * 1024 * 1024
GELU_C0 = math.sqrt(2.0 / math.pi)
GELU_C1 = 0.044715

REP_LAYER = 6 * 1024 + 2 * 16 * 64 * 64
REP_ROWS = (2 * REP_LAYER + 1024) // LANES
CONV_LAYER = 4 * 1024 + 2048 + 3 * 6144
CONV_ROWS = 2 * CONV_LAYER // LANES
SMALL_ROWS = REP_ROWS + CONV_ROWS + 8
CONV_PACK_ROWS = 96

W_IN, W_OUT, W_UP, W_DOWN, SMALL = range(5)
COL_SHARDED = {W_IN: True, W_OUT: False, W_UP: True, W_DOWN: False}

ONCE = pl.Buffered(1)
ANY = pl.BlockSpec(memory_space=pl.ANY)


def _cp(*sem):
    return pltpu.CompilerParams(dimension_semantics=sem, vmem_limit_bytes=VMEM_LIMIT)


@dataclasses.dataclass
class Comm:
    srcs: Sequence[Any]
    bufs: Sequence[Any]
    outs: Sequence[Any]
    n_sem: int
    start: Callable
    finish: Callable
    mid: Optional[Callable] = None
    mid_at: Optional[Sequence[int]] = None


def _pallas(body, *, name, grid, in_specs, out_specs, out_shape, args, sem, scratch_shapes=(), comm=None):
    if comm is None:
        res = pl.pallas_call(
            body, name=name, grid=grid, in_specs=list(in_specs), out_specs=list(out_specs),
            out_shape=list(out_shape), scratch_shapes=list(scratch_shapes), compiler_params=_cp(*sem))(*args)
        return tuple(res), ()
    n_in, n_out, n_scr = len(in_specs), len(out_specs), len(scratch_shapes)
    ns, nb, no = len(comm.srcs), len(comm.bufs), len(comm.outs)

    def carrier(*refs):
        p = 0
        main_in = refs[p:p + n_in]
        p += n_in
        c_src = refs[p:p + ns]
        p += ns + nb
        main_out = refs[p:p + n_out]
        p += n_out
        c_buf = refs[p:p + nb]
        p += nb
        c_out = refs[p:p + no]
        p += no
        scr = refs[p:p + n_scr]
        send, recv = refs[p + n_scr], refs[p + n_scr + 1]
        ids = [pl.program_id(a) for a in range(len(grid))]

        def at(steps):
            return functools.reduce(operator.and_, [i == s for i, s in zip(ids, steps)])

        @pl.when(at([0] * len(grid)))
        def _():
            comm.start(c_src, c_buf, c_out, send, recv)

        if comm.mid is not None:
            @pl.when(at(comm.mid_at))
            def _():
                comm.mid(c_src, c_buf, c_out, send, recv)

        body(*main_in, *main_out, *scr)

        @pl.when(at([g - 1 for g in grid]))
        def _():
            comm.finish(c_src, c_buf, c_out, send, recv)

    res = pl.pallas_call(
        carrier, name=name, grid=grid,
        in_specs=list(in_specs) + [ANY] * (ns + nb),
        out_specs=list(out_specs) + [ANY] * (nb + no),
        out_shape=list(out_shape) + [jax.ShapeDtypeStruct(b.shape, b.dtype) for b in comm.bufs] + list(comm.outs),
        input_output_aliases={n_in + ns + j: n_out + j for j in range(nb)},
        scratch_shapes=list(scratch_shapes) + [pltpu.SemaphoreType.DMA((comm.n_sem,)),
                                               pltpu.SemaphoreType.DMA((comm.n_sem,))],
        compiler_params=_cp(*(["arbitrary"] * len(grid))),
    )(*args, *comm.srcs, *comm.bufs)
    return tuple(res[:n_out]), tuple(res[n_out:])


def _comm_call(comm, name):
    ns, nb, no = len(comm.srcs), len(comm.bufs), len(comm.outs)

    def body(*refs):
        c_src = refs[0:ns]
        c_buf = refs[ns + nb:ns + 2 * nb]
        c_out = refs[ns + 2 * nb:ns + 2 * nb + no]
        send, recv = refs[ns + 2 * nb + no], refs[ns + 2 * nb + no + 1]
        comm.start(c_src, c_buf, c_out, send, recv)
        if comm.mid is not None:
            comm.mid(c_src, c_buf, c_out, send, recv)
        comm.finish(c_src, c_buf, c_out, send, recv)

    return tuple(pl.pallas_call(
        body, name=name,
        in_specs=[ANY] * (ns + nb), out_specs=[ANY] * (nb + no),
        out_shape=[jax.ShapeDtypeStruct(b.shape, b.dtype) for b in comm.bufs] + list(comm.outs),
        input_output_aliases={ns + j: j for j in range(nb)},
        scratch_shapes=[pltpu.SemaphoreType.DMA((comm.n_sem,)), pltpu.SemaphoreType.DMA((comm.n_sem,))],
    )(*comm.srcs, *comm.bufs))


def _sigmoid(v):
    return 1.0 / (1.0 + jnp.exp(-v))


def _sigmoid_tanh(v):
    return 0.5 + 0.5 * jnp.tanh(0.5 * v)


def _gelu_parts(v):
    v2 = v * v
    t = jnp.tanh(GELU_C0 * v * (1.0 + GELU_C1 * v2))
    half = 0.5 * (1.0 + t)
    gel = v * half
    dgel = half + 0.5 * v * (1.0 - t * t) * (GELU_C0 * (1.0 + 3.0 * GELU_C1 * v2))
    return gel, dgel


def _gelu(v):
    t = jnp.tanh(GELU_C0 * v * (1.0 + GELU_C1 * (v * v)))
    return 0.5 * v * (1.0 + t)


def _neg_expm1(y, a):
    p = jnp.full_like(y, 1.0 / 120.0)
    for coef in (1.0 / 24.0, 1.0 / 6.0, 0.5, 1.0):
        p = p * y + coef
    return jnp.where(y > -0.1, -(p * y), 1.0 - a * a)


def _softplus_neg(lam):
    nl = -lam
    e = jnp.exp(-jnp.abs(nl))
    u = 1.0 + e
    l1p = jnp.where(u == 1.0, e, jnp.log(u) * e / (u - 1.0))
    return jnp.maximum(nl, 0.0) + l1p


def _conv_taps(ext, taps, n_out):
    kw = len(taps)
    acc = taps[kw - 1] * ext[SUBLANES:SUBLANES + n_out]
    for k in range(kw - 1):
        acc = acc + taps[k] * pltpu.roll(ext, kw - 1 - k, axis=0)[SUBLANES:SUBLANES + n_out]
    return acc


def _conv_taps_t(ext, taps, n_out):
    kw = len(taps)
    n = ext.shape[0]
    acc = taps[kw - 1] * ext[0:n_out]
    for k in range(kw - 1):
        acc = acc + taps[k] * pltpu.roll(ext, n - (kw - 1 - k), axis=0)[0:n_out]
    return acc


def _scan8(a, b, carry, row):
    for s in (1, 2, 4):
        m = row >= s
        a_sh = jnp.where(m, pltpu.roll(a, s, axis=0), 1.0)
        b_sh = jnp.where(m, pltpu.roll(b, s, axis=0), 0.0)
        b = a * b_sh + b
        a = a * a_sh
    return a * carry + b


def _scan8_rev(a, b, carry, row):
    for s in (1, 2, 4):
        m = row < SUBLANES - s
        a_sh = jnp.where(m, pltpu.roll(a, SUBLANES - s, axis=0), 1.0)
        b_sh = jnp.where(m, pltpu.roll(b, SUBLANES - s, axis=0), 0.0)
        b = a * b_sh + b
        a = a * a_sh
    return a * carry + b


def _cast_into_full(w, kind, idx, name):
    nl, r, c = w.shape
    tr = 256 if r % 256 == 0 else r
    nrb = r // tr

    def body(idx_ref, w_ref, o0_ref, o1_ref):
        o0_ref[...] = w_ref[0].astype(BF16)
        o1_ref[...] = w_ref[1].astype(BF16)

    if COL_SHARDED[kind]:
        full = (r, N_CHIP * c)
        o_spec = pl.BlockSpec((tr, c), lambda i, idx_ref: (i, idx_ref[1]))
    else:
        full = (N_CHIP * r, c)
        o_spec = pl.BlockSpec((tr, c), lambda i, idx_ref: (idx_ref[1] * nrb + i, 0))
    return pl.pallas_call(
        body, name=name,
        grid_spec=pltpu.PrefetchScalarGridSpec(
            num_scalar_prefetch=1, grid=(nrb,),
            in_specs=[pl.BlockSpec((nl, tr, c), lambda i, idx_ref: (0, i, 0))], out_specs=[o_spec, o_spec]),
        out_shape=[jax.ShapeDtypeStruct(full, BF16)] * 2,
        compiler_params=_cp("parallel"),
    )(idx, w)


def _norm_mm(x, g, w, name, planes=False, tm=512, tn=512, comm=None):
    t_len, d = x.shape
    n = w.shape[1]
    half = n // 2

    def body(x_ref, g_ref, w_ref, z_ref, h_ref):
        xv = x_ref[...]
        r = lax.rsqrt(jnp.mean(xv * xv, axis=-1, keepdims=True) + EPS)
        h_ref[...] = ((xv * r) * g_ref[...]).astype(BF16)
        for n0 in range(0, n, tn):
            blk = jnp.dot(h_ref[...], w_ref[:, n0:n0 + tn], preferred_element_type=F32).astype(BF16)
            if planes:
                z_ref[n0 // half, :, n0 % half:n0 % half + tn] = blk
            else:
                z_ref[:, n0:n0 + tn] = blk

    if planes:
        z_shape = jax.ShapeDtypeStruct((2, t_len, half), BF16)
        z_spec = pl.BlockSpec((2, tm, half), lambda i: (0, i, 0))
    else:
        z_shape = jax.ShapeDtypeStruct((t_len, n), BF16)
        z_spec = pl.BlockSpec((tm, n), lambda i: (i, 0))
    return _pallas(
        body, name=name, grid=(t_len // tm,),
        in_specs=[pl.BlockSpec((tm, d), lambda i: (i, 0)),
                  pl.BlockSpec((1, d), lambda i: (0, 0)),
                  pl.BlockSpec((d, n), lambda i: (0, 0), pipeline_mode=ONCE)],
        out_specs=[z_spec, pl.BlockSpec((tm, d), lambda i: (i, 0))],
        out_shape=[z_shape, jax.ShapeDtypeStruct((t_len, d), BF16)],
        args=(x, g, w), sem=("parallel",), comm=comm)


def _mm_res(a, w, res, name, tm=512, comm=None):
    t_len, k = a.shape
    n = w.shape[1]

    def body(a_ref, w_ref, r_ref, o_ref):
        o_ref[...] = r_ref[...] + jnp.dot(a_ref[...], w_ref[...], preferred_element_type=F32)

    return _pallas(
        body, name=name, grid=(t_len // tm,),
        in_specs=[pl.BlockSpec((tm, k), lambda i: (i, 0)),
                  pl.BlockSpec((k, n), lambda i: (0, 0), pipeline_mode=ONCE),
                  pl.BlockSpec((tm, n), lambda i: (i, 0))],
        out_specs=[pl.BlockSpec((tm, n), lambda i: (i, 0))],
        out_shape=[jax.ShapeDtypeStruct((t_len, n), F32)],
        args=(a, w, res), sem=("parallel",), comm=comm)


def _mm_nt(a, w, name, tm=512, comm=None):
    t_len, k = a.shape
    n = w.shape[0]

    def body(a_ref, w_ref, o_ref):
        o_ref[...] = lax.dot_general(a_ref[...], w_ref[...], (((1,), (1,)), ((), ())),
                                     preferred_element_type=F32).astype(BF16)

    return _pallas(
        body, name=name, grid=(t_len // tm,),
        in_specs=[pl.BlockSpec((tm, k), lambda i: (i, 0)),
                  pl.BlockSpec((n, k), lambda i: (0, 0), pipeline_mode=ONCE)],
        out_specs=[pl.BlockSpec((tm, n), lambda i: (i, 0))],
        out_shape=[jax.ShapeDtypeStruct((t_len, n), BF16)],
        args=(a, w), sem=("parallel",), comm=comm)


def _mm_nt_normbwd(dz, w, x, g, dres, name, planes=False, tm=512, comm=None):
    t_len, d = x.shape
    n = w.shape[1]
    half = n // 2
    nt_dims = (((1,), (1,)), ((), ()))

    def body(dz_ref, w_ref, x_ref, g_ref, r_ref, dx_ref, dxb_ref, dg_ref):
        @pl.when(pl.program_id(0) == 0)
        def _():
            dg_ref[...] = jnp.zeros_like(dg_ref)

        if planes:
            dh = (lax.dot_general(dz_ref[0], w_ref[:, 0:half], nt_dims, preferred_element_type=F32)
                  + lax.dot_general(dz_ref[1], w_ref[:, half:], nt_dims, preferred_element_type=F32))
        else:
            dh = lax.dot_general(dz_ref[...], w_ref[...], nt_dims, preferred_element_type=F32)
        xv = x_ref[...]
        r = lax.rsqrt(jnp.mean(xv * xv, axis=-1, keepdims=True) + EPS)
        xh = xv * r
        dhg = dh * g_ref[...]
        dx = r_ref[...] + r * (dhg - xh * jnp.mean(dhg * xh, axis=-1, keepdims=True))
        dx_ref[...] = dx
        dxb_ref[...] = dx.astype(BF16)
        dg_ref[0:1, :] += jnp.sum(dh * xh, axis=0, keepdims=True)

    if planes:
        dz_spec = pl.BlockSpec((2, tm, half), lambda i: (0, i, 0))
    else:
        dz_spec = pl.BlockSpec((tm, n), lambda i: (i, 0))
    return _pallas(
        body, name=name, grid=(t_len // tm,),
        in_specs=[dz_spec,
                  pl.BlockSpec((d, n), lambda i: (0, 0), pipeline_mode=ONCE),
                  pl.BlockSpec((tm, d), lambda i: (i, 0)),
                  pl.BlockSpec((1, d), lambda i: (0, 0)),
                  pl.BlockSpec((tm, d), lambda i: (i, 0))],
        out_specs=[pl.BlockSpec((tm, d), lambda i: (i, 0)),
                   pl.BlockSpec((tm, d), lambda i: (i, 0)),
                   pl.BlockSpec((SUBLANES, d), lambda i: (0, 0))],
        out_shape=[jax.ShapeDtypeStruct((t_len, d), F32),
                   jax.ShapeDtypeStruct((t_len, d), BF16),
                   jax.ShapeDtypeStruct((SUBLANES, d), F32)],
        args=(dz, w, x, g, dres), sem=("arbitrary",), comm=comm)


def _mm_tn(a, g, name, tk, tn, planes=False, tt=1024, comm=None):
    t_len, k = a.shape
    n = 2 * g.shape[2] if planes else g.shape[1]
    nn = n // tn
    half = nn // 2
    tt = min(tt, t_len)

    def body(a_ref, g_ref, o_ref):
        @pl.when(pl.program_id(2) == 0)
        def _():
            o_ref[...] = jnp.zeros_like(o_ref)

        o_ref[...] += lax.dot_general(a_ref[...], g_ref[...], (((0,), (0,)), ((), ())),
                                      preferred_element_type=F32)

    if planes:
        g_spec = pl.BlockSpec((None, tt, tn), lambda i, j, t: (j // half, t, j % half))
    else:
        g_spec = pl.BlockSpec((tt, tn), lambda i, j, t: (t, j))
    return _pallas(
        body, name=name, grid=(k // tk, nn, t_len // tt),
        in_specs=[pl.BlockSpec((tt, tk), lambda i, j, t: (t, i)), g_spec],
        out_specs=[pl.BlockSpec((tk, tn), lambda i, j, t: (i, j))],
        out_shape=[jax.ShapeDtypeStruct((k, n), F32)],
        args=(a, g), sem=("parallel", "parallel", "arbitrary"), comm=comm)


def _lru_gates(rp, ip, spn):
    r = _sigmoid(rp)
    i = _sigmoid_tanh(ip)
    la = r * spn
    a = jnp.exp(la)
    mult = jnp.sqrt(_neg_expm1(2.0 * la, a))
    return r, i, a, mult


def _mixer_fwd(z, prm, gates, layer, name, tb=256, comm=None):
    t_len = z.shape[0]

    def body(z_ref, p_ref, g_ref, y_ref, h_ref, xhalo, phalo, hcar, lx_s, rp_s, ip_s):
        @pl.when(pl.program_id(0) == 0)
        def _():
            xhalo[...] = jnp.zeros_like(xhalo)
            phalo[...] = jnp.zeros_like(phalo)
            hcar[...] = jnp.zeros_like(hcar)

        prm_v = p_ref[...]
        cw = prm_v[0:4]
        vec = prm_v[4:8]
        xp = z_ref[:, 0:D_LRU].astype(F32)
        ext = jnp.concatenate([xhalo[...], xp], axis=0)
        lx = vec[0:1] + _conv_taps(ext, [cw[k:k + 1] for k in range(4)], tb)
        xhalo[...] = xp[tb - SUBLANES:]
        lx_s[...] = lx
        lxb = lx.astype(BF16)
        for q in range(4):
            sl = slice(q * 256, (q + 1) * 256)
            rp_s[:, sl] = jnp.dot(lxb[:, sl], g_ref[q], preferred_element_type=F32) + vec[1:2, sl]
            ip_s[:, sl] = jnp.dot(lxb[:, sl], g_ref[4 + q], preferred_element_type=F32) + vec[2:3, sl]

        spn = jnp.broadcast_to(-RG_C * _softplus_neg(vec[3:4]), (SUBLANES, D_LRU))
        row = lax.broadcasted_iota(jnp.int32, (SUBLANES, D_LRU), 0)

        def step(ci, carry):
            o = pl.multiple_of(ci * PACKED, PACKED)
            gate = z_ref[pl.ds(o, PACKED), D_LRU:2 * D_LRU].astype(F32)
            ys = []
            for sub in range(2):
                rows = pl.ds(pl.multiple_of(o + sub * SUBLANES, SUBLANES), SUBLANES)
                lxv = lx_s[rows, :]
                _, i, a, mult = _lru_gates(rp_s[rows, :], ip_s[rows, :], spn)
                h = _scan8(a, mult * (i * lxv), carry, row)
                h_ref[rows, :] = h
                ys.append(h * _gelu(gate[sub * SUBLANES:(sub + 1) * SUBLANES]))
                carry = jnp.broadcast_to(h[SUBLANES - 1:SUBLANES, :], (SUBLANES, D_LRU))
            y_ref[pl.ds(o, PACKED), 0:D_LRU] = jnp.concatenate(ys, axis=0).astype(BF16)
            return carry

        hcar[...] = lax.fori_loop(0, tb // PACKED, step, hcar[...])

        scw = prm_v[8:11, 0:D_SC]
        o_b, o_c, o_x = 2 * D_LRU, 2 * D_LRU + D_SC, 2 * D_LRU + 2 * D_SC
        p = z_ref[:, o_c:o_x].astype(F32) * z_ref[:, o_x:].astype(F32)
        pext = jnp.concatenate([phalo[...], p], axis=0)
        q = _conv_taps(pext, [scw[k:k + 1] for k in range(3)], tb)
        phalo[...] = p[tb - SUBLANES:]
        y_ref[:, D_LRU:] = (z_ref[:, o_b:o_c].astype(F32) * q).astype(BF16)

    return _pallas(
        body, name=name, grid=(t_len // tb,),
        in_specs=[pl.BlockSpec((tb, D_IN), lambda t: (t, 0)),
                  pl.BlockSpec((None, 2 * SUBLANES, D_LRU), lambda t: (layer, 0, 0)),
                  pl.BlockSpec((None, 8, 256, 256), lambda t: (layer, 0, 0, 0))],
        out_specs=[pl.BlockSpec((tb, D_MIX), lambda t: (t, 0)),
                   pl.BlockSpec((tb, D_LRU), lambda t: (t, 0))],
        out_shape=[jax.ShapeDtypeStruct((t_len, D_MIX), BF16),
                   jax.ShapeDtypeStruct((t_len, D_LRU), F32)],
        scratch_shapes=[pltpu.VMEM((SUBLANES, D_LRU), F32), pltpu.VMEM((SUBLANES, D_SC), F32),
                        pltpu.VMEM((SUBLANES, D_LRU), F32), pltpu.VMEM((tb, D_LRU), F32),
                        pltpu.VMEM((tb, D_LRU), F32), pltpu.VMEM((tb, D_LRU), F32)],
        args=(z, prm, gates), sem=("arbitrary",), comm=comm)


def _mixer_bwd(z, h, dy, prm, gates, layer, name, tb=256, comm=None):
    t_len = z.shape[0]
    nb = t_len // tb

    def body(z_ref, zh_ref, h_ref, hh_ref, dy_ref, p_ref, g_ref, dz_ref, dp_ref, dg_ref,
             lx_s, rp_s, ip_s, drpb_s, dipb_s, dlx_s, hext_s, acc_s, acar, gcar, dqh):
        t = pl.program_id(0)
        first_block = t == nb - 1

        @pl.when(t == 0)
        def _():
            for ref in (dp_ref, dg_ref, acc_s, acar, gcar, dqh):
                ref[...] = jnp.zeros_like(ref)
            dlx_s[tb:, :] = jnp.zeros((SUBLANES, D_LRU), F32)

        prm_v = p_ref[...]
        cw = prm_v[0:4]
        vec = prm_v[4:8]
        scw = prm_v[8:11, 0:D_SC]
        wa_ref = [g_ref.at[q] for q in range(4)]
        wx_ref = [g_ref.at[4 + q] for q in range(4)]
        dwa_ref = [dg_ref.at[q] for q in range(4)]
        dwx_ref = [dg_ref.at[4 + q] for q in range(4)]
        ctaps = [cw[k:k + 1] for k in range(4)]
        staps = [scw[k:k + 1] for k in range(3)]
        keep = jnp.where(first_block, 0.0, 1.0)
        zh = zh_ref[...].astype(F32)[PACKED - SUBLANES:] * keep

        xp = z_ref[:, 0:D_LRU].astype(F32)
        xext = jnp.concatenate([zh[:, 0:D_LRU], xp], axis=0)
        lx = vec[0:1] + _conv_taps(xext, ctaps, tb)
        lx_s[...] = lx
        lxb = lx.astype(BF16)
        for q in range(4):
            sl = slice(q * 256, (q + 1) * 256)
            rp_s[:, sl] = jnp.dot(lxb[:, sl], wa_ref[q][...], preferred_element_type=F32) + vec[1:2, sl]
            ip_s[:, sl] = jnp.dot(lxb[:, sl], wx_ref[q][...], preferred_element_type=F32) + vec[2:3, sl]
        hext_s[0:SUBLANES, :] = hh_ref[...] * keep
        hext_s[SUBLANES:, :] = h_ref[...]

        spn = jnp.broadcast_to(-RG_C * _softplus_neg(vec[3:4]), (SUBLANES, D_LRU))
        row = lax.broadcasted_iota(jnp.int32, (SUBLANES, D_LRU), 0)

        def step(ci, carry):
            a_next, g_next = carry
            o = pl.multiple_of((tb // PACKED - 1 - ci) * PACKED, PACKED)
            rows16 = pl.ds(o, PACKED)
            gate16 = z_ref[rows16, D_LRU:2 * D_LRU].astype(F32)
            dyl16 = dy_ref[rows16, 0:D_LRU].astype(F32)
            dgs, drs, dis = [None, None], [None, None], [None, None]
            for sub in (1, 0):
                oo = pl.multiple_of(o + sub * SUBLANES, SUBLANES)
                rows = pl.ds(oo, SUBLANES)
                half = slice(sub * SUBLANES, (sub + 1) * SUBLANES)
                lxv = lx_s[rows, :]
                r, i, a, mult = _lru_gates(rp_s[rows, :], ip_s[rows, :], spn)
                hwin = hext_s[pl.ds(oo, 2 * SUBLANES), :]
                hv = hwin[SUBLANES:]
                hprev = pltpu.roll(hwin, 1, axis=0)[SUBLANES:]
                gel, dgel = _gelu_parts(gate16[half])
                dyl = dyl16[half]
                a_up = jnp.where(row < SUBLANES - 1, pltpu.roll(a, SUBLANES - 1, axis=0), a_next)
                gg = _scan8_rev(a_up, dyl * gel, g_next, row)
                dgs[sub] = dyl * hv * dgel
                ilx = i * lxv
                dla = gg * hprev * a - (gg * ilx) * (a * a) / mult
                dlx_s[rows, :] = gg * mult * i
                drp = dla * spn * r * (1.0 - r)
                dip = gg * mult * lxv * i * (1.0 - i)
                drs[sub] = drp
                dis[sub] = dip
                acc_s[0] += drp
                acc_s[1] += dip
                acc_s[2] += dla * r
                a_next = jnp.broadcast_to(a[0:1, :], (SUBLANES, D_LRU))
                g_next = jnp.broadcast_to(gg[0:1, :], (SUBLANES, D_LRU))
            dz_ref[rows16, D_LRU:2 * D_LRU] = jnp.concatenate(dgs, axis=0).astype(BF16)
            drpb_s[rows16, :] = jnp.concatenate(drs, axis=0).astype(BF16)
            dipb_s[rows16, :] = jnp.concatenate(dis, axis=0).astype(BF16)
            return a_next, g_next

        a_c, g_c = lax.fori_loop(0, tb // PACKED, step, (acar[...], gcar[...]))
        acar[...] = a_c
        gcar[...] = g_c

        drpb = drpb_s[...]
        dipb = dipb_s[...]
        nt_dims = (((1,), (1,)), ((), ()))
        tn_dims = (((0,), (0,)), ((), ()))
        for q in range(4):
            sl = slice(q * 256, (q + 1) * 256)
            dlx_s[0:tb, sl] += (
                lax.dot_general(drpb[:, sl], wa_ref[q][...], nt_dims, preferred_element_type=F32)
                + lax.dot_general(dipb[:, sl], wx_ref[q][...], nt_dims, preferred_element_type=F32))
            dwa_ref[q][...] += lax.dot_general(lxb[:, sl], drpb[:, sl], tn_dims, preferred_element_type=F32)
            dwx_ref[q][...] += lax.dot_general(lxb[:, sl], dipb[:, sl], tn_dims, preferred_element_type=F32)

        dlx_ext = dlx_s[...]
        dlx = dlx_ext[0:tb]
        dz_ref[:, 0:D_LRU] = _conv_taps_t(dlx_ext, ctaps, tb).astype(BF16)
        dp_ref[3:4, :] += jnp.sum(dlx * xp, axis=0, keepdims=True)
        for k in range(3):
            shifted = pltpu.roll(xext, 3 - k, axis=0)[SUBLANES:]
            dp_ref[k:k + 1, :] += jnp.sum(dlx * shifted, axis=0, keepdims=True)
        dp_ref[4:5, :] += jnp.sum(dlx, axis=0, keepdims=True)
        dlx_s[tb:, :] = dlx[0:SUBLANES]

        o_b, o_c, o_x = 2 * D_LRU, 2 * D_LRU + D_SC, 2 * D_LRU + 2 * D_SC
        sb = z_ref[:, o_b:o_c].astype(F32)
        scc = z_ref[:, o_c:o_x].astype(F32)
        sx = z_ref[:, o_x:].astype(F32)
        p = scc * sx
        pext = jnp.concatenate([zh[:, o_c:o_x] * zh[:, o_x:], p], axis=0)
        q = _conv_taps(pext, staps, tb)
        dys = dy_ref[:, D_LRU:].astype(F32)
        dq = dys * sb
        dp = _conv_taps_t(jnp.concatenate([dq, dqh[...]], axis=0), staps, tb)
        dp_ref[10:11, 0:D_SC] += jnp.sum(dq * p, axis=0, keepdims=True)
        for k in range(2):
            shifted = pltpu.roll(pext, 2 - k, axis=0)[SUBLANES:]
            dp_ref[8 + k:9 + k, 0:D_SC] += jnp.sum(dq * shifted, axis=0, keepdims=True)
        dqh[...] = dq[0:SUBLANES]
        dz_ref[:, o_b:o_c] = (dys * q).astype(BF16)
        dz_ref[:, o_c:o_x] = (dp * sx).astype(BF16)
        dz_ref[:, o_x:] = (dp * scc).astype(BF16)

        @pl.when(first_block)
        def _():
            dp_ref[5:6, :] = jnp.sum(acc_s[0], axis=0, keepdims=True)
            dp_ref[6:7, :] = jnp.sum(acc_s[1], axis=0, keepdims=True)
            dp_ref[7:8, :] = (jnp.sum(acc_s[2], axis=0, keepdims=True) * RG_C * _sigmoid(-vec[3:4]))

    blk = lambda t: (nb - 1 - t, 0)
    halo8 = lambda t: (jnp.maximum((nb - 1 - t) * (tb // SUBLANES) - 1, 0), 0)
    halo16 = lambda t: (jnp.maximum((nb - 1 - t) * (tb // PACKED) - 1, 0), 0)
    return _pallas(
        body, name=name, grid=(nb,),
        in_specs=[pl.BlockSpec((tb, D_IN), blk), pl.BlockSpec((PACKED, D_IN), halo16),
                  pl.BlockSpec((tb, D_LRU), blk), pl.BlockSpec((SUBLANES, D_LRU), halo8),
                  pl.BlockSpec((tb, D_MIX), blk),
                  pl.BlockSpec((None, 2 * SUBLANES, D_LRU), lambda t: (layer, 0, 0)),
                  pl.BlockSpec((None, 8, 256, 256), lambda t: (layer, 0, 0, 0))],
        out_specs=[pl.BlockSpec((tb, D_IN), blk),
                   pl.BlockSpec((2 * SUBLANES, D_LRU), lambda t: (0, 0)),
                   pl.BlockSpec((8, 256, 256), lambda t: (0, 0, 0))],
        out_shape=[jax.ShapeDtypeStruct((t_len, D_IN), BF16),
                   jax.ShapeDtypeStruct((2 * SUBLANES, D_LRU), F32),
                   jax.ShapeDtypeStruct((8, 256, 256), F32)],
        scratch_shapes=[pltpu.VMEM((tb, D_LRU), F32),
                        pltpu.VMEM((tb, D_LRU), F32), pltpu.VMEM((tb, D_LRU), F32),
                        pltpu.VMEM((tb, D_LRU), BF16), pltpu.VMEM((tb, D_LRU), BF16),
                        pltpu.VMEM((tb + SUBLANES, D_LRU), F32), pltpu.VMEM((tb + SUBLANES, D_LRU), F32),
                        pltpu.VMEM((3, SUBLANES, D_LRU), F32),
                        pltpu.VMEM((SUBLANES, D_LRU), F32), pltpu.VMEM((SUBLANES, D_LRU), F32),
                        pltpu.VMEM((SUBLANES, D_SC), F32)],
        args=(z, z, h, h, dy, prm, gates), sem=("arbitrary",), comm=comm)


def _ffn_act(u, fw, layer, name, tb=512, tn=1024, rc=64, comm=None):
    t_len = u.shape[1]
    hb = tb // PACKED

    def body(u_ref, uh_ref, fw_ref, o_ref, fg_ref, fu_ref, ext):
        keep = jnp.where(pl.program_id(0) == 0, 0.0, 1.0)
        ext[:, 0:SUBLANES, :] = uh_ref[...].astype(F32)[:, PACKED - SUBLANES:, :] * keep
        ext[:, SUBLANES:, :] = u_ref[...].astype(F32)
        fw_v = fw_ref[...]

        for lb in range(tn // LANES):
            lanes = slice(lb * LANES, (lb + 1) * LANES)
            wg = [fw_v[0, k:k + 1, lanes] for k in range(3)]
            wu = [fw_v[1, k:k + 1, lanes] for k in range(3)]

            def chunk(ci, c, lanes=lanes, wg=wg, wu=wu):
                o = pl.multiple_of(ci * rc, rc)
                win = pl.ds(o, rc + SUBLANES)
                gate = _conv_taps(ext[0, win, lanes], wg, rc)
                up = _conv_taps(ext[1, win, lanes], wu, rc)
                gel, dgel = _gelu_parts(gate)
                rows = pl.ds(o, rc)
                o_ref[rows, lanes] = (gel * up).astype(BF16)
                fg_ref[rows, lanes] = (up * dgel).astype(BF16)
                fu_ref[rows, lanes] = gel.astype(BF16)
                return c

            lax.fori_loop(0, tb // rc, chunk, 0)

    spec = pl.BlockSpec((tb, tn), lambda i, j: (i, j))
    shape = jax.ShapeDtypeStruct((t_len, D_FF), BF16)
    return _pallas(
        body, name=name, grid=(t_len // tb, D_FF // tn),
        in_specs=[pl.BlockSpec((2, tb, tn), lambda i, j: (0, i, j)),
                  pl.BlockSpec((2, PACKED, tn), lambda i, j: (0, jnp.maximum(i * hb - 1, 0), j)),
                  pl.BlockSpec((None, 2, SUBLANES, tn), lambda i, j: (layer, 0, 0, j))],
        out_specs=[spec] * 3, out_shape=[shape] * 3,
        scratch_shapes=[pltpu.VMEM((2, tb + SUBLANES, tn), F32)],
        args=(u, u, fw), sem=("parallel", "parallel"), comm=comm)


def _ffn_bwd(dact, fg, fu, u, fw, layer, name, tb=512, tn=1024, rc=32, comm=None):
    t_len = u.shape[1]
    ni = t_len // tb
    hb = tb // PACKED
    last_halo = t_len // PACKED - 1

    def body(d_ref, dn_ref, fg_ref, fgn_ref, fu_ref, fun_ref, u_ref, up_ref, fw_ref, du_ref, dfw_ref,
             extu, extp, acc):
        i = pl.program_id(1)

        @pl.when(i == 0)
        def _():
            acc[...] = jnp.zeros_like(acc)

        keep_prev = jnp.where(i == 0, 0.0, 1.0)
        keep_next = jnp.where(i == ni - 1, 0.0, 1.0)
        extu[:, 0:SUBLANES, :] = up_ref[...].astype(F32)[:, PACKED - SUBLANES:, :] * keep_prev
        extu[:, SUBLANES:, :] = u_ref[...].astype(F32)
        dv = d_ref[...].astype(F32)
        dn = dn_ref[...].astype(F32)[0:SUBLANES] * keep_next
        extp[0, 0:tb, :] = dv * fg_ref[...].astype(F32)
        extp[0, tb:, :] = dn * fgn_ref[...].astype(F32)[0:SUBLANES]
        extp[1, 0:tb, :] = dv * fu_ref[...].astype(F32)
        extp[1, tb:, :] = dn * fun_ref[...].astype(F32)[0:SUBLANES]
        fw_v = fw_ref[...]
        m = rc + SUBLANES

        for lb in range(tn // LANES):
            lanes = slice(lb * LANES, (lb + 1) * LANES)
            taps = [[fw_v[pln, k:k + 1, lanes] for k in range(3)] for pln in range(2)]

            def chunk(ci, c, lanes=lanes, taps=taps):
                o = pl.multiple_of(ci * rc, rc)
                for pln in range(2):
                    e = extu[pln, pl.ds(o, m), lanes]
                    sh = [pltpu.roll(e, 2, axis=0)[SUBLANES:], pltpu.roll(e, 1, axis=0)[SUBLANES:], e[SUBLANES:]]
                    dpost = extp[pln, pl.ds(o, m), lanes]
                    du_ref[pln, pl.ds(o, rc), lanes] = _conv_taps_t(dpost, taps[pln], rc).astype(BF16)
                    for k in range(3):
                        prod = dpost[0:rc] * sh[k]
                        acc[3 * pln + k, :, lanes] += sum(
                            prod[s:s + SUBLANES] for s in range(0, rc, SUBLANES))
                return c

            lax.fori_loop(0, tb // rc, chunk, 0)

        @pl.when(i == ni - 1)
        def _():
            dfw_ref[...] = jnp.zeros_like(dfw_ref)
            for pln in range(2):
                for k in range(3):
                    dfw_ref[pln, k:k + 1, :] = jnp.sum(acc[3 * pln + k], axis=0, keepdims=True)

    main = pl.BlockSpec((tb, tn), lambda j, i: (i, j))
    nxt = pl.BlockSpec((PACKED, tn), lambda j, i: (jnp.minimum((i + 1) * hb, last_halo), j))
    return _pallas(
        body, name=name, grid=(D_FF // tn, ni),
        in_specs=[main, nxt, main, nxt, main, nxt,
                  pl.BlockSpec((2, tb, tn), lambda j, i: (0, i, j)),
                  pl.BlockSpec((2, PACKED, tn), lambda j, i: (0, jnp.maximum(i * hb - 1, 0), j)),
                  pl.BlockSpec((None, 2, SUBLANES, tn), lambda j, i: (layer, 0, 0, j))],
        out_specs=[pl.BlockSpec((2, tb, tn), lambda j, i: (0, i, j)),
                   pl.BlockSpec((2, SUBLANES, tn), lambda j, i: (0, 0, j))],
        out_shape=[jax.ShapeDtypeStruct((2, t_len, D_FF), BF16),
                   jax.ShapeDtypeStruct((2, SUBLANES, D_FF), F32)],
        scratch_shapes=[pltpu.VMEM((2, tb + SUBLANES, tn), F32),
                        pltpu.VMEM((2, tb + SUBLANES, tn), F32),
                        pltpu.VMEM((6, SUBLANES, tn), F32)],
        args=(dact, dact, fg, fg, fu, fu, u, u, fw), sem=("parallel", "arbitrary"), comm=comm)


def _loss_head(x, g, target, name, tb=256):
    t_len, d = x.shape

    def body(x_ref, g_ref, t_ref, dx_ref, dxb_ref, dg_ref, loss_ref):
        @pl.when(pl.program_id(0) == 0)
        def _():
            dg_ref[...] = jnp.zeros_like(dg_ref)
            loss_ref[...] = jnp.zeros_like(loss_ref)

        xv = x_ref[...]
        gv = g_ref[...]
        r = lax.rsqrt(jnp.mean(xv * xv, axis=-1, keepdims=True) + EPS)
        xh = xv * r
        err = xh * gv - t_ref[...]
        loss_ref[...] += (0.5 / d) * jnp.sum(jnp.sum(err * err, axis=-1, keepdims=True), axis=0, keepdims=True)
        dy = err * (1.0 / d)
        dyg = dy * gv
        dx = r * (dyg - xh * jnp.mean(dyg * xh, axis=-1, keepdims=True))
        dx_ref[...] = dx
        dxb_ref[...] = dx.astype(BF16)
        dg_ref[0:1, :] += jnp.sum(dy * xh, axis=0, keepdims=True)

    return _pallas(
        body, name=name, grid=(t_len // tb,),
        in_specs=[pl.BlockSpec((tb, d), lambda i: (i, 0)), pl.BlockSpec((1, d), lambda i: (0, 0)),
                  pl.BlockSpec((tb, d), lambda i: (i, 0))],
        out_specs=[pl.BlockSpec((tb, d), lambda i: (i, 0)), pl.BlockSpec((tb, d), lambda i: (i, 0)),
                   pl.BlockSpec((SUBLANES, d), lambda i: (0, 0)),
                   pl.BlockSpec((SUBLANES, LANES), lambda i: (0, 0))],
        out_shape=[jax.ShapeDtypeStruct((t_len, d), F32), jax.ShapeDtypeStruct((t_len, d), BF16),
                   jax.ShapeDtypeStruct((SUBLANES, d), F32), jax.ShapeDtypeStruct((SUBLANES, LANES), F32)],
        args=(x, g, target), sem=("arbitrary",))[0]


def _adamw(w, g, m, v, name, emit_grad=False, comm=None):
    r, c = w.shape
    tr = 256 if r % 256 == 0 else r
    c1 = 1.0 / (1.0 - ADAM_B1 ** ADAM_STEP)
    c2 = 1.0 / (1.0 - ADAM_B2 ** ADAM_STEP)

    def body(w_ref, g_ref, m_ref, v_ref, d_ref, mo_ref, vo_ref, *go_ref):
        gv = g_ref[...]
        mn = ADAM_B1 * m_ref[...] + (1.0 - ADAM_B1) * gv
        vn = ADAM_B2 * v_ref[...] + (1.0 - ADAM_B2) * (gv * gv)
        d_ref[...] = -ADAM_LR * ((mn * c1) / (jnp.sqrt(vn * c2) + ADAM_EPS) + ADAM_WD * w_ref[...])
        mo_ref[...] = mn
        vo_ref[...] = vn
        if emit_grad:
            go_ref[0][...] = gv

    spec = pl.BlockSpec((tr, c), lambda i: (i, 0))
    shape = jax.ShapeDtypeStruct((r, c), F32)
    n_out = 4 if emit_grad else 3
    return _pallas(
        body, name=name, grid=(r // tr,),
        in_specs=[spec] * 4, out_specs=[spec] * n_out, out_shape=[shape] * n_out,
        args=(w, g, m, v), sem=("parallel",), comm=comm)


def _place():
    x, y, c = lax.axis_index("x"), lax.axis_index("y"), lax.axis_index("c")
    chips = [(1 - x, y), (x, 1 - y), (1 - x, 1 - y)]
    return x, y, c, chips


def _remote(src, dst, send, recv, sem, to):
    return pltpu.make_async_remote_copy(
        src_ref=src, dst_ref=dst, send_sem=send.at[sem], recv_sem=recv.at[sem], device_id=to, device_id_type=MESH)


def _gather_plan(fulls, kinds, mid_at=None, parts=None):
    parts = parts or [(0, 1)] * len(fulls)

    def region(it, f, k, cc):
        kind = kinds[it]
        p, n = parts[it][0:2]
        count = parts[it][2] if len(parts[it]) > 2 else 1
        if kind == SMALL:
            return f.at[k, pl.ds(cc * (CONV_PACK_ROWS // 2), CONV_PACK_ROWS // 2), :]
        if COL_SHARDED[kind]:
            rows, cols = f.shape[0] // (2 * n), f.shape[1] // N_CHIP
            return f.at[pl.ds((cc * n + p) * rows, count * rows), pl.ds(k * cols, cols)]
        assert n == 1
        rows = f.shape[0] // N_CHIP
        return f.at[pl.ds(k * rows + cc * (rows // 2), rows // 2), :]

    def first_hop(bufs, send, recv, it, j):
        x, y, c, chips = _place()
        reg = region(it, bufs[it], 2 * x + y, c)
        return _remote(reg, reg, send, recv, it * 6 + j, (*chips[j], c))

    def arrival(bufs, send, recv, it, j, second):
        x, y, c, chips = _place()
        px, py = chips[j]
        reg = region(it, bufs[it], 2 * px + py, 1 - c if second else c)
        to = (x, y, 1 - c) if second else (px, py, c)
        return _remote(reg, reg, send, recv, it * 6 + (3 + j if second else j), to)

    def forward(bufs, send, recv, it, j):
        x, y, c, chips = _place()
        px, py = chips[j]
        reg = region(it, bufs[it], 2 * px + py, c)
        return _remote(reg, reg, send, recv, it * 6 + 3 + j, (x, y, 1 - c))

    def start(srcs, bufs, outs, send, recv):
        for it in range(len(bufs)):
            for j in range(3):
                first_hop(bufs, send, recv, it, j).start()

    def mid(srcs, bufs, outs, send, recv):
        for it in range(len(bufs)):
            for j in range(3):
                arrival(bufs, send, recv, it, j, False).wait_recv()
                forward(bufs, send, recv, it, j).start()

    def finish(srcs, bufs, outs, send, recv):
        for it in range(len(bufs)):
            for j in range(3):
                arrival(bufs, send, recv, it, j, True).wait_recv()
        for it in range(len(bufs)):
            for j in range(3):
                first_hop(bufs, send, recv, it, j).wait_send()
                forward(bufs, send, recv, it, j).wait_send()

    return Comm(srcs=(), bufs=tuple(fulls), outs=(), n_sem=6 * len(fulls), start=start, mid=mid, finish=finish,
                mid_at=mid_at)


def _half_axis(kind):
    return 0 if kind == SMALL or COL_SHARDED[kind] else 1


def _half2(ref, kind, cc):
    if _half_axis(kind) == 0:
        return ref.at[pl.ds(cc * (ref.shape[0] // 2), ref.shape[0] // 2), :]
    return ref.at[:, pl.ds(cc * (ref.shape[1] // 2), ref.shape[1] // 2)]


def _pair_plan(grads, kinds):
    def land_shape(g, kind):
        s = list(g.shape)
        s[_half_axis(kind)] //= 2
        return jax.ShapeDtypeStruct(tuple(s), F32)

    def copy(srcs, outs, send, recv, it):
        x, y, c, _ = _place()
        return _remote(_half2(srcs[it], kinds[it], 1 - c), outs[it], send, recv, it, (x, y, 1 - c))

    def start(srcs, bufs, outs, send, recv):
        for it in range(len(srcs)):
            copy(srcs, outs, send, recv, it).start()

    def finish(srcs, bufs, outs, send, recv):
        for it in range(len(srcs)):
            copy(srcs, outs, send, recv, it).wait_send()
        for it in range(len(srcs)):
            copy(srcs, outs, send, recv, it).wait_recv()

    return Comm(srcs=tuple(grads), bufs=(), outs=tuple(land_shape(g, k) for g, k in zip(grads, kinds)),
                n_sem=len(grads), start=start, finish=finish)


def _scatter_plan(parts, slots, kinds):
    def piece(s, kind, k):
        if kind == SMALL:
            return s
        if COL_SHARDED[kind]:
            n = s.shape[1] // N_CHIP
            return s.at[:, pl.ds(k * n, n)]
        n = s.shape[0] // N_CHIP
        return s.at[pl.ds(k * n, n), :]

    def outbound(srcs, bufs, send, recv, it, j):
        x, y, c, chips = _place()
        px, py = chips[j]
        return _remote(piece(srcs[it], kinds[it], 2 * px + py), bufs[it].at[2 * x + y], send, recv, it * 3 + j,
                       (px, py, c))

    def inbound(bufs, send, recv, it, j):
        x, y, c, chips = _place()
        px, py = chips[j]
        got = bufs[it].at[2 * px + py]
        return _remote(got, got, send, recv, it * 3 + j, (px, py, c))

    def start(srcs, bufs, outs, send, recv):
        for it in range(len(srcs)):
            for j in range(3):
                outbound(srcs, bufs, send, recv, it, j).start()

    def finish(srcs, bufs, outs, send, recv):
        for it in range(len(srcs)):
            for j in range(3):
                inbound(bufs, send, recv, it, j).wait_recv()
        for it in range(len(srcs)):
            for j in range(3):
                outbound(srcs, bufs, send, recv, it, j).wait_send()

    return Comm(srcs=tuple(parts), bufs=tuple(slots), outs=(), n_sem=3 * len(parts), start=start, finish=finish)


def _share_plan(fulls, kinds, layer):
    def half(f, kind, cc):
        return _half2(f if kind == SMALL else f.at[layer], kind, cc)

    def copy(bufs, send, recv, it, cc):
        x, y, c, _ = _place()
        reg = half(bufs[it], kinds[it], c if cc == "mine" else 1 - c)
        return _remote(reg, reg, send, recv, it, (x, y, 1 - c))

    def start(srcs, bufs, outs, send, recv):
        for it in range(len(bufs)):
            copy(bufs, send, recv, it, "mine").start()

    def finish(srcs, bufs, outs, send, recv):
        for it in range(len(bufs)):
            copy(bufs, send, recv, it, "other").wait_recv()
        for it in range(len(bufs)):
            copy(bufs, send, recv, it, "mine").wait_send()

    return Comm(srcs=(), bufs=tuple(fulls), outs=(), n_sem=len(fulls), start=start, finish=finish)


def _pair_sum(g, land, idx, kind, name):
    odt = F32 if kind == SMALL else BF16
    r, cdim = land.shape

    def body(idx_ref, g_ref, l_ref, p_ref, s_ref):
        v = (g_ref[...] + l_ref[...]).astype(odt)
        p_ref[...] = v
        if kind == SMALL:
            s_ref[...] = v
        else:
            @pl.when(pl.program_id(1 if COL_SHARDED[kind] else 0) == idx_ref[1])
            def _():
                s_ref[...] = v

    if kind == SMALL:
        grid = (1,)
        g_spec = pl.BlockSpec((r, LANES), lambda i, idx_ref: (idx_ref[0], 0))
        spec = pl.BlockSpec((r, LANES), lambda i, idx_ref: (0, 0))
        s_spec = pl.BlockSpec((None, r, LANES), lambda i, idx_ref: (idx_ref[1], 0, 0))
        s_shape = (N_CHIP, r, LANES)
    elif COL_SHARDED[kind]:
        pc, tr = cdim // N_CHIP, 256
        nrb = r // tr
        grid = (nrb, N_CHIP)
        g_spec = pl.BlockSpec((tr, pc), lambda i, k, idx_ref: (idx_ref[0] * nrb + i, k))
        spec = pl.BlockSpec((tr, pc), lambda i, k, idx_ref: (i, k))
        s_spec = pl.BlockSpec((None, tr, pc), lambda i, k, idx_ref: (idx_ref[1], i, 0))
        s_shape = (N_CHIP, r, pc)
    else:
        pr = r // N_CHIP
        grid = (N_CHIP,)
        g_spec = pl.BlockSpec((pr, cdim), lambda k, idx_ref: (k, idx_ref[0]))
        spec = pl.BlockSpec((pr, cdim), lambda k, idx_ref: (k, 0))
        s_spec = pl.BlockSpec((None, pr, cdim), lambda k, idx_ref: (idx_ref[1], 0, 0))
        s_shape = (N_CHIP, pr, cdim)
    return pl.pallas_call(
        body, name=name,
        grid_spec=pltpu.PrefetchScalarGridSpec(
            num_scalar_prefetch=1, grid=grid, in_specs=[g_spec, spec], out_specs=[spec, s_spec]),
        out_shape=[jax.ShapeDtypeStruct(land.shape, odt), jax.ShapeDtypeStruct(s_shape, odt)],
        compiler_params=_cp(*(["arbitrary"] * len(grid))),
    )(idx, g, land)


def _sum_slots(slots, idx, kind, layer, prev, name):
    _, r, cdim = slots.shape

    def body(*refs):
        s_ref, o_ref = refs[1], refs[-1]
        v = s_ref[...].astype(F32)
        o_ref[...] = (v[0] + v[1]) + (v[2] + v[3])

    if kind == SMALL:
        grid = (1,)
        s_spec = pl.BlockSpec((N_CHIP, r, cdim), lambda i, idx_ref: (0, 0, 0))
        o_spec = pl.BlockSpec((r, cdim), lambda i, idx_ref: (idx_ref[0], 0))
        full = (2 * r, cdim)
    else:
        tr = 256 if r % 256 == 0 else 384
        nrb = r // tr
        grid = (nrb,)
        s_spec = pl.BlockSpec((N_CHIP, tr, cdim), lambda i, idx_ref: (0, i, 0))
        if COL_SHARDED[kind]:
            o_spec = pl.BlockSpec((None, tr, cdim), lambda i, idx_ref: (layer, idx_ref[0] * nrb + i, 0))
            full = (2, 2 * r, cdim)
        else:
            o_spec = pl.BlockSpec((None, tr, cdim), lambda i, idx_ref: (layer, i, idx_ref[0]))
            full = (2, r, 2 * cdim)
    in_specs, args, aliases = [s_spec], [idx, slots], {}
    if prev is not None:
        in_specs.append(ANY)
        args.append(prev)
        aliases = {2: 0}
    return pl.pallas_call(
        body, name=name,
        grid_spec=pltpu.PrefetchScalarGridSpec(
            num_scalar_prefetch=1, grid=grid, in_specs=in_specs, out_specs=o_spec),
        out_shape=jax.ShapeDtypeStruct(full, F32),
        input_output_aliases=aliases,
        compiler_params=_cp(*(["parallel"] * len(grid))),
    )(*args)


def _block_diag(w):
    w4 = w.reshape(2, 4, 4, 64, 64)
    eye = jnp.eye(4, dtype=w.dtype)[None, None, :, None, :, None]
    return (w4[:, :, :, :, None, :] * eye).reshape(2, 4, 256, 256)


def _block_diag_extract(d):
    d5 = d.reshape(4, 4, 64, 4, 64)
    return jnp.stack([d5[:, hh, :, hh, :] for hh in range(4)], axis=1).reshape(-1)


REP_NAMES = ("norm1_g", "lru_conv_b", "lru_ba", "lru_bx", "lru_lambda", "norm2_g", "lru_wa", "lru_wx")


def _pack_rep(norm1_g, conv_b, ba, bx, lam, norm2_g, wa, wx, final_g):
    parts = [a.reshape(-1) for a in (norm1_g, conv_b, ba, bx, lam, norm2_g, wa, wx, final_g)]
    return jnp.concatenate(parts).reshape(REP_ROWS, LANES)


def _unpack_rep(buf):
    flat = buf.reshape(-1)
    res, o = {}, 0
    for k in REP_NAMES:
        shape = (2, 16, 64, 64) if k in ("lru_wa", "lru_wx") else (2, 1024)
        n = math.prod(shape)
        res[k] = flat[o:o + n].reshape(shape)
        o += n
    res["final_g"] = flat[o:o + 1024]
    return res


def _pack_conv_shard(lru_cw, sc_cw, ffn_cw):
    return jnp.concatenate([lru_cw.reshape(16, LANES), jnp.pad(sc_cw.reshape(6, LANES), ((0, 2), (0, 0))),
                            ffn_cw.reshape(72, LANES)], axis=0)


def _unpack_conv_shard(buf):
    return (buf[0:16].reshape(2, 4, 256), buf[16:22].reshape(2, 3, 128), buf[24:96].reshape(2, 3, 1536))


def kernel(x, norm1_g, w_in, lru_conv_w, lru_conv_b, lru_wa, lru_ba, lru_wx, lru_bx, lru_lambda, sc_conv_w, w_out, norm2_g, w_up, ffn_conv_w, w_down, final_g, loss_target, m_norm1_g, m_w_in, m_lru_conv_w, m_lru_conv_b, m_lru_wa, m_lru_ba, m_lru_wx, m_lru_bx, m_lru_lambda, m_sc_conv_w, m_w_out, m_norm2_g, m_w_up, m_ffn_conv_w, m_w_down, m_final_g, v_norm1_g, v_w_in, v_lru_conv_w, v_lru_conv_b, v_lru_wa, v_lru_ba, v_lru_wx, v_lru_bx, v_lru_lambda, v_sc_conv_w, v_w_out, v_norm2_g, v_w_up, v_ffn_conv_w, v_w_down, v_final_g):
    me = 2 * lax.axis_index("x") + lax.axis_index("y")
    idx = jnp.stack([lax.axis_index("c"), me]).astype(jnp.int32)
    t_len = x.shape[1]

    s_conv = _pack_conv_shard(lru_conv_w, sc_conv_w, ffn_conv_w)
    conv_slots = lax.dynamic_update_slice(jnp.zeros((N_CHIP, CONV_PACK_ROWS, LANES), F32), s_conv[None], (me, 0, 0))
    wi = list(_cast_into_full(w_in, W_IN, idx, "cast_w_in"))
    wo = list(_cast_into_full(w_out, W_OUT, idx, "cast_w_out"))
    wu = list(_cast_into_full(w_up, W_UP, idx, "cast_w_up"))
    wd = list(_cast_into_full(w_down, W_DOWN, idx, "cast_w_down"))
    wi[0], convs = _comm_call(_gather_plan([wi[0], conv_slots], [W_IN, SMALL]), "ag_first")
    per_chip = [_unpack_conv_shard(convs[k]) for k in range(N_CHIP)]
    lru_cw = jnp.concatenate([p[0] for p in per_chip], axis=-1)
    sc_cw = jnp.concatenate([p[1] for p in per_chip], axis=-1)
    ffn_cw = jnp.concatenate([p[2] for p in per_chip], axis=-1)

    prm = jnp.concatenate(
        [lru_cw, jnp.stack([lru_conv_b, lru_ba, lru_bx, lru_lambda], axis=1),
         jnp.pad(sc_cw, ((0, 0), (0, 0), (0, D_LRU - D_SC))), jnp.zeros((2, 5, D_LRU), F32)], axis=1)
    gates = jnp.concatenate([_block_diag(lru_wa), _block_diag(lru_wx)], axis=1).astype(BF16)
    fw8 = jnp.pad(ffn_cw.reshape(2, 3, 2, D_FF).transpose(0, 2, 1, 3), ((0, 0), (0, 0), (0, 5), (0, 0)))

    xs = x[0]
    saved = []
    n512, n256 = t_len // 512, t_len // 256
    whole, lower, upper = (0, 1), (0, 2), (1, 2)
    carried_by = {
        "fwd_in_0": ([(wu, 0, W_UP, (0, 4))], (max(n512 - 3, 0),)),
        "fwd_mixer_0": ([(wu, 0, W_UP, (1, 4, 2)), (wo, 0, W_OUT, whole)], (max(n256 - 3, 0),)),
        "fwd_out_0": ([(wu, 0, W_UP, (3, 4))], (max(n512 - 2, 0),)),
        "fwd_up_0": ([(wd, 0, W_DOWN, whole)], (max(n512 - 2, 0),)),
        "fwd_act_0": ([(wi, 1, W_IN, whole), (wo, 1, W_OUT, whole)], (max(n512 - 2, 0), 0)),
        "fwd_down_0": ([(wu, 1, W_UP, (0, 4))], (max(n512 - 3, 0),)),
        "fwd_in_1": ([(wu, 1, W_UP, (1, 4))], (max(n512 - 3, 0),)),
        "fwd_mixer_1": ([(wu, 1, W_UP, (2, 4, 2))], (max(n256 - 4, 0),)),
        "fwd_act_1": ([(wd, 1, W_DOWN, whole)], (max(n512 - 3, 0), 0)),
    }

    def carried(name):
        if name not in carried_by:
            return None, lambda got: None
        items, mid_at = carried_by[name]

        def store(got):
            for (lst, i, _, _), arr in zip(items, got):
                lst[i] = arr

        return _gather_plan([lst[i] for lst, i, _, _ in items], [k for _, _, k, _ in items], mid_at=mid_at,
                            parts=[p for _, _, _, p in items]), store

    for l in range(2):
        comm, store = carried(f"fwd_in_{l}")
        (z, h1), got = _norm_mm(xs, norm1_g[l][None], wi[l], f"fwd_in_{l}", comm=comm)
        store(got)
        comm, store = carried(f"fwd_mixer_{l}")
        (ymix, hst), got = _mixer_fwd(z, prm, gates, l, f"fwd_mixer_{l}", comm=comm)
        store(got)
        comm, store = carried(f"fwd_out_{l}")
        (x2,), got = _mm_res(ymix, wo[l], xs, f"fwd_out_{l}", comm=comm)
        store(got)
        comm, store = carried(f"fwd_up_{l}")
        (u, h2), got = _norm_mm(x2, norm2_g[l][None], wu[l], f"fwd_up_{l}", planes=True, comm=comm)
        store(got)
        comm, store = carried(f"fwd_act_{l}")
        (act, fg, fu), got = _ffn_act(u, fw8, l, f"fwd_act_{l}", comm=comm)
        store(got)
        comm, store = carried(f"fwd_down_{l}")
        (x3,), got = _mm_res(act, wd[l], x2, f"fwd_down_{l}", comm=comm)
        store(got)
        saved.append((xs, h1, z, hst, ymix, x2, h2, u, act, fg, fu))
        xs = x3

    dx, dxb, dgf, loss_blk = _loss_head(xs, final_g[None], loss_target[0], "loss_head")

    kinds = [W_IN, W_OUT, W_UP, W_DOWN]
    grads = [None, None]
    small = [None, None]
    reduced = [None] * 4
    summed1 = [None] * 4
    slots1 = [None] * 4

    def scatter1(ws):
        return _scatter_plan([summed1[w][0] for w in ws], [summed1[w][1] for w in ws], ws)

    for l in (1, 0):
        x_in, h1, z, hst, ymix, x2, h2, u, act, fg, fu = saved[l]
        carry = l == 0
        comm = _pair_plan([grads[1][W_IN]], [W_IN]) if carry else None
        (g_down,), got = _mm_tn(act, dxb, f"bwd_wdown_{l}", tk=1536, tn=1024, comm=comm)
        if carry:
            summed1[W_IN] = _pair_sum(grads[1][W_IN], got[0], idx, W_IN, "rs_add1_0")
        (dact,), _ = _mm_nt(dxb, wd[l], f"bwd_dact_{l}")
        comm = scatter1((W_UP, W_IN)) if carry else None
        (du, dfw), got = _ffn_bwd(dact, fg, fu, u, fw8, l, f"bwd_act_{l}", comm=comm)
        if carry:
            slots1[W_UP], slots1[W_IN] = got
            reduced = [_sum_slots(slots1[w], idx, kinds[w], 1, None, f"rs_sum1_{w}") for w in range(4)]
        comm = _share_plan(reduced, kinds, 1) if carry else None
        (g_up,), got = _mm_tn(h2, du, f"bwd_wup_{l}", tk=1024, tn=1536, planes=True, comm=comm)
        if carry:
            reduced = list(got)
        comm = _pair_plan([g_up, g_down], [W_UP, W_DOWN]) if carry else None
        (dx2, dx2b, dg2), got = _mm_nt_normbwd(du, wu[l], x2, norm2_g[l][None], dx, f"bwd_up_{l}", planes=True,
                                               comm=comm)
        if carry:
            sum_up = _pair_sum(g_up, got[0], idx, W_UP, "rs_add0_2")
            sum_down = _pair_sum(g_down, got[1], idx, W_DOWN, "rs_add0_3")
        (g_out,), _ = _mm_tn(ymix, dx2b, f"bwd_wout_{l}", tk=1536, tn=1024)
        comm = _pair_plan([g_out], [W_OUT]) if carry else None
        (dymix,), got = _mm_nt(dx2b, wo[l], f"bwd_dymix_{l}", comm=comm)
        trio = (W_OUT, W_UP, W_DOWN)
        if carry:
            sum_out = _pair_sum(g_out, got[0], idx, W_OUT, "rs_add0_1")
            comm = _scatter_plan([sum_out[0], sum_up[0], sum_down[0]], [sum_out[1], sum_up[1], sum_down[1]], trio)
        else:
            comm = _pair_plan([g_out, g_up, g_down], trio)
        (dz, dprm, dgates), got = _mixer_bwd(z, hst, dymix, prm, gates, l, f"bwd_mixer_{l}", comm=comm)
        if carry:
            for w, s in zip(trio, got):
                reduced[w] = _sum_slots(s, idx, w, 0, reduced[w], f"rs_sum0_{w}")
        else:
            for w, g, land in zip(trio, (g_out, g_up, g_down), got):
                summed1[w] = _pair_sum(g, land, idx, w, f"rs_add1_{w}")
        comm = _share_plan([reduced[w] for w in trio], trio, 0) if carry else scatter1((W_DOWN,))
        (g_in,), got = _mm_tn(h1, dz, f"bwd_win_{l}", tk=1024, tn=1792, comm=comm)
        if carry:
            for w, full in zip(trio, got):
                reduced[w] = full
        else:
            slots1[W_DOWN], = got
        comm = None if carry else scatter1((W_OUT,))
        (dx, dxb, dg1), got = _mm_nt_normbwd(dz, wi[l], x_in, norm1_g[l][None], dx2, f"bwd_in_{l}", comm=comm)
        if not carry:
            slots1[W_OUT], = got
        grads[l] = [g_in, g_out, g_up, g_down]
        rep = dict(zip(REP_NAMES, [dg1[0], dprm[4], dprm[5], dprm[6], dprm[7], dg2[0],
                                   _block_diag_extract(dgates[0:4]), _block_diag_extract(dgates[4:8])]))
        conv = [dprm[0:4].reshape(-1), jnp.pad(dprm[8:11, 0:D_SC].reshape(-1), (0, 512)),
                dfw[:, 0:3, :].transpose(1, 0, 2).reshape(-1)]
        small[l] = (rep, conv)
    grad_x = dx[None]
    g_small = jnp.concatenate(
        [small[l][0][k] for k in REP_NAMES for l in range(2)] + [dgf[0]] + small[0][1] + small[1][1]
        + [loss_blk.reshape(-1)]).reshape(SMALL_ROWS, LANES)

    def big(w, g, m, v, name):
        shape = w.shape
        two_d = lambda a: a.reshape(-1, shape[-1])
        outs, _ = _adamw(two_d(w), two_d(g), two_d(m), two_d(v), name, emit_grad=True)
        return [o.reshape(shape) for o in outs]

    land_in, land_small = _comm_call(_pair_plan([grads[0][W_IN], g_small], [W_IN, SMALL]), "rs_pair_last")
    sum_in = _pair_sum(grads[0][W_IN], land_in, idx, W_IN, "rs_add0_0")
    sum_small = _pair_sum(g_small, land_small, idx, SMALL, "rs_add0_4")
    slot_in, slot_small = _comm_call(
        _scatter_plan([sum_in[0], sum_small[0]], [sum_in[1], sum_small[1]], [W_IN, SMALL]), "rs_scatter_last")
    gw_in, gs = _comm_call(
        _share_plan([_sum_slots(slot_in, idx, W_IN, 0, reduced[W_IN], "rs_sum0_0"),
                     _sum_slots(slot_small, idx, SMALL, 0, None, "rs_sum0_4")], [W_IN, SMALL], 0), "rs_share0")
    upd = {"w_up": big(w_up, reduced[W_UP], m_w_up, v_w_up, "adamw_w_up"),
           "w_down": big(w_down, reduced[W_DOWN], m_w_down, v_w_down, "adamw_w_down"),
           "w_out": big(w_out, reduced[W_OUT], m_w_out, v_w_out, "adamw_w_out"),
           "w_in": big(w_in, gw_in, m_w_in, v_w_in, "adamw_w_in")}

    loss = gs[REP_ROWS + CONV_ROWS, 0]
    g_rep = gs[0:REP_ROWS]
    g_conv = gs[REP_ROWS:REP_ROWS + CONV_ROWS].reshape(2, CONV_LAYER)
    g_lru_cw = lax.dynamic_slice_in_dim(g_conv[:, 0:4096].reshape(2, 4, 1024), me * 256, 256, axis=2)
    g_sc_cw = lax.dynamic_slice_in_dim(g_conv[:, 4096:4096 + 1536].reshape(2, 3, 512), me * 128, 128, axis=2)
    g_ffn_cw = lax.dynamic_slice_in_dim(g_conv[:, 6144:].reshape(2, 3, 6144), me * 1536, 1536, axis=2)

    rep_out, _ = _adamw(
        _pack_rep(norm1_g, lru_conv_b, lru_ba, lru_bx, lru_lambda, norm2_g, lru_wa, lru_wx, final_g), g_rep,
        _pack_rep(m_norm1_g, m_lru_conv_b, m_lru_ba, m_lru_bx, m_lru_lambda, m_norm2_g, m_lru_wa, m_lru_wx, m_final_g),
        _pack_rep(v_norm1_g, v_lru_conv_b, v_lru_ba, v_lru_bx, v_lru_lambda, v_norm2_g, v_lru_wa, v_lru_wx, v_final_g),
        "adamw_rep")
    conv_out, _ = _adamw(s_conv, _pack_conv_shard(g_lru_cw, g_sc_cw, g_ffn_cw),
                         _pack_conv_shard(m_lru_conv_w, m_sc_conv_w, m_ffn_conv_w),
                         _pack_conv_shard(v_lru_conv_w, v_sc_conv_w, v_ffn_conv_w), "adamw_conv")

    names = ["norm1_g", "w_in", "lru_conv_w", "lru_conv_b", "lru_wa", "lru_ba", "lru_wx", "lru_bx", "lru_lambda",
             "sc_conv_w", "w_out", "norm2_g", "w_up", "ffn_conv_w", "w_down", "final_g"]
    groups = []
    g_all = dict(_unpack_rep(g_rep))
    g_all.update({k: v[3] for k, v in upd.items()})
    g_all.update(lru_conv_w=g_lru_cw, sc_conv_w=g_sc_cw, ffn_conv_w=g_ffn_cw)
    groups.append(g_all)
    for i in range(3):
        d = dict(_unpack_rep(rep_out[i]))
        cl, cs, cf = _unpack_conv_shard(conv_out[i])
        d.update(lru_conv_w=cl, sc_conv_w=cs, ffn_conv_w=cf)
        d.update({k: v[i] for k, v in upd.items()})
        groups.append(d)
    return (loss, grad_x, *[grp[n] for grp in groups for n in names])
```

```python
import dataclasses
import functools
import math
import operator
from typing import Any, Callable, Optional, Sequence

import jax
import jax.numpy as jnp
from jax import lax
from jax.experimental import pallas as pl
from jax.experimental.pallas import tpu as pltpu

F32 = jnp.float32
BF16 = jnp.bfloat16
MESH = pl.DeviceIdType.MESH

D_MODEL = 1024
D_LRU = 1024
D_SC = 512
D_MIX = D_LRU + D_SC
D_IN = 2 * D_LRU + 3 * D_SC
D_FF = 3072
N_CHIP = 4
RG_C = 8.0
EPS = 1e-6
ADAM_LR = 0.001
ADAM_B1 = 0.9
ADAM_B2 = 0.999
ADAM_EPS = 1e-08
ADAM_WD = 0.01
ADAM_STEP = 10

SUBLANES = 8
PACKED = 16
LANES = 128
VMEM_LIMIT = 56 * 1024 * 1024
GELU_C0 = math.sqrt(2.0 / math.pi)
GELU_C1 = 0.044715

REP_LAYER = 6 * 1024 + 2 * 16 * 64 * 64
REP_ROWS = (2 * REP_LAYER + 1024) // LANES
CONV_LAYER = 4 * 1024 + 2048 + 3 * 6144
CONV_ROWS = 2 * CONV_LAYER // LANES
SMALL_ROWS = REP_ROWS + CONV_ROWS + 8
CONV_PACK_ROWS = 96

W_IN, W_OUT, W_UP, W_DOWN, SMALL = range(5)
COL_SHARDED = {W_IN: True, W_OUT: False, W_UP: True, W_DOWN: False}

ONCE = pl.Buffered(1)
ANY = pl.BlockSpec(memory_space=pl.ANY)


def _cp(*sem):
    return pltpu.CompilerParams(dimension_semantics=sem, vmem_limit_bytes=VMEM_LIMIT)


@dataclasses.dataclass
class Comm:
    srcs: Sequence[Any]
    bufs: Sequence[Any]
    outs: Sequence[Any]
    n_sem: int
    start: Callable
    finish: Callable
    mid: Optional[Callable] = None
    mid_at: Optional[Sequence[int]] = None


def _pallas(body, *, name, grid, in_specs, out_specs, out_shape, args, sem, scratch_shapes=(), comm=None):
    if comm is None:
        res = pl.pallas_call(
            body, name=name, grid=grid, in_specs=list(in_specs), out_specs=list(out_specs),
            out_shape=list(out_shape), scratch_shapes=list(scratch_shapes), compiler_params=_cp(*sem))(*args)
        return tuple(res), ()
    n_in, n_out, n_scr = len(in_specs), len(out_specs), len(scratch_shapes)
    ns, nb, no = len(comm.srcs), len(comm.bufs), len(comm.outs)

    def carrier(*refs):
        p = 0
        main_in = refs[p:p + n_in]
        p += n_in
        c_src = refs[p:p + ns]
        p += ns + nb
        main_out = refs[p:p + n_out]
        p += n_out
        c_buf = refs[p:p + nb]
        p += nb
        c_out = refs[p:p + no]
        p += no
        scr = refs[p:p + n_scr]
        send, recv = refs[p + n_scr], refs[p + n_scr + 1]
        ids = [pl.program_id(a) for a in range(len(grid))]

        def at(steps):
            return functools.reduce(operator.and_, [i == s for i, s in zip(ids, steps)])

        @pl.when(at([0] * len(grid)))
        def _():
            comm.start(c_src, c_buf, c_out, send, recv)

        if comm.mid is not None:
            @pl.when(at(comm.mid_at))
            def _():
                comm.mid(c_src, c_buf, c_out, send, recv)

        body(*main_in, *main_out, *scr)

        @pl.when(at([g - 1 for g in grid]))
        def _():
            comm.finish(c_src, c_buf, c_out, send, recv)

    res = pl.pallas_call(
        carrier, name=name, grid=grid,
        in_specs=list(in_specs) + [ANY] * (ns + nb),
        out_specs=list(out_specs) + [ANY] * (nb + no),
        out_shape=list(out_shape) + [jax.ShapeDtypeStruct(b.shape, b.dtype) for b in comm.bufs] + list(comm.outs),
        input_output_aliases={n_in + ns + j: n_out + j for j in range(nb)},
        scratch_shapes=list(scratch_shapes) + [pltpu.SemaphoreType.DMA((comm.n_sem,)),
                                               pltpu.SemaphoreType.DMA((comm.n_sem,))],
        compiler_params=_cp(*(["arbitrary"] * len(grid))),
    )(*args, *comm.srcs, *comm.bufs)
    return tuple(res[:n_out]), tuple(res[n_out:])


def _comm_call(comm, name):
    ns, nb, no = len(comm.srcs), len(comm.bufs), len(comm.outs)

    def body(*refs):
        c_src = refs[0:ns]
        c_buf = refs[ns + nb:ns + 2 * nb]
        c_out = refs[ns + 2 * nb:ns + 2 * nb + no]
        send, recv = refs[ns + 2 * nb + no], refs[ns + 2 * nb + no + 1]
        comm.start(c_src, c_buf, c_out, send, recv)
        if comm.mid is not None:
            comm.mid(c_src, c_buf, c_out, send, recv)
        comm.finish(c_src, c_buf, c_out, send, recv)

    return tuple(pl.pallas_call(
        body, name=name,
        in_specs=[ANY] * (ns + nb), out_specs=[ANY] * (nb + no),
        out_shape=[jax.ShapeDtypeStruct(b.shape, b.dtype) for b in comm.bufs] + list(comm.outs),
        input_output_aliases={ns + j: j for j in range(nb)},
        scratch_shapes=[pltpu.SemaphoreType.DMA((comm.n_sem,)), pltpu.SemaphoreType.DMA((comm.n_sem,))],
    )(*comm.srcs, *comm.bufs))


def _sigmoid(v):
    return 1.0 / (1.0 + jnp.exp(-v))


def _sigmoid_tanh(v):
    return 0.5 + 0.5 * jnp.tanh(0.5 * v)


def _gelu_parts(v):
    v2 = v * v
    t = jnp.tanh(GELU_C0 * v * (1.0 + GELU_C1 * v2))
    half = 0.5 * (1.0 + t)
    gel = v * half
    dgel = half + 0.5 * v * (1.0 - t * t) * (GELU_C0 * (1.0 + 3.0 * GELU_C1 * v2))
    return gel, dgel


def _gelu(v):
    t = jnp.tanh(GELU_C0 * v * (1.0 + GELU_C1 * (v * v)))
    return 0.5 * v * (1.0 + t)


def _neg_expm1(y, a):
    p = jnp.full_like(y, 1.0 / 120.0)
    for coef in (1.0 / 24.0, 1.0 / 6.0, 0.5, 1.0):
        p = p * y + coef
    return jnp.where(y > -0.1, -(p * y), 1.0 - a * a)


def _softplus_neg(lam):
    nl = -lam
    e = jnp.exp(-jnp.abs(nl))
    u = 1.0 + e
    l1p = jnp.where(u == 1.0, e, jnp.log(u) * e / (u - 1.0))
    return jnp.maximum(nl, 0.0) + l1p


def _conv_taps(ext, taps, n_out):
    kw = len(taps)
    acc = taps[kw - 1] * ext[SUBLANES:SUBLANES + n_out]
    for k in range(kw - 1):
        acc = acc + taps[k] * pltpu.roll(ext, kw - 1 - k, axis=0)[SUBLANES:SUBLANES + n_out]
    return acc


def _conv_taps_t(ext, taps, n_out):
    kw = len(taps)
    n = ext.shape[0]
    acc = taps[kw - 1] * ext[0:n_out]
    for k in range(kw - 1):
        acc = acc + taps[k] * pltpu.roll(ext, n - (kw - 1 - k), axis=0)[0:n_out]
    return acc


def _scan8(a, b, carry, row):
    for s in (1, 2, 4):
        m = row >= s
        a_sh = jnp.where(m, pltpu.roll(a, s, axis=0), 1.0)
        b_sh = jnp.where(m, pltpu.roll(b, s, axis=0), 0.0)
        b = a * b_sh + b
        a = a * a_sh
    return a * carry + b


def _scan8_rev(a, b, carry, row):
    for s in (1, 2, 4):
        m = row < SUBLANES - s
        a_sh = jnp.where(m, pltpu.roll(a, SUBLANES - s, axis=0), 1.0)
        b_sh = jnp.where(m, pltpu.roll(b, SUBLANES - s, axis=0), 0.0)
        b = a * b_sh + b
        a = a * a_sh
    return a * carry + b


def _cast_into_full(w, kind, idx, name):
    nl, r, c = w.shape
    tr = 256 if r % 256 == 0 else r
    nrb = r // tr

    def body(idx_ref, w_ref, o0_ref, o1_ref):
        o0_ref[...] = w_ref[0].astype(BF16)
        o1_ref[...] = w_ref[1].astype(BF16)

    if COL_SHARDED[kind]:
        full = (r, N_CHIP * c)
        o_spec = pl.BlockSpec((tr, c), lambda i, idx_ref: (i, idx_ref[1]))
    else:
        full = (N_CHIP * r, c)
        o_spec = pl.BlockSpec((tr, c), lambda i, idx_ref: (idx_ref[1] * nrb + i, 0))
    return pl.pallas_call(
        body, name=name,
        grid_spec=pltpu.PrefetchScalarGridSpec(
            num_scalar_prefetch=1, grid=(nrb,),
            in_specs=[pl.BlockSpec((nl, tr, c), lambda i, idx_ref: (0, i, 0))], out_specs=[o_spec, o_spec]),
        out_shape=[jax.ShapeDtypeStruct(full, BF16)] * 2,
        compiler_params=_cp("parallel"),
    )(idx, w)


def _norm_mm(x, g, w, name, planes=False, tm=512, tn=512, comm=None):
    t_len, d = x.shape
    n = w.shape[1]
    half = n // 2

    def body(x_ref, g_ref, w_ref, z_ref, h_ref):
        xv = x_ref[...]
        r = lax.rsqrt(jnp.mean(xv * xv, axis=-1, keepdims=True) + EPS)
        h_ref[...] = ((xv * r) * g_ref[...]).astype(BF16)
        for n0 in range(0, n, tn):
            blk = jnp.dot(h_ref[...], w_ref[:, n0:n0 + tn], preferred_element_type=F32).astype(BF16)
            if planes:
                z_ref[n0 // half, :, n0 % half:n0 % half + tn] = blk
            else:
                z_ref[:, n0:n0 + tn] = blk

    if planes:
        z_shape = jax.ShapeDtypeStruct((2, t_len, half), BF16)
        z_spec = pl.BlockSpec((2, tm, half), lambda i: (0, i, 0))
    else:
        z_shape = jax.ShapeDtypeStruct((t_len, n), BF16)
        z_spec = pl.BlockSpec((tm, n), lambda i: (i, 0))
    return _pallas(
        body, name=name, grid=(t_len // tm,),
        in_specs=[pl.BlockSpec((tm, d), lambda i: (i, 0)),
                  pl.BlockSpec((1, d), lambda i: (0, 0)),
                  pl.BlockSpec((d, n), lambda i: (0, 0), pipeline_mode=ONCE)],
        out_specs=[z_spec, pl.BlockSpec((tm, d), lambda i: (i, 0))],
        out_shape=[z_shape, jax.ShapeDtypeStruct((t_len, d), BF16)],
        args=(x, g, w), sem=("parallel",), comm=comm)


def _mm_res(a, w, res, name, tm=512, comm=None):
    t_len, k = a.shape
    n = w.shape[1]

    def body(a_ref, w_ref, r_ref, o_ref):
        o_ref[...] = r_ref[...] + jnp.dot(a_ref[...], w_ref[...], preferred_element_type=F32)

    return _pallas(
        body, name=name, grid=(t_len // tm,),
        in_specs=[pl.BlockSpec((tm, k), lambda i: (i, 0)),
                  pl.BlockSpec((k, n), lambda i: (0, 0), pipeline_mode=ONCE),
                  pl.BlockSpec((tm, n), lambda i: (i, 0))],
        out_specs=[pl.BlockSpec((tm, n), lambda i: (i, 0))],
        out_shape=[jax.ShapeDtypeStruct((t_len, n), F32)],
        args=(a, w, res), sem=("parallel",), comm=comm)


def _mm_nt(a, w, name, tm=512, comm=None):
    t_len, k = a.shape
    n = w.shape[0]

    def body(a_ref, w_ref, o_ref):
        o_ref[...] = lax.dot_general(a_ref[...], w_ref[...], (((1,), (1,)), ((), ())),
                                     preferred_element_type=F32).astype(BF16)

    return _pallas(
        body, name=name, grid=(t_len // tm,),
        in_specs=[pl.BlockSpec((tm, k), lambda i: (i, 0)),
                  pl.BlockSpec((n, k), lambda i: (0, 0), pipeline_mode=ONCE)],
        out_specs=[pl.BlockSpec((tm, n), lambda i: (i, 0))],
        out_shape=[jax.ShapeDtypeStruct((t_len, n), BF16)],
        args=(a, w), sem=("parallel",), comm=comm)


def _mm_nt_normbwd(dz, w, x, g, dres, name, planes=False, tm=512, comm=None):
    t_len, d = x.shape
    n = w.shape[1]
    half = n // 2
    nt_dims = (((1,), (1,)), ((), ()))

    def body(dz_ref, w_ref, x_ref, g_ref, r_ref, dx_ref, dxb_ref, dg_ref):
        @pl.when(pl.program_id(0) == 0)
        def _():
            dg_ref[...] = jnp.zeros_like(dg_ref)

        if planes:
            dh = (lax.dot_general(dz_ref[0], w_ref[:, 0:half], nt_dims, preferred_element_type=F32)
                  + lax.dot_general(dz_ref[1], w_ref[:, half:], nt_dims, preferred_element_type=F32))
        else:
            dh = lax.dot_general(dz_ref[...], w_ref[...], nt_dims, preferred_element_type=F32)
        xv = x_ref[...]
        r = lax.rsqrt(jnp.mean(xv * xv, axis=-1, keepdims=True) + EPS)
        xh = xv * r
        dhg = dh * g_ref[...]
        dx = r_ref[...] + r * (dhg - xh * jnp.mean(dhg * xh, axis=-1, keepdims=True))
        dx_ref[...] = dx
        dxb_ref[...] = dx.astype(BF16)
        dg_ref[0:1, :] += jnp.sum(dh * xh, axis=0, keepdims=True)

    if planes:
        dz_spec = pl.BlockSpec((2, tm, half), lambda i: (0, i, 0))
    else:
        dz_spec = pl.BlockSpec((tm, n), lambda i: (i, 0))
    return _pallas(
        body, name=name, grid=(t_len // tm,),
        in_specs=[dz_spec,
                  pl.BlockSpec((d, n), lambda i: (0, 0), pipeline_mode=ONCE),
                  pl.BlockSpec((tm, d), lambda i: (i, 0)),
                  pl.BlockSpec((1, d), lambda i: (0, 0)),
                  pl.BlockSpec((tm, d), lambda i: (i, 0))],
        out_specs=[pl.BlockSpec((tm, d), lambda i: (i, 0)),
                   pl.BlockSpec((tm, d), lambda i: (i, 0)),
                   pl.BlockSpec((SUBLANES, d), lambda i: (0, 0))],
        out_shape=[jax.ShapeDtypeStruct((t_len, d), F32),
                   jax.ShapeDtypeStruct((t_len, d), BF16),
                   jax.ShapeDtypeStruct((SUBLANES, d), F32)],
        args=(dz, w, x, g, dres), sem=("arbitrary",), comm=comm)


def _mm_tn(a, g, name, tk, tn, planes=False, tt=1024, comm=None):
    t_len, k = a.shape
    n = 2 * g.shape[2] if planes else g.shape[1]
    nn = n // tn
    half = nn // 2
    tt = min(tt, t_len)

    def body(a_ref, g_ref, o_ref):
        @pl.when(pl.program_id(2) == 0)
        def _():
            o_ref[...] = jnp.zeros_like(o_ref)

        o_ref[...] += lax.dot_general(a_ref[...], g_ref[...], (((0,), (0,)), ((), ())),
                                      preferred_element_type=F32)

    if planes:
        g_spec = pl.BlockSpec((None, tt, tn), lambda i, j, t: (j // half, t, j % half))
    else:
        g_spec = pl.BlockSpec((tt, tn), lambda i, j, t: (t, j))
    return _pallas(
        body, name=name, grid=(k // tk, nn, t_len // tt),
        in_specs=[pl.BlockSpec((tt, tk), lambda i, j, t: (t, i)), g_spec],
        out_specs=[pl.BlockSpec((tk, tn), lambda i, j, t: (i, j))],
        out_shape=[jax.ShapeDtypeStruct((k, n), F32)],
        args=(a, g), sem=("parallel", "parallel", "arbitrary"), comm=comm)


def _lru_gates(rp, ip, spn):
    r = _sigmoid(rp)
    i = _sigmoid_tanh(ip)
    la = r * spn
    a = jnp.exp(la)
    mult = jnp.sqrt(_neg_expm1(2.0 * la, a))
    return r, i, a, mult


def _mixer_fwd(z, prm, gates, layer, name, tb=256, comm=None):
    t_len = z.shape[0]

    def body(z_ref, p_ref, g_ref, y_ref, h_ref, xhalo, phalo, hcar, lx_s, rp_s, ip_s):
        @pl.when(pl.program_id(0) == 0)
        def _():
            xhalo[...] = jnp.zeros_like(xhalo)
            phalo[...] = jnp.zeros_like(phalo)
            hcar[...] = jnp.zeros_like(hcar)

        prm_v = p_ref[...]
        cw = prm_v[0:4]
        vec = prm_v[4:8]
        xp = z_ref[:, 0:D_LRU].astype(F32)
        ext = jnp.concatenate([xhalo[...], xp], axis=0)
        lx = vec[0:1] + _conv_taps(ext, [cw[k:k + 1] for k in range(4)], tb)
        xhalo[...] = xp[tb - SUBLANES:]
        lx_s[...] = lx
        lxb = lx.astype(BF16)
        for q in range(4):
            sl = slice(q * 256, (q + 1) * 256)
            rp_s[:, sl] = jnp.dot(lxb[:, sl], g_ref[q], preferred_element_type=F32) + vec[1:2, sl]
            ip_s[:, sl] = jnp.dot(lxb[:, sl], g_ref[4 + q], preferred_element_type=F32) + vec[2:3, sl]

        spn = jnp.broadcast_to(-RG_C * _softplus_neg(vec[3:4]), (SUBLANES, D_LRU))
        row = lax.broadcasted_iota(jnp.int32, (SUBLANES, D_LRU), 0)

        def step(ci, carry):
            o = pl.multiple_of(ci * PACKED, PACKED)
            gate = z_ref[pl.ds(o, PACKED), D_LRU:2 * D_LRU].astype(F32)
            ys = []
            for sub in range(2):
                rows = pl.ds(pl.multiple_of(o + sub * SUBLANES, SUBLANES), SUBLANES)
                lxv = lx_s[rows, :]
                _, i, a, mult = _lru_gates(rp_s[rows, :], ip_s[rows, :], spn)
                h = _scan8(a, mult * (i * lxv), carry, row)
                h_ref[rows, :] = h
                ys.append(h * _gelu(gate[sub * SUBLANES:(sub + 1) * SUBLANES]))
                carry = jnp.broadcast_to(h[SUBLANES - 1:SUBLANES, :], (SUBLANES, D_LRU))
            y_ref[pl.ds(o, PACKED), 0:D_LRU] = jnp.concatenate(ys, axis=0).astype(BF16)
            return carry

        hcar[...] = lax.fori_loop(0, tb // PACKED, step, hcar[...])

        scw = prm_v[8:11, 0:D_SC]
        o_b, o_c, o_x = 2 * D_LRU, 2 * D_LRU + D_SC, 2 * D_LRU + 2 * D_SC
        p = z_ref[:, o_c:o_x].astype(F32) * z_ref[:, o_x:].astype(F32)
        pext = jnp.concatenate([phalo[...], p], axis=0)
        q = _conv_taps(pext, [scw[k:k + 1] for k in range(3)], tb)
        phalo[...] = p[tb - SUBLANES:]
        y_ref[:, D_LRU:] = (z_ref[:, o_b:o_c].astype(F32) * q).astype(BF16)

    return _pallas(
        body, name=name, grid=(t_len // tb,),
        in_specs=[pl.BlockSpec((tb, D_IN), lambda t: (t, 0)),
                  pl.BlockSpec((None, 2 * SUBLANES, D_LRU), lambda t: (layer, 0, 0)),
                  pl.BlockSpec((None, 8, 256, 256), lambda t: (layer, 0, 0, 0))],
        out_specs=[pl.BlockSpec((tb, D_MIX), lambda t: (t, 0)),
                   pl.BlockSpec((tb, D_LRU), lambda t: (t, 0))],
        out_shape=[jax.ShapeDtypeStruct((t_len, D_MIX), BF16),
                   jax.ShapeDtypeStruct((t_len, D_LRU), F32)],
        scratch_shapes=[pltpu.VMEM((SUBLANES, D_LRU), F32), pltpu.VMEM((SUBLANES, D_SC), F32),
                        pltpu.VMEM((SUBLANES, D_LRU), F32), pltpu.VMEM((tb, D_LRU), F32),
                        pltpu.VMEM((tb, D_LRU), F32), pltpu.VMEM((tb, D_LRU), F32)],
        args=(z, prm, gates), sem=("arbitrary",), comm=comm)


def _mixer_bwd(z, h, dy, prm, gates, layer, name, tb=256, comm=None):
    t_len = z.shape[0]
    nb = t_len // tb

    def body(z_ref, zh_ref, h_ref, hh_ref, dy_ref, p_ref, g_ref, dz_ref, dp_ref, dg_ref,
             lx_s, rp_s, ip_s, drpb_s, dipb_s, dlx_s, hext_s, acc_s, acar, gcar, dqh):
        t = pl.program_id(0)
        first_block = t == nb - 1

        @pl.when(t == 0)
        def _():
            for ref in (dp_ref, dg_ref, acc_s, acar, gcar, dqh):
                ref[...] = jnp.zeros_like(ref)
            dlx_s[tb:, :] = jnp.zeros((SUBLANES, D_LRU), F32)

        prm_v = p_ref[...]
        cw = prm_v[0:4]
        vec = prm_v[4:8]
        scw = prm_v[8:11, 0:D_SC]
        wa_ref = [g_ref.at[q] for q in range(4)]
        wx_ref = [g_ref.at[4 + q] for q in range(4)]
        dwa_ref = [dg_ref.at[q] for q in range(4)]
        dwx_ref = [dg_ref.at[4 + q] for q in range(4)]
        ctaps = [cw[k:k + 1] for k in range(4)]
        staps = [scw[k:k + 1] for k in range(3)]
        keep = jnp.where(first_block, 0.0, 1.0)
        zh = zh_ref[...].astype(F32)[PACKED - SUBLANES:] * keep

        xp = z_ref[:, 0:D_LRU].astype(F32)
        xext = jnp.concatenate([zh[:, 0:D_LRU], xp], axis=0)
        lx = vec[0:1] + _conv_taps(xext, ctaps, tb)
        lx_s[...] = lx
        lxb = lx.astype(BF16)
        for q in range(4):
            sl = slice(q * 256, (q + 1) * 256)
            rp_s[:, sl] = jnp.dot(lxb[:, sl], wa_ref[q][...], preferred_element_type=F32) + vec[1:2, sl]
            ip_s[:, sl] = jnp.dot(lxb[:, sl], wx_ref[q][...], preferred_element_type=F32) + vec[2:3, sl]
        hext_s[0:SUBLANES, :] = hh_ref[...] * keep
        hext_s[SUBLANES:, :] = h_ref[...]

        spn = jnp.broadcast_to(-RG_C * _softplus_neg(vec[3:4]), (SUBLANES, D_LRU))
        row = lax.broadcasted_iota(jnp.int32, (SUBLANES, D_LRU), 0)

        def step(ci, carry):
            a_next, g_next = carry
            o = pl.multiple_of((tb // PACKED - 1 - ci) * PACKED, PACKED)
            rows16 = pl.ds(o, PACKED)
            gate16 = z_ref[rows16, D_LRU:2 * D_LRU].astype(F32)
            dyl16 = dy_ref[rows16, 0:D_LRU].astype(F32)
            dgs, drs, dis = [None, None], [None, None], [None, None]
            for sub in (1, 0):
                oo = pl.multiple_of(o + sub * SUBLANES, SUBLANES)
                rows = pl.ds(oo, SUBLANES)
                half = slice(sub * SUBLANES, (sub + 1) * SUBLANES)
                lxv = lx_s[rows, :]
                r, i, a, mult = _lru_gates(rp_s[rows, :], ip_s[rows, :], spn)
                hwin = hext_s[pl.ds(oo, 2 * SUBLANES), :]
                hv = hwin[SUBLANES:]
                hprev = pltpu.roll(hwin, 1, axis=0)[SUBLANES:]
                gel, dgel = _gelu_parts(gate16[half])
                dyl = dyl16[half]
                a_up = jnp.where(row < SUBLANES - 1, pltpu.roll(a, SUBLANES - 1, axis=0), a_next)
                gg = _scan8_rev(a_up, dyl * gel, g_next, row)
                dgs[sub] = dyl * hv * dgel
                ilx = i * lxv
                dla = gg * hprev * a - (gg * ilx) * (a * a) / mult
                dlx_s[rows, :] = gg * mult * i
                drp = dla * spn * r * (1.0 - r)
                dip = gg * mult * lxv * i * (1.0 - i)
                drs[sub] = drp
                dis[sub] = dip
                acc_s[0] += drp
                acc_s[1] += dip
                acc_s[2] += dla * r
                a_next = jnp.broadcast_to(a[0:1, :], (SUBLANES, D_LRU))
                g_next = jnp.broadcast_to(gg[0:1, :], (SUBLANES, D_LRU))
            dz_ref[rows16, D_LRU:2 * D_LRU] = jnp.concatenate(dgs, axis=0).astype(BF16)
            drpb_s[rows16, :] = jnp.concatenate(drs, axis=0).astype(BF16)
            dipb_s[rows16, :] = jnp.concatenate(dis, axis=0).astype(BF16)
            return a_next, g_next

        a_c, g_c = lax.fori_loop(0, tb // PACKED, step, (acar[...], gcar[...]))
        acar[...] = a_c
        gcar[...] = g_c

        drpb = drpb_s[...]
        dipb = dipb_s[...]
        nt_dims = (((1,), (1,)), ((), ()))
        tn_dims = (((0,), (0,)), ((), ()))
        for q in range(4):
            sl = slice(q * 256, (q + 1) * 256)
            dlx_s[0:tb, sl] += (
                lax.dot_general(drpb[:, sl], wa_ref[q][...], nt_dims, preferred_element_type=F32)
                + lax.dot_general(dipb[:, sl], wx_ref[q][...], nt_dims, preferred_element_type=F32))
            dwa_ref[q][...] += lax.dot_general(lxb[:, sl], drpb[:, sl], tn_dims, preferred_element_type=F32)
            dwx_ref[q][...] += lax.dot_general(lxb[:, sl], dipb[:, sl], tn_dims, preferred_element_type=F32)

        dlx_ext = dlx_s[...]
        dlx = dlx_ext[0:tb]
        dz_ref[:, 0:D_LRU] = _conv_taps_t(dlx_ext, ctaps, tb).astype(BF16)
        dp_ref[3:4, :] += jnp.sum(dlx * xp, axis=0, keepdims=True)
        for k in range(3):
            shifted = pltpu.roll(xext, 3 - k, axis=0)[SUBLANES:]
            dp_ref[k:k + 1, :] += jnp.sum(dlx * shifted, axis=0, keepdims=True)
        dp_ref[4:5, :] += jnp.sum(dlx, axis=0, keepdims=True)
        dlx_s[tb:, :] = dlx[0:SUBLANES]

        o_b, o_c, o_x = 2 * D_LRU, 2 * D_LRU + D_SC, 2 * D_LRU + 2 * D_SC
        sb = z_ref[:, o_b:o_c].astype(F32)
        scc = z_ref[:, o_c:o_x].astype(F32)
        sx = z_ref[:, o_x:].astype(F32)
        p = scc * sx
        pext = jnp.concatenate([zh[:, o_c:o_x] * zh[:, o_x:], p], axis=0)
        q = _conv_taps(pext, staps, tb)
        dys = dy_ref[:, D_LRU:].astype(F32)
        dq = dys * sb
        dp = _conv_taps_t(jnp.concatenate([dq, dqh[...]], axis=0), staps, tb)
        dp_ref[10:11, 0:D_SC] += jnp.sum(dq * p, axis=0, keepdims=True)
        for k in range(2):
            shifted = pltpu.roll(pext, 2 - k, axis=0)[SUBLANES:]
            dp_ref[8 + k:9 + k, 0:D_SC] += jnp.sum(dq * shifted, axis=0, keepdims=True)
        dqh[...] = dq[0:SUBLANES]
        dz_ref[:, o_b:o_c] = (dys * q).astype(BF16)
        dz_ref[:, o_c:o_x] = (dp * sx).astype(BF16)
        dz_ref[:, o_x:] = (dp * scc).astype(BF16)

        @pl.when(first_block)
        def _():
            dp_ref[5:6, :] = jnp.sum(acc_s[0], axis=0, keepdims=True)
            dp_ref[6:7, :] = jnp.sum(acc_s[1], axis=0, keepdims=True)
            dp_ref[7:8, :] = (jnp.sum(acc_s[2], axis=0, keepdims=True) * RG_C * _sigmoid(-vec[3:4]))

    blk = lambda t: (nb - 1 - t, 0)
    halo8 = lambda t: (jnp.maximum((nb - 1 - t) * (tb // SUBLANES) - 1, 0), 0)
    halo16 = lambda t: (jnp.maximum((nb - 1 - t) * (tb // PACKED) - 1, 0), 0)
    return _pallas(
        body, name=name, grid=(nb,),
        in_specs=[pl.BlockSpec((tb, D_IN), blk), pl.BlockSpec((PACKED, D_IN), halo16),
                  pl.BlockSpec((tb, D_LRU), blk), pl.BlockSpec((SUBLANES, D_LRU), halo8),
                  pl.BlockSpec((tb, D_MIX), blk),
                  pl.BlockSpec((None, 2 * SUBLANES, D_LRU), lambda t: (layer, 0, 0)),
                  pl.BlockSpec((None, 8, 256, 256), lambda t: (layer, 0, 0, 0))],
        out_specs=[pl.BlockSpec((tb, D_IN), blk),
                   pl.BlockSpec((2 * SUBLANES, D_LRU), lambda t: (0, 0)),
                   pl.BlockSpec((8, 256, 256), lambda t: (0, 0, 0))],
        out_shape=[jax.ShapeDtypeStruct((t_len, D_IN), BF16),
                   jax.ShapeDtypeStruct((2 * SUBLANES, D_LRU), F32),
                   jax.ShapeDtypeStruct((8, 256, 256), F32)],
        scratch_shapes=[pltpu.VMEM((tb, D_LRU), F32),
                        pltpu.VMEM((tb, D_LRU), F32), pltpu.VMEM((tb, D_LRU), F32),
                        pltpu.VMEM((tb, D_LRU), BF16), pltpu.VMEM((tb, D_LRU), BF16),
                        pltpu.VMEM((tb + SUBLANES, D_LRU), F32), pltpu.VMEM((tb + SUBLANES, D_LRU), F32),
                        pltpu.VMEM((3, SUBLANES, D_LRU), F32),
                        pltpu.VMEM((SUBLANES, D_LRU), F32), pltpu.VMEM((SUBLANES, D_LRU), F32),
                        pltpu.VMEM((SUBLANES, D_SC), F32)],
        args=(z, z, h, h, dy, prm, gates), sem=("arbitrary",), comm=comm)


def _ffn_act(u, fw, layer, name, tb=512, tn=1024, rc=64, comm=None):
    t_len = u.shape[1]
    hb = tb // PACKED

    def body(u_ref, uh_ref, fw_ref, o_ref, fg_ref, fu_ref, ext):
        keep = jnp.where(pl.program_id(0) == 0, 0.0, 1.0)
        ext[:, 0:SUBLANES, :] = uh_ref[...].astype(F32)[:, PACKED - SUBLANES:, :] * keep
        ext[:, SUBLANES:, :] = u_ref[...].astype(F32)
        fw_v = fw_ref[...]

        for lb in range(tn // LANES):
            lanes = slice(lb * LANES, (lb + 1) * LANES)
            wg = [fw_v[0, k:k + 1, lanes] for k in range(3)]
            wu = [fw_v[1, k:k + 1, lanes] for k in range(3)]

            def chunk(ci, c, lanes=lanes, wg=wg, wu=wu):
                o = pl.multiple_of(ci * rc, rc)
                win = pl.ds(o, rc + SUBLANES)
                gate = _conv_taps(ext[0, win, lanes], wg, rc)
                up = _conv_taps(ext[1, win, lanes], wu, rc)
                gel, dgel = _gelu_parts(gate)
                rows = pl.ds(o, rc)
                o_ref[rows, lanes] = (gel * up).astype(BF16)
                fg_ref[rows, lanes] = (up * dgel).astype(BF16)
                fu_ref[rows, lanes] = gel.astype(BF16)
                return c

            lax.fori_loop(0, tb // rc, chunk, 0)

    spec = pl.BlockSpec((tb, tn), lambda i, j: (i, j))
    shape = jax.ShapeDtypeStruct((t_len, D_FF), BF16)
    return _pallas(
        body, name=name, grid=(t_len // tb, D_FF // tn),
        in_specs=[pl.BlockSpec((2, tb, tn), lambda i, j: (0, i, j)),
                  pl.BlockSpec((2, PACKED, tn), lambda i, j: (0, jnp.maximum(i * hb - 1, 0), j)),
                  pl.BlockSpec((None, 2, SUBLANES, tn), lambda i, j: (layer, 0, 0, j))],
        out_specs=[spec] * 3, out_shape=[shape] * 3,
        scratch_shapes=[pltpu.VMEM((2, tb + SUBLANES, tn), F32)],
        args=(u, u, fw), sem=("parallel", "parallel"), comm=comm)


def _ffn_bwd(dact, fg, fu, u, fw, layer, name, tb=512, tn=1024, rc=32, comm=None):
    t_len = u.shape[1]
    ni = t_len // tb
    hb = tb // PACKED
    last_halo = t_len // PACKED - 1

    def body(d_ref, dn_ref, fg_ref, fgn_ref, fu_ref, fun_ref, u_ref, up_ref, fw_ref, du_ref, dfw_ref,
             extu, extp, acc):
        i = pl.program_id(1)

        @pl.when(i == 0)
        def _():
            acc[...] = jnp.zeros_like(acc)

        keep_prev = jnp.where(i == 0, 0.0, 1.0)
        keep_next = jnp.where(i == ni - 1, 0.0, 1.0)
        extu[:, 0:SUBLANES, :] = up_ref[...].astype(F32)[:, PACKED - SUBLANES:, :] * keep_prev
        extu[:, SUBLANES:, :] = u_ref[...].astype(F32)
        dv = d_ref[...].astype(F32)
        dn = dn_ref[...].astype(F32)[0:SUBLANES] * keep_next
        extp[0, 0:tb, :] = dv * fg_ref[...].astype(F32)
        extp[0, tb:, :] = dn * fgn_ref[...].astype(F32)[0:SUBLANES]
        extp[1, 0:tb, :] = dv * fu_ref[...].astype(F32)
        extp[1, tb:, :] = dn * fun_ref[...].astype(F32)[0:SUBLANES]
        fw_v = fw_ref[...]
        m = rc + SUBLANES

        for lb in range(tn // LANES):
            lanes = slice(lb * LANES, (lb + 1) * LANES)
            taps = [[fw_v[pln, k:k + 1, lanes] for k in range(3)] for pln in range(2)]

            def chunk(ci, c, lanes=lanes, taps=taps):
                o = pl.multiple_of(ci * rc, rc)
                for pln in range(2):
                    e = extu[pln, pl.ds(o, m), lanes]
                    sh = [pltpu.roll(e, 2, axis=0)[SUBLANES:], pltpu.roll(e, 1, axis=0)[SUBLANES:], e[SUBLANES:]]
                    dpost = extp[pln, pl.ds(o, m), lanes]
                    du_ref[pln, pl.ds(o, rc), lanes] = _conv_taps_t(dpost, taps[pln], rc).astype(BF16)
                    for k in range(3):
                        prod = dpost[0:rc] * sh[k]
                        acc[3 * pln + k, :, lanes] += sum(
                            prod[s:s + SUBLANES] for s in range(0, rc, SUBLANES))
                return c

            lax.fori_loop(0, tb // rc, chunk, 0)

        @pl.when(i == ni - 1)
        def _():
            dfw_ref[...] = jnp.zeros_like(dfw_ref)
            for pln in range(2):
                for k in range(3):
                    dfw_ref[pln, k:k + 1, :] = jnp.sum(acc[3 * pln + k], axis=0, keepdims=True)

    main = pl.BlockSpec((tb, tn), lambda j, i: (i, j))
    nxt = pl.BlockSpec((PACKED, tn), lambda j, i: (jnp.minimum((i + 1) * hb, last_halo), j))
    return _pallas(
        body, name=name, grid=(D_FF // tn, ni),
        in_specs=[main, nxt, main, nxt, main, nxt,
                  pl.BlockSpec((2, tb, tn), lambda j, i: (0, i, j)),
                  pl.BlockSpec((2, PACKED, tn), lambda j, i: (0, jnp.maximum(i * hb - 1, 0), j)),
                  pl.BlockSpec((None, 2, SUBLANES, tn), lambda j, i: (layer, 0, 0, j))],
        out_specs=[pl.BlockSpec((2, tb, tn), lambda j, i: (0, i, j)),
                   pl.BlockSpec((2, SUBLANES, tn), lambda j, i: (0, 0, j))],
        out_shape=[jax.ShapeDtypeStruct((2, t_len, D_FF), BF16),
                   jax.ShapeDtypeStruct((2, SUBLANES, D_FF), F32)],
        scratch_shapes=[pltpu.VMEM((2, tb + SUBLANES, tn), F32),
                        pltpu.VMEM((2, tb + SUBLANES, tn), F32),
                        pltpu.VMEM((6, SUBLANES, tn), F32)],
        args=(dact, dact, fg, fg, fu, fu, u, u, fw), sem=("parallel", "arbitrary"), comm=comm)


def _loss_head(x, g, target, name, tb=256):
    t_len, d = x.shape

    def body(x_ref, g_ref, t_ref, dx_ref, dxb_ref, dg_ref, loss_ref):
        @pl.when(pl.program_id(0) == 0)
        def _():
            dg_ref[...] = jnp.zeros_like(dg_ref)
            loss_ref[...] = jnp.zeros_like(loss_ref)

        xv = x_ref[...]
        gv = g_ref[...]
        r = lax.rsqrt(jnp.mean(xv * xv, axis=-1, keepdims=True) + EPS)
        xh = xv * r
        err = xh * gv - t_ref[...]
        loss_ref[...] += (0.5 / d) * jnp.sum(jnp.sum(err * err, axis=-1, keepdims=True), axis=0, keepdims=True)
        dy = err * (1.0 / d)
        dyg = dy * gv
        dx = r * (dyg - xh * jnp.mean(dyg * xh, axis=-1, keepdims=True))
        dx_ref[...] = dx
        dxb_ref[...] = dx.astype(BF16)
        dg_ref[0:1, :] += jnp.sum(dy * xh, axis=0, keepdims=True)

    return _pallas(
        body, name=name, grid=(t_len // tb,),
        in_specs=[pl.BlockSpec((tb, d), lambda i: (i, 0)), pl.BlockSpec((1, d), lambda i: (0, 0)),
                  pl.BlockSpec((tb, d), lambda i: (i, 0))],
        out_specs=[pl.BlockSpec((tb, d), lambda i: (i, 0)), pl.BlockSpec((tb, d), lambda i: (i, 0)),
                   pl.BlockSpec((SUBLANES, d), lambda i: (0, 0)),
                   pl.BlockSpec((SUBLANES, LANES), lambda i: (0, 0))],
        out_shape=[jax.ShapeDtypeStruct((t_len, d), F32), jax.ShapeDtypeStruct((t_len, d), BF16),
                   jax.ShapeDtypeStruct((SUBLANES, d), F32), jax.ShapeDtypeStruct((SUBLANES, LANES), F32)],
        args=(x, g, target), sem=("arbitrary",))[0]


def _adamw(w, g, m, v, name, emit_grad=False, comm=None):
    r, c = w.shape
    tr = 256 if r % 256 == 0 else r
    c1 = 1.0 / (1.0 - ADAM_B1 ** ADAM_STEP)
    c2 = 1.0 / (1.0 - ADAM_B2 ** ADAM_STEP)

    def body(w_ref, g_ref, m_ref, v_ref, d_ref, mo_ref, vo_ref, *go_ref):
        gv = g_ref[...]
        mn = ADAM_B1 * m_ref[...] + (1.0 - ADAM_B1) * gv
        vn = ADAM_B2 * v_ref[...] + (1.0 - ADAM_B2) * (gv * gv)
        d_ref[...] = -ADAM_LR * ((mn * c1) / (jnp.sqrt(vn * c2) + ADAM_EPS) + ADAM_WD * w_ref[...])
        mo_ref[...] = mn
        vo_ref[...] = vn
        if emit_grad:
            go_ref[0][...] = gv

    spec = pl.BlockSpec((tr, c), lambda i: (i, 0))
    shape = jax.ShapeDtypeStruct((r, c), F32)
    n_out = 4 if emit_grad else 3
    return _pallas(
        body, name=name, grid=(r // tr,),
        in_specs=[spec] * 4, out_specs=[spec] * n_out, out_shape=[shape] * n_out,
        args=(w, g, m, v), sem=("parallel",), comm=comm)


def _place():
    x, y, c = lax.axis_index("x"), lax.axis_index("y"), lax.axis_index("c")
    chips = [(1 - x, y), (x, 1 - y), (1 - x, 1 - y)]
    return x, y, c, chips


def _remote(src, dst, send, recv, sem, to):
    return pltpu.make_async_remote_copy(
        src_ref=src, dst_ref=dst, send_sem=send.at[sem], recv_sem=recv.at[sem], device_id=to, device_id_type=MESH)


def _gather_plan(fulls, kinds, mid_at=None, parts=None):
    parts = parts or [(0, 1)] * len(fulls)

    def region(it, f, k, cc):
        kind = kinds[it]
        p, n = parts[it][0:2]
        count = parts[it][2] if len(parts[it]) > 2 else 1
        if kind == SMALL:
            return f.at[k, pl.ds(cc * (CONV_PACK_ROWS // 2), CONV_PACK_ROWS // 2), :]
        if COL_SHARDED[kind]:
            rows, cols = f.shape[0] // (2 * n), f.shape[1] // N_CHIP
            return f.at[pl.ds((cc * n + p) * rows, count * rows), pl.ds(k * cols, cols)]
        assert n == 1
        rows = f.shape[0] // N_CHIP
        return f.at[pl.ds(k * rows + cc * (rows // 2), rows // 2), :]

    def first_hop(bufs, send, recv, it, j):
        x, y, c, chips = _place()
        reg = region(it, bufs[it], 2 * x + y, c)
        return _remote(reg, reg, send, recv, it * 6 + j, (*chips[j], c))

    def arrival(bufs, send, recv, it, j, second):
        x, y, c, chips = _place()
        px, py = chips[j]
        reg = region(it, bufs[it], 2 * px + py, 1 - c if second else c)
        to = (x, y, 1 - c) if second else (px, py, c)
        return _remote(reg, reg, send, recv, it * 6 + (3 + j if second else j), to)

    def forward(bufs, send, recv, it, j):
        x, y, c, chips = _place()
        px, py = chips[j]
        reg = region(it, bufs[it], 2 * px + py, c)
        return _remote(reg, reg, send, recv, it * 6 + 3 + j, (x, y, 1 - c))

    def start(srcs, bufs, outs, send, recv):
        for it in range(len(bufs)):
            for j in range(3):
                first_hop(bufs, send, recv, it, j).start()

    def mid(srcs, bufs, outs, send, recv):
        for it in range(len(bufs)):
            for j in range(3):
                arrival(bufs, send, recv, it, j, False).wait_recv()
                forward(bufs, send, recv, it, j).start()

    def finish(srcs, bufs, outs, send, recv):
        for it in range(len(bufs)):
            for j in range(3):
                arrival(bufs, send, recv, it, j, True).wait_recv()
        for it in range(len(bufs)):
            for j in range(3):
                first_hop(bufs, send, recv, it, j).wait_send()
                forward(bufs, send, recv, it, j).wait_send()

    return Comm(srcs=(), bufs=tuple(fulls), outs=(), n_sem=6 * len(fulls), start=start, mid=mid, finish=finish,
                mid_at=mid_at)


def _half_axis(kind):
    return 0 if kind == SMALL or COL_SHARDED[kind] else 1


def _half2(ref, kind, cc):
    if _half_axis(kind) == 0:
        return ref.at[pl.ds(cc * (ref.shape[0] // 2), ref.shape[0] // 2), :]
    return ref.at[:, pl.ds(cc * (ref.shape[1] // 2), ref.shape[1] // 2)]


def _pair_plan(grads, kinds):
    def land_shape(g, kind):
        s = list(g.shape)
        s[_half_axis(kind)] //= 2
        return jax.ShapeDtypeStruct(tuple(s), F32)

    def copy(srcs, outs, send, recv, it):
        x, y, c, _ = _place()
        return _remote(_half2(srcs[it], kinds[it], 1 - c), outs[it], send, recv, it, (x, y, 1 - c))

    def start(srcs, bufs, outs, send, recv):
        for it in range(len(srcs)):
            copy(srcs, outs, send, recv, it).start()

    def finish(srcs, bufs, outs, send, recv):
        for it in range(len(srcs)):
            copy(srcs, outs, send, recv, it).wait_send()
        for it in range(len(srcs)):
            copy(srcs, outs, send, recv, it).wait_recv()

    return Comm(srcs=tuple(grads), bufs=(), outs=tuple(land_shape(g, k) for g, k in zip(grads, kinds)),
                n_sem=len(grads), start=start, finish=finish)


def _scatter_plan(parts, slots, kinds):
    def piece(s, kind, k):
        if kind == SMALL:
            return s
        if COL_SHARDED[kind]:
            n = s.shape[1] // N_CHIP
            return s.at[:, pl.ds(k * n, n)]
        n = s.shape[0] // N_CHIP
        return s.at[pl.ds(k * n, n), :]

    def outbound(srcs, bufs, send, recv, it, j):
        x, y, c, chips = _place()
        px, py = chips[j]
        return _remote(piece(srcs[it], kinds[it], 2 * px + py), bufs[it].at[2 * x + y], send, recv, it * 3 + j,
                       (px, py, c))

    def inbound(bufs, send, recv, it, j):
        x, y, c, chips = _place()
        px, py = chips[j]
        got = bufs[it].at[2 * px + py]
        return _remote(got, got, send, recv, it * 3 + j, (px, py, c))

    def start(srcs, bufs, outs, send, recv):
        for it in range(len(srcs)):
            for j in range(3):
                outbound(srcs, bufs, send, recv, it, j).start()

    def finish(srcs, bufs, outs, send, recv):
        for it in range(len(srcs)):
            for j in range(3):
                inbound(bufs, send, recv, it, j).wait_recv()
        for it in range(len(srcs)):
            for j in range(3):
                outbound(srcs, bufs, send, recv, it, j).wait_send()

    return Comm(srcs=tuple(parts), bufs=tuple(slots), outs=(), n_sem=3 * len(parts), start=start, finish=finish)


def _share_plan(fulls, kinds, layer):
    def half(f, kind, cc):
        return _half2(f if kind == SMALL else f.at[layer], kind, cc)

    def copy(bufs, send, recv, it, cc):
        x, y, c, _ = _place()
        reg = half(bufs[it], kinds[it], c if cc == "mine" else 1 - c)
        return _remote(reg, reg, send, recv, it, (x, y, 1 - c))

    def start(srcs, bufs, outs, send, recv):
        for it in range(len(bufs)):
            copy(bufs, send, recv, it, "mine").start()

    def finish(srcs, bufs, outs, send, recv):
        for it in range(len(bufs)):
            copy(bufs, send, recv, it, "other").wait_recv()
        for it in range(len(bufs)):
            copy(bufs, send, recv, it, "mine").wait_send()

    return Comm(srcs=(), bufs=tuple(fulls), outs=(), n_sem=len(fulls), start=start, finish=finish)


def _pair_sum(g, land, idx, kind, name):
    odt = F32 if kind == SMALL else BF16
    r, cdim = land.shape

    def body(idx_ref, g_ref, l_ref, p_ref, s_ref):
        v = (g_ref[...] + l_ref[...]).astype(odt)
        p_ref[...] = v
        if kind == SMALL:
            s_ref[...] = v
        else:
            @pl.when(pl.program_id(1 if COL_SHARDED[kind] else 0) == idx_ref[1])
            def _():
                s_ref[...] = v

    if kind == SMALL:
        grid = (1,)
        g_spec = pl.BlockSpec((r, LANES), lambda i, idx_ref: (idx_ref[0], 0))
        spec = pl.BlockSpec((r, LANES), lambda i, idx_ref: (0, 0))
        s_spec = pl.BlockSpec((None, r, LANES), lambda i, idx_ref: (idx_ref[1], 0, 0))
        s_shape = (N_CHIP, r, LANES)
    elif COL_SHARDED[kind]:
        pc, tr = cdim // N_CHIP, 256
        nrb = r // tr
        grid = (nrb, N_CHIP)
        g_spec = pl.BlockSpec((tr, pc), lambda i, k, idx_ref: (idx_ref[0] * nrb + i, k))
        spec = pl.BlockSpec((tr, pc), lambda i, k, idx_ref: (i, k))
        s_spec = pl.BlockSpec((None, tr, pc), lambda i, k, idx_ref: (idx_ref[1], i, 0))
        s_shape = (N_CHIP, r, pc)
    else:
        pr = r // N_CHIP
        grid = (N_CHIP,)
        g_spec = pl.BlockSpec((pr, cdim), lambda k, idx_ref: (k, idx_ref[0]))
        spec = pl.BlockSpec((pr, cdim), lambda k, idx_ref: (k, 0))
        s_spec = pl.BlockSpec((None, pr, cdim), lambda k, idx_ref: (idx_ref[1], 0, 0))
        s_shape = (N_CHIP, pr, cdim)
    return pl.pallas_call(
        body, name=name,
        grid_spec=pltpu.PrefetchScalarGridSpec(
            num_scalar_prefetch=1, grid=grid, in_specs=[g_spec, spec], out_specs=[spec, s_spec]),
        out_shape=[jax.ShapeDtypeStruct(land.shape, odt), jax.ShapeDtypeStruct(s_shape, odt)],
        compiler_params=_cp(*(["arbitrary"] * len(grid))),
    )(idx, g, land)


def _sum_slots(slots, idx, kind, layer, prev, name):
    _, r, cdim = slots.shape

    def body(*refs):
        s_ref, o_ref = refs[1], refs[-1]
        v = s_ref[...].astype(F32)
        o_ref[...] = (v[0] + v[1]) + (v[2] + v[3])

    if kind == SMALL:
        grid = (1,)
        s_spec = pl.BlockSpec((N_CHIP, r, cdim), lambda i, idx_ref: (0, 0, 0))
        o_spec = pl.BlockSpec((r, cdim), lambda i, idx_ref: (idx_ref[0], 0))
        full = (2 * r, cdim)
    else:
        tr = 256 if r % 256 == 0 else 384
        nrb = r // tr
        grid = (nrb,)
        s_spec = pl.BlockSpec((N_CHIP, tr, cdim), lambda i, idx_ref: (0, i, 0))
        if COL_SHARDED[kind]:
            o_spec = pl.BlockSpec((None, tr, cdim), lambda i, idx_ref: (layer, idx_ref[0] * nrb + i, 0))
            full = (2, 2 * r, cdim)
        else:
            o_spec = pl.BlockSpec((None, tr, cdim), lambda i, idx_ref: (layer, i, idx_ref[0]))
            full = (2, r, 2 * cdim)
    in_specs, args, aliases = [s_spec], [idx, slots], {}
    if prev is not None:
        in_specs.append(ANY)
        args.append(prev)
        aliases = {2: 0}
    return pl.pallas_call(
        body, name=name,
        grid_spec=pltpu.PrefetchScalarGridSpec(
            num_scalar_prefetch=1, grid=grid, in_specs=in_specs, out_specs=o_spec),
        out_shape=jax.ShapeDtypeStruct(full, F32),
        input_output_aliases=aliases,
        compiler_params=_cp(*(["parallel"] * len(grid))),
    )(*args)


def _block_diag(w):
    w4 = w.reshape(2, 4, 4, 64, 64)
    eye = jnp.eye(4, dtype=w.dtype)[None, None, :, None, :, None]
    return (w4[:, :, :, :, None, :] * eye).reshape(2, 4, 256, 256)


def _block_diag_extract(d):
    d5 = d.reshape(4, 4, 64, 4, 64)
    return jnp.stack([d5[:, hh, :, hh, :] for hh in range(4)], axis=1).reshape(-1)


REP_NAMES = ("norm1_g", "lru_conv_b", "lru_ba", "lru_bx", "lru_lambda", "norm2_g", "lru_wa", "lru_wx")


def _pack_rep(norm1_g, conv_b, ba, bx, lam, norm2_g, wa, wx, final_g):
    parts = [a.reshape(-1) for a in (norm1_g, conv_b, ba, bx, lam, norm2_g, wa, wx, final_g)]
    return jnp.concatenate(parts).reshape(REP_ROWS, LANES)


def _unpack_rep(buf):
    flat = buf.reshape(-1)
    res, o = {}, 0
    for k in REP_NAMES:
        shape = (2, 16, 64, 64) if k in ("lru_wa", "lru_wx") else (2, 1024)
        n = math.prod(shape)
        res[k] = flat[o:o + n].reshape(shape)
        o += n
    res["final_g"] = flat[o:o + 1024]
    return res


def _pack_conv_shard(lru_cw, sc_cw, ffn_cw):
    return jnp.concatenate([lru_cw.reshape(16, LANES), jnp.pad(sc_cw.reshape(6, LANES), ((0, 2), (0, 0))),
                            ffn_cw.reshape(72, LANES)], axis=0)


def _unpack_conv_shard(buf):
    return (buf[0:16].reshape(2, 4, 256), buf[16:22].reshape(2, 3, 128), buf[24:96].reshape(2, 3, 1536))


def kernel(x, norm1_g, w_in, lru_conv_w, lru_conv_b, lru_wa, lru_ba, lru_wx, lru_bx, lru_lambda, sc_conv_w, w_out, norm2_g, w_up, ffn_conv_w, w_down, final_g, loss_target, m_norm1_g, m_w_in, m_lru_conv_w, m_lru_conv_b, m_lru_wa, m_lru_ba, m_lru_wx, m_lru_bx, m_lru_lambda, m_sc_conv_w, m_w_out, m_norm2_g, m_w_up, m_ffn_conv_w, m_w_down, m_final_g, v_norm1_g, v_w_in, v_lru_conv_w, v_lru_conv_b, v_lru_wa, v_lru_ba, v_lru_wx, v_lru_bx, v_lru_lambda, v_sc_conv_w, v_w_out, v_norm2_g, v_w_up, v_ffn_conv_w, v_w_down, v_final_g):
    me = 2 * lax.axis_index("x") + lax.axis_index("y")
    idx = jnp.stack([lax.axis_index("c"), me]).astype(jnp.int32)
    t_len = x.shape[1]

    s_conv = _pack_conv_shard(lru_conv_w, sc_conv_w, ffn_conv_w)
    conv_slots = lax.dynamic_update_slice(jnp.zeros((N_CHIP, CONV_PACK_ROWS, LANES), F32), s_conv[None], (me, 0, 0))
    wi = list(_cast_into_full(w_in, W_IN, idx, "cast_w_in"))
    wo = list(_cast_into_full(w_out, W_OUT, idx, "cast_w_out"))
    wu = list(_cast_into_full(w_up, W_UP, idx, "cast_w_up"))
    wd = list(_cast_into_full(w_down, W_DOWN, idx, "cast_w_down"))
    wi[0], convs = _comm_call(_gather_plan([wi[0], conv_slots], [W_IN, SMALL]), "ag_first")
    per_chip = [_unpack_conv_shard(convs[k]) for k in range(N_CHIP)]
    lru_cw = jnp.concatenate([p[0] for p in per_chip], axis=-1)
    sc_cw = jnp.concatenate([p[1] for p in per_chip], axis=-1)
    ffn_cw = jnp.concatenate([p[2] for p in per_chip], axis=-1)

    prm = jnp.concatenate(
        [lru_cw, jnp.stack([lru_conv_b, lru_ba, lru_bx, lru_lambda], axis=1),
         jnp.pad(sc_cw, ((0, 0), (0, 0), (0, D_LRU - D_SC))), jnp.zeros((2, 5, D_LRU), F32)], axis=1)
    gates = jnp.concatenate([_block_diag(lru_wa), _block_diag(lru_wx)], axis=1).astype(BF16)
    fw8 = jnp.pad(ffn_cw.reshape(2, 3, 2, D_FF).transpose(0, 2, 1, 3), ((0, 0), (0, 0), (0, 5), (0, 0)))

    xs = x[0]
    saved = []
    n512, n256 = t_len // 512, t_len // 256
    whole, lower, upper = (0, 1), (0, 2), (1, 2)
    carried_by = {
        "fwd_in_0": ([(wu, 0, W_UP, (0, 4))], (max(n512 - 3, 0),)),
        "fwd_mixer_0": ([(wu, 0, W_UP, (1, 4, 2)), (wo, 0, W_OUT, whole)], (max(n256 - 3, 0),)),
        "fwd_out_0": ([(wu, 0, W_UP, (3, 4))], (max(n512 - 2, 0),)),
        "fwd_up_0": ([(wd, 0, W_DOWN, whole)], (max(n512 - 2, 0),)),
        "fwd_act_0": ([(wi, 1, W_IN, whole), (wo, 1, W_OUT, whole)], (max(n512 - 2, 0), 0)),
        "fwd_down_0": ([(wu, 1, W_UP, (0, 4))], (max(n512 - 3, 0),)),
        "fwd_in_1": ([(wu, 1, W_UP, (1, 4))], (max(n512 - 3, 0),)),
        "fwd_mixer_1": ([(wu, 1, W_UP, (2, 4, 2))], (max(n256 - 4, 0),)),
        "fwd_act_1": ([(wd, 1, W_DOWN, whole)], (max(n512 - 3, 0), 0)),
    }

    def carried(name):
        if name not in carried_by:
            return None, lambda got: None
        items, mid_at = carried_by[name]

        def store(got):
            for (lst, i, _, _), arr in zip(items, got):
                lst[i] = arr

        return _gather_plan([lst[i] for lst, i, _, _ in items], [k for _, _, k, _ in items], mid_at=mid_at,
                            parts=[p for _, _, _, p in items]), store

    for l in range(2):
        comm, store = carried(f"fwd_in_{l}")
        (z, h1), got = _norm_mm(xs, norm1_g[l][None], wi[l], f"fwd_in_{l}", comm=comm)
        store(got)
        comm, store = carried(f"fwd_mixer_{l}")
        (ymix, hst), got = _mixer_fwd(z, prm, gates, l, f"fwd_mixer_{l}", comm=comm)
        store(got)
        comm, store = carried(f"fwd_out_{l}")
        (x2,), got = _mm_res(ymix, wo[l], xs, f"fwd_out_{l}", comm=comm)
        store(got)
        comm, store = carried(f"fwd_up_{l}")
        (u, h2), got = _norm_mm(x2, norm2_g[l][None], wu[l], f"fwd_up_{l}", planes=True, comm=comm)
        store(got)
        comm, store = carried(f"fwd_act_{l}")
        (act, fg, fu), got = _ffn_act(u, fw8, l, f"fwd_act_{l}", comm=comm)
        store(got)
        comm, store = carried(f"fwd_down_{l}")
        (x3,), got = _mm_res(act, wd[l], x2, f"fwd_down_{l}", comm=comm)
        store(got)
        saved.append((xs, h1, z, hst, ymix, x2, h2, u, act, fg, fu))
        xs = x3

    dx, dxb, dgf, loss_blk = _loss_head(xs, final_g[None], loss_target[0], "loss_head")

    kinds = [W_IN, W_OUT, W_UP, W_DOWN]
    grads = [None, None]
    small = [None, None]
    reduced = [None] * 4
    summed1 = [None] * 4
    slots1 = [None] * 4

    def scatter1(ws):
        return _scatter_plan([summed1[w][0] for w in ws], [summed1[w][1] for w in ws], ws)

    for l in (1, 0):
        x_in, h1, z, hst, ymix, x2, h2, u, act, fg, fu = saved[l]
        carry = l == 0
        comm = _pair_plan([grads[1][W_IN]], [W_IN]) if carry else None
        (g_down,), got = _mm_tn(act, dxb, f"bwd_wdown_{l}", tk=1536, tn=1024, comm=comm)
        if carry:
            summed1[W_IN] = _pair_sum(grads[1][W_IN], got[0], idx, W_IN, "rs_add1_0")
        (dact,), _ = _mm_nt(dxb, wd[l], f"bwd_dact_{l}")
        comm = scatter1((W_UP, W_IN)) if carry else None
        (du, dfw), got = _ffn_bwd(dact, fg, fu, u, fw8, l, f"bwd_act_{l}", comm=comm)
        if carry:
            slots1[W_UP], slots1[W_IN] = got
            reduced = [_sum_slots(slots1[w], idx, kinds[w], 1, None, f"rs_sum1_{w}") for w in range(4)]
        comm = _share_plan(reduced, kinds, 1) if carry else None
        (g_up,), got = _mm_tn(h2, du, f"bwd_wup_{l}", tk=1024, tn=1536, planes=True, comm=comm)
        if carry:
            reduced = list(got)
        comm = _pair_plan([g_up, g_down], [W_UP, W_DOWN]) if carry else None
        (dx2, dx2b, dg2), got = _mm_nt_normbwd(du, wu[l], x2, norm2_g[l][None], dx, f"bwd_up_{l}", planes=True,
                                               comm=comm)
        if carry:
            sum_up = _pair_sum(g_up, got[0], idx, W_UP, "rs_add0_2")
            sum_down = _pair_sum(g_down, got[1], idx, W_DOWN, "rs_add0_3")
        (g_out,), _ = _mm_tn(ymix, dx2b, f"bwd_wout_{l}", tk=1536, tn=1024)
        comm = _pair_plan([g_out], [W_OUT]) if carry else None
        (dymix,), got = _mm_nt(dx2b, wo[l], f"bwd_dymix_{l}", comm=comm)
        trio = (W_OUT, W_UP, W_DOWN)
        if carry:
            sum_out = _pair_sum(g_out, got[0], idx, W_OUT, "rs_add0_1")
            comm = _scatter_plan([sum_out[0], sum_up[0], sum_down[0]], [sum_out[1], sum_up[1], sum_down[1]], trio)
        else:
            comm = _pair_plan([g_out, g_up, g_down], trio)
        (dz, dprm, dgates), got = _mixer_bwd(z, hst, dymix, prm, gates, l, f"bwd_mixer_{l}", comm=comm)
        if carry:
            for w, s in zip(trio, got):
                reduced[w] = _sum_slots(s, idx, w, 0, reduced[w], f"rs_sum0_{w}")
        else:
            for w, g, land in zip(trio, (g_out, g_up, g_down), got):
                summed1[w] = _pair_sum(g, land, idx, w, f"rs_add1_{w}")
        comm = _share_plan([reduced[w] for w in trio], trio, 0) if carry else scatter1((W_DOWN,))
        (g_in,), got = _mm_tn(h1, dz, f"bwd_win_{l}", tk=1024, tn=1792, comm=comm)
        if carry:
            for w, full in zip(trio, got):
                reduced[w] = full
        else:
            slots1[W_DOWN], = got
        if carry:
            land_in, = _comm_call(_pair_plan([g_in], [W_IN]), "rs_pair_in")
            sum_in = _pair_sum(g_in, land_in, idx, W_IN, "rs_add0_0")
            comm = _scatter_plan([sum_in[0]], [sum_in[1]], [W_IN])
        else:
            comm = scatter1((W_OUT,))
        (dx, dxb, dg1), got = _mm_nt_normbwd(dz, wi[l], x_in, norm1_g[l][None], dx2, f"bwd_in_{l}", comm=comm)
        if carry:
            slot_in, = got
        else:
            slots1[W_OUT], = got
        grads[l] = [g_in, g_out, g_up, g_down]
        rep = dict(zip(REP_NAMES, [dg1[0], dprm[4], dprm[5], dprm[6], dprm[7], dg2[0],
                                   _block_diag_extract(dgates[0:4]), _block_diag_extract(dgates[4:8])]))
        conv = [dprm[0:4].reshape(-1), jnp.pad(dprm[8:11, 0:D_SC].reshape(-1), (0, 512)),
                dfw[:, 0:3, :].transpose(1, 0, 2).reshape(-1)]
        small[l] = (rep, conv)
    grad_x = dx[None]
    g_small = jnp.concatenate(
        [small[l][0][k] for k in REP_NAMES for l in range(2)] + [dgf[0]] + small[0][1] + small[1][1]
        + [loss_blk.reshape(-1)]).reshape(SMALL_ROWS, LANES)

    def big(w, g, m, v, name):
        shape = w.shape
        two_d = lambda a: a.reshape(-1, shape[-1])
        outs, _ = _adamw(two_d(w), two_d(g), two_d(m), two_d(v), name, emit_grad=True)
        return [o.reshape(shape) for o in outs]

    land_small, = _comm_call(_pair_plan([g_small], [SMALL]), "rs_pair_small")
    sum_small = _pair_sum(g_small, land_small, idx, SMALL, "rs_add0_4")
    slot_small, = _comm_call(_scatter_plan([sum_small[0]], [sum_small[1]], [SMALL]), "rs_scatter_small")
    gw_in, gs = _comm_call(
        _share_plan([_sum_slots(slot_in, idx, W_IN, 0, reduced[W_IN], "rs_sum0_0"),
                     _sum_slots(slot_small, idx, SMALL, 0, None, "rs_sum0_4")], [W_IN, SMALL], 0), "rs_share0")
    upd = {"w_up": big(w_up, reduced[W_UP], m_w_up, v_w_up, "adamw_w_up"),
           "w_down": big(w_down, reduced[W_DOWN], m_w_down, v_w_down, "adamw_w_down"),
           "w_out": big(w_out, reduced[W_OUT], m_w_out, v_w_out, "adamw_w_out"),
           "w_in": big(w_in, gw_in, m_w_in, v_w_in, "adamw_w_in")}

    loss = gs[REP_ROWS + CONV_ROWS, 0]
    g_rep = gs[0:REP_ROWS]
    g_conv = gs[REP_ROWS:REP_ROWS + CONV_ROWS].reshape(2, CONV_LAYER)
    g_lru_cw = lax.dynamic_slice_in_dim(g_conv[:, 0:4096].reshape(2, 4, 1024), me * 256, 256, axis=2)
    g_sc_cw = lax.dynamic_slice_in_dim(g_conv[:, 4096:4096 + 1536].reshape(2, 3, 512), me * 128, 128, axis=2)
    g_ffn_cw = lax.dynamic_slice_in_dim(g_conv[:, 6144:].reshape(2, 3, 6144), me * 1536, 1536, axis=2)

    rep_out, _ = _adamw(
        _pack_rep(norm1_g, lru_conv_b, lru_ba, lru_bx, lru_lambda, norm2_g, lru_wa, lru_wx, final_g), g_rep,
        _pack_rep(m_norm1_g, m_lru_conv_b, m_lru_ba, m_lru_bx, m_lru_lambda, m_norm2_g, m_lru_wa, m_lru_wx, m_final_g),
        _pack_rep(v_norm1_g, v_lru_conv_b, v_lru_ba, v_lru_bx, v_lru_lambda, v_norm2_g, v_lru_wa, v_lru_wx, v_final_g),
        "adamw_rep")
    conv_out, _ = _adamw(s_conv, _pack_conv_shard(g_lru_cw, g_sc_cw, g_ffn_cw),
                         _pack_conv_shard(m_lru_conv_w, m_sc_conv_w, m_ffn_conv_w),
                         _pack_conv_shard(v_lru_conv_w, v_sc_conv_w, v_ffn_conv_w), "adamw_conv")

    names = ["norm1_g", "w_in", "lru_conv_w", "lru_conv_b", "lru_wa", "lru_ba", "lru_wx", "lru_bx", "lru_lambda",
             "sc_conv_w", "w_out", "norm2_g", "w_up", "ffn_conv_w", "w_down", "final_g"]
    groups = []
    g_all = dict(_unpack_rep(g_rep))
    g_all.update({k: v[3] for k, v in upd.items()})
    g_all.update(lru_conv_w=g_lru_cw, sc_conv_w=g_sc_cw, ffn_conv_w=g_ffn_cw)
    groups.append(g_all)
    for i in range(3):
        d = dict(_unpack_rep(rep_out[i]))
        cl, cs, cf = _unpack_conv_shard(conv_out[i])
        d.update(lru_conv_w=cl, sc_conv_w=cs, ffn_conv_w=cf)
        d.update({k: v[i] for k, v in upd.items()})
        groups.append(d)
    return (loss, grad_x, *[grp[n] for grp in groups for n in names])
```

```python
import dataclasses
import functools
import math
import operator
from typing import Any, Callable, Optional, Sequence

import jax
import jax.numpy as jnp
from jax import lax
from jax.experimental import pallas as pl
from jax.experimental.pallas import tpu as pltpu

F32 = jnp.float32
BF16 = jnp.bfloat16
MESH = pl.DeviceIdType.MESH

D_MODEL = 1024
D_LRU = 1024
D_SC = 512
D_MIX = D_LRU + D_SC
D_IN = 2 * D_LRU + 3 * D_SC
D_FF = 3072
N_CHIP = 4
RG_C = 8.0
EPS = 1e-6
ADAM_LR = 0.001
ADAM_B1 = 0.9
ADAM_B2 = 0.999
ADAM_EPS = 1e-08
ADAM_WD = 0.01
ADAM_STEP = 10

SUBLANES = 8
PACKED = 16
LANES = 128
VMEM_LIMIT = 56 * 1024 * 1024
GELU_C0 = math.sqrt(2.0 / math.pi)
GELU_C1 = 0.044715

REP_LAYER = 6 * 1024 + 2 * 16 * 64 * 64
REP_ROWS = (2 * REP_LAYER + 1024) // LANES
CONV_LAYER = 4 * 1024 + 2048 + 3 * 6144
CONV_ROWS = 2 * CONV_LAYER // LANES
SMALL_ROWS = REP_ROWS + CONV_ROWS + 8
CONV_PACK_ROWS = 96

W_IN, W_OUT, W_UP, W_DOWN, SMALL = range(5)
COL_SHARDED = {W_IN: True, W_OUT: False, W_UP: True, W_DOWN: False}

ONCE = pl.Buffered(1)
ANY = pl.BlockSpec(memory_space=pl.ANY)


def _cp(*sem):
    return pltpu.CompilerParams(dimension_semantics=sem, vmem_limit_bytes=VMEM_LIMIT)


@dataclasses.dataclass
class Comm:
    srcs: Sequence[Any]
    bufs: Sequence[Any]
    outs: Sequence[Any]
    n_sem: int
    start: Callable
    finish: Callable
    mid: Optional[Callable] = None
    mid_at: Optional[Sequence[int]] = None


def _pallas(body, *, name, grid, in_specs, out_specs, out_shape, args, sem, scratch_shapes=(), comm=None):
    if comm is None:
        res = pl.pallas_call(
            body, name=name, grid=grid, in_specs=list(in_specs), out_specs=list(out_specs),
            out_shape=list(out_shape), scratch_shapes=list(scratch_shapes), compiler_params=_cp(*sem))(*args)
        return tuple(res), ()
    n_in, n_out, n_scr = len(in_specs), len(out_specs), len(scratch_shapes)
    ns, nb, no = len(comm.srcs), len(comm.bufs), len(comm.outs)

    def carrier(*refs):
        p = 0
        main_in = refs[p:p + n_in]
        p += n_in
        c_src = refs[p:p + ns]
        p += ns + nb
        main_out = refs[p:p + n_out]
        p += n_out
        c_buf = refs[p:p + nb]
        p += nb
        c_out = refs[p:p + no]
        p += no
        scr = refs[p:p + n_scr]
        send, recv = refs[p + n_scr], refs[p + n_scr + 1]
        ids = [pl.program_id(a) for a in range(len(grid))]

        def at(steps):
            return functools.reduce(operator.and_, [i == s for i, s in zip(ids, steps)])

        @pl.when(at([0] * len(grid)))
        def _():
            comm.start(c_src, c_buf, c_out, send, recv)

        if comm.mid is not None:
            @pl.when(at(comm.mid_at))
            def _():
                comm.mid(c_src, c_buf, c_out, send, recv)

        body(*main_in, *main_out, *scr)

        @pl.when(at([g - 1 for g in grid]))
        def _():
            comm.finish(c_src, c_buf, c_out, send, recv)

    res = pl.pallas_call(
        carrier, name=name, grid=grid,
        in_specs=list(in_specs) + [ANY] * (ns + nb),
        out_specs=list(out_specs) + [ANY] * (nb + no),
        out_shape=list(out_shape) + [jax.ShapeDtypeStruct(b.shape, b.dtype) for b in comm.bufs] + list(comm.outs),
        input_output_aliases={n_in + ns + j: n_out + j for j in range(nb)},
        scratch_shapes=list(scratch_shapes) + [pltpu.SemaphoreType.DMA((comm.n_sem,)),
                                               pltpu.SemaphoreType.DMA((comm.n_sem,))],
        compiler_params=_cp(*(["arbitrary"] * len(grid))),
    )(*args, *comm.srcs, *comm.bufs)
    return tuple(res[:n_out]), tuple(res[n_out:])


def _comm_call(comm, name):
    ns, nb, no = len(comm.srcs), len(comm.bufs), len(comm.outs)

    def body(*refs):
        c_src = refs[0:ns]
        c_buf = refs[ns + nb:ns + 2 * nb]
        c_out = refs[ns + 2 * nb:ns + 2 * nb + no]
        send, recv = refs[ns + 2 * nb + no], refs[ns + 2 * nb + no + 1]
        comm.start(c_src, c_buf, c_out, send, recv)
        if comm.mid is not None:
            comm.mid(c_src, c_buf, c_out, send, recv)
        comm.finish(c_src, c_buf, c_out, send, recv)

    return tuple(pl.pallas_call(
        body, name=name,
        in_specs=[ANY] * (ns + nb), out_specs=[ANY] * (nb + no),
        out_shape=[jax.ShapeDtypeStruct(b.shape, b.dtype) for b in comm.bufs] + list(comm.outs),
        input_output_aliases={ns + j: j for j in range(nb)},
        scratch_shapes=[pltpu.SemaphoreType.DMA((comm.n_sem,)), pltpu.SemaphoreType.DMA((comm.n_sem,))],
    )(*comm.srcs, *comm.bufs))


def _sigmoid(v):
    return 1.0 / (1.0 + jnp.exp(-v))


def _sigmoid_tanh(v):
    return 0.5 + 0.5 * jnp.tanh(0.5 * v)


def _gelu_parts(v):
    v2 = v * v
    t = jnp.tanh(v * (GELU_C0 + (GELU_C0 * GELU_C1) * v2))
    half = 0.5 + 0.5 * t
    gel = v * half
    dgel = half + (0.5 * v) * (1.0 - t * t) * (GELU_C0 + (3.0 * GELU_C0 * GELU_C1) * v2)
    return gel, dgel


def _gelu(v):
    t = jnp.tanh(v * (GELU_C0 + (GELU_C0 * GELU_C1) * (v * v)))
    return v * (0.5 + 0.5 * t)


def _neg_expm1(y, a):
    p = jnp.full_like(y, 1.0 / 120.0)
    for coef in (1.0 / 24.0, 1.0 / 6.0, 0.5, 1.0):
        p = p * y + coef
    return jnp.where(y > -0.1, -(p * y), 1.0 - a * a)


def _softplus_neg(lam):
    nl = -lam
    e = jnp.exp(-jnp.abs(nl))
    u = 1.0 + e
    l1p = jnp.where(u == 1.0, e, jnp.log(u) * e / (u - 1.0))
    return jnp.maximum(nl, 0.0) + l1p


def _conv_taps(ext, taps, n_out):
    kw = len(taps)
    acc = taps[kw - 1] * ext[SUBLANES:SUBLANES + n_out]
    for k in range(kw - 1):
        acc = acc + taps[k] * pltpu.roll(ext, kw - 1 - k, axis=0)[SUBLANES:SUBLANES + n_out]
    return acc


def _conv_taps_t(ext, taps, n_out):
    kw = len(taps)
    n = ext.shape[0]
    acc = taps[kw - 1] * ext[0:n_out]
    for k in range(kw - 1):
        acc = acc + taps[k] * pltpu.roll(ext, n - (kw - 1 - k), axis=0)[0:n_out]
    return acc


def _scan8(a, b, carry, row):
    for s in (1, 2, 4):
        m = row >= s
        a_sh = jnp.where(m, pltpu.roll(a, s, axis=0), 1.0)
        b_sh = jnp.where(m, pltpu.roll(b, s, axis=0), 0.0)
        b = a * b_sh + b
        a = a * a_sh
    return a * carry + b


def _scan8_rev(a, b, carry, row):
    for s in (1, 2, 4):
        m = row < SUBLANES - s
        a_sh = jnp.where(m, pltpu.roll(a, SUBLANES - s, axis=0), 1.0)
        b_sh = jnp.where(m, pltpu.roll(b, SUBLANES - s, axis=0), 0.0)
        b = a * b_sh + b
        a = a * a_sh
    return a * carry + b


def _cast_into_full(w, kind, idx, name):
    nl, r, c = w.shape
    tr = 256 if r % 256 == 0 else r
    nrb = r // tr

    def body(idx_ref, w_ref, o0_ref, o1_ref):
        o0_ref[...] = w_ref[0].astype(BF16)
        o1_ref[...] = w_ref[1].astype(BF16)

    if COL_SHARDED[kind]:
        full = (r, N_CHIP * c)
        o_spec = pl.BlockSpec((tr, c), lambda i, idx_ref: (i, idx_ref[1]))
    else:
        full = (N_CHIP * r, c)
        o_spec = pl.BlockSpec((tr, c), lambda i, idx_ref: (idx_ref[1] * nrb + i, 0))
    return pl.pallas_call(
        body, name=name,
        grid_spec=pltpu.PrefetchScalarGridSpec(
            num_scalar_prefetch=1, grid=(nrb,),
            in_specs=[pl.BlockSpec((nl, tr, c), lambda i, idx_ref: (0, i, 0))], out_specs=[o_spec, o_spec]),
        out_shape=[jax.ShapeDtypeStruct(full, BF16)] * 2,
        compiler_params=_cp("parallel"),
    )(idx, w)


def _norm_mm(x, g, w, name, planes=False, tm=512, tn=512, comm=None):
    t_len, d = x.shape
    n = w.shape[1]
    half = n // 2

    def body(x_ref, g_ref, w_ref, z_ref, h_ref):
        xv = x_ref[...]
        r = lax.rsqrt(jnp.mean(xv * xv, axis=-1, keepdims=True) + EPS)
        h_ref[...] = ((xv * r) * g_ref[...]).astype(BF16)
        for n0 in range(0, n, tn):
            blk = jnp.dot(h_ref[...], w_ref[:, n0:n0 + tn], preferred_element_type=F32).astype(BF16)
            if planes:
                z_ref[n0 // half, :, n0 % half:n0 % half + tn] = blk
            else:
                z_ref[:, n0:n0 + tn] = blk

    if planes:
        z_shape = jax.ShapeDtypeStruct((2, t_len, half), BF16)
        z_spec = pl.BlockSpec((2, tm, half), lambda i: (0, i, 0))
    else:
        z_shape = jax.ShapeDtypeStruct((t_len, n), BF16)
        z_spec = pl.BlockSpec((tm, n), lambda i: (i, 0))
    return _pallas(
        body, name=name, grid=(t_len // tm,),
        in_specs=[pl.BlockSpec((tm, d), lambda i: (i, 0)),
                  pl.BlockSpec((1, d), lambda i: (0, 0)),
                  pl.BlockSpec((d, n), lambda i: (0, 0), pipeline_mode=ONCE)],
        out_specs=[z_spec, pl.BlockSpec((tm, d), lambda i: (i, 0))],
        out_shape=[z_shape, jax.ShapeDtypeStruct((t_len, d), BF16)],
        args=(x, g, w), sem=("parallel",), comm=comm)


def _mm_res(a, w, res, name, tm=512, comm=None):
    t_len, k = a.shape
    n = w.shape[1]

    def body(a_ref, w_ref, r_ref, o_ref):
        o_ref[...] = r_ref[...] + jnp.dot(a_ref[...], w_ref[...], preferred_element_type=F32)

    return _pallas(
        body, name=name, grid=(t_len // tm,),
        in_specs=[pl.BlockSpec((tm, k), lambda i: (i, 0)),
                  pl.BlockSpec((k, n), lambda i: (0, 0), pipeline_mode=ONCE),
                  pl.BlockSpec((tm, n), lambda i: (i, 0))],
        out_specs=[pl.BlockSpec((tm, n), lambda i: (i, 0))],
        out_shape=[jax.ShapeDtypeStruct((t_len, n), F32)],
        args=(a, w, res), sem=("parallel",), comm=comm)


def _mm_nt(a, w, name, tm=512, comm=None):
    t_len, k = a.shape
    n = w.shape[0]

    def body(a_ref, w_ref, o_ref):
        o_ref[...] = lax.dot_general(a_ref[...], w_ref[...], (((1,), (1,)), ((), ())),
                                     preferred_element_type=F32).astype(BF16)

    return _pallas(
        body, name=name, grid=(t_len // tm,),
        in_specs=[pl.BlockSpec((tm, k), lambda i: (i, 0)),
                  pl.BlockSpec((n, k), lambda i: (0, 0), pipeline_mode=ONCE)],
        out_specs=[pl.BlockSpec((tm, n), lambda i: (i, 0))],
        out_shape=[jax.ShapeDtypeStruct((t_len, n), BF16)],
        args=(a, w), sem=("parallel",), comm=comm)


def _mm_nt_normbwd(dz, w, x, g, dres, name, planes=False, tm=512, comm=None):
    t_len, d = x.shape
    n = w.shape[1]
    half = n // 2
    nt_dims = (((1,), (1,)), ((), ()))

    def body(dz_ref, w_ref, x_ref, g_ref, r_ref, dx_ref, dxb_ref, dg_ref):
        @pl.when(pl.program_id(0) == 0)
        def _():
            dg_ref[...] = jnp.zeros_like(dg_ref)

        if planes:
            dh = (lax.dot_general(dz_ref[0], w_ref[:, 0:half], nt_dims, preferred_element_type=F32)
                  + lax.dot_general(dz_ref[1], w_ref[:, half:], nt_dims, preferred_element_type=F32))
        else:
            dh = lax.dot_general(dz_ref[...], w_ref[...], nt_dims, preferred_element_type=F32)
        xv = x_ref[...]
        r = lax.rsqrt(jnp.mean(xv * xv, axis=-1, keepdims=True) + EPS)
        xh = xv * r
        dhg = dh * g_ref[...]
        dx = r_ref[...] + r * (dhg - xh * jnp.mean(dhg * xh, axis=-1, keepdims=True))
        dx_ref[...] = dx
        dxb_ref[...] = dx.astype(BF16)
        dg_ref[0:1, :] += jnp.sum(dh * xh, axis=0, keepdims=True)

    if planes:
        dz_spec = pl.BlockSpec((2, tm, half), lambda i: (0, i, 0))
    else:
        dz_spec = pl.BlockSpec((tm, n), lambda i: (i, 0))
    return _pallas(
        body, name=name, grid=(t_len // tm,),
        in_specs=[dz_spec,
                  pl.BlockSpec((d, n), lambda i: (0, 0), pipeline_mode=ONCE),
                  pl.BlockSpec((tm, d), lambda i: (i, 0)),
                  pl.BlockSpec((1, d), lambda i: (0, 0)),
                  pl.BlockSpec((tm, d), lambda i: (i, 0))],
        out_specs=[pl.BlockSpec((tm, d), lambda i: (i, 0)),
                   pl.BlockSpec((tm, d), lambda i: (i, 0)),
                   pl.BlockSpec((SUBLANES, d), lambda i: (0, 0))],
        out_shape=[jax.ShapeDtypeStruct((t_len, d), F32),
                   jax.ShapeDtypeStruct((t_len, d), BF16),
                   jax.ShapeDtypeStruct((SUBLANES, d), F32)],
        args=(dz, w, x, g, dres), sem=("arbitrary",), comm=comm)


def _mm_tn(a, g, name, tk, tn, planes=False, tt=1024, comm=None):
    t_len, k = a.shape
    n = 2 * g.shape[2] if planes else g.shape[1]
    nn = n // tn
    half = nn // 2
    tt = min(tt, t_len)

    def body(a_ref, g_ref, o_ref):
        @pl.when(pl.program_id(2) == 0)
        def _():
            o_ref[...] = jnp.zeros_like(o_ref)

        o_ref[...] += lax.dot_general(a_ref[...], g_ref[...], (((0,), (0,)), ((), ())),
                                      preferred_element_type=F32)

    if planes:
        g_spec = pl.BlockSpec((None, tt, tn), lambda i, j, t: (j // half, t, j % half))
    else:
        g_spec = pl.BlockSpec((tt, tn), lambda i, j, t: (t, j))
    return _pallas(
        body, name=name, grid=(k // tk, nn, t_len // tt),
        in_specs=[pl.BlockSpec((tt, tk), lambda i, j, t: (t, i)), g_spec],
        out_specs=[pl.BlockSpec((tk, tn), lambda i, j, t: (i, j))],
        out_shape=[jax.ShapeDtypeStruct((k, n), F32)],
        args=(a, g), sem=("parallel", "parallel", "arbitrary"), comm=comm)


def _lru_gates(rp, ip, spn):
    r = _sigmoid(rp)
    i = _sigmoid_tanh(ip)
    la = r * spn
    a = jnp.exp(la)
    mult = jnp.sqrt(_neg_expm1(2.0 * la, a))
    return r, i, a, mult


def _mixer_fwd(z, prm, gates, layer, name, tb=256, comm=None):
    t_len = z.shape[0]

    def body(z_ref, p_ref, g_ref, y_ref, h_ref, xhalo, phalo, hcar, lx_s, rp_s, ip_s):
        @pl.when(pl.program_id(0) == 0)
        def _():
            xhalo[...] = jnp.zeros_like(xhalo)
            phalo[...] = jnp.zeros_like(phalo)
            hcar[...] = jnp.zeros_like(hcar)

        prm_v = p_ref[...]
        cw = prm_v[0:4]
        vec = prm_v[4:8]
        xp = z_ref[:, 0:D_LRU].astype(F32)
        ext = jnp.concatenate([xhalo[...], xp], axis=0)
        lx = vec[0:1] + _conv_taps(ext, [cw[k:k + 1] for k in range(4)], tb)
        xhalo[...] = xp[tb - SUBLANES:]
        lx_s[...] = lx
        lxb = lx.astype(BF16)
        for q in range(4):
            sl = slice(q * 256, (q + 1) * 256)
            rp_s[:, sl] = jnp.dot(lxb[:, sl], g_ref[q], preferred_element_type=F32) + vec[1:2, sl]
            ip_s[:, sl] = jnp.dot(lxb[:, sl], g_ref[4 + q], preferred_element_type=F32) + vec[2:3, sl]

        spn = jnp.broadcast_to(-RG_C * _softplus_neg(vec[3:4]), (SUBLANES, D_LRU))
        row = lax.broadcasted_iota(jnp.int32, (SUBLANES, D_LRU), 0)

        def step(ci, carry):
            o = pl.multiple_of(ci * PACKED, PACKED)
            gate = z_ref[pl.ds(o, PACKED), D_LRU:2 * D_LRU].astype(F32)
            ys = []
            for sub in range(2):
                rows = pl.ds(pl.multiple_of(o + sub * SUBLANES, SUBLANES), SUBLANES)
                lxv = lx_s[rows, :]
                _, i, a, mult = _lru_gates(rp_s[rows, :], ip_s[rows, :], spn)
                h = _scan8(a, mult * (i * lxv), carry, row)
                h_ref[rows, :] = h
                ys.append(h * _gelu(gate[sub * SUBLANES:(sub + 1) * SUBLANES]))
                carry = jnp.broadcast_to(h[SUBLANES - 1:SUBLANES, :], (SUBLANES, D_LRU))
            y_ref[pl.ds(o, PACKED), 0:D_LRU] = jnp.concatenate(ys, axis=0).astype(BF16)
            return carry

        hcar[...] = lax.fori_loop(0, tb // PACKED, step, hcar[...])

        scw = prm_v[8:11, 0:D_SC]
        o_b, o_c, o_x = 2 * D_LRU, 2 * D_LRU + D_SC, 2 * D_LRU + 2 * D_SC
        p = z_ref[:, o_c:o_x].astype(F32) * z_ref[:, o_x:].astype(F32)
        pext = jnp.concatenate([phalo[...], p], axis=0)
        q = _conv_taps(pext, [scw[k:k + 1] for k in range(3)], tb)
        phalo[...] = p[tb - SUBLANES:]
        y_ref[:, D_LRU:] = (z_ref[:, o_b:o_c].astype(F32) * q).astype(BF16)

    return _pallas(
        body, name=name, grid=(t_len // tb,),
        in_specs=[pl.BlockSpec((tb, D_IN), lambda t: (t, 0)),
                  pl.BlockSpec((None, 2 * SUBLANES, D_LRU), lambda t: (layer, 0, 0)),
                  pl.BlockSpec((None, 8, 256, 256), lambda t: (layer, 0, 0, 0))],
        out_specs=[pl.BlockSpec((tb, D_MIX), lambda t: (t, 0)),
                   pl.BlockSpec((tb, D_LRU), lambda t: (t, 0))],
        out_shape=[jax.ShapeDtypeStruct((t_len, D_MIX), BF16),
                   jax.ShapeDtypeStruct((t_len, D_LRU), F32)],
        scratch_shapes=[pltpu.VMEM((SUBLANES, D_LRU), F32), pltpu.VMEM((SUBLANES, D_SC), F32),
                        pltpu.VMEM((SUBLANES, D_LRU), F32), pltpu.VMEM((tb, D_LRU), F32),
                        pltpu.VMEM((tb, D_LRU), F32), pltpu.VMEM((tb, D_LRU), F32)],
        args=(z, prm, gates), sem=("arbitrary",), comm=comm)


def _mixer_bwd(z, h, dy, prm, gates, layer, name, tb=256, comm=None):
    t_len = z.shape[0]
    nb = t_len // tb

    def body(z_ref, zh_ref, h_ref, hh_ref, dy_ref, p_ref, g_ref, dz_ref, dp_ref, dg_ref,
             lx_s, rp_s, ip_s, drpb_s, dipb_s, dlx_s, hext_s, acc_s, acar, gcar, dqh):
        t = pl.program_id(0)
        first_block = t == nb - 1

        @pl.when(t == 0)
        def _():
            for ref in (dp_ref, dg_ref, acc_s, acar, gcar, dqh):
                ref[...] = jnp.zeros_like(ref)
            dlx_s[tb:, :] = jnp.zeros((SUBLANES, D_LRU), F32)

        prm_v = p_ref[...]
        cw = prm_v[0:4]
        vec = prm_v[4:8]
        scw = prm_v[8:11, 0:D_SC]
        wa_ref = [g_ref.at[q] for q in range(4)]
        wx_ref = [g_ref.at[4 + q] for q in range(4)]
        dwa_ref = [dg_ref.at[q] for q in range(4)]
        dwx_ref = [dg_ref.at[4 + q] for q in range(4)]
        ctaps = [cw[k:k + 1] for k in range(4)]
        staps = [scw[k:k + 1] for k in range(3)]
        keep = jnp.where(first_block, 0.0, 1.0)
        zh = zh_ref[...].astype(F32)[PACKED - SUBLANES:] * keep

        xp = z_ref[:, 0:D_LRU].astype(F32)
        xext = jnp.concatenate([zh[:, 0:D_LRU], xp], axis=0)
        lx = vec[0:1] + _conv_taps(xext, ctaps, tb)
        lx_s[...] = lx
        lxb = lx.astype(BF16)
        for q in range(4):
            sl = slice(q * 256, (q + 1) * 256)
            rp_s[:, sl] = jnp.dot(lxb[:, sl], wa_ref[q][...], preferred_element_type=F32) + vec[1:2, sl]
            ip_s[:, sl] = jnp.dot(lxb[:, sl], wx_ref[q][...], preferred_element_type=F32) + vec[2:3, sl]
        hext_s[0:SUBLANES, :] = hh_ref[...] * keep
        hext_s[SUBLANES:, :] = h_ref[...]

        spn = jnp.broadcast_to(-RG_C * _softplus_neg(vec[3:4]), (SUBLANES, D_LRU))
        row = lax.broadcasted_iota(jnp.int32, (SUBLANES, D_LRU), 0)

        def step(ci, carry):
            a_next, g_next = carry
            o = pl.multiple_of((tb // PACKED - 1 - ci) * PACKED, PACKED)
            rows16 = pl.ds(o, PACKED)
            gate16 = z_ref[rows16, D_LRU:2 * D_LRU].astype(F32)
            dyl16 = dy_ref[rows16, 0:D_LRU].astype(F32)
            dgs, drs, dis = [None, None], [None, None], [None, None]
            for sub in (1, 0):
                oo = pl.multiple_of(o + sub * SUBLANES, SUBLANES)
                rows = pl.ds(oo, SUBLANES)
                half = slice(sub * SUBLANES, (sub + 1) * SUBLANES)
                lxv = lx_s[rows, :]
                r, i, a, mult = _lru_gates(rp_s[rows, :], ip_s[rows, :], spn)
                hwin = hext_s[pl.ds(oo, 2 * SUBLANES), :]
                hv = hwin[SUBLANES:]
                hprev = pltpu.roll(hwin, 1, axis=0)[SUBLANES:]
                gel, dgel = _gelu_parts(gate16[half])
                dyl = dyl16[half]
                a_up = jnp.where(row < SUBLANES - 1, pltpu.roll(a, SUBLANES - 1, axis=0), a_next)
                gg = _scan8_rev(a_up, dyl * gel, g_next, row)
                dgs[sub] = dyl * hv * dgel
                ilx = i * lxv
                dla = a * (gg * hprev - (gg * ilx) * a / mult)
                dlx_s[rows, :] = gg * mult * i
                drp = dla * spn * r * (1.0 - r)
                dip = gg * mult * lxv * i * (1.0 - i)
                drs[sub] = drp
                dis[sub] = dip
                acc_s[0] += drp
                acc_s[1] += dip
                acc_s[2] += dla * r
                a_next = jnp.broadcast_to(a[0:1, :], (SUBLANES, D_LRU))
                g_next = jnp.broadcast_to(gg[0:1, :], (SUBLANES, D_LRU))
            dz_ref[rows16, D_LRU:2 * D_LRU] = jnp.concatenate(dgs, axis=0).astype(BF16)
            drpb_s[rows16, :] = jnp.concatenate(drs, axis=0).astype(BF16)
            dipb_s[rows16, :] = jnp.concatenate(dis, axis=0).astype(BF16)
            return a_next, g_next

        a_c, g_c = lax.fori_loop(0, tb // PACKED, step, (acar[...], gcar[...]))
        acar[...] = a_c
        gcar[...] = g_c

        drpb = drpb_s[...]
        dipb = dipb_s[...]
        nt_dims = (((1,), (1,)), ((), ()))
        tn_dims = (((0,), (0,)), ((), ()))
        for q in range(4):
            sl = slice(q * 256, (q + 1) * 256)
            dlx_s[0:tb, sl] += (
                lax.dot_general(drpb[:, sl], wa_ref[q][...], nt_dims, preferred_element_type=F32)
                + lax.dot_general(dipb[:, sl], wx_ref[q][...], nt_dims, preferred_element_type=F32))
            dwa_ref[q][...] += lax.dot_general(lxb[:, sl], drpb[:, sl], tn_dims, preferred_element_type=F32)
            dwx_ref[q][...] += lax.dot_general(lxb[:, sl], dipb[:, sl], tn_dims, preferred_element_type=F32)

        dlx_ext = dlx_s[...]
        dlx = dlx_ext[0:tb]
        dz_ref[:, 0:D_LRU] = _conv_taps_t(dlx_ext, ctaps, tb).astype(BF16)
        dp_ref[3:4, :] += jnp.sum(dlx * xp, axis=0, keepdims=True)
        for k in range(3):
            shifted = pltpu.roll(xext, 3 - k, axis=0)[SUBLANES:]
            dp_ref[k:k + 1, :] += jnp.sum(dlx * shifted, axis=0, keepdims=True)
        dp_ref[4:5, :] += jnp.sum(dlx, axis=0, keepdims=True)
        dlx_s[tb:, :] = dlx[0:SUBLANES]

        o_b, o_c, o_x = 2 * D_LRU, 2 * D_LRU + D_SC, 2 * D_LRU + 2 * D_SC
        sb = z_ref[:, o_b:o_c].astype(F32)
        scc = z_ref[:, o_c:o_x].astype(F32)
        sx = z_ref[:, o_x:].astype(F32)
        p = scc * sx
        pext = jnp.concatenate([zh[:, o_c:o_x] * zh[:, o_x:], p], axis=0)
        q = _conv_taps(pext, staps, tb)
        dys = dy_ref[:, D_LRU:].astype(F32)
        dq = dys * sb
        dp = _conv_taps_t(jnp.concatenate([dq, dqh[...]], axis=0), staps, tb)
        dp_ref[10:11, 0:D_SC] += jnp.sum(dq * p, axis=0, keepdims=True)
        for k in range(2):
            shifted = pltpu.roll(pext, 2 - k, axis=0)[SUBLANES:]
            dp_ref[8 + k:9 + k, 0:D_SC] += jnp.sum(dq * shifted, axis=0, keepdims=True)
        dqh[...] = dq[0:SUBLANES]
        dz_ref[:, o_b:o_c] = (dys * q).astype(BF16)
        dz_ref[:, o_c:o_x] = (dp * sx).astype(BF16)
        dz_ref[:, o_x:] = (dp * scc).astype(BF16)

        @pl.when(first_block)
        def _():
            dp_ref[5:6, :] = jnp.sum(acc_s[0], axis=0, keepdims=True)
            dp_ref[6:7, :] = jnp.sum(acc_s[1], axis=0, keepdims=True)
            dp_ref[7:8, :] = (jnp.sum(acc_s[2], axis=0, keepdims=True) * RG_C * _sigmoid(-vec[3:4]))

    blk = lambda t: (nb - 1 - t, 0)
    halo8 = lambda t: (jnp.maximum((nb - 1 - t) * (tb // SUBLANES) - 1, 0), 0)
    halo16 = lambda t: (jnp.maximum((nb - 1 - t) * (tb // PACKED) - 1, 0), 0)
    return _pallas(
        body, name=name, grid=(nb,),
        in_specs=[pl.BlockSpec((tb, D_IN), blk), pl.BlockSpec((PACKED, D_IN), halo16),
                  pl.BlockSpec((tb, D_LRU), blk), pl.BlockSpec((SUBLANES, D_LRU), halo8),
                  pl.BlockSpec((tb, D_MIX), blk),
                  pl.BlockSpec((None, 2 * SUBLANES, D_LRU), lambda t: (layer, 0, 0)),
                  pl.BlockSpec((None, 8, 256, 256), lambda t: (layer, 0, 0, 0))],
        out_specs=[pl.BlockSpec((tb, D_IN), blk),
                   pl.BlockSpec((2 * SUBLANES, D_LRU), lambda t: (0, 0)),
                   pl.BlockSpec((8, 256, 256), lambda t: (0, 0, 0))],
        out_shape=[jax.ShapeDtypeStruct((t_len, D_IN), BF16),
                   jax.ShapeDtypeStruct((2 * SUBLANES, D_LRU), F32),
                   jax.ShapeDtypeStruct((8, 256, 256), F32)],
        scratch_shapes=[pltpu.VMEM((tb, D_LRU), F32),
                        pltpu.VMEM((tb, D_LRU), F32), pltpu.VMEM((tb, D_LRU), F32),
                        pltpu.VMEM((tb, D_LRU), BF16), pltpu.VMEM((tb, D_LRU), BF16),
                        pltpu.VMEM((tb + SUBLANES, D_LRU), F32), pltpu.VMEM((tb + SUBLANES, D_LRU), F32),
                        pltpu.VMEM((3, SUBLANES, D_LRU), F32),
                        pltpu.VMEM((SUBLANES, D_LRU), F32), pltpu.VMEM((SUBLANES, D_LRU), F32),
                        pltpu.VMEM((SUBLANES, D_SC), F32)],
        args=(z, z, h, h, dy, prm, gates), sem=("arbitrary",), comm=comm)


def _ffn_act(u, fw, layer, name, tb=512, tn=1024, rc=64, comm=None):
    t_len = u.shape[1]
    hb = tb // PACKED

    def body(u_ref, uh_ref, fw_ref, o_ref, fg_ref, fu_ref, ext):
        keep = jnp.where(pl.program_id(0) == 0, 0.0, 1.0)
        ext[:, 0:SUBLANES, :] = uh_ref[...].astype(F32)[:, PACKED - SUBLANES:, :] * keep
        ext[:, SUBLANES:, :] = u_ref[...].astype(F32)
        fw_v = fw_ref[...]

        for lb in range(tn // LANES):
            lanes = slice(lb * LANES, (lb + 1) * LANES)
            wg = [fw_v[0, k:k + 1, lanes] for k in range(3)]
            wu = [fw_v[1, k:k + 1, lanes] for k in range(3)]

            def chunk(ci, c, lanes=lanes, wg=wg, wu=wu):
                o = pl.multiple_of(ci * rc, rc)
                win = pl.ds(o, rc + SUBLANES)
                gate = _conv_taps(ext[0, win, lanes], wg, rc)
                up = _conv_taps(ext[1, win, lanes], wu, rc)
                gel, dgel = _gelu_parts(gate)
                rows = pl.ds(o, rc)
                o_ref[rows, lanes] = (gel * up).astype(BF16)
                fg_ref[rows, lanes] = (up * dgel).astype(BF16)
                fu_ref[rows, lanes] = gel.astype(BF16)
                return c

            lax.fori_loop(0, tb // rc, chunk, 0)

    spec = pl.BlockSpec((tb, tn), lambda i, j: (i, j))
    shape = jax.ShapeDtypeStruct((t_len, D_FF), BF16)
    return _pallas(
        body, name=name, grid=(t_len // tb, D_FF // tn),
        in_specs=[pl.BlockSpec((2, tb, tn), lambda i, j: (0, i, j)),
                  pl.BlockSpec((2, PACKED, tn), lambda i, j: (0, jnp.maximum(i * hb - 1, 0), j)),
                  pl.BlockSpec((None, 2, SUBLANES, tn), lambda i, j: (layer, 0, 0, j))],
        out_specs=[spec] * 3, out_shape=[shape] * 3,
        scratch_shapes=[pltpu.VMEM((2, tb + SUBLANES, tn), F32)],
        args=(u, u, fw), sem=("parallel", "parallel"), comm=comm)


def _ffn_bwd(dact, fg, fu, u, fw, layer, name, tb=512, tn=1024, rc=32, comm=None):
    t_len = u.shape[1]
    ni = t_len // tb
    hb = tb // PACKED
    last_halo = t_len // PACKED - 1

    def body(d_ref, dn_ref, fg_ref, fgn_ref, fu_ref, fun_ref, u_ref, up_ref, fw_ref, du_ref, dfw_ref,
             extu, extp, acc):
        i = pl.program_id(1)

        @pl.when(i == 0)
        def _():
            acc[...] = jnp.zeros_like(acc)

        keep_prev = jnp.where(i == 0, 0.0, 1.0)
        keep_next = jnp.where(i == ni - 1, 0.0, 1.0)
        extu[:, 0:SUBLANES, :] = up_ref[...].astype(F32)[:, PACKED - SUBLANES:, :] * keep_prev
        extu[:, SUBLANES:, :] = u_ref[...].astype(F32)
        dv = d_ref[...].astype(F32)
        dn = dn_ref[...].astype(F32)[0:SUBLANES] * keep_next
        extp[0, 0:tb, :] = dv * fg_ref[...].astype(F32)
        extp[0, tb:, :] = dn * fgn_ref[...].astype(F32)[0:SUBLANES]
        extp[1, 0:tb, :] = dv * fu_ref[...].astype(F32)
        extp[1, tb:, :] = dn * fun_ref[...].astype(F32)[0:SUBLANES]
        fw_v = fw_ref[...]
        m = rc + SUBLANES

        for lb in range(tn // LANES):
            lanes = slice(lb * LANES, (lb + 1) * LANES)
            taps = [[fw_v[pln, k:k + 1, lanes] for k in range(3)] for pln in range(2)]

            def chunk(ci, c, lanes=lanes, taps=taps):
                o = pl.multiple_of(ci * rc, rc)
                for pln in range(2):
                    e = extu[pln, pl.ds(o, m), lanes]
                    sh = [pltpu.roll(e, 2, axis=0)[SUBLANES:], pltpu.roll(e, 1, axis=0)[SUBLANES:], e[SUBLANES:]]
                    dpost = extp[pln, pl.ds(o, m), lanes]
                    du_ref[pln, pl.ds(o, rc), lanes] = _conv_taps_t(dpost, taps[pln], rc).astype(BF16)
                    for k in range(3):
                        prod = dpost[0:rc] * sh[k]
                        acc[3 * pln + k, :, lanes] += sum(
                            prod[s:s + SUBLANES] for s in range(0, rc, SUBLANES))
                return c

            lax.fori_loop(0, tb // rc, chunk, 0)

        @pl.when(i == ni - 1)
        def _():
            dfw_ref[...] = jnp.zeros_like(dfw_ref)
            for pln in range(2):
                for k in range(3):
                    dfw_ref[pln, k:k + 1, :] = jnp.sum(acc[3 * pln + k], axis=0, keepdims=True)

    main = pl.BlockSpec((tb, tn), lambda j, i: (i, j))
    nxt = pl.BlockSpec((PACKED, tn), lambda j, i: (jnp.minimum((i + 1) * hb, last_halo), j))
    return _pallas(
        body, name=name, grid=(D_FF // tn, ni),
        in_specs=[main, nxt, main, nxt, main, nxt,
                  pl.BlockSpec((2, tb, tn), lambda j, i: (0, i, j)),
                  pl.BlockSpec((2, PACKED, tn), lambda j, i: (0, jnp.maximum(i * hb - 1, 0), j)),
                  pl.BlockSpec((None, 2, SUBLANES, tn), lambda j, i: (layer, 0, 0, j))],
        out_specs=[pl.BlockSpec((2, tb, tn), lambda j, i: (0, i, j)),
                   pl.BlockSpec((2, SUBLANES, tn), lambda j, i: (0, 0, j))],
        out_shape=[jax.ShapeDtypeStruct((2, t_len, D_FF), BF16),
                   jax.ShapeDtypeStruct((2, SUBLANES, D_FF), F32)],
        scratch_shapes=[pltpu.VMEM((2, tb + SUBLANES, tn), F32),
                        pltpu.VMEM((2, tb + SUBLANES, tn), F32),
                        pltpu.VMEM((6, SUBLANES, tn), F32)],
        args=(dact, dact, fg, fg, fu, fu, u, u, fw), sem=("parallel", "arbitrary"), comm=comm)


def _loss_head(x, g, target, name, tb=256):
    t_len, d = x.shape

    def body(x_ref, g_ref, t_ref, dx_ref, dxb_ref, dg_ref, loss_ref):
        @pl.when(pl.program_id(0) == 0)
        def _():
            dg_ref[...] = jnp.zeros_like(dg_ref)
            loss_ref[...] = jnp.zeros_like(loss_ref)

        xv = x_ref[...]
        gv = g_ref[...]
        r = lax.rsqrt(jnp.mean(xv * xv, axis=-1, keepdims=True) + EPS)
        xh = xv * r
        err = xh * gv - t_ref[...]
        loss_ref[...] += (0.5 / d) * jnp.sum(jnp.sum(err * err, axis=-1, keepdims=True), axis=0, keepdims=True)
        dy = err * (1.0 / d)
        dyg = dy * gv
        dx = r * (dyg - xh * jnp.mean(dyg * xh, axis=-1, keepdims=True))
        dx_ref[...] = dx
        dxb_ref[...] = dx.astype(BF16)
        dg_ref[0:1, :] += jnp.sum(dy * xh, axis=0, keepdims=True)

    return _pallas(
        body, name=name, grid=(t_len // tb,),
        in_specs=[pl.BlockSpec((tb, d), lambda i: (i, 0)), pl.BlockSpec((1, d), lambda i: (0, 0)),
                  pl.BlockSpec((tb, d), lambda i: (i, 0))],
        out_specs=[pl.BlockSpec((tb, d), lambda i: (i, 0)), pl.BlockSpec((tb, d), lambda i: (i, 0)),
                   pl.BlockSpec((SUBLANES, d), lambda i: (0, 0)),
                   pl.BlockSpec((SUBLANES, LANES), lambda i: (0, 0))],
        out_shape=[jax.ShapeDtypeStruct((t_len, d), F32), jax.ShapeDtypeStruct((t_len, d), BF16),
                   jax.ShapeDtypeStruct((SUBLANES, d), F32), jax.ShapeDtypeStruct((SUBLANES, LANES), F32)],
        args=(x, g, target), sem=("arbitrary",))[0]


def _adamw(w, g, m, v, name, emit_grad=False, comm=None):
    r, c = w.shape
    tr = 256 if r % 256 == 0 else r
    c1 = 1.0 / (1.0 - ADAM_B1 ** ADAM_STEP)
    c2 = 1.0 / (1.0 - ADAM_B2 ** ADAM_STEP)

    def body(w_ref, g_ref, m_ref, v_ref, d_ref, mo_ref, vo_ref, *go_ref):
        gv = g_ref[...]
        mn = ADAM_B1 * m_ref[...] + (1.0 - ADAM_B1) * gv
        vn = ADAM_B2 * v_ref[...] + (1.0 - ADAM_B2) * (gv * gv)
        d_ref[...] = -ADAM_LR * ((mn * c1) / (jnp.sqrt(vn * c2) + ADAM_EPS) + ADAM_WD * w_ref[...])
        mo_ref[...] = mn
        vo_ref[...] = vn
        if emit_grad:
            go_ref[0][...] = gv

    spec = pl.BlockSpec((tr, c), lambda i: (i, 0))
    shape = jax.ShapeDtypeStruct((r, c), F32)
    n_out = 4 if emit_grad else 3
    return _pallas(
        body, name=name, grid=(r // tr,),
        in_specs=[spec] * 4, out_specs=[spec] * n_out, out_shape=[shape] * n_out,
        args=(w, g, m, v), sem=("parallel",), comm=comm)


def _place():
    x, y, c = lax.axis_index("x"), lax.axis_index("y"), lax.axis_index("c")
    chips = [(1 - x, y), (x, 1 - y), (1 - x, 1 - y)]
    return x, y, c, chips


def _remote(src, dst, send, recv, sem, to):
    return pltpu.make_async_remote_copy(
        src_ref=src, dst_ref=dst, send_sem=send.at[sem], recv_sem=recv.at[sem], device_id=to, device_id_type=MESH)


def _gather_plan(fulls, kinds, mid_at=None, parts=None):
    parts = parts or [(0, 1)] * len(fulls)

    def region(it, f, k, cc):
        kind = kinds[it]
        p, n = parts[it][0:2]
        count = parts[it][2] if len(parts[it]) > 2 else 1
        if kind == SMALL:
            return f.at[k, pl.ds(cc * (CONV_PACK_ROWS // 2), CONV_PACK_ROWS // 2), :]
        if COL_SHARDED[kind]:
            rows, cols = f.shape[0] // (2 * n), f.shape[1] // N_CHIP
            return f.at[pl.ds((cc * n + p) * rows, count * rows), pl.ds(k * cols, cols)]
        assert n == 1
        rows = f.shape[0] // N_CHIP
        return f.at[pl.ds(k * rows + cc * (rows // 2), rows // 2), :]

    def first_hop(bufs, send, recv, it, j):
        x, y, c, chips = _place()
        reg = region(it, bufs[it], 2 * x + y, c)
        return _remote(reg, reg, send, recv, it * 6 + j, (*chips[j], c))

    def arrival(bufs, send, recv, it, j, second):
        x, y, c, chips = _place()
        px, py = chips[j]
        reg = region(it, bufs[it], 2 * px + py, 1 - c if second else c)
        to = (x, y, 1 - c) if second else (px, py, c)
        return _remote(reg, reg, send, recv, it * 6 + (3 + j if second else j), to)

    def forward(bufs, send, recv, it, j):
        x, y, c, chips = _place()
        px, py = chips[j]
        reg = region(it, bufs[it], 2 * px + py, c)
        return _remote(reg, reg, send, recv, it * 6 + 3 + j, (x, y, 1 - c))

    def start(srcs, bufs, outs, send, recv):
        for it in range(len(bufs)):
            for j in range(3):
                first_hop(bufs, send, recv, it, j).start()

    def mid(srcs, bufs, outs, send, recv):
        for it in range(len(bufs)):
            for j in range(3):
                arrival(bufs, send, recv, it, j, False).wait_recv()
                forward(bufs, send, recv, it, j).start()

    def finish(srcs, bufs, outs, send, recv):
        for it in range(len(bufs)):
            for j in range(3):
                arrival(bufs, send, recv, it, j, True).wait_recv()
        for it in range(len(bufs)):
            for j in range(3):
                first_hop(bufs, send, recv, it, j).wait_send()
                forward(bufs, send, recv, it, j).wait_send()

    return Comm(srcs=(), bufs=tuple(fulls), outs=(), n_sem=6 * len(fulls), start=start, mid=mid, finish=finish,
                mid_at=mid_at)


def _half_axis(kind):
    return 0 if kind == SMALL or COL_SHARDED[kind] else 1


def _half2(ref, kind, cc):
    if _half_axis(kind) == 0:
        return ref.at[pl.ds(cc * (ref.shape[0] // 2), ref.shape[0] // 2), :]
    return ref.at[:, pl.ds(cc * (ref.shape[1] // 2), ref.shape[1] // 2)]


def _pair_plan(grads, kinds):
    def land_shape(g, kind):
        s = list(g.shape)
        s[_half_axis(kind)] //= 2
        return jax.ShapeDtypeStruct(tuple(s), F32)

    def copy(srcs, outs, send, recv, it):
        x, y, c, _ = _place()
        return _remote(_half2(srcs[it], kinds[it], 1 - c), outs[it], send, recv, it, (x, y, 1 - c))

    def start(srcs, bufs, outs, send, recv):
        for it in range(len(srcs)):
            copy(srcs, outs, send, recv, it).start()

    def finish(srcs, bufs, outs, send, recv):
        for it in range(len(srcs)):
            copy(srcs, outs, send, recv, it).wait_send()
        for it in range(len(srcs)):
            copy(srcs, outs, send, recv, it).wait_recv()

    return Comm(srcs=tuple(grads), bufs=(), outs=tuple(land_shape(g, k) for g, k in zip(grads, kinds)),
                n_sem=len(grads), start=start, finish=finish)


def _scatter_plan(parts, slots, kinds):
    def piece(s, kind, k):
        if kind == SMALL:
            return s
        if COL_SHARDED[kind]:
            n = s.shape[1] // N_CHIP
            return s.at[:, pl.ds(k * n, n)]
        n = s.shape[0] // N_CHIP
        return s.at[pl.ds(k * n, n), :]

    def outbound(srcs, bufs, send, recv, it, j):
        x, y, c, chips = _place()
        px, py = chips[j]
        return _remote(piece(srcs[it], kinds[it], 2 * px + py), bufs[it].at[2 * x + y], send, recv, it * 3 + j,
                       (px, py, c))

    def inbound(bufs, send, recv, it, j):
        x, y, c, chips = _place()
        px, py = chips[j]
        got = bufs[it].at[2 * px + py]
        return _remote(got, got, send, recv, it * 3 + j, (px, py, c))

    def start(srcs, bufs, outs, send, recv):
        for it in range(len(srcs)):
            for j in range(3):
                outbound(srcs, bufs, send, recv, it, j).start()

    def finish(srcs, bufs, outs, send, recv):
        for it in range(len(srcs)):
            for j in range(3):
                inbound(bufs, send, recv, it, j).wait_recv()
        for it in range(len(srcs)):
            for j in range(3):
                outbound(srcs, bufs, send, recv, it, j).wait_send()

    return Comm(srcs=tuple(parts), bufs=tuple(slots), outs=(), n_sem=3 * len(parts), start=start, finish=finish)


def _share_plan(fulls, kinds, layer):
    def half(f, kind, cc):
        return _half2(f if kind == SMALL else f.at[layer], kind, cc)

    def copy(bufs, send, recv, it, cc):
        x, y, c, _ = _place()
        reg = half(bufs[it], kinds[it], c if cc == "mine" else 1 - c)
        return _remote(reg, reg, send, recv, it, (x, y, 1 - c))

    def start(srcs, bufs, outs, send, recv):
        for it in range(len(bufs)):
            copy(bufs, send, recv, it, "mine").start()

    def finish(srcs, bufs, outs, send, recv):
        for it in range(len(bufs)):
            copy(bufs, send, recv, it, "other").wait_recv()
        for it in range(len(bufs)):
            copy(bufs, send, recv, it, "mine").wait_send()

    return Comm(srcs=(), bufs=tuple(fulls), outs=(), n_sem=len(fulls), start=start, finish=finish)


def _pair_sum(g, land, idx, kind, name):
    odt = F32 if kind == SMALL else BF16
    r, cdim = land.shape

    def body(idx_ref, g_ref, l_ref, p_ref, s_ref):
        v = (g_ref[...] + l_ref[...]).astype(odt)
        p_ref[...] = v
        if kind == SMALL:
            s_ref[...] = v
        else:
            @pl.when(pl.program_id(1 if COL_SHARDED[kind] else 0) == idx_ref[1])
            def _():
                s_ref[...] = v

    if kind == SMALL:
        grid = (1,)
        g_spec = pl.BlockSpec((r, LANES), lambda i, idx_ref: (idx_ref[0], 0))
        spec = pl.BlockSpec((r, LANES), lambda i, idx_ref: (0, 0))
        s_spec = pl.BlockSpec((None, r, LANES), lambda i, idx_ref: (idx_ref[1], 0, 0))
        s_shape = (N_CHIP, r, LANES)
    elif COL_SHARDED[kind]:
        pc, tr = cdim // N_CHIP, 256
        nrb = r // tr
        grid = (nrb, N_CHIP)
        g_spec = pl.BlockSpec((tr, pc), lambda i, k, idx_ref: (idx_ref[0] * nrb + i, k))
        spec = pl.BlockSpec((tr, pc), lambda i, k, idx_ref: (i, k))
        s_spec = pl.BlockSpec((None, tr, pc), lambda i, k, idx_ref: (idx_ref[1], i, 0))
        s_shape = (N_CHIP, r, pc)
    else:
        pr = r // N_CHIP
        grid = (N_CHIP,)
        g_spec = pl.BlockSpec((pr, cdim), lambda k, idx_ref: (k, idx_ref[0]))
        spec = pl.BlockSpec((pr, cdim), lambda k, idx_ref: (k, 0))
        s_spec = pl.BlockSpec((None, pr, cdim), lambda k, idx_ref: (idx_ref[1], 0, 0))
        s_shape = (N_CHIP, pr, cdim)
    return pl.pallas_call(
        body, name=name,
        grid_spec=pltpu.PrefetchScalarGridSpec(
            num_scalar_prefetch=1, grid=grid, in_specs=[g_spec, spec], out_specs=[spec, s_spec]),
        out_shape=[jax.ShapeDtypeStruct(land.shape, odt), jax.ShapeDtypeStruct(s_shape, odt)],
        compiler_params=_cp(*(["arbitrary"] * len(grid))),
    )(idx, g, land)


def _sum_slots(slots, idx, kind, layer, prev, name):
    _, r, cdim = slots.shape

    def body(*refs):
        s_ref, o_ref = refs[1], refs[-1]
        v = s_ref[...].astype(F32)
        o_ref[...] = (v[0] + v[1]) + (v[2] + v[3])

    if kind == SMALL:
        grid = (1,)
        s_spec = pl.BlockSpec((N_CHIP, r, cdim), lambda i, idx_ref: (0, 0, 0))
        o_spec = pl.BlockSpec((r, cdim), lambda i, idx_ref: (idx_ref[0], 0))
        full = (2 * r, cdim)
    else:
        tr = 256 if r % 256 == 0 else 384
        nrb = r // tr
        grid = (nrb,)
        s_spec = pl.BlockSpec((N_CHIP, tr, cdim), lambda i, idx_ref: (0, i, 0))
        if COL_SHARDED[kind]:
            o_spec = pl.BlockSpec((None, tr, cdim), lambda i, idx_ref: (layer, idx_ref[0] * nrb + i, 0))
            full = (2, 2 * r, cdim)
        else:
            o_spec = pl.BlockSpec((None, tr, cdim), lambda i, idx_ref: (layer, i, idx_ref[0]))
            full = (2, r, 2 * cdim)
    in_specs, args, aliases = [s_spec], [idx, slots], {}
    if prev is not None:
        in_specs.append(ANY)
        args.append(prev)
        aliases = {2: 0}
    return pl.pallas_call(
        body, name=name,
        grid_spec=pltpu.PrefetchScalarGridSpec(
            num_scalar_prefetch=1, grid=grid, in_specs=in_specs, out_specs=o_spec),
        out_shape=jax.ShapeDtypeStruct(full, F32),
        input_output_aliases=aliases,
        compiler_params=_cp(*(["parallel"] * len(grid))),
    )(*args)


def _block_diag(w):
    w4 = w.reshape(2, 4, 4, 64, 64)
    eye = jnp.eye(4, dtype=w.dtype)[None, None, :, None, :, None]
    return (w4[:, :, :, :, None, :] * eye).reshape(2, 4, 256, 256)


def _block_diag_extract(d):
    d5 = d.reshape(4, 4, 64, 4, 64)
    return jnp.stack([d5[:, hh, :, hh, :] for hh in range(4)], axis=1).reshape(-1)


REP_NAMES = ("norm1_g", "lru_conv_b", "lru_ba", "lru_bx", "lru_lambda", "norm2_g", "lru_wa", "lru_wx")


def _pack_rep(norm1_g, conv_b, ba, bx, lam, norm2_g, wa, wx, final_g):
    parts = [a.reshape(-1) for a in (norm1_g, conv_b, ba, bx, lam, norm2_g, wa, wx, final_g)]
    return jnp.concatenate(parts).reshape(REP_ROWS, LANES)


def _unpack_rep(buf):
    flat = buf.reshape(-1)
    res, o = {}, 0
    for k in REP_NAMES:
        shape = (2, 16, 64, 64) if k in ("lru_wa", "lru_wx") else (2, 1024)
        n = math.prod(shape)
        res[k] = flat[o:o + n].reshape(shape)
        o += n
    res["final_g"] = flat[o:o + 1024]
    return res


def _pack_conv_shard(lru_cw, sc_cw, ffn_cw):
    return jnp.concatenate([lru_cw.reshape(16, LANES), jnp.pad(sc_cw.reshape(6, LANES), ((0, 2), (0, 0))),
                            ffn_cw.reshape(72, LANES)], axis=0)


def _unpack_conv_shard(buf):
    return (buf[0:16].reshape(2, 4, 256), buf[16:22].reshape(2, 3, 128), buf[24:96].reshape(2, 3, 1536))


def kernel(x, norm1_g, w_in, lru_conv_w, lru_conv_b, lru_wa, lru_ba, lru_wx, lru_bx, lru_lambda, sc_conv_w, w_out, norm2_g, w_up, ffn_conv_w, w_down, final_g, loss_target, m_norm1_g, m_w_in, m_lru_conv_w, m_lru_conv_b, m_lru_wa, m_lru_ba, m_lru_wx, m_lru_bx, m_lru_lambda, m_sc_conv_w, m_w_out, m_norm2_g, m_w_up, m_ffn_conv_w, m_w_down, m_final_g, v_norm1_g, v_w_in, v_lru_conv_w, v_lru_conv_b, v_lru_wa, v_lru_ba, v_lru_wx, v_lru_bx, v_lru_lambda, v_sc_conv_w, v_w_out, v_norm2_g, v_w_up, v_ffn_conv_w, v_w_down, v_final_g):
    me = 2 * lax.axis_index("x") + lax.axis_index("y")
    idx = jnp.stack([lax.axis_index("c"), me]).astype(jnp.int32)
    t_len = x.shape[1]

    s_conv = _pack_conv_shard(lru_conv_w, sc_conv_w, ffn_conv_w)
    conv_slots = lax.dynamic_update_slice(jnp.zeros((N_CHIP, CONV_PACK_ROWS, LANES), F32), s_conv[None], (me, 0, 0))
    wi = list(_cast_into_full(w_in, W_IN, idx, "cast_w_in"))
    wo = list(_cast_into_full(w_out, W_OUT, idx, "cast_w_out"))
    wu = list(_cast_into_full(w_up, W_UP, idx, "cast_w_up"))
    wd = list(_cast_into_full(w_down, W_DOWN, idx, "cast_w_down"))
    wi[0], convs = _comm_call(_gather_plan([wi[0], conv_slots], [W_IN, SMALL]), "ag_first")
    per_chip = [_unpack_conv_shard(convs[k]) for k in range(N_CHIP)]
    lru_cw = jnp.concatenate([p[0] for p in per_chip], axis=-1)
    sc_cw = jnp.concatenate([p[1] for p in per_chip], axis=-1)
    ffn_cw = jnp.concatenate([p[2] for p in per_chip], axis=-1)

    prm = jnp.concatenate(
        [lru_cw, jnp.stack([lru_conv_b, lru_ba, lru_bx, lru_lambda], axis=1),
         jnp.pad(sc_cw, ((0, 0), (0, 0), (0, D_LRU - D_SC))), jnp.zeros((2, 5, D_LRU), F32)], axis=1)
    gates = jnp.concatenate([_block_diag(lru_wa), _block_diag(lru_wx)], axis=1).astype(BF16)
    fw8 = jnp.pad(ffn_cw.reshape(2, 3, 2, D_FF).transpose(0, 2, 1, 3), ((0, 0), (0, 0), (0, 5), (0, 0)))

    xs = x[0]
    saved = []
    n512, n256 = t_len // 512, t_len // 256
    whole, lower, upper = (0, 1), (0, 2), (1, 2)
    carried_by = {
        "fwd_in_0": ([(wu, 0, W_UP, (0, 4))], (max(n512 - 3, 0),)),
        "fwd_mixer_0": ([(wu, 0, W_UP, (1, 4, 2)), (wo, 0, W_OUT, whole)], (max(n256 - 3, 0),)),
        "fwd_out_0": ([(wu, 0, W_UP, (3, 4))], (max(n512 - 2, 0),)),
        "fwd_up_0": ([(wd, 0, W_DOWN, whole)], (max(n512 - 2, 0),)),
        "fwd_act_0": ([(wi, 1, W_IN, whole), (wo, 1, W_OUT, whole)], (max(n512 - 2, 0), 0)),
        "fwd_down_0": ([(wu, 1, W_UP, (0, 4))], (max(n512 - 3, 0),)),
        "fwd_in_1": ([(wu, 1, W_UP, (1, 4))], (max(n512 - 3, 0),)),
        "fwd_mixer_1": ([(wu, 1, W_UP, (2, 4, 2))], (max(n256 - 4, 0),)),
        "fwd_act_1": ([(wd, 1, W_DOWN, whole)], (max(n512 - 3, 0), 0)),
    }

    def carried(name):
        if name not in carried_by:
            return None, lambda got: None
        items, mid_at = carried_by[name]

        def store(got):
            for (lst, i, _, _), arr in zip(items, got):
                lst[i] = arr

        return _gather_plan([lst[i] for lst, i, _, _ in items], [k for _, _, k, _ in items], mid_at=mid_at,
                            parts=[p for _, _, _, p in items]), store

    for l in range(2):
        comm, store = carried(f"fwd_in_{l}")
        (z, h1), got = _norm_mm(xs, norm1_g[l][None], wi[l], f"fwd_in_{l}", comm=comm)
        store(got)
        comm, store = carried(f"fwd_mixer_{l}")
        (ymix, hst), got = _mixer_fwd(z, prm, gates, l, f"fwd_mixer_{l}", comm=comm)
        store(got)
        comm, store = carried(f"fwd_out_{l}")
        (x2,), got = _mm_res(ymix, wo[l], xs, f"fwd_out_{l}", comm=comm)
        store(got)
        comm, store = carried(f"fwd_up_{l}")
        (u, h2), got = _norm_mm(x2, norm2_g[l][None], wu[l], f"fwd_up_{l}", planes=True, comm=comm)
        store(got)
        comm, store = carried(f"fwd_act_{l}")
        (act, fg, fu), got = _ffn_act(u, fw8, l, f"fwd_act_{l}", comm=comm)
        store(got)
        comm, store = carried(f"fwd_down_{l}")
        (x3,), got = _mm_res(act, wd[l], x2, f"fwd_down_{l}", comm=comm)
        store(got)
        saved.append((xs, h1, z, hst, ymix, x2, h2, u, act, fg, fu))
        xs = x3

    dx, dxb, dgf, loss_blk = _loss_head(xs, final_g[None], loss_target[0], "loss_head")

    kinds = [W_IN, W_OUT, W_UP, W_DOWN]
    grads = [None, None]
    small = [None, None]
    reduced = [None] * 4
    summed1 = [None] * 4
    slots1 = [None] * 4

    def scatter1(ws):
        return _scatter_plan([summed1[w][0] for w in ws], [summed1[w][1] for w in ws], ws)

    for l in (1, 0):
        x_in, h1, z, hst, ymix, x2, h2, u, act, fg, fu = saved[l]
        carry = l == 0
        comm = _pair_plan([grads[1][W_IN]], [W_IN]) if carry else None
        (g_down,), got = _mm_tn(act, dxb, f"bwd_wdown_{l}", tk=1536, tn=1024, comm=comm)
        if carry:
            summed1[W_IN] = _pair_sum(grads[1][W_IN], got[0], idx, W_IN, "rs_add1_0")
        (dact,), _ = _mm_nt(dxb, wd[l], f"bwd_dact_{l}")
        comm = scatter1((W_UP, W_IN)) if carry else None
        (du, dfw), got = _ffn_bwd(dact, fg, fu, u, fw8, l, f"bwd_act_{l}", comm=comm)
        if carry:
            slots1[W_UP], slots1[W_IN] = got
            reduced = [_sum_slots(slots1[w], idx, kinds[w], 1, None, f"rs_sum1_{w}") for w in range(4)]
        comm = _share_plan(reduced, kinds, 1) if carry else None
        (g_up,), got = _mm_tn(h2, du, f"bwd_wup_{l}", tk=1024, tn=1536, planes=True, comm=comm)
        if carry:
            reduced = list(got)
        comm = _pair_plan([g_up, g_down], [W_UP, W_DOWN]) if carry else None
        (dx2, dx2b, dg2), got = _mm_nt_normbwd(du, wu[l], x2, norm2_g[l][None], dx, f"bwd_up_{l}", planes=True,
                                               comm=comm)
        if carry:
            sum_up = _pair_sum(g_up, got[0], idx, W_UP, "rs_add0_2")
            sum_down = _pair_sum(g_down, got[1], idx, W_DOWN, "rs_add0_3")
        (g_out,), _ = _mm_tn(ymix, dx2b, f"bwd_wout_{l}", tk=1536, tn=1024)
        comm = _pair_plan([g_out], [W_OUT]) if carry else None
        (dymix,), got = _mm_nt(dx2b, wo[l], f"bwd_dymix_{l}", comm=comm)
        trio = (W_OUT, W_UP, W_DOWN)
        if carry:
            sum_out = _pair_sum(g_out, got[0], idx, W_OUT, "rs_add0_1")
            comm = _scatter_plan([sum_out[0], sum_up[0], sum_down[0]], [sum_out[1], sum_up[1], sum_down[1]], trio)
        else:
            comm = _pair_plan([g_out, g_up, g_down], trio)
        (dz, dprm, dgates), got = _mixer_bwd(z, hst, dymix, prm, gates, l, f"bwd_mixer_{l}", comm=comm)
        if carry:
            for w, s in zip(trio, got):
                reduced[w] = _sum_slots(s, idx, w, 0, reduced[w], f"rs_sum0_{w}")
        else:
            for w, g, land in zip(trio, (g_out, g_up, g_down), got):
                summed1[w] = _pair_sum(g, land, idx, w, f"rs_add1_{w}")
        comm = _share_plan([reduced[w] for w in trio], trio, 0) if carry else scatter1((W_DOWN,))
        (g_in,), got = _mm_tn(h1, dz, f"bwd_win_{l}", tk=1024, tn=1792, comm=comm)
        if carry:
            for w, full in zip(trio, got):
                reduced[w] = full
        else:
            slots1[W_DOWN], = got
        if carry:
            land_in, = _comm_call(_pair_plan([g_in], [W_IN]), "rs_pair_in")
            sum_in = _pair_sum(g_in, land_in, idx, W_IN, "rs_add0_0")
            comm = _scatter_plan([sum_in[0]], [sum_in[1]], [W_IN])
        else:
            comm = scatter1((W_OUT,))
        (dx, dxb, dg1), got = _mm_nt_normbwd(dz, wi[l], x_in, norm1_g[l][None], dx2, f"bwd_in_{l}", comm=comm)
        if carry:
            slot_in, = got
        else:
            slots1[W_OUT], = got
        grads[l] = [g_in, g_out, g_up, g_down]
        rep = dict(zip(REP_NAMES, [dg1[0], dprm[4], dprm[5], dprm[6], dprm[7], dg2[0],
                                   _block_diag_extract(dgates[0:4]), _block_diag_extract(dgates[4:8])]))
        conv = [dprm[0:4].reshape(-1), jnp.pad(dprm[8:11, 0:D_SC].reshape(-1), (0, 512)),
                dfw[:, 0:3, :].transpose(1, 0, 2).reshape(-1)]
        small[l] = (rep, conv)
    grad_x = dx[None]
    g_small = jnp.concatenate(
        [small[l][0][k] for k in REP_NAMES for l in range(2)] + [dgf[0]] + small[0][1] + small[1][1]
        + [loss_blk.reshape(-1)]).reshape(SMALL_ROWS, LANES)

    def big(w, g, m, v, name):
        shape = w.shape
        two_d = lambda a: a.reshape(-1, shape[-1])
        outs, _ = _adamw(two_d(w), two_d(g), two_d(m), two_d(v), name, emit_grad=True)
        return [o.reshape(shape) for o in outs]

    land_small, = _comm_call(_pair_plan([g_small], [SMALL]), "rs_pair_small")
    sum_small = _pair_sum(g_small, land_small, idx, SMALL, "rs_add0_4")
    slot_small, = _comm_call(_scatter_plan([sum_small[0]], [sum_small[1]], [SMALL]), "rs_scatter_small")
    gw_in, gs = _comm_call(
        _share_plan([_sum_slots(slot_in, idx, W_IN, 0, reduced[W_IN], "rs_sum0_0"),
                     _sum_slots(slot_small, idx, SMALL, 0, None, "rs_sum0_4")], [W_IN, SMALL], 0), "rs_share0")
    upd = {"w_up": big(w_up, reduced[W_UP], m_w_up, v_w_up, "adamw_w_up"),
           "w_down": big(w_down, reduced[W_DOWN], m_w_down, v_w_down, "adamw_w_down"),
           "w_out": big(w_out, reduced[W_OUT], m_w_out, v_w_out, "adamw_w_out"),
           "w_in": big(w_in, gw_in, m_w_in, v_w_in, "adamw_w_in")}

    loss = gs[REP_ROWS + CONV_ROWS, 0]
    g_rep = gs[0:REP_ROWS]
    g_conv = gs[REP_ROWS:REP_ROWS + CONV_ROWS].reshape(2, CONV_LAYER)
    g_lru_cw = lax.dynamic_slice_in_dim(g_conv[:, 0:4096].reshape(2, 4, 1024), me * 256, 256, axis=2)
    g_sc_cw = lax.dynamic_slice_in_dim(g_conv[:, 4096:4096 + 1536].reshape(2, 3, 512), me * 128, 128, axis=2)
    g_ffn_cw = lax.dynamic_slice_in_dim(g_conv[:, 6144:].reshape(2, 3, 6144), me * 1536, 1536, axis=2)

    rep_out, _ = _adamw(
        _pack_rep(norm1_g, lru_conv_b, lru_ba, lru_bx, lru_lambda, norm2_g, lru_wa, lru_wx, final_g), g_rep,
        _pack_rep(m_norm1_g, m_lru_conv_b, m_lru_ba, m_lru_bx, m_lru_lambda, m_norm2_g, m_lru_wa, m_lru_wx, m_final_g),
        _pack_rep(v_norm1_g, v_lru_conv_b, v_lru_ba, v_lru_bx, v_lru_lambda, v_norm2_g, v_lru_wa, v_lru_wx, v_final_g),
        "adamw_rep")
    conv_out, _ = _adamw(s_conv, _pack_conv_shard(g_lru_cw, g_sc_cw, g_ffn_cw),
                         _pack_conv_shard(m_lru_conv_w, m_sc_conv_w, m_ffn_conv_w),
                         _pack_conv_shard(v_lru_conv_w, v_sc_conv_w, v_ffn_conv_w), "adamw_conv")

    names = ["norm1_g", "w_in", "lru_conv_w", "lru_conv_b", "lru_wa", "lru_ba", "lru_wx", "lru_bx", "lru_lambda",
             "sc_conv_w", "w_out", "norm2_g", "w_up", "ffn_conv_w", "w_down", "final_g"]
    groups = []
    g_all = dict(_unpack_rep(g_rep))
    g_all.update({k: v[3] for k, v in upd.items()})
    g_all.update(lru_conv_w=g_lru_cw, sc_conv_w=g_sc_cw, ffn_conv_w=g_ffn_cw)
    groups.append(g_all)
    for i in range(3):
        d = dict(_unpack_rep(rep_out[i]))
        cl, cs, cf = _unpack_conv_shard(conv_out[i])
        d.update(lru_conv_w=cl, sc_conv_w=cs, ffn_conv_w=cf)
        d.update({k: v[i] for k, v in upd.items()})
        groups.append(d)
    return (loss, grad_x, *[grp[n] for grp in groups for n in names])
```

```python
import dataclasses
import functools
import math
import operator
from typing import Any, Callable, Optional, Sequence

import jax
import jax.numpy as jnp
from jax import lax
from jax.experimental import pallas as pl
from jax.experimental.pallas import tpu as pltpu

F32 = jnp.float32
BF16 = jnp.bfloat16
MESH = pl.DeviceIdType.MESH

D_MODEL = 1024
D_LRU = 1024
D_SC = 512
D_MIX = D_LRU + D_SC
D_IN = 2 * D_LRU + 3 * D_SC
D_FF = 3072
N_CHIP = 4
RG_C = 8.0
EPS = 1e-6
ADAM_LR = 0.001
ADAM_B1 = 0.9
ADAM_B2 = 0.999
ADAM_EPS = 1e-08
ADAM_WD = 0.01
ADAM_STEP = 10

SUBLANES = 8
PACKED = 16
LANES = 128
VMEM_LIMIT = 56 * 1024 * 1024
GELU_C0 = math.sqrt(2.0 / math.pi)
GELU_C1 = 0.044715

REP_LAYER = 6 * 1024 + 2 * 16 * 64 * 64
REP_ROWS = (2 * REP_LAYER + 1024) // LANES
CONV_LAYER = 4 * 1024 + 2048 + 3 * 6144
CONV_ROWS = 2 * CONV_LAYER // LANES
SMALL_ROWS = REP_ROWS + CONV_ROWS + 8
CONV_PACK_ROWS = 96

W_IN, W_OUT, W_UP, W_DOWN, SMALL = range(5)
COL_SHARDED = {W_IN: True, W_OUT: False, W_UP: True, W_DOWN: False}

ONCE = pl.Buffered(1)
ANY = pl.BlockSpec(memory_space=pl.ANY)


def _cp(*sem):
    return pltpu.CompilerParams(dimension_semantics=sem, vmem_limit_bytes=VMEM_LIMIT)


@dataclasses.dataclass
class Comm:
    srcs: Sequence[Any]
    bufs: Sequence[Any]
    outs: Sequence[Any]
    n_sem: int
    start: Callable
    finish: Callable
    mid: Optional[Callable] = None
    mid_at: Optional[Sequence[int]] = None


def _pallas(body, *, name, grid, in_specs, out_specs, out_shape, args, sem, scratch_shapes=(), comm=None):
    if comm is None:
        res = pl.pallas_call(
            body, name=name, grid=grid, in_specs=list(in_specs), out_specs=list(out_specs),
            out_shape=list(out_shape), scratch_shapes=list(scratch_shapes), compiler_params=_cp(*sem))(*args)
        return tuple(res), ()
    n_in, n_out, n_scr = len(in_specs), len(out_specs), len(scratch_shapes)
    ns, nb, no = len(comm.srcs), len(comm.bufs), len(comm.outs)

    def carrier(*refs):
        p = 0
        main_in = refs[p:p + n_in]
        p += n_in
        c_src = refs[p:p + ns]
        p += ns + nb
        main_out = refs[p:p + n_out]
        p += n_out
        c_buf = refs[p:p + nb]
        p += nb
        c_out = refs[p:p + no]
        p += no
        scr = refs[p:p + n_scr]
        send, recv = refs[p + n_scr], refs[p + n_scr + 1]
        ids = [pl.program_id(a) for a in range(len(grid))]

        def at(steps):
            return functools.reduce(operator.and_, [i == s for i, s in zip(ids, steps)])

        @pl.when(at([0] * len(grid)))
        def _():
            comm.start(c_src, c_buf, c_out, send, recv)

        if comm.mid is not None:
            @pl.when(at(comm.mid_at))
            def _():
                comm.mid(c_src, c_buf, c_out, send, recv)

        body(*main_in, *main_out, *scr)

        @pl.when(at([g - 1 for g in grid]))
        def _():
            comm.finish(c_src, c_buf, c_out, send, recv)

    res = pl.pallas_call(
        carrier, name=name, grid=grid,
        in_specs=list(in_specs) + [ANY] * (ns + nb),
        out_specs=list(out_specs) + [ANY] * (nb + no),
        out_shape=list(out_shape) + [jax.ShapeDtypeStruct(b.shape, b.dtype) for b in comm.bufs] + list(comm.outs),
        input_output_aliases={n_in + ns + j: n_out + j for j in range(nb)},
        scratch_shapes=list(scratch_shapes) + [pltpu.SemaphoreType.DMA((comm.n_sem,)),
                                               pltpu.SemaphoreType.DMA((comm.n_sem,))],
        compiler_params=_cp(*(["arbitrary"] * len(grid))),
    )(*args, *comm.srcs, *comm.bufs)
    return tuple(res[:n_out]), tuple(res[n_out:])


def _comm_call(comm, name):
    ns, nb, no = len(comm.srcs), len(comm.bufs), len(comm.outs)

    def body(*refs):
        c_src = refs[0:ns]
        c_buf = refs[ns + nb:ns + 2 * nb]
        c_out = refs[ns + 2 * nb:ns + 2 * nb + no]
        send, recv = refs[ns + 2 * nb + no], refs[ns + 2 * nb + no + 1]
        comm.start(c_src, c_buf, c_out, send, recv)
        if comm.mid is not None:
            comm.mid(c_src, c_buf, c_out, send, recv)
        comm.finish(c_src, c_buf, c_out, send, recv)

    return tuple(pl.pallas_call(
        body, name=name,
        in_specs=[ANY] * (ns + nb), out_specs=[ANY] * (nb + no),
        out_shape=[jax.ShapeDtypeStruct(b.shape, b.dtype) for b in comm.bufs] + list(comm.outs),
        input_output_aliases={ns + j: j for j in range(nb)},
        scratch_shapes=[pltpu.SemaphoreType.DMA((comm.n_sem,)), pltpu.SemaphoreType.DMA((comm.n_sem,))],
    )(*comm.srcs, *comm.bufs))


def _sigmoid(v):
    return 1.0 / (1.0 + jnp.exp(-v))


def _sigmoid_tanh(v):
    return 0.5 + 0.5 * jnp.tanh(0.5 * v)


def _gelu_parts(v):
    v2 = v * v
    t = jnp.tanh(v * (GELU_C0 + (GELU_C0 * GELU_C1) * v2))
    half = 0.5 + 0.5 * t
    gel = v * half
    dgel = half + (0.5 * v) * (1.0 - t * t) * (GELU_C0 + (3.0 * GELU_C0 * GELU_C1) * v2)
    return gel, dgel


def _gelu(v):
    t = jnp.tanh(v * (GELU_C0 + (GELU_C0 * GELU_C1) * (v * v)))
    return v * (0.5 + 0.5 * t)


def _neg_expm1(y, a):
    p = jnp.full_like(y, 1.0 / 120.0)
    for coef in (1.0 / 24.0, 1.0 / 6.0, 0.5, 1.0):
        p = p * y + coef
    return jnp.where(y > -0.1, -(p * y), 1.0 - a * a)


def _softplus_neg(lam):
    nl = -lam
    e = jnp.exp(-jnp.abs(nl))
    u = 1.0 + e
    l1p = jnp.where(u == 1.0, e, jnp.log(u) * e / (u - 1.0))
    return jnp.maximum(nl, 0.0) + l1p


def _conv_taps(ext, taps, n_out):
    kw = len(taps)
    acc = taps[kw - 1] * ext[SUBLANES:SUBLANES + n_out]
    for k in range(kw - 1):
        acc = acc + taps[k] * pltpu.roll(ext, kw - 1 - k, axis=0)[SUBLANES:SUBLANES + n_out]
    return acc


def _conv_taps_t(ext, taps, n_out):
    kw = len(taps)
    n = ext.shape[0]
    acc = taps[kw - 1] * ext[0:n_out]
    for k in range(kw - 1):
        acc = acc + taps[k] * pltpu.roll(ext, n - (kw - 1 - k), axis=0)[0:n_out]
    return acc


def _scan8(a, b, carry, row):
    for s in (1, 2, 4):
        m = row >= s
        a_sh = jnp.where(m, pltpu.roll(a, s, axis=0), 1.0)
        b_sh = jnp.where(m, pltpu.roll(b, s, axis=0), 0.0)
        b = a * b_sh + b
        a = a * a_sh
    return a * carry + b


def _scan8_rev(a, b, carry, row):
    for s in (1, 2, 4):
        m = row < SUBLANES - s
        a_sh = jnp.where(m, pltpu.roll(a, SUBLANES - s, axis=0), 1.0)
        b_sh = jnp.where(m, pltpu.roll(b, SUBLANES - s, axis=0), 0.0)
        b = a * b_sh + b
        a = a * a_sh
    return a * carry + b


def _cast_into_full(w, kind, idx, name):
    nl, r, c = w.shape
    tr = 256 if r % 256 == 0 else r
    nrb = r // tr

    def body(idx_ref, w_ref, o0_ref, o1_ref):
        o0_ref[...] = w_ref[0].astype(BF16)
        o1_ref[...] = w_ref[1].astype(BF16)

    if COL_SHARDED[kind]:
        full = (r, N_CHIP * c)
        o_spec = pl.BlockSpec((tr, c), lambda i, idx_ref: (i, idx_ref[1]))
    else:
        full = (N_CHIP * r, c)
        o_spec = pl.BlockSpec((tr, c), lambda i, idx_ref: (idx_ref[1] * nrb + i, 0))
    return pl.pallas_call(
        body, name=name,
        grid_spec=pltpu.PrefetchScalarGridSpec(
            num_scalar_prefetch=1, grid=(nrb,),
            in_specs=[pl.BlockSpec((nl, tr, c), lambda i, idx_ref: (0, i, 0))], out_specs=[o_spec, o_spec]),
        out_shape=[jax.ShapeDtypeStruct(full, BF16)] * 2,
        compiler_params=_cp("parallel"),
    )(idx, w)


def _norm_mm(x, g, w, name, planes=False, tm=512, tn=512, comm=None):
    t_len, d = x.shape
    n = w.shape[1]
    half = n // 2

    def body(x_ref, g_ref, w_ref, z_ref, h_ref):
        xv = x_ref[...]
        r = lax.rsqrt(jnp.mean(xv * xv, axis=-1, keepdims=True) + EPS)
        h_ref[...] = ((xv * r) * g_ref[...]).astype(BF16)
        for n0 in range(0, n, tn):
            blk = jnp.dot(h_ref[...], w_ref[:, n0:n0 + tn], preferred_element_type=F32).astype(BF16)
            if planes:
                z_ref[n0 // half, :, n0 % half:n0 % half + tn] = blk
            else:
                z_ref[:, n0:n0 + tn] = blk

    if planes:
        z_shape = jax.ShapeDtypeStruct((2, t_len, half), BF16)
        z_spec = pl.BlockSpec((2, tm, half), lambda i: (0, i, 0))
    else:
        z_shape = jax.ShapeDtypeStruct((t_len, n), BF16)
        z_spec = pl.BlockSpec((tm, n), lambda i: (i, 0))
    return _pallas(
        body, name=name, grid=(t_len // tm,),
        in_specs=[pl.BlockSpec((tm, d), lambda i: (i, 0)),
                  pl.BlockSpec((1, d), lambda i: (0, 0)),
                  pl.BlockSpec((d, n), lambda i: (0, 0), pipeline_mode=ONCE)],
        out_specs=[z_spec, pl.BlockSpec((tm, d), lambda i: (i, 0))],
        out_shape=[z_shape, jax.ShapeDtypeStruct((t_len, d), BF16)],
        args=(x, g, w), sem=("parallel",), comm=comm)


def _mm_res(a, w, res, name, tm=512, comm=None):
    t_len, k = a.shape
    n = w.shape[1]

    def body(a_ref, w_ref, r_ref, o_ref):
        o_ref[...] = r_ref[...] + jnp.dot(a_ref[...], w_ref[...], preferred_element_type=F32)

    return _pallas(
        body, name=name, grid=(t_len // tm,),
        in_specs=[pl.BlockSpec((tm, k), lambda i: (i, 0)),
                  pl.BlockSpec((k, n), lambda i: (0, 0), pipeline_mode=ONCE),
                  pl.BlockSpec((tm, n), lambda i: (i, 0))],
        out_specs=[pl.BlockSpec((tm, n), lambda i: (i, 0))],
        out_shape=[jax.ShapeDtypeStruct((t_len, n), F32)],
        args=(a, w, res), sem=("parallel",), comm=comm)


def _mm_nt(a, w, name, tm=512, comm=None):
    t_len, k = a.shape
    n = w.shape[0]

    def body(a_ref, w_ref, o_ref):
        o_ref[...] = lax.dot_general(a_ref[...], w_ref[...], (((1,), (1,)), ((), ())),
                                     preferred_element_type=F32).astype(BF16)

    return _pallas(
        body, name=name, grid=(t_len // tm,),
        in_specs=[pl.BlockSpec((tm, k), lambda i: (i, 0)),
                  pl.BlockSpec((n, k), lambda i: (0, 0), pipeline_mode=ONCE)],
        out_specs=[pl.BlockSpec((tm, n), lambda i: (i, 0))],
        out_shape=[jax.ShapeDtypeStruct((t_len, n), BF16)],
        args=(a, w), sem=("parallel",), comm=comm)


def _mm_nt_normbwd(dz, w, x, g, dres, name, planes=False, tm=512, comm=None):
    t_len, d = x.shape
    n = w.shape[1]
    half = n // 2
    nt_dims = (((1,), (1,)), ((), ()))

    def body(dz_ref, w_ref, x_ref, g_ref, r_ref, dx_ref, dxb_ref, dg_ref):
        @pl.when(pl.program_id(0) == 0)
        def _():
            dg_ref[...] = jnp.zeros_like(dg_ref)

        if planes:
            dh = (lax.dot_general(dz_ref[0], w_ref[:, 0:half], nt_dims, preferred_element_type=F32)
                  + lax.dot_general(dz_ref[1], w_ref[:, half:], nt_dims, preferred_element_type=F32))
        else:
            dh = lax.dot_general(dz_ref[...], w_ref[...], nt_dims, preferred_element_type=F32)
        xv = x_ref[...]
        r = lax.rsqrt(jnp.mean(xv * xv, axis=-1, keepdims=True) + EPS)
        xh = xv * r
        dhg = dh * g_ref[...]
        dx = r_ref[...] + r * (dhg - xh * jnp.mean(dhg * xh, axis=-1, keepdims=True))
        dx_ref[...] = dx
        dxb_ref[...] = dx.astype(BF16)
        dg_ref[0:1, :] += jnp.sum(dh * xh, axis=0, keepdims=True)

    if planes:
        dz_spec = pl.BlockSpec((2, tm, half), lambda i: (0, i, 0))
    else:
        dz_spec = pl.BlockSpec((tm, n), lambda i: (i, 0))
    return _pallas(
        body, name=name, grid=(t_len // tm,),
        in_specs=[dz_spec,
                  pl.BlockSpec((d, n), lambda i: (0, 0), pipeline_mode=ONCE),
                  pl.BlockSpec((tm, d), lambda i: (i, 0)),
                  pl.BlockSpec((1, d), lambda i: (0, 0)),
                  pl.BlockSpec((tm, d), lambda i: (i, 0))],
        out_specs=[pl.BlockSpec((tm, d), lambda i: (i, 0)),
                   pl.BlockSpec((tm, d), lambda i: (i, 0)),
                   pl.BlockSpec((SUBLANES, d), lambda i: (0, 0))],
        out_shape=[jax.ShapeDtypeStruct((t_len, d), F32),
                   jax.ShapeDtypeStruct((t_len, d), BF16),
                   jax.ShapeDtypeStruct((SUBLANES, d), F32)],
        args=(dz, w, x, g, dres), sem=("arbitrary",), comm=comm)


def _mm_tn(a, g, name, tk, tn, planes=False, tt=1024, comm=None):
    t_len, k = a.shape
    n = 2 * g.shape[2] if planes else g.shape[1]
    nn = n // tn
    half = nn // 2
    tt = min(tt, t_len)

    def body(a_ref, g_ref, o_ref):
        @pl.when(pl.program_id(2) == 0)
        def _():
            o_ref[...] = jnp.zeros_like(o_ref)

        o_ref[...] += lax.dot_general(a_ref[...], g_ref[...], (((0,), (0,)), ((), ())),
                                      preferred_element_type=F32)

    if planes:
        g_spec = pl.BlockSpec((None, tt, tn), lambda i, j, t: (j // half, t, j % half))
    else:
        g_spec = pl.BlockSpec((tt, tn), lambda i, j, t: (t, j))
    return _pallas(
        body, name=name, grid=(k // tk, nn, t_len // tt),
        in_specs=[pl.BlockSpec((tt, tk), lambda i, j, t: (t, i)), g_spec],
        out_specs=[pl.BlockSpec((tk, tn), lambda i, j, t: (i, j))],
        out_shape=[jax.ShapeDtypeStruct((k, n), F32)],
        args=(a, g), sem=("parallel", "parallel", "arbitrary"), comm=comm)


def _lru_gates(rp, ip, spn):
    r = _sigmoid(rp)
    i = _sigmoid_tanh(ip)
    la = r * spn
    a = jnp.exp(la)
    mult = jnp.sqrt(_neg_expm1(2.0 * la, a))
    return r, i, a, mult


def _mixer_fwd(z, prm, gates, layer, name, tb=256, comm=None):
    t_len = z.shape[0]

    def body(z_ref, p_ref, g_ref, y_ref, h_ref, xhalo, phalo, hcar, lx_s, rp_s, ip_s):
        @pl.when(pl.program_id(0) == 0)
        def _():
            xhalo[...] = jnp.zeros_like(xhalo)
            phalo[...] = jnp.zeros_like(phalo)
            hcar[...] = jnp.zeros_like(hcar)

        prm_v = p_ref[...]
        cw = prm_v[0:4]
        vec = prm_v[4:8]
        xp = z_ref[:, 0:D_LRU].astype(F32)
        ext = jnp.concatenate([xhalo[...], xp], axis=0)
        lx = vec[0:1] + _conv_taps(ext, [cw[k:k + 1] for k in range(4)], tb)
        xhalo[...] = xp[tb - SUBLANES:]
        lx_s[...] = lx
        lxb = lx.astype(BF16)
        for q in range(4):
            sl = slice(q * 256, (q + 1) * 256)
            rp_s[:, sl] = jnp.dot(lxb[:, sl], g_ref[q], preferred_element_type=F32) + vec[1:2, sl]
            ip_s[:, sl] = jnp.dot(lxb[:, sl], g_ref[4 + q], preferred_element_type=F32) + vec[2:3, sl]

        spn = jnp.broadcast_to(-RG_C * _softplus_neg(vec[3:4]), (SUBLANES, D_LRU))
        row = lax.broadcasted_iota(jnp.int32, (SUBLANES, D_LRU), 0)

        def step(ci, carry):
            o = pl.multiple_of(ci * PACKED, PACKED)
            gate = z_ref[pl.ds(o, PACKED), D_LRU:2 * D_LRU].astype(F32)
            ys = []
            for sub in range(2):
                rows = pl.ds(pl.multiple_of(o + sub * SUBLANES, SUBLANES), SUBLANES)
                lxv = lx_s[rows, :]
                _, i, a, mult = _lru_gates(rp_s[rows, :], ip_s[rows, :], spn)
                h = _scan8(a, mult * (i * lxv), carry, row)
                h_ref[rows, :] = h
                ys.append(h * _gelu(gate[sub * SUBLANES:(sub + 1) * SUBLANES]))
                carry = jnp.broadcast_to(h[SUBLANES - 1:SUBLANES, :], (SUBLANES, D_LRU))
            y_ref[pl.ds(o, PACKED), 0:D_LRU] = jnp.concatenate(ys, axis=0).astype(BF16)
            return carry

        hcar[...] = lax.fori_loop(0, tb // PACKED, step, hcar[...])

        scw = prm_v[8:11, 0:D_SC]
        o_b, o_c, o_x = 2 * D_LRU, 2 * D_LRU + D_SC, 2 * D_LRU + 2 * D_SC
        p = z_ref[:, o_c:o_x].astype(F32) * z_ref[:, o_x:].astype(F32)
        pext = jnp.concatenate([phalo[...], p], axis=0)
        q = _conv_taps(pext, [scw[k:k + 1] for k in range(3)], tb)
        phalo[...] = p[tb - SUBLANES:]
        y_ref[:, D_LRU:] = (z_ref[:, o_b:o_c].astype(F32) * q).astype(BF16)

    return _pallas(
        body, name=name, grid=(t_len // tb,),
        in_specs=[pl.BlockSpec((tb, D_IN), lambda t: (t, 0)),
                  pl.BlockSpec((None, 2 * SUBLANES, D_LRU), lambda t: (layer, 0, 0)),
                  pl.BlockSpec((None, 8, 256, 256), lambda t: (layer, 0, 0, 0))],
        out_specs=[pl.BlockSpec((tb, D_MIX), lambda t: (t, 0)),
                   pl.BlockSpec((tb, D_LRU), lambda t: (t, 0))],
        out_shape=[jax.ShapeDtypeStruct((t_len, D_MIX), BF16),
                   jax.ShapeDtypeStruct((t_len, D_LRU), F32)],
        scratch_shapes=[pltpu.VMEM((SUBLANES, D_LRU), F32), pltpu.VMEM((SUBLANES, D_SC), F32),
                        pltpu.VMEM((SUBLANES, D_LRU), F32), pltpu.VMEM((tb, D_LRU), F32),
                        pltpu.VMEM((tb, D_LRU), F32), pltpu.VMEM((tb, D_LRU), F32)],
        args=(z, prm, gates), sem=("arbitrary",), comm=comm)


def _mixer_bwd(z, h, dy, prm, gates, layer, name, tb=256, comm=None):
    t_len = z.shape[0]
    nb = t_len // tb

    def body(z_ref, zh_ref, h_ref, hh_ref, dy_ref, p_ref, g_ref, dz_ref, dp_ref, dg_ref,
             lx_s, rp_s, ip_s, drpb_s, dipb_s, dlx_s, hext_s, acc_s, acar, gcar, dqh):
        t = pl.program_id(0)
        first_block = t == nb - 1

        @pl.when(t == 0)
        def _():
            for ref in (dp_ref, dg_ref, acc_s, acar, gcar, dqh):
                ref[...] = jnp.zeros_like(ref)
            dlx_s[tb:, :] = jnp.zeros((SUBLANES, D_LRU), F32)

        prm_v = p_ref[...]
        cw = prm_v[0:4]
        vec = prm_v[4:8]
        scw = prm_v[8:11, 0:D_SC]
        wa_ref = [g_ref.at[q] for q in range(4)]
        wx_ref = [g_ref.at[4 + q] for q in range(4)]
        dwa_ref = [dg_ref.at[q] for q in range(4)]
        dwx_ref = [dg_ref.at[4 + q] for q in range(4)]
        ctaps = [cw[k:k + 1] for k in range(4)]
        staps = [scw[k:k + 1] for k in range(3)]
        keep = jnp.where(first_block, 0.0, 1.0)
        zh = zh_ref[...].astype(F32)[PACKED - SUBLANES:] * keep

        xp = z_ref[:, 0:D_LRU].astype(F32)
        xext = jnp.concatenate([zh[:, 0:D_LRU], xp], axis=0)
        lx = vec[0:1] + _conv_taps(xext, ctaps, tb)
        lx_s[...] = lx
        lxb = lx.astype(BF16)
        for q in range(4):
            sl = slice(q * 256, (q + 1) * 256)
            rp_s[:, sl] = jnp.dot(lxb[:, sl], wa_ref[q][...], preferred_element_type=F32) + vec[1:2, sl]
            ip_s[:, sl] = jnp.dot(lxb[:, sl], wx_ref[q][...], preferred_element_type=F32) + vec[2:3, sl]
        hext_s[0:SUBLANES, :] = hh_ref[...] * keep
        hext_s[SUBLANES:, :] = h_ref[...]

        spn = jnp.broadcast_to(-RG_C * _softplus_neg(vec[3:4]), (SUBLANES, D_LRU))
        row = lax.broadcasted_iota(jnp.int32, (SUBLANES, D_LRU), 0)

        def step(ci, carry):
            a_next, g_next = carry
            o = pl.multiple_of((tb // PACKED - 1 - ci) * PACKED, PACKED)
            rows16 = pl.ds(o, PACKED)
            gate16 = z_ref[rows16, D_LRU:2 * D_LRU].astype(F32)
            dyl16 = dy_ref[rows16, 0:D_LRU].astype(F32)
            dgs, drs, dis = [None, None], [None, None], [None, None]
            for sub in (1, 0):
                oo = pl.multiple_of(o + sub * SUBLANES, SUBLANES)
                rows = pl.ds(oo, SUBLANES)
                half = slice(sub * SUBLANES, (sub + 1) * SUBLANES)
                lxv = lx_s[rows, :]
                r, i, a, mult = _lru_gates(rp_s[rows, :], ip_s[rows, :], spn)
                hwin = hext_s[pl.ds(oo, 2 * SUBLANES), :]
                hv = hwin[SUBLANES:]
                hprev = pltpu.roll(hwin, 1, axis=0)[SUBLANES:]
                gel, dgel = _gelu_parts(gate16[half])
                dyl = dyl16[half]
                a_up = jnp.where(row < SUBLANES - 1, pltpu.roll(a, SUBLANES - 1, axis=0), a_next)
                gg = _scan8_rev(a_up, dyl * gel, g_next, row)
                dgs[sub] = dyl * hv * dgel
                ilx = i * lxv
                dla = a * (gg * hprev - (gg * ilx) * a / mult)
                dlx_s[rows, :] = gg * mult * i
                drp = dla * spn * r * (1.0 - r)
                dip = gg * mult * lxv * i * (1.0 - i)
                drs[sub] = drp
                dis[sub] = dip
                acc_s[0] += drp
                acc_s[1] += dip
                acc_s[2] += dla * r
                a_next = jnp.broadcast_to(a[0:1, :], (SUBLANES, D_LRU))
                g_next = jnp.broadcast_to(gg[0:1, :], (SUBLANES, D_LRU))
            dz_ref[rows16, D_LRU:2 * D_LRU] = jnp.concatenate(dgs, axis=0).astype(BF16)
            drpb_s[rows16, :] = jnp.concatenate(drs, axis=0).astype(BF16)
            dipb_s[rows16, :] = jnp.concatenate(dis, axis=0).astype(BF16)
            return a_next, g_next

        a_c, g_c = lax.fori_loop(0, tb // PACKED, step, (acar[...], gcar[...]))
        acar[...] = a_c
        gcar[...] = g_c

        drpb = drpb_s[...]
        dipb = dipb_s[...]
        nt_dims = (((1,), (1,)), ((), ()))
        tn_dims = (((0,), (0,)), ((), ()))
        for q in range(4):
            sl = slice(q * 256, (q + 1) * 256)
            dlx_s[0:tb, sl] += (
                lax.dot_general(drpb[:, sl], wa_ref[q][...], nt_dims, preferred_element_type=F32)
                + lax.dot_general(dipb[:, sl], wx_ref[q][...], nt_dims, preferred_element_type=F32))
            dwa_ref[q][...] += lax.dot_general(lxb[:, sl], drpb[:, sl], tn_dims, preferred_element_type=F32)
            dwx_ref[q][...] += lax.dot_general(lxb[:, sl], dipb[:, sl], tn_dims, preferred_element_type=F32)

        dlx_ext = dlx_s[...]
        dlx = dlx_ext[0:tb]
        dz_ref[:, 0:D_LRU] = _conv_taps_t(dlx_ext, ctaps, tb).astype(BF16)
        dp_ref[3:4, :] += jnp.sum(dlx * xp, axis=0, keepdims=True)
        for k in range(3):
            shifted = pltpu.roll(xext, 3 - k, axis=0)[SUBLANES:]
            dp_ref[k:k + 1, :] += jnp.sum(dlx * shifted, axis=0, keepdims=True)
        dp_ref[4:5, :] += jnp.sum(dlx, axis=0, keepdims=True)
        dlx_s[tb:, :] = dlx[0:SUBLANES]

        o_b, o_c, o_x = 2 * D_LRU, 2 * D_LRU + D_SC, 2 * D_LRU + 2 * D_SC
        sb = z_ref[:, o_b:o_c].astype(F32)
        scc = z_ref[:, o_c:o_x].astype(F32)
        sx = z_ref[:, o_x:].astype(F32)
        p = scc * sx
        pext = jnp.concatenate([zh[:, o_c:o_x] * zh[:, o_x:], p], axis=0)
        q = _conv_taps(pext, staps, tb)
        dys = dy_ref[:, D_LRU:].astype(F32)
        dq = dys * sb
        dp = _conv_taps_t(jnp.concatenate([dq, dqh[...]], axis=0), staps, tb)
        dp_ref[10:11, 0:D_SC] += jnp.sum(dq * p, axis=0, keepdims=True)
        for k in range(2):
            shifted = pltpu.roll(pext, 2 - k, axis=0)[SUBLANES:]
            dp_ref[8 + k:9 + k, 0:D_SC] += jnp.sum(dq * shifted, axis=0, keepdims=True)
        dqh[...] = dq[0:SUBLANES]
        dz_ref[:, o_b:o_c] = (dys * q).astype(BF16)
        dz_ref[:, o_c:o_x] = (dp * sx).astype(BF16)
        dz_ref[:, o_x:] = (dp * scc).astype(BF16)

        @pl.when(first_block)
        def _():
            dp_ref[5:6, :] = jnp.sum(acc_s[0], axis=0, keepdims=True)
            dp_ref[6:7, :] = jnp.sum(acc_s[1], axis=0, keepdims=True)
            dp_ref[7:8, :] = (jnp.sum(acc_s[2], axis=0, keepdims=True) * RG_C * _sigmoid(-vec[3:4]))

    blk = lambda t: (nb - 1 - t, 0)
    halo8 = lambda t: (jnp.maximum((nb - 1 - t) * (tb // SUBLANES) - 1, 0), 0)
    halo16 = lambda t: (jnp.maximum((nb - 1 - t) * (tb // PACKED) - 1, 0), 0)
    return _pallas(
        body, name=name, grid=(nb,),
        in_specs=[pl.BlockSpec((tb, D_IN), blk), pl.BlockSpec((PACKED, D_IN), halo16),
                  pl.BlockSpec((tb, D_LRU), blk), pl.BlockSpec((SUBLANES, D_LRU), halo8),
                  pl.BlockSpec((tb, D_MIX), blk),
                  pl.BlockSpec((None, 2 * SUBLANES, D_LRU), lambda t: (layer, 0, 0)),
                  pl.BlockSpec((None, 8, 256, 256), lambda t: (layer, 0, 0, 0))],
        out_specs=[pl.BlockSpec((tb, D_IN), blk),
                   pl.BlockSpec((2 * SUBLANES, D_LRU), lambda t: (0, 0)),
                   pl.BlockSpec((8, 256, 256), lambda t: (0, 0, 0))],
        out_shape=[jax.ShapeDtypeStruct((t_len, D_IN), BF16),
                   jax.ShapeDtypeStruct((2 * SUBLANES, D_LRU), F32),
                   jax.ShapeDtypeStruct((8, 256, 256), F32)],
        scratch_shapes=[pltpu.VMEM((tb, D_LRU), F32),
                        pltpu.VMEM((tb, D_LRU), F32), pltpu.VMEM((tb, D_LRU), F32),
                        pltpu.VMEM((tb, D_LRU), BF16), pltpu.VMEM((tb, D_LRU), BF16),
                        pltpu.VMEM((tb + SUBLANES, D_LRU), F32), pltpu.VMEM((tb + SUBLANES, D_LRU), F32),
                        pltpu.VMEM((3, SUBLANES, D_LRU), F32),
                        pltpu.VMEM((SUBLANES, D_LRU), F32), pltpu.VMEM((SUBLANES, D_LRU), F32),
                        pltpu.VMEM((SUBLANES, D_SC), F32)],
        args=(z, z, h, h, dy, prm, gates), sem=("arbitrary",), comm=comm)


def _ffn_act(u, fw, layer, name, tb=512, tn=1536, rc=64, comm=None):
    t_len = u.shape[1]
    hb = tb // PACKED

    def body(u_ref, uh_ref, fw_ref, o_ref, fg_ref, fu_ref, ext):
        keep = jnp.where(pl.program_id(0) == 0, 0.0, 1.0)
        ext[:, 0:SUBLANES, :] = uh_ref[...].astype(F32)[:, PACKED - SUBLANES:, :] * keep
        ext[:, SUBLANES:, :] = u_ref[...].astype(F32)
        fw_v = fw_ref[...]

        for lb in range(tn // LANES):
            lanes = slice(lb * LANES, (lb + 1) * LANES)
            wg = [fw_v[0, k:k + 1, lanes] for k in range(3)]
            wu = [fw_v[1, k:k + 1, lanes] for k in range(3)]

            def chunk(ci, c, lanes=lanes, wg=wg, wu=wu):
                o = pl.multiple_of(ci * rc, rc)
                win = pl.ds(o, rc + SUBLANES)
                gate = _conv_taps(ext[0, win, lanes], wg, rc)
                up = _conv_taps(ext[1, win, lanes], wu, rc)
                gel, dgel = _gelu_parts(gate)
                rows = pl.ds(o, rc)
                o_ref[rows, lanes] = (gel * up).astype(BF16)
                fg_ref[rows, lanes] = (up * dgel).astype(BF16)
                fu_ref[rows, lanes] = gel.astype(BF16)
                return c

            lax.fori_loop(0, tb // rc, chunk, 0)

    spec = pl.BlockSpec((tb, tn), lambda i, j: (i, j))
    shape = jax.ShapeDtypeStruct((t_len, D_FF), BF16)
    return _pallas(
        body, name=name, grid=(t_len // tb, D_FF // tn),
        in_specs=[pl.BlockSpec((2, tb, tn), lambda i, j: (0, i, j)),
                  pl.BlockSpec((2, PACKED, tn), lambda i, j: (0, jnp.maximum(i * hb - 1, 0), j)),
                  pl.BlockSpec((None, 2, SUBLANES, tn), lambda i, j: (layer, 0, 0, j))],
        out_specs=[spec] * 3, out_shape=[shape] * 3,
        scratch_shapes=[pltpu.VMEM((2, tb + SUBLANES, tn), F32)],
        args=(u, u, fw), sem=("parallel", "parallel"), comm=comm)


def _ffn_bwd(dact, fg, fu, u, fw, layer, name, tb=512, tn=1536, rc=32, comm=None):
    t_len = u.shape[1]
    ni = t_len // tb
    hb = tb // PACKED
    last_halo = t_len // PACKED - 1

    def body(d_ref, dn_ref, fg_ref, fgn_ref, fu_ref, fun_ref, u_ref, up_ref, fw_ref, du_ref, dfw_ref,
             extu, extp, acc):
        i = pl.program_id(1)

        @pl.when(i == 0)
        def _():
            acc[...] = jnp.zeros_like(acc)

        keep_prev = jnp.where(i == 0, 0.0, 1.0)
        keep_next = jnp.where(i == ni - 1, 0.0, 1.0)
        extu[:, 0:SUBLANES, :] = up_ref[...].astype(F32)[:, PACKED - SUBLANES:, :] * keep_prev
        extu[:, SUBLANES:, :] = u_ref[...].astype(F32)
        dv = d_ref[...].astype(F32)
        dn = dn_ref[...].astype(F32)[0:SUBLANES] * keep_next
        extp[0, 0:tb, :] = dv * fg_ref[...].astype(F32)
        extp[0, tb:, :] = dn * fgn_ref[...].astype(F32)[0:SUBLANES]
        extp[1, 0:tb, :] = dv * fu_ref[...].astype(F32)
        extp[1, tb:, :] = dn * fun_ref[...].astype(F32)[0:SUBLANES]
        fw_v = fw_ref[...]
        m = rc + SUBLANES

        for lb in range(tn // LANES):
            lanes = slice(lb * LANES, (lb + 1) * LANES)
            taps = [[fw_v[pln, k:k + 1, lanes] for k in range(3)] for pln in range(2)]

            def chunk(ci, c, lanes=lanes, taps=taps):
                o = pl.multiple_of(ci * rc, rc)
                for pln in range(2):
                    e = extu[pln, pl.ds(o, m), lanes]
                    sh = [pltpu.roll(e, 2, axis=0)[SUBLANES:], pltpu.roll(e, 1, axis=0)[SUBLANES:], e[SUBLANES:]]
                    dpost = extp[pln, pl.ds(o, m), lanes]
                    du_ref[pln, pl.ds(o, rc), lanes] = _conv_taps_t(dpost, taps[pln], rc).astype(BF16)
                    for k in range(3):
                        prod = dpost[0:rc] * sh[k]
                        acc[3 * pln + k, :, lanes] += sum(
                            prod[s:s + SUBLANES] for s in range(0, rc, SUBLANES))
                return c

            lax.fori_loop(0, tb // rc, chunk, 0)

        @pl.when(i == ni - 1)
        def _():
            dfw_ref[...] = jnp.zeros_like(dfw_ref)
            for pln in range(2):
                for k in range(3):
                    dfw_ref[pln, k:k + 1, :] = jnp.sum(acc[3 * pln + k], axis=0, keepdims=True)

    main = pl.BlockSpec((tb, tn), lambda j, i: (i, j))
    nxt = pl.BlockSpec((PACKED, tn), lambda j, i: (jnp.minimum((i + 1) * hb, last_halo), j))
    return _pallas(
        body, name=name, grid=(D_FF // tn, ni),
        in_specs=[main, nxt, main, nxt, main, nxt,
                  pl.BlockSpec((2, tb, tn), lambda j, i: (0, i, j)),
                  pl.BlockSpec((2, PACKED, tn), lambda j, i: (0, jnp.maximum(i * hb - 1, 0), j)),
                  pl.BlockSpec((None, 2, SUBLANES, tn), lambda j, i: (layer, 0, 0, j))],
        out_specs=[pl.BlockSpec((2, tb, tn), lambda j, i: (0, i, j)),
                   pl.BlockSpec((2, SUBLANES, tn), lambda j, i: (0, 0, j))],
        out_shape=[jax.ShapeDtypeStruct((2, t_len, D_FF), BF16),
                   jax.ShapeDtypeStruct((2, SUBLANES, D_FF), F32)],
        scratch_shapes=[pltpu.VMEM((2, tb + SUBLANES, tn), F32),
                        pltpu.VMEM((2, tb + SUBLANES, tn), F32),
                        pltpu.VMEM((6, SUBLANES, tn), F32)],
        args=(dact, dact, fg, fg, fu, fu, u, u, fw), sem=("parallel", "arbitrary"), comm=comm)


def _loss_head(x, g, target, name, tb=256):
    t_len, d = x.shape

    def body(x_ref, g_ref, t_ref, dx_ref, dxb_ref, dg_ref, loss_ref):
        @pl.when(pl.program_id(0) == 0)
        def _():
            dg_ref[...] = jnp.zeros_like(dg_ref)
            loss_ref[...] = jnp.zeros_like(loss_ref)

        xv = x_ref[...]
        gv = g_ref[...]
        r = lax.rsqrt(jnp.mean(xv * xv, axis=-1, keepdims=True) + EPS)
        xh = xv * r
        err = xh * gv - t_ref[...]
        loss_ref[...] += (0.5 / d) * jnp.sum(jnp.sum(err * err, axis=-1, keepdims=True), axis=0, keepdims=True)
        dy = err * (1.0 / d)
        dyg = dy * gv
        dx = r * (dyg - xh * jnp.mean(dyg * xh, axis=-1, keepdims=True))
        dx_ref[...] = dx
        dxb_ref[...] = dx.astype(BF16)
        dg_ref[0:1, :] += jnp.sum(dy * xh, axis=0, keepdims=True)

    return _pallas(
        body, name=name, grid=(t_len // tb,),
        in_specs=[pl.BlockSpec((tb, d), lambda i: (i, 0)), pl.BlockSpec((1, d), lambda i: (0, 0)),
                  pl.BlockSpec((tb, d), lambda i: (i, 0))],
        out_specs=[pl.BlockSpec((tb, d), lambda i: (i, 0)), pl.BlockSpec((tb, d), lambda i: (i, 0)),
                   pl.BlockSpec((SUBLANES, d), lambda i: (0, 0)),
                   pl.BlockSpec((SUBLANES, LANES), lambda i: (0, 0))],
        out_shape=[jax.ShapeDtypeStruct((t_len, d), F32), jax.ShapeDtypeStruct((t_len, d), BF16),
                   jax.ShapeDtypeStruct((SUBLANES, d), F32), jax.ShapeDtypeStruct((SUBLANES, LANES), F32)],
        args=(x, g, target), sem=("arbitrary",))[0]


def _adamw(w, g, m, v, name, emit_grad=False, comm=None):
    r, c = w.shape
    tr = 256 if r % 256 == 0 else r
    c1 = 1.0 / (1.0 - ADAM_B1 ** ADAM_STEP)
    c2 = 1.0 / (1.0 - ADAM_B2 ** ADAM_STEP)

    def body(w_ref, g_ref, m_ref, v_ref, d_ref, mo_ref, vo_ref, *go_ref):
        gv = g_ref[...]
        mn = ADAM_B1 * m_ref[...] + (1.0 - ADAM_B1) * gv
        vn = ADAM_B2 * v_ref[...] + (1.0 - ADAM_B2) * (gv * gv)
        d_ref[...] = -ADAM_LR * ((mn * c1) / (jnp.sqrt(vn * c2) + ADAM_EPS) + ADAM_WD * w_ref[...])
        mo_ref[...] = mn
        vo_ref[...] = vn
        if emit_grad:
            go_ref[0][...] = gv

    spec = pl.BlockSpec((tr, c), lambda i: (i, 0))
    shape = jax.ShapeDtypeStruct((r, c), F32)
    n_out = 4 if emit_grad else 3
    return _pallas(
        body, name=name, grid=(r // tr,),
        in_specs=[spec] * 4, out_specs=[spec] * n_out, out_shape=[shape] * n_out,
        args=(w, g, m, v), sem=("parallel",), comm=comm)


def _place():
    x, y, c = lax.axis_index("x"), lax.axis_index("y"), lax.axis_index("c")
    chips = [(1 - x, y), (x, 1 - y), (1 - x, 1 - y)]
    return x, y, c, chips


def _remote(src, dst, send, recv, sem, to):
    return pltpu.make_async_remote_copy(
        src_ref=src, dst_ref=dst, send_sem=send.at[sem], recv_sem=recv.at[sem], device_id=to, device_id_type=MESH)


def _gather_plan(fulls, kinds, mid_at=None, parts=None):
    parts = parts or [(0, 1)] * len(fulls)

    def region(it, f, k, cc):
        kind = kinds[it]
        p, n = parts[it][0:2]
        count = parts[it][2] if len(parts[it]) > 2 else 1
        if kind == SMALL:
            return f.at[k, pl.ds(cc * (CONV_PACK_ROWS // 2), CONV_PACK_ROWS // 2), :]
        if COL_SHARDED[kind]:
            rows, cols = f.shape[0] // (2 * n), f.shape[1] // N_CHIP
            return f.at[pl.ds((cc * n + p) * rows, count * rows), pl.ds(k * cols, cols)]
        assert n == 1
        rows = f.shape[0] // N_CHIP
        return f.at[pl.ds(k * rows + cc * (rows // 2), rows // 2), :]

    def first_hop(bufs, send, recv, it, j):
        x, y, c, chips = _place()
        reg = region(it, bufs[it], 2 * x + y, c)
        return _remote(reg, reg, send, recv, it * 6 + j, (*chips[j], c))

    def arrival(bufs, send, recv, it, j, second):
        x, y, c, chips = _place()
        px, py = chips[j]
        reg = region(it, bufs[it], 2 * px + py, 1 - c if second else c)
        to = (x, y, 1 - c) if second else (px, py, c)
        return _remote(reg, reg, send, recv, it * 6 + (3 + j if second else j), to)

    def forward(bufs, send, recv, it, j):
        x, y, c, chips = _place()
        px, py = chips[j]
        reg = region(it, bufs[it], 2 * px + py, c)
        return _remote(reg, reg, send, recv, it * 6 + 3 + j, (x, y, 1 - c))

    def start(srcs, bufs, outs, send, recv):
        for it in range(len(bufs)):
            for j in range(3):
                first_hop(bufs, send, recv, it, j).start()

    def mid(srcs, bufs, outs, send, recv):
        for it in range(len(bufs)):
            for j in range(3):
                arrival(bufs, send, recv, it, j, False).wait_recv()
                forward(bufs, send, recv, it, j).start()

    def finish(srcs, bufs, outs, send, recv):
        for it in range(len(bufs)):
            for j in range(3):
                arrival(bufs, send, recv, it, j, True).wait_recv()
        for it in range(len(bufs)):
            for j in range(3):
                first_hop(bufs, send, recv, it, j).wait_send()
                forward(bufs, send, recv, it, j).wait_send()

    return Comm(srcs=(), bufs=tuple(fulls), outs=(), n_sem=6 * len(fulls), start=start, mid=mid, finish=finish,
                mid_at=mid_at)


def _half_axis(kind):
    return 0 if kind == SMALL or COL_SHARDED[kind] else 1


def _half2(ref, kind, cc):
    if _half_axis(kind) == 0:
        return ref.at[pl.ds(cc * (ref.shape[0] // 2), ref.shape[0] // 2), :]
    return ref.at[:, pl.ds(cc * (ref.shape[1] // 2), ref.shape[1] // 2)]


def _pair_plan(grads, kinds):
    def land_shape(g, kind):
        s = list(g.shape)
        s[_half_axis(kind)] //= 2
        return jax.ShapeDtypeStruct(tuple(s), F32)

    def copy(srcs, outs, send, recv, it):
        x, y, c, _ = _place()
        return _remote(_half2(srcs[it], kinds[it], 1 - c), outs[it], send, recv, it, (x, y, 1 - c))

    def start(srcs, bufs, outs, send, recv):
        for it in range(len(srcs)):
            copy(srcs, outs, send, recv, it).start()

    def finish(srcs, bufs, outs, send, recv):
        for it in range(len(srcs)):
            copy(srcs, outs, send, recv, it).wait_send()
        for it in range(len(srcs)):
            copy(srcs, outs, send, recv, it).wait_recv()

    return Comm(srcs=tuple(grads), bufs=(), outs=tuple(land_shape(g, k) for g, k in zip(grads, kinds)),
                n_sem=len(grads), start=start, finish=finish)


def _scatter_plan(parts, slots, kinds):
    def piece(s, kind, k):
        if kind == SMALL:
            return s
        if COL_SHARDED[kind]:
            n = s.shape[1] // N_CHIP
            return s.at[:, pl.ds(k * n, n)]
        n = s.shape[0] // N_CHIP
        return s.at[pl.ds(k * n, n), :]

    def outbound(srcs, bufs, send, recv, it, j):
        x, y, c, chips = _place()
        px, py = chips[j]
        return _remote(piece(srcs[it], kinds[it], 2 * px + py), bufs[it].at[2 * x + y], send, recv, it * 3 + j,
                       (px, py, c))

    def inbound(bufs, send, recv, it, j):
        x, y, c, chips = _place()
        px, py = chips[j]
        got = bufs[it].at[2 * px + py]
        return _remote(got, got, send, recv, it * 3 + j, (px, py, c))

    def start(srcs, bufs, outs, send, recv):
        for it in range(len(srcs)):
            for j in range(3):
                outbound(srcs, bufs, send, recv, it, j).start()

    def finish(srcs, bufs, outs, send, recv):
        for it in range(len(srcs)):
            for j in range(3):
                inbound(bufs, send, recv, it, j).wait_recv()
        for it in range(len(srcs)):
            for j in range(3):
                outbound(srcs, bufs, send, recv, it, j).wait_send()

    return Comm(srcs=tuple(parts), bufs=tuple(slots), outs=(), n_sem=3 * len(parts), start=start, finish=finish)


def _share_plan(fulls, kinds, layer):
    def half(f, kind, cc):
        return _half2(f if kind == SMALL else f.at[layer], kind, cc)

    def copy(bufs, send, recv, it, cc):
        x, y, c, _ = _place()
        reg = half(bufs[it], kinds[it], c if cc == "mine" else 1 - c)
        return _remote(reg, reg, send, recv, it, (x, y, 1 - c))

    def start(srcs, bufs, outs, send, recv):
        for it in range(len(bufs)):
            copy(bufs, send, recv, it, "mine").start()

    def finish(srcs, bufs, outs, send, recv):
        for it in range(len(bufs)):
            copy(bufs, send, recv, it, "other").wait_recv()
        for it in range(len(bufs)):
            copy(bufs, send, recv, it, "mine").wait_send()

    return Comm(srcs=(), bufs=tuple(fulls), outs=(), n_sem=len(fulls), start=start, finish=finish)


def _pair_sum(g, land, idx, kind, name):
    odt = F32 if kind == SMALL else BF16
    r, cdim = land.shape

    def body(idx_ref, g_ref, l_ref, p_ref, s_ref):
        v = (g_ref[...] + l_ref[...]).astype(odt)
        p_ref[...] = v
        if kind == SMALL:
            s_ref[...] = v
        else:
            @pl.when(pl.program_id(1 if COL_SHARDED[kind] else 0) == idx_ref[1])
            def _():
                s_ref[...] = v

    if kind == SMALL:
        grid = (1,)
        g_spec = pl.BlockSpec((r, LANES), lambda i, idx_ref: (idx_ref[0], 0))
        spec = pl.BlockSpec((r, LANES), lambda i, idx_ref: (0, 0))
        s_spec = pl.BlockSpec((None, r, LANES), lambda i, idx_ref: (idx_ref[1], 0, 0))
        s_shape = (N_CHIP, r, LANES)
    elif COL_SHARDED[kind]:
        pc, tr = cdim // N_CHIP, 256
        nrb = r // tr
        grid = (nrb, N_CHIP)
        g_spec = pl.BlockSpec((tr, pc), lambda i, k, idx_ref: (idx_ref[0] * nrb + i, k))
        spec = pl.BlockSpec((tr, pc), lambda i, k, idx_ref: (i, k))
        s_spec = pl.BlockSpec((None, tr, pc), lambda i, k, idx_ref: (idx_ref[1], i, 0))
        s_shape = (N_CHIP, r, pc)
    else:
        pr = r // N_CHIP
        grid = (N_CHIP,)
        g_spec = pl.BlockSpec((pr, cdim), lambda k, idx_ref: (k, idx_ref[0]))
        spec = pl.BlockSpec((pr, cdim), lambda k, idx_ref: (k, 0))
        s_spec = pl.BlockSpec((None, pr, cdim), lambda k, idx_ref: (idx_ref[1], 0, 0))
        s_shape = (N_CHIP, pr, cdim)
    return pl.pallas_call(
        body, name=name,
        grid_spec=pltpu.PrefetchScalarGridSpec(
            num_scalar_prefetch=1, grid=grid, in_specs=[g_spec, spec], out_specs=[spec, s_spec]),
        out_shape=[jax.ShapeDtypeStruct(land.shape, odt), jax.ShapeDtypeStruct(s_shape, odt)],
        compiler_params=_cp(*(["arbitrary"] * len(grid))),
    )(idx, g, land)


def _sum_slots(slots, idx, kind, layer, prev, name):
    _, r, cdim = slots.shape

    def body(*refs):
        s_ref, o_ref = refs[1], refs[-1]
        v = s_ref[...].astype(F32)
        o_ref[...] = (v[0] + v[1]) + (v[2] + v[3])

    if kind == SMALL:
        grid = (1,)
        s_spec = pl.BlockSpec((N_CHIP, r, cdim), lambda i, idx_ref: (0, 0, 0))
        o_spec = pl.BlockSpec((r, cdim), lambda i, idx_ref: (idx_ref[0], 0))
        full = (2 * r, cdim)
    else:
        tr = 256 if r % 256 == 0 else 384
        nrb = r // tr
        grid = (nrb,)
        s_spec = pl.BlockSpec((N_CHIP, tr, cdim), lambda i, idx_ref: (0, i, 0))
        if COL_SHARDED[kind]:
            o_spec = pl.BlockSpec((None, tr, cdim), lambda i, idx_ref: (layer, idx_ref[0] * nrb + i, 0))
            full = (2, 2 * r, cdim)
        else:
            o_spec = pl.BlockSpec((None, tr, cdim), lambda i, idx_ref: (layer, i, idx_ref[0]))
            full = (2, r, 2 * cdim)
    in_specs, args, aliases = [s_spec], [idx, slots], {}
    if prev is not None:
        in_specs.append(ANY)
        args.append(prev)
        aliases = {2: 0}
    return pl.pallas_call(
        body, name=name,
        grid_spec=pltpu.PrefetchScalarGridSpec(
            num_scalar_prefetch=1, grid=grid, in_specs=in_specs, out_specs=o_spec),
        out_shape=jax.ShapeDtypeStruct(full, F32),
        input_output_aliases=aliases,
        compiler_params=_cp(*(["parallel"] * len(grid))),
    )(*args)


def _block_diag(w):
    w4 = w.reshape(2, 4, 4, 64, 64)
    eye = jnp.eye(4, dtype=w.dtype)[None, None, :, None, :, None]
    return (w4[:, :, :, :, None, :] * eye).reshape(2, 4, 256, 256)


def _block_diag_extract(d):
    d5 = d.reshape(4, 4, 64, 4, 64)
    return jnp.stack([d5[:, hh, :, hh, :] for hh in range(4)], axis=1).reshape(-1)


REP_NAMES = ("norm1_g", "lru_conv_b", "lru_ba", "lru_bx", "lru_lambda", "norm2_g", "lru_wa", "lru_wx")


def _pack_rep(norm1_g, conv_b, ba, bx, lam, norm2_g, wa, wx, final_g):
    parts = [a.reshape(-1) for a in (norm1_g, conv_b, ba, bx, lam, norm2_g, wa, wx, final_g)]
    return jnp.concatenate(parts).reshape(REP_ROWS, LANES)


def _unpack_rep(buf):
    flat = buf.reshape(-1)
    res, o = {}, 0
    for k in REP_NAMES:
        shape = (2, 16, 64, 64) if k in ("lru_wa", "lru_wx") else (2, 1024)
        n = math.prod(shape)
        res[k] = flat[o:o + n].reshape(shape)
        o += n
    res["final_g"] = flat[o:o + 1024]
    return res


def _pack_conv_shard(lru_cw, sc_cw, ffn_cw):
    return jnp.concatenate([lru_cw.reshape(16, LANES), jnp.pad(sc_cw.reshape(6, LANES), ((0, 2), (0, 0))),
                            ffn_cw.reshape(72, LANES)], axis=0)


def _unpack_conv_shard(buf):
    return (buf[0:16].reshape(2, 4, 256), buf[16:22].reshape(2, 3, 128), buf[24:96].reshape(2, 3, 1536))


def kernel(x, norm1_g, w_in, lru_conv_w, lru_conv_b, lru_wa, lru_ba, lru_wx, lru_bx, lru_lambda, sc_conv_w, w_out, norm2_g, w_up, ffn_conv_w, w_down, final_g, loss_target, m_norm1_g, m_w_in, m_lru_conv_w, m_lru_conv_b, m_lru_wa, m_lru_ba, m_lru_wx, m_lru_bx, m_lru_lambda, m_sc_conv_w, m_w_out, m_norm2_g, m_w_up, m_ffn_conv_w, m_w_down, m_final_g, v_norm1_g, v_w_in, v_lru_conv_w, v_lru_conv_b, v_lru_wa, v_lru_ba, v_lru_wx, v_lru_bx, v_lru_lambda, v_sc_conv_w, v_w_out, v_norm2_g, v_w_up, v_ffn_conv_w, v_w_down, v_final_g):
    me = 2 * lax.axis_index("x") + lax.axis_index("y")
    idx = jnp.stack([lax.axis_index("c"), me]).astype(jnp.int32)
    t_len = x.shape[1]

    s_conv = _pack_conv_shard(lru_conv_w, sc_conv_w, ffn_conv_w)
    conv_slots = lax.dynamic_update_slice(jnp.zeros((N_CHIP, CONV_PACK_ROWS, LANES), F32), s_conv[None], (me, 0, 0))
    wi = list(_cast_into_full(w_in, W_IN, idx, "cast_w_in"))
    wo = list(_cast_into_full(w_out, W_OUT, idx, "cast_w_out"))
    wu = list(_cast_into_full(w_up, W_UP, idx, "cast_w_up"))
    wd = list(_cast_into_full(w_down, W_DOWN, idx, "cast_w_down"))
    wi[0], convs = _comm_call(_gather_plan([wi[0], conv_slots], [W_IN, SMALL]), "ag_first")
    per_chip = [_unpack_conv_shard(convs[k]) for k in range(N_CHIP)]
    lru_cw = jnp.concatenate([p[0] for p in per_chip], axis=-1)
    sc_cw = jnp.concatenate([p[1] for p in per_chip], axis=-1)
    ffn_cw = jnp.concatenate([p[2] for p in per_chip], axis=-1)

    prm = jnp.concatenate(
        [lru_cw, jnp.stack([lru_conv_b, lru_ba, lru_bx, lru_lambda], axis=1),
         jnp.pad(sc_cw, ((0, 0), (0, 0), (0, D_LRU - D_SC))), jnp.zeros((2, 5, D_LRU), F32)], axis=1)
    gates = jnp.concatenate([_block_diag(lru_wa), _block_diag(lru_wx)], axis=1).astype(BF16)
    fw8 = jnp.pad(ffn_cw.reshape(2, 3, 2, D_FF).transpose(0, 2, 1, 3), ((0, 0), (0, 0), (0, 5), (0, 0)))

    xs = x[0]
    saved = []
    n512, n256 = t_len // 512, t_len // 256
    whole, lower, upper = (0, 1), (0, 2), (1, 2)
    carried_by = {
        "fwd_in_0": ([(wu, 0, W_UP, (0, 4))], (max(n512 - 3, 0),)),
        "fwd_mixer_0": ([(wu, 0, W_UP, (1, 4, 2)), (wo, 0, W_OUT, whole)], (max(n256 - 3, 0),)),
        "fwd_out_0": ([(wu, 0, W_UP, (3, 4))], (max(n512 - 2, 0),)),
        "fwd_up_0": ([(wd, 0, W_DOWN, whole)], (max(n512 - 2, 0),)),
        "fwd_act_0": ([(wi, 1, W_IN, whole), (wo, 1, W_OUT, whole)], (max(n512 - 2, 0), 0)),
        "fwd_down_0": ([(wu, 1, W_UP, (0, 4))], (max(n512 - 3, 0),)),
        "fwd_in_1": ([(wu, 1, W_UP, (1, 4))], (max(n512 - 3, 0),)),
        "fwd_mixer_1": ([(wu, 1, W_UP, (2, 4, 2))], (max(n256 - 4, 0),)),
        "fwd_act_1": ([(wd, 1, W_DOWN, whole)], (max(n512 - 3, 0), 0)),
    }

    def carried(name):
        if name not in carried_by:
            return None, lambda got: None
        items, mid_at = carried_by[name]

        def store(got):
            for (lst, i, _, _), arr in zip(items, got):
                lst[i] = arr

        return _gather_plan([lst[i] for lst, i, _, _ in items], [k for _, _, k, _ in items], mid_at=mid_at,
                            parts=[p for _, _, _, p in items]), store

    for l in range(2):
        comm, store = carried(f"fwd_in_{l}")
        (z, h1), got = _norm_mm(xs, norm1_g[l][None], wi[l], f"fwd_in_{l}", comm=comm)
        store(got)
        comm, store = carried(f"fwd_mixer_{l}")
        (ymix, hst), got = _mixer_fwd(z, prm, gates, l, f"fwd_mixer_{l}", comm=comm)
        store(got)
        comm, store = carried(f"fwd_out_{l}")
        (x2,), got = _mm_res(ymix, wo[l], xs, f"fwd_out_{l}", comm=comm)
        store(got)
        comm, store = carried(f"fwd_up_{l}")
        (u, h2), got = _norm_mm(x2, norm2_g[l][None], wu[l], f"fwd_up_{l}", planes=True, comm=comm)
        store(got)
        comm, store = carried(f"fwd_act_{l}")
        (act, fg, fu), got = _ffn_act(u, fw8, l, f"fwd_act_{l}", comm=comm)
        store(got)
        comm, store = carried(f"fwd_down_{l}")
        (x3,), got = _mm_res(act, wd[l], x2, f"fwd_down_{l}", comm=comm)
        store(got)
        saved.append((xs, h1, z, hst, ymix, x2, h2, u, act, fg, fu))
        xs = x3

    dx, dxb, dgf, loss_blk = _loss_head(xs, final_g[None], loss_target[0], "loss_head")

    kinds = [W_IN, W_OUT, W_UP, W_DOWN]
    grads = [None, None]
    small = [None, None]
    reduced = [None] * 4
    summed1 = [None] * 4
    slots1 = [None] * 4

    def scatter1(ws):
        return _scatter_plan([summed1[w][0] for w in ws], [summed1[w][1] for w in ws], ws)

    for l in (1, 0):
        x_in, h1, z, hst, ymix, x2, h2, u, act, fg, fu = saved[l]
        carry = l == 0
        comm = _pair_plan([grads[1][W_IN]], [W_IN]) if carry else None
        (g_down,), got = _mm_tn(act, dxb, f"bwd_wdown_{l}", tk=1536, tn=1024, comm=comm)
        if carry:
            summed1[W_IN] = _pair_sum(grads[1][W_IN], got[0], idx, W_IN, "rs_add1_0")
        (dact,), _ = _mm_nt(dxb, wd[l], f"bwd_dact_{l}")
        comm = scatter1((W_UP, W_IN)) if carry else None
        (du, dfw), got = _ffn_bwd(dact, fg, fu, u, fw8, l, f"bwd_act_{l}", comm=comm)
        if carry:
            slots1[W_UP], slots1[W_IN] = got
            reduced = [_sum_slots(slots1[w], idx, kinds[w], 1, None, f"rs_sum1_{w}") for w in range(4)]
        comm = _share_plan(reduced, kinds, 1) if carry else None
        (g_up,), got = _mm_tn(h2, du, f"bwd_wup_{l}", tk=1024, tn=1536, planes=True, comm=comm)
        if carry:
            reduced = list(got)
        comm = _pair_plan([g_up, g_down], [W_UP, W_DOWN]) if carry else None
        (dx2, dx2b, dg2), got = _mm_nt_normbwd(du, wu[l], x2, norm2_g[l][None], dx, f"bwd_up_{l}", planes=True,
                                               comm=comm)
        if carry:
            sum_up = _pair_sum(g_up, got[0], idx, W_UP, "rs_add0_2")
            sum_down = _pair_sum(g_down, got[1], idx, W_DOWN, "rs_add0_3")
        (g_out,), _ = _mm_tn(ymix, dx2b, f"bwd_wout_{l}", tk=1536, tn=1024)
        comm = _pair_plan([g_out], [W_OUT]) if carry else None
        (dymix,), got = _mm_nt(dx2b, wo[l], f"bwd_dymix_{l}", comm=comm)
        trio = (W_OUT, W_UP, W_DOWN)
        if carry:
            sum_out = _pair_sum(g_out, got[0], idx, W_OUT, "rs_add0_1")
            comm = _scatter_plan([sum_out[0], sum_up[0], sum_down[0]], [sum_out[1], sum_up[1], sum_down[1]], trio)
        else:
            comm = _pair_plan([g_out, g_up, g_down], trio)
        (dz, dprm, dgates), got = _mixer_bwd(z, hst, dymix, prm, gates, l, f"bwd_mixer_{l}", comm=comm)
        if carry:
            for w, s in zip(trio, got):
                reduced[w] = _sum_slots(s, idx, w, 0, reduced[w], f"rs_sum0_{w}")
        else:
            for w, g, land in zip(trio, (g_out, g_up, g_down), got):
                summed1[w] = _pair_sum(g, land, idx, w, f"rs_add1_{w}")
        comm = _share_plan([reduced[w] for w in trio], trio, 0) if carry else scatter1((W_DOWN,))
        (g_in,), got = _mm_tn(h1, dz, f"bwd_win_{l}", tk=1024, tn=1792, comm=comm)
        if carry:
            for w, full in zip(trio, got):
                reduced[w] = full
        else:
            slots1[W_DOWN], = got
        if carry:
            land_in, = _comm_call(_pair_plan([g_in], [W_IN]), "rs_pair_in")
            sum_in = _pair_sum(g_in, land_in, idx, W_IN, "rs_add0_0")
            comm = _scatter_plan([sum_in[0]], [sum_in[1]], [W_IN])
        else:
            comm = scatter1((W_OUT,))
        (dx, dxb, dg1), got = _mm_nt_normbwd(dz, wi[l], x_in, norm1_g[l][None], dx2, f"bwd_in_{l}", comm=comm)
        if carry:
            slot_in, = got
        else:
            slots1[W_OUT], = got
        grads[l] = [g_in, g_out, g_up, g_down]
        rep = dict(zip(REP_NAMES, [dg1[0], dprm[4], dprm[5], dprm[6], dprm[7], dg2[0],
                                   _block_diag_extract(dgates[0:4]), _block_diag_extract(dgates[4:8])]))
        conv = [dprm[0:4].reshape(-1), jnp.pad(dprm[8:11, 0:D_SC].reshape(-1), (0, 512)),
                dfw[:, 0:3, :].transpose(1, 0, 2).reshape(-1)]
        small[l] = (rep, conv)
    grad_x = dx[None]
    g_small = jnp.concatenate(
        [small[l][0][k] for k in REP_NAMES for l in range(2)] + [dgf[0]] + small[0][1] + small[1][1]
        + [loss_blk.reshape(-1)]).reshape(SMALL_ROWS, LANES)

    def big(w, g, m, v, name):
        shape = w.shape
        two_d = lambda a: a.reshape(-1, shape[-1])
        outs, _ = _adamw(two_d(w), two_d(g), two_d(m), two_d(v), name, emit_grad=True)
        return [o.reshape(shape) for o in outs]

    land_small, = _comm_call(_pair_plan([g_small], [SMALL]), "rs_pair_small")
    sum_small = _pair_sum(g_small, land_small, idx, SMALL, "rs_add0_4")
    slot_small, = _comm_call(_scatter_plan([sum_small[0]], [sum_small[1]], [SMALL]), "rs_scatter_small")
    gw_in, gs = _comm_call(
        _share_plan([_sum_slots(slot_in, idx, W_IN, 0, reduced[W_IN], "rs_sum0_0"),
                     _sum_slots(slot_small, idx, SMALL, 0, None, "rs_sum0_4")], [W_IN, SMALL], 0), "rs_share0")
    upd = {"w_up": big(w_up, reduced[W_UP], m_w_up, v_w_up, "adamw_w_up"),
           "w_down": big(w_down, reduced[W_DOWN], m_w_down, v_w_down, "adamw_w_down"),
           "w_out": big(w_out, reduced[W_OUT], m_w_out, v_w_out, "adamw_w_out"),
           "w_in": big(w_in, gw_in, m_w_in, v_w_in, "adamw_w_in")}

    loss = gs[REP_ROWS + CONV_ROWS, 0]
    g_rep = gs[0:REP_ROWS]
    g_conv = gs[REP_ROWS:REP_ROWS + CONV_ROWS].reshape(2, CONV_LAYER)
    g_lru_cw = lax.dynamic_slice_in_dim(g_conv[:, 0:4096].reshape(2, 4, 1024), me * 256, 256, axis=2)
    g_sc_cw = lax.dynamic_slice_in_dim(g_conv[:, 4096:4096 + 1536].reshape(2, 3, 512), me * 128, 128, axis=2)
    g_ffn_cw = lax.dynamic_slice_in_dim(g_conv[:, 6144:].reshape(2, 3, 6144), me * 1536, 1536, axis=2)

    rep_out, _ = _adamw(
        _pack_rep(norm1_g, lru_conv_b, lru_ba, lru_bx, lru_lambda, norm2_g, lru_wa, lru_wx, final_g), g_rep,
        _pack_rep(m_norm1_g, m_lru_conv_b, m_lru_ba, m_lru_bx, m_lru_lambda, m_norm2_g, m_lru_wa, m_lru_wx, m_final_g),
        _pack_rep(v_norm1_g, v_lru_conv_b, v_lru_ba, v_lru_bx, v_lru_lambda, v_norm2_g, v_lru_wa, v_lru_wx, v_final_g),
        "adamw_rep")
    conv_out, _ = _adamw(s_conv, _pack_conv_shard(g_lru_cw, g_sc_cw, g_ffn_cw),
                         _pack_conv_shard(m_lru_conv_w, m_sc_conv_w, m_ffn_conv_w),
                         _pack_conv_shard(v_lru_conv_w, v_sc_conv_w, v_ffn_conv_w), "adamw_conv")

    names = ["norm1_g", "w_in", "lru_conv_w", "lru_conv_b", "lru_wa", "lru_ba", "lru_wx", "lru_bx", "lru_lambda",
             "sc_conv_w", "w_out", "norm2_g", "w_up", "ffn_conv_w", "w_down", "final_g"]
    groups = []
    g_all = dict(_unpack_rep(g_rep))
    g_all.update({k: v[3] for k, v in upd.items()})
    g_all.update(lru_conv_w=g_lru_cw, sc_conv_w=g_sc_cw, ffn_conv_w=g_ffn_cw)
    groups.append(g_all)
    for i in range(3):
        d = dict(_unpack_rep(rep_out[i]))
        cl, cs, cf = _unpack_conv_shard(conv_out[i])
        d.update(lru_conv_w=cl, sc_conv_w=cs, ffn_conv_w=cf)
        d.update({k: v[i] for k, v in upd.items()})
        groups.append(d)
    return (loss, grad_x, *[grp[n] for grp in groups for n in names])
```

```python
import dataclasses
import functools
import math
import operator
from typing import Any, Callable, Optional, Sequence

import jax
import jax.numpy as jnp
from jax import lax
from jax.experimental import pallas as pl
from jax.experimental.pallas import tpu as pltpu

F32 = jnp.float32
BF16 = jnp.bfloat16
MESH = pl.DeviceIdType.MESH

D_MODEL = 1024
D_LRU = 1024
D_SC = 512
D_MIX = D_LRU + D_SC
D_IN = 2 * D_LRU + 3 * D_SC
D_FF = 3072
N_CHIP = 4
RG_C = 8.0
EPS = 1e-6
ADAM_LR = 0.001
ADAM_B1 = 0.9
ADAM_B2 = 0.999
ADAM_EPS = 1e-08
ADAM_WD = 0.01
ADAM_STEP = 10

SUBLANES = 8
PACKED = 16
LANES = 128
VMEM_LIMIT = 56 * 1024 * 1024
GELU_C0 = math.sqrt(2.0 / math.pi)
GELU_C1 = 0.044715

REP_LAYER = 6 * 1024 + 2 * 16 * 64 * 64
REP_ROWS = (2 * REP_LAYER + 1024) // LANES
CONV_LAYER = 4 * 1024 + 2048 + 3 * 6144
CONV_ROWS = 2 * CONV_LAYER // LANES
SMALL_ROWS = REP_ROWS + CONV_ROWS + 8
CONV_PACK_ROWS = 96

W_IN, W_OUT, W_UP, W_DOWN, SMALL = range(5)
COL_SHARDED = {W_IN: True, W_OUT: False, W_UP: True, W_DOWN: False}

ONCE = pl.Buffered(1)
ANY = pl.BlockSpec(memory_space=pl.ANY)


def _cp(*sem):
    return pltpu.CompilerParams(dimension_semantics=sem, vmem_limit_bytes=VMEM_LIMIT)


@dataclasses.dataclass
class Comm:
    srcs: Sequence[Any]
    bufs: Sequence[Any]
    outs: Sequence[Any]
    n_sem: int
    start: Callable
    finish: Callable
    mid: Optional[Callable] = None
    mid_at: Optional[Sequence[int]] = None


def _pallas(body, *, name, grid, in_specs, out_specs, out_shape, args, sem, scratch_shapes=(), comm=None):
    if comm is None:
        res = pl.pallas_call(
            body, name=name, grid=grid, in_specs=list(in_specs), out_specs=list(out_specs),
            out_shape=list(out_shape), scratch_shapes=list(scratch_shapes), compiler_params=_cp(*sem))(*args)
        return tuple(res), ()
    n_in, n_out, n_scr = len(in_specs), len(out_specs), len(scratch_shapes)
    ns, nb, no = len(comm.srcs), len(comm.bufs), len(comm.outs)

    def carrier(*refs):
        p = 0
        main_in = refs[p:p + n_in]
        p += n_in
        c_src = refs[p:p + ns]
        p += ns + nb
        main_out = refs[p:p + n_out]
        p += n_out
        c_buf = refs[p:p + nb]
        p += nb
        c_out = refs[p:p + no]
        p += no
        scr = refs[p:p + n_scr]
        send, recv = refs[p + n_scr], refs[p + n_scr + 1]
        ids = [pl.program_id(a) for a in range(len(grid))]

        def at(steps):
            return functools.reduce(operator.and_, [i == s for i, s in zip(ids, steps)])

        @pl.when(at([0] * len(grid)))
        def _():
            comm.start(c_src, c_buf, c_out, send, recv)

        if comm.mid is not None:
            @pl.when(at(comm.mid_at))
            def _():
                comm.mid(c_src, c_buf, c_out, send, recv)

        body(*main_in, *main_out, *scr)

        @pl.when(at([g - 1 for g in grid]))
        def _():
            comm.finish(c_src, c_buf, c_out, send, recv)

    res = pl.pallas_call(
        carrier, name=name, grid=grid,
        in_specs=list(in_specs) + [ANY] * (ns + nb),
        out_specs=list(out_specs) + [ANY] * (nb + no),
        out_shape=list(out_shape) + [jax.ShapeDtypeStruct(b.shape, b.dtype) for b in comm.bufs] + list(comm.outs),
        input_output_aliases={n_in + ns + j: n_out + j for j in range(nb)},
        scratch_shapes=list(scratch_shapes) + [pltpu.SemaphoreType.DMA((comm.n_sem,)),
                                               pltpu.SemaphoreType.DMA((comm.n_sem,))],
        compiler_params=_cp(*(["arbitrary"] * len(grid))),
    )(*args, *comm.srcs, *comm.bufs)
    return tuple(res[:n_out]), tuple(res[n_out:])


def _comm_call(comm, name):
    ns, nb, no = len(comm.srcs), len(comm.bufs), len(comm.outs)

    def body(*refs):
        c_src = refs[0:ns]
        c_buf = refs[ns + nb:ns + 2 * nb]
        c_out = refs[ns + 2 * nb:ns + 2 * nb + no]
        send, recv = refs[ns + 2 * nb + no], refs[ns + 2 * nb + no + 1]
        comm.start(c_src, c_buf, c_out, send, recv)
        if comm.mid is not None:
            comm.mid(c_src, c_buf, c_out, send, recv)
        comm.finish(c_src, c_buf, c_out, send, recv)

    return tuple(pl.pallas_call(
        body, name=name,
        in_specs=[ANY] * (ns + nb), out_specs=[ANY] * (nb + no),
        out_shape=[jax.ShapeDtypeStruct(b.shape, b.dtype) for b in comm.bufs] + list(comm.outs),
        input_output_aliases={ns + j: j for j in range(nb)},
        scratch_shapes=[pltpu.SemaphoreType.DMA((comm.n_sem,)), pltpu.SemaphoreType.DMA((comm.n_sem,))],
    )(*comm.srcs, *comm.bufs))


def _sigmoid(v):
    return 1.0 / (1.0 + jnp.exp(-v))


def _sigmoid_tanh(v):
    return 0.5 + 0.5 * jnp.tanh(0.5 * v)


def _gelu_parts(v):
    v2 = v * v
    t = jnp.tanh(v * (GELU_C0 + (GELU_C0 * GELU_C1) * v2))
    half = 0.5 + 0.5 * t
    gel = v * half
    dgel = half + (0.5 * v) * (1.0 - t * t) * (GELU_C0 + (3.0 * GELU_C0 * GELU_C1) * v2)
    return gel, dgel


def _gelu(v):
    t = jnp.tanh(v * (GELU_C0 + (GELU_C0 * GELU_C1) * (v * v)))
    return v * (0.5 + 0.5 * t)


def _neg_expm1(y, a):
    p = jnp.full_like(y, 1.0 / 120.0)
    for coef in (1.0 / 24.0, 1.0 / 6.0, 0.5, 1.0):
        p = p * y + coef
    return jnp.where(y > -0.1, -(p * y), 1.0 - a * a)


def _softplus_neg(lam):
    nl = -lam
    e = jnp.exp(-jnp.abs(nl))
    u = 1.0 + e
    l1p = jnp.where(u == 1.0, e, jnp.log(u) * e / (u - 1.0))
    return jnp.maximum(nl, 0.0) + l1p


def _conv_taps(ext, taps, n_out):
    kw = len(taps)
    acc = taps[kw - 1] * ext[SUBLANES:SUBLANES + n_out]
    for k in range(kw - 1):
        acc = acc + taps[k] * pltpu.roll(ext, kw - 1 - k, axis=0)[SUBLANES:SUBLANES + n_out]
    return acc


def _conv_taps_t(ext, taps, n_out):
    kw = len(taps)
    n = ext.shape[0]
    acc = taps[kw - 1] * ext[0:n_out]
    for k in range(kw - 1):
        acc = acc + taps[k] * pltpu.roll(ext, n - (kw - 1 - k), axis=0)[0:n_out]
    return acc


def _scan8(a, b, carry, row):
    for s in (1, 2, 4):
        m = row >= s
        a_sh = jnp.where(m, pltpu.roll(a, s, axis=0), 1.0)
        b_sh = jnp.where(m, pltpu.roll(b, s, axis=0), 0.0)
        b = a * b_sh + b
        a = a * a_sh
    return a * carry + b


def _scan8_rev(a, b, carry, row):
    for s in (1, 2, 4):
        m = row < SUBLANES - s
        a_sh = jnp.where(m, pltpu.roll(a, SUBLANES - s, axis=0), 1.0)
        b_sh = jnp.where(m, pltpu.roll(b, SUBLANES - s, axis=0), 0.0)
        b = a * b_sh + b
        a = a * a_sh
    return a * carry + b


def _cast_into_full(w, kind, idx, name):
    nl, r, c = w.shape
    tr = 256 if r % 256 == 0 else r
    nrb = r // tr

    def body(idx_ref, w_ref, o0_ref, o1_ref):
        o0_ref[...] = w_ref[0].astype(BF16)
        o1_ref[...] = w_ref[1].astype(BF16)

    if COL_SHARDED[kind]:
        full = (r, N_CHIP * c)
        o_spec = pl.BlockSpec((tr, c), lambda i, idx_ref: (i, idx_ref[1]))
    else:
        full = (N_CHIP * r, c)
        o_spec = pl.BlockSpec((tr, c), lambda i, idx_ref: (idx_ref[1] * nrb + i, 0))
    return pl.pallas_call(
        body, name=name,
        grid_spec=pltpu.PrefetchScalarGridSpec(
            num_scalar_prefetch=1, grid=(nrb,),
            in_specs=[pl.BlockSpec((nl, tr, c), lambda i, idx_ref: (0, i, 0))], out_specs=[o_spec, o_spec]),
        out_shape=[jax.ShapeDtypeStruct(full, BF16)] * 2,
        compiler_params=_cp("parallel"),
    )(idx, w)


def _norm_mm(x, g, w, name, planes=False, tm=512, tn=512, comm=None):
    t_len, d = x.shape
    n = w.shape[1]
    half = n // 2

    def body(x_ref, g_ref, w_ref, z_ref, h_ref):
        xv = x_ref[...]
        r = lax.rsqrt(jnp.mean(xv * xv, axis=-1, keepdims=True) + EPS)
        h_ref[...] = ((xv * r) * g_ref[...]).astype(BF16)
        for n0 in range(0, n, tn):
            blk = jnp.dot(h_ref[...], w_ref[:, n0:n0 + tn], preferred_element_type=F32).astype(BF16)
            if planes:
                z_ref[n0 // half, :, n0 % half:n0 % half + tn] = blk
            else:
                z_ref[:, n0:n0 + tn] = blk

    if planes:
        z_shape = jax.ShapeDtypeStruct((2, t_len, half), BF16)
        z_spec = pl.BlockSpec((2, tm, half), lambda i: (0, i, 0))
    else:
        z_shape = jax.ShapeDtypeStruct((t_len, n), BF16)
        z_spec = pl.BlockSpec((tm, n), lambda i: (i, 0))
    return _pallas(
        body, name=name, grid=(t_len // tm,),
        in_specs=[pl.BlockSpec((tm, d), lambda i: (i, 0)),
                  pl.BlockSpec((1, d), lambda i: (0, 0)),
                  pl.BlockSpec((d, n), lambda i: (0, 0), pipeline_mode=ONCE)],
        out_specs=[z_spec, pl.BlockSpec((tm, d), lambda i: (i, 0))],
        out_shape=[z_shape, jax.ShapeDtypeStruct((t_len, d), BF16)],
        args=(x, g, w), sem=("parallel",), comm=comm)


def _mm_res(a, w, res, name, tm=512, comm=None):
    t_len, k = a.shape
    n = w.shape[1]

    def body(a_ref, w_ref, r_ref, o_ref):
        o_ref[...] = r_ref[...] + jnp.dot(a_ref[...], w_ref[...], preferred_element_type=F32)

    return _pallas(
        body, name=name, grid=(t_len // tm,),
        in_specs=[pl.BlockSpec((tm, k), lambda i: (i, 0)),
                  pl.BlockSpec((k, n), lambda i: (0, 0), pipeline_mode=ONCE),
                  pl.BlockSpec((tm, n), lambda i: (i, 0))],
        out_specs=[pl.BlockSpec((tm, n), lambda i: (i, 0))],
        out_shape=[jax.ShapeDtypeStruct((t_len, n), F32)],
        args=(a, w, res), sem=("parallel",), comm=comm)


def _mm_nt(a, w, name, tm=512, comm=None):
    t_len, k = a.shape
    n = w.shape[0]

    def body(a_ref, w_ref, o_ref):
        o_ref[...] = lax.dot_general(a_ref[...], w_ref[...], (((1,), (1,)), ((), ())),
                                     preferred_element_type=F32).astype(BF16)

    return _pallas(
        body, name=name, grid=(t_len // tm,),
        in_specs=[pl.BlockSpec((tm, k), lambda i: (i, 0)),
                  pl.BlockSpec((n, k), lambda i: (0, 0), pipeline_mode=ONCE)],
        out_specs=[pl.BlockSpec((tm, n), lambda i: (i, 0))],
        out_shape=[jax.ShapeDtypeStruct((t_len, n), BF16)],
        args=(a, w), sem=("parallel",), comm=comm)


def _mm_nt_normbwd(dz, w, x, g, dres, name, planes=False, tm=512, comm=None):
    t_len, d = x.shape
    n = w.shape[1]
    half = n // 2
    nt_dims = (((1,), (1,)), ((), ()))

    def body(dz_ref, w_ref, x_ref, g_ref, r_ref, dx_ref, dxb_ref, dg_ref):
        @pl.when(pl.program_id(0) == 0)
        def _():
            dg_ref[...] = jnp.zeros_like(dg_ref)

        if planes:
            dh = (lax.dot_general(dz_ref[0], w_ref[:, 0:half], nt_dims, preferred_element_type=F32)
                  + lax.dot_general(dz_ref[1], w_ref[:, half:], nt_dims, preferred_element_type=F32))
        else:
            dh = lax.dot_general(dz_ref[...], w_ref[...], nt_dims, preferred_element_type=F32)
        xv = x_ref[...]
        r = lax.rsqrt(jnp.mean(xv * xv, axis=-1, keepdims=True) + EPS)
        xh = xv * r
        dhg = dh * g_ref[...]
        dx = r_ref[...] + r * (dhg - xh * jnp.mean(dhg * xh, axis=-1, keepdims=True))
        dx_ref[...] = dx
        dxb_ref[...] = dx.astype(BF16)
        dg_ref[0:1, :] += jnp.sum(dh * xh, axis=0, keepdims=True)

    if planes:
        dz_spec = pl.BlockSpec((2, tm, half), lambda i: (0, i, 0))
    else:
        dz_spec = pl.BlockSpec((tm, n), lambda i: (i, 0))
    return _pallas(
        body, name=name, grid=(t_len // tm,),
        in_specs=[dz_spec,
                  pl.BlockSpec((d, n), lambda i: (0, 0), pipeline_mode=ONCE),
                  pl.BlockSpec((tm, d), lambda i: (i, 0)),
                  pl.BlockSpec((1, d), lambda i: (0, 0)),
                  pl.BlockSpec((tm, d), lambda i: (i, 0))],
        out_specs=[pl.BlockSpec((tm, d), lambda i: (i, 0)),
                   pl.BlockSpec((tm, d), lambda i: (i, 0)),
                   pl.BlockSpec((SUBLANES, d), lambda i: (0, 0))],
        out_shape=[jax.ShapeDtypeStruct((t_len, d), F32),
                   jax.ShapeDtypeStruct((t_len, d), BF16),
                   jax.ShapeDtypeStruct((SUBLANES, d), F32)],
        args=(dz, w, x, g, dres), sem=("arbitrary",), comm=comm)


def _mm_tn(a, g, name, tk, tn, planes=False, tt=1024, comm=None):
    t_len, k = a.shape
    n = 2 * g.shape[2] if planes else g.shape[1]
    nn = n // tn
    half = nn // 2
    tt = min(tt, t_len)

    def body(a_ref, g_ref, o_ref):
        @pl.when(pl.program_id(2) == 0)
        def _():
            o_ref[...] = jnp.zeros_like(o_ref)

        o_ref[...] += lax.dot_general(a_ref[...], g_ref[...], (((0,), (0,)), ((), ())),
                                      preferred_element_type=F32)

    if planes:
        g_spec = pl.BlockSpec((None, tt, tn), lambda i, j, t: (j // half, t, j % half))
    else:
        g_spec = pl.BlockSpec((tt, tn), lambda i, j, t: (t, j))
    return _pallas(
        body, name=name, grid=(k // tk, nn, t_len // tt),
        in_specs=[pl.BlockSpec((tt, tk), lambda i, j, t: (t, i)), g_spec],
        out_specs=[pl.BlockSpec((tk, tn), lambda i, j, t: (i, j))],
        out_shape=[jax.ShapeDtypeStruct((k, n), F32)],
        args=(a, g), sem=("parallel", "parallel", "arbitrary"), comm=comm)


def _lru_gates(rp, ip, spn):
    r = _sigmoid(rp)
    i = _sigmoid_tanh(ip)
    la = r * spn
    a = jnp.exp(la)
    mult = jnp.sqrt(_neg_expm1(2.0 * la, a))
    return r, i, a, mult


def _mixer_fwd(z, prm, gates, layer, name, tb=256, comm=None):
    t_len = z.shape[0]

    def body(z_ref, p_ref, g_ref, y_ref, h_ref, xhalo, phalo, hcar, lx_s, rp_s, ip_s):
        @pl.when(pl.program_id(0) == 0)
        def _():
            xhalo[...] = jnp.zeros_like(xhalo)
            phalo[...] = jnp.zeros_like(phalo)
            hcar[...] = jnp.zeros_like(hcar)

        prm_v = p_ref[...]
        cw = prm_v[0:4]
        vec = prm_v[4:8]
        xp = z_ref[:, 0:D_LRU].astype(F32)
        ext = jnp.concatenate([xhalo[...], xp], axis=0)
        lx = vec[0:1] + _conv_taps(ext, [cw[k:k + 1] for k in range(4)], tb)
        xhalo[...] = xp[tb - SUBLANES:]
        lx_s[...] = lx
        lxb = lx.astype(BF16)
        for q in range(4):
            sl = slice(q * 256, (q + 1) * 256)
            rp_s[:, sl] = jnp.dot(lxb[:, sl], g_ref[q], preferred_element_type=F32) + vec[1:2, sl]
            ip_s[:, sl] = jnp.dot(lxb[:, sl], g_ref[4 + q], preferred_element_type=F32) + vec[2:3, sl]

        spn = jnp.broadcast_to(-RG_C * _softplus_neg(vec[3:4]), (SUBLANES, D_LRU))
        row = lax.broadcasted_iota(jnp.int32, (SUBLANES, D_LRU), 0)

        def step(ci, carry):
            o = pl.multiple_of(ci * PACKED, PACKED)
            gate = z_ref[pl.ds(o, PACKED), D_LRU:2 * D_LRU].astype(F32)
            ys = []
            for sub in range(2):
                rows = pl.ds(pl.multiple_of(o + sub * SUBLANES, SUBLANES), SUBLANES)
                lxv = lx_s[rows, :]
                _, i, a, mult = _lru_gates(rp_s[rows, :], ip_s[rows, :], spn)
                h = _scan8(a, mult * (i * lxv), carry, row)
                h_ref[rows, :] = h
                ys.append(h * _gelu(gate[sub * SUBLANES:(sub + 1) * SUBLANES]))
                carry = jnp.broadcast_to(h[SUBLANES - 1:SUBLANES, :], (SUBLANES, D_LRU))
            y_ref[pl.ds(o, PACKED), 0:D_LRU] = jnp.concatenate(ys, axis=0).astype(BF16)
            return carry

        hcar[...] = lax.fori_loop(0, tb // PACKED, step, hcar[...])

        scw = prm_v[8:11, 0:D_SC]
        o_b, o_c, o_x = 2 * D_LRU, 2 * D_LRU + D_SC, 2 * D_LRU + 2 * D_SC
        p = z_ref[:, o_c:o_x].astype(F32) * z_ref[:, o_x:].astype(F32)
        pext = jnp.concatenate([phalo[...], p], axis=0)
        q = _conv_taps(pext, [scw[k:k + 1] for k in range(3)], tb)
        phalo[...] = p[tb - SUBLANES:]
        y_ref[:, D_LRU:] = (z_ref[:, o_b:o_c].astype(F32) * q).astype(BF16)

    return _pallas(
        body, name=name, grid=(t_len // tb,),
        in_specs=[pl.BlockSpec((tb, D_IN), lambda t: (t, 0)),
                  pl.BlockSpec((None, 2 * SUBLANES, D_LRU), lambda t: (layer, 0, 0)),
                  pl.BlockSpec((None, 8, 256, 256), lambda t: (layer, 0, 0, 0))],
        out_specs=[pl.BlockSpec((tb, D_MIX), lambda t: (t, 0)),
                   pl.BlockSpec((tb, D_LRU), lambda t: (t, 0))],
        out_shape=[jax.ShapeDtypeStruct((t_len, D_MIX), BF16),
                   jax.ShapeDtypeStruct((t_len, D_LRU), F32)],
        scratch_shapes=[pltpu.VMEM((SUBLANES, D_LRU), F32), pltpu.VMEM((SUBLANES, D_SC), F32),
                        pltpu.VMEM((SUBLANES, D_LRU), F32), pltpu.VMEM((tb, D_LRU), F32),
                        pltpu.VMEM((tb, D_LRU), F32), pltpu.VMEM((tb, D_LRU), F32)],
        args=(z, prm, gates), sem=("arbitrary",), comm=comm)


def _mixer_bwd(z, h, dy, prm, gates, layer, name, tb=256, comm=None):
    t_len = z.shape[0]
    nb = t_len // tb

    def body(z_ref, zh_ref, h_ref, hh_ref, dy_ref, p_ref, g_ref, dz_ref, dp_ref, dg_ref,
             lx_s, rp_s, ip_s, drpb_s, dipb_s, dlx_s, hext_s, acc_s, acar, gcar, dqh):
        t = pl.program_id(0)
        first_block = t == nb - 1

        @pl.when(t == 0)
        def _():
            for ref in (dp_ref, dg_ref, acc_s, acar, gcar, dqh):
                ref[...] = jnp.zeros_like(ref)
            dlx_s[tb:, :] = jnp.zeros((SUBLANES, D_LRU), F32)

        prm_v = p_ref[...]
        cw = prm_v[0:4]
        vec = prm_v[4:8]
        scw = prm_v[8:11, 0:D_SC]
        wa_ref = [g_ref.at[q] for q in range(4)]
        wx_ref = [g_ref.at[4 + q] for q in range(4)]
        dwa_ref = [dg_ref.at[q] for q in range(4)]
        dwx_ref = [dg_ref.at[4 + q] for q in range(4)]
        ctaps = [cw[k:k + 1] for k in range(4)]
        staps = [scw[k:k + 1] for k in range(3)]
        keep = jnp.where(first_block, 0.0, 1.0)
        zh = zh_ref[...].astype(F32)[PACKED - SUBLANES:] * keep

        xp = z_ref[:, 0:D_LRU].astype(F32)
        xext = jnp.concatenate([zh[:, 0:D_LRU], xp], axis=0)
        lx = vec[0:1] + _conv_taps(xext, ctaps, tb)
        lx_s[...] = lx
        lxb = lx.astype(BF16)
        for q in range(4):
            sl = slice(q * 256, (q + 1) * 256)
            rp_s[:, sl] = jnp.dot(lxb[:, sl], wa_ref[q][...], preferred_element_type=F32) + vec[1:2, sl]
            ip_s[:, sl] = jnp.dot(lxb[:, sl], wx_ref[q][...], preferred_element_type=F32) + vec[2:3, sl]
        hext_s[0:SUBLANES, :] = hh_ref[...] * keep
        hext_s[SUBLANES:, :] = h_ref[...]

        spn = jnp.broadcast_to(-RG_C * _softplus_neg(vec[3:4]), (SUBLANES, D_LRU))
        row = lax.broadcasted_iota(jnp.int32, (SUBLANES, D_LRU), 0)

        def step(ci, carry):
            a_next, g_next = carry
            o = pl.multiple_of((tb // PACKED - 1 - ci) * PACKED, PACKED)
            rows16 = pl.ds(o, PACKED)
            gate16 = z_ref[rows16, D_LRU:2 * D_LRU].astype(F32)
            dyl16 = dy_ref[rows16, 0:D_LRU].astype(F32)
            dgs, drs, dis = [None, None], [None, None], [None, None]
            for sub in (1, 0):
                oo = pl.multiple_of(o + sub * SUBLANES, SUBLANES)
                rows = pl.ds(oo, SUBLANES)
                half = slice(sub * SUBLANES, (sub + 1) * SUBLANES)
                lxv = lx_s[rows, :]
                r, i, a, mult = _lru_gates(rp_s[rows, :], ip_s[rows, :], spn)
                hwin = hext_s[pl.ds(oo, 2 * SUBLANES), :]
                hv = hwin[SUBLANES:]
                hprev = pltpu.roll(hwin, 1, axis=0)[SUBLANES:]
                gel, dgel = _gelu_parts(gate16[half])
                dyl = dyl16[half]
                a_up = jnp.where(row < SUBLANES - 1, pltpu.roll(a, SUBLANES - 1, axis=0), a_next)
                gg = _scan8_rev(a_up, dyl * gel, g_next, row)
                dgs[sub] = dyl * hv * dgel
                ilx = i * lxv
                dla = a * (gg * hprev - (gg * ilx) * a / mult)
                dlx_s[rows, :] = gg * mult * i
                drp = dla * spn * r * (1.0 - r)
                dip = gg * mult * lxv * i * (1.0 - i)
                drs[sub] = drp
                dis[sub] = dip
                acc_s[0] += drp
                acc_s[1] += dip
                acc_s[2] += dla * r
                a_next = jnp.broadcast_to(a[0:1, :], (SUBLANES, D_LRU))
                g_next = jnp.broadcast_to(gg[0:1, :], (SUBLANES, D_LRU))
            dz_ref[rows16, D_LRU:2 * D_LRU] = jnp.concatenate(dgs, axis=0).astype(BF16)
            drpb_s[rows16, :] = jnp.concatenate(drs, axis=0).astype(BF16)
            dipb_s[rows16, :] = jnp.concatenate(dis, axis=0).astype(BF16)
            return a_next, g_next

        a_c, g_c = lax.fori_loop(0, tb // PACKED, step, (acar[...], gcar[...]))
        acar[...] = a_c
        gcar[...] = g_c

        drpb = drpb_s[...]
        dipb = dipb_s[...]
        nt_dims = (((1,), (1,)), ((), ()))
        tn_dims = (((0,), (0,)), ((), ()))
        for q in range(4):
            sl = slice(q * 256, (q + 1) * 256)
            dlx_s[0:tb, sl] += (
                lax.dot_general(drpb[:, sl], wa_ref[q][...], nt_dims, preferred_element_type=F32)
                + lax.dot_general(dipb[:, sl], wx_ref[q][...], nt_dims, preferred_element_type=F32))
            dwa_ref[q][...] += lax.dot_general(lxb[:, sl], drpb[:, sl], tn_dims, preferred_element_type=F32)
            dwx_ref[q][...] += lax.dot_general(lxb[:, sl], dipb[:, sl], tn_dims, preferred_element_type=F32)

        dlx_ext = dlx_s[...]
        dlx = dlx_ext[0:tb]
        dz_ref[:, 0:D_LRU] = _conv_taps_t(dlx_ext, ctaps, tb).astype(BF16)
        dp_ref[3:4, :] += jnp.sum(dlx * xp, axis=0, keepdims=True)
        for k in range(3):
            shifted = pltpu.roll(xext, 3 - k, axis=0)[SUBLANES:]
            dp_ref[k:k + 1, :] += jnp.sum(dlx * shifted, axis=0, keepdims=True)
        dp_ref[4:5, :] += jnp.sum(dlx, axis=0, keepdims=True)
        dlx_s[tb:, :] = dlx[0:SUBLANES]

        o_b, o_c, o_x = 2 * D_LRU, 2 * D_LRU + D_SC, 2 * D_LRU + 2 * D_SC
        sb = z_ref[:, o_b:o_c].astype(F32)
        scc = z_ref[:, o_c:o_x].astype(F32)
        sx = z_ref[:, o_x:].astype(F32)
        p = scc * sx
        pext = jnp.concatenate([zh[:, o_c:o_x] * zh[:, o_x:], p], axis=0)
        q = _conv_taps(pext, staps, tb)
        dys = dy_ref[:, D_LRU:].astype(F32)
        dq = dys * sb
        dp = _conv_taps_t(jnp.concatenate([dq, dqh[...]], axis=0), staps, tb)
        dp_ref[10:11, 0:D_SC] += jnp.sum(dq * p, axis=0, keepdims=True)
        for k in range(2):
            shifted = pltpu.roll(pext, 2 - k, axis=0)[SUBLANES:]
            dp_ref[8 + k:9 + k, 0:D_SC] += jnp.sum(dq * shifted, axis=0, keepdims=True)
        dqh[...] = dq[0:SUBLANES]
        dz_ref[:, o_b:o_c] = (dys * q).astype(BF16)
        dz_ref[:, o_c:o_x] = (dp * sx).astype(BF16)
        dz_ref[:, o_x:] = (dp * scc).astype(BF16)

        @pl.when(first_block)
        def _():
            dp_ref[5:6, :] = jnp.sum(acc_s[0], axis=0, keepdims=True)
            dp_ref[6:7, :] = jnp.sum(acc_s[1], axis=0, keepdims=True)
            dp_ref[7:8, :] = (jnp.sum(acc_s[2], axis=0, keepdims=True) * RG_C * _sigmoid(-vec[3:4]))

    blk = lambda t: (nb - 1 - t, 0)
    halo8 = lambda t: (jnp.maximum((nb - 1 - t) * (tb // SUBLANES) - 1, 0), 0)
    halo16 = lambda t: (jnp.maximum((nb - 1 - t) * (tb // PACKED) - 1, 0), 0)
    return _pallas(
        body, name=name, grid=(nb,),
        in_specs=[pl.BlockSpec((tb, D_IN), blk), pl.BlockSpec((PACKED, D_IN), halo16),
                  pl.BlockSpec((tb, D_LRU), blk), pl.BlockSpec((SUBLANES, D_LRU), halo8),
                  pl.BlockSpec((tb, D_MIX), blk),
                  pl.BlockSpec((None, 2 * SUBLANES, D_LRU), lambda t: (layer, 0, 0)),
                  pl.BlockSpec((None, 8, 256, 256), lambda t: (layer, 0, 0, 0))],
        out_specs=[pl.BlockSpec((tb, D_IN), blk),
                   pl.BlockSpec((2 * SUBLANES, D_LRU), lambda t: (0, 0)),
                   pl.BlockSpec((8, 256, 256), lambda t: (0, 0, 0))],
        out_shape=[jax.ShapeDtypeStruct((t_len, D_IN), BF16),
                   jax.ShapeDtypeStruct((2 * SUBLANES, D_LRU), F32),
                   jax.ShapeDtypeStruct((8, 256, 256), F32)],
        scratch_shapes=[pltpu.VMEM((tb, D_LRU), F32),
                        pltpu.VMEM((tb, D_LRU), F32), pltpu.VMEM((tb, D_LRU), F32),
                        pltpu.VMEM((tb, D_LRU), BF16), pltpu.VMEM((tb, D_LRU), BF16),
                        pltpu.VMEM((tb + SUBLANES, D_LRU), F32), pltpu.VMEM((tb + SUBLANES, D_LRU), F32),
                        pltpu.VMEM((3, SUBLANES, D_LRU), F32),
                        pltpu.VMEM((SUBLANES, D_LRU), F32), pltpu.VMEM((SUBLANES, D_LRU), F32),
                        pltpu.VMEM((SUBLANES, D_SC), F32)],
        args=(z, z, h, h, dy, prm, gates), sem=("arbitrary",), comm=comm)


def _ffn_act(u, fw, layer, name, tb=512, tn=1536, rc=128, comm=None):
    t_len = u.shape[1]
    hb = tb // PACKED

    def body(u_ref, uh_ref, fw_ref, o_ref, fg_ref, fu_ref, ext):
        keep = jnp.where(pl.program_id(0) == 0, 0.0, 1.0)
        ext[:, 0:SUBLANES, :] = uh_ref[...].astype(F32)[:, PACKED - SUBLANES:, :] * keep
        ext[:, SUBLANES:, :] = u_ref[...].astype(F32)
        fw_v = fw_ref[...]

        for lb in range(tn // LANES):
            lanes = slice(lb * LANES, (lb + 1) * LANES)
            wg = [fw_v[0, k:k + 1, lanes] for k in range(3)]
            wu = [fw_v[1, k:k + 1, lanes] for k in range(3)]

            def chunk(ci, c, lanes=lanes, wg=wg, wu=wu):
                o = pl.multiple_of(ci * rc, rc)
                win = pl.ds(o, rc + SUBLANES)
                gate = _conv_taps(ext[0, win, lanes], wg, rc)
                up = _conv_taps(ext[1, win, lanes], wu, rc)
                gel, dgel = _gelu_parts(gate)
                rows = pl.ds(o, rc)
                o_ref[rows, lanes] = (gel * up).astype(BF16)
                fg_ref[rows, lanes] = (up * dgel).astype(BF16)
                fu_ref[rows, lanes] = gel.astype(BF16)
                return c

            lax.fori_loop(0, tb // rc, chunk, 0)

    spec = pl.BlockSpec((tb, tn), lambda i, j: (i, j))
    shape = jax.ShapeDtypeStruct((t_len, D_FF), BF16)
    return _pallas(
        body, name=name, grid=(t_len // tb, D_FF // tn),
        in_specs=[pl.BlockSpec((2, tb, tn), lambda i, j: (0, i, j)),
                  pl.BlockSpec((2, PACKED, tn), lambda i, j: (0, jnp.maximum(i * hb - 1, 0), j)),
                  pl.BlockSpec((None, 2, SUBLANES, tn), lambda i, j: (layer, 0, 0, j))],
        out_specs=[spec] * 3, out_shape=[shape] * 3,
        scratch_shapes=[pltpu.VMEM((2, tb + SUBLANES, tn), F32)],
        args=(u, u, fw), sem=("parallel", "parallel"), comm=comm)


def _ffn_bwd(dact, fg, fu, u, fw, layer, name, tb=512, tn=1536, rc=64, comm=None):
    t_len = u.shape[1]
    ni = t_len // tb
    hb = tb // PACKED
    last_halo = t_len // PACKED - 1

    def body(d_ref, dn_ref, fg_ref, fgn_ref, fu_ref, fun_ref, u_ref, up_ref, fw_ref, du_ref, dfw_ref,
             extu, extp, acc):
        i = pl.program_id(1)

        @pl.when(i == 0)
        def _():
            acc[...] = jnp.zeros_like(acc)

        keep_prev = jnp.where(i == 0, 0.0, 1.0)
        keep_next = jnp.where(i == ni - 1, 0.0, 1.0)
        extu[:, 0:SUBLANES, :] = up_ref[...].astype(F32)[:, PACKED - SUBLANES:, :] * keep_prev
        extu[:, SUBLANES:, :] = u_ref[...].astype(F32)
        dv = d_ref[...].astype(F32)
        dn = dn_ref[...].astype(F32)[0:SUBLANES] * keep_next
        extp[0, 0:tb, :] = dv * fg_ref[...].astype(F32)
        extp[0, tb:, :] = dn * fgn_ref[...].astype(F32)[0:SUBLANES]
        extp[1, 0:tb, :] = dv * fu_ref[...].astype(F32)
        extp[1, tb:, :] = dn * fun_ref[...].astype(F32)[0:SUBLANES]
        fw_v = fw_ref[...]
        m = rc + SUBLANES

        for lb in range(tn // LANES):
            lanes = slice(lb * LANES, (lb + 1) * LANES)
            taps = [[fw_v[pln, k:k + 1, lanes] for k in range(3)] for pln in range(2)]

            def chunk(ci, c, lanes=lanes, taps=taps):
                o = pl.multiple_of(ci * rc, rc)
                for pln in range(2):
                    e = extu[pln, pl.ds(o, m), lanes]
                    sh = [pltpu.roll(e, 2, axis=0)[SUBLANES:], pltpu.roll(e, 1, axis=0)[SUBLANES:], e[SUBLANES:]]
                    dpost = extp[pln, pl.ds(o, m), lanes]
                    du_ref[pln, pl.ds(o, rc), lanes] = _conv_taps_t(dpost, taps[pln], rc).astype(BF16)
                    for k in range(3):
                        prod = dpost[0:rc] * sh[k]
                        acc[3 * pln + k, :, lanes] += sum(
                            prod[s:s + SUBLANES] for s in range(0, rc, SUBLANES))
                return c

            lax.fori_loop(0, tb // rc, chunk, 0)

        @pl.when(i == ni - 1)
        def _():
            dfw_ref[...] = jnp.zeros_like(dfw_ref)
            for pln in range(2):
                for k in range(3):
                    dfw_ref[pln, k:k + 1, :] = jnp.sum(acc[3 * pln + k], axis=0, keepdims=True)

    main = pl.BlockSpec((tb, tn), lambda j, i: (i, j))
    nxt = pl.BlockSpec((PACKED, tn), lambda j, i: (jnp.minimum((i + 1) * hb, last_halo), j))
    return _pallas(
        body, name=name, grid=(D_FF // tn, ni),
        in_specs=[main, nxt, main, nxt, main, nxt,
                  pl.BlockSpec((2, tb, tn), lambda j, i: (0, i, j)),
                  pl.BlockSpec((2, PACKED, tn), lambda j, i: (0, jnp.maximum(i * hb - 1, 0), j)),
                  pl.BlockSpec((None, 2, SUBLANES, tn), lambda j, i: (layer, 0, 0, j))],
        out_specs=[pl.BlockSpec((2, tb, tn), lambda j, i: (0, i, j)),
                   pl.BlockSpec((2, SUBLANES, tn), lambda j, i: (0, 0, j))],
        out_shape=[jax.ShapeDtypeStruct((2, t_len, D_FF), BF16),
                   jax.ShapeDtypeStruct((2, SUBLANES, D_FF), F32)],
        scratch_shapes=[pltpu.VMEM((2, tb + SUBLANES, tn), F32),
                        pltpu.VMEM((2, tb + SUBLANES, tn), F32),
                        pltpu.VMEM((6, SUBLANES, tn), F32)],
        args=(dact, dact, fg, fg, fu, fu, u, u, fw), sem=("parallel", "arbitrary"), comm=comm)


def _loss_head(x, g, target, name, tb=256):
    t_len, d = x.shape

    def body(x_ref, g_ref, t_ref, dx_ref, dxb_ref, dg_ref, loss_ref):
        @pl.when(pl.program_id(0) == 0)
        def _():
            dg_ref[...] = jnp.zeros_like(dg_ref)
            loss_ref[...] = jnp.zeros_like(loss_ref)

        xv = x_ref[...]
        gv = g_ref[...]
        r = lax.rsqrt(jnp.mean(xv * xv, axis=-1, keepdims=True) + EPS)
        xh = xv * r
        err = xh * gv - t_ref[...]
        loss_ref[...] += (0.5 / d) * jnp.sum(jnp.sum(err * err, axis=-1, keepdims=True), axis=0, keepdims=True)
        dy = err * (1.0 / d)
        dyg = dy * gv
        dx = r * (dyg - xh * jnp.mean(dyg * xh, axis=-1, keepdims=True))
        dx_ref[...] = dx
        dxb_ref[...] = dx.astype(BF16)
        dg_ref[0:1, :] += jnp.sum(dy * xh, axis=0, keepdims=True)

    return _pallas(
        body, name=name, grid=(t_len // tb,),
        in_specs=[pl.BlockSpec((tb, d), lambda i: (i, 0)), pl.BlockSpec((1, d), lambda i: (0, 0)),
                  pl.BlockSpec((tb, d), lambda i: (i, 0))],
        out_specs=[pl.BlockSpec((tb, d), lambda i: (i, 0)), pl.BlockSpec((tb, d), lambda i: (i, 0)),
                   pl.BlockSpec((SUBLANES, d), lambda i: (0, 0)),
                   pl.BlockSpec((SUBLANES, LANES), lambda i: (0, 0))],
        out_shape=[jax.ShapeDtypeStruct((t_len, d), F32), jax.ShapeDtypeStruct((t_len, d), BF16),
                   jax.ShapeDtypeStruct((SUBLANES, d), F32), jax.ShapeDtypeStruct((SUBLANES, LANES), F32)],
        args=(x, g, target), sem=("arbitrary",))[0]


def _adamw(w, g, m, v, name, emit_grad=False, comm=None):
    r, c = w.shape
    tr = 256 if r % 256 == 0 else r
    c1 = 1.0 / (1.0 - ADAM_B1 ** ADAM_STEP)
    c2 = 1.0 / (1.0 - ADAM_B2 ** ADAM_STEP)

    def body(w_ref, g_ref, m_ref, v_ref, d_ref, mo_ref, vo_ref, *go_ref):
        gv = g_ref[...]
        mn = ADAM_B1 * m_ref[...] + (1.0 - ADAM_B1) * gv
        vn = ADAM_B2 * v_ref[...] + (1.0 - ADAM_B2) * (gv * gv)
        d_ref[...] = -ADAM_LR * ((mn * c1) / (jnp.sqrt(vn * c2) + ADAM_EPS) + ADAM_WD * w_ref[...])
        mo_ref[...] = mn
        vo_ref[...] = vn
        if emit_grad:
            go_ref[0][...] = gv

    spec = pl.BlockSpec((tr, c), lambda i: (i, 0))
    shape = jax.ShapeDtypeStruct((r, c), F32)
    n_out = 4 if emit_grad else 3
    return _pallas(
        body, name=name, grid=(r // tr,),
        in_specs=[spec] * 4, out_specs=[spec] * n_out, out_shape=[shape] * n_out,
        args=(w, g, m, v), sem=("parallel",), comm=comm)


def _place():
    x, y, c = lax.axis_index("x"), lax.axis_index("y"), lax.axis_index("c")
    chips = [(1 - x, y), (x, 1 - y), (1 - x, 1 - y)]
    return x, y, c, chips


def _remote(src, dst, send, recv, sem, to):
    return pltpu.make_async_remote_copy(
        src_ref=src, dst_ref=dst, send_sem=send.at[sem], recv_sem=recv.at[sem], device_id=to, device_id_type=MESH)


def _gather_plan(fulls, kinds, mid_at=None, parts=None):
    parts = parts or [(0, 1)] * len(fulls)

    def region(it, f, k, cc):
        kind = kinds[it]
        p, n = parts[it][0:2]
        count = parts[it][2] if len(parts[it]) > 2 else 1
        if kind == SMALL:
            return f.at[k, pl.ds(cc * (CONV_PACK_ROWS // 2), CONV_PACK_ROWS // 2), :]
        if COL_SHARDED[kind]:
            rows, cols = f.shape[0] // (2 * n), f.shape[1] // N_CHIP
            return f.at[pl.ds((cc * n + p) * rows, count * rows), pl.ds(k * cols, cols)]
        assert n == 1
        rows = f.shape[0] // N_CHIP
        return f.at[pl.ds(k * rows + cc * (rows // 2), rows // 2), :]

    def first_hop(bufs, send, recv, it, j):
        x, y, c, chips = _place()
        reg = region(it, bufs[it], 2 * x + y, c)
        return _remote(reg, reg, send, recv, it * 6 + j, (*chips[j], c))

    def arrival(bufs, send, recv, it, j, second):
        x, y, c, chips = _place()
        px, py = chips[j]
        reg = region(it, bufs[it], 2 * px + py, 1 - c if second else c)
        to = (x, y, 1 - c) if second else (px, py, c)
        return _remote(reg, reg, send, recv, it * 6 + (3 + j if second else j), to)

    def forward(bufs, send, recv, it, j):
        x, y, c, chips = _place()
        px, py = chips[j]
        reg = region(it, bufs[it], 2 * px + py, c)
        return _remote(reg, reg, send, recv, it * 6 + 3 + j, (x, y, 1 - c))

    def start(srcs, bufs, outs, send, recv):
        for it in range(len(bufs)):
            for j in range(3):
                first_hop(bufs, send, recv, it, j).start()

    def mid(srcs, bufs, outs, send, recv):
        for it in range(len(bufs)):
            for j in range(3):
                arrival(bufs, send, recv, it, j, False).wait_recv()
                forward(bufs, send, recv, it, j).start()

    def finish(srcs, bufs, outs, send, recv):
        for it in range(len(bufs)):
            for j in range(3):
                arrival(bufs, send, recv, it, j, True).wait_recv()
        for it in range(len(bufs)):
            for j in range(3):
                first_hop(bufs, send, recv, it, j).wait_send()
                forward(bufs, send, recv, it, j).wait_send()

    return Comm(srcs=(), bufs=tuple(fulls), outs=(), n_sem=6 * len(fulls), start=start, mid=mid, finish=finish,
                mid_at=mid_at)


def _half_axis(kind):
    return 0 if kind == SMALL or COL_SHARDED[kind] else 1


def _half2(ref, kind, cc):
    if _half_axis(kind) == 0:
        return ref.at[pl.ds(cc * (ref.shape[0] // 2), ref.shape[0] // 2), :]
    return ref.at[:, pl.ds(cc * (ref.shape[1] // 2), ref.shape[1] // 2)]


def _pair_plan(grads, kinds):
    def land_shape(g, kind):
        s = list(g.shape)
        s[_half_axis(kind)] //= 2
        return jax.ShapeDtypeStruct(tuple(s), F32)

    def copy(srcs, outs, send, recv, it):
        x, y, c, _ = _place()
        return _remote(_half2(srcs[it], kinds[it], 1 - c), outs[it], send, recv, it, (x, y, 1 - c))

    def start(srcs, bufs, outs, send, recv):
        for it in range(len(srcs)):
            copy(srcs, outs, send, recv, it).start()

    def finish(srcs, bufs, outs, send, recv):
        for it in range(len(srcs)):
            copy(srcs, outs, send, recv, it).wait_send()
        for it in range(len(srcs)):
            copy(srcs, outs, send, recv, it).wait_recv()

    return Comm(srcs=tuple(grads), bufs=(), outs=tuple(land_shape(g, k) for g, k in zip(grads, kinds)),
                n_sem=len(grads), start=start, finish=finish)


def _scatter_plan(parts, slots, kinds):
    def piece(s, kind, k):
        if kind == SMALL:
            return s
        if COL_SHARDED[kind]:
            n = s.shape[1] // N_CHIP
            return s.at[:, pl.ds(k * n, n)]
        n = s.shape[0] // N_CHIP
        return s.at[pl.ds(k * n, n), :]

    def outbound(srcs, bufs, send, recv, it, j):
        x, y, c, chips = _place()
        px, py = chips[j]
        return _remote(piece(srcs[it], kinds[it], 2 * px + py), bufs[it].at[2 * x + y], send, recv, it * 3 + j,
                       (px, py, c))

    def inbound(bufs, send, recv, it, j):
        x, y, c, chips = _place()
        px, py = chips[j]
        got = bufs[it].at[2 * px + py]
        return _remote(got, got, send, recv, it * 3 + j, (px, py, c))

    def start(srcs, bufs, outs, send, recv):
        for it in range(len(srcs)):
            for j in range(3):
                outbound(srcs, bufs, send, recv, it, j).start()

    def finish(srcs, bufs, outs, send, recv):
        for it in range(len(srcs)):
            for j in range(3):
                inbound(bufs, send, recv, it, j).wait_recv()
        for it in range(len(srcs)):
            for j in range(3):
                outbound(srcs, bufs, send, recv, it, j).wait_send()

    return Comm(srcs=tuple(parts), bufs=tuple(slots), outs=(), n_sem=3 * len(parts), start=start, finish=finish)


def _share_plan(fulls, kinds, layer):
    def half(f, kind, cc):
        return _half2(f if kind == SMALL else f.at[layer], kind, cc)

    def copy(bufs, send, recv, it, cc):
        x, y, c, _ = _place()
        reg = half(bufs[it], kinds[it], c if cc == "mine" else 1 - c)
        return _remote(reg, reg, send, recv, it, (x, y, 1 - c))

    def start(srcs, bufs, outs, send, recv):
        for it in range(len(bufs)):
            copy(bufs, send, recv, it, "mine").start()

    def finish(srcs, bufs, outs, send, recv):
        for it in range(len(bufs)):
            copy(bufs, send, recv, it, "other").wait_recv()
        for it in range(len(bufs)):
            copy(bufs, send, recv, it, "mine").wait_send()

    return Comm(srcs=(), bufs=tuple(fulls), outs=(), n_sem=len(fulls), start=start, finish=finish)


def _pair_sum(g, land, idx, kind, name):
    odt = F32 if kind == SMALL else BF16
    r, cdim = land.shape

    def body(idx_ref, g_ref, l_ref, p_ref, s_ref):
        v = (g_ref[...] + l_ref[...]).astype(odt)
        p_ref[...] = v
        if kind == SMALL:
            s_ref[...] = v
        else:
            @pl.when(pl.program_id(1 if COL_SHARDED[kind] else 0) == idx_ref[1])
            def _():
                s_ref[...] = v

    if kind == SMALL:
        grid = (1,)
        g_spec = pl.BlockSpec((r, LANES), lambda i, idx_ref: (idx_ref[0], 0))
        spec = pl.BlockSpec((r, LANES), lambda i, idx_ref: (0, 0))
        s_spec = pl.BlockSpec((None, r, LANES), lambda i, idx_ref: (idx_ref[1], 0, 0))
        s_shape = (N_CHIP, r, LANES)
    elif COL_SHARDED[kind]:
        pc, tr = cdim // N_CHIP, 256
        nrb = r // tr
        grid = (nrb, N_CHIP)
        g_spec = pl.BlockSpec((tr, pc), lambda i, k, idx_ref: (idx_ref[0] * nrb + i, k))
        spec = pl.BlockSpec((tr, pc), lambda i, k, idx_ref: (i, k))
        s_spec = pl.BlockSpec((None, tr, pc), lambda i, k, idx_ref: (idx_ref[1], i, 0))
        s_shape = (N_CHIP, r, pc)
    else:
        pr = r // N_CHIP
        grid = (N_CHIP,)
        g_spec = pl.BlockSpec((pr, cdim), lambda k, idx_ref: (k, idx_ref[0]))
        spec = pl.BlockSpec((pr, cdim), lambda k, idx_ref: (k, 0))
        s_spec = pl.BlockSpec((None, pr, cdim), lambda k, idx_ref: (idx_ref[1], 0, 0))
        s_shape = (N_CHIP, pr, cdim)
    return pl.pallas_call(
        body, name=name,
        grid_spec=pltpu.PrefetchScalarGridSpec(
            num_scalar_prefetch=1, grid=grid, in_specs=[g_spec, spec], out_specs=[spec, s_spec]),
        out_shape=[jax.ShapeDtypeStruct(land.shape, odt), jax.ShapeDtypeStruct(s_shape, odt)],
        compiler_params=_cp(*(["arbitrary"] * len(grid))),
    )(idx, g, land)


def _sum_slots(slots, idx, kind, layer, prev, name):
    _, r, cdim = slots.shape

    def body(*refs):
        s_ref, o_ref = refs[1], refs[-1]
        v = s_ref[...].astype(F32)
        o_ref[...] = (v[0] + v[1]) + (v[2] + v[3])

    if kind == SMALL:
        grid = (1,)
        s_spec = pl.BlockSpec((N_CHIP, r, cdim), lambda i, idx_ref: (0, 0, 0))
        o_spec = pl.BlockSpec((r, cdim), lambda i, idx_ref: (idx_ref[0], 0))
        full = (2 * r, cdim)
    else:
        tr = 256 if r % 256 == 0 else 384
        nrb = r // tr
        grid = (nrb,)
        s_spec = pl.BlockSpec((N_CHIP, tr, cdim), lambda i, idx_ref: (0, i, 0))
        if COL_SHARDED[kind]:
            o_spec = pl.BlockSpec((None, tr, cdim), lambda i, idx_ref: (layer, idx_ref[0] * nrb + i, 0))
            full = (2, 2 * r, cdim)
        else:
            o_spec = pl.BlockSpec((None, tr, cdim), lambda i, idx_ref: (layer, i, idx_ref[0]))
            full = (2, r, 2 * cdim)
    in_specs, args, aliases = [s_spec], [idx, slots], {}
    if prev is not None:
        in_specs.append(ANY)
        args.append(prev)
        aliases = {2: 0}
    return pl.pallas_call(
        body, name=name,
        grid_spec=pltpu.PrefetchScalarGridSpec(
            num_scalar_prefetch=1, grid=grid, in_specs=in_specs, out_specs=o_spec),
        out_shape=jax.ShapeDtypeStruct(full, F32),
        input_output_aliases=aliases,
        compiler_params=_cp(*(["parallel"] * len(grid))),
    )(*args)


def _block_diag(w):
    w4 = w.reshape(2, 4, 4, 64, 64)
    eye = jnp.eye(4, dtype=w.dtype)[None, None, :, None, :, None]
    return (w4[:, :, :, :, None, :] * eye).reshape(2, 4, 256, 256)


def _block_diag_extract(d):
    d5 = d.reshape(4, 4, 64, 4, 64)
    return jnp.stack([d5[:, hh, :, hh, :] for hh in range(4)], axis=1).reshape(-1)


REP_NAMES = ("norm1_g", "lru_conv_b", "lru_ba", "lru_bx", "lru_lambda", "norm2_g", "lru_wa", "lru_wx")


def _pack_rep(norm1_g, conv_b, ba, bx, lam, norm2_g, wa, wx, final_g):
    parts = [a.reshape(-1) for a in (norm1_g, conv_b, ba, bx, lam, norm2_g, wa, wx, final_g)]
    return jnp.concatenate(parts).reshape(REP_ROWS, LANES)


def _unpack_rep(buf):
    flat = buf.reshape(-1)
    res, o = {}, 0
    for k in REP_NAMES:
        shape = (2, 16, 64, 64) if k in ("lru_wa", "lru_wx") else (2, 1024)
        n = math.prod(shape)
        res[k] = flat[o:o + n].reshape(shape)
        o += n
    res["final_g"] = flat[o:o + 1024]
    return res


def _pack_conv_shard(lru_cw, sc_cw, ffn_cw):
    return jnp.concatenate([lru_cw.reshape(16, LANES), jnp.pad(sc_cw.reshape(6, LANES), ((0, 2), (0, 0))),
                            ffn_cw.reshape(72, LANES)], axis=0)


def _unpack_conv_shard(buf):
    return (buf[0:16].reshape(2, 4, 256), buf[16:22].reshape(2, 3, 128), buf[24:96].reshape(2, 3, 1536))


def kernel(x, norm1_g, w_in, lru_conv_w, lru_conv_b, lru_wa, lru_ba, lru_wx, lru_bx, lru_lambda, sc_conv_w, w_out, norm2_g, w_up, ffn_conv_w, w_down, final_g, loss_target, m_norm1_g, m_w_in, m_lru_conv_w, m_lru_conv_b, m_lru_wa, m_lru_ba, m_lru_wx, m_lru_bx, m_lru_lambda, m_sc_conv_w, m_w_out, m_norm2_g, m_w_up, m_ffn_conv_w, m_w_down, m_final_g, v_norm1_g, v_w_in, v_lru_conv_w, v_lru_conv_b, v_lru_wa, v_lru_ba, v_lru_wx, v_lru_bx, v_lru_lambda, v_sc_conv_w, v_w_out, v_norm2_g, v_w_up, v_ffn_conv_w, v_w_down, v_final_g):
    me = 2 * lax.axis_index("x") + lax.axis_index("y")
    idx = jnp.stack([lax.axis_index("c"), me]).astype(jnp.int32)
    t_len = x.shape[1]

    s_conv = _pack_conv_shard(lru_conv_w, sc_conv_w, ffn_conv_w)
    conv_slots = lax.dynamic_update_slice(jnp.zeros((N_CHIP, CONV_PACK_ROWS, LANES), F32), s_conv[None], (me, 0, 0))
    wi = list(_cast_into_full(w_in, W_IN, idx, "cast_w_in"))
    wo = list(_cast_into_full(w_out, W_OUT, idx, "cast_w_out"))
    wu = list(_cast_into_full(w_up, W_UP, idx, "cast_w_up"))
    wd = list(_cast_into_full(w_down, W_DOWN, idx, "cast_w_down"))
    wi[0], convs = _comm_call(_gather_plan([wi[0], conv_slots], [W_IN, SMALL]), "ag_first")
    per_chip = [_unpack_conv_shard(convs[k]) for k in range(N_CHIP)]
    lru_cw = jnp.concatenate([p[0] for p in per_chip], axis=-1)
    sc_cw = jnp.concatenate([p[1] for p in per_chip], axis=-1)
    ffn_cw = jnp.concatenate([p[2] for p in per_chip], axis=-1)

    prm = jnp.concatenate(
        [lru_cw, jnp.stack([lru_conv_b, lru_ba, lru_bx, lru_lambda], axis=1),
         jnp.pad(sc_cw, ((0, 0), (0, 0), (0, D_LRU - D_SC))), jnp.zeros((2, 5, D_LRU), F32)], axis=1)
    gates = jnp.concatenate([_block_diag(lru_wa), _block_diag(lru_wx)], axis=1).astype(BF16)
    fw8 = jnp.pad(ffn_cw.reshape(2, 3, 2, D_FF).transpose(0, 2, 1, 3), ((0, 0), (0, 0), (0, 5), (0, 0)))

    xs = x[0]
    saved = []
    n512, n256 = t_len // 512, t_len // 256
    whole, lower, upper = (0, 1), (0, 2), (1, 2)
    carried_by = {
        "fwd_in_0": ([(wu, 0, W_UP, (0, 4))], (max(n512 - 3, 0),)),
        "fwd_mixer_0": ([(wu, 0, W_UP, (1, 4, 2)), (wo, 0, W_OUT, whole)], (max(n256 - 3, 0),)),
        "fwd_out_0": ([(wu, 0, W_UP, (3, 4))], (max(n512 - 2, 0),)),
        "fwd_up_0": ([(wd, 0, W_DOWN, whole)], (max(n512 - 2, 0),)),
        "fwd_act_0": ([(wi, 1, W_IN, whole), (wo, 1, W_OUT, whole)], (max(n512 - 2, 0), 0)),
        "fwd_down_0": ([(wu, 1, W_UP, (0, 4))], (max(n512 - 3, 0),)),
        "fwd_in_1": ([(wu, 1, W_UP, (1, 4))], (max(n512 - 3, 0),)),
        "fwd_mixer_1": ([(wu, 1, W_UP, (2, 4, 2))], (max(n256 - 4, 0),)),
        "fwd_act_1": ([(wd, 1, W_DOWN, whole)], (max(n512 - 3, 0), 0)),
    }

    def carried(name):
        if name not in carried_by:
            return None, lambda got: None
        items, mid_at = carried_by[name]

        def store(got):
            for (lst, i, _, _), arr in zip(items, got):
                lst[i] = arr

        return _gather_plan([lst[i] for lst, i, _, _ in items], [k for _, _, k, _ in items], mid_at=mid_at,
                            parts=[p for _, _, _, p in items]), store

    for l in range(2):
        comm, store = carried(f"fwd_in_{l}")
        (z, h1), got = _norm_mm(xs, norm1_g[l][None], wi[l], f"fwd_in_{l}", comm=comm)
        store(got)
        comm, store = carried(f"fwd_mixer_{l}")
        (ymix, hst), got = _mixer_fwd(z, prm, gates, l, f"fwd_mixer_{l}", comm=comm)
        store(got)
        comm, store = carried(f"fwd_out_{l}")
        (x2,), got = _mm_res(ymix, wo[l], xs, f"fwd_out_{l}", comm=comm)
        store(got)
        comm, store = carried(f"fwd_up_{l}")
        (u, h2), got = _norm_mm(x2, norm2_g[l][None], wu[l], f"fwd_up_{l}", planes=True, comm=comm)
        store(got)
        comm, store = carried(f"fwd_act_{l}")
        (act, fg, fu), got = _ffn_act(u, fw8, l, f"fwd_act_{l}", comm=comm)
        store(got)
        comm, store = carried(f"fwd_down_{l}")
        (x3,), got = _mm_res(act, wd[l], x2, f"fwd_down_{l}", comm=comm)
        store(got)
        saved.append((xs, h1, z, hst, ymix, x2, h2, u, act, fg, fu))
        xs = x3

    dx, dxb, dgf, loss_blk = _loss_head(xs, final_g[None], loss_target[0], "loss_head")

    kinds = [W_IN, W_OUT, W_UP, W_DOWN]
    grads = [None, None]
    small = [None, None]
    reduced = [None] * 4
    summed1 = [None] * 4
    slots1 = [None] * 4

    def scatter1(ws):
        return _scatter_plan([summed1[w][0] for w in ws], [summed1[w][1] for w in ws], ws)

    for l in (1, 0):
        x_in, h1, z, hst, ymix, x2, h2, u, act, fg, fu = saved[l]
        carry = l == 0
        comm = _pair_plan([grads[1][W_IN]], [W_IN]) if carry else None
        (g_down,), got = _mm_tn(act, dxb, f"bwd_wdown_{l}", tk=1536, tn=1024, comm=comm)
        if carry:
            summed1[W_IN] = _pair_sum(grads[1][W_IN], got[0], idx, W_IN, "rs_add1_0")
        (dact,), _ = _mm_nt(dxb, wd[l], f"bwd_dact_{l}")
        comm = scatter1((W_UP, W_IN)) if carry else None
        (du, dfw), got = _ffn_bwd(dact, fg, fu, u, fw8, l, f"bwd_act_{l}", comm=comm)
        if carry:
            slots1[W_UP], slots1[W_IN] = got
            reduced = [_sum_slots(slots1[w], idx, kinds[w], 1, None, f"rs_sum1_{w}") for w in range(4)]
        comm = _share_plan(reduced, kinds, 1) if carry else None
        (g_up,), got = _mm_tn(h2, du, f"bwd_wup_{l}", tk=1024, tn=1536, planes=True, comm=comm)
        if carry:
            reduced = list(got)
        comm = _pair_plan([g_up, g_down], [W_UP, W_DOWN]) if carry else None
        (dx2, dx2b, dg2), got = _mm_nt_normbwd(du, wu[l], x2, norm2_g[l][None], dx, f"bwd_up_{l}", planes=True,
                                               comm=comm)
        if carry:
            sum_up = _pair_sum(g_up, got[0], idx, W_UP, "rs_add0_2")
            sum_down = _pair_sum(g_down, got[1], idx, W_DOWN, "rs_add0_3")
        (g_out,), _ = _mm_tn(ymix, dx2b, f"bwd_wout_{l}", tk=1536, tn=1024)
        comm = _pair_plan([g_out], [W_OUT]) if carry else None
        (dymix,), got = _mm_nt(dx2b, wo[l], f"bwd_dymix_{l}", comm=comm)
        trio = (W_OUT, W_UP, W_DOWN)
        if carry:
            sum_out = _pair_sum(g_out, got[0], idx, W_OUT, "rs_add0_1")
            comm = _scatter_plan([sum_out[0], sum_up[0], sum_down[0]], [sum_out[1], sum_up[1], sum_down[1]], trio)
        else:
            comm = _pair_plan([g_out, g_up, g_down], trio)
        (dz, dprm, dgates), got = _mixer_bwd(z, hst, dymix, prm, gates, l, f"bwd_mixer_{l}", comm=comm)
        if carry:
            for w, s in zip(trio, got):
                reduced[w] = _sum_slots(s, idx, w, 0, reduced[w], f"rs_sum0_{w}")
        else:
            for w, g, land in zip(trio, (g_out, g_up, g_down), got):
                summed1[w] = _pair_sum(g, land, idx, w, f"rs_add1_{w}")
        comm = _share_plan([reduced[w] for w in trio], trio, 0) if carry else scatter1((W_DOWN,))
        (g_in,), got = _mm_tn(h1, dz, f"bwd_win_{l}", tk=1024, tn=1792, comm=comm)
        if carry:
            for w, full in zip(trio, got):
                reduced[w] = full
        else:
            slots1[W_DOWN], = got
        if carry:
            land_in, = _comm_call(_pair_plan([g_in], [W_IN]), "rs_pair_in")
            sum_in = _pair_sum(g_in, land_in, idx, W_IN, "rs_add0_0")
            comm = _scatter_plan([sum_in[0]], [sum_in[1]], [W_IN])
        else:
            comm = scatter1((W_OUT,))
        (dx, dxb, dg1), got = _mm_nt_normbwd(dz, wi[l], x_in, norm1_g[l][None], dx2, f"bwd_in_{l}", comm=comm)
        if carry:
            slot_in, = got
        else:
            slots1[W_OUT], = got
        grads[l] = [g_in, g_out, g_up, g_down]
        rep = dict(zip(REP_NAMES, [dg1[0], dprm[4], dprm[5], dprm[6], dprm[7], dg2[0],
                                   _block_diag_extract(dgates[0:4]), _block_diag_extract(dgates[4:8])]))
        conv = [dprm[0:4].reshape(-1), jnp.pad(dprm[8:11, 0:D_SC].reshape(-1), (0, 512)),
                dfw[:, 0:3, :].transpose(1, 0, 2).reshape(-1)]
        small[l] = (rep, conv)
    grad_x = dx[None]
    g_small = jnp.concatenate(
        [small[l][0][k] for k in REP_NAMES for l in range(2)] + [dgf[0]] + small[0][1] + small[1][1]
        + [loss_blk.reshape(-1)]).reshape(SMALL_ROWS, LANES)

    def big(w, g, m, v, name):
        shape = w.shape
        two_d = lambda a: a.reshape(-1, shape[-1])
        outs, _ = _adamw(two_d(w), two_d(g), two_d(m), two_d(v), name, emit_grad=True)
        return [o.reshape(shape) for o in outs]

    land_small, = _comm_call(_pair_plan([g_small], [SMALL]), "rs_pair_small")
    sum_small = _pair_sum(g_small, land_small, idx, SMALL, "rs_add0_4")
    slot_small, = _comm_call(_scatter_plan([sum_small[0]], [sum_small[1]], [SMALL]), "rs_scatter_small")
    gw_in, gs = _comm_call(
        _share_plan([_sum_slots(slot_in, idx, W_IN, 0, reduced[W_IN], "rs_sum0_0"),
                     _sum_slots(slot_small, idx, SMALL, 0, None, "rs_sum0_4")], [W_IN, SMALL], 0), "rs_share0")
    upd = {"w_up": big(w_up, reduced[W_UP], m_w_up, v_w_up, "adamw_w_up"),
           "w_down": big(w_down, reduced[W_DOWN], m_w_down, v_w_down, "adamw_w_down"),
           "w_out": big(w_out, reduced[W_OUT], m_w_out, v_w_out, "adamw_w_out"),
           "w_in": big(w_in, gw_in, m_w_in, v_w_in, "adamw_w_in")}

    loss = gs[REP_ROWS + CONV_ROWS, 0]
    g_rep = gs[0:REP_ROWS]
    g_conv = gs[REP_ROWS:REP_ROWS + CONV_ROWS].reshape(2, CONV_LAYER)
    g_lru_cw = lax.dynamic_slice_in_dim(g_conv[:, 0:4096].reshape(2, 4, 1024), me * 256, 256, axis=2)
    g_sc_cw = lax.dynamic_slice_in_dim(g_conv[:, 4096:4096 + 1536].reshape(2, 3, 512), me * 128, 128, axis=2)
    g_ffn_cw = lax.dynamic_slice_in_dim(g_conv[:, 6144:].reshape(2, 3, 6144), me * 1536, 1536, axis=2)

    rep_out, _ = _adamw(
        _pack_rep(norm1_g, lru_conv_b, lru_ba, lru_bx, lru_lambda, norm2_g, lru_wa, lru_wx, final_g), g_rep,
        _pack_rep(m_norm1_g, m_lru_conv_b, m_lru_ba, m_lru_bx, m_lru_lambda, m_norm2_g, m_lru_wa, m_lru_wx, m_final_g),
        _pack_rep(v_norm1_g, v_lru_conv_b, v_lru_ba, v_lru_bx, v_lru_lambda, v_norm2_g, v_lru_wa, v_lru_wx, v_final_g),
        "adamw_rep")
    conv_out, _ = _adamw(s_conv, _pack_conv_shard(g_lru_cw, g_sc_cw, g_ffn_cw),
                         _pack_conv_shard(m_lru_conv_w, m_sc_conv_w, m_ffn_conv_w),
                         _pack_conv_shard(v_lru_conv_w, v_sc_conv_w, v_ffn_conv_w), "adamw_conv")

    names = ["norm1_g", "w_in", "lru_conv_w", "lru_conv_b", "lru_wa", "lru_ba", "lru_wx", "lru_bx", "lru_lambda",
             "sc_conv_w", "w_out", "norm2_g", "w_up", "ffn_conv_w", "w_down", "final_g"]
    groups = []
    g_all = dict(_unpack_rep(g_rep))
    g_all.update({k: v[3] for k, v in upd.items()})
    g_all.update(lru_conv_w=g_lru_cw, sc_conv_w=g_sc_cw, ffn_conv_w=g_ffn_cw)
    groups.append(g_all)
    for i in range(3):
        d = dict(_unpack_rep(rep_out[i]))
        cl, cs, cf = _unpack_conv_shard(conv_out[i])
        d.update(lru_conv_w=cl, sc_conv_w=cs, ffn_conv_w=cf)
        d.update({k: v[i] for k, v in upd.items()})
        groups.append(d)
    return (loss, grad_x, *[grp[n] for grp in groups for n in names])
```

```python
import dataclasses
import functools
import math
import operator
from typing import Any, Callable, Optional, Sequence

import jax
import jax.numpy as jnp
from jax import lax
from jax.experimental import pallas as pl
from jax.experimental.pallas import tpu as pltpu

F32 = jnp.float32
BF16 = jnp.bfloat16
MESH = pl.DeviceIdType.MESH

D_MODEL = 1024
D_LRU = 1024
D_SC = 512
D_MIX = D_LRU + D_SC
D_IN = 2 * D_LRU + 3 * D_SC
D_FF = 3072
N_CHIP = 4
RG_C = 8.0
EPS = 1e-6
ADAM_LR = 0.001
ADAM_B1 = 0.9
ADAM_B2 = 0.999
ADAM_EPS = 1e-08
ADAM_WD = 0.01
ADAM_STEP = 10

SUBLANES = 8
PACKED = 16
LANES = 128
VMEM_LIMIT = 56 * 1024 * 1024
GELU_C0 = math.sqrt(2.0 / math.pi)
GELU_C1 = 0.044715

REP_LAYER = 6 * 1024 + 2 * 16 * 64 * 64
REP_ROWS = (2 * REP_LAYER + 1024) // LANES
CONV_LAYER = 4 * 1024 + 2048 + 3 * 6144
CONV_ROWS = 2 * CONV_LAYER // LANES
SMALL_ROWS = REP_ROWS + CONV_ROWS + 8
CONV_PACK_ROWS = 96

W_IN, W_OUT, W_UP, W_DOWN, SMALL = range(5)
COL_SHARDED = {W_IN: True, W_OUT: False, W_UP: True, W_DOWN: False}

ONCE = pl.Buffered(1)
ANY = pl.BlockSpec(memory_space=pl.ANY)


def _cp(*sem):
    return pltpu.CompilerParams(dimension_semantics=sem, vmem_limit_bytes=VMEM_LIMIT)


@dataclasses.dataclass
class Comm:
    srcs: Sequence[Any]
    bufs: Sequence[Any]
    outs: Sequence[Any]
    n_sem: int
    start: Callable
    finish: Callable
    mid: Optional[Callable] = None
    mid_at: Optional[Sequence[int]] = None


def _pallas(body, *, name, grid, in_specs, out_specs, out_shape, args, sem, scratch_shapes=(), comm=None):
    if comm is None:
        res = pl.pallas_call(
            body, name=name, grid=grid, in_specs=list(in_specs), out_specs=list(out_specs),
            out_shape=list(out_shape), scratch_shapes=list(scratch_shapes), compiler_params=_cp(*sem))(*args)
        return tuple(res), ()
    n_in, n_out, n_scr = len(in_specs), len(out_specs), len(scratch_shapes)
    ns, nb, no = len(comm.srcs), len(comm.bufs), len(comm.outs)

    def carrier(*refs):
        p = 0
        main_in = refs[p:p + n_in]
        p += n_in
        c_src = refs[p:p + ns]
        p += ns + nb
        main_out = refs[p:p + n_out]
        p += n_out
        c_buf = refs[p:p + nb]
        p += nb
        c_out = refs[p:p + no]
        p += no
        scr = refs[p:p + n_scr]
        send, recv = refs[p + n_scr], refs[p + n_scr + 1]
        ids = [pl.program_id(a) for a in range(len(grid))]

        def at(steps):
            return functools.reduce(operator.and_, [i == s for i, s in zip(ids, steps)])

        @pl.when(at([0] * len(grid)))
        def _():
            comm.start(c_src, c_buf, c_out, send, recv)

        if comm.mid is not None:
            @pl.when(at(comm.mid_at))
            def _():
                comm.mid(c_src, c_buf, c_out, send, recv)

        body(*main_in, *main_out, *scr)

        @pl.when(at([g - 1 for g in grid]))
        def _():
            comm.finish(c_src, c_buf, c_out, send, recv)

    res = pl.pallas_call(
        carrier, name=name, grid=grid,
        in_specs=list(in_specs) + [ANY] * (ns + nb),
        out_specs=list(out_specs) + [ANY] * (nb + no),
        out_shape=list(out_shape) + [jax.ShapeDtypeStruct(b.shape, b.dtype) for b in comm.bufs] + list(comm.outs),
        input_output_aliases={n_in + ns + j: n_out + j for j in range(nb)},
        scratch_shapes=list(scratch_shapes) + [pltpu.SemaphoreType.DMA((comm.n_sem,)),
                                               pltpu.SemaphoreType.DMA((comm.n_sem,))],
        compiler_params=_cp(*(["arbitrary"] * len(grid))),
    )(*args, *comm.srcs, *comm.bufs)
    return tuple(res[:n_out]), tuple(res[n_out:])


def _comm_call(comm, name):
    ns, nb, no = len(comm.srcs), len(comm.bufs), len(comm.outs)

    def body(*refs):
        c_src = refs[0:ns]
        c_buf = refs[ns + nb:ns + 2 * nb]
        c_out = refs[ns + 2 * nb:ns + 2 * nb + no]
        send, recv = refs[ns + 2 * nb + no], refs[ns + 2 * nb + no + 1]
        comm.start(c_src, c_buf, c_out, send, recv)
        if comm.mid is not None:
            comm.mid(c_src, c_buf, c_out, send, recv)
        comm.finish(c_src, c_buf, c_out, send, recv)

    return tuple(pl.pallas_call(
        body, name=name,
        in_specs=[ANY] * (ns + nb), out_specs=[ANY] * (nb + no),
        out_shape=[jax.ShapeDtypeStruct(b.shape, b.dtype) for b in comm.bufs] + list(comm.outs),
        input_output_aliases={ns + j: j for j in range(nb)},
        scratch_shapes=[pltpu.SemaphoreType.DMA((comm.n_sem,)), pltpu.SemaphoreType.DMA((comm.n_sem,))],
    )(*comm.srcs, *comm.bufs))


def _sigmoid(v):
    return 1.0 / (1.0 + jnp.exp(-v))


def _sigmoid_tanh(v):
    return 0.5 + 0.5 * jnp.tanh(0.5 * v)


def _gelu_parts(v):
    v2 = v * v
    t = jnp.tanh(v * (GELU_C0 + (GELU_C0 * GELU_C1) * v2))
    half = 0.5 + 0.5 * t
    gel = v * half
    dgel = half + (0.5 * v) * (1.0 - t * t) * (GELU_C0 + (3.0 * GELU_C0 * GELU_C1) * v2)
    return gel, dgel


def _gelu(v):
    t = jnp.tanh(v * (GELU_C0 + (GELU_C0 * GELU_C1) * (v * v)))
    return v * (0.5 + 0.5 * t)


def _neg_expm1(y, a):
    p = jnp.full_like(y, 1.0 / 120.0)
    for coef in (1.0 / 24.0, 1.0 / 6.0, 0.5, 1.0):
        p = p * y + coef
    return jnp.where(y > -0.1, -(p * y), 1.0 - a * a)


def _softplus_neg(lam):
    nl = -lam
    e = jnp.exp(-jnp.abs(nl))
    u = 1.0 + e
    l1p = jnp.where(u == 1.0, e, jnp.log(u) * e / (u - 1.0))
    return jnp.maximum(nl, 0.0) + l1p


def _conv_taps(ext, taps, n_out):
    kw = len(taps)
    acc = taps[kw - 1] * ext[SUBLANES:SUBLANES + n_out]
    for k in range(kw - 1):
        acc = acc + taps[k] * pltpu.roll(ext, kw - 1 - k, axis=0)[SUBLANES:SUBLANES + n_out]
    return acc


def _conv_taps_t(ext, taps, n_out):
    kw = len(taps)
    n = ext.shape[0]
    acc = taps[kw - 1] * ext[0:n_out]
    for k in range(kw - 1):
        acc = acc + taps[k] * pltpu.roll(ext, n - (kw - 1 - k), axis=0)[0:n_out]
    return acc


def _scan8(a, b, carry, row):
    for s in (1, 2, 4):
        m = row >= s
        a_sh = jnp.where(m, pltpu.roll(a, s, axis=0), 1.0)
        b_sh = jnp.where(m, pltpu.roll(b, s, axis=0), 0.0)
        b = a * b_sh + b
        a = a * a_sh
    return a * carry + b


def _scan8_rev(a, b, carry, row):
    for s in (1, 2, 4):
        m = row < SUBLANES - s
        a_sh = jnp.where(m, pltpu.roll(a, SUBLANES - s, axis=0), 1.0)
        b_sh = jnp.where(m, pltpu.roll(b, SUBLANES - s, axis=0), 0.0)
        b = a * b_sh + b
        a = a * a_sh
    return a * carry + b


def _cast_into_full(w, kind, idx, name):
    nl, r, c = w.shape
    tr = 256 if r % 256 == 0 else r
    nrb = r // tr

    def body(idx_ref, w_ref, o0_ref, o1_ref):
        o0_ref[...] = w_ref[0].astype(BF16)
        o1_ref[...] = w_ref[1].astype(BF16)

    if COL_SHARDED[kind]:
        full = (r, N_CHIP * c)
        o_spec = pl.BlockSpec((tr, c), lambda i, idx_ref: (i, idx_ref[1]))
    else:
        full = (N_CHIP * r, c)
        o_spec = pl.BlockSpec((tr, c), lambda i, idx_ref: (idx_ref[1] * nrb + i, 0))
    return pl.pallas_call(
        body, name=name,
        grid_spec=pltpu.PrefetchScalarGridSpec(
            num_scalar_prefetch=1, grid=(nrb,),
            in_specs=[pl.BlockSpec((nl, tr, c), lambda i, idx_ref: (0, i, 0))], out_specs=[o_spec, o_spec]),
        out_shape=[jax.ShapeDtypeStruct(full, BF16)] * 2,
        compiler_params=_cp("parallel"),
    )(idx, w)


def _norm_mm(x, g, w, name, planes=False, tm=512, tn=512, comm=None):
    t_len, d = x.shape
    n = w.shape[1]
    half = n // 2

    def body(x_ref, g_ref, w_ref, z_ref, h_ref):
        xv = x_ref[...]
        r = lax.rsqrt(jnp.mean(xv * xv, axis=-1, keepdims=True) + EPS)
        h_ref[...] = ((xv * r) * g_ref[...]).astype(BF16)
        for n0 in range(0, n, tn):
            blk = jnp.dot(h_ref[...], w_ref[:, n0:n0 + tn], preferred_element_type=F32).astype(BF16)
            if planes:
                z_ref[n0 // half, :, n0 % half:n0 % half + tn] = blk
            else:
                z_ref[:, n0:n0 + tn] = blk

    if planes:
        z_shape = jax.ShapeDtypeStruct((2, t_len, half), BF16)
        z_spec = pl.BlockSpec((2, tm, half), lambda i: (0, i, 0))
    else:
        z_shape = jax.ShapeDtypeStruct((t_len, n), BF16)
        z_spec = pl.BlockSpec((tm, n), lambda i: (i, 0))
    return _pallas(
        body, name=name, grid=(t_len // tm,),
        in_specs=[pl.BlockSpec((tm, d), lambda i: (i, 0)),
                  pl.BlockSpec((1, d), lambda i: (0, 0)),
                  pl.BlockSpec((d, n), lambda i: (0, 0), pipeline_mode=ONCE)],
        out_specs=[z_spec, pl.BlockSpec((tm, d), lambda i: (i, 0))],
        out_shape=[z_shape, jax.ShapeDtypeStruct((t_len, d), BF16)],
        args=(x, g, w), sem=("parallel",), comm=comm)


def _mm_res(a, w, res, name, tm=512, comm=None):
    t_len, k = a.shape
    n = w.shape[1]

    def body(a_ref, w_ref, r_ref, o_ref):
        o_ref[...] = r_ref[...] + jnp.dot(a_ref[...], w_ref[...], preferred_element_type=F32)

    return _pallas(
        body, name=name, grid=(t_len // tm,),
        in_specs=[pl.BlockSpec((tm, k), lambda i: (i, 0)),
                  pl.BlockSpec((k, n), lambda i: (0, 0), pipeline_mode=ONCE),
                  pl.BlockSpec((tm, n), lambda i: (i, 0))],
        out_specs=[pl.BlockSpec((tm, n), lambda i: (i, 0))],
        out_shape=[jax.ShapeDtypeStruct((t_len, n), F32)],
        args=(a, w, res), sem=("parallel",), comm=comm)


def _mm_res_loss(a, w, res, g, target, name, tm=512):
    t_len, k = a.shape
    n = w.shape[1]

    def body(a_ref, w_ref, r_ref, g_ref, t_ref, dx_ref, dxb_ref, dg_ref, loss_ref):
        @pl.when(pl.program_id(0) == 0)
        def _():
            dg_ref[...] = jnp.zeros_like(dg_ref)
            loss_ref[...] = jnp.zeros_like(loss_ref)

        xv = r_ref[...] + jnp.dot(a_ref[...], w_ref[...], preferred_element_type=F32)
        gv = g_ref[...]
        r = lax.rsqrt(jnp.mean(xv * xv, axis=-1, keepdims=True) + EPS)
        xh = xv * r
        err = xh * gv - t_ref[...]
        loss_ref[...] += (0.5 / n) * jnp.sum(jnp.sum(err * err, axis=-1, keepdims=True), axis=0, keepdims=True)
        dy = err * (1.0 / n)
        dyg = dy * gv
        dx = r * (dyg - xh * jnp.mean(dyg * xh, axis=-1, keepdims=True))
        dx_ref[...] = dx
        dxb_ref[...] = dx.astype(BF16)
        dg_ref[0:1, :] += jnp.sum(dy * xh, axis=0, keepdims=True)

    row = pl.BlockSpec((tm, n), lambda i: (i, 0))
    return _pallas(
        body, name=name, grid=(t_len // tm,),
        in_specs=[pl.BlockSpec((tm, k), lambda i: (i, 0)),
                  pl.BlockSpec((k, n), lambda i: (0, 0), pipeline_mode=ONCE),
                  row, pl.BlockSpec((1, n), lambda i: (0, 0)), row],
        out_specs=[row, row, pl.BlockSpec((SUBLANES, n), lambda i: (0, 0)),
                   pl.BlockSpec((SUBLANES, LANES), lambda i: (0, 0))],
        out_shape=[jax.ShapeDtypeStruct((t_len, n), F32), jax.ShapeDtypeStruct((t_len, n), BF16),
                   jax.ShapeDtypeStruct((SUBLANES, n), F32), jax.ShapeDtypeStruct((SUBLANES, LANES), F32)],
        args=(a, w, res, g, target), sem=("arbitrary",))[0]


def _mm_nt(a, w, name, tm=512, comm=None):
    t_len, k = a.shape
    n = w.shape[0]

    def body(a_ref, w_ref, o_ref):
        o_ref[...] = lax.dot_general(a_ref[...], w_ref[...], (((1,), (1,)), ((), ())),
                                     preferred_element_type=F32).astype(BF16)

    return _pallas(
        body, name=name, grid=(t_len // tm,),
        in_specs=[pl.BlockSpec((tm, k), lambda i: (i, 0)),
                  pl.BlockSpec((n, k), lambda i: (0, 0), pipeline_mode=ONCE)],
        out_specs=[pl.BlockSpec((tm, n), lambda i: (i, 0))],
        out_shape=[jax.ShapeDtypeStruct((t_len, n), BF16)],
        args=(a, w), sem=("parallel",), comm=comm)


def _mm_nt_normbwd(dz, w, x, g, dres, name, planes=False, tm=512, comm=None):
    t_len, d = x.shape
    n = w.shape[1]
    half = n // 2
    nt_dims = (((1,), (1,)), ((), ()))

    def body(dz_ref, w_ref, x_ref, g_ref, r_ref, dx_ref, dxb_ref, dg_ref):
        @pl.when(pl.program_id(0) == 0)
        def _():
            dg_ref[...] = jnp.zeros_like(dg_ref)

        if planes:
            dh = (lax.dot_general(dz_ref[0], w_ref[:, 0:half], nt_dims, preferred_element_type=F32)
                  + lax.dot_general(dz_ref[1], w_ref[:, half:], nt_dims, preferred_element_type=F32))
        else:
            dh = lax.dot_general(dz_ref[...], w_ref[...], nt_dims, preferred_element_type=F32)
        xv = x_ref[...]
        r = lax.rsqrt(jnp.mean(xv * xv, axis=-1, keepdims=True) + EPS)
        xh = xv * r
        dhg = dh * g_ref[...]
        dx = r_ref[...] + r * (dhg - xh * jnp.mean(dhg * xh, axis=-1, keepdims=True))
        dx_ref[...] = dx
        dxb_ref[...] = dx.astype(BF16)
        dg_ref[0:1, :] += jnp.sum(dh * xh, axis=0, keepdims=True)

    if planes:
        dz_spec = pl.BlockSpec((2, tm, half), lambda i: (0, i, 0))
    else:
        dz_spec = pl.BlockSpec((tm, n), lambda i: (i, 0))
    return _pallas(
        body, name=name, grid=(t_len // tm,),
        in_specs=[dz_spec,
                  pl.BlockSpec((d, n), lambda i: (0, 0), pipeline_mode=ONCE),
                  pl.BlockSpec((tm, d), lambda i: (i, 0)),
                  pl.BlockSpec((1, d), lambda i: (0, 0)),
                  pl.BlockSpec((tm, d), lambda i: (i, 0))],
        out_specs=[pl.BlockSpec((tm, d), lambda i: (i, 0)),
                   pl.BlockSpec((tm, d), lambda i: (i, 0)),
                   pl.BlockSpec((SUBLANES, d), lambda i: (0, 0))],
        out_shape=[jax.ShapeDtypeStruct((t_len, d), F32),
                   jax.ShapeDtypeStruct((t_len, d), BF16),
                   jax.ShapeDtypeStruct((SUBLANES, d), F32)],
        args=(dz, w, x, g, dres), sem=("arbitrary",), comm=comm)


def _mm_tn(a, g, name, tk, tn, planes=False, tt=1024, comm=None):
    t_len, k = a.shape
    n = 2 * g.shape[2] if planes else g.shape[1]
    nn = n // tn
    half = nn // 2
    tt = min(tt, t_len)

    def body(a_ref, g_ref, o_ref):
        @pl.when(pl.program_id(2) == 0)
        def _():
            o_ref[...] = jnp.zeros_like(o_ref)

        o_ref[...] += lax.dot_general(a_ref[...], g_ref[...], (((0,), (0,)), ((), ())),
                                      preferred_element_type=F32)

    if planes:
        g_spec = pl.BlockSpec((None, tt, tn), lambda i, j, t: (j // half, t, j % half))
    else:
        g_spec = pl.BlockSpec((tt, tn), lambda i, j, t: (t, j))
    return _pallas(
        body, name=name, grid=(k // tk, nn, t_len // tt),
        in_specs=[pl.BlockSpec((tt, tk), lambda i, j, t: (t, i)), g_spec],
        out_specs=[pl.BlockSpec((tk, tn), lambda i, j, t: (i, j))],
        out_shape=[jax.ShapeDtypeStruct((k, n), F32)],
        args=(a, g), sem=("parallel", "parallel", "arbitrary"), comm=comm)


def _lru_gates(rp, ip, spn):
    r = _sigmoid(rp)
    i = _sigmoid_tanh(ip)
    la = r * spn
    a = jnp.exp(la)
    mult = jnp.sqrt(_neg_expm1(2.0 * la, a))
    return r, i, a, mult


def _mixer_fwd(z, prm, gates, layer, name, tb=256, comm=None):
    t_len = z.shape[0]

    def body(z_ref, p_ref, g_ref, y_ref, h_ref, xhalo, phalo, hcar, lx_s, rp_s, ip_s):
        @pl.when(pl.program_id(0) == 0)
        def _():
            xhalo[...] = jnp.zeros_like(xhalo)
            phalo[...] = jnp.zeros_like(phalo)
            hcar[...] = jnp.zeros_like(hcar)

        prm_v = p_ref[...]
        cw = prm_v[0:4]
        vec = prm_v[4:8]
        xp = z_ref[:, 0:D_LRU].astype(F32)
        ext = jnp.concatenate([xhalo[...], xp], axis=0)
        lx = vec[0:1] + _conv_taps(ext, [cw[k:k + 1] for k in range(4)], tb)
        xhalo[...] = xp[tb - SUBLANES:]
        lx_s[...] = lx
        lxb = lx.astype(BF16)
        for q in range(4):
            sl = slice(q * 256, (q + 1) * 256)
            rp_s[:, sl] = jnp.dot(lxb[:, sl], g_ref[q], preferred_element_type=F32) + vec[1:2, sl]
            ip_s[:, sl] = jnp.dot(lxb[:, sl], g_ref[4 + q], preferred_element_type=F32) + vec[2:3, sl]

        spn = jnp.broadcast_to(-RG_C * _softplus_neg(vec[3:4]), (SUBLANES, D_LRU))
        row = lax.broadcasted_iota(jnp.int32, (SUBLANES, D_LRU), 0)

        def step(ci, carry):
            o = pl.multiple_of(ci * PACKED, PACKED)
            gate = z_ref[pl.ds(o, PACKED), D_LRU:2 * D_LRU].astype(F32)
            ys = []
            for sub in range(2):
                rows = pl.ds(pl.multiple_of(o + sub * SUBLANES, SUBLANES), SUBLANES)
                lxv = lx_s[rows, :]
                _, i, a, mult = _lru_gates(rp_s[rows, :], ip_s[rows, :], spn)
                h = _scan8(a, mult * (i * lxv), carry, row)
                h_ref[rows, :] = h
                ys.append(h * _gelu(gate[sub * SUBLANES:(sub + 1) * SUBLANES]))
                carry = jnp.broadcast_to(h[SUBLANES - 1:SUBLANES, :], (SUBLANES, D_LRU))
            y_ref[pl.ds(o, PACKED), 0:D_LRU] = jnp.concatenate(ys, axis=0).astype(BF16)
            return carry

        hcar[...] = lax.fori_loop(0, tb // PACKED, step, hcar[...])

        scw = prm_v[8:11, 0:D_SC]
        o_b, o_c, o_x = 2 * D_LRU, 2 * D_LRU + D_SC, 2 * D_LRU + 2 * D_SC
        p = z_ref[:, o_c:o_x].astype(F32) * z_ref[:, o_x:].astype(F32)
        pext = jnp.concatenate([phalo[...], p], axis=0)
        q = _conv_taps(pext, [scw[k:k + 1] for k in range(3)], tb)
        phalo[...] = p[tb - SUBLANES:]
        y_ref[:, D_LRU:] = (z_ref[:, o_b:o_c].astype(F32) * q).astype(BF16)

    return _pallas(
        body, name=name, grid=(t_len // tb,),
        in_specs=[pl.BlockSpec((tb, D_IN), lambda t: (t, 0)),
                  pl.BlockSpec((None, 2 * SUBLANES, D_LRU), lambda t: (layer, 0, 0)),
                  pl.BlockSpec((None, 8, 256, 256), lambda t: (layer, 0, 0, 0))],
        out_specs=[pl.BlockSpec((tb, D_MIX), lambda t: (t, 0)),
                   pl.BlockSpec((tb, D_LRU), lambda t: (t, 0))],
        out_shape=[jax.ShapeDtypeStruct((t_len, D_MIX), BF16),
                   jax.ShapeDtypeStruct((t_len, D_LRU), F32)],
        scratch_shapes=[pltpu.VMEM((SUBLANES, D_LRU), F32), pltpu.VMEM((SUBLANES, D_SC), F32),
                        pltpu.VMEM((SUBLANES, D_LRU), F32), pltpu.VMEM((tb, D_LRU), F32),
                        pltpu.VMEM((tb, D_LRU), F32), pltpu.VMEM((tb, D_LRU), F32)],
        args=(z, prm, gates), sem=("arbitrary",), comm=comm)


def _mixer_bwd(z, h, dy, prm, gates, layer, name, tb=256, comm=None):
    t_len = z.shape[0]
    nb = t_len // tb

    def body(z_ref, zh_ref, h_ref, hh_ref, dy_ref, p_ref, g_ref, dz_ref, dp_ref, dg_ref,
             lx_s, rp_s, ip_s, drpb_s, dipb_s, dlx_s, hext_s, acc_s, acar, gcar, dqh):
        t = pl.program_id(0)
        first_block = t == nb - 1

        @pl.when(t == 0)
        def _():
            for ref in (dp_ref, dg_ref, acc_s, acar, gcar, dqh):
                ref[...] = jnp.zeros_like(ref)
            dlx_s[tb:, :] = jnp.zeros((SUBLANES, D_LRU), F32)

        prm_v = p_ref[...]
        cw = prm_v[0:4]
        vec = prm_v[4:8]
        scw = prm_v[8:11, 0:D_SC]
        wa_ref = [g_ref.at[q] for q in range(4)]
        wx_ref = [g_ref.at[4 + q] for q in range(4)]
        dwa_ref = [dg_ref.at[q] for q in range(4)]
        dwx_ref = [dg_ref.at[4 + q] for q in range(4)]
        ctaps = [cw[k:k + 1] for k in range(4)]
        staps = [scw[k:k + 1] for k in range(3)]
        keep = jnp.where(first_block, 0.0, 1.0)
        zh = zh_ref[...].astype(F32)[PACKED - SUBLANES:] * keep

        xp = z_ref[:, 0:D_LRU].astype(F32)
        xext = jnp.concatenate([zh[:, 0:D_LRU], xp], axis=0)
        lx = vec[0:1] + _conv_taps(xext, ctaps, tb)
        lx_s[...] = lx
        lxb = lx.astype(BF16)
        for q in range(4):
            sl = slice(q * 256, (q + 1) * 256)
            rp_s[:, sl] = jnp.dot(lxb[:, sl], wa_ref[q][...], preferred_element_type=F32) + vec[1:2, sl]
            ip_s[:, sl] = jnp.dot(lxb[:, sl], wx_ref[q][...], preferred_element_type=F32) + vec[2:3, sl]
        hext_s[0:SUBLANES, :] = hh_ref[...] * keep
        hext_s[SUBLANES:, :] = h_ref[...]

        spn = jnp.broadcast_to(-RG_C * _softplus_neg(vec[3:4]), (SUBLANES, D_LRU))
        row = lax.broadcasted_iota(jnp.int32, (SUBLANES, D_LRU), 0)

        def step(ci, carry):
            a_next, g_next = carry
            o = pl.multiple_of((tb // PACKED - 1 - ci) * PACKED, PACKED)
            rows16 = pl.ds(o, PACKED)
            gate16 = z_ref[rows16, D_LRU:2 * D_LRU].astype(F32)
            dyl16 = dy_ref[rows16, 0:D_LRU].astype(F32)
            dgs, drs, dis = [None, None], [None, None], [None, None]
            for sub in (1, 0):
                oo = pl.multiple_of(o + sub * SUBLANES, SUBLANES)
                rows = pl.ds(oo, SUBLANES)
                half = slice(sub * SUBLANES, (sub + 1) * SUBLANES)
                lxv = lx_s[rows, :]
                r, i, a, mult = _lru_gates(rp_s[rows, :], ip_s[rows, :], spn)
                hwin = hext_s[pl.ds(oo, 2 * SUBLANES), :]
                hv = hwin[SUBLANES:]
                hprev = pltpu.roll(hwin, 1, axis=0)[SUBLANES:]
                gel, dgel = _gelu_parts(gate16[half])
                dyl = dyl16[half]
                a_up = jnp.where(row < SUBLANES - 1, pltpu.roll(a, SUBLANES - 1, axis=0), a_next)
                gg = _scan8_rev(a_up, dyl * gel, g_next, row)
                dgs[sub] = dyl * hv * dgel
                ilx = i * lxv
                dla = a * (gg * hprev - (gg * ilx) * a / mult)
                dlx_s[rows, :] = gg * mult * i
                drp = dla * spn * r * (1.0 - r)
                dip = gg * mult * lxv * i * (1.0 - i)
                drs[sub] = drp
                dis[sub] = dip
                acc_s[0] += drp
                acc_s[1] += dip
                acc_s[2] += dla * r
                a_next = jnp.broadcast_to(a[0:1, :], (SUBLANES, D_LRU))
                g_next = jnp.broadcast_to(gg[0:1, :], (SUBLANES, D_LRU))
            dz_ref[rows16, D_LRU:2 * D_LRU] = jnp.concatenate(dgs, axis=0).astype(BF16)
            drpb_s[rows16, :] = jnp.concatenate(drs, axis=0).astype(BF16)
            dipb_s[rows16, :] = jnp.concatenate(dis, axis=0).astype(BF16)
            return a_next, g_next

        a_c, g_c = lax.fori_loop(0, tb // PACKED, step, (acar[...], gcar[...]))
        acar[...] = a_c
        gcar[...] = g_c

        drpb = drpb_s[...]
        dipb = dipb_s[...]
        nt_dims = (((1,), (1,)), ((), ()))
        tn_dims = (((0,), (0,)), ((), ()))
        for q in range(4):
            sl = slice(q * 256, (q + 1) * 256)
            dlx_s[0:tb, sl] += (
                lax.dot_general(drpb[:, sl], wa_ref[q][...], nt_dims, preferred_element_type=F32)
                + lax.dot_general(dipb[:, sl], wx_ref[q][...], nt_dims, preferred_element_type=F32))
            dwa_ref[q][...] += lax.dot_general(lxb[:, sl], drpb[:, sl], tn_dims, preferred_element_type=F32)
            dwx_ref[q][...] += lax.dot_general(lxb[:, sl], dipb[:, sl], tn_dims, preferred_element_type=F32)

        dlx_ext = dlx_s[...]
        dlx = dlx_ext[0:tb]
        dz_ref[:, 0:D_LRU] = _conv_taps_t(dlx_ext, ctaps, tb).astype(BF16)
        dp_ref[3:4, :] += jnp.sum(dlx * xp, axis=0, keepdims=True)
        for k in range(3):
            shifted = pltpu.roll(xext, 3 - k, axis=0)[SUBLANES:]
            dp_ref[k:k + 1, :] += jnp.sum(dlx * shifted, axis=0, keepdims=True)
        dp_ref[4:5, :] += jnp.sum(dlx, axis=0, keepdims=True)
        dlx_s[tb:, :] = dlx[0:SUBLANES]

        o_b, o_c, o_x = 2 * D_LRU, 2 * D_LRU + D_SC, 2 * D_LRU + 2 * D_SC
        sb = z_ref[:, o_b:o_c].astype(F32)
        scc = z_ref[:, o_c:o_x].astype(F32)
        sx = z_ref[:, o_x:].astype(F32)
        p = scc * sx
        pext = jnp.concatenate([zh[:, o_c:o_x] * zh[:, o_x:], p], axis=0)
        q = _conv_taps(pext, staps, tb)
        dys = dy_ref[:, D_LRU:].astype(F32)
        dq = dys * sb
        dp = _conv_taps_t(jnp.concatenate([dq, dqh[...]], axis=0), staps, tb)
        dp_ref[10:11, 0:D_SC] += jnp.sum(dq * p, axis=0, keepdims=True)
        for k in range(2):
            shifted = pltpu.roll(pext, 2 - k, axis=0)[SUBLANES:]
            dp_ref[8 + k:9 + k, 0:D_SC] += jnp.sum(dq * shifted, axis=0, keepdims=True)
        dqh[...] = dq[0:SUBLANES]
        dz_ref[:, o_b:o_c] = (dys * q).astype(BF16)
        dz_ref[:, o_c:o_x] = (dp * sx).astype(BF16)
        dz_ref[:, o_x:] = (dp * scc).astype(BF16)

        @pl.when(first_block)
        def _():
            dp_ref[5:6, :] = jnp.sum(acc_s[0], axis=0, keepdims=True)
            dp_ref[6:7, :] = jnp.sum(acc_s[1], axis=0, keepdims=True)
            dp_ref[7:8, :] = (jnp.sum(acc_s[2], axis=0, keepdims=True) * RG_C * _sigmoid(-vec[3:4]))

    blk = lambda t: (nb - 1 - t, 0)
    halo8 = lambda t: (jnp.maximum((nb - 1 - t) * (tb // SUBLANES) - 1, 0), 0)
    halo16 = lambda t: (jnp.maximum((nb - 1 - t) * (tb // PACKED) - 1, 0), 0)
    return _pallas(
        body, name=name, grid=(nb,),
        in_specs=[pl.BlockSpec((tb, D_IN), blk), pl.BlockSpec((PACKED, D_IN), halo16),
                  pl.BlockSpec((tb, D_LRU), blk), pl.BlockSpec((SUBLANES, D_LRU), halo8),
                  pl.BlockSpec((tb, D_MIX), blk),
                  pl.BlockSpec((None, 2 * SUBLANES, D_LRU), lambda t: (layer, 0, 0)),
                  pl.BlockSpec((None, 8, 256, 256), lambda t: (layer, 0, 0, 0))],
        out_specs=[pl.BlockSpec((tb, D_IN), blk),
                   pl.BlockSpec((2 * SUBLANES, D_LRU), lambda t: (0, 0)),
                   pl.BlockSpec((8, 256, 256), lambda t: (0, 0, 0))],
        out_shape=[jax.ShapeDtypeStruct((t_len, D_IN), BF16),
                   jax.ShapeDtypeStruct((2 * SUBLANES, D_LRU), F32),
                   jax.ShapeDtypeStruct((8, 256, 256), F32)],
        scratch_shapes=[pltpu.VMEM((tb, D_LRU), F32),
                        pltpu.VMEM((tb, D_LRU), F32), pltpu.VMEM((tb, D_LRU), F32),
                        pltpu.VMEM((tb, D_LRU), BF16), pltpu.VMEM((tb, D_LRU), BF16),
                        pltpu.VMEM((tb + SUBLANES, D_LRU), F32), pltpu.VMEM((tb + SUBLANES, D_LRU), F32),
                        pltpu.VMEM((3, SUBLANES, D_LRU), F32),
                        pltpu.VMEM((SUBLANES, D_LRU), F32), pltpu.VMEM((SUBLANES, D_LRU), F32),
                        pltpu.VMEM((SUBLANES, D_SC), F32)],
        args=(z, z, h, h, dy, prm, gates), sem=("arbitrary",), comm=comm)


def _ffn_act(u, fw, layer, name, tb=512, tn=1536, rc=128, comm=None):
    t_len = u.shape[1]
    hb = tb // PACKED

    def body(u_ref, uh_ref, fw_ref, o_ref, fg_ref, fu_ref, ext):
        keep = jnp.where(pl.program_id(0) == 0, 0.0, 1.0)
        ext[:, 0:SUBLANES, :] = uh_ref[...].astype(F32)[:, PACKED - SUBLANES:, :] * keep
        ext[:, SUBLANES:, :] = u_ref[...].astype(F32)
        fw_v = fw_ref[...]

        for lb in range(tn // LANES):
            lanes = slice(lb * LANES, (lb + 1) * LANES)
            wg = [fw_v[0, k:k + 1, lanes] for k in range(3)]
            wu = [fw_v[1, k:k + 1, lanes] for k in range(3)]

            def chunk(ci, c, lanes=lanes, wg=wg, wu=wu):
                o = pl.multiple_of(ci * rc, rc)
                win = pl.ds(o, rc + SUBLANES)
                gate = _conv_taps(ext[0, win, lanes], wg, rc)
                up = _conv_taps(ext[1, win, lanes], wu, rc)
                gel, dgel = _gelu_parts(gate)
                rows = pl.ds(o, rc)
                o_ref[rows, lanes] = (gel * up).astype(BF16)
                fg_ref[rows, lanes] = (up * dgel).astype(BF16)
                fu_ref[rows, lanes] = gel.astype(BF16)
                return c

            lax.fori_loop(0, tb // rc, chunk, 0)

    spec = pl.BlockSpec((tb, tn), lambda i, j: (i, j))
    shape = jax.ShapeDtypeStruct((t_len, D_FF), BF16)
    return _pallas(
        body, name=name, grid=(t_len // tb, D_FF // tn),
        in_specs=[pl.BlockSpec((2, tb, tn), lambda i, j: (0, i, j)),
                  pl.BlockSpec((2, PACKED, tn), lambda i, j: (0, jnp.maximum(i * hb - 1, 0), j)),
                  pl.BlockSpec((None, 2, SUBLANES, tn), lambda i, j: (layer, 0, 0, j))],
        out_specs=[spec] * 3, out_shape=[shape] * 3,
        scratch_shapes=[pltpu.VMEM((2, tb + SUBLANES, tn), F32)],
        args=(u, u, fw), sem=("parallel", "parallel"), comm=comm)


def _ffn_bwd(dact, fg, fu, u, fw, layer, name, tb=512, tn=1536, rc=64, comm=None):
    t_len = u.shape[1]
    ni = t_len // tb
    hb = tb // PACKED
    last_halo = t_len // PACKED - 1

    def body(d_ref, dn_ref, fg_ref, fgn_ref, fu_ref, fun_ref, u_ref, up_ref, fw_ref, du_ref, dfw_ref,
             extu, extp, acc):
        i = pl.program_id(1)

        @pl.when(i == 0)
        def _():
            acc[...] = jnp.zeros_like(acc)

        keep_prev = jnp.where(i == 0, 0.0, 1.0)
        keep_next = jnp.where(i == ni - 1, 0.0, 1.0)
        extu[:, 0:SUBLANES, :] = up_ref[...].astype(F32)[:, PACKED - SUBLANES:, :] * keep_prev
        extu[:, SUBLANES:, :] = u_ref[...].astype(F32)
        dv = d_ref[...].astype(F32)
        dn = dn_ref[...].astype(F32)[0:SUBLANES] * keep_next
        extp[0, 0:tb, :] = dv * fg_ref[...].astype(F32)
        extp[0, tb:, :] = dn * fgn_ref[...].astype(F32)[0:SUBLANES]
        extp[1, 0:tb, :] = dv * fu_ref[...].astype(F32)
        extp[1, tb:, :] = dn * fun_ref[...].astype(F32)[0:SUBLANES]
        fw_v = fw_ref[...]
        m = rc + SUBLANES

        for lb in range(tn // LANES):
            lanes = slice(lb * LANES, (lb + 1) * LANES)
            taps = [[fw_v[pln, k:k + 1, lanes] for k in range(3)] for pln in range(2)]

            def chunk(ci, c, lanes=lanes, taps=taps):
                o = pl.multiple_of(ci * rc, rc)
                for pln in range(2):
                    e = extu[pln, pl.ds(o, m), lanes]
                    sh = [pltpu.roll(e, 2, axis=0)[SUBLANES:], pltpu.roll(e, 1, axis=0)[SUBLANES:], e[SUBLANES:]]
                    dpost = extp[pln, pl.ds(o, m), lanes]
                    du_ref[pln, pl.ds(o, rc), lanes] = _conv_taps_t(dpost, taps[pln], rc).astype(BF16)
                    for k in range(3):
                        prod = dpost[0:rc] * sh[k]
                        acc[3 * pln + k, :, lanes] += sum(
                            prod[s:s + SUBLANES] for s in range(0, rc, SUBLANES))
                return c

            lax.fori_loop(0, tb // rc, chunk, 0)

        @pl.when(i == ni - 1)
        def _():
            dfw_ref[...] = jnp.zeros_like(dfw_ref)
            for pln in range(2):
                for k in range(3):
                    dfw_ref[pln, k:k + 1, :] = jnp.sum(acc[3 * pln + k], axis=0, keepdims=True)

    main = pl.BlockSpec((tb, tn), lambda j, i: (i, j))
    nxt = pl.BlockSpec((PACKED, tn), lambda j, i: (jnp.minimum((i + 1) * hb, last_halo), j))
    return _pallas(
        body, name=name, grid=(D_FF // tn, ni),
        in_specs=[main, nxt, main, nxt, main, nxt,
                  pl.BlockSpec((2, tb, tn), lambda j, i: (0, i, j)),
                  pl.BlockSpec((2, PACKED, tn), lambda j, i: (0, jnp.maximum(i * hb - 1, 0), j)),
                  pl.BlockSpec((None, 2, SUBLANES, tn), lambda j, i: (layer, 0, 0, j))],
        out_specs=[pl.BlockSpec((2, tb, tn), lambda j, i: (0, i, j)),
                   pl.BlockSpec((2, SUBLANES, tn), lambda j, i: (0, 0, j))],
        out_shape=[jax.ShapeDtypeStruct((2, t_len, D_FF), BF16),
                   jax.ShapeDtypeStruct((2, SUBLANES, D_FF), F32)],
        scratch_shapes=[pltpu.VMEM((2, tb + SUBLANES, tn), F32),
                        pltpu.VMEM((2, tb + SUBLANES, tn), F32),
                        pltpu.VMEM((6, SUBLANES, tn), F32)],
        args=(dact, dact, fg, fg, fu, fu, u, u, fw), sem=("parallel", "arbitrary"), comm=comm)


def _loss_head(x, g, target, name, tb=256):
    t_len, d = x.shape

    def body(x_ref, g_ref, t_ref, dx_ref, dxb_ref, dg_ref, loss_ref):
        @pl.when(pl.program_id(0) == 0)
        def _():
            dg_ref[...] = jnp.zeros_like(dg_ref)
            loss_ref[...] = jnp.zeros_like(loss_ref)

        xv = x_ref[...]
        gv = g_ref[...]
        r = lax.rsqrt(jnp.mean(xv * xv, axis=-1, keepdims=True) + EPS)
        xh = xv * r
        err = xh * gv - t_ref[...]
        loss_ref[...] += (0.5 / d) * jnp.sum(jnp.sum(err * err, axis=-1, keepdims=True), axis=0, keepdims=True)
        dy = err * (1.0 / d)
        dyg = dy * gv
        dx = r * (dyg - xh * jnp.mean(dyg * xh, axis=-1, keepdims=True))
        dx_ref[...] = dx
        dxb_ref[...] = dx.astype(BF16)
        dg_ref[0:1, :] += jnp.sum(dy * xh, axis=0, keepdims=True)

    return _pallas(
        body, name=name, grid=(t_len // tb,),
        in_specs=[pl.BlockSpec((tb, d), lambda i: (i, 0)), pl.BlockSpec((1, d), lambda i: (0, 0)),
                  pl.BlockSpec((tb, d), lambda i: (i, 0))],
        out_specs=[pl.BlockSpec((tb, d), lambda i: (i, 0)), pl.BlockSpec((tb, d), lambda i: (i, 0)),
                   pl.BlockSpec((SUBLANES, d), lambda i: (0, 0)),
                   pl.BlockSpec((SUBLANES, LANES), lambda i: (0, 0))],
        out_shape=[jax.ShapeDtypeStruct((t_len, d), F32), jax.ShapeDtypeStruct((t_len, d), BF16),
                   jax.ShapeDtypeStruct((SUBLANES, d), F32), jax.ShapeDtypeStruct((SUBLANES, LANES), F32)],
        args=(x, g, target), sem=("arbitrary",))[0]


def _adamw(w, g, m, v, name, emit_grad=False, comm=None):
    r, c = w.shape
    tr = 256 if r % 256 == 0 else r
    c1 = 1.0 / (1.0 - ADAM_B1 ** ADAM_STEP)
    c2 = 1.0 / (1.0 - ADAM_B2 ** ADAM_STEP)

    def body(w_ref, g_ref, m_ref, v_ref, d_ref, mo_ref, vo_ref, *go_ref):
        gv = g_ref[...]
        mn = ADAM_B1 * m_ref[...] + (1.0 - ADAM_B1) * gv
        vn = ADAM_B2 * v_ref[...] + (1.0 - ADAM_B2) * (gv * gv)
        d_ref[...] = -ADAM_LR * ((mn * c1) / (jnp.sqrt(vn * c2) + ADAM_EPS) + ADAM_WD * w_ref[...])
        mo_ref[...] = mn
        vo_ref[...] = vn
        if emit_grad:
            go_ref[0][...] = gv

    spec = pl.BlockSpec((tr, c), lambda i: (i, 0))
    shape = jax.ShapeDtypeStruct((r, c), F32)
    n_out = 4 if emit_grad else 3
    return _pallas(
        body, name=name, grid=(r // tr,),
        in_specs=[spec] * 4, out_specs=[spec] * n_out, out_shape=[shape] * n_out,
        args=(w, g, m, v), sem=("parallel",), comm=comm)


def _place():
    x, y, c = lax.axis_index("x"), lax.axis_index("y"), lax.axis_index("c")
    chips = [(1 - x, y), (x, 1 - y), (1 - x, 1 - y)]
    return x, y, c, chips


def _remote(src, dst, send, recv, sem, to):
    return pltpu.make_async_remote_copy(
        src_ref=src, dst_ref=dst, send_sem=send.at[sem], recv_sem=recv.at[sem], device_id=to, device_id_type=MESH)


def _gather_plan(fulls, kinds, mid_at=None, parts=None):
    parts = parts or [(0, 1)] * len(fulls)

    def region(it, f, k, cc):
        kind = kinds[it]
        p, n = parts[it][0:2]
        count = parts[it][2] if len(parts[it]) > 2 else 1
        if kind == SMALL:
            return f.at[k, pl.ds(cc * (CONV_PACK_ROWS // 2), CONV_PACK_ROWS // 2), :]
        if COL_SHARDED[kind]:
            rows, cols = f.shape[0] // (2 * n), f.shape[1] // N_CHIP
            return f.at[pl.ds((cc * n + p) * rows, count * rows), pl.ds(k * cols, cols)]
        assert n == 1
        rows = f.shape[0] // N_CHIP
        return f.at[pl.ds(k * rows + cc * (rows // 2), rows // 2), :]

    def first_hop(bufs, send, recv, it, j):
        x, y, c, chips = _place()
        reg = region(it, bufs[it], 2 * x + y, c)
        return _remote(reg, reg, send, recv, it * 6 + j, (*chips[j], c))

    def arrival(bufs, send, recv, it, j, second):
        x, y, c, chips = _place()
        px, py = chips[j]
        reg = region(it, bufs[it], 2 * px + py, 1 - c if second else c)
        to = (x, y, 1 - c) if second else (px, py, c)
        return _remote(reg, reg, send, recv, it * 6 + (3 + j if second else j), to)

    def forward(bufs, send, recv, it, j):
        x, y, c, chips = _place()
        px, py = chips[j]
        reg = region(it, bufs[it], 2 * px + py, c)
        return _remote(reg, reg, send, recv, it * 6 + 3 + j, (x, y, 1 - c))

    def start(srcs, bufs, outs, send, recv):
        for it in range(len(bufs)):
            for j in range(3):
                first_hop(bufs, send, recv, it, j).start()

    def mid(srcs, bufs, outs, send, recv):
        for it in range(len(bufs)):
            for j in range(3):
                arrival(bufs, send, recv, it, j, False).wait_recv()
                forward(bufs, send, recv, it, j).start()

    def finish(srcs, bufs, outs, send, recv):
        for it in range(len(bufs)):
            for j in range(3):
                arrival(bufs, send, recv, it, j, True).wait_recv()
        for it in range(len(bufs)):
            for j in range(3):
                first_hop(bufs, send, recv, it, j).wait_send()
                forward(bufs, send, recv, it, j).wait_send()

    return Comm(srcs=(), bufs=tuple(fulls), outs=(), n_sem=6 * len(fulls), start=start, mid=mid, finish=finish,
                mid_at=mid_at)


def _half_axis(kind):
    return 0 if kind == SMALL or COL_SHARDED[kind] else 1


def _half2(ref, kind, cc):
    if _half_axis(kind) == 0:
        return ref.at[pl.ds(cc * (ref.shape[0] // 2), ref.shape[0] // 2), :]
    return ref.at[:, pl.ds(cc * (ref.shape[1] // 2), ref.shape[1] // 2)]


def _pair_plan(grads, kinds):
    def land_shape(g, kind):
        s = list(g.shape)
        s[_half_axis(kind)] //= 2
        return jax.ShapeDtypeStruct(tuple(s), F32)

    def copy(srcs, outs, send, recv, it):
        x, y, c, _ = _place()
        return _remote(_half2(srcs[it], kinds[it], 1 - c), outs[it], send, recv, it, (x, y, 1 - c))

    def start(srcs, bufs, outs, send, recv):
        for it in range(len(srcs)):
            copy(srcs, outs, send, recv, it).start()

    def finish(srcs, bufs, outs, send, recv):
        for it in range(len(srcs)):
            copy(srcs, outs, send, recv, it).wait_send()
        for it in range(len(srcs)):
            copy(srcs, outs, send, recv, it).wait_recv()

    return Comm(srcs=tuple(grads), bufs=(), outs=tuple(land_shape(g, k) for g, k in zip(grads, kinds)),
                n_sem=len(grads), start=start, finish=finish)


def _scatter_plan(parts, slots, kinds):
    def piece(s, kind, k):
        if kind == SMALL:
            return s
        if COL_SHARDED[kind]:
            n = s.shape[1] // N_CHIP
            return s.at[:, pl.ds(k * n, n)]
        n = s.shape[0] // N_CHIP
        return s.at[pl.ds(k * n, n), :]

    def outbound(srcs, bufs, send, recv, it, j):
        x, y, c, chips = _place()
        px, py = chips[j]
        return _remote(piece(srcs[it], kinds[it], 2 * px + py), bufs[it].at[2 * x + y], send, recv, it * 3 + j,
                       (px, py, c))

    def inbound(bufs, send, recv, it, j):
        x, y, c, chips = _place()
        px, py = chips[j]
        got = bufs[it].at[2 * px + py]
        return _remote(got, got, send, recv, it * 3 + j, (px, py, c))

    def start(srcs, bufs, outs, send, recv):
        for it in range(len(srcs)):
            for j in range(3):
                outbound(srcs, bufs, send, recv, it, j).start()

    def finish(srcs, bufs, outs, send, recv):
        for it in range(len(srcs)):
            for j in range(3):
                inbound(bufs, send, recv, it, j).wait_recv()
        for it in range(len(srcs)):
            for j in range(3):
                outbound(srcs, bufs, send, recv, it, j).wait_send()

    return Comm(srcs=tuple(parts), bufs=tuple(slots), outs=(), n_sem=3 * len(parts), start=start, finish=finish)


def _share_plan(fulls, kinds, layer):
    def half(f, kind, cc):
        return _half2(f if kind == SMALL else f.at[layer], kind, cc)

    def copy(bufs, send, recv, it, cc):
        x, y, c, _ = _place()
        reg = half(bufs[it], kinds[it], c if cc == "mine" else 1 - c)
        return _remote(reg, reg, send, recv, it, (x, y, 1 - c))

    def start(srcs, bufs, outs, send, recv):
        for it in range(len(bufs)):
            copy(bufs, send, recv, it, "mine").start()

    def finish(srcs, bufs, outs, send, recv):
        for it in range(len(bufs)):
            copy(bufs, send, recv, it, "other").wait_recv()
        for it in range(len(bufs)):
            copy(bufs, send, recv, it, "mine").wait_send()

    return Comm(srcs=(), bufs=tuple(fulls), outs=(), n_sem=len(fulls), start=start, finish=finish)


def _pair_sum(g, land, idx, kind, name):
    odt = F32 if kind == SMALL else BF16
    r, cdim = land.shape

    def body(idx_ref, g_ref, l_ref, p_ref, s_ref):
        v = (g_ref[...] + l_ref[...]).astype(odt)
        p_ref[...] = v
        if kind == SMALL:
            s_ref[...] = v
        else:
            @pl.when(pl.program_id(1 if COL_SHARDED[kind] else 0) == idx_ref[1])
            def _():
                s_ref[...] = v

    if kind == SMALL:
        grid = (1,)
        g_spec = pl.BlockSpec((r, LANES), lambda i, idx_ref: (idx_ref[0], 0))
        spec = pl.BlockSpec((r, LANES), lambda i, idx_ref: (0, 0))
        s_spec = pl.BlockSpec((None, r, LANES), lambda i, idx_ref: (idx_ref[1], 0, 0))
        s_shape = (N_CHIP, r, LANES)
    elif COL_SHARDED[kind]:
        pc, tr = cdim // N_CHIP, 256
        nrb = r // tr
        grid = (nrb, N_CHIP)
        g_spec = pl.BlockSpec((tr, pc), lambda i, k, idx_ref: (idx_ref[0] * nrb + i, k))
        spec = pl.BlockSpec((tr, pc), lambda i, k, idx_ref: (i, k))
        s_spec = pl.BlockSpec((None, tr, pc), lambda i, k, idx_ref: (idx_ref[1], i, 0))
        s_shape = (N_CHIP, r, pc)
    else:
        pr = r // N_CHIP
        grid = (N_CHIP,)
        g_spec = pl.BlockSpec((pr, cdim), lambda k, idx_ref: (k, idx_ref[0]))
        spec = pl.BlockSpec((pr, cdim), lambda k, idx_ref: (k, 0))
        s_spec = pl.BlockSpec((None, pr, cdim), lambda k, idx_ref: (idx_ref[1], 0, 0))
        s_shape = (N_CHIP, pr, cdim)
    return pl.pallas_call(
        body, name=name,
        grid_spec=pltpu.PrefetchScalarGridSpec(
            num_scalar_prefetch=1, grid=grid, in_specs=[g_spec, spec], out_specs=[spec, s_spec]),
        out_shape=[jax.ShapeDtypeStruct(land.shape, odt), jax.ShapeDtypeStruct(s_shape, odt)],
        compiler_params=_cp(*(["arbitrary"] * len(grid))),
    )(idx, g, land)


def _sum_slots(slots, idx, kind, layer, prev, name):
    _, r, cdim = slots.shape

    def body(*refs):
        s_ref, o_ref = refs[1], refs[-1]
        v = s_ref[...].astype(F32)
        o_ref[...] = (v[0] + v[1]) + (v[2] + v[3])

    if kind == SMALL:
        grid = (1,)
        s_spec = pl.BlockSpec((N_CHIP, r, cdim), lambda i, idx_ref: (0, 0, 0))
        o_spec = pl.BlockSpec((r, cdim), lambda i, idx_ref: (idx_ref[0], 0))
        full = (2 * r, cdim)
    else:
        tr = 256 if r % 256 == 0 else 384
        nrb = r // tr
        grid = (nrb,)
        s_spec = pl.BlockSpec((N_CHIP, tr, cdim), lambda i, idx_ref: (0, i, 0))
        if COL_SHARDED[kind]:
            o_spec = pl.BlockSpec((None, tr, cdim), lambda i, idx_ref: (layer, idx_ref[0] * nrb + i, 0))
            full = (2, 2 * r, cdim)
        else:
            o_spec = pl.BlockSpec((None, tr, cdim), lambda i, idx_ref: (layer, i, idx_ref[0]))
            full = (2, r, 2 * cdim)
    in_specs, args, aliases = [s_spec], [idx, slots], {}
    if prev is not None:
        in_specs.append(ANY)
        args.append(prev)
        aliases = {2: 0}
    return pl.pallas_call(
        body, name=name,
        grid_spec=pltpu.PrefetchScalarGridSpec(
            num_scalar_prefetch=1, grid=grid, in_specs=in_specs, out_specs=o_spec),
        out_shape=jax.ShapeDtypeStruct(full, F32),
        input_output_aliases=aliases,
        compiler_params=_cp(*(["parallel"] * len(grid))),
    )(*args)


def _block_diag(w):
    w4 = w.reshape(2, 4, 4, 64, 64)
    eye = jnp.eye(4, dtype=w.dtype)[None, None, :, None, :, None]
    return (w4[:, :, :, :, None, :] * eye).reshape(2, 4, 256, 256)


def _block_diag_extract(d):
    d5 = d.reshape(4, 4, 64, 4, 64)
    return jnp.stack([d5[:, hh, :, hh, :] for hh in range(4)], axis=1).reshape(-1)


REP_NAMES = ("norm1_g", "lru_conv_b", "lru_ba", "lru_bx", "lru_lambda", "norm2_g", "lru_wa", "lru_wx")


def _pack_rep(norm1_g, conv_b, ba, bx, lam, norm2_g, wa, wx, final_g):
    parts = [a.reshape(-1) for a in (norm1_g, conv_b, ba, bx, lam, norm2_g, wa, wx, final_g)]
    return jnp.concatenate(parts).reshape(REP_ROWS, LANES)


def _unpack_rep(buf):
    flat = buf.reshape(-1)
    res, o = {}, 0
    for k in REP_NAMES:
        shape = (2, 16, 64, 64) if k in ("lru_wa", "lru_wx") else (2, 1024)
        n = math.prod(shape)
        res[k] = flat[o:o + n].reshape(shape)
        o += n
    res["final_g"] = flat[o:o + 1024]
    return res


def _pack_conv_shard(lru_cw, sc_cw, ffn_cw):
    return jnp.concatenate([lru_cw.reshape(16, LANES), jnp.pad(sc_cw.reshape(6, LANES), ((0, 2), (0, 0))),
                            ffn_cw.reshape(72, LANES)], axis=0)


def _unpack_conv_shard(buf):
    return (buf[0:16].reshape(2, 4, 256), buf[16:22].reshape(2, 3, 128), buf[24:96].reshape(2, 3, 1536))


def kernel(x, norm1_g, w_in, lru_conv_w, lru_conv_b, lru_wa, lru_ba, lru_wx, lru_bx, lru_lambda, sc_conv_w, w_out, norm2_g, w_up, ffn_conv_w, w_down, final_g, loss_target, m_norm1_g, m_w_in, m_lru_conv_w, m_lru_conv_b, m_lru_wa, m_lru_ba, m_lru_wx, m_lru_bx, m_lru_lambda, m_sc_conv_w, m_w_out, m_norm2_g, m_w_up, m_ffn_conv_w, m_w_down, m_final_g, v_norm1_g, v_w_in, v_lru_conv_w, v_lru_conv_b, v_lru_wa, v_lru_ba, v_lru_wx, v_lru_bx, v_lru_lambda, v_sc_conv_w, v_w_out, v_norm2_g, v_w_up, v_ffn_conv_w, v_w_down, v_final_g):
    me = 2 * lax.axis_index("x") + lax.axis_index("y")
    idx = jnp.stack([lax.axis_index("c"), me]).astype(jnp.int32)
    t_len = x.shape[1]

    s_conv = _pack_conv_shard(lru_conv_w, sc_conv_w, ffn_conv_w)
    conv_slots = lax.dynamic_update_slice(jnp.zeros((N_CHIP, CONV_PACK_ROWS, LANES), F32), s_conv[None], (me, 0, 0))
    wi = list(_cast_into_full(w_in, W_IN, idx, "cast_w_in"))
    wo = list(_cast_into_full(w_out, W_OUT, idx, "cast_w_out"))
    wu = list(_cast_into_full(w_up, W_UP, idx, "cast_w_up"))
    wd = list(_cast_into_full(w_down, W_DOWN, idx, "cast_w_down"))
    wi[0], convs = _comm_call(_gather_plan([wi[0], conv_slots], [W_IN, SMALL]), "ag_first")
    per_chip = [_unpack_conv_shard(convs[k]) for k in range(N_CHIP)]
    lru_cw = jnp.concatenate([p[0] for p in per_chip], axis=-1)
    sc_cw = jnp.concatenate([p[1] for p in per_chip], axis=-1)
    ffn_cw = jnp.concatenate([p[2] for p in per_chip], axis=-1)

    prm = jnp.concatenate(
        [lru_cw, jnp.stack([lru_conv_b, lru_ba, lru_bx, lru_lambda], axis=1),
         jnp.pad(sc_cw, ((0, 0), (0, 0), (0, D_LRU - D_SC))), jnp.zeros((2, 5, D_LRU), F32)], axis=1)
    gates = jnp.concatenate([_block_diag(lru_wa), _block_diag(lru_wx)], axis=1).astype(BF16)
    fw8 = jnp.pad(ffn_cw.reshape(2, 3, 2, D_FF).transpose(0, 2, 1, 3), ((0, 0), (0, 0), (0, 5), (0, 0)))

    xs = x[0]
    saved = []
    n512, n256 = t_len // 512, t_len // 256
    whole, lower, upper = (0, 1), (0, 2), (1, 2)
    carried_by = {
        "fwd_in_0": ([(wu, 0, W_UP, (0, 4))], (max(n512 - 3, 0),)),
        "fwd_mixer_0": ([(wu, 0, W_UP, (1, 4, 2)), (wo, 0, W_OUT, whole)], (max(n256 - 3, 0),)),
        "fwd_out_0": ([(wu, 0, W_UP, (3, 4))], (max(n512 - 2, 0),)),
        "fwd_up_0": ([(wd, 0, W_DOWN, whole)], (max(n512 - 2, 0),)),
        "fwd_act_0": ([(wi, 1, W_IN, whole), (wo, 1, W_OUT, whole)], (max(n512 - 2, 0), 0)),
        "fwd_down_0": ([(wu, 1, W_UP, (0, 4))], (max(n512 - 3, 0),)),
        "fwd_in_1": ([(wu, 1, W_UP, (1, 4))], (max(n512 - 3, 0),)),
        "fwd_mixer_1": ([(wu, 1, W_UP, (2, 4, 2))], (max(n256 - 4, 0),)),
        "fwd_act_1": ([(wd, 1, W_DOWN, whole)], (max(n512 - 3, 0), 0)),
    }

    def carried(name):
        if name not in carried_by:
            return None, lambda got: None
        items, mid_at = carried_by[name]

        def store(got):
            for (lst, i, _, _), arr in zip(items, got):
                lst[i] = arr

        return _gather_plan([lst[i] for lst, i, _, _ in items], [k for _, _, k, _ in items], mid_at=mid_at,
                            parts=[p for _, _, _, p in items]), store

    for l in range(2):
        comm, store = carried(f"fwd_in_{l}")
        (z, h1), got = _norm_mm(xs, norm1_g[l][None], wi[l], f"fwd_in_{l}", comm=comm)
        store(got)
        comm, store = carried(f"fwd_mixer_{l}")
        (ymix, hst), got = _mixer_fwd(z, prm, gates, l, f"fwd_mixer_{l}", comm=comm)
        store(got)
        comm, store = carried(f"fwd_out_{l}")
        (x2,), got = _mm_res(ymix, wo[l], xs, f"fwd_out_{l}", comm=comm)
        store(got)
        comm, store = carried(f"fwd_up_{l}")
        (u, h2), got = _norm_mm(x2, norm2_g[l][None], wu[l], f"fwd_up_{l}", planes=True, comm=comm)
        store(got)
        comm, store = carried(f"fwd_act_{l}")
        (act, fg, fu), got = _ffn_act(u, fw8, l, f"fwd_act_{l}", comm=comm)
        store(got)
        saved.append((xs, h1, z, hst, ymix, x2, h2, u, act, fg, fu))
        if l == 0:
            comm, store = carried(f"fwd_down_{l}")
            (xs,), got = _mm_res(act, wd[l], x2, f"fwd_down_{l}", comm=comm)
            store(got)
        else:
            dx, dxb, dgf, loss_blk = _mm_res_loss(act, wd[l], x2, final_g[None], loss_target[0], "fwd_down_loss")

    kinds = [W_IN, W_OUT, W_UP, W_DOWN]
    grads = [None, None]
    small = [None, None]
    reduced = [None] * 4
    summed1 = [None] * 4
    slots1 = [None] * 4

    def scatter1(ws):
        return _scatter_plan([summed1[w][0] for w in ws], [summed1[w][1] for w in ws], ws)

    for l in (1, 0):
        x_in, h1, z, hst, ymix, x2, h2, u, act, fg, fu = saved[l]
        carry = l == 0
        comm = _pair_plan([grads[1][W_IN]], [W_IN]) if carry else None
        (g_down,), got = _mm_tn(act, dxb, f"bwd_wdown_{l}", tk=1536, tn=1024, comm=comm)
        if carry:
            summed1[W_IN] = _pair_sum(grads[1][W_IN], got[0], idx, W_IN, "rs_add1_0")
        (dact,), _ = _mm_nt(dxb, wd[l], f"bwd_dact_{l}")
        comm = scatter1((W_UP, W_IN)) if carry else None
        (du, dfw), got = _ffn_bwd(dact, fg, fu, u, fw8, l, f"bwd_act_{l}", comm=comm)
        if carry:
            slots1[W_UP], slots1[W_IN] = got
            reduced = [_sum_slots(slots1[w], idx, kinds[w], 1, None, f"rs_sum1_{w}") for w in range(4)]
        comm = _share_plan(reduced, kinds, 1) if carry else None
        (g_up,), got = _mm_tn(h2, du, f"bwd_wup_{l}", tk=1024, tn=1536, planes=True, comm=comm)
        if carry:
            reduced = list(got)
        comm = _pair_plan([g_up, g_down], [W_UP, W_DOWN]) if carry else None
        (dx2, dx2b, dg2), got = _mm_nt_normbwd(du, wu[l], x2, norm2_g[l][None], dx, f"bwd_up_{l}", planes=True,
                                               comm=comm)
        if carry:
            sum_up = _pair_sum(g_up, got[0], idx, W_UP, "rs_add0_2")
            sum_down = _pair_sum(g_down, got[1], idx, W_DOWN, "rs_add0_3")
        (g_out,), _ = _mm_tn(ymix, dx2b, f"bwd_wout_{l}", tk=1536, tn=1024)
        comm = _pair_plan([g_out], [W_OUT]) if carry else None
        (dymix,), got = _mm_nt(dx2b, wo[l], f"bwd_dymix_{l}", comm=comm)
        trio = (W_OUT, W_UP, W_DOWN)
        if carry:
            sum_out = _pair_sum(g_out, got[0], idx, W_OUT, "rs_add0_1")
            comm = _scatter_plan([sum_out[0], sum_up[0], sum_down[0]], [sum_out[1], sum_up[1], sum_down[1]], trio)
        else:
            comm = _pair_plan([g_out, g_up, g_down], trio)
        (dz, dprm, dgates), got = _mixer_bwd(z, hst, dymix, prm, gates, l, f"bwd_mixer_{l}", comm=comm)
        if carry:
            for w, s in zip(trio, got):
                reduced[w] = _sum_slots(s, idx, w, 0, reduced[w], f"rs_sum0_{w}")
        else:
            for w, g, land in zip(trio, (g_out, g_up, g_down), got):
                summed1[w] = _pair_sum(g, land, idx, w, f"rs_add1_{w}")
        comm = _share_plan([reduced[w] for w in trio], trio, 0) if carry else scatter1((W_DOWN,))
        (g_in,), got = _mm_tn(h1, dz, f"bwd_win_{l}", tk=1024, tn=1792, comm=comm)
        if carry:
            for w, full in zip(trio, got):
                reduced[w] = full
        else:
            slots1[W_DOWN], = got
        if carry:
            land_in, = _comm_call(_pair_plan([g_in], [W_IN]), "rs_pair_in")
            sum_in = _pair_sum(g_in, land_in, idx, W_IN, "rs_add0_0")
            comm = _scatter_plan([sum_in[0]], [sum_in[1]], [W_IN])
        else:
            comm = scatter1((W_OUT,))
        (dx, dxb, dg1), got = _mm_nt_normbwd(dz, wi[l], x_in, norm1_g[l][None], dx2, f"bwd_in_{l}", comm=comm)
        if carry:
            slot_in, = got
        else:
            slots1[W_OUT], = got
        grads[l] = [g_in, g_out, g_up, g_down]
        rep = dict(zip(REP_NAMES, [dg1[0], dprm[4], dprm[5], dprm[6], dprm[7], dg2[0],
                                   _block_diag_extract(dgates[0:4]), _block_diag_extract(dgates[4:8])]))
        conv = [dprm[0:4].reshape(-1), jnp.pad(dprm[8:11, 0:D_SC].reshape(-1), (0, 512)),
                dfw[:, 0:3, :].transpose(1, 0, 2).reshape(-1)]
        small[l] = (rep, conv)
    grad_x = dx[None]
    g_small = jnp.concatenate(
        [small[l][0][k] for k in REP_NAMES for l in range(2)] + [dgf[0]] + small[0][1] + small[1][1]
        + [loss_blk.reshape(-1)]).reshape(SMALL_ROWS, LANES)

    def big(w, g, m, v, name):
        shape = w.shape
        two_d = lambda a: a.reshape(-1, shape[-1])
        outs, _ = _adamw(two_d(w), two_d(g), two_d(m), two_d(v), name, emit_grad=True)
        return [o.reshape(shape) for o in outs]

    land_small, = _comm_call(_pair_plan([g_small], [SMALL]), "rs_pair_small")
    sum_small = _pair_sum(g_small, land_small, idx, SMALL, "rs_add0_4")
    slot_small, = _comm_call(_scatter_plan([sum_small[0]], [sum_small[1]], [SMALL]), "rs_scatter_small")
    gw_in, gs = _comm_call(
        _share_plan([_sum_slots(slot_in, idx, W_IN, 0, reduced[W_IN], "rs_sum0_0"),
                     _sum_slots(slot_small, idx, SMALL, 0, None, "rs_sum0_4")], [W_IN, SMALL], 0), "rs_share0")
    upd = {"w_up": big(w_up, reduced[W_UP], m_w_up, v_w_up, "adamw_w_up"),
           "w_down": big(w_down, reduced[W_DOWN], m_w_down, v_w_down, "adamw_w_down"),
           "w_out": big(w_out, reduced[W_OUT], m_w_out, v_w_out, "adamw_w_out"),
           "w_in": big(w_in, gw_in, m_w_in, v_w_in, "adamw_w_in")}

    loss = gs[REP_ROWS + CONV_ROWS, 0]
    g_rep = gs[0:REP_ROWS]
    g_conv = gs[REP_ROWS:REP_ROWS + CONV_ROWS].reshape(2, CONV_LAYER)
    g_lru_cw = lax.dynamic_slice_in_dim(g_conv[:, 0:4096].reshape(2, 4, 1024), me * 256, 256, axis=2)
    g_sc_cw = lax.dynamic_slice_in_dim(g_conv[:, 4096:4096 + 1536].reshape(2, 3, 512), me * 128, 128, axis=2)
    g_ffn_cw = lax.dynamic_slice_in_dim(g_conv[:, 6144:].reshape(2, 3, 6144), me * 1536, 1536, axis=2)

    rep_out, _ = _adamw(
        _pack_rep(norm1_g, lru_conv_b, lru_ba, lru_bx, lru_lambda, norm2_g, lru_wa, lru_wx, final_g), g_rep,
        _pack_rep(m_norm1_g, m_lru_conv_b, m_lru_ba, m_lru_bx, m_lru_lambda, m_norm2_g, m_lru_wa, m_lru_wx, m_final_g),
        _pack_rep(v_norm1_g, v_lru_conv_b, v_lru_ba, v_lru_bx, v_lru_lambda, v_norm2_g, v_lru_wa, v_lru_wx, v_final_g),
        "adamw_rep")
    conv_out, _ = _adamw(s_conv, _pack_conv_shard(g_lru_cw, g_sc_cw, g_ffn_cw),
                         _pack_conv_shard(m_lru_conv_w, m_sc_conv_w, m_ffn_conv_w),
                         _pack_conv_shard(v_lru_conv_w, v_sc_conv_w, v_ffn_conv_w), "adamw_conv")

    names = ["norm1_g", "w_in", "lru_conv_w", "lru_conv_b", "lru_wa", "lru_ba", "lru_wx", "lru_bx", "lru_lambda",
             "sc_conv_w", "w_out", "norm2_g", "w_up", "ffn_conv_w", "w_down", "final_g"]
    groups = []
    g_all = dict(_unpack_rep(g_rep))
    g_all.update({k: v[3] for k, v in upd.items()})
    g_all.update(lru_conv_w=g_lru_cw, sc_conv_w=g_sc_cw, ffn_conv_w=g_ffn_cw)
    groups.append(g_all)
    for i in range(3):
        d = dict(_unpack_rep(rep_out[i]))
        cl, cs, cf = _unpack_conv_shard(conv_out[i])
        d.update(lru_conv_w=cl, sc_conv_w=cs, ffn_conv_w=cf)
        d.update({k: v[i] for k, v in upd.items()})
        groups.append(d)
    return (loss, grad_x, *[grp[n] for grp in groups for n in names])
```
